```python
import jax, jax.numpy as jnp
from jax import lax
import numpy as np

D_MODEL = 1024
BATCH = 8
SEQ = 4096
DEPTH = 4

CHUNK = 64
Q_BLOCK = 128

GDN_HEADS = 4
GDN_HEAD_DIM = 128
GDN_WIDTH = GDN_HEADS * GDN_HEAD_DIM
GDN_CONV = 4
SB_HEADS = 8
SB_HEAD_DIM = 64
SB_WIDTH = SB_HEADS * SB_HEAD_DIM
SC_GROUPS = 8
SC_WIDTH = 512
SC_CONV = 3
N_BRANCH = 3
BRANCH_WIDTH = 512
D_FF = 4 * D_MODEL
EPS = 1e-6

_SIZES = (3 * GDN_WIDTH,
          GDN_WIDTH,
          GDN_HEADS,
          GDN_HEADS,
          3 * SB_WIDTH,
          SC_WIDTH,
          SC_WIDTH,
          SC_WIDTH,
          N_BRANCH * D_MODEL)
IN_PROJ_WIDTH = sum(_SIZES)
SPLIT_POINTS = tuple(int(s) for s in np.cumsum(_SIZES)[:-1])

kernel_name = "hybrid_gdn_stickbreak_shortconv_block"


def rms_norm(x, w):
    x32 = x.astype(jnp.float32)
    y = x32 * lax.rsqrt(jnp.mean(x32 * x32, axis=-1, keepdims=True) + EPS)
    return (y * w.astype(jnp.float32)).astype(x.dtype)


def l2_normalize(x):
    return x * lax.rsqrt(jnp.sum(x * x, axis=-1, keepdims=True) + EPS)


def causal_depthwise_conv(x, w):
    k_len, ch = w.shape
    return lax.conv_general_dilated(
        x, w.astype(x.dtype)[:, None, :], window_strides=(1,), padding=[(k_len - 1, 0)],
        dimension_numbers=("NWC", "WIO", "NWC"), feature_group_count=ch)


def gated_delta_rule(q, k, v, g, beta):
    bsz, seq, heads, dk = q.shape
    dv = v.shape[-1]
    n_chunks = seq // CHUNK

    def chunkify(t):
        return jnp.moveaxis(t.reshape((bsz, n_chunks, CHUNK) + t.shape[2:]), 3, 2)

    q, k, v, g, beta = (chunkify(t) for t in (q, k, v, g, beta))
    g = jnp.cumsum(g, axis=-1)
    tri = jnp.tril(jnp.ones((CHUNK, CHUNK), dtype=bool))
    strict = jnp.tril(jnp.ones((CHUNK, CHUNK), dtype=bool), k=-1)
    decay = jnp.exp(jnp.where(tri, g[..., :, None] - g[..., None, :], -jnp.inf))

    k_beta = k * beta[..., None]
    v_beta = v * beta[..., None]
    a_kk = jnp.where(strict, jnp.einsum("bnhcd,bnhed->bnhce", k_beta, k) * decay, 0.0)
    lhs = a_kk + jnp.eye(CHUNK, dtype=a_kk.dtype)
    rhs = jnp.concatenate([v_beta, k_beta * jnp.exp(g)[..., None]], axis=-1)
    sol = lax.linalg.triangular_solve(lhs, rhs, left_side=True, lower=True, unit_diagonal=True)
    u, w = sol[..., :dv], sol[..., dv:]

    a_qk = jnp.where(tri, jnp.einsum("bnhcd,bnhed->bnhce", q, k) * decay, 0.0)
    q_dec = q * jnp.exp(g)[..., None]
    k_dec = k * jnp.exp(g[..., -1:] - g)[..., None]
    g_last = jnp.exp(g[..., -1])

    def step(state, inp):
        q_c, k_c, u_c, w_c, a_c, gl_c = inp
        v_new = u_c - jnp.einsum("bhcd,bhde->bhce", w_c, state)
        o_c = (jnp.einsum("bhcd,bhde->bhce", q_c, state)
               + jnp.einsum("bhcs,bhse->bhce", a_c, v_new))
        state = state * gl_c[..., None, None] + jnp.einsum("bhcd,bhce->bhde", k_c, v_new)
        return state, o_c

    xs = tuple(jnp.moveaxis(t, 1, 0) for t in (q_dec, k_dec, u, w, a_qk, g_last))
    state0 = jnp.zeros((bsz, heads, dk, dv), jnp.float32)
    _, o = lax.scan(step, state0, xs)
    return o.transpose(1, 0, 3, 2, 4).reshape(bsz, seq, heads, dv)


def gdn_branch(qkv, gate, a, b, conv_w, a_log, dt_bias, norm_w):
    bsz, seq, _ = qkv.shape
    out_dtype = qkv.dtype
    qkv = jax.nn.silu(causal_depthwise_conv(qkv, conv_w)).astype(jnp.float32)
    q, k, v = jnp.split(qkv, 3, axis=-1)
    shp = (bsz, seq, GDN_HEADS, GDN_HEAD_DIM)
    q = l2_normalize(q.reshape(shp)) * (GDN_HEAD_DIM ** -0.5)
    k = l2_normalize(k.reshape(shp))
    v = v.reshape(shp)
    beta = jax.nn.sigmoid(b.astype(jnp.float32))
    g = -jnp.exp(a_log.astype(jnp.float32)) * jax.nn.softplus(a.astype(jnp.float32) + dt_bias.astype(jnp.float32))
    o = gated_delta_rule(q, k, v, g, beta)
    o = o * lax.rsqrt(jnp.mean(o * o, axis=-1, keepdims=True) + EPS) * norm_w.astype(jnp.float32)
    o = o * jax.nn.silu(gate.astype(jnp.float32).reshape(shp))
    return o.reshape(bsz, seq, GDN_WIDTH).astype(out_dtype)


def stick_breaking_branch(qkv):
    bsz, seq, _ = qkv.shape
    q, k, v = jnp.split(qkv, 3, axis=-1)
    shp = (bsz, seq, SB_HEADS, SB_HEAD_DIM)
    q, k, v = q.reshape(shp), k.reshape(shp), v.reshape(shp)
    scale = SB_HEAD_DIM ** -0.5
    outs = []
    for blk in range(seq // Q_BLOCK):
        q0 = blk * Q_BLOCK
        k_len = q0 + Q_BLOCK
        z = jnp.einsum("bqhd,bkhd->bhqk", q[:, q0:k_len], k[:, :k_len]).astype(jnp.float32) * scale
        t_idx = q0 + jnp.arange(Q_BLOCK)
        s_idx = jnp.arange(k_len)
        mask = s_idx[None, :] < t_idx[:, None]
        log_1m = jnp.where(mask, jax.nn.log_sigmoid(-z), 0.0)
        after = lax.cumsum(log_1m, axis=3, reverse=True) - log_1m
        att = jnp.where(mask, jnp.exp(jax.nn.log_sigmoid(z) + after), 0.0)
        outs.append(jnp.einsum("bhqk,bkhd->bqhd", att.astype(v.dtype), v[:, :k_len]))
    return jnp.concatenate(outs, axis=1).reshape(bsz, seq, SB_WIDTH)


def short_conv_branch(xin, gate_b, gate_c, conv_w):
    return gate_b * causal_depthwise_conv(gate_c * xin, conv_w)


def _fwd_setup_inputs(seed: int = 0) -> dict:
    key = jax.random.key(seed)
    ks = jax.random.split(key, 16)
    f32 = jnp.float32

    def normal(k, shape, scale):
        return jax.random.normal(k, shape, f32) * scale

    def gain(k):
        return 1.0 + 0.02 * jax.random.normal(k, (DEPTH, D_MODEL), f32)

    dt = jnp.exp(jax.random.uniform(ks[5], (DEPTH, GDN_HEADS), f32, np.log(1e-3), np.log(1e-1)))
    return {
        "x": jax.random.normal(ks[0], (BATCH, SEQ, D_MODEL), f32),
        "norm_mix_pre": gain(ks[1]),
        "w_in": normal(ks[2], (DEPTH, D_MODEL, IN_PROJ_WIDTH), D_MODEL ** -0.5),
        "conv_qkv_w": normal(ks[3], (DEPTH, GDN_CONV, 3 * GDN_WIDTH), GDN_CONV ** -0.5),
        "gdn_a_log": jnp.log(jax.random.uniform(ks[4], (DEPTH, GDN_HEADS), f32, 1.0, 16.0)),
        "gdn_dt_bias": dt + jnp.log(-jnp.expm1(-dt)),
        "gdn_norm_w": 1.0 + 0.02 * jax.random.normal(ks[6], (DEPTH, GDN_HEAD_DIM), f32),
        "conv_sc_w": normal(ks[7], (DEPTH, SC_CONV, SC_WIDTH), SC_CONV ** -0.5),
        "w_branch": normal(ks[8], (DEPTH, N_BRANCH, BRANCH_WIDTH, D_MODEL), BRANCH_WIDTH ** -0.5),
        "w_out": normal(ks[9], (DEPTH, D_MODEL, D_MODEL), D_MODEL ** -0.5),
        "norm_mix_post": gain(ks[10]),
        "norm_ffn_pre": gain(ks[11]),
        "w_ff1": normal(ks[12], (DEPTH, D_MODEL, D_FF), D_MODEL ** -0.5),
        "w_ff2": normal(ks[13], (DEPTH, D_FF, D_MODEL), D_FF ** -0.5),
        "norm_ffn_post": gain(ks[14]),
    }


def _fwd_reference(x, norm_mix_pre, w_in, conv_qkv_w, gdn_a_log, gdn_dt_bias, gdn_norm_w, conv_sc_w,
              w_branch, w_out, norm_mix_post, norm_ffn_pre, w_ff1, w_ff2, norm_ffn_post):
    bsz, seq, _ = x.shape
    for l in range(DEPTH):
        h = rms_norm(x, norm_mix_pre[l])
        proj = h @ w_in[l]
        (gdn_qkv, gdn_gate, gdn_a, gdn_b, sb_qkv, sc_x, sc_b, sc_c, gates) = jnp.split(
            proj, SPLIT_POINTS, axis=-1)
        y_a = gdn_branch(gdn_qkv, gdn_gate, gdn_a, gdn_b, conv_qkv_w[l], gdn_a_log[l],
                         gdn_dt_bias[l], gdn_norm_w[l])
        y_b = stick_breaking_branch(sb_qkv)
        y_c = short_conv_branch(sc_x, sc_b, sc_c, conv_sc_w[l])
        gates = jax.nn.sigmoid(gates.reshape(bsz, seq, N_BRANCH, D_MODEL))
        merged = (gates[:, :, 0] * (y_a @ w_branch[l, 0])
                  + gates[:, :, 1] * (y_b @ w_branch[l, 1])
                  + gates[:, :, 2] * (y_c @ w_branch[l, 2]))
        x = x + rms_norm(merged @ w_out[l], norm_mix_post[l])
        h = rms_norm(x, norm_ffn_pre[l])
        f = jnp.square(jax.nn.relu(h @ w_ff1[l])) @ w_ff2[l]
        x = x + rms_norm(f, norm_ffn_post[l])
    return x


import jax as _jax
import jax.numpy as _jnp

TWIN_FORMAT = 'train_step'
FWD_PARAMS = ['x', 'norm_mix_pre', 'w_in', 'conv_qkv_w', 'gdn_a_log', 'gdn_dt_bias', 'gdn_norm_w', 'conv_sc_w', 'w_branch', 'w_out', 'norm_mix_post', 'norm_ffn_pre', 'w_ff1', 'w_ff2', 'norm_ffn_post']
TWIN_WEIGHTS = ['norm_mix_pre', 'w_in', 'conv_qkv_w', 'gdn_a_log', 'gdn_dt_bias', 'gdn_norm_w', 'conv_sc_w', 'w_branch', 'w_out', 'norm_mix_post', 'norm_ffn_pre', 'w_ff1', 'w_ff2', 'norm_ffn_post']
TWIN_DIFF_INPUT = 'x'
TWIN_INPUTS = ['x', 'norm_mix_pre', 'w_in', 'conv_qkv_w', 'gdn_a_log', 'gdn_dt_bias', 'gdn_norm_w', 'conv_sc_w', 'w_branch', 'w_out', 'norm_mix_post', 'norm_ffn_pre', 'w_ff1', 'w_ff2', 'norm_ffn_post', 'loss_target', 'm_norm_mix_pre', 'm_w_in', 'm_conv_qkv_w', 'm_gdn_a_log', 'm_gdn_dt_bias', 'm_gdn_norm_w', 'm_conv_sc_w', 'm_w_branch', 'm_w_out', 'm_norm_mix_post', 'm_norm_ffn_pre', 'm_w_ff1', 'm_w_ff2', 'm_norm_ffn_post', 'v_norm_mix_pre', 'v_w_in', 'v_conv_qkv_w', 'v_gdn_a_log', 'v_gdn_dt_bias', 'v_gdn_norm_w', 'v_conv_sc_w', 'v_w_branch', 'v_w_out', 'v_norm_mix_post', 'v_norm_ffn_pre', 'v_w_ff1', 'v_w_ff2', 'v_norm_ffn_post']
TWIN_OUTPUTS = ['loss', 'grad_x', 'grad_norm_mix_pre', 'grad_w_in', 'grad_conv_qkv_w', 'grad_gdn_a_log', 'grad_gdn_dt_bias', 'grad_gdn_norm_w', 'grad_conv_sc_w', 'grad_w_branch', 'grad_w_out', 'grad_norm_mix_post', 'grad_norm_ffn_pre', 'grad_w_ff1', 'grad_w_ff2', 'grad_norm_ffn_post', 'delta_norm_mix_pre', 'delta_w_in', 'delta_conv_qkv_w', 'delta_gdn_a_log', 'delta_gdn_dt_bias', 'delta_gdn_norm_w', 'delta_conv_sc_w', 'delta_w_branch', 'delta_w_out', 'delta_norm_mix_post', 'delta_norm_ffn_pre', 'delta_w_ff1', 'delta_w_ff2', 'delta_norm_ffn_post', 'new_m_norm_mix_pre', 'new_m_w_in', 'new_m_conv_qkv_w', 'new_m_gdn_a_log', 'new_m_gdn_dt_bias', 'new_m_gdn_norm_w', 'new_m_conv_sc_w', 'new_m_w_branch', 'new_m_w_out', 'new_m_norm_mix_post', 'new_m_norm_ffn_pre', 'new_m_w_ff1', 'new_m_w_ff2', 'new_m_norm_ffn_post', 'new_v_norm_mix_pre', 'new_v_w_in', 'new_v_conv_qkv_w', 'new_v_gdn_a_log', 'new_v_gdn_dt_bias', 'new_v_gdn_norm_w', 'new_v_conv_sc_w', 'new_v_w_branch', 'new_v_w_out', 'new_v_norm_mix_post', 'new_v_norm_ffn_pre', 'new_v_w_ff1', 'new_v_w_ff2', 'new_v_norm_ffn_post']
TWIN_LEAF_KINDS = {'loss': 'loss', 'grad_x': 'grad_x', 'grad_norm_mix_pre': 'grad_w', 'grad_w_in': 'grad_w', 'grad_conv_qkv_w': 'grad_w', 'grad_gdn_a_log': 'grad_w', 'grad_gdn_dt_bias': 'grad_w', 'grad_gdn_norm_w': 'grad_w', 'grad_conv_sc_w': 'grad_w', 'grad_w_branch': 'grad_w', 'grad_w_out': 'grad_w', 'grad_norm_mix_post': 'grad_w', 'grad_norm_ffn_pre': 'grad_w', 'grad_w_ff1': 'grad_w', 'grad_w_ff2': 'grad_w', 'grad_norm_ffn_post': 'grad_w', 'delta_norm_mix_pre': 'delta_w', 'delta_w_in': 'delta_w', 'delta_conv_qkv_w': 'delta_w', 'delta_gdn_a_log': 'delta_w', 'delta_gdn_dt_bias': 'delta_w', 'delta_gdn_norm_w': 'delta_w', 'delta_conv_sc_w': 'delta_w', 'delta_w_branch': 'delta_w', 'delta_w_out': 'delta_w', 'delta_norm_mix_post': 'delta_w', 'delta_norm_ffn_pre': 'delta_w', 'delta_w_ff1': 'delta_w', 'delta_w_ff2': 'delta_w', 'delta_norm_ffn_post': 'delta_w', 'new_m_norm_mix_pre': 'new_m', 'new_m_w_in': 'new_m', 'new_m_conv_qkv_w': 'new_m', 'new_m_gdn_a_log': 'new_m', 'new_m_gdn_dt_bias': 'new_m', 'new_m_gdn_norm_w': 'new_m', 'new_m_conv_sc_w': 'new_m', 'new_m_w_branch': 'new_m', 'new_m_w_out': 'new_m', 'new_m_norm_mix_post': 'new_m', 'new_m_norm_ffn_pre': 'new_m', 'new_m_w_ff1': 'new_m', 'new_m_w_ff2': 'new_m', 'new_m_norm_ffn_post': 'new_m', 'new_v_norm_mix_pre': 'new_v', 'new_v_w_in': 'new_v', 'new_v_conv_qkv_w': 'new_v', 'new_v_gdn_a_log': 'new_v', 'new_v_gdn_dt_bias': 'new_v', 'new_v_gdn_norm_w': 'new_v', 'new_v_conv_sc_w': 'new_v', 'new_v_w_branch': 'new_v', 'new_v_w_out': 'new_v', 'new_v_norm_mix_post': 'new_v', 'new_v_norm_ffn_pre': 'new_v', 'new_v_w_ff1': 'new_v', 'new_v_w_ff2': 'new_v', 'new_v_norm_ffn_post': 'new_v'}


def _forward(args):
    return _fwd_reference(*[args[k] for k in FWD_PARAMS])


def _output_shape():
    out = _jax.eval_shape(lambda: _forward(_fwd_setup_inputs(0)))
    return out.shape, out.dtype

N_MICROBATCH = 1
ADAM_LR = 0.001
ADAM_B1 = 0.9
ADAM_B2 = 0.999
ADAM_EPS = 1e-08
ADAM_WD = 0.01
ADAM_STEP = 10
PER_EXAMPLE_BATCH_AXIS = {'x': 0, 'loss_target': 0}
SHARED_INPUTS = []
_WEIGHT_DTYPES = {'norm_mix_pre': _jnp.float32, 'w_in': _jnp.float32, 'conv_qkv_w': _jnp.float32, 'gdn_a_log': _jnp.float32, 'gdn_dt_bias': _jnp.float32, 'gdn_norm_w': _jnp.float32, 'conv_sc_w': _jnp.float32, 'w_branch': _jnp.float32, 'w_out': _jnp.float32, 'norm_mix_post': _jnp.float32, 'norm_ffn_pre': _jnp.float32, 'w_ff1': _jnp.float32, 'w_ff2': _jnp.float32, 'norm_ffn_post': _jnp.float32}
MOMENT_SCALE = {'norm_mix_pre': 4.283479e+00, 'w_in': 1.565017e+00, 'conv_qkv_w': 1.624641e+00, 'gdn_a_log': 4.582258e+00, 'gdn_dt_bias': 4.356744e+00, 'gdn_norm_w': 8.350572e+00, 'conv_sc_w': 1.322629e+00, 'w_branch': 3.073861e+00, 'w_out': 5.368278e+00, 'norm_mix_post': 3.223987e+01, 'norm_ffn_pre': 4.026880e+00, 'w_ff1': 1.993659e+00, 'w_ff2': 1.342916e+01, 'norm_ffn_post': 3.600949e+01}


def _to_microbatches(a, axis):
    t = _jnp.moveaxis(a, axis, 0)
    t = t.reshape((N_MICROBATCH, t.shape[0] // N_MICROBATCH) + t.shape[1:])
    return _jnp.moveaxis(t, 1, axis + 1)


def setup_inputs(seed: int = 0) -> dict:
    inp = _fwd_setup_inputs(seed)
    key = _jax.random.fold_in(_jax.random.key(seed), 7919)
    shape, _ = _output_shape()
    out = dict(inp)
    out["loss_target"] = _jax.random.normal(_jax.random.fold_in(key, 0), shape, _jnp.float32)
    for i, name in enumerate(TWIN_WEIGHTS):
        w = inp[name].astype(_jnp.float32)
        if MOMENT_SCALE is None:
            s = _jnp.sqrt(_jnp.mean(_jnp.square(w)) + 1e-30)
        else:
            s = MOMENT_SCALE[name]
        km, kv = _jax.random.split(_jax.random.fold_in(key, i + 1))
        out[name] = w
        out["m_" + name] = s * _jax.random.normal(km, w.shape, _jnp.float32)
        out["v_" + name] = (s * s) * _jax.random.uniform(kv, w.shape, _jnp.float32, 0.5, 1.5)
    if N_MICROBATCH > 1:
        for name, axis in PER_EXAMPLE_BATCH_AXIS.items():
            out[name] = _to_microbatches(out[name], axis)
    return {'x': out['x'], 'norm_mix_pre': out['norm_mix_pre'], 'w_in': out['w_in'], 'conv_qkv_w': out['conv_qkv_w'], 'gdn_a_log': out['gdn_a_log'], 'gdn_dt_bias': out['gdn_dt_bias'], 'gdn_norm_w': out['gdn_norm_w'], 'conv_sc_w': out['conv_sc_w'], 'w_branch': out['w_branch'], 'w_out': out['w_out'], 'norm_mix_post': out['norm_mix_post'], 'norm_ffn_pre': out['norm_ffn_pre'], 'w_ff1': out['w_ff1'], 'w_ff2': out['w_ff2'], 'norm_ffn_post': out['norm_ffn_post'], 'loss_target': out['loss_target'], 'm_norm_mix_pre': out['m_norm_mix_pre'], 'm_w_in': out['m_w_in'], 'm_conv_qkv_w': out['m_conv_qkv_w'], 'm_gdn_a_log': out['m_gdn_a_log'], 'm_gdn_dt_bias': out['m_gdn_dt_bias'], 'm_gdn_norm_w': out['m_gdn_norm_w'], 'm_conv_sc_w': out['m_conv_sc_w'], 'm_w_branch': out['m_w_branch'], 'm_w_out': out['m_w_out'], 'm_norm_mix_post': out['m_norm_mix_post'], 'm_norm_ffn_pre': out['m_norm_ffn_pre'], 'm_w_ff1': out['m_w_ff1'], 'm_w_ff2': out['m_w_ff2'], 'm_norm_ffn_post': out['m_norm_ffn_post'], 'v_norm_mix_pre': out['v_norm_mix_pre'], 'v_w_in': out['v_w_in'], 'v_conv_qkv_w': out['v_conv_qkv_w'], 'v_gdn_a_log': out['v_gdn_a_log'], 'v_gdn_dt_bias': out['v_gdn_dt_bias'], 'v_gdn_norm_w': out['v_gdn_norm_w'], 'v_conv_sc_w': out['v_conv_sc_w'], 'v_w_branch': out['v_w_branch'], 'v_w_out': out['v_w_out'], 'v_norm_mix_post': out['v_norm_mix_post'], 'v_norm_ffn_pre': out['v_norm_ffn_pre'], 'v_w_ff1': out['v_w_ff1'], 'v_w_ff2': out['v_w_ff2'], 'v_norm_ffn_post': out['v_norm_ffn_post']}


def _loss(weights, diff, rest, loss_target):
    with _jax.named_scope("forward"):
        args = {**rest, TWIN_DIFF_INPUT: diff, **{k: w.astype(_WEIGHT_DTYPES[k]) for k, w in weights.items()}}
        y = _forward(args)
    with _jax.named_scope("loss_head"):
        err = _jnp.square(y.astype(_jnp.float32) - loss_target)
        return 0.5 * _jnp.sum(_jnp.mean(err, axis=-1)) if err.ndim else 0.5 * err


def _adamw(w, g, m, v):
    m = ADAM_B1 * m + (1.0 - ADAM_B1) * g
    v = ADAM_B2 * v + (1.0 - ADAM_B2) * _jnp.square(g)
    m_hat = m / (1.0 - ADAM_B1 ** ADAM_STEP)
    v_hat = v / (1.0 - ADAM_B2 ** ADAM_STEP)
    delta = -ADAM_LR * (m_hat / (_jnp.sqrt(v_hat) + ADAM_EPS) + ADAM_WD * w)
    return delta, m, v


def reference(x, norm_mix_pre, w_in, conv_qkv_w, gdn_a_log, gdn_dt_bias, gdn_norm_w, conv_sc_w, w_branch, w_out, norm_mix_post, norm_ffn_pre, w_ff1, w_ff2, norm_ffn_post, loss_target, m_norm_mix_pre, m_w_in, m_conv_qkv_w, m_gdn_a_log, m_gdn_dt_bias, m_gdn_norm_w, m_conv_sc_w, m_w_branch, m_w_out, m_norm_mix_post, m_norm_ffn_pre, m_w_ff1, m_w_ff2, m_norm_ffn_post, v_norm_mix_pre, v_w_in, v_conv_qkv_w, v_gdn_a_log, v_gdn_dt_bias, v_gdn_norm_w, v_conv_sc_w, v_w_branch, v_w_out, v_norm_mix_post, v_norm_ffn_pre, v_w_ff1, v_w_ff2, v_norm_ffn_post):
    given = dict(x=x, norm_mix_pre=norm_mix_pre, w_in=w_in, conv_qkv_w=conv_qkv_w, gdn_a_log=gdn_a_log, gdn_dt_bias=gdn_dt_bias, gdn_norm_w=gdn_norm_w, conv_sc_w=conv_sc_w, w_branch=w_branch, w_out=w_out, norm_mix_post=norm_mix_post, norm_ffn_pre=norm_ffn_pre, w_ff1=w_ff1, w_ff2=w_ff2, norm_ffn_post=norm_ffn_post, loss_target=loss_target, m_norm_mix_pre=m_norm_mix_pre, m_w_in=m_w_in, m_conv_qkv_w=m_conv_qkv_w, m_gdn_a_log=m_gdn_a_log, m_gdn_dt_bias=m_gdn_dt_bias, m_gdn_norm_w=m_gdn_norm_w, m_conv_sc_w=m_conv_sc_w, m_w_branch=m_w_branch, m_w_out=m_w_out, m_norm_mix_post=m_norm_mix_post, m_norm_ffn_pre=m_norm_ffn_pre, m_w_ff1=m_w_ff1, m_w_ff2=m_w_ff2, m_norm_ffn_post=m_norm_ffn_post, v_norm_mix_pre=v_norm_mix_pre, v_w_in=v_w_in, v_conv_qkv_w=v_conv_qkv_w, v_gdn_a_log=v_gdn_a_log, v_gdn_dt_bias=v_gdn_dt_bias, v_gdn_norm_w=v_gdn_norm_w, v_conv_sc_w=v_conv_sc_w, v_w_branch=v_w_branch, v_w_out=v_w_out, v_norm_mix_post=v_norm_mix_post, v_norm_ffn_pre=v_norm_ffn_pre, v_w_ff1=v_w_ff1, v_w_ff2=v_w_ff2, v_norm_ffn_post=v_norm_ffn_post)
    weights = {n: given[n] for n in TWIN_WEIGHTS}
    shared = {n: given[n] for n in SHARED_INPUTS}
    per_example = {n: given[n] for n in ['x']}
    grad_fn = _jax.value_and_grad(_loss, argnums=(0, 1))

    def one_microbatch(ex, loss_target):
        ex = dict(ex)
        diff = ex.pop(TWIN_DIFF_INPUT)
        return grad_fn(weights, diff, {**shared, **ex}, loss_target)

    if N_MICROBATCH == 1:
        loss, (grad_w, grad_x) = one_microbatch(per_example, given["loss_target"])
    else:
        def body(carry, xs):
            loss_sum, grad_sum = carry
            l_k, (gw_k, gx_k) = one_microbatch(xs[0], xs[1])
            with _jax.named_scope("update"):
                return (loss_sum + l_k, _jax.tree.map(_jnp.add, grad_sum, gw_k)), gx_k

        init = (_jnp.zeros((), _jnp.float32), _jax.tree.map(_jnp.zeros_like, weights))
        (loss, grad_w), grad_x = _jax.lax.scan(body, init, (per_example, given["loss_target"]))
    with _jax.named_scope("update"):
        delta_w, new_m, new_v = {}, {}, {}
        for n in TWIN_WEIGHTS:
            delta_w[n], new_m[n], new_v[n] = _adamw(weights[n], grad_w[n], given["m_" + n], given["v_" + n])
    return (loss, grad_x, *[grad_w[n] for n in TWIN_WEIGHTS], *[delta_w[n] for n in TWIN_WEIGHTS],
            *[new_m[n] for n in TWIN_WEIGHTS], *[new_v[n] for n in TWIN_WEIGHTS])
```

```python
import functools

import jax
import jax.numpy as jnp
from jax import lax
from jax.experimental import pallas as pl
from jax.experimental.pallas import tpu as pltpu

F32, BF16 = jnp.float32, jnp.bfloat16
HIGHEST = lax.Precision.HIGHEST
MESH_ID = pl.DeviceIdType.MESH

N_DEV = 8
DEPTH = 4
D_MODEL = 1024
D_FF = 4096
EPS = 1e-6
GDN_HEADS, GDN_DIM, GDN_CONV = 4, 128, 4
GDN_W = GDN_HEADS * GDN_DIM
CHUNK = 64
SB_HEADS, SB_DIM = 8, 64
SB_W = SB_HEADS * SB_DIM
SB_BLOCK = 128
SC_W, SC_CONV = 512, 3
IN_W = 8200
C_GQKV, C_GGATE, C_SBQKV, C_SCX, C_SCB, C_SCC, C_GATES, MAIN_W = 0, 1536, 2048, 3584, 4096, 4608, 5120, 8192
AB_W = 128

ADAM_LR, ADAM_B1, ADAM_B2, ADAM_EPS, ADAM_WD, ADAM_STEP = 0.001, 0.9, 0.999, 1e-08, 0.01, 10

VMEM_LIMIT = 48 * 2 ** 20
FLAT_COLS = 512


def _cparams(*sem):
    return pltpu.CompilerParams(dimension_semantics=sem or None, vmem_limit_bytes=VMEM_LIMIT)


def _tile(n, pref):
    if n <= pref:
        return n
    t = pref
    while n % t:
        t -= 128
    assert t > 0
    return t


def _dot(a, b, ca=1, cb=0, hi=False):
    dims = (((ca,), (cb,)), ((), ()))
    if hi:
        return lax.dot_general(a.astype(F32), b.astype(F32), dims, precision=HIGHEST, preferred_element_type=F32)
    return lax.dot_general(a.astype(BF16), b.astype(BF16), dims, preferred_element_type=F32)


def _sigmoid(z):
    e = jnp.exp(-jnp.abs(z))
    return jnp.where(z >= 0, 1.0, e) / (1.0 + e)


def _softplus(z):
    return jnp.maximum(z, 0.0) + jnp.log(1.0 + jnp.exp(-jnp.abs(z)))


def _mm(a, b, *, name, ta=False, tb=False, out_dtypes=(BF16,), epi=None, extras=()):
    assert a.dtype == BF16 and b.dtype == BF16
    m, k = (a.shape[1], a.shape[0]) if ta else a.shape
    n = b.shape[0] if tb else b.shape[1]
    assert (b.shape[1] if tb else b.shape[0]) == k
    tm, tn, tk = _tile(m, 1024), _tile(n, 1024), _tile(k, 1024)
    nk = k // tk
    ca, cb = (0 if ta else 1), (1 if tb else 0)
    n_ex, n_out = len(extras), len(out_dtypes)

    def body(*refs):
        a_ref, b_ref = refs[0], refs[1]
        ex = refs[2:2 + n_ex]
        outs = refs[2 + n_ex:2 + n_ex + n_out]
        acc = refs[-1]
        kk = pl.program_id(2)
        part = lax.dot_general(a_ref[...], b_ref[...], (((ca,), (cb,)), ((), ())), preferred_element_type=F32)

        def finish(r):
            vals = (r,) if epi is None else epi(r, *[e[...] for e in ex])
            for o, v in zip(outs, vals):
                o[...] = v.astype(o.dtype)

        if nk == 1:
            finish(part)
        else:
            @pl.when(kk == 0)
            def _():
                acc[...] = part

            @pl.when(kk > 0)
            def _():
                acc[...] += part

            @pl.when(kk == nk - 1)
            def _():
                finish(acc[...])

    a_spec = pl.BlockSpec((tk, tm), lambda i, j, kk: (kk, i)) if ta else pl.BlockSpec((tm, tk), lambda i, j, kk: (i, kk))
    b_spec = pl.BlockSpec((tn, tk), lambda i, j, kk: (j, kk)) if tb else pl.BlockSpec((tk, tn), lambda i, j, kk: (kk, j))
    io_spec = pl.BlockSpec((tm, tn), lambda i, j, kk: (i, j))
    res = pl.pallas_call(
        body, name=name, grid=(m // tm, n // tn, nk),
        in_specs=[a_spec, b_spec] + [io_spec] * n_ex,
        out_specs=[io_spec] * n_out,
        out_shape=[jax.ShapeDtypeStruct((m, n), dt) for dt in out_dtypes],
        scratch_shapes=[pltpu.VMEM((tm, tn) if nk > 1 else (8, 128), F32)],
        compiler_params=_cparams("parallel", "parallel", "arbitrary"),
    )(a, b, *extras)
    return res[0] if n_out == 1 else res


ROW_TILE = 512


def _norm_fwd(y, w, res=None, *, out_dtype, name):
    t, d = y.shape
    tm = _tile(t, ROW_TILE)
    has_res = res is not None

    def body(*refs):
        y_ref, w_ref = refs[0], refs[1]
        o_ref = refs[-1]
        yv = y_ref[...]
        r = lax.rsqrt(jnp.mean(yv * yv, axis=-1, keepdims=True) + EPS)
        out = yv * r * w_ref[...]
        if has_res:
            out = out + refs[2][...]
        o_ref[...] = out.astype(o_ref.dtype)

    row = pl.BlockSpec((tm, d), lambda i: (i, 0))
    vec = pl.BlockSpec((1, d), lambda i: (0, 0))
    args = (y, w.reshape(1, d)) + ((res,) if has_res else ())
    return pl.pallas_call(
        body, name=name, grid=(t // tm,), in_specs=[row, vec] + [row] * has_res, out_specs=row,
        out_shape=jax.ShapeDtypeStruct((t, d), out_dtype), compiler_params=_cparams("parallel"),
    )(*args)


def _norm_bwd(y, w, dout, add=None, *, out_dtype, name):
    t, d = y.shape
    tm = _tile(t, ROW_TILE)
    has_add = add is not None

    def body(*refs):
        y_ref, w_ref, do_ref = refs[0], refs[1], refs[2]
        dy_ref, dw_ref = refs[-2], refs[-1]
        yv = y_ref[...]
        r = lax.rsqrt(jnp.mean(yv * yv, axis=-1, keepdims=True) + EPS)
        yh = yv * r
        dov = do_ref[...].astype(F32)
        gw = dov * w_ref[...]
        dy = r * (gw - yh * jnp.mean(gw * yh, axis=-1, keepdims=True))
        if has_add:
            dy = dy + refs[3][...]
        dy_ref[...] = dy.astype(dy_ref.dtype)
        part = jnp.sum(dov * yh, axis=0, keepdims=True)

        @pl.when(pl.program_id(0) == 0)
        def _():
            dw_ref[...] = part

        @pl.when(pl.program_id(0) > 0)
        def _():
            dw_ref[...] += part

    row = pl.BlockSpec((tm, d), lambda i: (i, 0))
    vec = pl.BlockSpec((1, d), lambda i: (0, 0))
    args = (y, w.reshape(1, d), dout) + ((add,) if has_add else ())
    return pl.pallas_call(
        body, name=name, grid=(t // tm,), in_specs=[row, vec, row] + [row] * has_add, out_specs=[row, vec],
        out_shape=[jax.ShapeDtypeStruct((t, d), out_dtype), jax.ShapeDtypeStruct((1, d), F32)],
        compiler_params=_cparams("arbitrary"),
    )(*args)


def _shift_down(u, s):
    if s == 0:
        return u
    rows = lax.broadcasted_iota(jnp.int32, u.shape, 0)
    return jnp.where(rows >= s, pltpu.roll(u, s, 0), 0.0)


def _shift_up(u, s):
    if s == 0:
        return u
    t = u.shape[0]
    rows = lax.broadcasted_iota(jnp.int32, u.shape, 0)
    return jnp.where(rows < t - s, pltpu.roll(u, t - s, 0), 0.0)


def _conv_fwd(u, w):
    kk = w.shape[0]
    out = u * w[kk - 1:kk, :]
    for i in range(kk - 1):
        out = out + _shift_down(u, kk - 1 - i) * w[i:i + 1, :]
    return out


def _conv_bwd(u, w, dc):
    kk = w.shape[0]
    du = dc * w[kk - 1:kk, :]
    dws = []
    for i in range(kk):
        s = kk - 1 - i
        if s:
            du = du + _shift_up(dc, s) * w[i:i + 1, :]
        dws.append(jnp.sum(dc * _shift_down(u, s), axis=0, keepdims=True))
    return du, dws


def _gdn_pre_math(x, w, slab):
    c = _conv_fwd(x, w)
    sig = _sigmoid(c)
    s = c * sig
    r = lax.rsqrt(jnp.sum(s * s, axis=-1, keepdims=True) + EPS)
    scale = jnp.where(slab < GDN_HEADS, GDN_DIM ** -0.5, 1.0)
    return c, sig, s, r, scale


def _gdn_pre_fwd(proj, conv_w, *, name):
    t = proj.shape[0]
    nslab = 3 * GDN_HEADS

    def body(x_ref, w_ref, o_ref):
        slab = pl.program_id(0)
        _, _, s, r, scale = _gdn_pre_math(x_ref[...].astype(F32), w_ref[...], slab)
        o_ref[0] = jnp.where(slab < 2 * GDN_HEADS, s * r * scale, s)

    return pl.pallas_call(
        body, name=name, grid=(nslab,),
        in_specs=[pl.BlockSpec((t, GDN_DIM), lambda j: (0, j)), pl.BlockSpec((GDN_CONV, GDN_DIM), lambda j: (0, j))],
        out_specs=pl.BlockSpec((1, t, GDN_DIM), lambda j: (j, 0, 0)),
        out_shape=jax.ShapeDtypeStruct((nslab, t, GDN_DIM), F32), compiler_params=_cparams("parallel"),
    )(proj, conv_w)


def _gdn_pre_bwd(proj, conv_w, dqkv, *, name):
    t = proj.shape[0]
    nslab = 3 * GDN_HEADS

    def body(x_ref, w_ref, d_ref, dx_ref, dw_ref):
        slab = pl.program_id(0)
        x = x_ref[...].astype(F32)
        w = w_ref[...]
        c, sig, s, r, scale = _gdn_pre_math(x, w, slab)
        dout = d_ref[0]
        yn = s * r
        dn = dout * scale
        ds_norm = r * (dn - yn * jnp.sum(dn * yn, axis=-1, keepdims=True))
        ds = jnp.where(slab < 2 * GDN_HEADS, ds_norm, dout)
        dc = ds * (sig + c * sig * (1.0 - sig))
        dx, dws = _conv_bwd(x, w, dc)
        dx_ref[...] = dx.astype(dx_ref.dtype)
        for i, dwi in enumerate(dws):
            dw_ref[i:i + 1, :] = dwi

    return pl.pallas_call(
        body, name=name, grid=(nslab,),
        in_specs=[pl.BlockSpec((t, GDN_DIM), lambda j: (0, j)), pl.BlockSpec((GDN_CONV, GDN_DIM), lambda j: (0, j)),
                  pl.BlockSpec((1, t, GDN_DIM), lambda j: (j, 0, 0))],
        out_specs=[pl.BlockSpec((t, GDN_DIM), lambda j: (0, j)), pl.BlockSpec((GDN_CONV, GDN_DIM), lambda j: (0, j))],
        out_shape=[jax.ShapeDtypeStruct((t, 3 * GDN_W), BF16), jax.ShapeDtypeStruct((GDN_CONV, 3 * GDN_W), F32)],
        compiler_params=_cparams("parallel"),
    )(proj, conv_w, dqkv)


def _sc_specs(t):
    def col(base):
        return pl.BlockSpec((t, 128), lambda j: (0, base // 128 + j))
    return [col(C_SCX), col(C_SCB), col(C_SCC), pl.BlockSpec((SC_CONV, 128), lambda j: (0, j))]


def _sc_fwd(proj, conv_w, *, name):
    t = proj.shape[0]

    def body(x_ref, b_ref, c_ref, w_ref, o_ref):
        u = c_ref[...].astype(F32) * x_ref[...].astype(F32)
        o_ref[...] = (b_ref[...].astype(F32) * _conv_fwd(u, w_ref[...])).astype(o_ref.dtype)

    return pl.pallas_call(
        body, name=name, grid=(SC_W // 128,), in_specs=_sc_specs(t),
        out_specs=pl.BlockSpec((t, 128), lambda j: (0, j)),
        out_shape=jax.ShapeDtypeStruct((t, SC_W), BF16), compiler_params=_cparams("parallel"),
    )(proj, proj, proj, conv_w)


def _sc_bwd(proj, conv_w, dy, *, name):
    t = proj.shape[0]
    nj = SC_W // 128

    def body(x_ref, b_ref, c_ref, w_ref, dy_ref, dx_ref, db_ref, dc_ref, dw_ref):
        x, b, c = x_ref[...].astype(F32), b_ref[...].astype(F32), c_ref[...].astype(F32)
        w = w_ref[...]
        u = c * x
        dyv = dy_ref[...].astype(F32)
        db_ref[...] = (dyv * _conv_fwd(u, w)).astype(db_ref.dtype)
        du, dws = _conv_bwd(u, w, dyv * b)
        dx_ref[...] = (du * c).astype(dx_ref.dtype)
        dc_ref[...] = (du * x).astype(dc_ref.dtype)
        for i, dwi in enumerate(dws):
            dw_ref[i:i + 1, :] = dwi

    return pl.pallas_call(
        body, name=name, grid=(nj,),
        in_specs=_sc_specs(t) + [pl.BlockSpec((t, 128), lambda j: (0, j))],
        out_specs=[pl.BlockSpec((t, 128), lambda j: (0, j))] * 3 + [pl.BlockSpec((SC_CONV, 128), lambda j: (0, j))],
        out_shape=[jax.ShapeDtypeStruct((t, SC_W), BF16)] * 3 + [jax.ShapeDtypeStruct((SC_CONV, SC_W), F32)],
        compiler_params=_cparams("parallel"),
    )(proj, proj, proj, conv_w, dy)


def _tri_inv(a_strict):
    c = a_strict.shape[0]
    ri = lax.broadcasted_iota(jnp.int32, (c, c), 0)
    ci = lax.broadcasted_iota(jnp.int32, (c, c), 1)
    eye = (ri == ci).astype(F32)
    blk = 8
    bm = -jnp.where(ri // blk == ci // blk, a_strict, 0.0)
    inv = eye + bm
    pw = bm
    for _ in range(2):
        pw = _dot(pw, pw, hi=True)
        inv = inv + _dot(inv, pw, hi=True)
    while blk < c:
        off = jnp.where((ri // (2 * blk) == ci // (2 * blk)) & (ri // blk != ci // blk), a_strict, 0.0)
        inv = inv - _dot(_dot(inv, off, hi=True), inv, hi=True)
        blk *= 2
    return inv


def _gdn_chunk(q, k, v, ab, head, ea, dtb):
    c = q.shape[0]
    lane = lax.broadcasted_iota(jnp.int32, ab.shape, 1)
    a = jnp.sum(jnp.where(lane == head, ab, 0.0), axis=1, keepdims=True)
    b = jnp.sum(jnp.where(lane == GDN_HEADS + head, ab, 0.0), axis=1, keepdims=True)
    ri = lax.broadcasted_iota(jnp.int32, (c, c), 0)
    ci = lax.broadcasted_iota(jnp.int32, (c, c), 1)
    tri, strict = ri >= ci, ri > ci
    ltri = tri.astype(F32)
    beta = _sigmoid(b)
    sig_a = _sigmoid(a + dtb)
    g = -ea * _softplus(a + dtb)
    g_cc = jnp.broadcast_to(g, (c, c))
    gi = _dot(ltri, g_cc, hi=True)
    gj = _dot(g_cc, (ri <= ci).astype(F32), 0, 0, hi=True)
    decay = jnp.exp(jnp.where(tri, gi - gj, -1e30))
    gc = _dot(ltri, jnp.broadcast_to(g, (c, GDN_DIM)), hi=True)
    g_tot = jnp.sum(g, axis=0, keepdims=True)
    egc = jnp.exp(gc)
    ekd = jnp.exp(g_tot - gc)
    kb, vb = k * beta, v * beta
    kbg = kb * egc
    mkk = _dot(kb, k, 1, 1, hi=True)
    a_kk = jnp.where(strict, mkk * decay, 0.0)
    tinv = _tri_inv(a_kk)
    u = _dot(tinv, vb, hi=True)
    w = _dot(tinv, kbg, hi=True)
    mqk = _dot(q, k, 1, 1, hi=True)
    a_qk = jnp.where(tri, mqk * decay, 0.0)
    return dict(beta=beta, sig_a=sig_a, g=g, decay=decay, egc=egc, ekd=ekd, g_tot=g_tot, kb=kb, vb=vb, kbg=kbg,
                a_kk=a_kk, tinv=tinv, u=u, w=w, a_qk=a_qk, qd=q * egc, kd=k * ekd, tri=tri, strict=strict)


def _chunks_per_step(n):
    return 4 if n % 4 == 0 else 1


def _gdn_local_specs(t, cps):
    rows = cps * CHUNK

    def slab(base):
        return pl.BlockSpec((1, rows, GDN_DIM), lambda h, n: (base + h, n, 0))
    smem = pl.BlockSpec(memory_space=pltpu.SMEM)
    return [slab(0), slab(GDN_HEADS), slab(2 * GDN_HEADS), pl.BlockSpec((rows, AB_W), lambda h, n: (n, 0)), smem, smem]


def _scalar_row(ref, head):
    return jnp.full((1, 1), ref[0, head], F32)


def _gdn_local_fwd(qkv, ab, a_log, dt_bias, *, name):
    t = qkv.shape[1]
    n = t // CHUNK
    cps = _chunks_per_step(n)
    rows = cps * CHUNK

    def body(q_ref, k_ref, v_ref, ab_ref, al_ref, dt_ref, u_ref, w_ref, qd_ref, kd_ref, aqk_ref, gl_ref):
        head = pl.program_id(0)
        ea = jnp.exp(_scalar_row(al_ref, head))
        dtb = _scalar_row(dt_ref, head)
        for j in range(cps):
            sl = slice(j * CHUNK, (j + 1) * CHUNK)
            r = _gdn_chunk(q_ref[0, sl, :], k_ref[0, sl, :], v_ref[0, sl, :], ab_ref[sl, :], head, ea, dtb)
            u_ref[0, sl, :] = r["u"]
            w_ref[0, sl, :] = r["w"]
            qd_ref[0, sl, :] = r["qd"]
            kd_ref[0, sl, :] = r["kd"]
            aqk_ref[0, sl, :] = r["a_qk"]
            gl_ref[0, j] = jnp.exp(jnp.broadcast_to(r["g_tot"], (1, GDN_DIM)))

    big = pl.BlockSpec((1, rows, GDN_DIM), lambda h, i: (h, i, 0))
    big_shape = jax.ShapeDtypeStruct((GDN_HEADS, t, GDN_DIM), F32)
    return pl.pallas_call(
        body, name=name, grid=(GDN_HEADS, n // cps), in_specs=_gdn_local_specs(t, cps),
        out_specs=[big] * 4 + [pl.BlockSpec((1, rows, CHUNK), lambda h, i: (h, i, 0)),
                               pl.BlockSpec((1, cps, 1, GDN_DIM), lambda h, i: (h, i, 0, 0))],
        out_shape=[big_shape] * 4 + [jax.ShapeDtypeStruct((GDN_HEADS, t, CHUNK), F32),
                                     jax.ShapeDtypeStruct((GDN_HEADS, n, 1, GDN_DIM), F32)],
        compiler_params=_cparams("parallel", "parallel"),
    )(qkv, qkv, qkv, ab, a_log, dt_bias)


def _gdn_scan_fwd(u, w, qd, kd, aqk, gl, *, name):
    h, t, _ = u.shape
    n = t // CHUNK

    def body(u_ref, w_ref, qd_ref, kd_ref, aqk_ref, gl_ref, o_ref, s_ref, state):
        @pl.when(pl.program_id(0) == 0)
        def _():
            state[...] = jnp.zeros_like(state)

        for hh in range(h):
            s = state[hh]
            s_ref[hh, 0] = s
            vn = u_ref[hh] - _dot(w_ref[hh], s, hi=True)
            o_ref[hh] = _dot(qd_ref[hh], s, hi=True) + _dot(aqk_ref[hh], vn, hi=True)
            state[hh] = s * gl_ref[hh, 0] + _dot(kd_ref[hh], vn, 0, 0, hi=True)

    big = pl.BlockSpec((h, CHUNK, GDN_DIM), lambda i: (0, i, 0))
    return pl.pallas_call(
        body, name=name, grid=(n,),
        in_specs=[big] * 4 + [pl.BlockSpec((h, CHUNK, CHUNK), lambda i: (0, i, 0)),
                              pl.BlockSpec((h, 1, 1, GDN_DIM), lambda i: (0, i, 0, 0))],
        out_specs=[big, pl.BlockSpec((h, 1, GDN_DIM, GDN_DIM), lambda i: (0, i, 0, 0))],
        out_shape=[jax.ShapeDtypeStruct((h, t, GDN_DIM), F32), jax.ShapeDtypeStruct((h, n, GDN_DIM, GDN_DIM), F32)],
        scratch_shapes=[pltpu.VMEM((h, GDN_DIM, GDN_DIM), F32)],
        compiler_params=_cparams("arbitrary"),
    )(u, w, qd, kd, aqk, gl)


def _gdn_scan_bwd(u, w, qd, kd, aqk, gl, states, do, *, name):
    h, t, _ = u.shape
    n = t // CHUNK

    def body(u_ref, w_ref, qd_ref, kd_ref, aqk_ref, gl_ref, s_ref, do_ref,
             du_ref, dw_ref, dqd_ref, dkd_ref, daqk_ref, dgl_ref, dstate):
        @pl.when(pl.program_id(0) == 0)
        def _():
            dstate[...] = jnp.zeros_like(dstate)

        ri = lax.broadcasted_iota(jnp.int32, (CHUNK, CHUNK), 0)
        ci = lax.broadcasted_iota(jnp.int32, (CHUNK, CHUNK), 1)
        for hh in range(h):
            s, ds_next, dov, wv = s_ref[hh, 0], dstate[hh], do_ref[hh], w_ref[hh]
            vn = u_ref[hh] - _dot(wv, s, hi=True)
            dvn = _dot(aqk_ref[hh], dov, 0, 0, hi=True) + _dot(kd_ref[hh], ds_next, hi=True)
            du_ref[hh] = dvn
            dw_ref[hh] = -_dot(dvn, s, 1, 1, hi=True)
            dqd_ref[hh] = _dot(dov, s, 1, 1, hi=True)
            dkd_ref[hh] = _dot(vn, ds_next, 1, 1, hi=True)
            daqk_ref[hh] = jnp.where(ri >= ci, _dot(dov, vn, 1, 1, hi=True), 0.0)
            dgl_ref[hh, 0] = jnp.sum(ds_next * s, axis=0, keepdims=True)
            dstate[hh] = (_dot(qd_ref[hh], dov, 0, 0, hi=True) + ds_next * gl_ref[hh, 0]
                          - _dot(wv, dvn, 0, 0, hi=True))

    big = pl.BlockSpec((h, CHUNK, GDN_DIM), lambda i: (0, n - 1 - i, 0))
    sq = pl.BlockSpec((h, CHUNK, CHUNK), lambda i: (0, n - 1 - i, 0))
    glb = pl.BlockSpec((h, 1, 1, GDN_DIM), lambda i: (0, n - 1 - i, 0, 0))
    big_shape = jax.ShapeDtypeStruct((h, t, GDN_DIM), F32)
    return pl.pallas_call(
        body, name=name, grid=(n,),
        in_specs=[big] * 4 + [sq, glb, pl.BlockSpec((h, 1, GDN_DIM, GDN_DIM), lambda i: (0, n - 1 - i, 0, 0)), big],
        out_specs=[big] * 4 + [sq, glb],
        out_shape=[big_shape] * 4 + [jax.ShapeDtypeStruct((h, t, CHUNK), F32),
                                     jax.ShapeDtypeStruct((h, n, 1, GDN_DIM), F32)],
        scratch_shapes=[pltpu.VMEM((h, GDN_DIM, GDN_DIM), F32)],
        compiler_params=_cparams("arbitrary"),
    )(u, w, qd, kd, aqk, gl, states, do)


def _gdn_local_bwd(qkv, ab, a_log, dt_bias, du, dw, dqd, dkd, daqk, dgl, *, name):
    t = qkv.shape[1]
    n = t // CHUNK
    cps = _chunks_per_step(n)
    rows = cps * CHUNK

    def body(q_ref, k_ref, v_ref, ab_ref, al_ref, dt_ref, du_ref, dw_ref, dqd_ref, dkd_ref, daqk_ref, dgl_ref,
             dq_ref, dk_ref, dv_ref, dab_ref, dsc_ref):
        head = pl.program_id(0)
        ea = jnp.exp(_scalar_row(al_ref, head))
        dtb = _scalar_row(dt_ref, head)
        lane = lax.broadcasted_iota(jnp.int32, (CHUNK, AB_W), 1)
        lane1 = lax.broadcasted_iota(jnp.int32, (1, GDN_DIM), 1)
        ri = lax.broadcasted_iota(jnp.int32, (CHUNK, CHUNK), 0)
        ci = lax.broadcasted_iota(jnp.int32, (CHUNK, CHUNK), 1)
        utri = (ri <= ci).astype(F32)
        ones = jnp.ones((CHUNK, GDN_DIM), F32)
        acc_alog = jnp.zeros((1, 1), F32)
        acc_dtb = jnp.zeros((1, 1), F32)
        for j in range(cps):
            sl = slice(j * CHUNK, (j + 1) * CHUNK)
            q, k, v = q_ref[0, sl, :], k_ref[0, sl, :], v_ref[0, sl, :]
            r = _gdn_chunk(q, k, v, ab_ref[sl, :], head, ea, dtb)
            duv, dwv, dqdv, dkdv = du_ref[0, sl, :], dw_ref[0, sl, :], dqd_ref[0, sl, :], dkd_ref[0, sl, :]
            d_aqk = jnp.where(r["tri"], daqk_ref[0, sl, :], 0.0)
            dvb = _dot(r["tinv"], duv, 0, 0, hi=True)
            dkbg = _dot(r["tinv"], dwv, 0, 0, hi=True)
            d_akk = -jnp.where(r["strict"], _dot(dvb, r["u"], 1, 1, hi=True) + _dot(dkbg, r["w"], 1, 1, hi=True), 0.0)
            e = d_akk * r["a_kk"] + d_aqk * r["a_qk"]
            dmkk, dmqk = d_akk * r["decay"], d_aqk * r["decay"]
            dkb = _dot(dmkk, k, hi=True) + dkbg * r["egc"]
            dk = (_dot(dmkk, r["kb"], 0, 0, hi=True) + _dot(dmqk, q, 0, 0, hi=True) + dkdv * r["ekd"]
                  + dkb * r["beta"])
            dq = _dot(dmqk, k, hi=True) + dqdv * r["egc"]
            dq_ref[0, sl, :] = dq
            dk_ref[0, sl, :] = dk
            dv_ref[0, sl, :] = dvb * r["beta"]
            dbeta = jnp.sum(dkb * k + dvb * v, axis=1, keepdims=True)
            kd_term = jnp.sum(dkdv * r["kd"], axis=1, keepdims=True)
            dgc = (jnp.sum(e, axis=1, keepdims=True) + jnp.sum(dqdv * r["qd"] + dkbg * r["kbg"], axis=1, keepdims=True)
                   - kd_term)
            dgc_lanes = jnp.broadcast_to(dgc, (CHUNK, GDN_DIM)) - _dot(e, ones, 0, 0, hi=True)
            dgl_tot = jnp.sum(dgl_ref[0, j], axis=1, keepdims=True) * jnp.exp(r["g_tot"])
            d_tot = jnp.sum(kd_term, axis=0, keepdims=True) + dgl_tot
            dg = _dot(utri, dgc_lanes, hi=True) + d_tot
            dg = jnp.sum(jnp.where(lane == 0, dg, 0.0), axis=1, keepdims=True)
            da = dg * (-ea) * r["sig_a"]
            db = dbeta * r["beta"] * (1.0 - r["beta"])
            dab_ref[0, sl, :] = jnp.where(lane == head, da, 0.0) + jnp.where(lane == GDN_HEADS + head, db, 0.0)
            acc_alog = acc_alog + jnp.sum(dg * r["g"], axis=0, keepdims=True)
            acc_dtb = acc_dtb + jnp.sum(da, axis=0, keepdims=True)
        dsc_ref[0, 0] = jnp.where(lane1 == 0, acc_alog, 0.0) + jnp.where(lane1 == 1, acc_dtb, 0.0)

    big = pl.BlockSpec((1, rows, GDN_DIM), lambda h, i: (h, i, 0))
    big_shape = jax.ShapeDtypeStruct((GDN_HEADS, t, GDN_DIM), F32)
    return pl.pallas_call(
        body, name=name, grid=(GDN_HEADS, n // cps),
        in_specs=_gdn_local_specs(t, cps) + [big] * 4 + [pl.BlockSpec((1, rows, CHUNK), lambda h, i: (h, i, 0)),
                                                        pl.BlockSpec((1, cps, 1, GDN_DIM), lambda h, i: (h, i, 0, 0))],
        out_specs=[big] * 4 + [pl.BlockSpec((1, 1, 1, GDN_DIM), lambda h, i: (h, i, 0, 0))],
        out_shape=[big_shape] * 4 + [jax.ShapeDtypeStruct((GDN_HEADS, n // cps, 1, GDN_DIM), F32)],
        compiler_params=_cparams("parallel", "parallel"),
    )(qkv, qkv, qkv, ab, a_log, dt_bias, du, dw, dqd, dkd, daqk, dgl)


def _gdn_post_fwd(o, proj, norm_w, *, name):
    h, t, _ = o.shape
    tm = _tile(t, ROW_TILE)

    def body(o_ref, g_ref, w_ref, y_ref):
        for hh in range(h):
            sl = slice(hh * GDN_DIM, (hh + 1) * GDN_DIM)
            ov = o_ref[hh]
            gate = g_ref[:, sl].astype(F32)
            r = lax.rsqrt(jnp.mean(ov * ov, axis=-1, keepdims=True) + EPS)
            y_ref[:, sl] = (ov * r * w_ref[...] * (gate * _sigmoid(gate))).astype(y_ref.dtype)

    return pl.pallas_call(
        body, name=name, grid=(t // tm,),
        in_specs=[pl.BlockSpec((h, tm, GDN_DIM), lambda i: (0, i, 0)),
                  pl.BlockSpec((tm, GDN_W), lambda i: (i, C_GGATE // GDN_W)),
                  pl.BlockSpec((1, GDN_DIM), lambda i: (0, 0))],
        out_specs=pl.BlockSpec((tm, GDN_W), lambda i: (i, 0)),
        out_shape=jax.ShapeDtypeStruct((t, GDN_W), BF16), compiler_params=_cparams("parallel"),
    )(o, proj, norm_w.reshape(1, GDN_DIM))


def _gdn_post_bwd(o, proj, norm_w, dy, *, name):
    h, t, _ = o.shape
    tm = _tile(t, ROW_TILE)

    def body(o_ref, g_ref, w_ref, dy_ref, do_ref, dg_ref, dw_ref):
        part = jnp.zeros((1, GDN_DIM), F32)
        for hh in range(h):
            sl = slice(hh * GDN_DIM, (hh + 1) * GDN_DIM)
            ov = o_ref[hh]
            gate = g_ref[:, sl].astype(F32)
            sig = _sigmoid(gate)
            silu = gate * sig
            r = lax.rsqrt(jnp.mean(ov * ov, axis=-1, keepdims=True) + EPS)
            oh = ov * r
            dyv = dy_ref[:, sl].astype(F32)
            dg_ref[:, sl] = (dyv * oh * w_ref[...] * (sig + silu * (1.0 - sig))).astype(dg_ref.dtype)
            dn = dyv * silu
            part = part + jnp.sum(dn * oh, axis=0, keepdims=True)
            gw = dn * w_ref[...]
            do_ref[hh] = r * (gw - oh * jnp.mean(gw * oh, axis=-1, keepdims=True))

        @pl.when(pl.program_id(0) == 0)
        def _():
            dw_ref[...] = part

        @pl.when(pl.program_id(0) > 0)
        def _():
            dw_ref[...] += part

    return pl.pallas_call(
        body, name=name, grid=(t // tm,),
        in_specs=[pl.BlockSpec((h, tm, GDN_DIM), lambda i: (0, i, 0)),
                  pl.BlockSpec((tm, GDN_W), lambda i: (i, C_GGATE // GDN_W)),
                  pl.BlockSpec((1, GDN_DIM), lambda i: (0, 0)),
                  pl.BlockSpec((tm, GDN_W), lambda i: (i, 0))],
        out_specs=[pl.BlockSpec((h, tm, GDN_DIM), lambda i: (0, i, 0)), pl.BlockSpec((tm, GDN_W), lambda i: (i, 0)),
                   pl.BlockSpec((1, GDN_DIM), lambda i: (0, 0))],
        out_shape=[jax.ShapeDtypeStruct((h, t, GDN_DIM), F32), jax.ShapeDtypeStruct((t, GDN_W), BF16),
                   jax.ShapeDtypeStruct((1, GDN_DIM), F32)],
        compiler_params=_cparams("arbitrary"),
    )(o, proj, norm_w.reshape(1, GDN_DIM), dy)


def _split_dot(x, m):
    hi = x.astype(BF16)
    lo = (x - hi.astype(F32)).astype(BF16)
    return _dot(hi, m) + _dot(lo, m)


def _sb_block(q, kblk, qi, kb):
    blk = q.shape[0]
    z = _dot(q, kblk, 1, 1) * (SB_DIM ** -0.5)
    t_idx = qi * blk + lax.broadcasted_iota(jnp.int32, (blk, blk), 0)
    s_idx = kb * blk + lax.broadcasted_iota(jnp.int32, (blk, blk), 1)
    mask = s_idx < t_idx
    e = jnp.exp(-jnp.abs(z))
    sp = jnp.where(mask, jnp.maximum(z, 0.0) + jnp.log(1.0 + e), 0.0)
    return z, mask, e, sp


def _suffix_ones(blk):
    ri = lax.broadcasted_iota(jnp.int32, (blk, blk), 0)
    ci = lax.broadcasted_iota(jnp.int32, (blk, blk), 1)
    return (ri >= ci).astype(BF16)


def _sb_fwd(q, k, v, *, name):
    h, t, d = q.shape
    blk = _tile(t, SB_BLOCK)

    def body(q_ref, k_ref, v_ref, o_ref):
        qi = pl.program_id(1)
        qv = q_ref[0]
        suffix = _suffix_ones(blk)

        def step(it, carry):
            later, acc = carry
            kb = qi - it
            rows = pl.ds(pl.multiple_of(kb * blk, blk), blk)
            z, mask, _, sp = _sb_block(qv, k_ref[0, rows, :], qi, kb)
            csum = _split_dot(sp, suffix)
            att = jnp.where(mask, jnp.exp(z - csum - later), 0.0)
            acc = acc + _dot(att, v_ref[0, rows, :])
            return later + jnp.sum(sp, axis=1, keepdims=True), acc

        _, acc = lax.fori_loop(0, qi + 1, step, (jnp.zeros((blk, 1), F32), jnp.zeros((blk, d), F32)))
        o_ref[0] = acc

    qspec = pl.BlockSpec((1, blk, d), lambda hh, i: (hh, i, 0))
    kvspec = pl.BlockSpec((1, t, d), lambda hh, i: (hh, 0, 0))
    return pl.pallas_call(
        body, name=name, grid=(h, t // blk), in_specs=[qspec, kvspec, kvspec], out_specs=qspec,
        out_shape=jax.ShapeDtypeStruct((h, t, d), F32), compiler_params=_cparams("parallel", "parallel"),
    )(q, k, v)


def _prefix_ones(blk):
    ri = lax.broadcasted_iota(jnp.int32, (blk, blk), 0)
    ci = lax.broadcasted_iota(jnp.int32, (blk, blk), 1)
    return (ri <= ci).astype(BF16)


def _sb_bwd(q, k, v, do, *, name):
    h, t, d = q.shape
    blk = _tile(t, SB_BLOCK)
    nblk = t // blk
    scale = SB_DIM ** -0.5

    def body(q_ref, k_ref, v_ref, do_ref, dq_ref, dk_ref, dv_ref, dl_keep, sig_keep):
        qi = pl.program_id(1)

        @pl.when(qi == 0)
        def _():
            dk_ref[...] = jnp.zeros_like(dk_ref)
            dv_ref[...] = jnp.zeros_like(dv_ref)

        qv = q_ref[0]
        dov = do_ref[0]
        suffix = _suffix_ones(blk)
        prefix = _prefix_ones(blk)

        def back(it, later):
            kb = qi - it
            rows = pl.ds(pl.multiple_of(kb * blk, blk), blk)
            vblk = v_ref[0, rows, :]
            z, mask, e, sp = _sb_block(qv, k_ref[0, rows, :], qi, kb)
            csum = _split_dot(sp, suffix)
            att = jnp.where(mask, jnp.exp(z - csum - later), 0.0)
            dv_ref[0, rows, :] += _dot(att, dov, 0, 0)
            dl_keep[kb] = att * _dot(dov, vblk, 1, 1)
            sig_keep[kb] = jnp.where(mask, jnp.where(z >= 0, 1.0, e) / (1.0 + e), 0.0)
            return later + jnp.sum(sp, axis=1, keepdims=True)

        lax.fori_loop(0, qi + 1, back, jnp.zeros((blk, 1), F32))

        def forth(kb, carry):
            before, dq = carry
            rows = pl.ds(pl.multiple_of(kb * blk, blk), blk)
            dl = dl_keep[kb]
            dz = (dl - sig_keep[kb] * (before + _split_dot(dl, prefix))) * scale
            dk_ref[0, rows, :] += _dot(dz, qv, 0, 0)
            return before + jnp.sum(dl, axis=1, keepdims=True), dq + _dot(dz, k_ref[0, rows, :])

        _, dq = lax.fori_loop(0, qi + 1, forth, (jnp.zeros((blk, 1), F32), jnp.zeros((blk, d), F32)))
        dq_ref[0] = dq

    qspec = pl.BlockSpec((1, blk, d), lambda hh, i: (hh, i, 0))
    kvspec = pl.BlockSpec((1, t, d), lambda hh, i: (hh, 0, 0))
    shape = jax.ShapeDtypeStruct((h, t, d), F32)
    return pl.pallas_call(
        body, name=name, grid=(h, nblk), in_specs=[qspec, kvspec, kvspec, qspec],
        out_specs=[qspec, kvspec, kvspec], out_shape=[shape] * 3,
        scratch_shapes=[pltpu.VMEM((nblk, blk, blk), F32), pltpu.VMEM((nblk, blk, blk), F32)],
        compiler_params=_cparams("parallel", "arbitrary"),
    )(q, k, v, do)


def _gate_specs(tm):
    return [pl.BlockSpec((tm, D_MODEL), lambda i, b=b: (i, C_GATES // D_MODEL + b)) for b in range(3)]


def _merge_fwd(p, proj, *, name):
    t = proj.shape[0]
    tm = _tile(t, ROW_TILE)

    def body(p0, p1, p2, g0, g1, g2, o_ref):
        acc = jnp.zeros((tm, D_MODEL), F32)
        for pr, gr in ((p0, g0), (p1, g1), (p2, g2)):
            acc = acc + _sigmoid(gr[...].astype(F32)) * pr[...].astype(F32)
        o_ref[...] = acc.astype(o_ref.dtype)

    row = pl.BlockSpec((tm, D_MODEL), lambda i: (i, 0))
    return pl.pallas_call(
        body, name=name, grid=(t // tm,), in_specs=[row] * 3 + _gate_specs(tm), out_specs=row,
        out_shape=jax.ShapeDtypeStruct((t, D_MODEL), BF16), compiler_params=_cparams("parallel"),
    )(*p, proj, proj, proj)


def _merge_bwd(p, proj, dmerged, *, name):
    t = proj.shape[0]
    tm = _tile(t, ROW_TILE)

    def body(p0, p1, p2, g0, g1, g2, dm_ref, dp0, dp1, dp2, dg_ref):
        dm = dm_ref[...].astype(F32)
        for b, (pr, gr, dpr) in enumerate(((p0, g0, dp0), (p1, g1, dp1), (p2, g2, dp2))):
            s = _sigmoid(gr[...].astype(F32))
            dpr[...] = (dm * s).astype(dpr.dtype)
            dg_ref[:, b * D_MODEL:(b + 1) * D_MODEL] = (dm * pr[...].astype(F32) * s * (1.0 - s)).astype(dg_ref.dtype)

    row = pl.BlockSpec((tm, D_MODEL), lambda i: (i, 0))
    res = pl.pallas_call(
        body, name=name, grid=(t // tm,), in_specs=[row] * 3 + _gate_specs(tm) + [row],
        out_specs=[row] * 3 + [pl.BlockSpec((tm, 3 * D_MODEL), lambda i: (i, 0))],
        out_shape=[jax.ShapeDtypeStruct((t, D_MODEL), BF16)] * 3 + [jax.ShapeDtypeStruct((t, 3 * D_MODEL), BF16)],
        compiler_params=_cparams("parallel"),
    )(*p, proj, proj, proj, dmerged)
    return res[:3], res[3]


def _loss_head(y, target, *, name):
    t, d = y.shape
    tm = _tile(t, ROW_TILE)

    def body(y_ref, t_ref, dy_ref, l_ref):
        err = y_ref[...] - t_ref[...]
        dy_ref[...] = err * (1.0 / d)
        part = jnp.sum(err * err, axis=0, keepdims=True) * (0.5 / d)

        @pl.when(pl.program_id(0) == 0)
        def _():
            l_ref[...] = part

        @pl.when(pl.program_id(0) > 0)
        def _():
            l_ref[...] += part

    row = pl.BlockSpec((tm, d), lambda i: (i, 0))
    vec = pl.BlockSpec((1, d), lambda i: (0, 0))
    return pl.pallas_call(
        body, name=name, grid=(t // tm,), in_specs=[row, row], out_specs=[row, vec],
        out_shape=[jax.ShapeDtypeStruct((t, d), F32), jax.ShapeDtypeStruct((1, d), F32)],
        compiler_params=_cparams("arbitrary"),
    )(y, target)


def _adamw(w, g, m, v, *, name):
    r, c = w.shape
    tr = r if r * c * 4 <= 2 ** 21 else max(8, (2 ** 21 // (c * 4)) // 8 * 8)
    while r % tr:
        tr -= 8
    c1 = 1.0 - ADAM_B1 ** ADAM_STEP
    c2 = 1.0 - ADAM_B2 ** ADAM_STEP

    def body(w_ref, g_ref, m_ref, v_ref, d_ref, nm_ref, nv_ref):
        gv = g_ref[...]
        nm = ADAM_B1 * m_ref[...] + (1.0 - ADAM_B1) * gv
        nv = ADAM_B2 * v_ref[...] + (1.0 - ADAM_B2) * (gv * gv)
        nm_ref[...] = nm
        nv_ref[...] = nv
        d_ref[...] = -ADAM_LR * ((nm / c1) / (jnp.sqrt(nv / c2) + ADAM_EPS) + ADAM_WD * w_ref[...])

    spec = pl.BlockSpec((tr, c), lambda i: (i, 0))
    return pl.pallas_call(
        body, name=name, grid=(r // tr,), in_specs=[spec] * 4, out_specs=[spec] * 3,
        out_shape=[jax.ShapeDtypeStruct((r, c), F32)] * 3, compiler_params=_cparams("parallel"),
    )(w, g, m, v)


def _all_gather(x, *, name):
    shape = x.shape

    def body(x_ref, out_ref, send_sems, recv_sems, local_sem):
        mx, my, mc = lax.axis_index("x"), lax.axis_index("y"), lax.axis_index("c")
        me, sibling = (mx, my, mc), (mx, my, 1 - mc)
        chips = [(1 - mx, my), (mx, 1 - my), (1 - mx, 1 - my)]

        def slot(px, py, pc):
            return out_ref.at[4 * px + 2 * py + pc]

        def copy(k, block, to, src=None):
            return pltpu.make_async_remote_copy(
                src_ref=slot(*block) if src is None else src, dst_ref=slot(*block),
                send_sem=send_sems.at[k], recv_sem=recv_sems.at[k], device_id=to, device_id_type=MESH_ID)

        mine = pltpu.make_async_copy(x_ref, slot(*me), local_sem)
        mine.start()
        first = [copy(0, me, sibling, src=x_ref)]
        first += [copy(1 + j, me, (*chip, mc), src=x_ref) for j, chip in enumerate(chips)]
        for cp in first:
            cp.start()
        passed = [copy(4 + j, (*chip, mc), sibling) for j, chip in enumerate(chips)]
        for j, chip in enumerate(chips):
            copy(1 + j, (*chip, mc), me).wait_recv()
            passed[j].start()
        copy(0, sibling, me).wait_recv()
        for j, chip in enumerate(chips):
            copy(4 + j, (*chip, 1 - mc), me).wait_recv()
        for cp in first + passed:
            cp.wait_send()
        mine.wait()

    anyspace = pl.BlockSpec(memory_space=pl.ANY)
    return pl.pallas_call(
        body, name=name, in_specs=[anyspace], out_specs=anyspace,
        out_shape=jax.ShapeDtypeStruct((N_DEV,) + shape, x.dtype),
        scratch_shapes=[pltpu.SemaphoreType.DMA((7,)), pltpu.SemaphoreType.DMA((7,)), pltpu.SemaphoreType.DMA],
    )(x)


def _exchange_sibling(g, *, name):
    _, r, c = g.shape

    def body(g_ref, out_ref, send_sems, recv_sems):
        mx, my, mc = lax.axis_index("x"), lax.axis_index("y"), lax.axis_index("c")
        sibling = (mx, my, 1 - mc)
        copies = []
        for px in range(2):
            for py in range(2):
                kk = 2 * px + py
                copies.append(pltpu.make_async_remote_copy(
                    src_ref=g_ref.at[4 * px + 2 * py + (1 - mc)], dst_ref=out_ref.at[kk],
                    send_sem=send_sems.at[kk], recv_sem=recv_sems.at[kk], device_id=sibling, device_id_type=MESH_ID))
        for cp in copies:
            cp.start()
        for cp in copies:
            cp.wait_recv()
        for cp in copies:
            cp.wait_send()

    anyspace = pl.BlockSpec(memory_space=pl.ANY)
    return pl.pallas_call(
        body, name=name, in_specs=[anyspace], out_specs=anyspace,
        out_shape=jax.ShapeDtypeStruct((4, r, c), g.dtype),
        scratch_shapes=[pltpu.SemaphoreType.DMA((4,)), pltpu.SemaphoreType.DMA((4,))],
    )(g)


def _pair_sum(g, got, *, name):
    _, r, c = g.shape
    tr = _tile(r, 1024)

    def body(core_ref, a_ref, b_ref, o_ref):
        del core_ref
        o_ref[...] = (a_ref[...].astype(F32) + b_ref[...].astype(F32)).astype(o_ref.dtype)

    grid_spec = pltpu.PrefetchScalarGridSpec(
        num_scalar_prefetch=1, grid=(4, r // tr),
        in_specs=[pl.BlockSpec((1, tr, c), lambda kk, i, core: (2 * kk + core[0], i, 0)),
                  pl.BlockSpec((1, tr, c), lambda kk, i, core: (kk, i, 0))],
        out_specs=pl.BlockSpec((1, tr, c), lambda kk, i, core: (kk, i, 0)))
    return pl.pallas_call(
        body, name=name, grid_spec=grid_spec, out_shape=jax.ShapeDtypeStruct((4, r, c), g.dtype),
        compiler_params=_cparams("parallel", "parallel"),
    )(lax.axis_index("c").astype(jnp.int32).reshape(1), g, got)


def _exchange_chips(part, *, name):
    _, r, c = part.shape

    def body(p_ref, out_ref, send_sems, recv_sems):
        mx, my, mc = lax.axis_index("x"), lax.axis_index("y"), lax.axis_index("c")
        chips = [(1 - mx, my), (mx, 1 - my), (1 - mx, 1 - my)]
        copies = [pltpu.make_async_remote_copy(
            src_ref=p_ref.at[2 * px + py], dst_ref=out_ref.at[j], send_sem=send_sems.at[j], recv_sem=recv_sems.at[j],
            device_id=(px, py, mc), device_id_type=MESH_ID) for j, (px, py) in enumerate(chips)]
        for cp in copies:
            cp.start()
        for cp in copies:
            cp.wait_recv()
        for cp in copies:
            cp.wait_send()

    anyspace = pl.BlockSpec(memory_space=pl.ANY)
    return pl.pallas_call(
        body, name=name, in_specs=[anyspace], out_specs=anyspace,
        out_shape=jax.ShapeDtypeStruct((3, r, c), part.dtype),
        scratch_shapes=[pltpu.SemaphoreType.DMA((3,)), pltpu.SemaphoreType.DMA((3,))],
    )(part)


def _final_sum(part, got, *, name):
    _, r, c = part.shape
    tr = _tile(r, 1024)

    def body(chip_ref, a_ref, b_ref, o_ref):
        del chip_ref
        acc = a_ref[0].astype(F32)
        for j in range(3):
            acc = acc + b_ref[j].astype(F32)
        o_ref[...] = acc

    grid_spec = pltpu.PrefetchScalarGridSpec(
        num_scalar_prefetch=1, grid=(r // tr,),
        in_specs=[pl.BlockSpec((1, tr, c), lambda i, chip: (chip[0], i, 0)),
                  pl.BlockSpec((3, tr, c), lambda i, chip: (0, i, 0))],
        out_specs=pl.BlockSpec((tr, c), lambda i, chip: (i, 0)))
    chip = (2 * lax.axis_index("x") + lax.axis_index("y")).astype(jnp.int32).reshape(1)
    return pl.pallas_call(
        body, name=name, grid_spec=grid_spec, out_shape=jax.ShapeDtypeStruct((r, c), F32),
        compiler_params=_cparams("parallel"),
    )(chip, part, got)


def _sum_devices(x, *, name):
    _, r, c = x.shape

    def body(x_ref, o_ref):
        acc = x_ref[0]
        for j in range(1, N_DEV):
            acc = acc + x_ref[j]
        o_ref[...] = acc

    return pl.pallas_call(body, name=name, out_shape=jax.ShapeDtypeStruct((r, c), F32),
                          compiler_params=_cparams())(x)


BIG = ("w_in", "w_branch", "w_out", "w_ff1", "w_ff2")
BIG_AXIS = {"w_in": 2, "w_branch": 3, "w_out": 1, "w_ff1": 2, "w_ff2": 1}
FLAT_ROW_TILE = 256


def _flat_rows(n_elems):
    rows = -(-n_elems // FLAT_COLS)
    return -(-rows // FLAT_ROW_TILE) * FLAT_ROW_TILE


def _pack(blocks):
    flat = jnp.concatenate([b.reshape(-1) for b in blocks])
    rows = _flat_rows(flat.shape[0])
    return jnp.pad(flat, (0, rows * FLAT_COLS - flat.shape[0])).reshape(rows, FLAT_COLS)


def _unpack(flat, shapes):
    lead = flat.shape[:-2]
    flat = flat.reshape(lead + (-1,))
    out, off = [], 0
    for s in shapes:
        n = 1
        for dim in s:
            n *= dim
        out.append(flat[..., off:off + n].reshape(lead + tuple(s)))
        off += n
    return out


def _to_global(blocks, axis):
    moved = jnp.moveaxis(blocks, 0, axis)
    shp = moved.shape
    return moved.reshape(shp[:axis] + (shp[axis] * shp[axis + 1],) + shp[axis + 2:])


def _to_blocks(full, axis):
    shp = full.shape
    split = full.reshape(shp[:axis] + (N_DEV, shp[axis] // N_DEV) + shp[axis + 1:])
    return jnp.moveaxis(split, axis, 0)


def _vec_rows(n):
    return -(-n // 128 // 8) * 8


def _pack_vec(parts):
    flat = jnp.concatenate([p.reshape(-1).astype(F32) for p in parts])
    rows = _vec_rows(flat.shape[0])
    return jnp.pad(flat, (0, rows * 128 - flat.shape[0])).reshape(rows, 128)


def _unpack_vec(flat, shapes):
    lead = flat.shape[:-2]
    flat = flat.reshape(lead + (-1,))
    out, off = [], 0
    for s in shapes:
        n = 1
        for dim in s:
            n *= dim
        out.append(flat[..., off:off + n].reshape(lead + tuple(s)))
        off += n
    return out


def _heads_major(cols, heads, dim):
    t = cols.shape[0]
    return cols.reshape(t, heads, dim).transpose(1, 0, 2)


def _heads_minor(x):
    h, t, dim = x.shape
    return x.transpose(1, 0, 2).reshape(t, h * dim)


def _sb_qkv(proj):
    return [_heads_major(proj[:, C_SBQKV + i * SB_W:C_SBQKV + (i + 1) * SB_W], SB_HEADS, SB_DIM) for i in range(3)]


def _relu2_epilogue(r):
    a = jnp.maximum(r, 0.0)
    return r, a * a


def _relu2_bwd_epilogue(r, a):
    return (r * 2.0 * jnp.maximum(a.astype(F32), 0.0),)


def _layer_fwd(x, p):
    h = _norm_fwd(x, p["norm_mix_pre"], out_dtype=BF16, name="norm_pre_fwd")
    proj = _mm(h, p["w_main"], name="mm_in")
    ab = _mm(h, p["w_ab"], out_dtypes=(F32,), name="mm_ab")
    qkv = _gdn_pre_fwd(proj, p["conv_qkv_w"], name="gdn_pre_fwd")
    a_log, dt_bias = p["gdn_a_log"].reshape(1, GDN_HEADS), p["gdn_dt_bias"].reshape(1, GDN_HEADS)
    u, w, qd, kd, aqk, gl = _gdn_local_fwd(qkv, ab, a_log, dt_bias, name="gdn_local_fwd")
    o_gdn, states = _gdn_scan_fwd(u, w, qd, kd, aqk, gl, name="gdn_scan_fwd")
    y_a = _gdn_post_fwd(o_gdn, proj, p["gdn_norm_w"], name="gdn_post_fwd")
    sq, sk, sv = _sb_qkv(proj)
    o_sb = _sb_fwd(sq, sk, sv, name="sb_fwd")
    y_b = _heads_minor(o_sb).astype(BF16)
    y_c = _sc_fwd(proj, p["conv_sc_w"], name="sc_fwd")
    ys = (y_a, y_b, y_c)
    ps = tuple(_mm(ys[b], p["w_branch"][b], name="mm_branch") for b in range(3))
    merged = _merge_fwd(ps, proj, name="merge_fwd")
    mo = _mm(merged, p["w_out"], out_dtypes=(F32,), name="mm_out")
    x1 = _norm_fwd(mo, p["norm_mix_post"], x, out_dtype=F32, name="norm_post_fwd")
    h2 = _norm_fwd(x1, p["norm_ffn_pre"], out_dtype=BF16, name="norm_pre_fwd")
    a1, r1 = _mm(h2, p["w_ff1"], out_dtypes=(BF16, BF16), epi=_relu2_epilogue, name="mm_ff1")
    f = _mm(r1, p["w_ff2"], out_dtypes=(F32,), name="mm_ff2")
    x2 = _norm_fwd(f, p["norm_ffn_post"], x1, out_dtype=F32, name="norm_post_fwd")
    saved = dict(x=x, h=h, proj=proj, ab=ab, qkv=qkv, u=u, w=w, qd=qd, kd=kd, aqk=aqk, gl=gl, o_gdn=o_gdn,
                 states=states, sq=sq, sk=sk, sv=sv, ys=ys, ps=ps, merged=merged, mo=mo, x1=x1, h2=h2,
                 a1=a1, r1=r1, f=f)
    return x2, saved


def _layer_bwd(dx2, p, s):
    g = {}
    df, g["norm_ffn_post"] = _norm_bwd(s["f"], p["norm_ffn_post"], dx2, out_dtype=BF16, name="norm_bwd_b")
    da1 = _mm(df, p["w_ff2"], tb=True, epi=_relu2_bwd_epilogue, extras=(s["a1"],), name="mm_ff2_dx")
    g["w_ff2"] = _mm(s["r1"], df, ta=True, name="mm_ff2_dw")
    g["w_ff1"] = _mm(s["h2"], da1, ta=True, name="mm_ff1_dw")
    dh2 = _mm(da1, p["w_ff1"], tb=True, out_dtypes=(F32,), name="mm_ff1_dx")
    dx1, g["norm_ffn_pre"] = _norm_bwd(s["x1"], p["norm_ffn_pre"], dh2, dx2, out_dtype=F32, name="norm_bwd_f")
    dmo, g["norm_mix_post"] = _norm_bwd(s["mo"], p["norm_mix_post"], dx1, out_dtype=BF16, name="norm_bwd_b")
    dmerged = _mm(dmo, p["w_out"], tb=True, name="mm_out_dx")
    g["w_out"] = _mm(s["merged"], dmo, ta=True, name="mm_out_dw")
    dps, dgates = _merge_bwd(s["ps"], s["proj"], dmerged, name="merge_bwd")
    dys = [_mm(dps[b], p["w_branch"][b], tb=True, name="mm_branch_dx") for b in range(3)]
    g["w_branch"] = jnp.stack([_mm(s["ys"][b], dps[b], ta=True, name="mm_branch_dw") for b in range(3)])
    dscx, dscb, dscc, g["conv_sc_w"] = _sc_bwd(s["proj"], p["conv_sc_w"], dys[2], name="sc_bwd")
    do_sb = _heads_major(dys[1], SB_HEADS, SB_DIM).astype(F32)
    dsq, dsk, dsv = _sb_bwd(s["sq"], s["sk"], s["sv"], do_sb, name="sb_bwd")
    dsb = jnp.concatenate([_heads_minor(dsq), _heads_minor(dsk), _heads_minor(dsv)], axis=1).astype(BF16)
    a_log, dt_bias = p["gdn_a_log"].reshape(1, GDN_HEADS), p["gdn_dt_bias"].reshape(1, GDN_HEADS)
    do_gdn, dggate, g["gdn_norm_w"] = _gdn_post_bwd(s["o_gdn"], s["proj"], p["gdn_norm_w"], dys[0], name="gdn_post_bwd")
    du, dw, dqd, dkd, daqk, dgl = _gdn_scan_bwd(s["u"], s["w"], s["qd"], s["kd"], s["aqk"], s["gl"], s["states"],
                                                do_gdn, name="gdn_scan_bwd")
    dq, dk, dv, dab_h, dsc = _gdn_local_bwd(s["qkv"], s["ab"], a_log, dt_bias, du, dw, dqd, dkd, daqk, dgl,
                                            name="gdn_local_bwd")
    dsc = jnp.sum(dsc, axis=(1, 2))
    g["gdn_a_log"], g["gdn_dt_bias"] = dsc[:, 0], dsc[:, 1]
    dqkv = jnp.concatenate([dq, dk, dv], axis=0)
    dgqkv, g["conv_qkv_w"] = _gdn_pre_bwd(s["proj"], p["conv_qkv_w"], dqkv, name="gdn_pre_bwd")
    dab = jnp.sum(dab_h, axis=0).astype(BF16)
    dproj = jnp.concatenate([dgqkv, dggate, dsb, dscx, dscb, dscc, dgates], axis=1)
    g["w_main"] = _mm(s["h"], dproj, ta=True, name="mm_in_dw")
    g["w_ab"] = _mm(s["h"], dab, ta=True, out_dtypes=(F32,), name="mm_ab_dw")
    dh = _mm(dproj, p["w_main"], tb=True, out_dtypes=(F32,), name="mm_in_dx")
    dh_ab = _mm(dab, p["w_ab"], tb=True, out_dtypes=(F32,), name="mm_ab_dx")
    dx, g["norm_mix_pre"] = _norm_bwd(s["x"], p["norm_mix_pre"], dh + dh_ab, dx1, out_dtype=F32, name="norm_bwd_f")
    return dx, g


NORMS = ("norm_mix_pre", "norm_mix_post", "norm_ffn_pre", "norm_ffn_post")
SMALL = NORMS + ("gdn_a_log", "gdn_dt_bias", "gdn_norm_w")
CONVS = ("conv_qkv_w", "conv_sc_w")
AB_LO = 2048


def _split_w_in(w_in):
    main = jnp.concatenate([w_in[..., :AB_LO], w_in[..., AB_LO + 2 * GDN_HEADS:]], axis=-1)
    ab = w_in[..., AB_LO:AB_LO + 2 * GDN_HEADS]
    pad = [(0, 0)] * (ab.ndim - 1) + [(0, AB_W - 2 * GDN_HEADS)]
    return main, jnp.pad(ab, pad)


def _join_w_in(main, ab):
    return jnp.concatenate([main[..., :AB_LO], ab[..., :2 * GDN_HEADS].astype(main.dtype), main[..., AB_LO:]], axis=-1)


def kernel(x, norm_mix_pre, w_in, conv_qkv_w, gdn_a_log, gdn_dt_bias, gdn_norm_w, conv_sc_w, w_branch, w_out, norm_mix_post, norm_ffn_pre, w_ff1, w_ff2, norm_ffn_post, loss_target, m_norm_mix_pre, m_w_in, m_conv_qkv_w, m_gdn_a_log, m_gdn_dt_bias, m_gdn_norm_w, m_conv_sc_w, m_w_branch, m_w_out, m_norm_mix_post, m_norm_ffn_pre, m_w_ff1, m_w_ff2, m_norm_ffn_post, v_norm_mix_pre, v_w_in, v_conv_qkv_w, v_gdn_a_log, v_gdn_dt_bias, v_gdn_norm_w, v_conv_sc_w, v_w_branch, v_w_out, v_norm_mix_post, v_norm_ffn_pre, v_w_ff1, v_w_ff2, v_norm_ffn_post):
    names = ("norm_mix_pre", "w_in", "conv_qkv_w", "gdn_a_log", "gdn_dt_bias", "gdn_norm_w", "conv_sc_w", "w_branch",
             "w_out", "norm_mix_post", "norm_ffn_pre", "w_ff1", "w_ff2", "norm_ffn_post")
    w = dict(zip(names, (norm_mix_pre, w_in, conv_qkv_w, gdn_a_log, gdn_dt_bias, gdn_norm_w, conv_sc_w, w_branch,
                         w_out, norm_mix_post, norm_ffn_pre, w_ff1, w_ff2, norm_ffn_post)))
    m = dict(zip(names, (m_norm_mix_pre, m_w_in, m_conv_qkv_w, m_gdn_a_log, m_gdn_dt_bias, m_gdn_norm_w, m_conv_sc_w,
                         m_w_branch, m_w_out, m_norm_mix_post, m_norm_ffn_pre, m_w_ff1, m_w_ff2, m_norm_ffn_post)))
    v = dict(zip(names, (v_norm_mix_pre, v_w_in, v_conv_qkv_w, v_gdn_a_log, v_gdn_dt_bias, v_gdn_norm_w, v_conv_sc_w,
                         v_w_branch, v_w_out, v_norm_mix_post, v_norm_ffn_pre, v_w_ff1, v_w_ff2, v_norm_ffn_post)))
    me = 4 * lax.axis_index("x") + 2 * lax.axis_index("y") + lax.axis_index("c")

    big_shapes = [w[k].shape for k in BIG]
    gathered = _all_gather(_pack([w[k].astype(BF16) for k in BIG]), name="gather_weights")
    full = {k: _to_global(blk, BIG_AXIS[k]) for k, blk in zip(BIG, _unpack(gathered, big_shapes))}
    conv_shapes = [w[k].shape for k in CONVS]
    conv_all = _all_gather(_pack_vec([w[k] for k in CONVS]), name="gather_small")
    for k, blk in zip(CONVS, _unpack_vec(conv_all, conv_shapes)):
        full[k] = _to_global(blk, 2)
    full["w_main"], full["w_ab"] = _split_w_in(full.pop("w_in"))

    xs = x[0]
    layers = []
    for l in range(DEPTH):
        p = {k: full[k][l] for k in full}
        p.update({k: w[k][l] for k in SMALL})
        layers.append(p)

    saved = []
    for l in range(DEPTH):
        xs, s = _layer_fwd(xs, layers[l])
        saved.append(s)
    dy, loss_lanes = _loss_head(xs, loss_target[0], name="loss_head")

    grads = []
    for l in reversed(range(DEPTH)):
        dy, g = _layer_bwd(dy, layers[l], saved[l])
        grads.append(g)
    grads = grads[::-1]
    stack = {k: jnp.stack([g[k] for g in grads]) for k in grads[0]}
    stack["w_in"] = _join_w_in(stack.pop("w_main"), stack.pop("w_ab"))

    flat = _pack_grads(stack)
    got = _exchange_sibling(flat, name="rs_sibling")
    part = _pair_sum(flat, got, name="rs_pair_sum")
    got2 = _exchange_chips(part, name="rs_chips")
    mine = _final_sum(part, got2, name="rs_final_sum")
    gsum = dict(zip(BIG, _unpack(mine, big_shapes)))

    small_parts = [stack[k] for k in SMALL + CONVS] + [jnp.sum(loss_lanes).reshape(1)]
    small_shapes = [stack[k].shape for k in SMALL + CONVS] + [(1,)]
    summed = _sum_devices(_all_gather(_pack_vec(small_parts), name="gather_small_grads"), name="sum_small")
    small = _unpack_vec(summed, small_shapes)
    loss = small[-1][0]
    for k, val in zip(SMALL + CONVS, small[:-1]):
        gsum[k] = val
    for k in CONVS:
        per = gsum[k].shape[2] // N_DEV
        gsum[k] = lax.dynamic_slice_in_dim(gsum[k], me * per, per, axis=2)

    delta, new_m, new_v = {}, {}, {}
    for k in names:
        shp = w[k].shape
        two_d = (-1, shp[-1]) if len(shp) > 1 else (1, -1)
        d_, m_, v_ = _adamw(w[k].reshape(two_d), gsum[k].reshape(two_d), m[k].reshape(two_d), v[k].reshape(two_d),
                            name="adamw")
        delta[k], new_m[k], new_v[k] = d_.reshape(shp), m_.reshape(shp), v_.reshape(shp)

    return (loss, dy[None], *[gsum[k].reshape(w[k].shape) for k in names], *[delta[k] for k in names], *[new_m[k] for k in names],
            *[new_v[k] for k in names])


def _pack_grads(stack):
    blocks = [_to_blocks(stack[k].astype(BF16), BIG_AXIS[k]) for k in BIG]
    flat = jnp.concatenate([b.reshape(N_DEV, -1) for b in blocks], axis=1)
    rows = _flat_rows(flat.shape[1])
    return jnp.pad(flat, ((0, 0), (0, rows * FLAT_COLS - flat.shape[1]))).reshape(N_DEV, rows, FLAT_COLS)
```

```python
import functools

import jax
import jax.numpy as jnp
from jax import lax
from jax.experimental import pallas as pl
from jax.experimental.pallas import tpu as pltpu

F32, BF16 = jnp.float32, jnp.bfloat16
MESH_ID = pl.DeviceIdType.MESH

N_DEV = 8
DEPTH = 4
D_MODEL = 1024
D_FF = 4096
EPS = 1e-6
GDN_HEADS, GDN_DIM, GDN_CONV = 4, 128, 4
GDN_W = GDN_HEADS * GDN_DIM
CHUNK = 64
SB_HEADS, SB_DIM = 8, 64
SB_W = SB_HEADS * SB_DIM
SB_QBLOCK, SB_KBLOCK = 512, 128
SC_W, SC_CONV = 512, 3
IN_W = 8200
C_GQKV, C_GGATE, C_SBQKV, C_SCX, C_SCB, C_SCC, C_GATES, MAIN_W = 0, 1536, 2048, 3584, 4096, 4608, 5120, 8192
AB_W = 128

ADAM_LR, ADAM_B1, ADAM_B2, ADAM_EPS, ADAM_WD, ADAM_STEP = 0.001, 0.9, 0.999, 1e-08, 0.01, 10

VMEM_LIMIT = 48 * 2 ** 20


def _cparams(*sem):
    return pltpu.CompilerParams(dimension_semantics=sem or None, vmem_limit_bytes=VMEM_LIMIT)


def _tile(n, pref):
    if n <= pref:
        return n
    t = pref
    while n % t:
        t -= 128
    assert t > 0
    return t


def _dot(a, b, ca=1, cb=0):
    return lax.dot_general(a.astype(BF16), b.astype(BF16), (((ca,), (cb,)), ((), ())), preferred_element_type=F32)


def _split2(x):
    hi = x.astype(BF16)
    return hi, (x - hi.astype(F32)).astype(BF16)


def _dot3(a, b, ca=1, cb=0):
    a1, a2 = _split2(a)
    b1, b2 = _split2(b)
    return _dot(a1, b1, ca, cb) + (_dot(a1, b2, ca, cb) + _dot(a2, b1, ca, cb))


def _dot_exact(a, b, ca=1, cb=0, ones="a"):
    x = b if ones == "a" else a
    m = (a if ones == "a" else b).astype(BF16)
    hi, rest = x.astype(BF16), None
    rest = x - hi.astype(F32)
    mid = rest.astype(BF16)
    lo = (rest - mid.astype(F32)).astype(BF16)
    parts = [_dot(m, p, ca, cb) if ones == "a" else _dot(p, m, ca, cb) for p in (hi, mid, lo)]
    return parts[0] + (parts[1] + parts[2])


def _sigmoid(z):
    e = jnp.exp(-jnp.abs(z))
    return jnp.where(z >= 0, 1.0, e) / (1.0 + e)


def _softplus(z):
    return jnp.maximum(z, 0.0) + jnp.log(1.0 + jnp.exp(-jnp.abs(z)))


def _mm(a, b, *, name, ta=False, tb=False, out_dtypes=(BF16,), epi=None, extras=()):
    assert a.dtype == BF16 and b.dtype == BF16
    m, k = (a.shape[1], a.shape[0]) if ta else a.shape
    n = b.shape[0] if tb else b.shape[1]
    assert (b.shape[1] if tb else b.shape[0]) == k
    tm, tn, tk = _tile(m, 1024), _tile(n, 1024), _tile(k, 1024)
    nk = k // tk
    ca, cb = (0 if ta else 1), (1 if tb else 0)
    n_ex, n_out = len(extras), len(out_dtypes)

    def body(*refs):
        a_ref, b_ref = refs[0], refs[1]
        ex = refs[2:2 + n_ex]
        outs = refs[2 + n_ex:2 + n_ex + n_out]
        acc = refs[-1]
        kk = pl.program_id(2)
        part = lax.dot_general(a_ref[...], b_ref[...], (((ca,), (cb,)), ((), ())), preferred_element_type=F32)

        def finish(r):
            vals = (r,) if epi is None else epi(r, *[e[...] for e in ex])
            for o, v in zip(outs, vals):
                o[...] = v.astype(o.dtype)

        if nk == 1:
            finish(part)
        else:
            @pl.when(kk == 0)
            def _():
                acc[...] = part

            @pl.when(kk > 0)
            def _():
                acc[...] += part

            @pl.when(kk == nk - 1)
            def _():
                finish(acc[...])

    a_spec = pl.BlockSpec((tk, tm), lambda i, j, kk: (kk, i)) if ta else pl.BlockSpec((tm, tk), lambda i, j, kk: (i, kk))
    b_spec = pl.BlockSpec((tn, tk), lambda i, j, kk: (j, kk)) if tb else pl.BlockSpec((tk, tn), lambda i, j, kk: (kk, j))
    io_spec = pl.BlockSpec((tm, tn), lambda i, j, kk: (i, j))
    res = pl.pallas_call(
        body, name=name, grid=(m // tm, n // tn, nk),
        in_specs=[a_spec, b_spec] + [io_spec] * n_ex,
        out_specs=[io_spec] * n_out,
        out_shape=[jax.ShapeDtypeStruct((m, n), dt) for dt in out_dtypes],
        scratch_shapes=[pltpu.VMEM((tm, tn) if nk > 1 else (8, 128), F32)],
        compiler_params=_cparams("parallel", "parallel", "arbitrary"),
    )(a, b, *extras)
    return res[0] if n_out == 1 else res


ROW_TILE = 512


def _norm_fwd(y, w, res=None, *, out_dtype, name):
    t, d = y.shape
    tm = _tile(t, ROW_TILE)
    has_res = res is not None

    def body(*refs):
        y_ref, w_ref = refs[0], refs[1]
        o_ref = refs[-1]
        yv = y_ref[...]
        r = lax.rsqrt(jnp.mean(yv * yv, axis=-1, keepdims=True) + EPS)
        out = yv * r * w_ref[...]
        if has_res:
            out = out + refs[2][...]
        o_ref[...] = out.astype(o_ref.dtype)

    row = pl.BlockSpec((tm, d), lambda i: (i, 0))
    vec = pl.BlockSpec((1, d), lambda i: (0, 0))
    args = (y, w.reshape(1, d)) + ((res,) if has_res else ())
    return pl.pallas_call(
        body, name=name, grid=(t // tm,), in_specs=[row, vec] + [row] * has_res, out_specs=row,
        out_shape=jax.ShapeDtypeStruct((t, d), out_dtype), compiler_params=_cparams("parallel"),
    )(*args)


def _norm_bwd(y, w, dout, add=None, *, out_dtype, name):
    t, d = y.shape
    tm = _tile(t, ROW_TILE)
    has_add = add is not None

    def body(*refs):
        y_ref, w_ref, do_ref = refs[0], refs[1], refs[2]
        dy_ref, dw_ref = refs[-2], refs[-1]
        yv = y_ref[...]
        r = lax.rsqrt(jnp.mean(yv * yv, axis=-1, keepdims=True) + EPS)
        yh = yv * r
        dov = do_ref[...].astype(F32)
        gw = dov * w_ref[...]
        dy = r * (gw - yh * jnp.mean(gw * yh, axis=-1, keepdims=True))
        if has_add:
            dy = dy + refs[3][...]
        dy_ref[...] = dy.astype(dy_ref.dtype)
        part = jnp.sum(dov * yh, axis=0, keepdims=True)

        @pl.when(pl.program_id(0) == 0)
        def _():
            dw_ref[...] = part

        @pl.when(pl.program_id(0) > 0)
        def _():
            dw_ref[...] += part

    row = pl.BlockSpec((tm, d), lambda i: (i, 0))
    vec = pl.BlockSpec((1, d), lambda i: (0, 0))
    args = (y, w.reshape(1, d), dout) + ((add,) if has_add else ())
    return pl.pallas_call(
        body, name=name, grid=(t // tm,), in_specs=[row, vec, row] + [row] * has_add, out_specs=[row, vec],
        out_shape=[jax.ShapeDtypeStruct((t, d), out_dtype), jax.ShapeDtypeStruct((1, d), F32)],
        compiler_params=_cparams("arbitrary"),
    )(*args)


def _shift_down(u, s):
    if s == 0:
        return u
    rows = lax.broadcasted_iota(jnp.int32, u.shape, 0)
    return jnp.where(rows >= s, pltpu.roll(u, s, 0), 0.0)


def _shift_up(u, s):
    if s == 0:
        return u
    t = u.shape[0]
    rows = lax.broadcasted_iota(jnp.int32, u.shape, 0)
    return jnp.where(rows < t - s, pltpu.roll(u, t - s, 0), 0.0)


def _conv_fwd(u, w):
    kk = w.shape[0]
    out = u * w[kk - 1:kk, :]
    for i in range(kk - 1):
        out = out + _shift_down(u, kk - 1 - i) * w[i:i + 1, :]
    return out


def _conv_bwd(u, w, dc):
    kk = w.shape[0]
    du = dc * w[kk - 1:kk, :]
    dws = []
    for i in range(kk):
        s = kk - 1 - i
        if s:
            du = du + _shift_up(dc, s) * w[i:i + 1, :]
        dws.append(jnp.sum(dc * _shift_down(u, s), axis=0, keepdims=True))
    return du, dws


def _gdn_pre_math(x, w, slab):
    c = _conv_fwd(x, w)
    sig = _sigmoid(c)
    s = c * sig
    r = lax.rsqrt(jnp.sum(s * s, axis=-1, keepdims=True) + EPS)
    scale = jnp.where(slab < GDN_HEADS, GDN_DIM ** -0.5, 1.0)
    return c, sig, s, r, scale


def _gdn_pre_fwd(proj, conv_w, *, name):
    t = proj.shape[0]
    nslab = 3 * GDN_HEADS

    def body(x_ref, w_ref, o_ref):
        slab = pl.program_id(0)
        _, _, s, r, scale = _gdn_pre_math(x_ref[...].astype(F32), w_ref[...], slab)
        o_ref[0] = jnp.where(slab < 2 * GDN_HEADS, s * r * scale, s)

    return pl.pallas_call(
        body, name=name, grid=(nslab,),
        in_specs=[pl.BlockSpec((t, GDN_DIM), lambda j: (0, j)), pl.BlockSpec((GDN_CONV, GDN_DIM), lambda j: (0, j))],
        out_specs=pl.BlockSpec((1, t, GDN_DIM), lambda j: (j, 0, 0)),
        out_shape=jax.ShapeDtypeStruct((nslab, t, GDN_DIM), F32), compiler_params=_cparams("parallel"),
    )(proj, conv_w)


def _gdn_pre_bwd(proj, conv_w, dqkv, *, name):
    t = proj.shape[0]
    nslab = 3 * GDN_HEADS

    def body(x_ref, w_ref, d_ref, dx_ref, dw_ref):
        slab = pl.program_id(0)
        x = x_ref[...].astype(F32)
        w = w_ref[...]
        c, sig, s, r, scale = _gdn_pre_math(x, w, slab)
        dout = d_ref[0]
        yn = s * r
        dn = dout * scale
        ds_norm = r * (dn - yn * jnp.sum(dn * yn, axis=-1, keepdims=True))
        ds = jnp.where(slab < 2 * GDN_HEADS, ds_norm, dout)
        dc = ds * (sig + c * sig * (1.0 - sig))
        dx, dws = _conv_bwd(x, w, dc)
        dx_ref[...] = dx.astype(dx_ref.dtype)
        for i, dwi in enumerate(dws):
            dw_ref[i:i + 1, :] = dwi

    return pl.pallas_call(
        body, name=name, grid=(nslab,),
        in_specs=[pl.BlockSpec((t, GDN_DIM), lambda j: (0, j)), pl.BlockSpec((GDN_CONV, GDN_DIM), lambda j: (0, j)),
                  pl.BlockSpec((1, t, GDN_DIM), lambda j: (j, 0, 0))],
        out_specs=[pl.BlockSpec((t, GDN_DIM), lambda j: (0, j)), pl.BlockSpec((GDN_CONV, GDN_DIM), lambda j: (0, j))],
        out_shape=[jax.ShapeDtypeStruct((t, 3 * GDN_W), BF16), jax.ShapeDtypeStruct((GDN_CONV, 3 * GDN_W), F32)],
        compiler_params=_cparams("parallel"),
    )(proj, conv_w, dqkv)


def _sc_specs(t):
    def col(base):
        return pl.BlockSpec((t, 128), lambda j: (0, base // 128 + j))
    return [col(C_SCX), col(C_SCB), col(C_SCC), pl.BlockSpec((SC_CONV, 128), lambda j: (0, j))]


def _sc_fwd(proj, conv_w, *, name):
    t = proj.shape[0]

    def body(x_ref, b_ref, c_ref, w_ref, o_ref):
        u = c_ref[...].astype(F32) * x_ref[...].astype(F32)
        o_ref[...] = (b_ref[...].astype(F32) * _conv_fwd(u, w_ref[...])).astype(o_ref.dtype)

    return pl.pallas_call(
        body, name=name, grid=(SC_W // 128,), in_specs=_sc_specs(t),
        out_specs=pl.BlockSpec((t, 128), lambda j: (0, j)),
        out_shape=jax.ShapeDtypeStruct((t, SC_W), BF16), compiler_params=_cparams("parallel"),
    )(proj, proj, proj, conv_w)


def _sc_bwd(proj, conv_w, dy, *, name):
    t = proj.shape[0]
    nj = SC_W // 128

    def body(x_ref, b_ref, c_ref, w_ref, dy_ref, dx_ref, db_ref, dc_ref, dw_ref):
        x, b, c = x_ref[...].astype(F32), b_ref[...].astype(F32), c_ref[...].astype(F32)
        w = w_ref[...]
        u = c * x
        dyv = dy_ref[...].astype(F32)
        db_ref[...] = (dyv * _conv_fwd(u, w)).astype(db_ref.dtype)
        du, dws = _conv_bwd(u, w, dyv * b)
        dx_ref[...] = (du * c).astype(dx_ref.dtype)
        dc_ref[...] = (du * x).astype(dc_ref.dtype)
        for i, dwi in enumerate(dws):
            dw_ref[i:i + 1, :] = dwi

    return pl.pallas_call(
        body, name=name, grid=(nj,),
        in_specs=_sc_specs(t) + [pl.BlockSpec((t, 128), lambda j: (0, j))],
        out_specs=[pl.BlockSpec((t, 128), lambda j: (0, j))] * 3 + [pl.BlockSpec((SC_CONV, 128), lambda j: (0, j))],
        out_shape=[jax.ShapeDtypeStruct((t, SC_W), BF16)] * 3 + [jax.ShapeDtypeStruct((SC_CONV, SC_W), F32)],
        compiler_params=_cparams("parallel"),
    )(proj, proj, proj, conv_w, dy)


def _tri_inv(a_strict):
    c = a_strict.shape[0]
    ri = lax.broadcasted_iota(jnp.int32, (c, c), 0)
    ci = lax.broadcasted_iota(jnp.int32, (c, c), 1)
    eye = (ri == ci).astype(F32)
    blk = 8
    bm = -jnp.where(ri // blk == ci // blk, a_strict, 0.0)
    inv = eye + bm
    pw = bm
    for _ in range(2):
        pw = _dot3(pw, pw)
        inv = inv + _dot3(inv, pw)
    while blk < c:
        off = jnp.where((ri // (2 * blk) == ci // (2 * blk)) & (ri // blk != ci // blk), a_strict, 0.0)
        inv = inv - _dot3(_dot3(inv, off), inv)
        blk *= 2
    return inv


def _gdn_chunk(q, k, v, ab, head, ea, dtb):
    c = q.shape[0]
    lane = lax.broadcasted_iota(jnp.int32, ab.shape, 1)
    a = jnp.sum(jnp.where(lane == head, ab, 0.0), axis=1, keepdims=True)
    b = jnp.sum(jnp.where(lane == GDN_HEADS + head, ab, 0.0), axis=1, keepdims=True)
    ri = lax.broadcasted_iota(jnp.int32, (c, c), 0)
    ci = lax.broadcasted_iota(jnp.int32, (c, c), 1)
    tri, strict = ri >= ci, ri > ci
    ltri = tri.astype(F32)
    beta = _sigmoid(b)
    sig_a = _sigmoid(a + dtb)
    g = -ea * _softplus(a + dtb)
    g_cc = jnp.broadcast_to(g, (c, c))
    gi = _dot_exact(ltri, g_cc)
    gj = _dot_exact(g_cc, (ri <= ci).astype(F32), 0, 0, ones="b")
    decay = jnp.exp(jnp.where(tri, gi - gj, -1e30))
    gc = _dot_exact(ltri, jnp.broadcast_to(g, (c, GDN_DIM)))
    g_tot = jnp.sum(g, axis=0, keepdims=True)
    egc = jnp.exp(gc)
    ekd = jnp.exp(g_tot - gc)
    kb, vb = k * beta, v * beta
    kbg = kb * egc
    mkk = _dot3(kb, k, 1, 1)
    a_kk = jnp.where(strict, mkk * decay, 0.0)
    tinv = _tri_inv(a_kk)
    u = _dot3(tinv, vb)
    w = _dot3(tinv, kbg)
    mqk = _dot3(q, k, 1, 1)
    a_qk = jnp.where(tri, mqk * decay, 0.0)
    return dict(beta=beta, sig_a=sig_a, g=g, decay=decay, egc=egc, ekd=ekd, g_tot=g_tot, kb=kb, vb=vb, kbg=kbg,
                a_kk=a_kk, tinv=tinv, u=u, w=w, a_qk=a_qk, qd=q * egc, kd=k * ekd, tri=tri, strict=strict)


def _chunks_per_step(n):
    return 4 if n % 4 == 0 else 1


def _gdn_local_specs(t, cps):
    rows = cps * CHUNK

    def slab(base):
        return pl.BlockSpec((1, rows, GDN_DIM), lambda h, n: (base + h, n, 0))
    smem = pl.BlockSpec(memory_space=pltpu.SMEM)
    return [slab(0), slab(GDN_HEADS), slab(2 * GDN_HEADS), pl.BlockSpec((rows, AB_W), lambda h, n: (n, 0)), smem, smem]


def _scalar_row(ref, head):
    return jnp.full((1, 1), ref[0, head], F32)


def _gdn_local_fwd(qkv, ab, a_log, dt_bias, *, name):
    t = qkv.shape[1]
    n = t // CHUNK
    cps = _chunks_per_step(n)
    rows = cps * CHUNK

    def body(q_ref, k_ref, v_ref, ab_ref, al_ref, dt_ref, u_ref, w_ref, qd_ref, kd_ref, aqk_ref, gl_ref):
        head = pl.program_id(0)
        ea = jnp.exp(_scalar_row(al_ref, head))
        dtb = _scalar_row(dt_ref, head)
        for j in range(cps):
            sl = slice(j * CHUNK, (j + 1) * CHUNK)
            r = _gdn_chunk(q_ref[0, sl, :], k_ref[0, sl, :], v_ref[0, sl, :], ab_ref[sl, :], head, ea, dtb)
            u_ref[0, sl, :] = r["u"]
            w_ref[0, sl, :] = r["w"]
            qd_ref[0, sl, :] = r["qd"]
            kd_ref[0, sl, :] = r["kd"]
            aqk_ref[0, sl, :] = r["a_qk"]
            gl_ref[0, j] = jnp.exp(jnp.broadcast_to(r["g_tot"], (1, GDN_DIM)))

    big = pl.BlockSpec((1, rows, GDN_DIM), lambda h, i: (h, i, 0))
    big_shape = jax.ShapeDtypeStruct((GDN_HEADS, t, GDN_DIM), F32)
    return pl.pallas_call(
        body, name=name, grid=(GDN_HEADS, n // cps), in_specs=_gdn_local_specs(t, cps),
        out_specs=[big] * 4 + [pl.BlockSpec((1, rows, CHUNK), lambda h, i: (h, i, 0)),
                               pl.BlockSpec((1, cps, 1, GDN_DIM), lambda h, i: (h, i, 0, 0))],
        out_shape=[big_shape] * 4 + [jax.ShapeDtypeStruct((GDN_HEADS, t, CHUNK), F32),
                                     jax.ShapeDtypeStruct((GDN_HEADS, n, 1, GDN_DIM), F32)],
        compiler_params=_cparams("parallel", "parallel"),
    )(qkv, qkv, qkv, ab, a_log, dt_bias)


def _gdn_scan_fwd(u, w, qd, kd, aqk, gl, *, name):
    h, t, _ = u.shape
    n = t // CHUNK

    def body(u_ref, w_ref, qd_ref, kd_ref, aqk_ref, gl_ref, o_ref, s_ref, state):
        @pl.when(pl.program_id(0) == 0)
        def _():
            state[...] = jnp.zeros_like(state)

        for hh in range(h):
            s = state[hh]
            s_ref[hh, 0] = s
            vn = u_ref[hh] - _dot3(w_ref[hh], s)
            o_ref[hh] = _dot3(qd_ref[hh], s) + _dot3(aqk_ref[hh], vn)
            state[hh] = s * gl_ref[hh, 0] + _dot3(kd_ref[hh], vn, 0, 0)

    big = pl.BlockSpec((h, CHUNK, GDN_DIM), lambda i: (0, i, 0))
    return pl.pallas_call(
        body, name=name, grid=(n,),
        in_specs=[big] * 4 + [pl.BlockSpec((h, CHUNK, CHUNK), lambda i: (0, i, 0)),
                              pl.BlockSpec((h, 1, 1, GDN_DIM), lambda i: (0, i, 0, 0))],
        out_specs=[big, pl.BlockSpec((h, 1, GDN_DIM, GDN_DIM), lambda i: (0, i, 0, 0))],
        out_shape=[jax.ShapeDtypeStruct((h, t, GDN_DIM), F32), jax.ShapeDtypeStruct((h, n, GDN_DIM, GDN_DIM), F32)],
        scratch_shapes=[pltpu.VMEM((h, GDN_DIM, GDN_DIM), F32)],
        compiler_params=_cparams("arbitrary"),
    )(u, w, qd, kd, aqk, gl)


def _gdn_scan_bwd(u, w, qd, kd, aqk, gl, states, do, *, name):
    h, t, _ = u.shape
    n = t // CHUNK

    def body(u_ref, w_ref, qd_ref, kd_ref, aqk_ref, gl_ref, s_ref, do_ref,
             du_ref, dw_ref, dqd_ref, dkd_ref, daqk_ref, dgl_ref, dstate):
        @pl.when(pl.program_id(0) == 0)
        def _():
            dstate[...] = jnp.zeros_like(dstate)

        ri = lax.broadcasted_iota(jnp.int32, (CHUNK, CHUNK), 0)
        ci = lax.broadcasted_iota(jnp.int32, (CHUNK, CHUNK), 1)
        for hh in range(h):
            s, ds_next, dov, wv = s_ref[hh, 0], dstate[hh], do_ref[hh], w_ref[hh]
            vn = u_ref[hh] - _dot3(wv, s)
            dvn = _dot3(aqk_ref[hh], dov, 0, 0) + _dot3(kd_ref[hh], ds_next)
            du_ref[hh] = dvn
            dw_ref[hh] = -_dot3(dvn, s, 1, 1)
            dqd_ref[hh] = _dot3(dov, s, 1, 1)
            dkd_ref[hh] = _dot3(vn, ds_next, 1, 1)
            daqk_ref[hh] = jnp.where(ri >= ci, _dot3(dov, vn, 1, 1), 0.0)
            dgl_ref[hh, 0] = jnp.sum(ds_next * s, axis=0, keepdims=True)
            dstate[hh] = (_dot3(qd_ref[hh], dov, 0, 0) + ds_next * gl_ref[hh, 0]
                          - _dot3(wv, dvn, 0, 0))

    big = pl.BlockSpec((h, CHUNK, GDN_DIM), lambda i: (0, n - 1 - i, 0))
    sq = pl.BlockSpec((h, CHUNK, CHUNK), lambda i: (0, n - 1 - i, 0))
    glb = pl.BlockSpec((h, 1, 1, GDN_DIM), lambda i: (0, n - 1 - i, 0, 0))
    big_shape = jax.ShapeDtypeStruct((h, t, GDN_DIM), F32)
    return pl.pallas_call(
        body, name=name, grid=(n,),
        in_specs=[big] * 4 + [sq, glb, pl.BlockSpec((h, 1, GDN_DIM, GDN_DIM), lambda i: (0, n - 1 - i, 0, 0)), big],
        out_specs=[big] * 4 + [sq, glb],
        out_shape=[big_shape] * 4 + [jax.ShapeDtypeStruct((h, t, CHUNK), F32),
                                     jax.ShapeDtypeStruct((h, n, 1, GDN_DIM), F32)],
        scratch_shapes=[pltpu.VMEM((h, GDN_DIM, GDN_DIM), F32)],
        compiler_params=_cparams("arbitrary"),
    )(u, w, qd, kd, aqk, gl, states, do)


def _gdn_local_bwd(qkv, ab, a_log, dt_bias, du, dw, dqd, dkd, daqk, dgl, *, name):
    t = qkv.shape[1]
    n = t // CHUNK
    cps = _chunks_per_step(n)
    rows = cps * CHUNK

    def body(q_ref, k_ref, v_ref, ab_ref, al_ref, dt_ref, du_ref, dw_ref, dqd_ref, dkd_ref, daqk_ref, dgl_ref,
             dq_ref, dk_ref, dv_ref, dab_ref, dsc_ref):
        head = pl.program_id(0)
        ea = jnp.exp(_scalar_row(al_ref, head))
        dtb = _scalar_row(dt_ref, head)
        lane = lax.broadcasted_iota(jnp.int32, (CHUNK, AB_W), 1)
        lane1 = lax.broadcasted_iota(jnp.int32, (1, GDN_DIM), 1)
        ri = lax.broadcasted_iota(jnp.int32, (CHUNK, CHUNK), 0)
        ci = lax.broadcasted_iota(jnp.int32, (CHUNK, CHUNK), 1)
        utri = (ri <= ci).astype(F32)
        ones = jnp.ones((CHUNK, GDN_DIM), F32)
        acc_alog = jnp.zeros((1, 1), F32)
        acc_dtb = jnp.zeros((1, 1), F32)
        for j in range(cps):
            sl = slice(j * CHUNK, (j + 1) * CHUNK)
            q, k, v = q_ref[0, sl, :], k_ref[0, sl, :], v_ref[0, sl, :]
            r = _gdn_chunk(q, k, v, ab_ref[sl, :], head, ea, dtb)
            duv, dwv, dqdv, dkdv = du_ref[0, sl, :], dw_ref[0, sl, :], dqd_ref[0, sl, :], dkd_ref[0, sl, :]
            d_aqk = jnp.where(r["tri"], daqk_ref[0, sl, :], 0.0)
            dvb = _dot3(r["tinv"], duv, 0, 0)
            dkbg = _dot3(r["tinv"], dwv, 0, 0)
            d_akk = -jnp.where(r["strict"], _dot3(dvb, r["u"], 1, 1) + _dot3(dkbg, r["w"], 1, 1), 0.0)
            e = d_akk * r["a_kk"] + d_aqk * r["a_qk"]
            dmkk, dmqk = d_akk * r["decay"], d_aqk * r["decay"]
            dkb = _dot3(dmkk, k) + dkbg * r["egc"]
            dk = (_dot3(dmkk, r["kb"], 0, 0) + _dot3(dmqk, q, 0, 0) + dkdv * r["ekd"]
                  + dkb * r["beta"])
            dq = _dot3(dmqk, k) + dqdv * r["egc"]
            dq_ref[0, sl, :] = dq
            dk_ref[0, sl, :] = dk
            dv_ref[0, sl, :] = dvb * r["beta"]
            dbeta = jnp.sum(dkb * k + dvb * v, axis=1, keepdims=True)
            kd_term = jnp.sum(dkdv * r["kd"], axis=1, keepdims=True)
            dgc = (jnp.sum(e, axis=1, keepdims=True) + jnp.sum(dqdv * r["qd"] + dkbg * r["kbg"], axis=1, keepdims=True)
                   - kd_term)
            dgc_lanes = jnp.broadcast_to(dgc, (CHUNK, GDN_DIM)) - _dot_exact(e, ones, 0, 0, ones="b")
            dgl_tot = jnp.sum(dgl_ref[0, j], axis=1, keepdims=True) * jnp.exp(r["g_tot"])
            d_tot = jnp.sum(kd_term, axis=0, keepdims=True) + dgl_tot
            dg = _dot_exact(utri, dgc_lanes) + d_tot
            dg = jnp.sum(jnp.where(lane == 0, dg, 0.0), axis=1, keepdims=True)
            da = dg * (-ea) * r["sig_a"]
            db = dbeta * r["beta"] * (1.0 - r["beta"])
            dab_ref[0, sl, :] = jnp.where(lane == head, da, 0.0) + jnp.where(lane == GDN_HEADS + head, db, 0.0)
            acc_alog = acc_alog + jnp.sum(dg * r["g"], axis=0, keepdims=True)
            acc_dtb = acc_dtb + jnp.sum(da, axis=0, keepdims=True)
        dsc_ref[0, 0] = jnp.where(lane1 == 0, acc_alog, 0.0) + jnp.where(lane1 == 1, acc_dtb, 0.0)

    big = pl.BlockSpec((1, rows, GDN_DIM), lambda h, i: (h, i, 0))
    big_shape = jax.ShapeDtypeStruct((GDN_HEADS, t, GDN_DIM), F32)
    return pl.pallas_call(
        body, name=name, grid=(GDN_HEADS, n // cps),
        in_specs=_gdn_local_specs(t, cps) + [big] * 4 + [pl.BlockSpec((1, rows, CHUNK), lambda h, i: (h, i, 0)),
                                                        pl.BlockSpec((1, cps, 1, GDN_DIM), lambda h, i: (h, i, 0, 0))],
        out_specs=[big] * 4 + [pl.BlockSpec((1, 1, 1, GDN_DIM), lambda h, i: (h, i, 0, 0))],
        out_shape=[big_shape] * 4 + [jax.ShapeDtypeStruct((GDN_HEADS, n // cps, 1, GDN_DIM), F32)],
        compiler_params=_cparams("parallel", "parallel"),
    )(qkv, qkv, qkv, ab, a_log, dt_bias, du, dw, dqd, dkd, daqk, dgl)


def _gdn_post_fwd(o, proj, norm_w, *, name):
    h, t, _ = o.shape
    tm = _tile(t, ROW_TILE)

    def body(o_ref, g_ref, w_ref, y_ref):
        for hh in range(h):
            sl = slice(hh * GDN_DIM, (hh + 1) * GDN_DIM)
            ov = o_ref[hh]
            gate = g_ref[:, sl].astype(F32)
            r = lax.rsqrt(jnp.mean(ov * ov, axis=-1, keepdims=True) + EPS)
            y_ref[:, sl] = (ov * r * w_ref[...] * (gate * _sigmoid(gate))).astype(y_ref.dtype)

    return pl.pallas_call(
        body, name=name, grid=(t // tm,),
        in_specs=[pl.BlockSpec((h, tm, GDN_DIM), lambda i: (0, i, 0)),
                  pl.BlockSpec((tm, GDN_W), lambda i: (i, C_GGATE // GDN_W)),
                  pl.BlockSpec((1, GDN_DIM), lambda i: (0, 0))],
        out_specs=pl.BlockSpec((tm, GDN_W), lambda i: (i, 0)),
        out_shape=jax.ShapeDtypeStruct((t, GDN_W), BF16), compiler_params=_cparams("parallel"),
    )(o, proj, norm_w.reshape(1, GDN_DIM))


def _gdn_post_bwd(o, proj, norm_w, dy, *, name):
    h, t, _ = o.shape
    tm = _tile(t, ROW_TILE)

    def body(o_ref, g_ref, w_ref, dy_ref, do_ref, dg_ref, dw_ref):
        part = jnp.zeros((1, GDN_DIM), F32)
        for hh in range(h):
            sl = slice(hh * GDN_DIM, (hh + 1) * GDN_DIM)
            ov = o_ref[hh]
            gate = g_ref[:, sl].astype(F32)
            sig = _sigmoid(gate)
            silu = gate * sig
            r = lax.rsqrt(jnp.mean(ov * ov, axis=-1, keepdims=True) + EPS)
            oh = ov * r
            dyv = dy_ref[:, sl].astype(F32)
            dg_ref[:, sl] = (dyv * oh * w_ref[...] * (sig + silu * (1.0 - sig))).astype(dg_ref.dtype)
            dn = dyv * silu
            part = part + jnp.sum(dn * oh, axis=0, keepdims=True)
            gw = dn * w_ref[...]
            do_ref[hh] = r * (gw - oh * jnp.mean(gw * oh, axis=-1, keepdims=True))

        @pl.when(pl.program_id(0) == 0)
        def _():
            dw_ref[...] = part

        @pl.when(pl.program_id(0) > 0)
        def _():
            dw_ref[...] += part

    return pl.pallas_call(
        body, name=name, grid=(t // tm,),
        in_specs=[pl.BlockSpec((h, tm, GDN_DIM), lambda i: (0, i, 0)),
                  pl.BlockSpec((tm, GDN_W), lambda i: (i, C_GGATE // GDN_W)),
                  pl.BlockSpec((1, GDN_DIM), lambda i: (0, 0)),
                  pl.BlockSpec((tm, GDN_W), lambda i: (i, 0))],
        out_specs=[pl.BlockSpec((h, tm, GDN_DIM), lambda i: (0, i, 0)), pl.BlockSpec((tm, GDN_W), lambda i: (i, 0)),
                   pl.BlockSpec((1, GDN_DIM), lambda i: (0, 0))],
        out_shape=[jax.ShapeDtypeStruct((h, t, GDN_DIM), F32), jax.ShapeDtypeStruct((t, GDN_W), BF16),
                   jax.ShapeDtypeStruct((1, GDN_DIM), F32)],
        compiler_params=_cparams("arbitrary"),
    )(o, proj, norm_w.reshape(1, GDN_DIM), dy)


def _split_dot(x, m):
    hi = x.astype(BF16)
    lo = (x - hi.astype(F32)).astype(BF16)
    return _dot(hi, m) + _dot(lo, m)


def _sb_block(q, kblk, qi, kb):
    bq, bk = q.shape[0], kblk.shape[0]
    z = _dot(q, kblk, 1, 1) * (SB_DIM ** -0.5)
    t_idx = qi * bq + lax.broadcasted_iota(jnp.int32, (bq, bk), 0)
    s_idx = kb * bk + lax.broadcasted_iota(jnp.int32, (bq, bk), 1)
    mask = s_idx < t_idx
    e = jnp.exp(-jnp.abs(z))
    sp = jnp.where(mask, jnp.maximum(z, 0.0) + jnp.log(1.0 + e), 0.0)
    return z, mask, e, sp


def _suffix_ones(blk):
    ri = lax.broadcasted_iota(jnp.int32, (blk, blk), 0)
    ci = lax.broadcasted_iota(jnp.int32, (blk, blk), 1)
    return (ri >= ci).astype(BF16)


def _prefix_ones(blk):
    ri = lax.broadcasted_iota(jnp.int32, (blk, blk), 0)
    ci = lax.broadcasted_iota(jnp.int32, (blk, blk), 1)
    return (ri <= ci).astype(BF16)


def _sb_blocks(t):
    bq = _tile(t, SB_QBLOCK)
    bk = _tile(bq, SB_KBLOCK)
    return bq, bk


def _sb_fwd(q, k, v, *, name):
    h, t, d = q.shape
    bq, bk = _sb_blocks(t)

    def body(q_ref, k_ref, v_ref, o_ref):
        qi = pl.program_id(1)
        qv = q_ref[0]
        suffix = _suffix_ones(bk)
        nkb = (qi + 1) * (bq // bk)

        def step(it, carry):
            later, acc = carry
            kb = nkb - 1 - it
            rows = pl.ds(pl.multiple_of(kb * bk, bk), bk)
            z, mask, _, sp = _sb_block(qv, k_ref[0, rows, :], qi, kb)
            csum = _split_dot(sp, suffix)
            att = jnp.where(mask, jnp.exp(z - csum - later), 0.0)
            acc = acc + _dot(att, v_ref[0, rows, :])
            return later + jnp.sum(sp, axis=1, keepdims=True), acc

        _, acc = lax.fori_loop(0, nkb, step, (jnp.zeros((bq, 1), F32), jnp.zeros((bq, d), F32)))
        o_ref[0] = acc

    qspec = pl.BlockSpec((1, bq, d), lambda hh, i: (hh, i, 0))
    kvspec = pl.BlockSpec((1, t, d), lambda hh, i: (hh, 0, 0))
    return pl.pallas_call(
        body, name=name, grid=(h, t // bq), in_specs=[qspec, kvspec, kvspec], out_specs=qspec,
        out_shape=jax.ShapeDtypeStruct((h, t, d), F32), compiler_params=_cparams("parallel", "parallel"),
    )(q, k, v)


def _sb_bwd(q, k, v, do, *, name):
    h, t, d = q.shape
    bq, bk = _sb_blocks(t)
    scale = SB_DIM ** -0.5

    def body(q_ref, k_ref, v_ref, do_ref, dq_ref, dk_ref, dv_ref, dl_keep, sig_keep):
        qi = pl.program_id(1)

        @pl.when(qi == 0)
        def _():
            dk_ref[...] = jnp.zeros_like(dk_ref)
            dv_ref[...] = jnp.zeros_like(dv_ref)

        qv = q_ref[0]
        dov = do_ref[0]
        suffix = _suffix_ones(bk)
        prefix = _prefix_ones(bk)
        nkb = (qi + 1) * (bq // bk)

        def back(it, later):
            kb = nkb - 1 - it
            rows = pl.ds(pl.multiple_of(kb * bk, bk), bk)
            vblk = v_ref[0, rows, :]
            z, mask, e, sp = _sb_block(qv, k_ref[0, rows, :], qi, kb)
            csum = _split_dot(sp, suffix)
            att = jnp.where(mask, jnp.exp(z - csum - later), 0.0)
            dv_ref[0, rows, :] += _dot(att, dov, 0, 0)
            dl_keep[kb] = att * _dot(dov, vblk, 1, 1)
            sig_keep[kb] = jnp.where(mask, jnp.where(z >= 0, 1.0, e) / (1.0 + e), 0.0)
            return later + jnp.sum(sp, axis=1, keepdims=True)

        lax.fori_loop(0, nkb, back, jnp.zeros((bq, 1), F32))

        def forth(kb, carry):
            before, dq = carry
            rows = pl.ds(pl.multiple_of(kb * bk, bk), bk)
            dl = dl_keep[kb]
            dz = (dl - sig_keep[kb] * (before + _split_dot(dl, prefix))) * scale
            dk_ref[0, rows, :] += _dot(dz, qv, 0, 0)
            return before + jnp.sum(dl, axis=1, keepdims=True), dq + _dot(dz, k_ref[0, rows, :])

        _, dq = lax.fori_loop(0, nkb, forth, (jnp.zeros((bq, 1), F32), jnp.zeros((bq, d), F32)))
        dq_ref[0] = dq

    qspec = pl.BlockSpec((1, bq, d), lambda hh, i: (hh, i, 0))
    kvspec = pl.BlockSpec((1, t, d), lambda hh, i: (hh, 0, 0))
    shape = jax.ShapeDtypeStruct((h, t, d), F32)
    return pl.pallas_call(
        body, name=name, grid=(h, t // bq), in_specs=[qspec, kvspec, kvspec, qspec],
        out_specs=[qspec, kvspec, kvspec], out_shape=[shape] * 3,
        scratch_shapes=[pltpu.VMEM((t // bk, bq, bk), F32), pltpu.VMEM((t // bk, bq, bk), F32)],
        compiler_params=_cparams("parallel", "arbitrary"),
    )(q, k, v, do)


def _gate_specs(tm):
    return [pl.BlockSpec((tm, D_MODEL), lambda i, b=b: (i, C_GATES // D_MODEL + b)) for b in range(3)]


def _merge_fwd(p, proj, *, name):
    t = proj.shape[0]
    tm = _tile(t, ROW_TILE)

    def body(p0, p1, p2, g0, g1, g2, o_ref):
        acc = jnp.zeros((tm, D_MODEL), F32)
        for pr, gr in ((p0, g0), (p1, g1), (p2, g2)):
            acc = acc + _sigmoid(gr[...].astype(F32)) * pr[...].astype(F32)
        o_ref[...] = acc.astype(o_ref.dtype)

    row = pl.BlockSpec((tm, D_MODEL), lambda i: (i, 0))
    return pl.pallas_call(
        body, name=name, grid=(t // tm,), in_specs=[row] * 3 + _gate_specs(tm), out_specs=row,
        out_shape=jax.ShapeDtypeStruct((t, D_MODEL), BF16), compiler_params=_cparams("parallel"),
    )(*p, proj, proj, proj)


def _merge_bwd(p, proj, dmerged, *, name):
    t = proj.shape[0]
    tm = _tile(t, ROW_TILE)

    def body(p0, p1, p2, g0, g1, g2, dm_ref, dp0, dp1, dp2, dg_ref):
        dm = dm_ref[...].astype(F32)
        for b, (pr, gr, dpr) in enumerate(((p0, g0, dp0), (p1, g1, dp1), (p2, g2, dp2))):
            s = _sigmoid(gr[...].astype(F32))
            dpr[...] = (dm * s).astype(dpr.dtype)
            dg_ref[:, b * D_MODEL:(b + 1) * D_MODEL] = (dm * pr[...].astype(F32) * s * (1.0 - s)).astype(dg_ref.dtype)

    row = pl.BlockSpec((tm, D_MODEL), lambda i: (i, 0))
    res = pl.pallas_call(
        body, name=name, grid=(t // tm,), in_specs=[row] * 3 + _gate_specs(tm) + [row],
        out_specs=[row] * 3 + [pl.BlockSpec((tm, 3 * D_MODEL), lambda i: (i, 0))],
        out_shape=[jax.ShapeDtypeStruct((t, D_MODEL), BF16)] * 3 + [jax.ShapeDtypeStruct((t, 3 * D_MODEL), BF16)],
        compiler_params=_cparams("parallel"),
    )(*p, proj, proj, proj, dmerged)
    return res[:3], res[3]


def _loss_head(y, target, *, name):
    t, d = y.shape
    tm = _tile(t, ROW_TILE)

    def body(y_ref, t_ref, dy_ref, l_ref):
        err = y_ref[...] - t_ref[...]
        dy_ref[...] = err * (1.0 / d)
        part = jnp.sum(err * err, axis=0, keepdims=True) * (0.5 / d)

        @pl.when(pl.program_id(0) == 0)
        def _():
            l_ref[...] = part

        @pl.when(pl.program_id(0) > 0)
        def _():
            l_ref[...] += part

    row = pl.BlockSpec((tm, d), lambda i: (i, 0))
    vec = pl.BlockSpec((1, d), lambda i: (0, 0))
    return pl.pallas_call(
        body, name=name, grid=(t // tm,), in_specs=[row, row], out_specs=[row, vec],
        out_shape=[jax.ShapeDtypeStruct((t, d), F32), jax.ShapeDtypeStruct((1, d), F32)],
        compiler_params=_cparams("arbitrary"),
    )(y, target)


def _adamw(w, g, m, v, *, name):
    r, c = w.shape
    tr = r if r * c * 4 <= 2 ** 21 else max(8, (2 ** 21 // (c * 4)) // 8 * 8)
    while r % tr:
        tr -= 8
    c1 = 1.0 - ADAM_B1 ** ADAM_STEP
    c2 = 1.0 - ADAM_B2 ** ADAM_STEP

    def body(w_ref, g_ref, m_ref, v_ref, d_ref, nm_ref, nv_ref):
        gv = g_ref[...]
        nm = ADAM_B1 * m_ref[...] + (1.0 - ADAM_B1) * gv
        nv = ADAM_B2 * v_ref[...] + (1.0 - ADAM_B2) * (gv * gv)
        nm_ref[...] = nm
        nv_ref[...] = nv
        d_ref[...] = -ADAM_LR * ((nm / c1) / (jnp.sqrt(nv / c2) + ADAM_EPS) + ADAM_WD * w_ref[...])

    spec = pl.BlockSpec((tr, c), lambda i: (i, 0))
    return pl.pallas_call(
        body, name=name, grid=(r // tr,), in_specs=[spec] * 4, out_specs=[spec] * 3,
        out_shape=[jax.ShapeDtypeStruct((r, c), F32)] * 3, compiler_params=_cparams("parallel"),
    )(w, g, m, v)


def _all_gather(xs, *, name):
    n = len(xs)

    def body(*refs):
        x_refs, out_refs = refs[:n], refs[n:2 * n]
        send_sems, recv_sems, local_sems = refs[2 * n:]
        mx, my, mc = lax.axis_index("x"), lax.axis_index("y"), lax.axis_index("c")
        me, sibling = (mx, my, mc), (mx, my, 1 - mc)
        chips = [(1 - mx, my), (mx, 1 - my), (1 - mx, 1 - my)]

        def slot(a, px, py, pc):
            return out_refs[a].at[4 * px + 2 * py + pc]

        def copy(a, k, block, to, src=None):
            return pltpu.make_async_remote_copy(
                src_ref=slot(a, *block) if src is None else src, dst_ref=slot(a, *block),
                send_sem=send_sems.at[a, k], recv_sem=recv_sems.at[a, k], device_id=to, device_id_type=MESH_ID)

        mine = [pltpu.make_async_copy(x_refs[a], slot(a, *me), local_sems.at[a]) for a in range(n)]
        for cp in mine:
            cp.start()
        first = [copy(a, 1 + j, me, (*chip, mc), src=x_refs[a]) for j, chip in enumerate(chips) for a in range(n)]
        first += [copy(a, 0, me, sibling, src=x_refs[a]) for a in range(n)]
        for cp in first:
            cp.start()
        passed = []
        for j, chip in enumerate(chips):
            for a in range(n):
                copy(a, 1 + j, (*chip, mc), me).wait_recv()
                passed.append(copy(a, 4 + j, (*chip, mc), sibling))
                passed[-1].start()
        for a in range(n):
            copy(a, 0, sibling, me).wait_recv()
        for j, chip in enumerate(chips):
            for a in range(n):
                copy(a, 4 + j, (*chip, 1 - mc), me).wait_recv()
        for cp in first + passed:
            cp.wait_send()
        for cp in mine:
            cp.wait()

    anyspace = pl.BlockSpec(memory_space=pl.ANY)
    return pl.pallas_call(
        body, name=name, in_specs=[anyspace] * n, out_specs=[anyspace] * n,
        out_shape=[jax.ShapeDtypeStruct((N_DEV,) + x.shape, x.dtype) for x in xs],
        scratch_shapes=[pltpu.SemaphoreType.DMA((n, 7)), pltpu.SemaphoreType.DMA((n, 7)), pltpu.SemaphoreType.DMA((n,))],
    )(*xs)


def _exchange_sibling(gs, *, name):
    n = len(gs)

    def body(*refs):
        g_refs, out_refs = refs[:n], refs[n:2 * n]
        send_sems, recv_sems = refs[2 * n:]
        mx, my, mc = lax.axis_index("x"), lax.axis_index("y"), lax.axis_index("c")
        sibling = (mx, my, 1 - mc)
        copies = []
        for a in range(n):
            for px in range(2):
                for py in range(2):
                    kk = 2 * px + py
                    copies.append(pltpu.make_async_remote_copy(
                        src_ref=g_refs[a].at[4 * px + 2 * py + (1 - mc)], dst_ref=out_refs[a].at[kk],
                        send_sem=send_sems.at[a, kk], recv_sem=recv_sems.at[a, kk], device_id=sibling,
                        device_id_type=MESH_ID))
        for cp in copies:
            cp.start()
        for cp in copies:
            cp.wait_recv()
        for cp in copies:
            cp.wait_send()

    anyspace = pl.BlockSpec(memory_space=pl.ANY)
    return pl.pallas_call(
        body, name=name, in_specs=[anyspace] * n, out_specs=[anyspace] * n,
        out_shape=[jax.ShapeDtypeStruct((4,) + g.shape[1:], g.dtype) for g in gs],
        scratch_shapes=[pltpu.SemaphoreType.DMA((n, 4)), pltpu.SemaphoreType.DMA((n, 4))],
    )(*gs)


def _pair_sum(g, got, *, name):
    _, r, c = g.shape
    tr = _tile(r, ROW_TILE)

    def body(core_ref, a_ref, b_ref, o_ref):
        del core_ref
        o_ref[...] = (a_ref[...].astype(F32) + b_ref[...].astype(F32)).astype(o_ref.dtype)

    grid_spec = pltpu.PrefetchScalarGridSpec(
        num_scalar_prefetch=1, grid=(4, r // tr),
        in_specs=[pl.BlockSpec((1, tr, c), lambda kk, i, core: (2 * kk + core[0], i, 0)),
                  pl.BlockSpec((1, tr, c), lambda kk, i, core: (kk, i, 0))],
        out_specs=pl.BlockSpec((1, tr, c), lambda kk, i, core: (kk, i, 0)))
    return pl.pallas_call(
        body, name=name, grid_spec=grid_spec, out_shape=jax.ShapeDtypeStruct((4, r, c), g.dtype),
        compiler_params=_cparams("parallel", "parallel"),
    )(lax.axis_index("c").astype(jnp.int32).reshape(1), g, got)


def _exchange_chips(parts, *, name):
    n = len(parts)

    def body(*refs):
        p_refs, out_refs = refs[:n], refs[n:2 * n]
        send_sems, recv_sems = refs[2 * n:]
        mx, my, mc = lax.axis_index("x"), lax.axis_index("y"), lax.axis_index("c")
        chips = [(1 - mx, my), (mx, 1 - my), (1 - mx, 1 - my)]
        copies = [pltpu.make_async_remote_copy(
            src_ref=p_refs[a].at[2 * px + py], dst_ref=out_refs[a].at[j], send_sem=send_sems.at[a, j],
            recv_sem=recv_sems.at[a, j], device_id=(px, py, mc), device_id_type=MESH_ID)
            for j, (px, py) in enumerate(chips) for a in range(n)]
        for cp in copies:
            cp.start()
        for cp in copies:
            cp.wait_recv()
        for cp in copies:
            cp.wait_send()

    anyspace = pl.BlockSpec(memory_space=pl.ANY)
    return pl.pallas_call(
        body, name=name, in_specs=[anyspace] * n, out_specs=[anyspace] * n,
        out_shape=[jax.ShapeDtypeStruct((3,) + p.shape[1:], p.dtype) for p in parts],
        scratch_shapes=[pltpu.SemaphoreType.DMA((n, 3)), pltpu.SemaphoreType.DMA((n, 3))],
    )(*parts)


def _final_sum(part, got, *, name):
    _, r, c = part.shape
    tr = _tile(r, ROW_TILE)

    def body(chip_ref, a_ref, b_ref, o_ref):
        del chip_ref
        acc = a_ref[0].astype(F32)
        for j in range(3):
            acc = acc + b_ref[j].astype(F32)
        o_ref[...] = acc

    grid_spec = pltpu.PrefetchScalarGridSpec(
        num_scalar_prefetch=1, grid=(r // tr,),
        in_specs=[pl.BlockSpec((1, tr, c), lambda i, chip: (chip[0], i, 0)),
                  pl.BlockSpec((3, tr, c), lambda i, chip: (0, i, 0))],
        out_specs=pl.BlockSpec((tr, c), lambda i, chip: (i, 0)))
    chip = (2 * lax.axis_index("x") + lax.axis_index("y")).astype(jnp.int32).reshape(1)
    return pl.pallas_call(
        body, name=name, grid_spec=grid_spec, out_shape=jax.ShapeDtypeStruct((r, c), F32),
        compiler_params=_cparams("parallel"),
    )(chip, part, got)


def _sum_devices(x, *, name):
    _, r, c = x.shape

    def body(x_ref, o_ref):
        acc = x_ref[0]
        for j in range(1, N_DEV):
            acc = acc + x_ref[j]
        o_ref[...] = acc

    return pl.pallas_call(body, name=name, out_shape=jax.ShapeDtypeStruct((r, c), F32),
                          compiler_params=_cparams())(x)


BIG = ("w_in", "w_branch", "w_out", "w_ff1", "w_ff2")
BIG_AXIS = {"w_in": 2, "w_branch": 3, "w_out": 1, "w_ff1": 2, "w_ff2": 1}


def _to_global(blocks, axis):
    moved = jnp.moveaxis(blocks, 0, axis)
    shp = moved.shape
    return moved.reshape(shp[:axis] + (shp[axis] * shp[axis + 1],) + shp[axis + 2:])


def _to_blocks(full, axis):
    shp = full.shape
    split = full.reshape(shp[:axis] + (N_DEV, shp[axis] // N_DEV) + shp[axis + 1:])
    return jnp.moveaxis(split, axis, 0)


def _rows(blocks):
    return blocks.reshape(blocks.shape[0], -1, blocks.shape[-1])


def _vec_rows(n):
    return -(-n // 128 // 8) * 8


def _pack_vec(parts):
    flat = jnp.concatenate([p.reshape(-1).astype(F32) for p in parts])
    rows = _vec_rows(flat.shape[0])
    return jnp.pad(flat, (0, rows * 128 - flat.shape[0])).reshape(rows, 128)


def _unpack_vec(flat, shapes):
    lead = flat.shape[:-2]
    flat = flat.reshape(lead + (-1,))
    out, off = [], 0
    for s in shapes:
        n = 1
        for dim in s:
            n *= dim
        out.append(flat[..., off:off + n].reshape(lead + tuple(s)))
        off += n
    return out


def _heads_major(cols, heads, dim):
    t = cols.shape[0]
    return cols.reshape(t, heads, dim).transpose(1, 0, 2)


def _heads_minor(x):
    h, t, dim = x.shape
    return x.transpose(1, 0, 2).reshape(t, h * dim)


def _sb_qkv(proj):
    return [_heads_major(proj[:, C_SBQKV + i * SB_W:C_SBQKV + (i + 1) * SB_W], SB_HEADS, SB_DIM) for i in range(3)]


def _relu2_epilogue(r):
    a = jnp.maximum(r, 0.0)
    return r, a * a


def _relu2_bwd_epilogue(r, a):
    return (r * 2.0 * jnp.maximum(a.astype(F32), 0.0),)


def _layer_fwd(x, p):
    h = _norm_fwd(x, p["norm_mix_pre"], out_dtype=BF16, name="norm_pre_fwd")
    proj = _mm(h, p["w_main"], name="mm_in")
    ab = _mm(h, p["w_ab"], out_dtypes=(F32,), name="mm_ab")
    qkv = _gdn_pre_fwd(proj, p["conv_qkv_w"], name="gdn_pre_fwd")
    a_log, dt_bias = p["gdn_a_log"].reshape(1, GDN_HEADS), p["gdn_dt_bias"].reshape(1, GDN_HEADS)
    u, w, qd, kd, aqk, gl = _gdn_local_fwd(qkv, ab, a_log, dt_bias, name="gdn_local_fwd")
    o_gdn, states = _gdn_scan_fwd(u, w, qd, kd, aqk, gl, name="gdn_scan_fwd")
    y_a = _gdn_post_fwd(o_gdn, proj, p["gdn_norm_w"], name="gdn_post_fwd")
    sq, sk, sv = _sb_qkv(proj)
    o_sb = _sb_fwd(sq, sk, sv, name="sb_fwd")
    y_b = _heads_minor(o_sb).astype(BF16)
    y_c = _sc_fwd(proj, p["conv_sc_w"], name="sc_fwd")
    ys = (y_a, y_b, y_c)
    ps = tuple(_mm(ys[b], p["w_branch"][b], name="mm_branch") for b in range(3))
    merged = _merge_fwd(ps, proj, name="merge_fwd")
    mo = _mm(merged, p["w_out"], out_dtypes=(F32,), name="mm_out")
    x1 = _norm_fwd(mo, p["norm_mix_post"], x, out_dtype=F32, name="norm_post_fwd")
    h2 = _norm_fwd(x1, p["norm_ffn_pre"], out_dtype=BF16, name="norm_pre_fwd")
    a1, r1 = _mm(h2, p["w_ff1"], out_dtypes=(BF16, BF16), epi=_relu2_epilogue, name="mm_ff1")
    f = _mm(r1, p["w_ff2"], out_dtypes=(F32,), name="mm_ff2")
    x2 = _norm_fwd(f, p["norm_ffn_post"], x1, out_dtype=F32, name="norm_post_fwd")
    saved = dict(x=x, h=h, proj=proj, ab=ab, qkv=qkv, u=u, w=w, qd=qd, kd=kd, aqk=aqk, gl=gl, o_gdn=o_gdn,
                 states=states, sq=sq, sk=sk, sv=sv, ys=ys, ps=ps, merged=merged, mo=mo, x1=x1, h2=h2,
                 a1=a1, r1=r1, f=f)
    return x2, saved


def _layer_bwd(dx2, p, s):
    g = {}
    df, g["norm_ffn_post"] = _norm_bwd(s["f"], p["norm_ffn_post"], dx2, out_dtype=BF16, name="norm_bwd_b")
    da1 = _mm(df, p["w_ff2"], tb=True, epi=_relu2_bwd_epilogue, extras=(s["a1"],), name="mm_ff2_dx")
    g["w_ff2"] = _mm(s["r1"], df, ta=True, name="mm_ff2_dw")
    g["w_ff1"] = _mm(s["h2"], da1, ta=True, name="mm_ff1_dw")
    dh2 = _mm(da1, p["w_ff1"], tb=True, out_dtypes=(F32,), name="mm_ff1_dx")
    dx1, g["norm_ffn_pre"] = _norm_bwd(s["x1"], p["norm_ffn_pre"], dh2, dx2, out_dtype=F32, name="norm_bwd_f")
    dmo, g["norm_mix_post"] = _norm_bwd(s["mo"], p["norm_mix_post"], dx1, out_dtype=BF16, name="norm_bwd_b")
    dmerged = _mm(dmo, p["w_out"], tb=True, name="mm_out_dx")
    g["w_out"] = _mm(s["merged"], dmo, ta=True, name="mm_out_dw")
    dps, dgates = _merge_bwd(s["ps"], s["proj"], dmerged, name="merge_bwd")
    dys = [_mm(dps[b], p["w_branch"][b], tb=True, name="mm_branch_dx") for b in range(3)]
    g["w_branch"] = jnp.stack([_mm(s["ys"][b], dps[b], ta=True, name="mm_branch_dw") for b in range(3)])
    dscx, dscb, dscc, g["conv_sc_w"] = _sc_bwd(s["proj"], p["conv_sc_w"], dys[2], name="sc_bwd")
    do_sb = _heads_major(dys[1], SB_HEADS, SB_DIM).astype(F32)
    dsq, dsk, dsv = _sb_bwd(s["sq"], s["sk"], s["sv"], do_sb, name="sb_bwd")
    dsb = jnp.concatenate([_heads_minor(dsq), _heads_minor(dsk), _heads_minor(dsv)], axis=1).astype(BF16)
    a_log, dt_bias = p["gdn_a_log"].reshape(1, GDN_HEADS), p["gdn_dt_bias"].reshape(1, GDN_HEADS)
    do_gdn, dggate, g["gdn_norm_w"] = _gdn_post_bwd(s["o_gdn"], s["proj"], p["gdn_norm_w"], dys[0], name="gdn_post_bwd")
    du, dw, dqd, dkd, daqk, dgl = _gdn_scan_bwd(s["u"], s["w"], s["qd"], s["kd"], s["aqk"], s["gl"], s["states"],
                                                do_gdn, name="gdn_scan_bwd")
    dq, dk, dv, dab_h, dsc = _gdn_local_bwd(s["qkv"], s["ab"], a_log, dt_bias, du, dw, dqd, dkd, daqk, dgl,
                                            name="gdn_local_bwd")
    dsc = jnp.sum(dsc, axis=(1, 2))
    g["gdn_a_log"], g["gdn_dt_bias"] = dsc[:, 0], dsc[:, 1]
    dqkv = jnp.concatenate([dq, dk, dv], axis=0)
    dgqkv, g["conv_qkv_w"] = _gdn_pre_bwd(s["proj"], p["conv_qkv_w"], dqkv, name="gdn_pre_bwd")
    dab = jnp.sum(dab_h, axis=0).astype(BF16)
    dproj = jnp.concatenate([dgqkv, dggate, dsb, dscx, dscb, dscc, dgates], axis=1)
    g["w_main"] = _mm(s["h"], dproj, ta=True, name="mm_in_dw")
    g["w_ab"] = _mm(s["h"], dab, ta=True, out_dtypes=(F32,), name="mm_ab_dw")
    dh = _mm(dproj, p["w_main"], tb=True, out_dtypes=(F32,), name="mm_in_dx")
    dh_ab = _mm(dab, p["w_ab"], tb=True, out_dtypes=(F32,), name="mm_ab_dx")
    dx, g["norm_mix_pre"] = _norm_bwd(s["x"], p["norm_mix_pre"], dh + dh_ab, dx1, out_dtype=F32, name="norm_bwd_f")
    return dx, g


NORMS = ("norm_mix_pre", "norm_mix_post", "norm_ffn_pre", "norm_ffn_post")
SMALL = NORMS + ("gdn_a_log", "gdn_dt_bias", "gdn_norm_w")
CONVS = ("conv_qkv_w", "conv_sc_w")
AB_LO = 2048


def _split_w_in(w_in):
    main = jnp.concatenate([w_in[..., :AB_LO], w_in[..., AB_LO + 2 * GDN_HEADS:]], axis=-1)
    ab = w_in[..., AB_LO:AB_LO + 2 * GDN_HEADS]
    pad = [(0, 0)] * (ab.ndim - 1) + [(0, AB_W - 2 * GDN_HEADS)]
    return main, jnp.pad(ab, pad)


def _join_w_in(main, ab):
    return jnp.concatenate([main[..., :AB_LO], ab[..., :2 * GDN_HEADS].astype(main.dtype), main[..., AB_LO:]], axis=-1)


def kernel(x, norm_mix_pre, w_in, conv_qkv_w, gdn_a_log, gdn_dt_bias, gdn_norm_w, conv_sc_w, w_branch, w_out, norm_mix_post, norm_ffn_pre, w_ff1, w_ff2, norm_ffn_post, loss_target, m_norm_mix_pre, m_w_in, m_conv_qkv_w, m_gdn_a_log, m_gdn_dt_bias, m_gdn_norm_w, m_conv_sc_w, m_w_branch, m_w_out, m_norm_mix_post, m_norm_ffn_pre, m_w_ff1, m_w_ff2, m_norm_ffn_post, v_norm_mix_pre, v_w_in, v_conv_qkv_w, v_gdn_a_log, v_gdn_dt_bias, v_gdn_norm_w, v_conv_sc_w, v_w_branch, v_w_out, v_norm_mix_post, v_norm_ffn_pre, v_w_ff1, v_w_ff2, v_norm_ffn_post):
    names = ("norm_mix_pre", "w_in", "conv_qkv_w", "gdn_a_log", "gdn_dt_bias", "gdn_norm_w", "conv_sc_w", "w_branch",
             "w_out", "norm_mix_post", "norm_ffn_pre", "w_ff1", "w_ff2", "norm_ffn_post")
    w = dict(zip(names, (norm_mix_pre, w_in, conv_qkv_w, gdn_a_log, gdn_dt_bias, gdn_norm_w, conv_sc_w, w_branch,
                         w_out, norm_mix_post, norm_ffn_pre, w_ff1, w_ff2, norm_ffn_post)))
    m = dict(zip(names, (m_norm_mix_pre, m_w_in, m_conv_qkv_w, m_gdn_a_log, m_gdn_dt_bias, m_gdn_norm_w, m_conv_sc_w,
                         m_w_branch, m_w_out, m_norm_mix_post, m_norm_ffn_pre, m_w_ff1, m_w_ff2, m_norm_ffn_post)))
    v = dict(zip(names, (v_norm_mix_pre, v_w_in, v_conv_qkv_w, v_gdn_a_log, v_gdn_dt_bias, v_gdn_norm_w, v_conv_sc_w,
                         v_w_branch, v_w_out, v_norm_mix_post, v_norm_ffn_pre, v_w_ff1, v_w_ff2, v_norm_ffn_post)))
    me = 4 * lax.axis_index("x") + 2 * lax.axis_index("y") + lax.axis_index("c")

    gathered = _all_gather([w[k].astype(BF16) for k in BIG], name="gather_weights")
    full = {k: _to_global(blk, BIG_AXIS[k]) for k, blk in zip(BIG, gathered)}
    conv_shapes = [w[k].shape for k in CONVS]
    conv_all, = _all_gather([_pack_vec([w[k] for k in CONVS])], name="gather_small")
    for k, blk in zip(CONVS, _unpack_vec(conv_all, conv_shapes)):
        full[k] = _to_global(blk, 2)
    full["w_main"], full["w_ab"] = _split_w_in(full.pop("w_in"))

    xs = x[0]
    layers = []
    for l in range(DEPTH):
        p = {k: full[k][l] for k in full}
        p.update({k: w[k][l] for k in SMALL})
        layers.append(p)

    saved = []
    for l in range(DEPTH):
        xs, s = _layer_fwd(xs, layers[l])
        saved.append(s)
    dy, loss_lanes = _loss_head(xs, loss_target[0], name="loss_head")

    grads = []
    for l in reversed(range(DEPTH)):
        dy, g = _layer_bwd(dy, layers[l], saved[l])
        grads.append(g)
    grads = grads[::-1]
    stack = {k: jnp.stack([g[k] for g in grads]) for k in grads[0]}
    stack["w_in"] = _join_w_in(stack.pop("w_main"), stack.pop("w_ab"))

    blocks = [_rows(_to_blocks(stack[k], BIG_AXIS[k])) for k in BIG]
    got = _exchange_sibling(blocks, name="rs_sibling")
    parts = [_pair_sum(b, g, name="rs_pair_sum") for b, g in zip(blocks, got)]
    got2 = _exchange_chips(parts, name="rs_chips")
    gsum = {k: _final_sum(p, g, name="rs_final_sum").reshape(w[k].shape) for k, p, g in zip(BIG, parts, got2)}

    small_parts = [stack[k] for k in SMALL + CONVS] + [jnp.sum(loss_lanes).reshape(1)]
    small_shapes = [stack[k].shape for k in SMALL + CONVS] + [(1,)]
    summed = _sum_devices(_all_gather([_pack_vec(small_parts)], name="gather_small_grads")[0], name="sum_small")
    small = _unpack_vec(summed, small_shapes)
    loss = small[-1][0]
    for k, val in zip(SMALL + CONVS, small[:-1]):
        gsum[k] = val
    for k in CONVS:
        per = gsum[k].shape[2] // N_DEV
        gsum[k] = lax.dynamic_slice_in_dim(gsum[k], me * per, per, axis=2)

    delta, new_m, new_v = {}, {}, {}
    for k in names:
        shp = w[k].shape
        two_d = (-1, shp[-1]) if len(shp) > 1 else (1, -1)
        d_, m_, v_ = _adamw(w[k].reshape(two_d), gsum[k].reshape(two_d), m[k].reshape(two_d), v[k].reshape(two_d),
                            name="adamw")
        delta[k], new_m[k], new_v[k] = d_.reshape(shp), m_.reshape(shp), v_.reshape(shp)

    return (loss, dy[None], *[gsum[k].reshape(w[k].shape) for k in names], *[delta[k] for k in names], *[new_m[k] for k in names],
            *[new_v[k] for k in names])
```

```python
import functools

import jax
import jax.numpy as jnp
from jax import lax
from jax.experimental import pallas as pl
from jax.experimental.pallas import tpu as pltpu

F32, BF16 = jnp.float32, jnp.bfloat16
MESH_ID = pl.DeviceIdType.MESH

N_DEV = 8
DEPTH = 4
D_MODEL = 1024
D_FF = 4096
EPS = 1e-6
GDN_HEADS, GDN_DIM, GDN_CONV = 4, 128, 4
GDN_W = GDN_HEADS * GDN_DIM
CHUNK = 64
SB_HEADS, SB_DIM = 8, 64
SB_W = SB_HEADS * SB_DIM
SB_QBLOCK, SB_KBLOCK = 512, 256
SC_W, SC_CONV = 512, 3
IN_W = 8200
C_GQKV, C_GGATE, C_SBQKV, C_SCX, C_SCB, C_SCC, C_GATES, MAIN_W = 0, 1536, 2048, 3584, 4096, 4608, 5120, 8192
AB_W = 128

ADAM_LR, ADAM_B1, ADAM_B2, ADAM_EPS, ADAM_WD, ADAM_STEP = 0.001, 0.9, 0.999, 1e-08, 0.01, 10

VMEM_LIMIT = 48 * 2 ** 20


def _cparams(*sem):
    return pltpu.CompilerParams(dimension_semantics=sem or None, vmem_limit_bytes=VMEM_LIMIT)


def _tile(n, pref):
    if n <= pref:
        return n
    t = pref
    while n % t:
        t -= 128
    assert t > 0
    return t


def _dot(a, b, ca=1, cb=0):
    return lax.dot_general(a.astype(BF16), b.astype(BF16), (((ca,), (cb,)), ((), ())), preferred_element_type=F32)


def _split2(x):
    hi = x.astype(BF16)
    return hi, (x - hi.astype(F32)).astype(BF16)


def _dot3(a, b, ca=1, cb=0):
    a1, a2 = _split2(a)
    b1, b2 = _split2(b)
    return _dot(a1, b1, ca, cb) + (_dot(a1, b2, ca, cb) + _dot(a2, b1, ca, cb))


def _dot_exact(a, b, ca=1, cb=0, ones="a"):
    x = b if ones == "a" else a
    m = (a if ones == "a" else b).astype(BF16)
    hi, rest = x.astype(BF16), None
    rest = x - hi.astype(F32)
    mid = rest.astype(BF16)
    lo = (rest - mid.astype(F32)).astype(BF16)
    parts = [_dot(m, p, ca, cb) if ones == "a" else _dot(p, m, ca, cb) for p in (hi, mid, lo)]
    return parts[0] + (parts[1] + parts[2])


def _sigmoid(z):
    e = jnp.exp(-jnp.abs(z))
    return jnp.where(z >= 0, 1.0, e) / (1.0 + e)


def _softplus(z):
    return jnp.maximum(z, 0.0) + jnp.log(1.0 + jnp.exp(-jnp.abs(z)))


def _mm(a, b, *, name, ta=False, tb=False, out_dtypes=(BF16,), epi=None, extras=()):
    assert a.dtype == BF16 and b.dtype == BF16
    m, k = (a.shape[1], a.shape[0]) if ta else a.shape
    n = b.shape[0] if tb else b.shape[1]
    assert (b.shape[1] if tb else b.shape[0]) == k
    tm, tn, tk = _tile(m, 1024), _tile(n, 1024), _tile(k, 1024)
    nk = k // tk
    ca, cb = (0 if ta else 1), (1 if tb else 0)
    n_ex, n_out = len(extras), len(out_dtypes)

    def body(*refs):
        a_ref, b_ref = refs[0], refs[1]
        ex = refs[2:2 + n_ex]
        outs = refs[2 + n_ex:2 + n_ex + n_out]
        acc = refs[-1]
        kk = pl.program_id(2)
        part = lax.dot_general(a_ref[...], b_ref[...], (((ca,), (cb,)), ((), ())), preferred_element_type=F32)

        def finish(r):
            vals = (r,) if epi is None else epi(r, *[e[...] for e in ex])
            for o, v in zip(outs, vals):
                o[...] = v.astype(o.dtype)

        if nk == 1:
            finish(part)
        else:
            @pl.when(kk == 0)
            def _():
                acc[...] = part

            @pl.when(kk > 0)
            def _():
                acc[...] += part

            @pl.when(kk == nk - 1)
            def _():
                finish(acc[...])

    a_spec = pl.BlockSpec((tk, tm), lambda i, j, kk: (kk, i)) if ta else pl.BlockSpec((tm, tk), lambda i, j, kk: (i, kk))
    b_spec = pl.BlockSpec((tn, tk), lambda i, j, kk: (j, kk)) if tb else pl.BlockSpec((tk, tn), lambda i, j, kk: (kk, j))
    io_spec = pl.BlockSpec((tm, tn), lambda i, j, kk: (i, j))
    res = pl.pallas_call(
        body, name=name, grid=(m // tm, n // tn, nk),
        in_specs=[a_spec, b_spec] + [io_spec] * n_ex,
        out_specs=[io_spec] * n_out,
        out_shape=[jax.ShapeDtypeStruct((m, n), dt) for dt in out_dtypes],
        scratch_shapes=[pltpu.VMEM((tm, tn) if nk > 1 else (8, 128), F32)],
        compiler_params=_cparams("parallel", "parallel", "arbitrary"),
    )(a, b, *extras)
    return res[0] if n_out == 1 else res


ROW_TILE = 512


def _norm_fwd(y, w, res=None, *, out_dtype, name):
    t, d = y.shape
    tm = _tile(t, ROW_TILE)
    has_res = res is not None

    def body(*refs):
        y_ref, w_ref = refs[0], refs[1]
        o_ref = refs[-1]
        yv = y_ref[...]
        r = lax.rsqrt(jnp.mean(yv * yv, axis=-1, keepdims=True) + EPS)
        out = yv * r * w_ref[...]
        if has_res:
            out = out + refs[2][...]
        o_ref[...] = out.astype(o_ref.dtype)

    row = pl.BlockSpec((tm, d), lambda i: (i, 0))
    vec = pl.BlockSpec((1, d), lambda i: (0, 0))
    args = (y, w.reshape(1, d)) + ((res,) if has_res else ())
    return pl.pallas_call(
        body, name=name, grid=(t // tm,), in_specs=[row, vec] + [row] * has_res, out_specs=row,
        out_shape=jax.ShapeDtypeStruct((t, d), out_dtype), compiler_params=_cparams("parallel"),
    )(*args)


def _norm_bwd(y, w, dout, add=None, *, out_dtype, name):
    t, d = y.shape
    tm = _tile(t, ROW_TILE)
    has_add = add is not None

    def body(*refs):
        y_ref, w_ref, do_ref = refs[0], refs[1], refs[2]
        dy_ref, dw_ref = refs[-2], refs[-1]
        yv = y_ref[...]
        r = lax.rsqrt(jnp.mean(yv * yv, axis=-1, keepdims=True) + EPS)
        yh = yv * r
        dov = do_ref[...].astype(F32)
        gw = dov * w_ref[...]
        dy = r * (gw - yh * jnp.mean(gw * yh, axis=-1, keepdims=True))
        if has_add:
            dy = dy + refs[3][...]
        dy_ref[...] = dy.astype(dy_ref.dtype)
        part = jnp.sum(dov * yh, axis=0, keepdims=True)

        @pl.when(pl.program_id(0) == 0)
        def _():
            dw_ref[...] = part

        @pl.when(pl.program_id(0) > 0)
        def _():
            dw_ref[...] += part

    row = pl.BlockSpec((tm, d), lambda i: (i, 0))
    vec = pl.BlockSpec((1, d), lambda i: (0, 0))
    args = (y, w.reshape(1, d), dout) + ((add,) if has_add else ())
    return pl.pallas_call(
        body, name=name, grid=(t // tm,), in_specs=[row, vec, row] + [row] * has_add, out_specs=[row, vec],
        out_shape=[jax.ShapeDtypeStruct((t, d), out_dtype), jax.ShapeDtypeStruct((1, d), F32)],
        compiler_params=_cparams("arbitrary"),
    )(*args)


def _shift_down(u, s):
    if s == 0:
        return u
    rows = lax.broadcasted_iota(jnp.int32, u.shape, 0)
    return jnp.where(rows >= s, pltpu.roll(u, s, 0), 0.0)


def _shift_up(u, s):
    if s == 0:
        return u
    t = u.shape[0]
    rows = lax.broadcasted_iota(jnp.int32, u.shape, 0)
    return jnp.where(rows < t - s, pltpu.roll(u, t - s, 0), 0.0)


def _conv_fwd(u, w):
    kk = w.shape[0]
    out = u * w[kk - 1:kk, :]
    for i in range(kk - 1):
        out = out + _shift_down(u, kk - 1 - i) * w[i:i + 1, :]
    return out


def _conv_bwd(u, w, dc):
    kk = w.shape[0]
    du = dc * w[kk - 1:kk, :]
    dws = []
    for i in range(kk):
        s = kk - 1 - i
        if s:
            du = du + _shift_up(dc, s) * w[i:i + 1, :]
        dws.append(jnp.sum(dc * _shift_down(u, s), axis=0, keepdims=True))
    return du, dws


def _gdn_pre_math(x, w, slab):
    c = _conv_fwd(x, w)
    sig = _sigmoid(c)
    s = c * sig
    r = lax.rsqrt(jnp.sum(s * s, axis=-1, keepdims=True) + EPS)
    scale = jnp.where(slab < GDN_HEADS, GDN_DIM ** -0.5, 1.0)
    return c, sig, s, r, scale


def _gdn_pre_fwd(proj, conv_w, *, name):
    t = proj.shape[0]
    nslab = 3 * GDN_HEADS

    def body(x_ref, w_ref, o_ref):
        slab = pl.program_id(0)
        _, _, s, r, scale = _gdn_pre_math(x_ref[...].astype(F32), w_ref[...], slab)
        o_ref[0] = jnp.where(slab < 2 * GDN_HEADS, s * r * scale, s)

    return pl.pallas_call(
        body, name=name, grid=(nslab,),
        in_specs=[pl.BlockSpec((t, GDN_DIM), lambda j: (0, j)), pl.BlockSpec((GDN_CONV, GDN_DIM), lambda j: (0, j))],
        out_specs=pl.BlockSpec((1, t, GDN_DIM), lambda j: (j, 0, 0)),
        out_shape=jax.ShapeDtypeStruct((nslab, t, GDN_DIM), F32), compiler_params=_cparams("parallel"),
    )(proj, conv_w)


def _gdn_pre_bwd(proj, conv_w, dqkv, *, name):
    t = proj.shape[0]
    nslab = 3 * GDN_HEADS

    def body(x_ref, w_ref, d_ref, dx_ref, dw_ref):
        slab = pl.program_id(0)
        x = x_ref[...].astype(F32)
        w = w_ref[...]
        c, sig, s, r, scale = _gdn_pre_math(x, w, slab)
        dout = d_ref[0]
        yn = s * r
        dn = dout * scale
        ds_norm = r * (dn - yn * jnp.sum(dn * yn, axis=-1, keepdims=True))
        ds = jnp.where(slab < 2 * GDN_HEADS, ds_norm, dout)
        dc = ds * (sig + c * sig * (1.0 - sig))
        dx, dws = _conv_bwd(x, w, dc)
        dx_ref[...] = dx.astype(dx_ref.dtype)
        for i, dwi in enumerate(dws):
            dw_ref[i:i + 1, :] = dwi

    return pl.pallas_call(
        body, name=name, grid=(nslab,),
        in_specs=[pl.BlockSpec((t, GDN_DIM), lambda j: (0, j)), pl.BlockSpec((GDN_CONV, GDN_DIM), lambda j: (0, j)),
                  pl.BlockSpec((1, t, GDN_DIM), lambda j: (j, 0, 0))],
        out_specs=[pl.BlockSpec((t, GDN_DIM), lambda j: (0, j)), pl.BlockSpec((GDN_CONV, GDN_DIM), lambda j: (0, j))],
        out_shape=[jax.ShapeDtypeStruct((t, 3 * GDN_W), BF16), jax.ShapeDtypeStruct((GDN_CONV, 3 * GDN_W), F32)],
        compiler_params=_cparams("parallel"),
    )(proj, conv_w, dqkv)


def _sc_specs(t):
    def col(base):
        return pl.BlockSpec((t, 128), lambda j: (0, base // 128 + j))
    return [col(C_SCX), col(C_SCB), col(C_SCC), pl.BlockSpec((SC_CONV, 128), lambda j: (0, j))]


def _sc_fwd(proj, conv_w, *, name):
    t = proj.shape[0]

    def body(x_ref, b_ref, c_ref, w_ref, o_ref):
        u = c_ref[...].astype(F32) * x_ref[...].astype(F32)
        o_ref[...] = (b_ref[...].astype(F32) * _conv_fwd(u, w_ref[...])).astype(o_ref.dtype)

    return pl.pallas_call(
        body, name=name, grid=(SC_W // 128,), in_specs=_sc_specs(t),
        out_specs=pl.BlockSpec((t, 128), lambda j: (0, j)),
        out_shape=jax.ShapeDtypeStruct((t, SC_W), BF16), compiler_params=_cparams("parallel"),
    )(proj, proj, proj, conv_w)


def _sc_bwd(proj, conv_w, dy, *, name):
    t = proj.shape[0]
    nj = SC_W // 128

    def body(x_ref, b_ref, c_ref, w_ref, dy_ref, dx_ref, db_ref, dc_ref, dw_ref):
        x, b, c = x_ref[...].astype(F32), b_ref[...].astype(F32), c_ref[...].astype(F32)
        w = w_ref[...]
        u = c * x
        dyv = dy_ref[...].astype(F32)
        db_ref[...] = (dyv * _conv_fwd(u, w)).astype(db_ref.dtype)
        du, dws = _conv_bwd(u, w, dyv * b)
        dx_ref[...] = (du * c).astype(dx_ref.dtype)
        dc_ref[...] = (du * x).astype(dc_ref.dtype)
        for i, dwi in enumerate(dws):
            dw_ref[i:i + 1, :] = dwi

    return pl.pallas_call(
        body, name=name, grid=(nj,),
        in_specs=_sc_specs(t) + [pl.BlockSpec((t, 128), lambda j: (0, j))],
        out_specs=[pl.BlockSpec((t, 128), lambda j: (0, j))] * 3 + [pl.BlockSpec((SC_CONV, 128), lambda j: (0, j))],
        out_shape=[jax.ShapeDtypeStruct((t, SC_W), BF16)] * 3 + [jax.ShapeDtypeStruct((SC_CONV, SC_W), F32)],
        compiler_params=_cparams("parallel"),
    )(proj, proj, proj, conv_w, dy)


def _tri_inv(a_strict):
    c = a_strict.shape[0]
    ri = lax.broadcasted_iota(jnp.int32, (c, c), 0)
    ci = lax.broadcasted_iota(jnp.int32, (c, c), 1)
    eye = (ri == ci).astype(F32)
    blk = 8
    bm = -jnp.where(ri // blk == ci // blk, a_strict, 0.0)
    inv = eye + bm
    pw = bm
    for _ in range(2):
        pw = _dot3(pw, pw)
        inv = inv + _dot3(inv, pw)
    while blk < c:
        off = jnp.where((ri // (2 * blk) == ci // (2 * blk)) & (ri // blk != ci // blk), a_strict, 0.0)
        inv = inv - _dot3(_dot3(inv, off), inv)
        blk *= 2
    return inv


def _gdn_chunk(q, k, v, ab, head, ea, dtb):
    c = q.shape[0]
    lane = lax.broadcasted_iota(jnp.int32, ab.shape, 1)
    a = jnp.sum(jnp.where(lane == head, ab, 0.0), axis=1, keepdims=True)
    b = jnp.sum(jnp.where(lane == GDN_HEADS + head, ab, 0.0), axis=1, keepdims=True)
    ri = lax.broadcasted_iota(jnp.int32, (c, c), 0)
    ci = lax.broadcasted_iota(jnp.int32, (c, c), 1)
    tri, strict = ri >= ci, ri > ci
    ltri = tri.astype(F32)
    beta = _sigmoid(b)
    sig_a = _sigmoid(a + dtb)
    g = -ea * _softplus(a + dtb)
    g_cc = jnp.broadcast_to(g, (c, c))
    gi = _dot_exact(ltri, g_cc)
    gj = _dot_exact(g_cc, (ri <= ci).astype(F32), 0, 0, ones="b")
    decay = jnp.exp(jnp.where(tri, gi - gj, -1e30))
    gc = _dot_exact(ltri, jnp.broadcast_to(g, (c, GDN_DIM)))
    g_tot = jnp.sum(g, axis=0, keepdims=True)
    egc = jnp.exp(gc)
    ekd = jnp.exp(g_tot - gc)
    kb, vb = k * beta, v * beta
    kbg = kb * egc
    mkk = _dot3(kb, k, 1, 1)
    a_kk = jnp.where(strict, mkk * decay, 0.0)
    tinv = _tri_inv(a_kk)
    u = _dot3(tinv, vb)
    w = _dot3(tinv, kbg)
    mqk = _dot3(q, k, 1, 1)
    a_qk = jnp.where(tri, mqk * decay, 0.0)
    return dict(beta=beta, sig_a=sig_a, g=g, decay=decay, egc=egc, ekd=ekd, g_tot=g_tot, kb=kb, vb=vb, kbg=kbg,
                a_kk=a_kk, tinv=tinv, u=u, w=w, a_qk=a_qk, qd=q * egc, kd=k * ekd, tri=tri, strict=strict)


def _chunks_per_step(n):
    return 4 if n % 4 == 0 else 1


def _gdn_local_specs(t, cps):
    rows = cps * CHUNK

    def slab(base):
        return pl.BlockSpec((1, rows, GDN_DIM), lambda h, n: (base + h, n, 0))
    smem = pl.BlockSpec(memory_space=pltpu.SMEM)
    return [slab(0), slab(GDN_HEADS), slab(2 * GDN_HEADS), pl.BlockSpec((rows, AB_W), lambda h, n: (n, 0)), smem, smem]


def _scalar_row(ref, head):
    return jnp.full((1, 1), ref[0, head], F32)


def _gdn_local_fwd(qkv, ab, a_log, dt_bias, *, name):
    t = qkv.shape[1]
    n = t // CHUNK
    cps = _chunks_per_step(n)
    rows = cps * CHUNK

    def body(q_ref, k_ref, v_ref, ab_ref, al_ref, dt_ref, u_ref, w_ref, qd_ref, kd_ref, aqk_ref, gl_ref):
        head = pl.program_id(0)
        ea = jnp.exp(_scalar_row(al_ref, head))
        dtb = _scalar_row(dt_ref, head)
        for j in range(cps):
            sl = slice(j * CHUNK, (j + 1) * CHUNK)
            r = _gdn_chunk(q_ref[0, sl, :], k_ref[0, sl, :], v_ref[0, sl, :], ab_ref[sl, :], head, ea, dtb)
            u_ref[0, sl, :] = r["u"]
            w_ref[0, sl, :] = r["w"]
            qd_ref[0, sl, :] = r["qd"]
            kd_ref[0, sl, :] = r["kd"]
            aqk_ref[0, sl, :] = r["a_qk"]
            gl_ref[0, j] = jnp.exp(jnp.broadcast_to(r["g_tot"], (1, GDN_DIM)))

    big = pl.BlockSpec((1, rows, GDN_DIM), lambda h, i: (h, i, 0))
    big_shape = jax.ShapeDtypeStruct((GDN_HEADS, t, GDN_DIM), F32)
    return pl.pallas_call(
        body, name=name, grid=(GDN_HEADS, n // cps), in_specs=_gdn_local_specs(t, cps),
        out_specs=[big] * 4 + [pl.BlockSpec((1, rows, CHUNK), lambda h, i: (h, i, 0)),
                               pl.BlockSpec((1, cps, 1, GDN_DIM), lambda h, i: (h, i, 0, 0))],
        out_shape=[big_shape] * 4 + [jax.ShapeDtypeStruct((GDN_HEADS, t, CHUNK), F32),
                                     jax.ShapeDtypeStruct((GDN_HEADS, n, 1, GDN_DIM), F32)],
        compiler_params=_cparams("parallel", "parallel"),
    )(qkv, qkv, qkv, ab, a_log, dt_bias)


def _gdn_scan_fwd(u, w, qd, kd, aqk, gl, *, name):
    h, t, _ = u.shape
    n = t // CHUNK

    def body(u_ref, w_ref, qd_ref, kd_ref, aqk_ref, gl_ref, o_ref, s_ref, state):
        @pl.when(pl.program_id(0) == 0)
        def _():
            state[...] = jnp.zeros_like(state)

        for hh in range(h):
            s = state[hh]
            s_ref[hh, 0] = s
            vn = u_ref[hh] - _dot3(w_ref[hh], s)
            o_ref[hh] = _dot3(qd_ref[hh], s) + _dot3(aqk_ref[hh], vn)
            state[hh] = s * gl_ref[hh, 0] + _dot3(kd_ref[hh], vn, 0, 0)

    big = pl.BlockSpec((h, CHUNK, GDN_DIM), lambda i: (0, i, 0))
    return pl.pallas_call(
        body, name=name, grid=(n,),
        in_specs=[big] * 4 + [pl.BlockSpec((h, CHUNK, CHUNK), lambda i: (0, i, 0)),
                              pl.BlockSpec((h, 1, 1, GDN_DIM), lambda i: (0, i, 0, 0))],
        out_specs=[big, pl.BlockSpec((h, 1, GDN_DIM, GDN_DIM), lambda i: (0, i, 0, 0))],
        out_shape=[jax.ShapeDtypeStruct((h, t, GDN_DIM), F32), jax.ShapeDtypeStruct((h, n, GDN_DIM, GDN_DIM), F32)],
        scratch_shapes=[pltpu.VMEM((h, GDN_DIM, GDN_DIM), F32)],
        compiler_params=_cparams("arbitrary"),
    )(u, w, qd, kd, aqk, gl)


def _gdn_scan_bwd(u, w, qd, kd, aqk, gl, states, do, *, name):
    h, t, _ = u.shape
    n = t // CHUNK

    def body(u_ref, w_ref, qd_ref, kd_ref, aqk_ref, gl_ref, s_ref, do_ref,
             du_ref, dw_ref, dqd_ref, dkd_ref, daqk_ref, dgl_ref, dstate):
        @pl.when(pl.program_id(0) == 0)
        def _():
            dstate[...] = jnp.zeros_like(dstate)

        ri = lax.broadcasted_iota(jnp.int32, (CHUNK, CHUNK), 0)
        ci = lax.broadcasted_iota(jnp.int32, (CHUNK, CHUNK), 1)
        for hh in range(h):
            s, ds_next, dov, wv = s_ref[hh, 0], dstate[hh], do_ref[hh], w_ref[hh]
            vn = u_ref[hh] - _dot3(wv, s)
            dvn = _dot3(aqk_ref[hh], dov, 0, 0) + _dot3(kd_ref[hh], ds_next)
            du_ref[hh] = dvn
            dw_ref[hh] = -_dot3(dvn, s, 1, 1)
            dqd_ref[hh] = _dot3(dov, s, 1, 1)
            dkd_ref[hh] = _dot3(vn, ds_next, 1, 1)
            daqk_ref[hh] = jnp.where(ri >= ci, _dot3(dov, vn, 1, 1), 0.0)
            dgl_ref[hh, 0] = jnp.sum(ds_next * s, axis=0, keepdims=True)
            dstate[hh] = (_dot3(qd_ref[hh], dov, 0, 0) + ds_next * gl_ref[hh, 0]
                          - _dot3(wv, dvn, 0, 0))

    big = pl.BlockSpec((h, CHUNK, GDN_DIM), lambda i: (0, n - 1 - i, 0))
    sq = pl.BlockSpec((h, CHUNK, CHUNK), lambda i: (0, n - 1 - i, 0))
    glb = pl.BlockSpec((h, 1, 1, GDN_DIM), lambda i: (0, n - 1 - i, 0, 0))
    big_shape = jax.ShapeDtypeStruct((h, t, GDN_DIM), F32)
    return pl.pallas_call(
        body, name=name, grid=(n,),
        in_specs=[big] * 4 + [sq, glb, pl.BlockSpec((h, 1, GDN_DIM, GDN_DIM), lambda i: (0, n - 1 - i, 0, 0)), big],
        out_specs=[big] * 4 + [sq, glb],
        out_shape=[big_shape] * 4 + [jax.ShapeDtypeStruct((h, t, CHUNK), F32),
                                     jax.ShapeDtypeStruct((h, n, 1, GDN_DIM), F32)],
        scratch_shapes=[pltpu.VMEM((h, GDN_DIM, GDN_DIM), F32)],
        compiler_params=_cparams("arbitrary"),
    )(u, w, qd, kd, aqk, gl, states, do)


def _gdn_local_bwd(qkv, ab, a_log, dt_bias, du, dw, dqd, dkd, daqk, dgl, *, name):
    t = qkv.shape[1]
    n = t // CHUNK
    cps = _chunks_per_step(n)
    rows = cps * CHUNK

    def body(q_ref, k_ref, v_ref, ab_ref, al_ref, dt_ref, du_ref, dw_ref, dqd_ref, dkd_ref, daqk_ref, dgl_ref,
             dq_ref, dk_ref, dv_ref, dab_ref, dsc_ref):
        head = pl.program_id(0)
        ea = jnp.exp(_scalar_row(al_ref, head))
        dtb = _scalar_row(dt_ref, head)
        lane = lax.broadcasted_iota(jnp.int32, (CHUNK, AB_W), 1)
        lane1 = lax.broadcasted_iota(jnp.int32, (1, GDN_DIM), 1)
        ri = lax.broadcasted_iota(jnp.int32, (CHUNK, CHUNK), 0)
        ci = lax.broadcasted_iota(jnp.int32, (CHUNK, CHUNK), 1)
        utri = (ri <= ci).astype(F32)
        ones = jnp.ones((CHUNK, GDN_DIM), F32)
        acc_alog = jnp.zeros((1, 1), F32)
        acc_dtb = jnp.zeros((1, 1), F32)
        for j in range(cps):
            sl = slice(j * CHUNK, (j + 1) * CHUNK)
            q, k, v = q_ref[0, sl, :], k_ref[0, sl, :], v_ref[0, sl, :]
            r = _gdn_chunk(q, k, v, ab_ref[sl, :], head, ea, dtb)
            duv, dwv, dqdv, dkdv = du_ref[0, sl, :], dw_ref[0, sl, :], dqd_ref[0, sl, :], dkd_ref[0, sl, :]
            d_aqk = jnp.where(r["tri"], daqk_ref[0, sl, :], 0.0)
            dvb = _dot3(r["tinv"], duv, 0, 0)
            dkbg = _dot3(r["tinv"], dwv, 0, 0)
            d_akk = -jnp.where(r["strict"], _dot3(dvb, r["u"], 1, 1) + _dot3(dkbg, r["w"], 1, 1), 0.0)
            e = d_akk * r["a_kk"] + d_aqk * r["a_qk"]
            dmkk, dmqk = d_akk * r["decay"], d_aqk * r["decay"]
            dkb = _dot3(dmkk, k) + dkbg * r["egc"]
            dk = (_dot3(dmkk, r["kb"], 0, 0) + _dot3(dmqk, q, 0, 0) + dkdv * r["ekd"]
                  + dkb * r["beta"])
            dq = _dot3(dmqk, k) + dqdv * r["egc"]
            dq_ref[0, sl, :] = dq
            dk_ref[0, sl, :] = dk
            dv_ref[0, sl, :] = dvb * r["beta"]
            dbeta = jnp.sum(dkb * k + dvb * v, axis=1, keepdims=True)
            kd_term = jnp.sum(dkdv * r["kd"], axis=1, keepdims=True)
            dgc = (jnp.sum(e, axis=1, keepdims=True) + jnp.sum(dqdv * r["qd"] + dkbg * r["kbg"], axis=1, keepdims=True)
                   - kd_term)
            dgc_lanes = jnp.broadcast_to(dgc, (CHUNK, GDN_DIM)) - _dot_exact(e, ones, 0, 0, ones="b")
            dgl_tot = jnp.sum(dgl_ref[0, j], axis=1, keepdims=True) * jnp.exp(r["g_tot"])
            d_tot = jnp.sum(kd_term, axis=0, keepdims=True) + dgl_tot
            dg = _dot_exact(utri, dgc_lanes) + d_tot
            dg = jnp.sum(jnp.where(lane == 0, dg, 0.0), axis=1, keepdims=True)
            da = dg * (-ea) * r["sig_a"]
            db = dbeta * r["beta"] * (1.0 - r["beta"])
            dab_ref[0, sl, :] = jnp.where(lane == head, da, 0.0) + jnp.where(lane == GDN_HEADS + head, db, 0.0)
            acc_alog = acc_alog + jnp.sum(dg * r["g"], axis=0, keepdims=True)
            acc_dtb = acc_dtb + jnp.sum(da, axis=0, keepdims=True)
        dsc_ref[0, 0] = jnp.where(lane1 == 0, acc_alog, 0.0) + jnp.where(lane1 == 1, acc_dtb, 0.0)

    big = pl.BlockSpec((1, rows, GDN_DIM), lambda h, i: (h, i, 0))
    big_shape = jax.ShapeDtypeStruct((GDN_HEADS, t, GDN_DIM), F32)
    return pl.pallas_call(
        body, name=name, grid=(GDN_HEADS, n // cps),
        in_specs=_gdn_local_specs(t, cps) + [big] * 4 + [pl.BlockSpec((1, rows, CHUNK), lambda h, i: (h, i, 0)),
                                                        pl.BlockSpec((1, cps, 1, GDN_DIM), lambda h, i: (h, i, 0, 0))],
        out_specs=[big] * 4 + [pl.BlockSpec((1, 1, 1, GDN_DIM), lambda h, i: (h, i, 0, 0))],
        out_shape=[big_shape] * 4 + [jax.ShapeDtypeStruct((GDN_HEADS, n // cps, 1, GDN_DIM), F32)],
        compiler_params=_cparams("parallel", "parallel"),
    )(qkv, qkv, qkv, ab, a_log, dt_bias, du, dw, dqd, dkd, daqk, dgl)


def _gdn_post_fwd(o, proj, norm_w, *, name):
    h, t, _ = o.shape
    tm = _tile(t, ROW_TILE)

    def body(o_ref, g_ref, w_ref, y_ref):
        for hh in range(h):
            sl = slice(hh * GDN_DIM, (hh + 1) * GDN_DIM)
            ov = o_ref[hh]
            gate = g_ref[:, sl].astype(F32)
            r = lax.rsqrt(jnp.mean(ov * ov, axis=-1, keepdims=True) + EPS)
            y_ref[:, sl] = (ov * r * w_ref[...] * (gate * _sigmoid(gate))).astype(y_ref.dtype)

    return pl.pallas_call(
        body, name=name, grid=(t // tm,),
        in_specs=[pl.BlockSpec((h, tm, GDN_DIM), lambda i: (0, i, 0)),
                  pl.BlockSpec((tm, GDN_W), lambda i: (i, C_GGATE // GDN_W)),
                  pl.BlockSpec((1, GDN_DIM), lambda i: (0, 0))],
        out_specs=pl.BlockSpec((tm, GDN_W), lambda i: (i, 0)),
        out_shape=jax.ShapeDtypeStruct((t, GDN_W), BF16), compiler_params=_cparams("parallel"),
    )(o, proj, norm_w.reshape(1, GDN_DIM))


def _gdn_post_bwd(o, proj, norm_w, dy, *, name):
    h, t, _ = o.shape
    tm = _tile(t, ROW_TILE)

    def body(o_ref, g_ref, w_ref, dy_ref, do_ref, dg_ref, dw_ref):
        part = jnp.zeros((1, GDN_DIM), F32)
        for hh in range(h):
            sl = slice(hh * GDN_DIM, (hh + 1) * GDN_DIM)
            ov = o_ref[hh]
            gate = g_ref[:, sl].astype(F32)
            sig = _sigmoid(gate)
            silu = gate * sig
            r = lax.rsqrt(jnp.mean(ov * ov, axis=-1, keepdims=True) + EPS)
            oh = ov * r
            dyv = dy_ref[:, sl].astype(F32)
            dg_ref[:, sl] = (dyv * oh * w_ref[...] * (sig + silu * (1.0 - sig))).astype(dg_ref.dtype)
            dn = dyv * silu
            part = part + jnp.sum(dn * oh, axis=0, keepdims=True)
            gw = dn * w_ref[...]
            do_ref[hh] = r * (gw - oh * jnp.mean(gw * oh, axis=-1, keepdims=True))

        @pl.when(pl.program_id(0) == 0)
        def _():
            dw_ref[...] = part

        @pl.when(pl.program_id(0) > 0)
        def _():
            dw_ref[...] += part

    return pl.pallas_call(
        body, name=name, grid=(t // tm,),
        in_specs=[pl.BlockSpec((h, tm, GDN_DIM), lambda i: (0, i, 0)),
                  pl.BlockSpec((tm, GDN_W), lambda i: (i, C_GGATE // GDN_W)),
                  pl.BlockSpec((1, GDN_DIM), lambda i: (0, 0)),
                  pl.BlockSpec((tm, GDN_W), lambda i: (i, 0))],
        out_specs=[pl.BlockSpec((h, tm, GDN_DIM), lambda i: (0, i, 0)), pl.BlockSpec((tm, GDN_W), lambda i: (i, 0)),
                   pl.BlockSpec((1, GDN_DIM), lambda i: (0, 0))],
        out_shape=[jax.ShapeDtypeStruct((h, t, GDN_DIM), F32), jax.ShapeDtypeStruct((t, GDN_W), BF16),
                   jax.ShapeDtypeStruct((1, GDN_DIM), F32)],
        compiler_params=_cparams("arbitrary"),
    )(o, proj, norm_w.reshape(1, GDN_DIM), dy)


def _split_dot(x, m):
    hi = x.astype(BF16)
    lo = (x - hi.astype(F32)).astype(BF16)
    return _dot(hi, m) + _dot(lo, m)


def _sb_block(q, kblk, qi, kb):
    bq, bk = q.shape[0], kblk.shape[0]
    z = _dot(q, kblk, 1, 1) * (SB_DIM ** -0.5)
    t_idx = qi * bq + lax.broadcasted_iota(jnp.int32, (bq, bk), 0)
    s_idx = kb * bk + lax.broadcasted_iota(jnp.int32, (bq, bk), 1)
    mask = s_idx < t_idx
    e = jnp.exp(-jnp.abs(z))
    sp = jnp.where(mask, jnp.maximum(z, 0.0) + jnp.log(1.0 + e), 0.0)
    return z, mask, e, sp


def _suffix_ones(blk):
    ri = lax.broadcasted_iota(jnp.int32, (blk, blk), 0)
    ci = lax.broadcasted_iota(jnp.int32, (blk, blk), 1)
    return (ri >= ci).astype(BF16)


def _prefix_ones(blk):
    ri = lax.broadcasted_iota(jnp.int32, (blk, blk), 0)
    ci = lax.broadcasted_iota(jnp.int32, (blk, blk), 1)
    return (ri <= ci).astype(BF16)


def _sb_blocks(t):
    bq = _tile(t, SB_QBLOCK)
    bk = _tile(bq, SB_KBLOCK)
    return bq, bk


def _sb_fwd(q, k, v, *, name):
    h, t, d = q.shape
    bq, bk = _sb_blocks(t)

    def body(q_ref, k_ref, v_ref, o_ref):
        qi = pl.program_id(1)
        qv = q_ref[0]
        suffix = _suffix_ones(bk)
        nkb = (qi + 1) * (bq // bk)

        def step(it, carry):
            later, acc = carry
            kb = nkb - 1 - it
            rows = pl.ds(pl.multiple_of(kb * bk, bk), bk)
            z, mask, _, sp = _sb_block(qv, k_ref[0, rows, :], qi, kb)
            csum = _split_dot(sp, suffix)
            att = jnp.where(mask, jnp.exp(z - csum - later), 0.0)
            acc = acc + _dot(att, v_ref[0, rows, :])
            return later + jnp.sum(sp, axis=1, keepdims=True), acc

        _, acc = lax.fori_loop(0, nkb, step, (jnp.zeros((bq, 1), F32), jnp.zeros((bq, d), F32)))
        o_ref[0] = acc

    qspec = pl.BlockSpec((1, bq, d), lambda hh, i: (hh, i, 0))
    kvspec = pl.BlockSpec((1, t, d), lambda hh, i: (hh, 0, 0))
    return pl.pallas_call(
        body, name=name, grid=(h, t // bq), in_specs=[qspec, kvspec, kvspec], out_specs=qspec,
        out_shape=jax.ShapeDtypeStruct((h, t, d), F32), compiler_params=_cparams("parallel", "parallel"),
    )(q, k, v)


def _sb_bwd(q, k, v, do, *, name):
    h, t, d = q.shape
    bq, bk = _sb_blocks(t)
    scale = SB_DIM ** -0.5

    def body(q_ref, k_ref, v_ref, do_ref, dq_ref, dk_ref, dv_ref, dl_keep, sig_keep):
        qi = pl.program_id(1)

        @pl.when(qi == 0)
        def _():
            dk_ref[...] = jnp.zeros_like(dk_ref)
            dv_ref[...] = jnp.zeros_like(dv_ref)

        qv = q_ref[0]
        dov = do_ref[0]
        suffix = _suffix_ones(bk)
        prefix = _prefix_ones(bk)
        nkb = (qi + 1) * (bq // bk)

        def back(it, later):
            kb = nkb - 1 - it
            rows = pl.ds(pl.multiple_of(kb * bk, bk), bk)
            vblk = v_ref[0, rows, :]
            z, mask, e, sp = _sb_block(qv, k_ref[0, rows, :], qi, kb)
            csum = _split_dot(sp, suffix)
            att = jnp.where(mask, jnp.exp(z - csum - later), 0.0)
            dv_ref[0, rows, :] += _dot(att, dov, 0, 0)
            dl_keep[kb] = att * _dot(dov, vblk, 1, 1)
            sig_keep[kb] = jnp.where(mask, jnp.where(z >= 0, 1.0, e) / (1.0 + e), 0.0)
            return later + jnp.sum(sp, axis=1, keepdims=True)

        lax.fori_loop(0, nkb, back, jnp.zeros((bq, 1), F32))

        def forth(kb, carry):
            before, dq = carry
            rows = pl.ds(pl.multiple_of(kb * bk, bk), bk)
            dl = dl_keep[kb]
            dz = (dl - sig_keep[kb] * (before + _split_dot(dl, prefix))) * scale
            dk_ref[0, rows, :] += _dot(dz, qv, 0, 0)
            return before + jnp.sum(dl, axis=1, keepdims=True), dq + _dot(dz, k_ref[0, rows, :])

        _, dq = lax.fori_loop(0, nkb, forth, (jnp.zeros((bq, 1), F32), jnp.zeros((bq, d), F32)))
        dq_ref[0] = dq

    qspec = pl.BlockSpec((1, bq, d), lambda hh, i: (hh, i, 0))
    kvspec = pl.BlockSpec((1, t, d), lambda hh, i: (hh, 0, 0))
    shape = jax.ShapeDtypeStruct((h, t, d), F32)
    return pl.pallas_call(
        body, name=name, grid=(h, t // bq), in_specs=[qspec, kvspec, kvspec, qspec],
        out_specs=[qspec, kvspec, kvspec], out_shape=[shape] * 3,
        scratch_shapes=[pltpu.VMEM((t // bk, bq, bk), F32), pltpu.VMEM((t // bk, bq, bk), F32)],
        compiler_params=_cparams("parallel", "arbitrary"),
    )(q, k, v, do)


def _gate_specs(tm):
    return [pl.BlockSpec((tm, D_MODEL), lambda i, b=b: (i, C_GATES // D_MODEL + b)) for b in range(3)]


def _merge_fwd(p, proj, *, name):
    t = proj.shape[0]
    tm = _tile(t, ROW_TILE)

    def body(p0, p1, p2, g0, g1, g2, o_ref):
        acc = jnp.zeros((tm, D_MODEL), F32)
        for pr, gr in ((p0, g0), (p1, g1), (p2, g2)):
            acc = acc + _sigmoid(gr[...].astype(F32)) * pr[...].astype(F32)
        o_ref[...] = acc.astype(o_ref.dtype)

    row = pl.BlockSpec((tm, D_MODEL), lambda i: (i, 0))
    return pl.pallas_call(
        body, name=name, grid=(t // tm,), in_specs=[row] * 3 + _gate_specs(tm), out_specs=row,
        out_shape=jax.ShapeDtypeStruct((t, D_MODEL), BF16), compiler_params=_cparams("parallel"),
    )(*p, proj, proj, proj)


def _merge_bwd(p, proj, dmerged, *, name):
    t = proj.shape[0]
    tm = _tile(t, ROW_TILE)

    def body(p0, p1, p2, g0, g1, g2, dm_ref, dp0, dp1, dp2, dg_ref):
        dm = dm_ref[...].astype(F32)
        for b, (pr, gr, dpr) in enumerate(((p0, g0, dp0), (p1, g1, dp1), (p2, g2, dp2))):
            s = _sigmoid(gr[...].astype(F32))
            dpr[...] = (dm * s).astype(dpr.dtype)
            dg_ref[:, b * D_MODEL:(b + 1) * D_MODEL] = (dm * pr[...].astype(F32) * s * (1.0 - s)).astype(dg_ref.dtype)

    row = pl.BlockSpec((tm, D_MODEL), lambda i: (i, 0))
    res = pl.pallas_call(
        body, name=name, grid=(t // tm,), in_specs=[row] * 3 + _gate_specs(tm) + [row],
        out_specs=[row] * 3 + [pl.BlockSpec((tm, 3 * D_MODEL), lambda i: (i, 0))],
        out_shape=[jax.ShapeDtypeStruct((t, D_MODEL), BF16)] * 3 + [jax.ShapeDtypeStruct((t, 3 * D_MODEL), BF16)],
        compiler_params=_cparams("parallel"),
    )(*p, proj, proj, proj, dmerged)
    return res[:3], res[3]


def _loss_head(y, target, *, name):
    t, d = y.shape
    tm = _tile(t, ROW_TILE)

    def body(y_ref, t_ref, dy_ref, l_ref):
        err = y_ref[...] - t_ref[...]
        dy_ref[...] = err * (1.0 / d)
        part = jnp.sum(err * err, axis=0, keepdims=True) * (0.5 / d)

        @pl.when(pl.program_id(0) == 0)
        def _():
            l_ref[...] = part

        @pl.when(pl.program_id(0) > 0)
        def _():
            l_ref[...] += part

    row = pl.BlockSpec((tm, d), lambda i: (i, 0))
    vec = pl.BlockSpec((1, d), lambda i: (0, 0))
    return pl.pallas_call(
        body, name=name, grid=(t // tm,), in_specs=[row, row], out_specs=[row, vec],
        out_shape=[jax.ShapeDtypeStruct((t, d), F32), jax.ShapeDtypeStruct((1, d), F32)],
        compiler_params=_cparams("arbitrary"),
    )(y, target)


def _adamw(w, g, m, v, *, name):
    r, c = w.shape
    tr = r if r * c * 4 <= 2 ** 21 else max(8, (2 ** 21 // (c * 4)) // 8 * 8)
    while r % tr:
        tr -= 8
    c1 = 1.0 - ADAM_B1 ** ADAM_STEP
    c2 = 1.0 - ADAM_B2 ** ADAM_STEP

    def body(w_ref, g_ref, m_ref, v_ref, d_ref, nm_ref, nv_ref):
        gv = g_ref[...]
        nm = ADAM_B1 * m_ref[...] + (1.0 - ADAM_B1) * gv
        nv = ADAM_B2 * v_ref[...] + (1.0 - ADAM_B2) * (gv * gv)
        nm_ref[...] = nm
        nv_ref[...] = nv
        d_ref[...] = -ADAM_LR * ((nm / c1) / (jnp.sqrt(nv / c2) + ADAM_EPS) + ADAM_WD * w_ref[...])

    spec = pl.BlockSpec((tr, c), lambda i: (i, 0))
    return pl.pallas_call(
        body, name=name, grid=(r // tr,), in_specs=[spec] * 4, out_specs=[spec] * 3,
        out_shape=[jax.ShapeDtypeStruct((r, c), F32)] * 3, compiler_params=_cparams("parallel"),
    )(w, g, m, v)


def _all_gather(xs, *, name):
    n = len(xs)

    def body(*refs):
        x_refs, out_refs = refs[:n], refs[n:2 * n]
        send_sems, recv_sems, local_sems = refs[2 * n:]
        mx, my, mc = lax.axis_index("x"), lax.axis_index("y"), lax.axis_index("c")
        me, sibling = (mx, my, mc), (mx, my, 1 - mc)
        chips = [(1 - mx, my), (mx, 1 - my), (1 - mx, 1 - my)]

        def slot(a, px, py, pc):
            return out_refs[a].at[4 * px + 2 * py + pc]

        def copy(a, k, block, to, src=None):
            return pltpu.make_async_remote_copy(
                src_ref=slot(a, *block) if src is None else src, dst_ref=slot(a, *block),
                send_sem=send_sems.at[a, k], recv_sem=recv_sems.at[a, k], device_id=to, device_id_type=MESH_ID)

        mine = [pltpu.make_async_copy(x_refs[a], slot(a, *me), local_sems.at[a]) for a in range(n)]
        for cp in mine:
            cp.start()
        first = [copy(a, 1 + j, me, (*chip, mc), src=x_refs[a]) for j, chip in enumerate(chips) for a in range(n)]
        first += [copy(a, 0, me, sibling, src=x_refs[a]) for a in range(n)]
        for cp in first:
            cp.start()
        passed = []
        for j, chip in enumerate(chips):
            for a in range(n):
                copy(a, 1 + j, (*chip, mc), me).wait_recv()
                passed.append(copy(a, 4 + j, (*chip, mc), sibling))
                passed[-1].start()
        for a in range(n):
            copy(a, 0, sibling, me).wait_recv()
        for j, chip in enumerate(chips):
            for a in range(n):
                copy(a, 4 + j, (*chip, 1 - mc), me).wait_recv()
        for cp in first + passed:
            cp.wait_send()
        for cp in mine:
            cp.wait()

    anyspace = pl.BlockSpec(memory_space=pl.ANY)
    return pl.pallas_call(
        body, name=name, in_specs=[anyspace] * n, out_specs=[anyspace] * n,
        out_shape=[jax.ShapeDtypeStruct((N_DEV,) + x.shape, x.dtype) for x in xs],
        scratch_shapes=[pltpu.SemaphoreType.DMA((n, 7)), pltpu.SemaphoreType.DMA((n, 7)), pltpu.SemaphoreType.DMA((n,))],
    )(*xs)


def _exchange_sibling(gs, *, name):
    n = len(gs)

    def body(*refs):
        g_refs, out_refs = refs[:n], refs[n:2 * n]
        send_sems, recv_sems = refs[2 * n:]
        mx, my, mc = lax.axis_index("x"), lax.axis_index("y"), lax.axis_index("c")
        sibling = (mx, my, 1 - mc)
        copies = []
        for a in range(n):
            for px in range(2):
                for py in range(2):
                    kk = 2 * px + py
                    copies.append(pltpu.make_async_remote_copy(
                        src_ref=g_refs[a].at[4 * px + 2 * py + (1 - mc)], dst_ref=out_refs[a].at[kk],
                        send_sem=send_sems.at[a, kk], recv_sem=recv_sems.at[a, kk], device_id=sibling,
                        device_id_type=MESH_ID))
        for cp in copies:
            cp.start()
        for cp in copies:
            cp.wait_recv()
        for cp in copies:
            cp.wait_send()

    anyspace = pl.BlockSpec(memory_space=pl.ANY)
    return pl.pallas_call(
        body, name=name, in_specs=[anyspace] * n, out_specs=[anyspace] * n,
        out_shape=[jax.ShapeDtypeStruct((4,) + g.shape[1:], g.dtype) for g in gs],
        scratch_shapes=[pltpu.SemaphoreType.DMA((n, 4)), pltpu.SemaphoreType.DMA((n, 4))],
    )(*gs)


def _pair_sum(g, got, *, name):
    _, r, c = g.shape
    tr = _tile(r, ROW_TILE)

    def body(core_ref, a_ref, b_ref, o_ref):
        del core_ref
        o_ref[...] = (a_ref[...].astype(F32) + b_ref[...].astype(F32)).astype(o_ref.dtype)

    grid_spec = pltpu.PrefetchScalarGridSpec(
        num_scalar_prefetch=1, grid=(4, r // tr),
        in_specs=[pl.BlockSpec((1, tr, c), lambda kk, i, core: (2 * kk + core[0], i, 0)),
                  pl.BlockSpec((1, tr, c), lambda kk, i, core: (kk, i, 0))],
        out_specs=pl.BlockSpec((1, tr, c), lambda kk, i, core: (kk, i, 0)))
    return pl.pallas_call(
        body, name=name, grid_spec=grid_spec, out_shape=jax.ShapeDtypeStruct((4, r, c), g.dtype),
        compiler_params=_cparams("parallel", "parallel"),
    )(lax.axis_index("c").astype(jnp.int32).reshape(1), g, got)


def _exchange_chips(parts, *, name):
    n = len(parts)

    def body(*refs):
        p_refs, out_refs = refs[:n], refs[n:2 * n]
        send_sems, recv_sems = refs[2 * n:]
        mx, my, mc = lax.axis_index("x"), lax.axis_index("y"), lax.axis_index("c")
        chips = [(1 - mx, my), (mx, 1 - my), (1 - mx, 1 - my)]
        copies = [pltpu.make_async_remote_copy(
            src_ref=p_refs[a].at[2 * px + py], dst_ref=out_refs[a].at[j], send_sem=send_sems.at[a, j],
            recv_sem=recv_sems.at[a, j], device_id=(px, py, mc), device_id_type=MESH_ID)
            for j, (px, py) in enumerate(chips) for a in range(n)]
        for cp in copies:
            cp.start()
        for cp in copies:
            cp.wait_recv()
        for cp in copies:
            cp.wait_send()

    anyspace = pl.BlockSpec(memory_space=pl.ANY)
    return pl.pallas_call(
        body, name=name, in_specs=[anyspace] * n, out_specs=[anyspace] * n,
        out_shape=[jax.ShapeDtypeStruct((3,) + p.shape[1:], p.dtype) for p in parts],
        scratch_shapes=[pltpu.SemaphoreType.DMA((n, 3)), pltpu.SemaphoreType.DMA((n, 3))],
    )(*parts)


def _final_sum(part, got, *, name):
    _, r, c = part.shape
    tr = _tile(r, ROW_TILE)

    def body(chip_ref, a_ref, b_ref, o_ref):
        del chip_ref
        acc = a_ref[0].astype(F32)
        for j in range(3):
            acc = acc + b_ref[j].astype(F32)
        o_ref[...] = acc

    grid_spec = pltpu.PrefetchScalarGridSpec(
        num_scalar_prefetch=1, grid=(r // tr,),
        in_specs=[pl.BlockSpec((1, tr, c), lambda i, chip: (chip[0], i, 0)),
                  pl.BlockSpec((3, tr, c), lambda i, chip: (0, i, 0))],
        out_specs=pl.BlockSpec((tr, c), lambda i, chip: (i, 0)))
    chip = (2 * lax.axis_index("x") + lax.axis_index("y")).astype(jnp.int32).reshape(1)
    return pl.pallas_call(
        body, name=name, grid_spec=grid_spec, out_shape=jax.ShapeDtypeStruct((r, c), F32),
        compiler_params=_cparams("parallel"),
    )(chip, part, got)


def _sum_devices(x, *, name):
    _, r, c = x.shape

    def body(x_ref, o_ref):
        acc = x_ref[0]
        for j in range(1, N_DEV):
            acc = acc + x_ref[j]
        o_ref[...] = acc

    return pl.pallas_call(body, name=name, out_shape=jax.ShapeDtypeStruct((r, c), F32),
                          compiler_params=_cparams())(x)


BIG = ("w_in", "w_branch", "w_out", "w_ff1", "w_ff2")
BIG_AXIS = {"w_in": 2, "w_branch": 3, "w_out": 1, "w_ff1": 2, "w_ff2": 1}


def _to_global(blocks, axis):
    moved = jnp.moveaxis(blocks, 0, axis)
    shp = moved.shape
    return moved.reshape(shp[:axis] + (shp[axis] * shp[axis + 1],) + shp[axis + 2:])


def _to_blocks(full, axis):
    shp = full.shape
    split = full.reshape(shp[:axis] + (N_DEV, shp[axis] // N_DEV) + shp[axis + 1:])
    return jnp.moveaxis(split, axis, 0)


def _rows(blocks):
    return blocks.reshape(blocks.shape[0], -1, blocks.shape[-1])


def _vec_rows(n):
    return -(-n // 128 // 8) * 8


def _pack_vec(parts):
    flat = jnp.concatenate([p.reshape(-1).astype(F32) for p in parts])
    rows = _vec_rows(flat.shape[0])
    return jnp.pad(flat, (0, rows * 128 - flat.shape[0])).reshape(rows, 128)


def _unpack_vec(flat, shapes):
    lead = flat.shape[:-2]
    flat = flat.reshape(lead + (-1,))
    out, off = [], 0
    for s in shapes:
        n = 1
        for dim in s:
            n *= dim
        out.append(flat[..., off:off + n].reshape(lead + tuple(s)))
        off += n
    return out


def _dot_each(a_list, b_list, ca=1, cb=0):
    return [_dot(a, b, ca, cb) for a, b in zip(a_list, b_list)]


def _dot3_each(a_list, b_list, ca=1, cb=0):
    sa = [_split2(a) for a in a_list]
    sb = [_split2(b) for b in b_list]
    prods = [(_dot(a1, b1, ca, cb), _dot(a1, b2, ca, cb), _dot(a2, b1, ca, cb)) for (a1, a2), (b1, b2) in zip(sa, sb)]
    return [x + (y + z) for x, y, z in prods]


def _split3(x):
    hi = x.astype(BF16)
    rest = x - hi.astype(F32)
    mid = rest.astype(BF16)
    return hi, mid, (rest - mid.astype(F32)).astype(BF16)


def _ones_dot_each(m, x_list, ca=1, cb=0):
    mb = m.astype(BF16)
    parts = [[_dot(mb, p, ca, cb) for p in _split3(x)] for x in x_list]
    return [p[0] + (p[1] + p[2]) for p in parts]


def _dot_ones_each(x_list, m, ca=1, cb=0):
    mb = m.astype(BF16)
    parts = [[_dot(p, mb, ca, cb) for p in _split3(x)] for x in x_list]
    return [p[0] + (p[1] + p[2]) for p in parts]


def _tri_inv_each(a_list):
    c = a_list[0].shape[0]
    ri = lax.broadcasted_iota(jnp.int32, (c, c), 0)
    ci = lax.broadcasted_iota(jnp.int32, (c, c), 1)
    eye = (ri == ci).astype(F32)
    blk = 8
    pws = [-jnp.where(ri // blk == ci // blk, a, 0.0) for a in a_list]
    invs = [eye + b for b in pws]
    for _ in range(2):
        pws = _dot3_each(pws, pws)
        invs = [i + u for i, u in zip(invs, _dot3_each(invs, pws))]
    while blk < c:
        sel = (ri // (2 * blk) == ci // (2 * blk)) & (ri // blk != ci // blk)
        offs = [jnp.where(sel, a, 0.0) for a in a_list]
        invs = [i - t for i, t in zip(invs, _dot3_each(_dot3_each(invs, offs), invs))]
        blk *= 2
    return invs


def _gdn_chunks(qs, ks, vs, abs_, head, ea, dtb):
    c = qs[0].shape[0]
    lane = lax.broadcasted_iota(jnp.int32, abs_[0].shape, 1)
    a_s = [jnp.sum(jnp.where(lane == head, ab, 0.0), axis=1, keepdims=True) for ab in abs_]
    b_s = [jnp.sum(jnp.where(lane == GDN_HEADS + head, ab, 0.0), axis=1, keepdims=True) for ab in abs_]
    ri = lax.broadcasted_iota(jnp.int32, (c, c), 0)
    ci = lax.broadcasted_iota(jnp.int32, (c, c), 1)
    tri, strict = ri >= ci, ri > ci
    ltri = tri.astype(F32)
    beta = [_sigmoid(b) for b in b_s]
    sig_a = [_sigmoid(a + dtb) for a in a_s]
    g = [-ea * _softplus(a + dtb) for a in a_s]
    g_cc = [jnp.broadcast_to(x, (c, c)) for x in g]
    gi = _ones_dot_each(ltri, g_cc)
    gj = _dot_ones_each(g_cc, (ri <= ci).astype(F32), 0, 0)
    decay = [jnp.exp(jnp.where(tri, x - y, -1e30)) for x, y in zip(gi, gj)]
    gc = _ones_dot_each(ltri, [jnp.broadcast_to(x, (c, GDN_DIM)) for x in g])
    g_tot = [jnp.sum(x, axis=0, keepdims=True) for x in g]
    egc = [jnp.exp(x) for x in gc]
    ekd = [jnp.exp(t - x) for t, x in zip(g_tot, gc)]
    kb = [k * b for k, b in zip(ks, beta)]
    vb = [v * b for v, b in zip(vs, beta)]
    kbg = [x * e for x, e in zip(kb, egc)]
    mkk = _dot3_each(kb, ks, 1, 1)
    a_kk = [jnp.where(strict, m * d, 0.0) for m, d in zip(mkk, decay)]
    tinv = _tri_inv_each(a_kk)
    u = _dot3_each(tinv, vb)
    w = _dot3_each(tinv, kbg)
    mqk = _dot3_each(qs, ks, 1, 1)
    a_qk = [jnp.where(tri, m * d, 0.0) for m, d in zip(mqk, decay)]
    qd = [q * e for q, e in zip(qs, egc)]
    kd = [k * e for k, e in zip(ks, ekd)]
    return dict(beta=beta, sig_a=sig_a, g=g, decay=decay, egc=egc, ekd=ekd, g_tot=g_tot, kb=kb, vb=vb, kbg=kbg,
                a_kk=a_kk, tinv=tinv, u=u, w=w, a_qk=a_qk, qd=qd, kd=kd, tri=tri, strict=strict)


def _chunks_in_step(n):
    for cps in (8, 4, 2):
        if n % cps == 0:
            return cps
    return 1


def _gdn_local_fwd_staged(qkv, ab, a_log, dt_bias, *, name):
    t = qkv.shape[1]
    n = t // CHUNK
    cps = _chunks_in_step(n)
    rows = cps * CHUNK
    sls = [slice(j * CHUNK, (j + 1) * CHUNK) for j in range(cps)]

    def body(q_ref, k_ref, v_ref, ab_ref, al_ref, dt_ref, u_ref, w_ref, qd_ref, kd_ref, aqk_ref, gl_ref):
        head = pl.program_id(0)
        ea = jnp.exp(_scalar_row(al_ref, head))
        dtb = _scalar_row(dt_ref, head)
        r = _gdn_chunks([q_ref[0, sl, :] for sl in sls], [k_ref[0, sl, :] for sl in sls],
                        [v_ref[0, sl, :] for sl in sls], [ab_ref[sl, :] for sl in sls], head, ea, dtb)
        for j, sl in enumerate(sls):
            u_ref[0, sl, :] = r["u"][j]
            w_ref[0, sl, :] = r["w"][j]
            qd_ref[0, sl, :] = r["qd"][j]
            kd_ref[0, sl, :] = r["kd"][j]
            aqk_ref[0, sl, :] = r["a_qk"][j]
            gl_ref[0, j] = jnp.exp(jnp.broadcast_to(r["g_tot"][j], (1, GDN_DIM)))

    big = pl.BlockSpec((1, rows, GDN_DIM), lambda h, i: (h, i, 0))
    big_shape = jax.ShapeDtypeStruct((GDN_HEADS, t, GDN_DIM), F32)
    return pl.pallas_call(
        body, name=name, grid=(GDN_HEADS, n // cps), in_specs=_gdn_local_specs(t, cps),
        out_specs=[big] * 4 + [pl.BlockSpec((1, rows, CHUNK), lambda h, i: (h, i, 0)),
                               pl.BlockSpec((1, cps, 1, GDN_DIM), lambda h, i: (h, i, 0, 0))],
        out_shape=[big_shape] * 4 + [jax.ShapeDtypeStruct((GDN_HEADS, t, CHUNK), F32),
                                     jax.ShapeDtypeStruct((GDN_HEADS, n, 1, GDN_DIM), F32)],
        compiler_params=_cparams("parallel", "parallel"),
    )(qkv, qkv, qkv, ab, a_log, dt_bias)


def _gdn_local_bwd_staged(qkv, ab, a_log, dt_bias, du, dw, dqd, dkd, daqk, dgl, *, name):
    t = qkv.shape[1]
    n = t // CHUNK
    cps = _chunks_in_step(n)
    rows = cps * CHUNK
    sls = [slice(j * CHUNK, (j + 1) * CHUNK) for j in range(cps)]

    def body(q_ref, k_ref, v_ref, ab_ref, al_ref, dt_ref, du_ref, dw_ref, dqd_ref, dkd_ref, daqk_ref, dgl_ref,
             dq_ref, dk_ref, dv_ref, dab_ref, dsc_ref):
        head = pl.program_id(0)
        ea = jnp.exp(_scalar_row(al_ref, head))
        dtb = _scalar_row(dt_ref, head)
        lane = lax.broadcasted_iota(jnp.int32, (CHUNK, AB_W), 1)
        lane1 = lax.broadcasted_iota(jnp.int32, (1, GDN_DIM), 1)
        ri = lax.broadcasted_iota(jnp.int32, (CHUNK, CHUNK), 0)
        ci = lax.broadcasted_iota(jnp.int32, (CHUNK, CHUNK), 1)
        utri = (ri <= ci).astype(F32)
        ones = jnp.ones((CHUNK, GDN_DIM), F32)
        qs, ks, vs = ([ref[0, sl, :] for sl in sls] for ref in (q_ref, k_ref, v_ref))
        r = _gdn_chunks(qs, ks, vs, [ab_ref[sl, :] for sl in sls], head, ea, dtb)
        duv, dwv, dqdv, dkdv = ([ref[0, sl, :] for sl in sls] for ref in (du_ref, dw_ref, dqd_ref, dkd_ref))
        d_aqk = [jnp.where(r["tri"], daqk_ref[0, sl, :], 0.0) for sl in sls]
        dvb = _dot3_each(r["tinv"], duv, 0, 0)
        dkbg = _dot3_each(r["tinv"], dwv, 0, 0)
        outer = [x + y for x, y in zip(_dot3_each(dvb, r["u"], 1, 1), _dot3_each(dkbg, r["w"], 1, 1))]
        d_akk = [-jnp.where(r["strict"], x, 0.0) for x in outer]
        e = [x * a + y * b for x, a, y, b in zip(d_akk, r["a_kk"], d_aqk, r["a_qk"])]
        dmkk = [x * d for x, d in zip(d_akk, r["decay"])]
        dmqk = [x * d for x, d in zip(d_aqk, r["decay"])]
        dkb = [x + y * eg for x, y, eg in zip(_dot3_each(dmkk, ks), dkbg, r["egc"])]
        dk = [a + b + x * ek + y * bt for a, b, x, ek, y, bt in zip(
            _dot3_each(dmkk, r["kb"], 0, 0), _dot3_each(dmqk, qs, 0, 0), dkdv, r["ekd"], dkb, r["beta"])]
        dq = [a + x * eg for a, x, eg in zip(_dot3_each(dmqk, ks), dqdv, r["egc"])]
        col_sums = _dot_ones_each(e, ones, 0, 0)
        acc_alog = jnp.zeros((1, 1), F32)
        acc_dtb = jnp.zeros((1, 1), F32)
        dgc_lanes, d_tots, dbetas = [], [], []
        for j in range(cps):
            dbetas.append(jnp.sum(dkb[j] * ks[j] + dvb[j] * vs[j], axis=1, keepdims=True))
            kd_term = jnp.sum(dkdv[j] * r["kd"][j], axis=1, keepdims=True)
            dgc = (jnp.sum(e[j], axis=1, keepdims=True)
                   + jnp.sum(dqdv[j] * r["qd"][j] + dkbg[j] * r["kbg"][j], axis=1, keepdims=True) - kd_term)
            dgc_lanes.append(jnp.broadcast_to(dgc, (CHUNK, GDN_DIM)) - col_sums[j])
            dgl_tot = jnp.sum(dgl_ref[0, j], axis=1, keepdims=True) * jnp.exp(r["g_tot"][j])
            d_tots.append(jnp.sum(kd_term, axis=0, keepdims=True) + dgl_tot)
        suffix_sums = _ones_dot_each(utri, dgc_lanes)
        for j, sl in enumerate(sls):
            dq_ref[0, sl, :] = dq[j]
            dk_ref[0, sl, :] = dk[j]
            dv_ref[0, sl, :] = dvb[j] * r["beta"][j]
            dg = jnp.sum(jnp.where(lane == 0, suffix_sums[j] + d_tots[j], 0.0), axis=1, keepdims=True)
            da = dg * (-ea) * r["sig_a"][j]
            db = dbetas[j] * r["beta"][j] * (1.0 - r["beta"][j])
            dab_ref[0, sl, :] = jnp.where(lane == head, da, 0.0) + jnp.where(lane == GDN_HEADS + head, db, 0.0)
            acc_alog = acc_alog + jnp.sum(dg * r["g"][j], axis=0, keepdims=True)
            acc_dtb = acc_dtb + jnp.sum(da, axis=0, keepdims=True)
        dsc_ref[0, 0] = jnp.where(lane1 == 0, acc_alog, 0.0) + jnp.where(lane1 == 1, acc_dtb, 0.0)

    big = pl.BlockSpec((1, rows, GDN_DIM), lambda h, i: (h, i, 0))
    big_shape = jax.ShapeDtypeStruct((GDN_HEADS, t, GDN_DIM), F32)
    return pl.pallas_call(
        body, name=name, grid=(GDN_HEADS, n // cps),
        in_specs=_gdn_local_specs(t, cps) + [big] * 4 + [pl.BlockSpec((1, rows, CHUNK), lambda h, i: (h, i, 0)),
                                                        pl.BlockSpec((1, cps, 1, GDN_DIM), lambda h, i: (h, i, 0, 0))],
        out_specs=[big] * 4 + [pl.BlockSpec((1, 1, 1, GDN_DIM), lambda h, i: (h, i, 0, 0))],
        out_shape=[big_shape] * 4 + [jax.ShapeDtypeStruct((GDN_HEADS, n // cps, 1, GDN_DIM), F32)],
        compiler_params=_cparams("parallel", "parallel"),
    )(qkv, qkv, qkv, ab, a_log, dt_bias, du, dw, dqd, dkd, daqk, dgl)


def _gdn_scan_fwd_staged(u, w, qd, kd, aqk, gl, *, name):
    h, t, _ = u.shape
    n = t // CHUNK
    hs = range(h)

    def body(u_ref, w_ref, qd_ref, kd_ref, aqk_ref, gl_ref, o_ref, s_ref, state):
        @pl.when(pl.program_id(0) == 0)
        def _():
            state[...] = jnp.zeros_like(state)

        ss = [state[hh] for hh in hs]
        for hh in hs:
            s_ref[hh, 0] = ss[hh]
        vn = [u_ref[hh] - x for hh, x in zip(hs, _dot3_each([w_ref[hh] for hh in hs], ss))]
        from_state = _dot3_each([qd_ref[hh] for hh in hs], ss)
        from_chunk = _dot3_each([aqk_ref[hh] for hh in hs], vn)
        writes = _dot3_each([kd_ref[hh] for hh in hs], vn, 0, 0)
        for hh in hs:
            o_ref[hh] = from_state[hh] + from_chunk[hh]
            state[hh] = ss[hh] * gl_ref[hh, 0] + writes[hh]

    big = pl.BlockSpec((h, CHUNK, GDN_DIM), lambda i: (0, i, 0))
    return pl.pallas_call(
        body, name=name, grid=(n,),
        in_specs=[big] * 4 + [pl.BlockSpec((h, CHUNK, CHUNK), lambda i: (0, i, 0)),
                              pl.BlockSpec((h, 1, 1, GDN_DIM), lambda i: (0, i, 0, 0))],
        out_specs=[big, pl.BlockSpec((h, 1, GDN_DIM, GDN_DIM), lambda i: (0, i, 0, 0))],
        out_shape=[jax.ShapeDtypeStruct((h, t, GDN_DIM), F32), jax.ShapeDtypeStruct((h, n, GDN_DIM, GDN_DIM), F32)],
        scratch_shapes=[pltpu.VMEM((h, GDN_DIM, GDN_DIM), F32)],
        compiler_params=_cparams("arbitrary"),
    )(u, w, qd, kd, aqk, gl)


def _gdn_scan_bwd_staged(u, w, qd, kd, aqk, gl, states, do, *, name):
    h, t, _ = u.shape
    n = t // CHUNK
    hs = range(h)

    def body(u_ref, w_ref, qd_ref, kd_ref, aqk_ref, gl_ref, s_ref, do_ref,
             du_ref, dw_ref, dqd_ref, dkd_ref, daqk_ref, dgl_ref, dstate):
        @pl.when(pl.program_id(0) == 0)
        def _():
            dstate[...] = jnp.zeros_like(dstate)

        ri = lax.broadcasted_iota(jnp.int32, (CHUNK, CHUNK), 0)
        ci = lax.broadcasted_iota(jnp.int32, (CHUNK, CHUNK), 1)
        ss = [s_ref[hh, 0] for hh in hs]
        dsn = [dstate[hh] for hh in hs]
        dov = [do_ref[hh] for hh in hs]
        wv = [w_ref[hh] for hh in hs]
        vn = [u_ref[hh] - x for hh, x in zip(hs, _dot3_each(wv, ss))]
        dvn = [x + y for x, y in zip(_dot3_each([aqk_ref[hh] for hh in hs], dov, 0, 0),
                                     _dot3_each([kd_ref[hh] for hh in hs], dsn))]
        dws = _dot3_each(dvn, ss, 1, 1)
        dqds = _dot3_each(dov, ss, 1, 1)
        dkds = _dot3_each(vn, dsn, 1, 1)
        daqks = _dot3_each(dov, vn, 1, 1)
        reads = _dot3_each([qd_ref[hh] for hh in hs], dov, 0, 0)
        solves = _dot3_each(wv, dvn, 0, 0)
        for hh in hs:
            du_ref[hh] = dvn[hh]
            dw_ref[hh] = -dws[hh]
            dqd_ref[hh] = dqds[hh]
            dkd_ref[hh] = dkds[hh]
            daqk_ref[hh] = jnp.where(ri >= ci, daqks[hh], 0.0)
            dgl_ref[hh, 0] = jnp.sum(dsn[hh] * ss[hh], axis=0, keepdims=True)
            dstate[hh] = reads[hh] + dsn[hh] * gl_ref[hh, 0] - solves[hh]

    big = pl.BlockSpec((h, CHUNK, GDN_DIM), lambda i: (0, n - 1 - i, 0))
    sq = pl.BlockSpec((h, CHUNK, CHUNK), lambda i: (0, n - 1 - i, 0))
    glb = pl.BlockSpec((h, 1, 1, GDN_DIM), lambda i: (0, n - 1 - i, 0, 0))
    big_shape = jax.ShapeDtypeStruct((h, t, GDN_DIM), F32)
    return pl.pallas_call(
        body, name=name, grid=(n,),
        in_specs=[big] * 4 + [sq, glb, pl.BlockSpec((h, 1, GDN_DIM, GDN_DIM), lambda i: (0, n - 1 - i, 0, 0)), big],
        out_specs=[big] * 4 + [sq, glb],
        out_shape=[big_shape] * 4 + [jax.ShapeDtypeStruct((h, t, CHUNK), F32),
                                     jax.ShapeDtypeStruct((h, n, 1, GDN_DIM), F32)],
        scratch_shapes=[pltpu.VMEM((h, GDN_DIM, GDN_DIM), F32)],
        compiler_params=_cparams("arbitrary"),
    )(u, w, qd, kd, aqk, gl, states, do)


SB_PAIRS = SB_HEADS // 2
SB_PAIR_QBLOCK = 256
SB_PAIR_QBLOCK_FWD = 512


def _sb_pair_blocks(t, pref=SB_PAIR_QBLOCK):
    bq = _tile(t, pref)
    return bq, _tile(bq, SB_KBLOCK)


def _sb_pair_specs(t, bq):
    base = C_SBQKV // 128
    return [pl.BlockSpec((bq, 128), lambda p, i: (i, base + p)),
            pl.BlockSpec((t, 128), lambda p, i: (0, base + SB_PAIRS + p)),
            pl.BlockSpec((t, 128), lambda p, i: (0, base + 2 * SB_PAIRS + p))]


def _halves(x, first):
    zero = jnp.zeros_like(x)
    return [jnp.where(first, x, zero), jnp.where(first, zero, x)]


def _sb_mask(qi, kb, bq, bk):
    t_idx = qi * bq + lax.broadcasted_iota(jnp.int32, (bq, bk), 0)
    s_idx = kb * bk + lax.broadcasted_iota(jnp.int32, (bq, bk), 1)
    return s_idx < t_idx


SB_SCALE = SB_DIM ** -0.5


def _sb_pair_scores(qh, kblk, mask):
    zs = _dot_each(qh, [kblk, kblk], 1, 1)
    es = [jnp.exp(-jnp.abs(z)) for z in zs]
    sps = [jnp.maximum(z, 0.0) + jnp.log(1.0 + e) for z, e in zip(zs, es)]
    if mask is not None:
        sps = [jnp.where(mask, sp, 0.0) for sp in sps]
    return zs, es, sps


def _sb_atts(zs, csums, laters, mask):
    atts = [jnp.exp(z - c - l) for z, c, l in zip(zs, csums, laters)]
    return atts if mask is None else [jnp.where(mask, a, 0.0) for a in atts]


def _scaled_queries(q_ref, first):
    return _halves(q_ref[...] * jnp.asarray(SB_SCALE, q_ref.dtype), first)


def _split_dot_each(x_list, m):
    parts = [_split2(x) for x in x_list]
    prods = [(_dot(hi, m), _dot(lo, m)) for hi, lo in parts]
    return [a + b for a, b in prods]


def _sb_pair_fwd(proj, *, name):
    t = proj.shape[0]
    bq, bk = _sb_pair_blocks(t, SB_PAIR_QBLOCK_FWD)

    def body(q_ref, k_ref, v_ref, o_ref):
        qi = pl.program_id(1)
        first = lax.broadcasted_iota(jnp.int32, (1, 128), 1) < SB_DIM
        qh = _scaled_queries(q_ref, first)
        suffix = _suffix_ones(bk)
        band = bq // bk
        nkb = (qi + 1) * band

        def make_step(masked):
            def step(it, carry):
                later0, later1, acc = carry
                kb = nkb - 1 - it
                rows = pl.ds(pl.multiple_of(kb * bk, bk), bk)
                mask = _sb_mask(qi, kb, bq, bk) if masked else None
                zs, _, sps = _sb_pair_scores(qh, k_ref[rows, :], mask)
                atts = _sb_atts(zs, _split_dot_each(sps, suffix), (later0, later1), mask)
                outs = _dot_each(atts, _halves(v_ref[rows, :], first))
                return (later0 + jnp.sum(sps[0], axis=1, keepdims=True),
                        later1 + jnp.sum(sps[1], axis=1, keepdims=True), acc + (outs[0] + outs[1]))
            return step

        zero = jnp.zeros((bq, 1), F32)
        carry = lax.fori_loop(0, band, make_step(True), (zero, zero, jnp.zeros((bq, 128), F32)))
        _, _, acc = lax.fori_loop(band, nkb, make_step(False), carry)
        o_ref[...] = acc.astype(o_ref.dtype)

    return pl.pallas_call(
        body, name=name, grid=(SB_PAIRS, t // bq), in_specs=_sb_pair_specs(t, bq),
        out_specs=pl.BlockSpec((bq, 128), lambda p, i: (i, p)),
        out_shape=jax.ShapeDtypeStruct((t, SB_W), BF16), compiler_params=_cparams("parallel", "parallel"),
    )(proj, proj, proj)


def _sb_pair_bwd(proj, dy, *, name):
    t = proj.shape[0]
    bq, bk = _sb_pair_blocks(t)
    nq = t // bq
    scale = SB_DIM ** -0.5

    def body(q_ref, k_ref, v_ref, do_ref, dq_ref, dk_ref, dv_ref, dl_keep, sig_keep, dk_acc, dv_acc):
        qi = pl.program_id(1)

        @pl.when(qi == 0)
        def _():
            dk_acc[...] = jnp.zeros_like(dk_acc)
            dv_acc[...] = jnp.zeros_like(dv_acc)

        first = lax.broadcasted_iota(jnp.int32, (1, 128), 1) < SB_DIM
        qh = _scaled_queries(q_ref, first)
        doh = _halves(do_ref[...], first)
        suffix = _suffix_ones(bk)
        prefix = _prefix_ones(bk)
        band = bq // bk
        nkb = (qi + 1) * band

        def make_back(masked):
            def back(it, carry):
                kb = nkb - 1 - it
                rows = pl.ds(pl.multiple_of(kb * bk, bk), bk)
                vblk = v_ref[rows, :]
                mask = _sb_mask(qi, kb, bq, bk) if masked else None
                zs, es, sps = _sb_pair_scores(qh, k_ref[rows, :], mask)
                atts = _sb_atts(zs, _split_dot_each(sps, suffix), carry, mask)
                dvs = _dot_each(atts, doh, 0, 0)
                datts = _dot_each(doh, [vblk, vblk], 1, 1)
                dv_acc[rows, :] += dvs[0] + dvs[1]
                for hh in range(2):
                    sig = jnp.where(zs[hh] >= 0, 1.0, es[hh]) / (1.0 + es[hh])
                    dl_keep[hh, kb] = atts[hh] * datts[hh]
                    sig_keep[hh, kb] = sig if mask is None else jnp.where(mask, sig, 0.0)
                return tuple(l + jnp.sum(sp, axis=1, keepdims=True) for l, sp in zip(carry, sps))
            return back

        zero = jnp.zeros((bq, 1), F32)
        lax.fori_loop(band, nkb, make_back(False), lax.fori_loop(0, band, make_back(True), (zero, zero)))

        def forth(kb, carry):
            before0, before1, dq = carry
            rows = pl.ds(pl.multiple_of(kb * bk, bk), bk)
            dls = [dl_keep[hh, kb] for hh in range(2)]
            sums = _split_dot_each(dls, prefix)
            dzs = [dl - sig_keep[hh, kb] * (b + s) for hh, (dl, b, s) in enumerate(zip(dls, (before0, before1), sums))]
            dks = _dot_each(dzs, qh, 0, 0)
            dqs = _dot_each(dzs, _halves(k_ref[rows, :], first))
            dk_acc[rows, :] += dks[0] + dks[1]
            return (before0 + jnp.sum(dls[0], axis=1, keepdims=True), before1 + jnp.sum(dls[1], axis=1, keepdims=True),
                    dq + (dqs[0] + dqs[1]))

        _, _, dq = lax.fori_loop(0, nkb, forth, (zero, zero, jnp.zeros((bq, 128), F32)))
        dq_ref[...] = (dq * SB_SCALE).astype(dq_ref.dtype)

        @pl.when(qi == nq - 1)
        def _():
            dk_ref[...] = dk_acc[...].astype(dk_ref.dtype)
            dv_ref[...] = dv_acc[...].astype(dv_ref.dtype)

    qspec = pl.BlockSpec((bq, 128), lambda p, i: (i, p))
    kvspec = pl.BlockSpec((t, 128), lambda p, i: (0, p))
    shape = jax.ShapeDtypeStruct((t, SB_W), BF16)
    return pl.pallas_call(
        body, name=name, grid=(SB_PAIRS, nq), in_specs=_sb_pair_specs(t, bq) + [qspec],
        out_specs=[qspec, kvspec, kvspec], out_shape=[shape] * 3,
        scratch_shapes=[pltpu.VMEM((2, t // bk, bq, bk), F32), pltpu.VMEM((2, t // bk, bq, bk), F32),
                        pltpu.VMEM((t, 128), F32), pltpu.VMEM((t, 128), F32)],
        compiler_params=_cparams("parallel", "arbitrary"),
    )(proj, proj, proj, dy)


def _heads_major(cols, heads, dim):
    t = cols.shape[0]
    return cols.reshape(t, heads, dim).transpose(1, 0, 2)


def _heads_minor(x):
    h, t, dim = x.shape
    return x.transpose(1, 0, 2).reshape(t, h * dim)


def _sb_qkv(proj):
    return [_heads_major(proj[:, C_SBQKV + i * SB_W:C_SBQKV + (i + 1) * SB_W], SB_HEADS, SB_DIM) for i in range(3)]


def _relu2_epilogue(r):
    a = jnp.maximum(r, 0.0)
    return r, a * a


def _relu2_bwd_epilogue(r, a):
    return (r * 2.0 * jnp.maximum(a.astype(F32), 0.0),)


def _layer_fwd(x, p):
    h = _norm_fwd(x, p["norm_mix_pre"], out_dtype=BF16, name="norm_pre_fwd")
    proj = _mm(h, p["w_main"], name="mm_in")
    ab = _mm(h, p["w_ab"], out_dtypes=(F32,), name="mm_ab")
    qkv = _gdn_pre_fwd(proj, p["conv_qkv_w"], name="gdn_pre_fwd")
    a_log, dt_bias = p["gdn_a_log"].reshape(1, GDN_HEADS), p["gdn_dt_bias"].reshape(1, GDN_HEADS)
    u, w, qd, kd, aqk, gl = _gdn_local_fwd_staged(qkv, ab, a_log, dt_bias, name="gdn_local_fwd")
    o_gdn, states = _gdn_scan_fwd_staged(u, w, qd, kd, aqk, gl, name="gdn_scan_fwd")
    y_a = _gdn_post_fwd(o_gdn, proj, p["gdn_norm_w"], name="gdn_post_fwd")
    y_b = _sb_pair_fwd(proj, name="sb_fwd")
    y_c = _sc_fwd(proj, p["conv_sc_w"], name="sc_fwd")
    ys = (y_a, y_b, y_c)
    ps = tuple(_mm(ys[b], p["w_branch"][b], name="mm_branch") for b in range(3))
    merged = _merge_fwd(ps, proj, name="merge_fwd")
    mo = _mm(merged, p["w_out"], out_dtypes=(F32,), name="mm_out")
    x1 = _norm_fwd(mo, p["norm_mix_post"], x, out_dtype=F32, name="norm_post_fwd")
    h2 = _norm_fwd(x1, p["norm_ffn_pre"], out_dtype=BF16, name="norm_pre_fwd")
    a1, r1 = _mm(h2, p["w_ff1"], out_dtypes=(BF16, BF16), epi=_relu2_epilogue, name="mm_ff1")
    f = _mm(r1, p["w_ff2"], out_dtypes=(F32,), name="mm_ff2")
    x2 = _norm_fwd(f, p["norm_ffn_post"], x1, out_dtype=F32, name="norm_post_fwd")
    saved = dict(x=x, h=h, proj=proj, ab=ab, qkv=qkv, u=u, w=w, qd=qd, kd=kd, aqk=aqk, gl=gl, o_gdn=o_gdn,
                 states=states, ys=ys, ps=ps, merged=merged, mo=mo, x1=x1, h2=h2,
                 a1=a1, r1=r1, f=f)
    return x2, saved


def _layer_bwd(dx2, p, s):
    g = {}
    df, g["norm_ffn_post"] = _norm_bwd(s["f"], p["norm_ffn_post"], dx2, out_dtype=BF16, name="norm_bwd_b")
    da1 = _mm(df, p["w_ff2"], tb=True, epi=_relu2_bwd_epilogue, extras=(s["a1"],), name="mm_ff2_dx")
    g["w_ff2"] = _mm(s["r1"], df, ta=True, name="mm_ff2_dw")
    g["w_ff1"] = _mm(s["h2"], da1, ta=True, name="mm_ff1_dw")
    dh2 = _mm(da1, p["w_ff1"], tb=True, out_dtypes=(F32,), name="mm_ff1_dx")
    dx1, g["norm_ffn_pre"] = _norm_bwd(s["x1"], p["norm_ffn_pre"], dh2, dx2, out_dtype=F32, name="norm_bwd_f")
    dmo, g["norm_mix_post"] = _norm_bwd(s["mo"], p["norm_mix_post"], dx1, out_dtype=BF16, name="norm_bwd_b")
    dmerged = _mm(dmo, p["w_out"], tb=True, name="mm_out_dx")
    g["w_out"] = _mm(s["merged"], dmo, ta=True, name="mm_out_dw")
    dps, dgates = _merge_bwd(s["ps"], s["proj"], dmerged, name="merge_bwd")
    dys = [_mm(dps[b], p["w_branch"][b], tb=True, name="mm_branch_dx") for b in range(3)]
    g["w_branch"] = jnp.stack([_mm(s["ys"][b], dps[b], ta=True, name="mm_branch_dw") for b in range(3)])
    dscx, dscb, dscc, g["conv_sc_w"] = _sc_bwd(s["proj"], p["conv_sc_w"], dys[2], name="sc_bwd")
    dsq, dsk, dsv = _sb_pair_bwd(s["proj"], dys[1], name="sb_bwd")
    a_log, dt_bias = p["gdn_a_log"].reshape(1, GDN_HEADS), p["gdn_dt_bias"].reshape(1, GDN_HEADS)
    do_gdn, dggate, g["gdn_norm_w"] = _gdn_post_bwd(s["o_gdn"], s["proj"], p["gdn_norm_w"], dys[0], name="gdn_post_bwd")
    du, dw, dqd, dkd, daqk, dgl = _gdn_scan_bwd_staged(s["u"], s["w"], s["qd"], s["kd"], s["aqk"], s["gl"],
                                                       s["states"], do_gdn, name="gdn_scan_bwd")
    dq, dk, dv, dab_h, dsc = _gdn_local_bwd_staged(s["qkv"], s["ab"], a_log, dt_bias, du, dw, dqd, dkd, daqk, dgl,
                                                   name="gdn_local_bwd")
    dsc = jnp.sum(dsc, axis=(1, 2))
    g["gdn_a_log"], g["gdn_dt_bias"] = dsc[:, 0], dsc[:, 1]
    dqkv = jnp.concatenate([dq, dk, dv], axis=0)
    dgqkv, g["conv_qkv_w"] = _gdn_pre_bwd(s["proj"], p["conv_qkv_w"], dqkv, name="gdn_pre_bwd")
    dab = jnp.sum(dab_h, axis=0).astype(BF16)
    dproj = jnp.concatenate([dgqkv, dggate, dsq, dsk, dsv, dscx, dscb, dscc, dgates], axis=1)
    g["w_main"] = _mm(s["h"], dproj, ta=True, name="mm_in_dw")
    g["w_ab"] = _mm(s["h"], dab, ta=True, out_dtypes=(F32,), name="mm_ab_dw")
    dh = _mm(dproj, p["w_main"], tb=True, out_dtypes=(F32,), name="mm_in_dx")
    dh_ab = _mm(dab, p["w_ab"], tb=True, out_dtypes=(F32,), name="mm_ab_dx")
    dx, g["norm_mix_pre"] = _norm_bwd(s["x"], p["norm_mix_pre"], dh + dh_ab, dx1, out_dtype=F32, name="norm_bwd_f")
    return dx, g


NORMS = ("norm_mix_pre", "norm_mix_post", "norm_ffn_pre", "norm_ffn_post")
SMALL = NORMS + ("gdn_a_log", "gdn_dt_bias", "gdn_norm_w")
CONVS = ("conv_qkv_w", "conv_sc_w")
AB_LO = 2048


def _split_w_in(w_in):
    main = jnp.concatenate([w_in[..., :AB_LO], w_in[..., AB_LO + 2 * GDN_HEADS:]], axis=-1)
    ab = w_in[..., AB_LO:AB_LO + 2 * GDN_HEADS]
    pad = [(0, 0)] * (ab.ndim - 1) + [(0, AB_W - 2 * GDN_HEADS)]
    return main, jnp.pad(ab, pad)


def _join_w_in(main, ab):
    return jnp.concatenate([main[..., :AB_LO], ab[..., :2 * GDN_HEADS].astype(main.dtype), main[..., AB_LO:]], axis=-1)


def kernel(x, norm_mix_pre, w_in, conv_qkv_w, gdn_a_log, gdn_dt_bias, gdn_norm_w, conv_sc_w, w_branch, w_out, norm_mix_post, norm_ffn_pre, w_ff1, w_ff2, norm_ffn_post, loss_target, m_norm_mix_pre, m_w_in, m_conv_qkv_w, m_gdn_a_log, m_gdn_dt_bias, m_gdn_norm_w, m_conv_sc_w, m_w_branch, m_w_out, m_norm_mix_post, m_norm_ffn_pre, m_w_ff1, m_w_ff2, m_norm_ffn_post, v_norm_mix_pre, v_w_in, v_conv_qkv_w, v_gdn_a_log, v_gdn_dt_bias, v_gdn_norm_w, v_conv_sc_w, v_w_branch, v_w_out, v_norm_mix_post, v_norm_ffn_pre, v_w_ff1, v_w_ff2, v_norm_ffn_post):
    names = ("norm_mix_pre", "w_in", "conv_qkv_w", "gdn_a_log", "gdn_dt_bias", "gdn_norm_w", "conv_sc_w", "w_branch",
             "w_out", "norm_mix_post", "norm_ffn_pre", "w_ff1", "w_ff2", "norm_ffn_post")
    w = dict(zip(names, (norm_mix_pre, w_in, conv_qkv_w, gdn_a_log, gdn_dt_bias, gdn_norm_w, conv_sc_w, w_branch,
                         w_out, norm_mix_post, norm_ffn_pre, w_ff1, w_ff2, norm_ffn_post)))
    m = dict(zip(names, (m_norm_mix_pre, m_w_in, m_conv_qkv_w, m_gdn_a_log, m_gdn_dt_bias, m_gdn_norm_w, m_conv_sc_w,
                         m_w_branch, m_w_out, m_norm_mix_post, m_norm_ffn_pre, m_w_ff1, m_w_ff2, m_norm_ffn_post)))
    v = dict(zip(names, (v_norm_mix_pre, v_w_in, v_conv_qkv_w, v_gdn_a_log, v_gdn_dt_bias, v_gdn_norm_w, v_conv_sc_w,
                         v_w_branch, v_w_out, v_norm_mix_post, v_norm_ffn_pre, v_w_ff1, v_w_ff2, v_norm_ffn_post)))
    me = 4 * lax.axis_index("x") + 2 * lax.axis_index("y") + lax.axis_index("c")

    gathered = _all_gather([w[k].astype(BF16) for k in BIG], name="gather_weights")
    full = {k: _to_global(blk, BIG_AXIS[k]) for k, blk in zip(BIG, gathered)}
    conv_shapes = [w[k].shape for k in CONVS]
    conv_all, = _all_gather([_pack_vec([w[k] for k in CONVS])], name="gather_small")
    for k, blk in zip(CONVS, _unpack_vec(conv_all, conv_shapes)):
        full[k] = _to_global(blk, 2)
    full["w_main"], full["w_ab"] = _split_w_in(full.pop("w_in"))

    xs = x[0]
    layers = []
    for l in range(DEPTH):
        p = {k: full[k][l] for k in full}
        p.update({k: w[k][l] for k in SMALL})
        layers.append(p)

    saved = []
    for l in range(DEPTH):
        xs, s = _layer_fwd(xs, layers[l])
        saved.append(s)
    dy, loss_lanes = _loss_head(xs, loss_target[0], name="loss_head")

    grads = []
    for l in reversed(range(DEPTH)):
        dy, g = _layer_bwd(dy, layers[l], saved[l])
        grads.append(g)
    grads = grads[::-1]
    stack = {k: jnp.stack([g[k] for g in grads]) for k in grads[0]}
    stack["w_in"] = _join_w_in(stack.pop("w_main"), stack.pop("w_ab"))

    blocks = [_rows(_to_blocks(stack[k], BIG_AXIS[k])) for k in BIG]
    got = _exchange_sibling(blocks, name="rs_sibling")
    parts = [_pair_sum(b, g, name="rs_pair_sum") for b, g in zip(blocks, got)]
    got2 = _exchange_chips(parts, name="rs_chips")
    gsum = {k: _final_sum(p, g, name="rs_final_sum").reshape(w[k].shape) for k, p, g in zip(BIG, parts, got2)}

    small_parts = [stack[k] for k in SMALL + CONVS] + [jnp.sum(loss_lanes).reshape(1)]
    small_shapes = [stack[k].shape for k in SMALL + CONVS] + [(1,)]
    summed = _sum_devices(_all_gather([_pack_vec(small_parts)], name="gather_small_grads")[0], name="sum_small")
    small = _unpack_vec(summed, small_shapes)
    loss = small[-1][0]
    for k, val in zip(SMALL + CONVS, small[:-1]):
        gsum[k] = val
    for k in CONVS:
        per = gsum[k].shape[2] // N_DEV
        gsum[k] = lax.dynamic_slice_in_dim(gsum[k], me * per, per, axis=2)

    delta, new_m, new_v = {}, {}, {}
    for k in names:
        shp = w[k].shape
        two_d = (-1, shp[-1]) if len(shp) > 1 else (1, -1)
        d_, m_, v_ = _adamw(w[k].reshape(two_d), gsum[k].reshape(two_d), m[k].reshape(two_d), v[k].reshape(two_d),
                            name="adamw")
        delta[k], new_m[k], new_v[k] = d_.reshape(shp), m_.reshape(shp), v_.reshape(shp)

    return (loss, dy[None], *[gsum[k].reshape(w[k].shape) for k in names], *[delta[k] for k in names], *[new_m[k] for k in names],
            *[new_v[k] for k in names])
```

```python
import functools

import jax
import jax.numpy as jnp
from jax import lax
from jax.experimental import pallas as pl
from jax.experimental.pallas import tpu as pltpu

F32, BF16 = jnp.float32, jnp.bfloat16
MESH_ID = pl.DeviceIdType.MESH

N_DEV = 8
DEPTH = 4
D_MODEL = 1024
D_FF = 4096
EPS = 1e-6
GDN_HEADS, GDN_DIM, GDN_CONV = 4, 128, 4
GDN_W = GDN_HEADS * GDN_DIM
CHUNK = 64
SB_HEADS, SB_DIM = 8, 64
SB_W = SB_HEADS * SB_DIM
SB_QBLOCK, SB_KBLOCK = 512, 256
SC_W, SC_CONV = 512, 3
IN_W = 8200
C_GQKV, C_GGATE, C_SBQKV, C_SCX, C_SCB, C_SCC, C_GATES, MAIN_W = 0, 1536, 2048, 3584, 4096, 4608, 5120, 8192
AB_W = 128

ADAM_LR, ADAM_B1, ADAM_B2, ADAM_EPS, ADAM_WD, ADAM_STEP = 0.001, 0.9, 0.999, 1e-08, 0.01, 10

VMEM_LIMIT = 48 * 2 ** 20


def _cparams(*sem):
    return pltpu.CompilerParams(dimension_semantics=sem or None, vmem_limit_bytes=VMEM_LIMIT)


def _tile(n, pref):
    if n <= pref:
        return n
    t = pref
    while n % t:
        t -= 128
    assert t > 0
    return t


def _dot(a, b, ca=1, cb=0):
    return lax.dot_general(a.astype(BF16), b.astype(BF16), (((ca,), (cb,)), ((), ())), preferred_element_type=F32)


def _split2(x):
    hi = x.astype(BF16)
    return hi, (x - hi.astype(F32)).astype(BF16)


def _dot3(a, b, ca=1, cb=0):
    a1, a2 = _split2(a)
    b1, b2 = _split2(b)
    return _dot(a1, b1, ca, cb) + (_dot(a1, b2, ca, cb) + _dot(a2, b1, ca, cb))


def _dot_exact(a, b, ca=1, cb=0, ones="a"):
    x = b if ones == "a" else a
    m = (a if ones == "a" else b).astype(BF16)
    hi, rest = x.astype(BF16), None
    rest = x - hi.astype(F32)
    mid = rest.astype(BF16)
    lo = (rest - mid.astype(F32)).astype(BF16)
    parts = [_dot(m, p, ca, cb) if ones == "a" else _dot(p, m, ca, cb) for p in (hi, mid, lo)]
    return parts[0] + (parts[1] + parts[2])


def _sigmoid(z):
    e = jnp.exp(-jnp.abs(z))
    return jnp.where(z >= 0, 1.0, e) / (1.0 + e)


def _softplus(z):
    return jnp.maximum(z, 0.0) + jnp.log(1.0 + jnp.exp(-jnp.abs(z)))


def _mm(a, b, *, name, ta=False, tb=False, out_dtypes=(BF16,), epi=None, extras=()):
    assert a.dtype == BF16 and b.dtype == BF16
    m, k = (a.shape[1], a.shape[0]) if ta else a.shape
    n = b.shape[0] if tb else b.shape[1]
    assert (b.shape[1] if tb else b.shape[0]) == k
    tm, tn, tk = _tile(m, 1024), _tile(n, 1024), _tile(k, 1024)
    nk = k // tk
    ca, cb = (0 if ta else 1), (1 if tb else 0)
    n_ex, n_out = len(extras), len(out_dtypes)

    def body(*refs):
        a_ref, b_ref = refs[0], refs[1]
        ex = refs[2:2 + n_ex]
        outs = refs[2 + n_ex:2 + n_ex + n_out]
        acc = refs[-1]
        kk = pl.program_id(2)
        part = lax.dot_general(a_ref[...], b_ref[...], (((ca,), (cb,)), ((), ())), preferred_element_type=F32)

        def finish(r):
            vals = (r,) if epi is None else epi(r, *[e[...] for e in ex])
            for o, v in zip(outs, vals):
                o[...] = v.astype(o.dtype)

        if nk == 1:
            finish(part)
        else:
            @pl.when(kk == 0)
            def _():
                acc[...] = part

            @pl.when(kk > 0)
            def _():
                acc[...] += part

            @pl.when(kk == nk - 1)
            def _():
                finish(acc[...])

    a_spec = pl.BlockSpec((tk, tm), lambda i, j, kk: (kk, i)) if ta else pl.BlockSpec((tm, tk), lambda i, j, kk: (i, kk))
    b_spec = pl.BlockSpec((tn, tk), lambda i, j, kk: (j, kk)) if tb else pl.BlockSpec((tk, tn), lambda i, j, kk: (kk, j))
    io_spec = pl.BlockSpec((tm, tn), lambda i, j, kk: (i, j))
    res = pl.pallas_call(
        body, name=name, grid=(m // tm, n // tn, nk),
        in_specs=[a_spec, b_spec] + [io_spec] * n_ex,
        out_specs=[io_spec] * n_out,
        out_shape=[jax.ShapeDtypeStruct((m, n), dt) for dt in out_dtypes],
        scratch_shapes=[pltpu.VMEM((tm, tn) if nk > 1 else (8, 128), F32)],
        compiler_params=_cparams("parallel", "parallel", "arbitrary"),
    )(a, b, *extras)
    return res[0] if n_out == 1 else res


ROW_TILE = 512


def _norm_fwd(y, w, res=None, *, out_dtype, name):
    t, d = y.shape
    tm = _tile(t, ROW_TILE)
    has_res = res is not None

    def body(*refs):
        y_ref, w_ref = refs[0], refs[1]
        o_ref = refs[-1]
        yv = y_ref[...]
        r = lax.rsqrt(jnp.mean(yv * yv, axis=-1, keepdims=True) + EPS)
        out = yv * r * w_ref[...]
        if has_res:
            out = out + refs[2][...]
        o_ref[...] = out.astype(o_ref.dtype)

    row = pl.BlockSpec((tm, d), lambda i: (i, 0))
    vec = pl.BlockSpec((1, d), lambda i: (0, 0))
    args = (y, w.reshape(1, d)) + ((res,) if has_res else ())
    return pl.pallas_call(
        body, name=name, grid=(t // tm,), in_specs=[row, vec] + [row] * has_res, out_specs=row,
        out_shape=jax.ShapeDtypeStruct((t, d), out_dtype), compiler_params=_cparams("parallel"),
    )(*args)


def _norm_bwd(y, w, dout, add=None, *, out_dtype, name):
    t, d = y.shape
    tm = _tile(t, ROW_TILE)
    has_add = add is not None

    def body(*refs):
        y_ref, w_ref, do_ref = refs[0], refs[1], refs[2]
        dy_ref, dw_ref = refs[-2], refs[-1]
        yv = y_ref[...]
        r = lax.rsqrt(jnp.mean(yv * yv, axis=-1, keepdims=True) + EPS)
        yh = yv * r
        dov = do_ref[...].astype(F32)
        gw = dov * w_ref[...]
        dy = r * (gw - yh * jnp.mean(gw * yh, axis=-1, keepdims=True))
        if has_add:
            dy = dy + refs[3][...]
        dy_ref[...] = dy.astype(dy_ref.dtype)
        part = jnp.sum(dov * yh, axis=0, keepdims=True)

        @pl.when(pl.program_id(0) == 0)
        def _():
            dw_ref[...] = part

        @pl.when(pl.program_id(0) > 0)
        def _():
            dw_ref[...] += part

    row = pl.BlockSpec((tm, d), lambda i: (i, 0))
    vec = pl.BlockSpec((1, d), lambda i: (0, 0))
    args = (y, w.reshape(1, d), dout) + ((add,) if has_add else ())
    return pl.pallas_call(
        body, name=name, grid=(t // tm,), in_specs=[row, vec, row] + [row] * has_add, out_specs=[row, vec],
        out_shape=[jax.ShapeDtypeStruct((t, d), out_dtype), jax.ShapeDtypeStruct((1, d), F32)],
        compiler_params=_cparams("arbitrary"),
    )(*args)


def _shift_down(u, s):
    if s == 0:
        return u
    rows = lax.broadcasted_iota(jnp.int32, u.shape, 0)
    return jnp.where(rows >= s, pltpu.roll(u, s, 0), 0.0)


def _shift_up(u, s):
    if s == 0:
        return u
    t = u.shape[0]
    rows = lax.broadcasted_iota(jnp.int32, u.shape, 0)
    return jnp.where(rows < t - s, pltpu.roll(u, t - s, 0), 0.0)


def _conv_fwd(u, w):
    kk = w.shape[0]
    out = u * w[kk - 1:kk, :]
    for i in range(kk - 1):
        out = out + _shift_down(u, kk - 1 - i) * w[i:i + 1, :]
    return out


def _conv_bwd(u, w, dc):
    kk = w.shape[0]
    du = dc * w[kk - 1:kk, :]
    dws = []
    for i in range(kk):
        s = kk - 1 - i
        if s:
            du = du + _shift_up(dc, s) * w[i:i + 1, :]
        dws.append(jnp.sum(dc * _shift_down(u, s), axis=0, keepdims=True))
    return du, dws


def _gdn_pre_math(x, w, slab):
    c = _conv_fwd(x, w)
    sig = _sigmoid(c)
    s = c * sig
    r = lax.rsqrt(jnp.sum(s * s, axis=-1, keepdims=True) + EPS)
    scale = jnp.where(slab < GDN_HEADS, GDN_DIM ** -0.5, 1.0)
    return c, sig, s, r, scale


def _gdn_pre_fwd(proj, conv_w, *, name):
    t = proj.shape[0]
    nslab = 3 * GDN_HEADS

    def body(x_ref, w_ref, o_ref):
        slab = pl.program_id(0)
        _, _, s, r, scale = _gdn_pre_math(x_ref[...].astype(F32), w_ref[...], slab)
        o_ref[0] = jnp.where(slab < 2 * GDN_HEADS, s * r * scale, s)

    return pl.pallas_call(
        body, name=name, grid=(nslab,),
        in_specs=[pl.BlockSpec((t, GDN_DIM), lambda j: (0, j)), pl.BlockSpec((GDN_CONV, GDN_DIM), lambda j: (0, j))],
        out_specs=pl.BlockSpec((1, t, GDN_DIM), lambda j: (j, 0, 0)),
        out_shape=jax.ShapeDtypeStruct((nslab, t, GDN_DIM), F32), compiler_params=_cparams("parallel"),
    )(proj, conv_w)


def _gdn_pre_bwd(proj, conv_w, dqkv, *, name):
    t = proj.shape[0]
    nslab = 3 * GDN_HEADS

    def body(x_ref, w_ref, d_ref, dx_ref, dw_ref):
        slab = pl.program_id(0)
        x = x_ref[...].astype(F32)
        w = w_ref[...]
        c, sig, s, r, scale = _gdn_pre_math(x, w, slab)
        dout = d_ref[0]
        yn = s * r
        dn = dout * scale
        ds_norm = r * (dn - yn * jnp.sum(dn * yn, axis=-1, keepdims=True))
        ds = jnp.where(slab < 2 * GDN_HEADS, ds_norm, dout)
        dc = ds * (sig + c * sig * (1.0 - sig))
        dx, dws = _conv_bwd(x, w, dc)
        dx_ref[...] = dx.astype(dx_ref.dtype)
        for i, dwi in enumerate(dws):
            dw_ref[i:i + 1, :] = dwi

    return pl.pallas_call(
        body, name=name, grid=(nslab,),
        in_specs=[pl.BlockSpec((t, GDN_DIM), lambda j: (0, j)), pl.BlockSpec((GDN_CONV, GDN_DIM), lambda j: (0, j)),
                  pl.BlockSpec((1, t, GDN_DIM), lambda j: (j, 0, 0))],
        out_specs=[pl.BlockSpec((t, GDN_DIM), lambda j: (0, j)), pl.BlockSpec((GDN_CONV, GDN_DIM), lambda j: (0, j))],
        out_shape=[jax.ShapeDtypeStruct((t, 3 * GDN_W), BF16), jax.ShapeDtypeStruct((GDN_CONV, 3 * GDN_W), F32)],
        compiler_params=_cparams("parallel"),
    )(proj, conv_w, dqkv)


def _sc_specs(t):
    def col(base):
        return pl.BlockSpec((t, 128), lambda j: (0, base // 128 + j))
    return [col(C_SCX), col(C_SCB), col(C_SCC), pl.BlockSpec((SC_CONV, 128), lambda j: (0, j))]


def _sc_fwd(proj, conv_w, *, name):
    t = proj.shape[0]

    def body(x_ref, b_ref, c_ref, w_ref, o_ref):
        u = c_ref[...].astype(F32) * x_ref[...].astype(F32)
        o_ref[...] = (b_ref[...].astype(F32) * _conv_fwd(u, w_ref[...])).astype(o_ref.dtype)

    return pl.pallas_call(
        body, name=name, grid=(SC_W // 128,), in_specs=_sc_specs(t),
        out_specs=pl.BlockSpec((t, 128), lambda j: (0, j)),
        out_shape=jax.ShapeDtypeStruct((t, SC_W), BF16), compiler_params=_cparams("parallel"),
    )(proj, proj, proj, conv_w)


def _sc_bwd(proj, conv_w, dy, *, name):
    t = proj.shape[0]
    nj = SC_W // 128

    def body(x_ref, b_ref, c_ref, w_ref, dy_ref, dx_ref, db_ref, dc_ref, dw_ref):
        x, b, c = x_ref[...].astype(F32), b_ref[...].astype(F32), c_ref[...].astype(F32)
        w = w_ref[...]
        u = c * x
        dyv = dy_ref[...].astype(F32)
        db_ref[...] = (dyv * _conv_fwd(u, w)).astype(db_ref.dtype)
        du, dws = _conv_bwd(u, w, dyv * b)
        dx_ref[...] = (du * c).astype(dx_ref.dtype)
        dc_ref[...] = (du * x).astype(dc_ref.dtype)
        for i, dwi in enumerate(dws):
            dw_ref[i:i + 1, :] = dwi

    return pl.pallas_call(
        body, name=name, grid=(nj,),
        in_specs=_sc_specs(t) + [pl.BlockSpec((t, 128), lambda j: (0, j))],
        out_specs=[pl.BlockSpec((t, 128), lambda j: (0, j))] * 3 + [pl.BlockSpec((SC_CONV, 128), lambda j: (0, j))],
        out_shape=[jax.ShapeDtypeStruct((t, SC_W), BF16)] * 3 + [jax.ShapeDtypeStruct((SC_CONV, SC_W), F32)],
        compiler_params=_cparams("parallel"),
    )(proj, proj, proj, conv_w, dy)


def _tri_inv(a_strict):
    c = a_strict.shape[0]
    ri = lax.broadcasted_iota(jnp.int32, (c, c), 0)
    ci = lax.broadcasted_iota(jnp.int32, (c, c), 1)
    eye = (ri == ci).astype(F32)
    blk = 8
    bm = -jnp.where(ri // blk == ci // blk, a_strict, 0.0)
    inv = eye + bm
    pw = bm
    for _ in range(2):
        pw = _dot3(pw, pw)
        inv = inv + _dot3(inv, pw)
    while blk < c:
        off = jnp.where((ri // (2 * blk) == ci // (2 * blk)) & (ri // blk != ci // blk), a_strict, 0.0)
        inv = inv - _dot3(_dot3(inv, off), inv)
        blk *= 2
    return inv


def _gdn_chunk(q, k, v, ab, head, ea, dtb):
    c = q.shape[0]
    lane = lax.broadcasted_iota(jnp.int32, ab.shape, 1)
    a = jnp.sum(jnp.where(lane == head, ab, 0.0), axis=1, keepdims=True)
    b = jnp.sum(jnp.where(lane == GDN_HEADS + head, ab, 0.0), axis=1, keepdims=True)
    ri = lax.broadcasted_iota(jnp.int32, (c, c), 0)
    ci = lax.broadcasted_iota(jnp.int32, (c, c), 1)
    tri, strict = ri >= ci, ri > ci
    ltri = tri.astype(F32)
    beta = _sigmoid(b)
    sig_a = _sigmoid(a + dtb)
    g = -ea * _softplus(a + dtb)
    g_cc = jnp.broadcast_to(g, (c, c))
    gi = _dot_exact(ltri, g_cc)
    gj = _dot_exact(g_cc, (ri <= ci).astype(F32), 0, 0, ones="b")
    decay = jnp.exp(jnp.where(tri, gi - gj, -1e30))
    gc = _dot_exact(ltri, jnp.broadcast_to(g, (c, GDN_DIM)))
    g_tot = jnp.sum(g, axis=0, keepdims=True)
    egc = jnp.exp(gc)
    ekd = jnp.exp(g_tot - gc)
    kb, vb = k * beta, v * beta
    kbg = kb * egc
    mkk = _dot3(kb, k, 1, 1)
    a_kk = jnp.where(strict, mkk * decay, 0.0)
    tinv = _tri_inv(a_kk)
    u = _dot3(tinv, vb)
    w = _dot3(tinv, kbg)
    mqk = _dot3(q, k, 1, 1)
    a_qk = jnp.where(tri, mqk * decay, 0.0)
    return dict(beta=beta, sig_a=sig_a, g=g, decay=decay, egc=egc, ekd=ekd, g_tot=g_tot, kb=kb, vb=vb, kbg=kbg,
                a_kk=a_kk, tinv=tinv, u=u, w=w, a_qk=a_qk, qd=q * egc, kd=k * ekd, tri=tri, strict=strict)


def _chunks_per_step(n):
    return 4 if n % 4 == 0 else 1


def _gdn_local_specs(t, cps):
    rows = cps * CHUNK

    def slab(base):
        return pl.BlockSpec((1, rows, GDN_DIM), lambda h, n: (base + h, n, 0))
    smem = pl.BlockSpec(memory_space=pltpu.SMEM)
    return [slab(0), slab(GDN_HEADS), slab(2 * GDN_HEADS), pl.BlockSpec((rows, AB_W), lambda h, n: (n, 0)), smem, smem]


def _scalar_row(ref, head):
    return jnp.full((1, 1), ref[0, head], F32)


def _gdn_local_fwd(qkv, ab, a_log, dt_bias, *, name):
    t = qkv.shape[1]
    n = t // CHUNK
    cps = _chunks_per_step(n)
    rows = cps * CHUNK

    def body(q_ref, k_ref, v_ref, ab_ref, al_ref, dt_ref, u_ref, w_ref, qd_ref, kd_ref, aqk_ref, gl_ref):
        head = pl.program_id(0)
        ea = jnp.exp(_scalar_row(al_ref, head))
        dtb = _scalar_row(dt_ref, head)
        for j in range(cps):
            sl = slice(j * CHUNK, (j + 1) * CHUNK)
            r = _gdn_chunk(q_ref[0, sl, :], k_ref[0, sl, :], v_ref[0, sl, :], ab_ref[sl, :], head, ea, dtb)
            u_ref[0, sl, :] = r["u"]
            w_ref[0, sl, :] = r["w"]
            qd_ref[0, sl, :] = r["qd"]
            kd_ref[0, sl, :] = r["kd"]
            aqk_ref[0, sl, :] = r["a_qk"]
            gl_ref[0, j] = jnp.exp(jnp.broadcast_to(r["g_tot"], (1, GDN_DIM)))

    big = pl.BlockSpec((1, rows, GDN_DIM), lambda h, i: (h, i, 0))
    big_shape = jax.ShapeDtypeStruct((GDN_HEADS, t, GDN_DIM), F32)
    return pl.pallas_call(
        body, name=name, grid=(GDN_HEADS, n // cps), in_specs=_gdn_local_specs(t, cps),
        out_specs=[big] * 4 + [pl.BlockSpec((1, rows, CHUNK), lambda h, i: (h, i, 0)),
                               pl.BlockSpec((1, cps, 1, GDN_DIM), lambda h, i: (h, i, 0, 0))],
        out_shape=[big_shape] * 4 + [jax.ShapeDtypeStruct((GDN_HEADS, t, CHUNK), F32),
                                     jax.ShapeDtypeStruct((GDN_HEADS, n, 1, GDN_DIM), F32)],
        compiler_params=_cparams("parallel", "parallel"),
    )(qkv, qkv, qkv, ab, a_log, dt_bias)


def _gdn_scan_fwd(u, w, qd, kd, aqk, gl, *, name):
    h, t, _ = u.shape
    n = t // CHUNK

    def body(u_ref, w_ref, qd_ref, kd_ref, aqk_ref, gl_ref, o_ref, s_ref, state):
        @pl.when(pl.program_id(0) == 0)
        def _():
            state[...] = jnp.zeros_like(state)

        for hh in range(h):
            s = state[hh]
            s_ref[hh, 0] = s
            vn = u_ref[hh] - _dot3(w_ref[hh], s)
            o_ref[hh] = _dot3(qd_ref[hh], s) + _dot3(aqk_ref[hh], vn)
            state[hh] = s * gl_ref[hh, 0] + _dot3(kd_ref[hh], vn, 0, 0)

    big = pl.BlockSpec((h, CHUNK, GDN_DIM), lambda i: (0, i, 0))
    return pl.pallas_call(
        body, name=name, grid=(n,),
        in_specs=[big] * 4 + [pl.BlockSpec((h, CHUNK, CHUNK), lambda i: (0, i, 0)),
                              pl.BlockSpec((h, 1, 1, GDN_DIM), lambda i: (0, i, 0, 0))],
        out_specs=[big, pl.BlockSpec((h, 1, GDN_DIM, GDN_DIM), lambda i: (0, i, 0, 0))],
        out_shape=[jax.ShapeDtypeStruct((h, t, GDN_DIM), F32), jax.ShapeDtypeStruct((h, n, GDN_DIM, GDN_DIM), F32)],
        scratch_shapes=[pltpu.VMEM((h, GDN_DIM, GDN_DIM), F32)],
        compiler_params=_cparams("arbitrary"),
    )(u, w, qd, kd, aqk, gl)


def _gdn_scan_bwd(u, w, qd, kd, aqk, gl, states, do, *, name):
    h, t, _ = u.shape
    n = t // CHUNK

    def body(u_ref, w_ref, qd_ref, kd_ref, aqk_ref, gl_ref, s_ref, do_ref,
             du_ref, dw_ref, dqd_ref, dkd_ref, daqk_ref, dgl_ref, dstate):
        @pl.when(pl.program_id(0) == 0)
        def _():
            dstate[...] = jnp.zeros_like(dstate)

        ri = lax.broadcasted_iota(jnp.int32, (CHUNK, CHUNK), 0)
        ci = lax.broadcasted_iota(jnp.int32, (CHUNK, CHUNK), 1)
        for hh in range(h):
            s, ds_next, dov, wv = s_ref[hh, 0], dstate[hh], do_ref[hh], w_ref[hh]
            vn = u_ref[hh] - _dot3(wv, s)
            dvn = _dot3(aqk_ref[hh], dov, 0, 0) + _dot3(kd_ref[hh], ds_next)
            du_ref[hh] = dvn
            dw_ref[hh] = -_dot3(dvn, s, 1, 1)
            dqd_ref[hh] = _dot3(dov, s, 1, 1)
            dkd_ref[hh] = _dot3(vn, ds_next, 1, 1)
            daqk_ref[hh] = jnp.where(ri >= ci, _dot3(dov, vn, 1, 1), 0.0)
            dgl_ref[hh, 0] = jnp.sum(ds_next * s, axis=0, keepdims=True)
            dstate[hh] = (_dot3(qd_ref[hh], dov, 0, 0) + ds_next * gl_ref[hh, 0]
                          - _dot3(wv, dvn, 0, 0))

    big = pl.BlockSpec((h, CHUNK, GDN_DIM), lambda i: (0, n - 1 - i, 0))
    sq = pl.BlockSpec((h, CHUNK, CHUNK), lambda i: (0, n - 1 - i, 0))
    glb = pl.BlockSpec((h, 1, 1, GDN_DIM), lambda i: (0, n - 1 - i, 0, 0))
    big_shape = jax.ShapeDtypeStruct((h, t, GDN_DIM), F32)
    return pl.pallas_call(
        body, name=name, grid=(n,),
        in_specs=[big] * 4 + [sq, glb, pl.BlockSpec((h, 1, GDN_DIM, GDN_DIM), lambda i: (0, n - 1 - i, 0, 0)), big],
        out_specs=[big] * 4 + [sq, glb],
        out_shape=[big_shape] * 4 + [jax.ShapeDtypeStruct((h, t, CHUNK), F32),
                                     jax.ShapeDtypeStruct((h, n, 1, GDN_DIM), F32)],
        scratch_shapes=[pltpu.VMEM((h, GDN_DIM, GDN_DIM), F32)],
        compiler_params=_cparams("arbitrary"),
    )(u, w, qd, kd, aqk, gl, states, do)


def _gdn_local_bwd(qkv, ab, a_log, dt_bias, du, dw, dqd, dkd, daqk, dgl, *, name):
    t = qkv.shape[1]
    n = t // CHUNK
    cps = _chunks_per_step(n)
    rows = cps * CHUNK

    def body(q_ref, k_ref, v_ref, ab_ref, al_ref, dt_ref, du_ref, dw_ref, dqd_ref, dkd_ref, daqk_ref, dgl_ref,
             dq_ref, dk_ref, dv_ref, dab_ref, dsc_ref):
        head = pl.program_id(0)
        ea = jnp.exp(_scalar_row(al_ref, head))
        dtb = _scalar_row(dt_ref, head)
        lane = lax.broadcasted_iota(jnp.int32, (CHUNK, AB_W), 1)
        lane1 = lax.broadcasted_iota(jnp.int32, (1, GDN_DIM), 1)
        ri = lax.broadcasted_iota(jnp.int32, (CHUNK, CHUNK), 0)
        ci = lax.broadcasted_iota(jnp.int32, (CHUNK, CHUNK), 1)
        utri = (ri <= ci).astype(F32)
        ones = jnp.ones((CHUNK, GDN_DIM), F32)
        acc_alog = jnp.zeros((1, 1), F32)
        acc_dtb = jnp.zeros((1, 1), F32)
        for j in range(cps):
            sl = slice(j * CHUNK, (j + 1) * CHUNK)
            q, k, v = q_ref[0, sl, :], k_ref[0, sl, :], v_ref[0, sl, :]
            r = _gdn_chunk(q, k, v, ab_ref[sl, :], head, ea, dtb)
            duv, dwv, dqdv, dkdv = du_ref[0, sl, :], dw_ref[0, sl, :], dqd_ref[0, sl, :], dkd_ref[0, sl, :]
            d_aqk = jnp.where(r["tri"], daqk_ref[0, sl, :], 0.0)
            dvb = _dot3(r["tinv"], duv, 0, 0)
            dkbg = _dot3(r["tinv"], dwv, 0, 0)
            d_akk = -jnp.where(r["strict"], _dot3(dvb, r["u"], 1, 1) + _dot3(dkbg, r["w"], 1, 1), 0.0)
            e = d_akk * r["a_kk"] + d_aqk * r["a_qk"]
            dmkk, dmqk = d_akk * r["decay"], d_aqk * r["decay"]
            dkb = _dot3(dmkk, k) + dkbg * r["egc"]
            dk = (_dot3(dmkk, r["kb"], 0, 0) + _dot3(dmqk, q, 0, 0) + dkdv * r["ekd"]
                  + dkb * r["beta"])
            dq = _dot3(dmqk, k) + dqdv * r["egc"]
            dq_ref[0, sl, :] = dq
            dk_ref[0, sl, :] = dk
            dv_ref[0, sl, :] = dvb * r["beta"]
            dbeta = jnp.sum(dkb * k + dvb * v, axis=1, keepdims=True)
            kd_term = jnp.sum(dkdv * r["kd"], axis=1, keepdims=True)
            dgc = (jnp.sum(e, axis=1, keepdims=True) + jnp.sum(dqdv * r["qd"] + dkbg * r["kbg"], axis=1, keepdims=True)
                   - kd_term)
            dgc_lanes = jnp.broadcast_to(dgc, (CHUNK, GDN_DIM)) - _dot_exact(e, ones, 0, 0, ones="b")
            dgl_tot = jnp.sum(dgl_ref[0, j], axis=1, keepdims=True) * jnp.exp(r["g_tot"])
            d_tot = jnp.sum(kd_term, axis=0, keepdims=True) + dgl_tot
            dg = _dot_exact(utri, dgc_lanes) + d_tot
            dg = jnp.sum(jnp.where(lane == 0, dg, 0.0), axis=1, keepdims=True)
            da = dg * (-ea) * r["sig_a"]
            db = dbeta * r["beta"] * (1.0 - r["beta"])
            dab_ref[0, sl, :] = jnp.where(lane == head, da, 0.0) + jnp.where(lane == GDN_HEADS + head, db, 0.0)
            acc_alog = acc_alog + jnp.sum(dg * r["g"], axis=0, keepdims=True)
            acc_dtb = acc_dtb + jnp.sum(da, axis=0, keepdims=True)
        dsc_ref[0, 0] = jnp.where(lane1 == 0, acc_alog, 0.0) + jnp.where(lane1 == 1, acc_dtb, 0.0)

    big = pl.BlockSpec((1, rows, GDN_DIM), lambda h, i: (h, i, 0))
    big_shape = jax.ShapeDtypeStruct((GDN_HEADS, t, GDN_DIM), F32)
    return pl.pallas_call(
        body, name=name, grid=(GDN_HEADS, n // cps),
        in_specs=_gdn_local_specs(t, cps) + [big] * 4 + [pl.BlockSpec((1, rows, CHUNK), lambda h, i: (h, i, 0)),
                                                        pl.BlockSpec((1, cps, 1, GDN_DIM), lambda h, i: (h, i, 0, 0))],
        out_specs=[big] * 4 + [pl.BlockSpec((1, 1, 1, GDN_DIM), lambda h, i: (h, i, 0, 0))],
        out_shape=[big_shape] * 4 + [jax.ShapeDtypeStruct((GDN_HEADS, n // cps, 1, GDN_DIM), F32)],
        compiler_params=_cparams("parallel", "parallel"),
    )(qkv, qkv, qkv, ab, a_log, dt_bias, du, dw, dqd, dkd, daqk, dgl)


def _gdn_post_fwd(o, proj, norm_w, *, name):
    h, t, _ = o.shape
    tm = _tile(t, ROW_TILE)

    def body(o_ref, g_ref, w_ref, y_ref):
        for hh in range(h):
            sl = slice(hh * GDN_DIM, (hh + 1) * GDN_DIM)
            ov = o_ref[hh]
            gate = g_ref[:, sl].astype(F32)
            r = lax.rsqrt(jnp.mean(ov * ov, axis=-1, keepdims=True) + EPS)
            y_ref[:, sl] = (ov * r * w_ref[...] * (gate * _sigmoid(gate))).astype(y_ref.dtype)

    return pl.pallas_call(
        body, name=name, grid=(t // tm,),
        in_specs=[pl.BlockSpec((h, tm, GDN_DIM), lambda i: (0, i, 0)),
                  pl.BlockSpec((tm, GDN_W), lambda i: (i, C_GGATE // GDN_W)),
                  pl.BlockSpec((1, GDN_DIM), lambda i: (0, 0))],
        out_specs=pl.BlockSpec((tm, GDN_W), lambda i: (i, 0)),
        out_shape=jax.ShapeDtypeStruct((t, GDN_W), BF16), compiler_params=_cparams("parallel"),
    )(o, proj, norm_w.reshape(1, GDN_DIM))


def _gdn_post_bwd(o, proj, norm_w, dy, *, name):
    h, t, _ = o.shape
    tm = _tile(t, ROW_TILE)

    def body(o_ref, g_ref, w_ref, dy_ref, do_ref, dg_ref, dw_ref):
        part = jnp.zeros((1, GDN_DIM), F32)
        for hh in range(h):
            sl = slice(hh * GDN_DIM, (hh + 1) * GDN_DIM)
            ov = o_ref[hh]
            gate = g_ref[:, sl].astype(F32)
            sig = _sigmoid(gate)
            silu = gate * sig
            r = lax.rsqrt(jnp.mean(ov * ov, axis=-1, keepdims=True) + EPS)
            oh = ov * r
            dyv = dy_ref[:, sl].astype(F32)
            dg_ref[:, sl] = (dyv * oh * w_ref[...] * (sig + silu * (1.0 - sig))).astype(dg_ref.dtype)
            dn = dyv * silu
            part = part + jnp.sum(dn * oh, axis=0, keepdims=True)
            gw = dn * w_ref[...]
            do_ref[hh] = r * (gw - oh * jnp.mean(gw * oh, axis=-1, keepdims=True))

        @pl.when(pl.program_id(0) == 0)
        def _():
            dw_ref[...] = part

        @pl.when(pl.program_id(0) > 0)
        def _():
            dw_ref[...] += part

    return pl.pallas_call(
        body, name=name, grid=(t // tm,),
        in_specs=[pl.BlockSpec((h, tm, GDN_DIM), lambda i: (0, i, 0)),
                  pl.BlockSpec((tm, GDN_W), lambda i: (i, C_GGATE // GDN_W)),
                  pl.BlockSpec((1, GDN_DIM), lambda i: (0, 0)),
                  pl.BlockSpec((tm, GDN_W), lambda i: (i, 0))],
        out_specs=[pl.BlockSpec((h, tm, GDN_DIM), lambda i: (0, i, 0)), pl.BlockSpec((tm, GDN_W), lambda i: (i, 0)),
                   pl.BlockSpec((1, GDN_DIM), lambda i: (0, 0))],
        out_shape=[jax.ShapeDtypeStruct((h, t, GDN_DIM), F32), jax.ShapeDtypeStruct((t, GDN_W), BF16),
                   jax.ShapeDtypeStruct((1, GDN_DIM), F32)],
        compiler_params=_cparams("arbitrary"),
    )(o, proj, norm_w.reshape(1, GDN_DIM), dy)


def _split_dot(x, m):
    hi = x.astype(BF16)
    lo = (x - hi.astype(F32)).astype(BF16)
    return _dot(hi, m) + _dot(lo, m)


def _sb_block(q, kblk, qi, kb):
    bq, bk = q.shape[0], kblk.shape[0]
    z = _dot(q, kblk, 1, 1) * (SB_DIM ** -0.5)
    t_idx = qi * bq + lax.broadcasted_iota(jnp.int32, (bq, bk), 0)
    s_idx = kb * bk + lax.broadcasted_iota(jnp.int32, (bq, bk), 1)
    mask = s_idx < t_idx
    e = jnp.exp(-jnp.abs(z))
    sp = jnp.where(mask, jnp.maximum(z, 0.0) + jnp.log(1.0 + e), 0.0)
    return z, mask, e, sp


def _suffix_ones(blk):
    ri = lax.broadcasted_iota(jnp.int32, (blk, blk), 0)
    ci = lax.broadcasted_iota(jnp.int32, (blk, blk), 1)
    return (ri >= ci).astype(BF16)


def _prefix_ones(blk):
    ri = lax.broadcasted_iota(jnp.int32, (blk, blk), 0)
    ci = lax.broadcasted_iota(jnp.int32, (blk, blk), 1)
    return (ri <= ci).astype(BF16)


def _sb_blocks(t):
    bq = _tile(t, SB_QBLOCK)
    bk = _tile(bq, SB_KBLOCK)
    return bq, bk


def _sb_fwd(q, k, v, *, name):
    h, t, d = q.shape
    bq, bk = _sb_blocks(t)

    def body(q_ref, k_ref, v_ref, o_ref):
        qi = pl.program_id(1)
        qv = q_ref[0]
        suffix = _suffix_ones(bk)
        nkb = (qi + 1) * (bq // bk)

        def step(it, carry):
            later, acc = carry
            kb = nkb - 1 - it
            rows = pl.ds(pl.multiple_of(kb * bk, bk), bk)
            z, mask, _, sp = _sb_block(qv, k_ref[0, rows, :], qi, kb)
            csum = _split_dot(sp, suffix)
            att = jnp.where(mask, jnp.exp(z - csum - later), 0.0)
            acc = acc + _dot(att, v_ref[0, rows, :])
            return later + jnp.sum(sp, axis=1, keepdims=True), acc

        _, acc = lax.fori_loop(0, nkb, step, (jnp.zeros((bq, 1), F32), jnp.zeros((bq, d), F32)))
        o_ref[0] = acc

    qspec = pl.BlockSpec((1, bq, d), lambda hh, i: (hh, i, 0))
    kvspec = pl.BlockSpec((1, t, d), lambda hh, i: (hh, 0, 0))
    return pl.pallas_call(
        body, name=name, grid=(h, t // bq), in_specs=[qspec, kvspec, kvspec], out_specs=qspec,
        out_shape=jax.ShapeDtypeStruct((h, t, d), F32), compiler_params=_cparams("parallel", "parallel"),
    )(q, k, v)


def _sb_bwd(q, k, v, do, *, name):
    h, t, d = q.shape
    bq, bk = _sb_blocks(t)
    scale = SB_DIM ** -0.5

    def body(q_ref, k_ref, v_ref, do_ref, dq_ref, dk_ref, dv_ref, dl_keep, sig_keep):
        qi = pl.program_id(1)

        @pl.when(qi == 0)
        def _():
            dk_ref[...] = jnp.zeros_like(dk_ref)
            dv_ref[...] = jnp.zeros_like(dv_ref)

        qv = q_ref[0]
        dov = do_ref[0]
        suffix = _suffix_ones(bk)
        prefix = _prefix_ones(bk)
        nkb = (qi + 1) * (bq // bk)

        def back(it, later):
            kb = nkb - 1 - it
            rows = pl.ds(pl.multiple_of(kb * bk, bk), bk)
            vblk = v_ref[0, rows, :]
            z, mask, e, sp = _sb_block(qv, k_ref[0, rows, :], qi, kb)
            csum = _split_dot(sp, suffix)
            att = jnp.where(mask, jnp.exp(z - csum - later), 0.0)
            dv_ref[0, rows, :] += _dot(att, dov, 0, 0)
            dl_keep[kb] = att * _dot(dov, vblk, 1, 1)
            sig_keep[kb] = jnp.where(mask, jnp.where(z >= 0, 1.0, e) / (1.0 + e), 0.0)
            return later + jnp.sum(sp, axis=1, keepdims=True)

        lax.fori_loop(0, nkb, back, jnp.zeros((bq, 1), F32))

        def forth(kb, carry):
            before, dq = carry
            rows = pl.ds(pl.multiple_of(kb * bk, bk), bk)
            dl = dl_keep[kb]
            dz = (dl - sig_keep[kb] * (before + _split_dot(dl, prefix))) * scale
            dk_ref[0, rows, :] += _dot(dz, qv, 0, 0)
            return before + jnp.sum(dl, axis=1, keepdims=True), dq + _dot(dz, k_ref[0, rows, :])

        _, dq = lax.fori_loop(0, nkb, forth, (jnp.zeros((bq, 1), F32), jnp.zeros((bq, d), F32)))
        dq_ref[0] = dq

    qspec = pl.BlockSpec((1, bq, d), lambda hh, i: (hh, i, 0))
    kvspec = pl.BlockSpec((1, t, d), lambda hh, i: (hh, 0, 0))
    shape = jax.ShapeDtypeStruct((h, t, d), F32)
    return pl.pallas_call(
        body, name=name, grid=(h, t // bq), in_specs=[qspec, kvspec, kvspec, qspec],
        out_specs=[qspec, kvspec, kvspec], out_shape=[shape] * 3,
        scratch_shapes=[pltpu.VMEM((t // bk, bq, bk), F32), pltpu.VMEM((t // bk, bq, bk), F32)],
        compiler_params=_cparams("parallel", "arbitrary"),
    )(q, k, v, do)


def _gate_specs(tm):
    return [pl.BlockSpec((tm, D_MODEL), lambda i, b=b: (i, C_GATES // D_MODEL + b)) for b in range(3)]


def _merge_fwd(p, proj, *, name):
    t = proj.shape[0]
    tm = _tile(t, ROW_TILE)

    def body(p0, p1, p2, g0, g1, g2, o_ref):
        acc = jnp.zeros((tm, D_MODEL), F32)
        for pr, gr in ((p0, g0), (p1, g1), (p2, g2)):
            acc = acc + _sigmoid(gr[...].astype(F32)) * pr[...].astype(F32)
        o_ref[...] = acc.astype(o_ref.dtype)

    row = pl.BlockSpec((tm, D_MODEL), lambda i: (i, 0))
    return pl.pallas_call(
        body, name=name, grid=(t // tm,), in_specs=[row] * 3 + _gate_specs(tm), out_specs=row,
        out_shape=jax.ShapeDtypeStruct((t, D_MODEL), BF16), compiler_params=_cparams("parallel"),
    )(*p, proj, proj, proj)


def _merge_bwd(p, proj, dmerged, *, name):
    t = proj.shape[0]
    tm = _tile(t, ROW_TILE)

    def body(p0, p1, p2, g0, g1, g2, dm_ref, dp0, dp1, dp2, dg_ref):
        dm = dm_ref[...].astype(F32)
        for b, (pr, gr, dpr) in enumerate(((p0, g0, dp0), (p1, g1, dp1), (p2, g2, dp2))):
            s = _sigmoid(gr[...].astype(F32))
            dpr[...] = (dm * s).astype(dpr.dtype)
            dg_ref[:, b * D_MODEL:(b + 1) * D_MODEL] = (dm * pr[...].astype(F32) * s * (1.0 - s)).astype(dg_ref.dtype)

    row = pl.BlockSpec((tm, D_MODEL), lambda i: (i, 0))
    res = pl.pallas_call(
        body, name=name, grid=(t // tm,), in_specs=[row] * 3 + _gate_specs(tm) + [row],
        out_specs=[row] * 3 + [pl.BlockSpec((tm, 3 * D_MODEL), lambda i: (i, 0))],
        out_shape=[jax.ShapeDtypeStruct((t, D_MODEL), BF16)] * 3 + [jax.ShapeDtypeStruct((t, 3 * D_MODEL), BF16)],
        compiler_params=_cparams("parallel"),
    )(*p, proj, proj, proj, dmerged)
    return res[:3], res[3]


def _loss_head(y, target, *, name):
    t, d = y.shape
    tm = _tile(t, ROW_TILE)

    def body(y_ref, t_ref, dy_ref, l_ref):
        err = y_ref[...] - t_ref[...]
        dy_ref[...] = err * (1.0 / d)
        part = jnp.sum(err * err, axis=0, keepdims=True) * (0.5 / d)

        @pl.when(pl.program_id(0) == 0)
        def _():
            l_ref[...] = part

        @pl.when(pl.program_id(0) > 0)
        def _():
            l_ref[...] += part

    row = pl.BlockSpec((tm, d), lambda i: (i, 0))
    vec = pl.BlockSpec((1, d), lambda i: (0, 0))
    return pl.pallas_call(
        body, name=name, grid=(t // tm,), in_specs=[row, row], out_specs=[row, vec],
        out_shape=[jax.ShapeDtypeStruct((t, d), F32), jax.ShapeDtypeStruct((1, d), F32)],
        compiler_params=_cparams("arbitrary"),
    )(y, target)


def _adamw(w, g, m, v, *, name):
    r, c = w.shape
    tr = r if r * c * 4 <= 2 ** 21 else max(8, (2 ** 21 // (c * 4)) // 8 * 8)
    while r % tr:
        tr -= 8
    c1 = 1.0 - ADAM_B1 ** ADAM_STEP
    c2 = 1.0 - ADAM_B2 ** ADAM_STEP

    def body(w_ref, g_ref, m_ref, v_ref, d_ref, nm_ref, nv_ref):
        gv = g_ref[...]
        nm = ADAM_B1 * m_ref[...] + (1.0 - ADAM_B1) * gv
        nv = ADAM_B2 * v_ref[...] + (1.0 - ADAM_B2) * (gv * gv)
        nm_ref[...] = nm
        nv_ref[...] = nv
        d_ref[...] = -ADAM_LR * ((nm / c1) / (jnp.sqrt(nv / c2) + ADAM_EPS) + ADAM_WD * w_ref[...])

    spec = pl.BlockSpec((tr, c), lambda i: (i, 0))
    return pl.pallas_call(
        body, name=name, grid=(r // tr,), in_specs=[spec] * 4, out_specs=[spec] * 3,
        out_shape=[jax.ShapeDtypeStruct((r, c), F32)] * 3, compiler_params=_cparams("parallel"),
    )(w, g, m, v)


def _all_gather(xs, *, name):
    n = len(xs)

    def body(*refs):
        x_refs, out_refs = refs[:n], refs[n:2 * n]
        send_sems, recv_sems, local_sems = refs[2 * n:]
        mx, my, mc = lax.axis_index("x"), lax.axis_index("y"), lax.axis_index("c")
        me, sibling = (mx, my, mc), (mx, my, 1 - mc)
        chips = [(1 - mx, my), (mx, 1 - my), (1 - mx, 1 - my)]

        def slot(a, px, py, pc):
            return out_refs[a].at[4 * px + 2 * py + pc]

        def copy(a, k, block, to, src=None):
            return pltpu.make_async_remote_copy(
                src_ref=slot(a, *block) if src is None else src, dst_ref=slot(a, *block),
                send_sem=send_sems.at[a, k], recv_sem=recv_sems.at[a, k], device_id=to, device_id_type=MESH_ID)

        mine = [pltpu.make_async_copy(x_refs[a], slot(a, *me), local_sems.at[a]) for a in range(n)]
        for cp in mine:
            cp.start()
        first = [copy(a, 1 + j, me, (*chip, mc), src=x_refs[a]) for j, chip in enumerate(chips) for a in range(n)]
        first += [copy(a, 0, me, sibling, src=x_refs[a]) for a in range(n)]
        for cp in first:
            cp.start()
        passed = []
        for j, chip in enumerate(chips):
            for a in range(n):
                copy(a, 1 + j, (*chip, mc), me).wait_recv()
                passed.append(copy(a, 4 + j, (*chip, mc), sibling))
                passed[-1].start()
        for a in range(n):
            copy(a, 0, sibling, me).wait_recv()
        for j, chip in enumerate(chips):
            for a in range(n):
                copy(a, 4 + j, (*chip, 1 - mc), me).wait_recv()
        for cp in first + passed:
            cp.wait_send()
        for cp in mine:
            cp.wait()

    anyspace = pl.BlockSpec(memory_space=pl.ANY)
    return pl.pallas_call(
        body, name=name, in_specs=[anyspace] * n, out_specs=[anyspace] * n,
        out_shape=[jax.ShapeDtypeStruct((N_DEV,) + x.shape, x.dtype) for x in xs],
        scratch_shapes=[pltpu.SemaphoreType.DMA((n, 7)), pltpu.SemaphoreType.DMA((n, 7)), pltpu.SemaphoreType.DMA((n,))],
    )(*xs)


def _exchange_sibling(gs, *, name):
    n = len(gs)

    def body(*refs):
        g_refs, out_refs = refs[:n], refs[n:2 * n]
        send_sems, recv_sems = refs[2 * n:]
        mx, my, mc = lax.axis_index("x"), lax.axis_index("y"), lax.axis_index("c")
        sibling = (mx, my, 1 - mc)
        copies = []
        for a in range(n):
            for px in range(2):
                for py in range(2):
                    kk = 2 * px + py
                    copies.append(pltpu.make_async_remote_copy(
                        src_ref=g_refs[a].at[4 * px + 2 * py + (1 - mc)], dst_ref=out_refs[a].at[kk],
                        send_sem=send_sems.at[a, kk], recv_sem=recv_sems.at[a, kk], device_id=sibling,
                        device_id_type=MESH_ID))
        for cp in copies:
            cp.start()
        for cp in copies:
            cp.wait_recv()
        for cp in copies:
            cp.wait_send()

    anyspace = pl.BlockSpec(memory_space=pl.ANY)
    return pl.pallas_call(
        body, name=name, in_specs=[anyspace] * n, out_specs=[anyspace] * n,
        out_shape=[jax.ShapeDtypeStruct((4,) + g.shape[1:], g.dtype) for g in gs],
        scratch_shapes=[pltpu.SemaphoreType.DMA((n, 4)), pltpu.SemaphoreType.DMA((n, 4))],
    )(*gs)


def _pair_sum(g, got, *, name):
    _, r, c = g.shape
    tr = _tile(r, ROW_TILE)

    def body(core_ref, a_ref, b_ref, o_ref):
        del core_ref
        o_ref[...] = (a_ref[...].astype(F32) + b_ref[...].astype(F32)).astype(o_ref.dtype)

    grid_spec = pltpu.PrefetchScalarGridSpec(
        num_scalar_prefetch=1, grid=(4, r // tr),
        in_specs=[pl.BlockSpec((1, tr, c), lambda kk, i, core: (2 * kk + core[0], i, 0)),
                  pl.BlockSpec((1, tr, c), lambda kk, i, core: (kk, i, 0))],
        out_specs=pl.BlockSpec((1, tr, c), lambda kk, i, core: (kk, i, 0)))
    return pl.pallas_call(
        body, name=name, grid_spec=grid_spec, out_shape=jax.ShapeDtypeStruct((4, r, c), g.dtype),
        compiler_params=_cparams("parallel", "parallel"),
    )(lax.axis_index("c").astype(jnp.int32).reshape(1), g, got)


def _exchange_chips(parts, *, name):
    n = len(parts)

    def body(*refs):
        p_refs, out_refs = refs[:n], refs[n:2 * n]
        send_sems, recv_sems = refs[2 * n:]
        mx, my, mc = lax.axis_index("x"), lax.axis_index("y"), lax.axis_index("c")
        chips = [(1 - mx, my), (mx, 1 - my), (1 - mx, 1 - my)]
        copies = [pltpu.make_async_remote_copy(
            src_ref=p_refs[a].at[2 * px + py], dst_ref=out_refs[a].at[j], send_sem=send_sems.at[a, j],
            recv_sem=recv_sems.at[a, j], device_id=(px, py, mc), device_id_type=MESH_ID)
            for j, (px, py) in enumerate(chips) for a in range(n)]
        for cp in copies:
            cp.start()
        for cp in copies:
            cp.wait_recv()
        for cp in copies:
            cp.wait_send()

    anyspace = pl.BlockSpec(memory_space=pl.ANY)
    return pl.pallas_call(
        body, name=name, in_specs=[anyspace] * n, out_specs=[anyspace] * n,
        out_shape=[jax.ShapeDtypeStruct((3,) + p.shape[1:], p.dtype) for p in parts],
        scratch_shapes=[pltpu.SemaphoreType.DMA((n, 3)), pltpu.SemaphoreType.DMA((n, 3))],
    )(*parts)


def _final_sum(part, got, *, name):
    _, r, c = part.shape
    tr = _tile(r, ROW_TILE)

    def body(chip_ref, a_ref, b_ref, o_ref):
        del chip_ref
        acc = a_ref[0].astype(F32)
        for j in range(3):
            acc = acc + b_ref[j].astype(F32)
        o_ref[...] = acc

    grid_spec = pltpu.PrefetchScalarGridSpec(
        num_scalar_prefetch=1, grid=(r // tr,),
        in_specs=[pl.BlockSpec((1, tr, c), lambda i, chip: (chip[0], i, 0)),
                  pl.BlockSpec((3, tr, c), lambda i, chip: (0, i, 0))],
        out_specs=pl.BlockSpec((tr, c), lambda i, chip: (i, 0)))
    chip = (2 * lax.axis_index("x") + lax.axis_index("y")).astype(jnp.int32).reshape(1)
    return pl.pallas_call(
        body, name=name, grid_spec=grid_spec, out_shape=jax.ShapeDtypeStruct((r, c), F32),
        compiler_params=_cparams("parallel"),
    )(chip, part, got)


def _sum_devices(x, *, name):
    _, r, c = x.shape

    def body(x_ref, o_ref):
        acc = x_ref[0]
        for j in range(1, N_DEV):
            acc = acc + x_ref[j]
        o_ref[...] = acc

    return pl.pallas_call(body, name=name, out_shape=jax.ShapeDtypeStruct((r, c), F32),
                          compiler_params=_cparams())(x)


BIG = ("w_in", "w_branch", "w_out", "w_ff1", "w_ff2")
BIG_AXIS = {"w_in": 2, "w_branch": 3, "w_out": 1, "w_ff1": 2, "w_ff2": 1}


def _to_global(blocks, axis):
    moved = jnp.moveaxis(blocks, 0, axis)
    shp = moved.shape
    return moved.reshape(shp[:axis] + (shp[axis] * shp[axis + 1],) + shp[axis + 2:])


def _to_blocks(full, axis):
    shp = full.shape
    split = full.reshape(shp[:axis] + (N_DEV, shp[axis] // N_DEV) + shp[axis + 1:])
    return jnp.moveaxis(split, axis, 0)


def _rows(blocks):
    return blocks.reshape(blocks.shape[0], -1, blocks.shape[-1])


def _vec_rows(n):
    return -(-n // 128 // 8) * 8


def _pack_vec(parts):
    flat = jnp.concatenate([p.reshape(-1).astype(F32) for p in parts])
    rows = _vec_rows(flat.shape[0])
    return jnp.pad(flat, (0, rows * 128 - flat.shape[0])).reshape(rows, 128)


def _unpack_vec(flat, shapes):
    lead = flat.shape[:-2]
    flat = flat.reshape(lead + (-1,))
    out, off = [], 0
    for s in shapes:
        n = 1
        for dim in s:
            n *= dim
        out.append(flat[..., off:off + n].reshape(lead + tuple(s)))
        off += n
    return out


def _dot_each(a_list, b_list, ca=1, cb=0):
    return [_dot(a, b, ca, cb) for a, b in zip(a_list, b_list)]


def _dot3_each(a_list, b_list, ca=1, cb=0):
    sa = [_split2(a) for a in a_list]
    sb = [_split2(b) for b in b_list]
    prods = [(_dot(a1, b1, ca, cb), _dot(a1, b2, ca, cb), _dot(a2, b1, ca, cb)) for (a1, a2), (b1, b2) in zip(sa, sb)]
    return [x + (y + z) for x, y, z in prods]


def _split3(x):
    hi = x.astype(BF16)
    rest = x - hi.astype(F32)
    mid = rest.astype(BF16)
    return hi, mid, (rest - mid.astype(F32)).astype(BF16)


def _ones_dot_each(m, x_list, ca=1, cb=0):
    mb = m.astype(BF16)
    parts = [[_dot(mb, p, ca, cb) for p in _split3(x)] for x in x_list]
    return [p[0] + (p[1] + p[2]) for p in parts]


def _dot_ones_each(x_list, m, ca=1, cb=0):
    mb = m.astype(BF16)
    parts = [[_dot(p, mb, ca, cb) for p in _split3(x)] for x in x_list]
    return [p[0] + (p[1] + p[2]) for p in parts]


def _tri_inv_each(a_list):
    c = a_list[0].shape[0]
    ri = lax.broadcasted_iota(jnp.int32, (c, c), 0)
    ci = lax.broadcasted_iota(jnp.int32, (c, c), 1)
    eye = (ri == ci).astype(F32)
    blk = 8
    pws = [-jnp.where(ri // blk == ci // blk, a, 0.0) for a in a_list]
    invs = [eye + b for b in pws]
    for _ in range(2):
        pws = _dot3_each(pws, pws)
        invs = [i + u for i, u in zip(invs, _dot3_each(invs, pws))]
    while blk < c:
        sel = (ri // (2 * blk) == ci // (2 * blk)) & (ri // blk != ci // blk)
        offs = [jnp.where(sel, a, 0.0) for a in a_list]
        invs = [i - t for i, t in zip(invs, _dot3_each(_dot3_each(invs, offs), invs))]
        blk *= 2
    return invs


def _gdn_chunks(qs, ks, vs, abs_, head, ea, dtb):
    c = qs[0].shape[0]
    lane = lax.broadcasted_iota(jnp.int32, abs_[0].shape, 1)
    a_s = [jnp.sum(jnp.where(lane == head, ab, 0.0), axis=1, keepdims=True) for ab in abs_]
    b_s = [jnp.sum(jnp.where(lane == GDN_HEADS + head, ab, 0.0), axis=1, keepdims=True) for ab in abs_]
    ri = lax.broadcasted_iota(jnp.int32, (c, c), 0)
    ci = lax.broadcasted_iota(jnp.int32, (c, c), 1)
    tri, strict = ri >= ci, ri > ci
    ltri = tri.astype(F32)
    beta = [_sigmoid(b) for b in b_s]
    sig_a = [_sigmoid(a + dtb) for a in a_s]
    g = [-ea * _softplus(a + dtb) for a in a_s]
    g_cc = [jnp.broadcast_to(x, (c, c)) for x in g]
    gi = _ones_dot_each(ltri, g_cc)
    gj = _dot_ones_each(g_cc, (ri <= ci).astype(F32), 0, 0)
    decay = [jnp.exp(jnp.where(tri, x - y, -1e30)) for x, y in zip(gi, gj)]
    gc = _ones_dot_each(ltri, [jnp.broadcast_to(x, (c, GDN_DIM)) for x in g])
    g_tot = [jnp.sum(x, axis=0, keepdims=True) for x in g]
    egc = [jnp.exp(x) for x in gc]
    ekd = [jnp.exp(t - x) for t, x in zip(g_tot, gc)]
    kb = [k * b for k, b in zip(ks, beta)]
    vb = [v * b for v, b in zip(vs, beta)]
    kbg = [x * e for x, e in zip(kb, egc)]
    mkk = _dot3_each(kb, ks, 1, 1)
    a_kk = [jnp.where(strict, m * d, 0.0) for m, d in zip(mkk, decay)]
    tinv = _tri_inv_each(a_kk)
    u = _dot3_each(tinv, vb)
    w = _dot3_each(tinv, kbg)
    mqk = _dot3_each(qs, ks, 1, 1)
    a_qk = [jnp.where(tri, m * d, 0.0) for m, d in zip(mqk, decay)]
    qd = [q * e for q, e in zip(qs, egc)]
    kd = [k * e for k, e in zip(ks, ekd)]
    return dict(beta=beta, sig_a=sig_a, g=g, decay=decay, egc=egc, ekd=ekd, g_tot=g_tot, kb=kb, vb=vb, kbg=kbg,
                a_kk=a_kk, tinv=tinv, u=u, w=w, a_qk=a_qk, qd=qd, kd=kd, tri=tri, strict=strict)


def _chunks_in_step(n):
    for cps in (8, 4, 2):
        if n % cps == 0:
            return cps
    return 1


def _gdn_local_fwd_staged(qkv, ab, a_log, dt_bias, *, name):
    t = qkv.shape[1]
    n = t // CHUNK
    cps = _chunks_in_step(n)
    rows = cps * CHUNK
    sls = [slice(j * CHUNK, (j + 1) * CHUNK) for j in range(cps)]

    def body(q_ref, k_ref, v_ref, ab_ref, al_ref, dt_ref, u_ref, w_ref, qd_ref, kd_ref, aqk_ref, gl_ref):
        head = pl.program_id(0)
        ea = jnp.exp(_scalar_row(al_ref, head))
        dtb = _scalar_row(dt_ref, head)
        r = _gdn_chunks([q_ref[0, sl, :] for sl in sls], [k_ref[0, sl, :] for sl in sls],
                        [v_ref[0, sl, :] for sl in sls], [ab_ref[sl, :] for sl in sls], head, ea, dtb)
        for j, sl in enumerate(sls):
            u_ref[0, sl, :] = r["u"][j]
            w_ref[0, sl, :] = r["w"][j]
            qd_ref[0, sl, :] = r["qd"][j]
            kd_ref[0, sl, :] = r["kd"][j]
            aqk_ref[0, sl, :] = r["a_qk"][j]
            gl_ref[0, j] = jnp.exp(jnp.broadcast_to(r["g_tot"][j], (1, GDN_DIM)))

    big = pl.BlockSpec((1, rows, GDN_DIM), lambda h, i: (h, i, 0))
    big_shape = jax.ShapeDtypeStruct((GDN_HEADS, t, GDN_DIM), F32)
    return pl.pallas_call(
        body, name=name, grid=(GDN_HEADS, n // cps), in_specs=_gdn_local_specs(t, cps),
        out_specs=[big] * 4 + [pl.BlockSpec((1, rows, CHUNK), lambda h, i: (h, i, 0)),
                               pl.BlockSpec((1, cps, 1, GDN_DIM), lambda h, i: (h, i, 0, 0))],
        out_shape=[big_shape] * 4 + [jax.ShapeDtypeStruct((GDN_HEADS, t, CHUNK), F32),
                                     jax.ShapeDtypeStruct((GDN_HEADS, n, 1, GDN_DIM), F32)],
        compiler_params=_cparams("parallel", "parallel"),
    )(qkv, qkv, qkv, ab, a_log, dt_bias)


def _gdn_local_bwd_staged(qkv, ab, a_log, dt_bias, du, dw, dqd, dkd, daqk, dgl, *, name):
    t = qkv.shape[1]
    n = t // CHUNK
    cps = _chunks_in_step(n)
    rows = cps * CHUNK
    sls = [slice(j * CHUNK, (j + 1) * CHUNK) for j in range(cps)]

    def body(q_ref, k_ref, v_ref, ab_ref, al_ref, dt_ref, du_ref, dw_ref, dqd_ref, dkd_ref, daqk_ref, dgl_ref,
             dq_ref, dk_ref, dv_ref, dab_ref, dsc_ref):
        head = pl.program_id(0)
        ea = jnp.exp(_scalar_row(al_ref, head))
        dtb = _scalar_row(dt_ref, head)
        lane = lax.broadcasted_iota(jnp.int32, (CHUNK, AB_W), 1)
        lane1 = lax.broadcasted_iota(jnp.int32, (1, GDN_DIM), 1)
        ri = lax.broadcasted_iota(jnp.int32, (CHUNK, CHUNK), 0)
        ci = lax.broadcasted_iota(jnp.int32, (CHUNK, CHUNK), 1)
        utri = (ri <= ci).astype(F32)
        ones = jnp.ones((CHUNK, GDN_DIM), F32)
        qs, ks, vs = ([ref[0, sl, :] for sl in sls] for ref in (q_ref, k_ref, v_ref))
        r = _gdn_chunks(qs, ks, vs, [ab_ref[sl, :] for sl in sls], head, ea, dtb)
        duv, dwv, dqdv, dkdv = ([ref[0, sl, :] for sl in sls] for ref in (du_ref, dw_ref, dqd_ref, dkd_ref))
        d_aqk = [jnp.where(r["tri"], daqk_ref[0, sl, :], 0.0) for sl in sls]
        dvb = _dot3_each(r["tinv"], duv, 0, 0)
        dkbg = _dot3_each(r["tinv"], dwv, 0, 0)
        outer = [x + y for x, y in zip(_dot3_each(dvb, r["u"], 1, 1), _dot3_each(dkbg, r["w"], 1, 1))]
        d_akk = [-jnp.where(r["strict"], x, 0.0) for x in outer]
        e = [x * a + y * b for x, a, y, b in zip(d_akk, r["a_kk"], d_aqk, r["a_qk"])]
        dmkk = [x * d for x, d in zip(d_akk, r["decay"])]
        dmqk = [x * d for x, d in zip(d_aqk, r["decay"])]
        dkb = [x + y * eg for x, y, eg in zip(_dot3_each(dmkk, ks), dkbg, r["egc"])]
        dk = [a + b + x * ek + y * bt for a, b, x, ek, y, bt in zip(
            _dot3_each(dmkk, r["kb"], 0, 0), _dot3_each(dmqk, qs, 0, 0), dkdv, r["ekd"], dkb, r["beta"])]
        dq = [a + x * eg for a, x, eg in zip(_dot3_each(dmqk, ks), dqdv, r["egc"])]
        col_sums = _dot_ones_each(e, ones, 0, 0)
        acc_alog = jnp.zeros((1, 1), F32)
        acc_dtb = jnp.zeros((1, 1), F32)
        dgc_lanes, d_tots, dbetas = [], [], []
        for j in range(cps):
            dbetas.append(jnp.sum(dkb[j] * ks[j] + dvb[j] * vs[j], axis=1, keepdims=True))
            kd_term = jnp.sum(dkdv[j] * r["kd"][j], axis=1, keepdims=True)
            dgc = (jnp.sum(e[j], axis=1, keepdims=True)
                   + jnp.sum(dqdv[j] * r["qd"][j] + dkbg[j] * r["kbg"][j], axis=1, keepdims=True) - kd_term)
            dgc_lanes.append(jnp.broadcast_to(dgc, (CHUNK, GDN_DIM)) - col_sums[j])
            dgl_tot = jnp.sum(dgl_ref[0, j], axis=1, keepdims=True) * jnp.exp(r["g_tot"][j])
            d_tots.append(jnp.sum(kd_term, axis=0, keepdims=True) + dgl_tot)
        suffix_sums = _ones_dot_each(utri, dgc_lanes)
        for j, sl in enumerate(sls):
            dq_ref[0, sl, :] = dq[j]
            dk_ref[0, sl, :] = dk[j]
            dv_ref[0, sl, :] = dvb[j] * r["beta"][j]
            dg = jnp.sum(jnp.where(lane == 0, suffix_sums[j] + d_tots[j], 0.0), axis=1, keepdims=True)
            da = dg * (-ea) * r["sig_a"][j]
            db = dbetas[j] * r["beta"][j] * (1.0 - r["beta"][j])
            dab_ref[0, sl, :] = jnp.where(lane == head, da, 0.0) + jnp.where(lane == GDN_HEADS + head, db, 0.0)
            acc_alog = acc_alog + jnp.sum(dg * r["g"][j], axis=0, keepdims=True)
            acc_dtb = acc_dtb + jnp.sum(da, axis=0, keepdims=True)
        dsc_ref[0, 0] = jnp.where(lane1 == 0, acc_alog, 0.0) + jnp.where(lane1 == 1, acc_dtb, 0.0)

    big = pl.BlockSpec((1, rows, GDN_DIM), lambda h, i: (h, i, 0))
    big_shape = jax.ShapeDtypeStruct((GDN_HEADS, t, GDN_DIM), F32)
    return pl.pallas_call(
        body, name=name, grid=(GDN_HEADS, n // cps),
        in_specs=_gdn_local_specs(t, cps) + [big] * 4 + [pl.BlockSpec((1, rows, CHUNK), lambda h, i: (h, i, 0)),
                                                        pl.BlockSpec((1, cps, 1, GDN_DIM), lambda h, i: (h, i, 0, 0))],
        out_specs=[big] * 4 + [pl.BlockSpec((1, 1, 1, GDN_DIM), lambda h, i: (h, i, 0, 0))],
        out_shape=[big_shape] * 4 + [jax.ShapeDtypeStruct((GDN_HEADS, n // cps, 1, GDN_DIM), F32)],
        compiler_params=_cparams("parallel", "parallel"),
    )(qkv, qkv, qkv, ab, a_log, dt_bias, du, dw, dqd, dkd, daqk, dgl)


def _gdn_scan_fwd_staged(u, w, qd, kd, aqk, gl, *, name):
    h, t, _ = u.shape
    n = t // CHUNK
    hs = range(h)

    def body(u_ref, w_ref, qd_ref, kd_ref, aqk_ref, gl_ref, o_ref, s_ref, state):
        @pl.when(pl.program_id(0) == 0)
        def _():
            state[...] = jnp.zeros_like(state)

        ss = [state[hh] for hh in hs]
        for hh in hs:
            s_ref[hh, 0] = ss[hh]
        vn = [u_ref[hh] - x for hh, x in zip(hs, _dot3_each([w_ref[hh] for hh in hs], ss))]
        from_state = _dot3_each([qd_ref[hh] for hh in hs], ss)
        from_chunk = _dot3_each([aqk_ref[hh] for hh in hs], vn)
        writes = _dot3_each([kd_ref[hh] for hh in hs], vn, 0, 0)
        for hh in hs:
            o_ref[hh] = from_state[hh] + from_chunk[hh]
            state[hh] = ss[hh] * gl_ref[hh, 0] + writes[hh]

    big = pl.BlockSpec((h, CHUNK, GDN_DIM), lambda i: (0, i, 0))
    return pl.pallas_call(
        body, name=name, grid=(n,),
        in_specs=[big] * 4 + [pl.BlockSpec((h, CHUNK, CHUNK), lambda i: (0, i, 0)),
                              pl.BlockSpec((h, 1, 1, GDN_DIM), lambda i: (0, i, 0, 0))],
        out_specs=[big, pl.BlockSpec((h, 1, GDN_DIM, GDN_DIM), lambda i: (0, i, 0, 0))],
        out_shape=[jax.ShapeDtypeStruct((h, t, GDN_DIM), F32), jax.ShapeDtypeStruct((h, n, GDN_DIM, GDN_DIM), F32)],
        scratch_shapes=[pltpu.VMEM((h, GDN_DIM, GDN_DIM), F32)],
        compiler_params=_cparams("arbitrary"),
    )(u, w, qd, kd, aqk, gl)


def _gdn_scan_bwd_staged(u, w, qd, kd, aqk, gl, states, do, *, name):
    h, t, _ = u.shape
    n = t // CHUNK
    hs = range(h)

    def body(u_ref, w_ref, qd_ref, kd_ref, aqk_ref, gl_ref, s_ref, do_ref,
             du_ref, dw_ref, dqd_ref, dkd_ref, daqk_ref, dgl_ref, dstate):
        @pl.when(pl.program_id(0) == 0)
        def _():
            dstate[...] = jnp.zeros_like(dstate)

        ri = lax.broadcasted_iota(jnp.int32, (CHUNK, CHUNK), 0)
        ci = lax.broadcasted_iota(jnp.int32, (CHUNK, CHUNK), 1)
        ss = [s_ref[hh, 0] for hh in hs]
        dsn = [dstate[hh] for hh in hs]
        dov = [do_ref[hh] for hh in hs]
        wv = [w_ref[hh] for hh in hs]
        vn = [u_ref[hh] - x for hh, x in zip(hs, _dot3_each(wv, ss))]
        dvn = [x + y for x, y in zip(_dot3_each([aqk_ref[hh] for hh in hs], dov, 0, 0),
                                     _dot3_each([kd_ref[hh] for hh in hs], dsn))]
        dws = _dot3_each(dvn, ss, 1, 1)
        dqds = _dot3_each(dov, ss, 1, 1)
        dkds = _dot3_each(vn, dsn, 1, 1)
        daqks = _dot3_each(dov, vn, 1, 1)
        reads = _dot3_each([qd_ref[hh] for hh in hs], dov, 0, 0)
        solves = _dot3_each(wv, dvn, 0, 0)
        for hh in hs:
            du_ref[hh] = dvn[hh]
            dw_ref[hh] = -dws[hh]
            dqd_ref[hh] = dqds[hh]
            dkd_ref[hh] = dkds[hh]
            daqk_ref[hh] = jnp.where(ri >= ci, daqks[hh], 0.0)
            dgl_ref[hh, 0] = jnp.sum(dsn[hh] * ss[hh], axis=0, keepdims=True)
            dstate[hh] = reads[hh] + dsn[hh] * gl_ref[hh, 0] - solves[hh]

    big = pl.BlockSpec((h, CHUNK, GDN_DIM), lambda i: (0, n - 1 - i, 0))
    sq = pl.BlockSpec((h, CHUNK, CHUNK), lambda i: (0, n - 1 - i, 0))
    glb = pl.BlockSpec((h, 1, 1, GDN_DIM), lambda i: (0, n - 1 - i, 0, 0))
    big_shape = jax.ShapeDtypeStruct((h, t, GDN_DIM), F32)
    return pl.pallas_call(
        body, name=name, grid=(n,),
        in_specs=[big] * 4 + [sq, glb, pl.BlockSpec((h, 1, GDN_DIM, GDN_DIM), lambda i: (0, n - 1 - i, 0, 0)), big],
        out_specs=[big] * 4 + [sq, glb],
        out_shape=[big_shape] * 4 + [jax.ShapeDtypeStruct((h, t, CHUNK), F32),
                                     jax.ShapeDtypeStruct((h, n, 1, GDN_DIM), F32)],
        scratch_shapes=[pltpu.VMEM((h, GDN_DIM, GDN_DIM), F32)],
        compiler_params=_cparams("arbitrary"),
    )(u, w, qd, kd, aqk, gl, states, do)


def _gather_phases(x_refs, out_refs, send_sems, recv_sems, local_sems):
    n = len(x_refs)
    mx, my, mc = lax.axis_index("x"), lax.axis_index("y"), lax.axis_index("c")
    me, sibling = (mx, my, mc), (mx, my, 1 - mc)
    chips = [(1 - mx, my), (mx, 1 - my), (1 - mx, 1 - my)]

    def slot(a, px, py, pc):
        return out_refs[a].at[4 * px + 2 * py + pc]

    def copy(a, k, block, to, src=None):
        return pltpu.make_async_remote_copy(
            src_ref=slot(a, *block) if src is None else src, dst_ref=slot(a, *block),
            send_sem=send_sems.at[a, k], recv_sem=recv_sems.at[a, k], device_id=to, device_id_type=MESH_ID)

    def mine():
        return [pltpu.make_async_copy(x_refs[a], slot(a, *me), local_sems.at[a]) for a in range(n)]

    def first():
        out = [copy(a, 1 + j, me, (*chip, mc), src=x_refs[a]) for j, chip in enumerate(chips) for a in range(n)]
        return out + [copy(a, 0, me, sibling, src=x_refs[a]) for a in range(n)]

    def passed():
        return [copy(a, 4 + j, (*chip, mc), sibling) for j, chip in enumerate(chips) for a in range(n)]

    def start():
        for cp in mine() + first():
            cp.start()

    def pass_on():
        for j, chip in enumerate(chips):
            for a in range(n):
                copy(a, 1 + j, (*chip, mc), me).wait_recv()
                copy(a, 4 + j, (*chip, mc), sibling).start()

    def finish():
        for a in range(n):
            copy(a, 0, sibling, me).wait_recv()
        for j, chip in enumerate(chips):
            for a in range(n):
                copy(a, 4 + j, (*chip, 1 - mc), me).wait_recv()
        for cp in first() + passed():
            cp.wait_send()
        for cp in mine():
            cp.wait()

    return start, pass_on, finish


def _gather_extras(shards):
    n = len(shards)
    anyspace = pl.BlockSpec(memory_space=pl.ANY)
    return ([anyspace] * n, [anyspace] * n, [jax.ShapeDtypeStruct((N_DEV,) + x.shape, x.dtype) for x in shards],
            [pltpu.SemaphoreType.DMA((n, 7)), pltpu.SemaphoreType.DMA((n, 7)), pltpu.SemaphoreType.DMA((n,))])


def _scatter_phases(g_refs, out_refs, send_sems, recv_sems, local_sems):
    n = len(g_refs)
    mx, my, mc = lax.axis_index("x"), lax.axis_index("y"), lax.axis_index("c")
    me_id = 4 * mx + 2 * my + mc

    def peer(r):
        return (1 - mx if r & 4 else mx, 1 - my if r & 2 else my, 1 - mc if r & 1 else mc)

    def peer_id(r):
        px, py, pc = peer(r)
        return 4 * px + 2 * py + pc

    def copies():
        return [pltpu.make_async_remote_copy(
            src_ref=g_refs[a].at[peer_id(r)], dst_ref=out_refs[a].at[me_id], send_sem=send_sems.at[a, r - 1],
            recv_sem=recv_sems.at[a, r - 1], device_id=peer(r), device_id_type=MESH_ID)
            for r in range(1, N_DEV) for a in range(n)]

    def arrivals():
        return [pltpu.make_async_remote_copy(
            src_ref=g_refs[a].at[peer_id(r)], dst_ref=out_refs[a].at[peer_id(r)], send_sem=send_sems.at[a, r - 1],
            recv_sem=recv_sems.at[a, r - 1], device_id=peer(r), device_id_type=MESH_ID)
            for r in range(1, N_DEV) for a in range(n)]

    def mine():
        return [pltpu.make_async_copy(g_refs[a].at[me_id], out_refs[a].at[me_id], local_sems.at[a]) for a in range(n)]

    def start():
        for cp in mine() + copies():
            cp.start()

    def finish():
        for cp in arrivals():
            cp.wait_recv()
        for cp in copies():
            cp.wait_send()
        for cp in mine():
            cp.wait()

    return start, finish


def _scatter_extras(blocks):
    n = len(blocks)
    anyspace = pl.BlockSpec(memory_space=pl.ANY)
    return ([anyspace] * n, [anyspace] * n, [jax.ShapeDtypeStruct(b.shape, b.dtype) for b in blocks],
            [pltpu.SemaphoreType.DMA((n, 7)), pltpu.SemaphoreType.DMA((n, 7)), pltpu.SemaphoreType.DMA((n,))])


def _sum_slots(x, *, name):
    _, r, c = x.shape
    tr = _tile(r, ROW_TILE)

    def body(x_ref, o_ref):
        acc = x_ref[0].astype(F32)
        for s in range(1, N_DEV):
            acc = acc + x_ref[s].astype(F32)
        o_ref[...] = acc

    return pl.pallas_call(
        body, name=name, grid=(r // tr,), in_specs=[pl.BlockSpec((N_DEV, tr, c), lambda i: (0, i, 0))],
        out_specs=pl.BlockSpec((tr, c), lambda i: (i, 0)), out_shape=jax.ShapeDtypeStruct((r, c), F32),
        compiler_params=_cparams("parallel"),
    )(x)


SB_PAIRS = SB_HEADS // 2
SB_PAIR_QBLOCK = 512
SB_PAIR_QBLOCK_FWD = 512


def _sb_pair_blocks(t, pref=SB_PAIR_QBLOCK):
    bq = _tile(t, pref)
    return bq, _tile(bq, SB_KBLOCK)


def _sb_pair_specs(t, bq):
    base = C_SBQKV // 128
    return [pl.BlockSpec((bq, 128), lambda p, i: (i, base + p)),
            pl.BlockSpec((t, 128), lambda p, i: (0, base + SB_PAIRS + p)),
            pl.BlockSpec((t, 128), lambda p, i: (0, base + 2 * SB_PAIRS + p))]


def _halves(x, first):
    zero = jnp.zeros_like(x)
    return [jnp.where(first, x, zero), jnp.where(first, zero, x)]


def _sb_mask(qi, kb, bq, bk):
    t_idx = qi * bq + lax.broadcasted_iota(jnp.int32, (bq, bk), 0)
    s_idx = kb * bk + lax.broadcasted_iota(jnp.int32, (bq, bk), 1)
    return s_idx < t_idx


SB_SCALE = SB_DIM ** -0.5


def _sb_pair_scores(qh, kblk, mask):
    zs = _dot_each(qh, [kblk, kblk], 1, 1)
    es = [jnp.exp(-jnp.abs(z)) for z in zs]
    sps = [jnp.maximum(z, 0.0) + jnp.log(1.0 + e) for z, e in zip(zs, es)]
    if mask is not None:
        sps = [jnp.where(mask, sp, 0.0) for sp in sps]
    return zs, es, sps


def _sb_atts(zs, csums, laters, mask):
    atts = [jnp.exp(z - c - l) for z, c, l in zip(zs, csums, laters)]
    return atts if mask is None else [jnp.where(mask, a, 0.0) for a in atts]


def _scaled_queries(q_ref, first):
    return _halves(q_ref[...] * jnp.asarray(SB_SCALE, q_ref.dtype), first)


def _split_dot_each(x_list, m):
    parts = [_split2(x) for x in x_list]
    prods = [(_dot(hi, m), _dot(lo, m)) for hi, lo in parts]
    return [a + b for a, b in prods]


def _sb_pair_fwd(proj, shards=(), *, name):
    t = proj.shape[0]
    bq, bk = _sb_pair_blocks(t, SB_PAIR_QBLOCK_FWD)
    n = len(shards)
    nq = t // bq
    nsteps = SB_PAIRS * nq

    def body(q_ref, k_ref, v_ref, *rest):
        o_ref = rest[n]
        qi = pl.program_id(1)
        if n:
            step_no = pl.program_id(0) * nq + qi
            start, pass_on, finish = _gather_phases(rest[:n], rest[n + 1:2 * n + 1], *rest[2 * n + 1:])
            pl.when(step_no == 0)(start)
            pl.when(step_no == (2 * nsteps) // 3)(pass_on)
        first = lax.broadcasted_iota(jnp.int32, (1, 128), 1) < SB_DIM
        qh = _scaled_queries(q_ref, first)
        suffix = _suffix_ones(bk)
        band = bq // bk
        nkb = (qi + 1) * band

        def make_step(masked):
            def step(it, carry):
                later0, later1, acc = carry
                kb = nkb - 1 - it
                rows = pl.ds(pl.multiple_of(kb * bk, bk), bk)
                mask = _sb_mask(qi, kb, bq, bk) if masked else None
                zs, _, sps = _sb_pair_scores(qh, k_ref[rows, :], mask)
                atts = _sb_atts(zs, _split_dot_each(sps, suffix), (later0, later1), mask)
                outs = _dot_each(atts, _halves(v_ref[rows, :], first))
                return (later0 + jnp.sum(sps[0], axis=1, keepdims=True),
                        later1 + jnp.sum(sps[1], axis=1, keepdims=True), acc + (outs[0] + outs[1]))
            return step

        zero = jnp.zeros((bq, 1), F32)
        carry = lax.fori_loop(0, band, make_step(True), (zero, zero, jnp.zeros((bq, 128), F32)))
        _, _, acc = lax.fori_loop(band, nkb, make_step(False), carry)
        o_ref[...] = acc.astype(o_ref.dtype)
        if n:
            pl.when(step_no == nsteps - 1)(finish)

    more_in, more_out, more_shapes, sems = _gather_extras(shards) if n else ([], [], [], [])
    res = pl.pallas_call(
        body, name=name, grid=(SB_PAIRS, nq), in_specs=_sb_pair_specs(t, bq) + more_in,
        out_specs=[pl.BlockSpec((bq, 128), lambda p, i: (i, p))] + more_out,
        out_shape=[jax.ShapeDtypeStruct((t, SB_W), BF16)] + more_shapes, scratch_shapes=sems,
        compiler_params=_cparams("arbitrary", "arbitrary"),
    )(proj, proj, proj, *shards)
    return res[0], list(res[1:])


def _sb_pair_bwd(proj, dy, blocks=(), *, name):
    t = proj.shape[0]
    bq, bk = _sb_pair_blocks(t)
    nq = t // bq
    n = len(blocks)

    def body(q_ref, k_ref, v_ref, do_ref, *rest):
        dq_ref, dk_ref, dv_ref = rest[n:n + 3]
        dl_keep, sig_keep, dk_acc, dv_acc = rest[2 * n + 3:2 * n + 7]
        qi = pl.program_id(1)
        if n:
            step_no = pl.program_id(0) * nq + qi
            start, finish = _scatter_phases(rest[:n], rest[n + 3:2 * n + 3], *rest[2 * n + 7:])
            pl.when(step_no == 0)(start)

        @pl.when(qi == 0)
        def _():
            dk_acc[...] = jnp.zeros_like(dk_acc)
            dv_acc[...] = jnp.zeros_like(dv_acc)

        first = lax.broadcasted_iota(jnp.int32, (1, 128), 1) < SB_DIM
        qh = _scaled_queries(q_ref, first)
        doh = _halves(do_ref[...], first)
        suffix = _suffix_ones(bk)
        prefix = _prefix_ones(bk)
        band = bq // bk
        nkb = (qi + 1) * band

        def make_back(masked):
            def back(it, carry):
                kb = nkb - 1 - it
                rows = pl.ds(pl.multiple_of(kb * bk, bk), bk)
                vblk = v_ref[rows, :]
                mask = _sb_mask(qi, kb, bq, bk) if masked else None
                zs, es, sps = _sb_pair_scores(qh, k_ref[rows, :], mask)
                atts = _sb_atts(zs, _split_dot_each(sps, suffix), carry, mask)
                dvs = _dot_each(atts, doh, 0, 0)
                datts = _dot_each(doh, [vblk, vblk], 1, 1)
                dv_acc[rows, :] += dvs[0] + dvs[1]
                for hh in range(2):
                    sig = jnp.where(zs[hh] >= 0, 1.0, es[hh]) * pl.reciprocal(1.0 + es[hh], approx=True)
                    dl_keep[hh, kb] = atts[hh] * datts[hh]
                    sig_keep[hh, kb] = (sig if mask is None else jnp.where(mask, sig, 0.0)).astype(sig_keep.dtype)
                return tuple(l + jnp.sum(sp, axis=1, keepdims=True) for l, sp in zip(carry, sps))
            return back

        zero = jnp.zeros((bq, 1), F32)
        lax.fori_loop(band, nkb, make_back(False), lax.fori_loop(0, band, make_back(True), (zero, zero)))

        def forth(kb, carry):
            before0, before1, dq = carry
            rows = pl.ds(pl.multiple_of(kb * bk, bk), bk)
            dls = [dl_keep[hh, kb] for hh in range(2)]
            sums = _split_dot_each(dls, prefix)
            dzs = [dl - sig_keep[hh, kb].astype(F32) * (b + s)
                   for hh, (dl, b, s) in enumerate(zip(dls, (before0, before1), sums))]
            dks = _dot_each(dzs, qh, 0, 0)
            dqs = _dot_each(dzs, _halves(k_ref[rows, :], first))
            dk_acc[rows, :] += dks[0] + dks[1]
            return (before0 + jnp.sum(dls[0], axis=1, keepdims=True), before1 + jnp.sum(dls[1], axis=1, keepdims=True),
                    dq + (dqs[0] + dqs[1]))

        _, _, dq = lax.fori_loop(0, nkb, forth, (zero, zero, jnp.zeros((bq, 128), F32)))
        dq_ref[...] = (dq * SB_SCALE).astype(dq_ref.dtype)

        @pl.when(qi == nq - 1)
        def _():
            dk_ref[...] = dk_acc[...].astype(dk_ref.dtype)
            dv_ref[...] = dv_acc[...].astype(dv_ref.dtype)

        if n:
            pl.when(step_no == SB_PAIRS * nq - 1)(finish)

    qspec = pl.BlockSpec((bq, 128), lambda p, i: (i, p))
    kvspec = pl.BlockSpec((t, 128), lambda p, i: (0, p))
    shape = jax.ShapeDtypeStruct((t, SB_W), BF16)
    more_in, more_out, more_shapes, sems = _scatter_extras(blocks) if n else ([], [], [], [])
    res = pl.pallas_call(
        body, name=name, grid=(SB_PAIRS, nq), in_specs=_sb_pair_specs(t, bq) + [qspec] + more_in,
        out_specs=[qspec, kvspec, kvspec] + more_out, out_shape=[shape] * 3 + more_shapes,
        scratch_shapes=[pltpu.VMEM((2, t // bk, bq, bk), F32), pltpu.VMEM((2, t // bk, bq, bk), BF16),
                        pltpu.VMEM((t, 128), F32), pltpu.VMEM((t, 128), F32)] + sems,
        compiler_params=_cparams("arbitrary", "arbitrary"),
    )(proj, proj, proj, dy, *blocks)
    return res[0], res[1], res[2], list(res[3:])


def _heads_major(cols, heads, dim):
    t = cols.shape[0]
    return cols.reshape(t, heads, dim).transpose(1, 0, 2)


def _heads_minor(x):
    h, t, dim = x.shape
    return x.transpose(1, 0, 2).reshape(t, h * dim)


def _sb_qkv(proj):
    return [_heads_major(proj[:, C_SBQKV + i * SB_W:C_SBQKV + (i + 1) * SB_W], SB_HEADS, SB_DIM) for i in range(3)]


def _relu2_epilogue(r):
    a = jnp.maximum(r, 0.0)
    return r, a * a


def _relu2_bwd_epilogue(r, a):
    return (r * 2.0 * jnp.maximum(a.astype(F32), 0.0),)


def _layer_fwd(x, p, next_shards=()):
    h = _norm_fwd(x, p["norm_mix_pre"], out_dtype=BF16, name="norm_pre_fwd")
    proj = _mm(h, p["w_main"], name="mm_in")
    ab = _mm(h, p["w_ab"], out_dtypes=(F32,), name="mm_ab")
    qkv = _gdn_pre_fwd(proj, p["conv_qkv_w"], name="gdn_pre_fwd")
    a_log, dt_bias = p["gdn_a_log"].reshape(1, GDN_HEADS), p["gdn_dt_bias"].reshape(1, GDN_HEADS)
    u, w, qd, kd, aqk, gl = _gdn_local_fwd_staged(qkv, ab, a_log, dt_bias, name="gdn_local_fwd")
    o_gdn, states = _gdn_scan_fwd_staged(u, w, qd, kd, aqk, gl, name="gdn_scan_fwd")
    y_a = _gdn_post_fwd(o_gdn, proj, p["gdn_norm_w"], name="gdn_post_fwd")
    y_b, gathered = _sb_pair_fwd(proj, next_shards, name="sb_fwd")
    y_c = _sc_fwd(proj, p["conv_sc_w"], name="sc_fwd")
    ys = (y_a, y_b, y_c)
    ps = tuple(_mm(ys[b], p["w_branch"][b], name="mm_branch") for b in range(3))
    merged = _merge_fwd(ps, proj, name="merge_fwd")
    mo = _mm(merged, p["w_out"], out_dtypes=(F32,), name="mm_out")
    x1 = _norm_fwd(mo, p["norm_mix_post"], x, out_dtype=F32, name="norm_post_fwd")
    h2 = _norm_fwd(x1, p["norm_ffn_pre"], out_dtype=BF16, name="norm_pre_fwd")
    a1, r1 = _mm(h2, p["w_ff1"], out_dtypes=(BF16, BF16), epi=_relu2_epilogue, name="mm_ff1")
    f = _mm(r1, p["w_ff2"], out_dtypes=(F32,), name="mm_ff2")
    x2 = _norm_fwd(f, p["norm_ffn_post"], x1, out_dtype=F32, name="norm_post_fwd")
    saved = dict(x=x, h=h, proj=proj, ab=ab, qkv=qkv, u=u, w=w, qd=qd, kd=kd, aqk=aqk, gl=gl, o_gdn=o_gdn,
                 states=states, ys=ys, ps=ps, merged=merged, mo=mo, x1=x1, h2=h2,
                 a1=a1, r1=r1, f=f)
    return x2, saved, gathered


def _layer_bwd(dx2, p, s, blocks=()):
    g = {}
    df, g["norm_ffn_post"] = _norm_bwd(s["f"], p["norm_ffn_post"], dx2, out_dtype=BF16, name="norm_bwd_b")
    da1 = _mm(df, p["w_ff2"], tb=True, epi=_relu2_bwd_epilogue, extras=(s["a1"],), name="mm_ff2_dx")
    g["w_ff2"] = _mm(s["r1"], df, ta=True, name="mm_ff2_dw")
    g["w_ff1"] = _mm(s["h2"], da1, ta=True, name="mm_ff1_dw")
    dh2 = _mm(da1, p["w_ff1"], tb=True, out_dtypes=(F32,), name="mm_ff1_dx")
    dx1, g["norm_ffn_pre"] = _norm_bwd(s["x1"], p["norm_ffn_pre"], dh2, dx2, out_dtype=F32, name="norm_bwd_f")
    dmo, g["norm_mix_post"] = _norm_bwd(s["mo"], p["norm_mix_post"], dx1, out_dtype=BF16, name="norm_bwd_b")
    dmerged = _mm(dmo, p["w_out"], tb=True, name="mm_out_dx")
    g["w_out"] = _mm(s["merged"], dmo, ta=True, name="mm_out_dw")
    dps, dgates = _merge_bwd(s["ps"], s["proj"], dmerged, name="merge_bwd")
    dys = [_mm(dps[b], p["w_branch"][b], tb=True, name="mm_branch_dx") for b in range(3)]
    g["w_branch"] = jnp.stack([_mm(s["ys"][b], dps[b], ta=True, name="mm_branch_dw") for b in range(3)])
    dscx, dscb, dscc, g["conv_sc_w"] = _sc_bwd(s["proj"], p["conv_sc_w"], dys[2], name="sc_bwd")
    dsq, dsk, dsv, received = _sb_pair_bwd(s["proj"], dys[1], blocks, name="sb_bwd")
    a_log, dt_bias = p["gdn_a_log"].reshape(1, GDN_HEADS), p["gdn_dt_bias"].reshape(1, GDN_HEADS)
    do_gdn, dggate, g["gdn_norm_w"] = _gdn_post_bwd(s["o_gdn"], s["proj"], p["gdn_norm_w"], dys[0], name="gdn_post_bwd")
    du, dw, dqd, dkd, daqk, dgl = _gdn_scan_bwd_staged(s["u"], s["w"], s["qd"], s["kd"], s["aqk"], s["gl"],
                                                       s["states"], do_gdn, name="gdn_scan_bwd")
    dq, dk, dv, dab_h, dsc = _gdn_local_bwd_staged(s["qkv"], s["ab"], a_log, dt_bias, du, dw, dqd, dkd, daqk, dgl,
                                                   name="gdn_local_bwd")
    dsc = jnp.sum(dsc, axis=(1, 2))
    g["gdn_a_log"], g["gdn_dt_bias"] = dsc[:, 0], dsc[:, 1]
    dqkv = jnp.concatenate([dq, dk, dv], axis=0)
    dgqkv, g["conv_qkv_w"] = _gdn_pre_bwd(s["proj"], p["conv_qkv_w"], dqkv, name="gdn_pre_bwd")
    dab = jnp.sum(dab_h, axis=0).astype(BF16)
    dproj = jnp.concatenate([dgqkv, dggate, dsq, dsk, dsv, dscx, dscb, dscc, dgates], axis=1)
    g["w_main"] = _mm(s["h"], dproj, ta=True, name="mm_in_dw")
    g["w_ab"] = _mm(s["h"], dab, ta=True, out_dtypes=(F32,), name="mm_ab_dw")
    dh = _mm(dproj, p["w_main"], tb=True, out_dtypes=(F32,), name="mm_in_dx")
    dh_ab = _mm(dab, p["w_ab"], tb=True, out_dtypes=(F32,), name="mm_ab_dx")
    dx, g["norm_mix_pre"] = _norm_bwd(s["x"], p["norm_mix_pre"], dh + dh_ab, dx1, out_dtype=F32, name="norm_bwd_f")
    return dx, g, received


NORMS = ("norm_mix_pre", "norm_mix_post", "norm_ffn_pre", "norm_ffn_post")
SMALL = NORMS + ("gdn_a_log", "gdn_dt_bias", "gdn_norm_w")
CONVS = ("conv_qkv_w", "conv_sc_w")
AB_LO = 2048


def _split_w_in(w_in):
    main = jnp.concatenate([w_in[..., :AB_LO], w_in[..., AB_LO + 2 * GDN_HEADS:]], axis=-1)
    ab = w_in[..., AB_LO:AB_LO + 2 * GDN_HEADS]
    pad = [(0, 0)] * (ab.ndim - 1) + [(0, AB_W - 2 * GDN_HEADS)]
    return main, jnp.pad(ab, pad)


def _join_w_in(main, ab):
    return jnp.concatenate([main[..., :AB_LO], ab[..., :2 * GDN_HEADS].astype(main.dtype), main[..., AB_LO:]], axis=-1)


def kernel(x, norm_mix_pre, w_in, conv_qkv_w, gdn_a_log, gdn_dt_bias, gdn_norm_w, conv_sc_w, w_branch, w_out, norm_mix_post, norm_ffn_pre, w_ff1, w_ff2, norm_ffn_post, loss_target, m_norm_mix_pre, m_w_in, m_conv_qkv_w, m_gdn_a_log, m_gdn_dt_bias, m_gdn_norm_w, m_conv_sc_w, m_w_branch, m_w_out, m_norm_mix_post, m_norm_ffn_pre, m_w_ff1, m_w_ff2, m_norm_ffn_post, v_norm_mix_pre, v_w_in, v_conv_qkv_w, v_gdn_a_log, v_gdn_dt_bias, v_gdn_norm_w, v_conv_sc_w, v_w_branch, v_w_out, v_norm_mix_post, v_norm_ffn_pre, v_w_ff1, v_w_ff2, v_norm_ffn_post):
    names = ("norm_mix_pre", "w_in", "conv_qkv_w", "gdn_a_log", "gdn_dt_bias", "gdn_norm_w", "conv_sc_w", "w_branch",
             "w_out", "norm_mix_post", "norm_ffn_pre", "w_ff1", "w_ff2", "norm_ffn_post")
    w = dict(zip(names, (norm_mix_pre, w_in, conv_qkv_w, gdn_a_log, gdn_dt_bias, gdn_norm_w, conv_sc_w, w_branch,
                         w_out, norm_mix_post, norm_ffn_pre, w_ff1, w_ff2, norm_ffn_post)))
    m = dict(zip(names, (m_norm_mix_pre, m_w_in, m_conv_qkv_w, m_gdn_a_log, m_gdn_dt_bias, m_gdn_norm_w, m_conv_sc_w,
                         m_w_branch, m_w_out, m_norm_mix_post, m_norm_ffn_pre, m_w_ff1, m_w_ff2, m_norm_ffn_post)))
    v = dict(zip(names, (v_norm_mix_pre, v_w_in, v_conv_qkv_w, v_gdn_a_log, v_gdn_dt_bias, v_gdn_norm_w, v_conv_sc_w,
                         v_w_branch, v_w_out, v_norm_mix_post, v_norm_ffn_pre, v_w_ff1, v_w_ff2, v_norm_ffn_post)))
    me = 4 * lax.axis_index("x") + 2 * lax.axis_index("y") + lax.axis_index("c")

    conv_shapes = [w[k].shape for k in CONVS]
    conv_all, = _all_gather([_pack_vec([w[k] for k in CONVS])], name="gather_small")
    convs = {k: _to_global(blk, 2) for k, blk in zip(CONVS, _unpack_vec(conv_all, conv_shapes))}
    shards = [[w[k][l].astype(BF16) for k in BIG] for l in range(DEPTH)]

    def layer_params(l, gathered):
        p = {k: _to_global(blk, BIG_AXIS[k] - 1) for k, blk in zip(BIG, gathered)}
        p["w_main"], p["w_ab"] = _split_w_in(p.pop("w_in"))
        p.update({k: convs[k][l] for k in CONVS})
        p.update({k: w[k][l] for k in SMALL})
        return p

    xs = x[0]
    gathered = _all_gather(shards[0], name="gather_weights")
    layers, saved = [], []
    for l in range(DEPTH):
        layers.append(layer_params(l, gathered))
        xs, s, gathered = _layer_fwd(xs, layers[l], shards[l + 1] if l + 1 < DEPTH else ())
        saved.append(s)
    dy, loss_lanes = _loss_head(xs, loss_target[0], name="loss_head")

    grads, big_sums, blocks = [None] * DEPTH, [None] * DEPTH, ()
    for l in reversed(range(DEPTH)):
        dy, g, received = _layer_bwd(dy, layers[l], saved[l], blocks)
        if received:
            big_sums[l + 1] = [_sum_slots(r, name="rs_sum_slots") for r in received]
        g["w_in"] = _join_w_in(g.pop("w_main"), g.pop("w_ab"))
        blocks = [_rows(_to_blocks(g[k], BIG_AXIS[k] - 1)) for k in BIG]
        grads[l] = g
    got = _exchange_sibling(blocks, name="rs_sibling")
    parts = [_pair_sum(b, r, name="rs_pair_sum") for b, r in zip(blocks, got)]
    got2 = _exchange_chips(parts, name="rs_chips")
    big_sums[0] = [_final_sum(p, r, name="rs_final_sum") for p, r in zip(parts, got2)]
    gsum = {k: jnp.stack([big_sums[l][i] for l in range(DEPTH)]).reshape(w[k].shape) for i, k in enumerate(BIG)}
    stack = {k: jnp.stack([g[k] for g in grads]) for k in SMALL + CONVS}

    small_parts = [stack[k] for k in SMALL + CONVS] + [jnp.sum(loss_lanes).reshape(1)]
    small_shapes = [stack[k].shape for k in SMALL + CONVS] + [(1,)]
    summed = _sum_devices(_all_gather([_pack_vec(small_parts)], name="gather_small_grads")[0], name="sum_small")
    small = _unpack_vec(summed, small_shapes)
    loss = small[-1][0]
    for k, val in zip(SMALL + CONVS, small[:-1]):
        gsum[k] = val
    for k in CONVS:
        per = gsum[k].shape[2] // N_DEV
        gsum[k] = lax.dynamic_slice_in_dim(gsum[k], me * per, per, axis=2)

    delta, new_m, new_v = {}, {}, {}
    for k in names:
        shp = w[k].shape
        two_d = (-1, shp[-1]) if len(shp) > 1 else (1, -1)
        d_, m_, v_ = _adamw(w[k].reshape(two_d), gsum[k].reshape(two_d), m[k].reshape(two_d), v[k].reshape(two_d),
                            name="adamw")
        delta[k], new_m[k], new_v[k] = d_.reshape(shp), m_.reshape(shp), v_.reshape(shp)

    return (loss, dy[None], *[gsum[k].reshape(w[k].shape) for k in names], *[delta[k] for k in names], *[new_m[k] for k in names],
            *[new_v[k] for k in names])
```

```python
import functools

import jax
import jax.numpy as jnp
from jax import lax
from jax.experimental import pallas as pl
from jax.experimental.pallas import tpu as pltpu

F32, BF16 = jnp.float32, jnp.bfloat16
MESH_ID = pl.DeviceIdType.MESH

N_DEV = 8
DEPTH = 4
D_MODEL = 1024
D_FF = 4096
EPS = 1e-6
GDN_HEADS, GDN_DIM, GDN_CONV = 4, 128, 4
GDN_W = GDN_HEADS * GDN_DIM
CHUNK = 64
SB_HEADS, SB_DIM = 8, 64
SB_W = SB_HEADS * SB_DIM
SB_QBLOCK, SB_KBLOCK = 512, 256
SC_W, SC_CONV = 512, 3
IN_W = 8200
C_GQKV, C_GGATE, C_SBQKV, C_SCX, C_SCB, C_SCC, C_GATES, MAIN_W = 0, 1536, 2048, 3584, 4096, 4608, 5120, 8192
AB_W = 128

ADAM_LR, ADAM_B1, ADAM_B2, ADAM_EPS, ADAM_WD, ADAM_STEP = 0.001, 0.9, 0.999, 1e-08, 0.01, 10

VMEM_LIMIT = 48 * 2 ** 20


def _cparams(*sem):
    return pltpu.CompilerParams(dimension_semantics=sem or None, vmem_limit_bytes=VMEM_LIMIT)


def _tile(n, pref):
    if n <= pref:
        return n
    t = pref
    while n % t:
        t -= 128
    assert t > 0
    return t


def _dot(a, b, ca=1, cb=0):
    return lax.dot_general(a.astype(BF16), b.astype(BF16), (((ca,), (cb,)), ((), ())), preferred_element_type=F32)


def _split2(x):
    hi = x.astype(BF16)
    return hi, (x - hi.astype(F32)).astype(BF16)


def _dot3(a, b, ca=1, cb=0):
    a1, a2 = _split2(a)
    b1, b2 = _split2(b)
    return _dot(a1, b1, ca, cb) + (_dot(a1, b2, ca, cb) + _dot(a2, b1, ca, cb))


def _dot_exact(a, b, ca=1, cb=0, ones="a"):
    x = b if ones == "a" else a
    m = (a if ones == "a" else b).astype(BF16)
    hi, rest = x.astype(BF16), None
    rest = x - hi.astype(F32)
    mid = rest.astype(BF16)
    lo = (rest - mid.astype(F32)).astype(BF16)
    parts = [_dot(m, p, ca, cb) if ones == "a" else _dot(p, m, ca, cb) for p in (hi, mid, lo)]
    return parts[0] + (parts[1] + parts[2])


def _sigmoid(z):
    e = jnp.exp(-jnp.abs(z))
    return jnp.where(z >= 0, 1.0, e) / (1.0 + e)


def _softplus(z):
    return jnp.maximum(z, 0.0) + jnp.log(1.0 + jnp.exp(-jnp.abs(z)))


def _mm(a, b, *, name, ta=False, tb=False, out_dtypes=(BF16,), epi=None, extras=()):
    assert a.dtype == BF16 and b.dtype == BF16
    m, k = (a.shape[1], a.shape[0]) if ta else a.shape
    n = b.shape[0] if tb else b.shape[1]
    assert (b.shape[1] if tb else b.shape[0]) == k
    tm, tn, tk = _tile(m, 1024), _tile(n, 1024), _tile(k, 1024)
    nk = k // tk
    ca, cb = (0 if ta else 1), (1 if tb else 0)
    n_ex, n_out = len(extras), len(out_dtypes)

    def body(*refs):
        a_ref, b_ref = refs[0], refs[1]
        ex = refs[2:2 + n_ex]
        outs = refs[2 + n_ex:2 + n_ex + n_out]
        acc = refs[-1]
        kk = pl.program_id(2)
        part = lax.dot_general(a_ref[...], b_ref[...], (((ca,), (cb,)), ((), ())), preferred_element_type=F32)

        def finish(r):
            vals = (r,) if epi is None else epi(r, *[e[...] for e in ex])
            for o, v in zip(outs, vals):
                o[...] = v.astype(o.dtype)

        if nk == 1:
            finish(part)
        else:
            @pl.when(kk == 0)
            def _():
                acc[...] = part

            @pl.when(kk > 0)
            def _():
                acc[...] += part

            @pl.when(kk == nk - 1)
            def _():
                finish(acc[...])

    a_spec = pl.BlockSpec((tk, tm), lambda i, j, kk: (kk, i)) if ta else pl.BlockSpec((tm, tk), lambda i, j, kk: (i, kk))
    b_spec = pl.BlockSpec((tn, tk), lambda i, j, kk: (j, kk)) if tb else pl.BlockSpec((tk, tn), lambda i, j, kk: (kk, j))
    io_spec = pl.BlockSpec((tm, tn), lambda i, j, kk: (i, j))
    res = pl.pallas_call(
        body, name=name, grid=(m // tm, n // tn, nk),
        in_specs=[a_spec, b_spec] + [io_spec] * n_ex,
        out_specs=[io_spec] * n_out,
        out_shape=[jax.ShapeDtypeStruct((m, n), dt) for dt in out_dtypes],
        scratch_shapes=[pltpu.VMEM((tm, tn) if nk > 1 else (8, 128), F32)],
        compiler_params=_cparams("parallel", "parallel", "arbitrary"),
    )(a, b, *extras)
    return res[0] if n_out == 1 else res


ROW_TILE = 512


def _norm_fwd(y, w, res=None, *, out_dtype, name):
    t, d = y.shape
    tm = _tile(t, ROW_TILE)
    has_res = res is not None

    def body(*refs):
        y_ref, w_ref = refs[0], refs[1]
        o_ref = refs[-1]
        yv = y_ref[...]
        r = lax.rsqrt(jnp.mean(yv * yv, axis=-1, keepdims=True) + EPS)
        out = yv * r * w_ref[...]
        if has_res:
            out = out + refs[2][...]
        o_ref[...] = out.astype(o_ref.dtype)

    row = pl.BlockSpec((tm, d), lambda i: (i, 0))
    vec = pl.BlockSpec((1, d), lambda i: (0, 0))
    args = (y, w.reshape(1, d)) + ((res,) if has_res else ())
    return pl.pallas_call(
        body, name=name, grid=(t // tm,), in_specs=[row, vec] + [row] * has_res, out_specs=row,
        out_shape=jax.ShapeDtypeStruct((t, d), out_dtype), compiler_params=_cparams("parallel"),
    )(*args)


def _norm_bwd(y, w, dout, add=None, *, out_dtype, name):
    t, d = y.shape
    tm = _tile(t, ROW_TILE)
    has_add = add is not None

    def body(*refs):
        y_ref, w_ref, do_ref = refs[0], refs[1], refs[2]
        dy_ref, dw_ref = refs[-2], refs[-1]
        yv = y_ref[...]
        r = lax.rsqrt(jnp.mean(yv * yv, axis=-1, keepdims=True) + EPS)
        yh = yv * r
        dov = do_ref[...].astype(F32)
        gw = dov * w_ref[...]
        dy = r * (gw - yh * jnp.mean(gw * yh, axis=-1, keepdims=True))
        if has_add:
            dy = dy + refs[3][...]
        dy_ref[...] = dy.astype(dy_ref.dtype)
        part = jnp.sum(dov * yh, axis=0, keepdims=True)

        @pl.when(pl.program_id(0) == 0)
        def _():
            dw_ref[...] = part

        @pl.when(pl.program_id(0) > 0)
        def _():
            dw_ref[...] += part

    row = pl.BlockSpec((tm, d), lambda i: (i, 0))
    vec = pl.BlockSpec((1, d), lambda i: (0, 0))
    args = (y, w.reshape(1, d), dout) + ((add,) if has_add else ())
    return pl.pallas_call(
        body, name=name, grid=(t // tm,), in_specs=[row, vec, row] + [row] * has_add, out_specs=[row, vec],
        out_shape=[jax.ShapeDtypeStruct((t, d), out_dtype), jax.ShapeDtypeStruct((1, d), F32)],
        compiler_params=_cparams("arbitrary"),
    )(*args)


def _shift_down(u, s):
    if s == 0:
        return u
    rows = lax.broadcasted_iota(jnp.int32, u.shape, 0)
    return jnp.where(rows >= s, pltpu.roll(u, s, 0), 0.0)


def _shift_up(u, s):
    if s == 0:
        return u
    t = u.shape[0]
    rows = lax.broadcasted_iota(jnp.int32, u.shape, 0)
    return jnp.where(rows < t - s, pltpu.roll(u, t - s, 0), 0.0)


def _conv_fwd(u, w):
    kk = w.shape[0]
    out = u * w[kk - 1:kk, :]
    for i in range(kk - 1):
        out = out + _shift_down(u, kk - 1 - i) * w[i:i + 1, :]
    return out


def _conv_bwd(u, w, dc):
    kk = w.shape[0]
    du = dc * w[kk - 1:kk, :]
    dws = []
    for i in range(kk):
        s = kk - 1 - i
        if s:
            du = du + _shift_up(dc, s) * w[i:i + 1, :]
        dws.append(jnp.sum(dc * _shift_down(u, s), axis=0, keepdims=True))
    return du, dws


def _gdn_pre_math(x, w, slab):
    c = _conv_fwd(x, w)
    sig = _sigmoid(c)
    s = c * sig
    r = lax.rsqrt(jnp.sum(s * s, axis=-1, keepdims=True) + EPS)
    scale = jnp.where(slab < GDN_HEADS, GDN_DIM ** -0.5, 1.0)
    return c, sig, s, r, scale


def _gdn_pre_fwd(proj, conv_w, *, name):
    t = proj.shape[0]
    nslab = 3 * GDN_HEADS

    def body(x_ref, w_ref, o_ref):
        slab = pl.program_id(0)
        _, _, s, r, scale = _gdn_pre_math(x_ref[...].astype(F32), w_ref[...], slab)
        o_ref[0] = jnp.where(slab < 2 * GDN_HEADS, s * r * scale, s)

    return pl.pallas_call(
        body, name=name, grid=(nslab,),
        in_specs=[pl.BlockSpec((t, GDN_DIM), lambda j: (0, j)), pl.BlockSpec((GDN_CONV, GDN_DIM), lambda j: (0, j))],
        out_specs=pl.BlockSpec((1, t, GDN_DIM), lambda j: (j, 0, 0)),
        out_shape=jax.ShapeDtypeStruct((nslab, t, GDN_DIM), F32), compiler_params=_cparams("parallel"),
    )(proj, conv_w)


def _gdn_pre_bwd(proj, conv_w, dqkv, *, name):
    t = proj.shape[0]
    nslab = 3 * GDN_HEADS

    def body(x_ref, w_ref, d_ref, dx_ref, dw_ref):
        slab = pl.program_id(0)
        x = x_ref[...].astype(F32)
        w = w_ref[...]
        c, sig, s, r, scale = _gdn_pre_math(x, w, slab)
        dout = d_ref[0]
        yn = s * r
        dn = dout * scale
        ds_norm = r * (dn - yn * jnp.sum(dn * yn, axis=-1, keepdims=True))
        ds = jnp.where(slab < 2 * GDN_HEADS, ds_norm, dout)
        dc = ds * (sig + c * sig * (1.0 - sig))
        dx, dws = _conv_bwd(x, w, dc)
        dx_ref[...] = dx.astype(dx_ref.dtype)
        for i, dwi in enumerate(dws):
            dw_ref[i:i + 1, :] = dwi

    return pl.pallas_call(
        body, name=name, grid=(nslab,),
        in_specs=[pl.BlockSpec((t, GDN_DIM), lambda j: (0, j)), pl.BlockSpec((GDN_CONV, GDN_DIM), lambda j: (0, j)),
                  pl.BlockSpec((1, t, GDN_DIM), lambda j: (j, 0, 0))],
        out_specs=[pl.BlockSpec((t, GDN_DIM), lambda j: (0, j)), pl.BlockSpec((GDN_CONV, GDN_DIM), lambda j: (0, j))],
        out_shape=[jax.ShapeDtypeStruct((t, 3 * GDN_W), BF16), jax.ShapeDtypeStruct((GDN_CONV, 3 * GDN_W), F32)],
        compiler_params=_cparams("parallel"),
    )(proj, conv_w, dqkv)


def _sc_specs(t):
    def col(base):
        return pl.BlockSpec((t, 128), lambda j: (0, base // 128 + j))
    return [col(C_SCX), col(C_SCB), col(C_SCC), pl.BlockSpec((SC_CONV, 128), lambda j: (0, j))]


def _sc_fwd(proj, conv_w, *, name):
    t = proj.shape[0]

    def body(x_ref, b_ref, c_ref, w_ref, o_ref):
        u = c_ref[...].astype(F32) * x_ref[...].astype(F32)
        o_ref[...] = (b_ref[...].astype(F32) * _conv_fwd(u, w_ref[...])).astype(o_ref.dtype)

    return pl.pallas_call(
        body, name=name, grid=(SC_W // 128,), in_specs=_sc_specs(t),
        out_specs=pl.BlockSpec((t, 128), lambda j: (0, j)),
        out_shape=jax.ShapeDtypeStruct((t, SC_W), BF16), compiler_params=_cparams("parallel"),
    )(proj, proj, proj, conv_w)


def _sc_bwd(proj, conv_w, dy, *, name):
    t = proj.shape[0]
    nj = SC_W // 128

    def body(x_ref, b_ref, c_ref, w_ref, dy_ref, dx_ref, db_ref, dc_ref, dw_ref):
        x, b, c = x_ref[...].astype(F32), b_ref[...].astype(F32), c_ref[...].astype(F32)
        w = w_ref[...]
        u = c * x
        dyv = dy_ref[...].astype(F32)
        db_ref[...] = (dyv * _conv_fwd(u, w)).astype(db_ref.dtype)
        du, dws = _conv_bwd(u, w, dyv * b)
        dx_ref[...] = (du * c).astype(dx_ref.dtype)
        dc_ref[...] = (du * x).astype(dc_ref.dtype)
        for i, dwi in enumerate(dws):
            dw_ref[i:i + 1, :] = dwi

    return pl.pallas_call(
        body, name=name, grid=(nj,),
        in_specs=_sc_specs(t) + [pl.BlockSpec((t, 128), lambda j: (0, j))],
        out_specs=[pl.BlockSpec((t, 128), lambda j: (0, j))] * 3 + [pl.BlockSpec((SC_CONV, 128), lambda j: (0, j))],
        out_shape=[jax.ShapeDtypeStruct((t, SC_W), BF16)] * 3 + [jax.ShapeDtypeStruct((SC_CONV, SC_W), F32)],
        compiler_params=_cparams("parallel"),
    )(proj, proj, proj, conv_w, dy)


def _tri_inv(a_strict):
    c = a_strict.shape[0]
    ri = lax.broadcasted_iota(jnp.int32, (c, c), 0)
    ci = lax.broadcasted_iota(jnp.int32, (c, c), 1)
    eye = (ri == ci).astype(F32)
    blk = 8
    bm = -jnp.where(ri // blk == ci // blk, a_strict, 0.0)
    inv = eye + bm
    pw = bm
    for _ in range(2):
        pw = _dot3(pw, pw)
        inv = inv + _dot3(inv, pw)
    while blk < c:
        off = jnp.where((ri // (2 * blk) == ci // (2 * blk)) & (ri // blk != ci // blk), a_strict, 0.0)
        inv = inv - _dot3(_dot3(inv, off), inv)
        blk *= 2
    return inv


def _gdn_chunk(q, k, v, ab, head, ea, dtb):
    c = q.shape[0]
    lane = lax.broadcasted_iota(jnp.int32, ab.shape, 1)
    a = jnp.sum(jnp.where(lane == head, ab, 0.0), axis=1, keepdims=True)
    b = jnp.sum(jnp.where(lane == GDN_HEADS + head, ab, 0.0), axis=1, keepdims=True)
    ri = lax.broadcasted_iota(jnp.int32, (c, c), 0)
    ci = lax.broadcasted_iota(jnp.int32, (c, c), 1)
    tri, strict = ri >= ci, ri > ci
    ltri = tri.astype(F32)
    beta = _sigmoid(b)
    sig_a = _sigmoid(a + dtb)
    g = -ea * _softplus(a + dtb)
    g_cc = jnp.broadcast_to(g, (c, c))
    gi = _dot_exact(ltri, g_cc)
    gj = _dot_exact(g_cc, (ri <= ci).astype(F32), 0, 0, ones="b")
    decay = jnp.exp(jnp.where(tri, gi - gj, -1e30))
    gc = _dot_exact(ltri, jnp.broadcast_to(g, (c, GDN_DIM)))
    g_tot = jnp.sum(g, axis=0, keepdims=True)
    egc = jnp.exp(gc)
    ekd = jnp.exp(g_tot - gc)
    kb, vb = k * beta, v * beta
    kbg = kb * egc
    mkk = _dot3(kb, k, 1, 1)
    a_kk = jnp.where(strict, mkk * decay, 0.0)
    tinv = _tri_inv(a_kk)
    u = _dot3(tinv, vb)
    w = _dot3(tinv, kbg)
    mqk = _dot3(q, k, 1, 1)
    a_qk = jnp.where(tri, mqk * decay, 0.0)
    return dict(beta=beta, sig_a=sig_a, g=g, decay=decay, egc=egc, ekd=ekd, g_tot=g_tot, kb=kb, vb=vb, kbg=kbg,
                a_kk=a_kk, tinv=tinv, u=u, w=w, a_qk=a_qk, qd=q * egc, kd=k * ekd, tri=tri, strict=strict)


def _chunks_per_step(n):
    return 4 if n % 4 == 0 else 1


def _gdn_local_specs(t, cps):
    rows = cps * CHUNK

    def slab(base):
        return pl.BlockSpec((1, rows, GDN_DIM), lambda h, n: (base + h, n, 0))
    smem = pl.BlockSpec(memory_space=pltpu.SMEM)
    return [slab(0), slab(GDN_HEADS), slab(2 * GDN_HEADS), pl.BlockSpec((rows, AB_W), lambda h, n: (n, 0)), smem, smem]


def _scalar_row(ref, head):
    return jnp.full((1, 1), ref[0, head], F32)


def _gdn_local_fwd(qkv, ab, a_log, dt_bias, *, name):
    t = qkv.shape[1]
    n = t // CHUNK
    cps = _chunks_per_step(n)
    rows = cps * CHUNK

    def body(q_ref, k_ref, v_ref, ab_ref, al_ref, dt_ref, u_ref, w_ref, qd_ref, kd_ref, aqk_ref, gl_ref):
        head = pl.program_id(0)
        ea = jnp.exp(_scalar_row(al_ref, head))
        dtb = _scalar_row(dt_ref, head)
        for j in range(cps):
            sl = slice(j * CHUNK, (j + 1) * CHUNK)
            r = _gdn_chunk(q_ref[0, sl, :], k_ref[0, sl, :], v_ref[0, sl, :], ab_ref[sl, :], head, ea, dtb)
            u_ref[0, sl, :] = r["u"]
            w_ref[0, sl, :] = r["w"]
            qd_ref[0, sl, :] = r["qd"]
            kd_ref[0, sl, :] = r["kd"]
            aqk_ref[0, sl, :] = r["a_qk"]
            gl_ref[0, j] = jnp.exp(jnp.broadcast_to(r["g_tot"], (1, GDN_DIM)))

    big = pl.BlockSpec((1, rows, GDN_DIM), lambda h, i: (h, i, 0))
    big_shape = jax.ShapeDtypeStruct((GDN_HEADS, t, GDN_DIM), F32)
    return pl.pallas_call(
        body, name=name, grid=(GDN_HEADS, n // cps), in_specs=_gdn_local_specs(t, cps),
        out_specs=[big] * 4 + [pl.BlockSpec((1, rows, CHUNK), lambda h, i: (h, i, 0)),
                               pl.BlockSpec((1, cps, 1, GDN_DIM), lambda h, i: (h, i, 0, 0))],
        out_shape=[big_shape] * 4 + [jax.ShapeDtypeStruct((GDN_HEADS, t, CHUNK), F32),
                                     jax.ShapeDtypeStruct((GDN_HEADS, n, 1, GDN_DIM), F32)],
        compiler_params=_cparams("parallel", "parallel"),
    )(qkv, qkv, qkv, ab, a_log, dt_bias)


def _gdn_scan_fwd(u, w, qd, kd, aqk, gl, *, name):
    h, t, _ = u.shape
    n = t // CHUNK

    def body(u_ref, w_ref, qd_ref, kd_ref, aqk_ref, gl_ref, o_ref, s_ref, state):
        @pl.when(pl.program_id(0) == 0)
        def _():
            state[...] = jnp.zeros_like(state)

        for hh in range(h):
            s = state[hh]
            s_ref[hh, 0] = s
            vn = u_ref[hh] - _dot3(w_ref[hh], s)
            o_ref[hh] = _dot3(qd_ref[hh], s) + _dot3(aqk_ref[hh], vn)
            state[hh] = s * gl_ref[hh, 0] + _dot3(kd_ref[hh], vn, 0, 0)

    big = pl.BlockSpec((h, CHUNK, GDN_DIM), lambda i: (0, i, 0))
    return pl.pallas_call(
        body, name=name, grid=(n,),
        in_specs=[big] * 4 + [pl.BlockSpec((h, CHUNK, CHUNK), lambda i: (0, i, 0)),
                              pl.BlockSpec((h, 1, 1, GDN_DIM), lambda i: (0, i, 0, 0))],
        out_specs=[big, pl.BlockSpec((h, 1, GDN_DIM, GDN_DIM), lambda i: (0, i, 0, 0))],
        out_shape=[jax.ShapeDtypeStruct((h, t, GDN_DIM), F32), jax.ShapeDtypeStruct((h, n, GDN_DIM, GDN_DIM), F32)],
        scratch_shapes=[pltpu.VMEM((h, GDN_DIM, GDN_DIM), F32)],
        compiler_params=_cparams("arbitrary"),
    )(u, w, qd, kd, aqk, gl)


def _gdn_scan_bwd(u, w, qd, kd, aqk, gl, states, do, *, name):
    h, t, _ = u.shape
    n = t // CHUNK

    def body(u_ref, w_ref, qd_ref, kd_ref, aqk_ref, gl_ref, s_ref, do_ref,
             du_ref, dw_ref, dqd_ref, dkd_ref, daqk_ref, dgl_ref, dstate):
        @pl.when(pl.program_id(0) == 0)
        def _():
            dstate[...] = jnp.zeros_like(dstate)

        ri = lax.broadcasted_iota(jnp.int32, (CHUNK, CHUNK), 0)
        ci = lax.broadcasted_iota(jnp.int32, (CHUNK, CHUNK), 1)
        for hh in range(h):
            s, ds_next, dov, wv = s_ref[hh, 0], dstate[hh], do_ref[hh], w_ref[hh]
            vn = u_ref[hh] - _dot3(wv, s)
            dvn = _dot3(aqk_ref[hh], dov, 0, 0) + _dot3(kd_ref[hh], ds_next)
            du_ref[hh] = dvn
            dw_ref[hh] = -_dot3(dvn, s, 1, 1)
            dqd_ref[hh] = _dot3(dov, s, 1, 1)
            dkd_ref[hh] = _dot3(vn, ds_next, 1, 1)
            daqk_ref[hh] = jnp.where(ri >= ci, _dot3(dov, vn, 1, 1), 0.0)
            dgl_ref[hh, 0] = jnp.sum(ds_next * s, axis=0, keepdims=True)
            dstate[hh] = (_dot3(qd_ref[hh], dov, 0, 0) + ds_next * gl_ref[hh, 0]
                          - _dot3(wv, dvn, 0, 0))

    big = pl.BlockSpec((h, CHUNK, GDN_DIM), lambda i: (0, n - 1 - i, 0))
    sq = pl.BlockSpec((h, CHUNK, CHUNK), lambda i: (0, n - 1 - i, 0))
    glb = pl.BlockSpec((h, 1, 1, GDN_DIM), lambda i: (0, n - 1 - i, 0, 0))
    big_shape = jax.ShapeDtypeStruct((h, t, GDN_DIM), F32)
    return pl.pallas_call(
        body, name=name, grid=(n,),
        in_specs=[big] * 4 + [sq, glb, pl.BlockSpec((h, 1, GDN_DIM, GDN_DIM), lambda i: (0, n - 1 - i, 0, 0)), big],
        out_specs=[big] * 4 + [sq, glb],
        out_shape=[big_shape] * 4 + [jax.ShapeDtypeStruct((h, t, CHUNK), F32),
                                     jax.ShapeDtypeStruct((h, n, 1, GDN_DIM), F32)],
        scratch_shapes=[pltpu.VMEM((h, GDN_DIM, GDN_DIM), F32)],
        compiler_params=_cparams("arbitrary"),
    )(u, w, qd, kd, aqk, gl, states, do)


def _gdn_local_bwd(qkv, ab, a_log, dt_bias, du, dw, dqd, dkd, daqk, dgl, *, name):
    t = qkv.shape[1]
    n = t // CHUNK
    cps = _chunks_per_step(n)
    rows = cps * CHUNK

    def body(q_ref, k_ref, v_ref, ab_ref, al_ref, dt_ref, du_ref, dw_ref, dqd_ref, dkd_ref, daqk_ref, dgl_ref,
             dq_ref, dk_ref, dv_ref, dab_ref, dsc_ref):
        head = pl.program_id(0)
        ea = jnp.exp(_scalar_row(al_ref, head))
        dtb = _scalar_row(dt_ref, head)
        lane = lax.broadcasted_iota(jnp.int32, (CHUNK, AB_W), 1)
        lane1 = lax.broadcasted_iota(jnp.int32, (1, GDN_DIM), 1)
        ri = lax.broadcasted_iota(jnp.int32, (CHUNK, CHUNK), 0)
        ci = lax.broadcasted_iota(jnp.int32, (CHUNK, CHUNK), 1)
        utri = (ri <= ci).astype(F32)
        ones = jnp.ones((CHUNK, GDN_DIM), F32)
        acc_alog = jnp.zeros((1, 1), F32)
        acc_dtb = jnp.zeros((1, 1), F32)
        for j in range(cps):
            sl = slice(j * CHUNK, (j + 1) * CHUNK)
            q, k, v = q_ref[0, sl, :], k_ref[0, sl, :], v_ref[0, sl, :]
            r = _gdn_chunk(q, k, v, ab_ref[sl, :], head, ea, dtb)
            duv, dwv, dqdv, dkdv = du_ref[0, sl, :], dw_ref[0, sl, :], dqd_ref[0, sl, :], dkd_ref[0, sl, :]
            d_aqk = jnp.where(r["tri"], daqk_ref[0, sl, :], 0.0)
            dvb = _dot3(r["tinv"], duv, 0, 0)
            dkbg = _dot3(r["tinv"], dwv, 0, 0)
            d_akk = -jnp.where(r["strict"], _dot3(dvb, r["u"], 1, 1) + _dot3(dkbg, r["w"], 1, 1), 0.0)
            e = d_akk * r["a_kk"] + d_aqk * r["a_qk"]
            dmkk, dmqk = d_akk * r["decay"], d_aqk * r["decay"]
            dkb = _dot3(dmkk, k) + dkbg * r["egc"]
            dk = (_dot3(dmkk, r["kb"], 0, 0) + _dot3(dmqk, q, 0, 0) + dkdv * r["ekd"]
                  + dkb * r["beta"])
            dq = _dot3(dmqk, k) + dqdv * r["egc"]
            dq_ref[0, sl, :] = dq
            dk_ref[0, sl, :] = dk
            dv_ref[0, sl, :] = dvb * r["beta"]
            dbeta = jnp.sum(dkb * k + dvb * v, axis=1, keepdims=True)
            kd_term = jnp.sum(dkdv * r["kd"], axis=1, keepdims=True)
            dgc = (jnp.sum(e, axis=1, keepdims=True) + jnp.sum(dqdv * r["qd"] + dkbg * r["kbg"], axis=1, keepdims=True)
                   - kd_term)
            dgc_lanes = jnp.broadcast_to(dgc, (CHUNK, GDN_DIM)) - _dot_exact(e, ones, 0, 0, ones="b")
            dgl_tot = jnp.sum(dgl_ref[0, j], axis=1, keepdims=True) * jnp.exp(r["g_tot"])
            d_tot = jnp.sum(kd_term, axis=0, keepdims=True) + dgl_tot
            dg = _dot_exact(utri, dgc_lanes) + d_tot
            dg = jnp.sum(jnp.where(lane == 0, dg, 0.0), axis=1, keepdims=True)
            da = dg * (-ea) * r["sig_a"]
            db = dbeta * r["beta"] * (1.0 - r["beta"])
            dab_ref[0, sl, :] = jnp.where(lane == head, da, 0.0) + jnp.where(lane == GDN_HEADS + head, db, 0.0)
            acc_alog = acc_alog + jnp.sum(dg * r["g"], axis=0, keepdims=True)
            acc_dtb = acc_dtb + jnp.sum(da, axis=0, keepdims=True)
        dsc_ref[0, 0] = jnp.where(lane1 == 0, acc_alog, 0.0) + jnp.where(lane1 == 1, acc_dtb, 0.0)

    big = pl.BlockSpec((1, rows, GDN_DIM), lambda h, i: (h, i, 0))
    big_shape = jax.ShapeDtypeStruct((GDN_HEADS, t, GDN_DIM), F32)
    return pl.pallas_call(
        body, name=name, grid=(GDN_HEADS, n // cps),
        in_specs=_gdn_local_specs(t, cps) + [big] * 4 + [pl.BlockSpec((1, rows, CHUNK), lambda h, i: (h, i, 0)),
                                                        pl.BlockSpec((1, cps, 1, GDN_DIM), lambda h, i: (h, i, 0, 0))],
        out_specs=[big] * 4 + [pl.BlockSpec((1, 1, 1, GDN_DIM), lambda h, i: (h, i, 0, 0))],
        out_shape=[big_shape] * 4 + [jax.ShapeDtypeStruct((GDN_HEADS, n // cps, 1, GDN_DIM), F32)],
        compiler_params=_cparams("parallel", "parallel"),
    )(qkv, qkv, qkv, ab, a_log, dt_bias, du, dw, dqd, dkd, daqk, dgl)


def _gdn_post_fwd(o, proj, norm_w, *, name):
    h, t, _ = o.shape
    tm = _tile(t, ROW_TILE)

    def body(o_ref, g_ref, w_ref, y_ref):
        for hh in range(h):
            sl = slice(hh * GDN_DIM, (hh + 1) * GDN_DIM)
            ov = o_ref[hh]
            gate = g_ref[:, sl].astype(F32)
            r = lax.rsqrt(jnp.mean(ov * ov, axis=-1, keepdims=True) + EPS)
            y_ref[:, sl] = (ov * r * w_ref[...] * (gate * _sigmoid(gate))).astype(y_ref.dtype)

    return pl.pallas_call(
        body, name=name, grid=(t // tm,),
        in_specs=[pl.BlockSpec((h, tm, GDN_DIM), lambda i: (0, i, 0)),
                  pl.BlockSpec((tm, GDN_W), lambda i: (i, C_GGATE // GDN_W)),
                  pl.BlockSpec((1, GDN_DIM), lambda i: (0, 0))],
        out_specs=pl.BlockSpec((tm, GDN_W), lambda i: (i, 0)),
        out_shape=jax.ShapeDtypeStruct((t, GDN_W), BF16), compiler_params=_cparams("parallel"),
    )(o, proj, norm_w.reshape(1, GDN_DIM))


def _gdn_post_bwd(o, proj, norm_w, dy, *, name):
    h, t, _ = o.shape
    tm = _tile(t, ROW_TILE)

    def body(o_ref, g_ref, w_ref, dy_ref, do_ref, dg_ref, dw_ref):
        part = jnp.zeros((1, GDN_DIM), F32)
        for hh in range(h):
            sl = slice(hh * GDN_DIM, (hh + 1) * GDN_DIM)
            ov = o_ref[hh]
            gate = g_ref[:, sl].astype(F32)
            sig = _sigmoid(gate)
            silu = gate * sig
            r = lax.rsqrt(jnp.mean(ov * ov, axis=-1, keepdims=True) + EPS)
            oh = ov * r
            dyv = dy_ref[:, sl].astype(F32)
            dg_ref[:, sl] = (dyv * oh * w_ref[...] * (sig + silu * (1.0 - sig))).astype(dg_ref.dtype)
            dn = dyv * silu
            part = part + jnp.sum(dn * oh, axis=0, keepdims=True)
            gw = dn * w_ref[...]
            do_ref[hh] = r * (gw - oh * jnp.mean(gw * oh, axis=-1, keepdims=True))

        @pl.when(pl.program_id(0) == 0)
        def _():
            dw_ref[...] = part

        @pl.when(pl.program_id(0) > 0)
        def _():
            dw_ref[...] += part

    return pl.pallas_call(
        body, name=name, grid=(t // tm,),
        in_specs=[pl.BlockSpec((h, tm, GDN_DIM), lambda i: (0, i, 0)),
                  pl.BlockSpec((tm, GDN_W), lambda i: (i, C_GGATE // GDN_W)),
                  pl.BlockSpec((1, GDN_DIM), lambda i: (0, 0)),
                  pl.BlockSpec((tm, GDN_W), lambda i: (i, 0))],
        out_specs=[pl.BlockSpec((h, tm, GDN_DIM), lambda i: (0, i, 0)), pl.BlockSpec((tm, GDN_W), lambda i: (i, 0)),
                   pl.BlockSpec((1, GDN_DIM), lambda i: (0, 0))],
        out_shape=[jax.ShapeDtypeStruct((h, t, GDN_DIM), F32), jax.ShapeDtypeStruct((t, GDN_W), BF16),
                   jax.ShapeDtypeStruct((1, GDN_DIM), F32)],
        compiler_params=_cparams("arbitrary"),
    )(o, proj, norm_w.reshape(1, GDN_DIM), dy)


def _split_dot(x, m):
    hi = x.astype(BF16)
    lo = (x - hi.astype(F32)).astype(BF16)
    return _dot(hi, m) + _dot(lo, m)


def _sb_block(q, kblk, qi, kb):
    bq, bk = q.shape[0], kblk.shape[0]
    z = _dot(q, kblk, 1, 1) * (SB_DIM ** -0.5)
    t_idx = qi * bq + lax.broadcasted_iota(jnp.int32, (bq, bk), 0)
    s_idx = kb * bk + lax.broadcasted_iota(jnp.int32, (bq, bk), 1)
    mask = s_idx < t_idx
    e = jnp.exp(-jnp.abs(z))
    sp = jnp.where(mask, jnp.maximum(z, 0.0) + jnp.log(1.0 + e), 0.0)
    return z, mask, e, sp


def _suffix_ones(blk):
    ri = lax.broadcasted_iota(jnp.int32, (blk, blk), 0)
    ci = lax.broadcasted_iota(jnp.int32, (blk, blk), 1)
    return (ri >= ci).astype(BF16)


def _prefix_ones(blk):
    ri = lax.broadcasted_iota(jnp.int32, (blk, blk), 0)
    ci = lax.broadcasted_iota(jnp.int32, (blk, blk), 1)
    return (ri <= ci).astype(BF16)


def _sb_blocks(t):
    bq = _tile(t, SB_QBLOCK)
    bk = _tile(bq, SB_KBLOCK)
    return bq, bk


def _sb_fwd(q, k, v, *, name):
    h, t, d = q.shape
    bq, bk = _sb_blocks(t)

    def body(q_ref, k_ref, v_ref, o_ref):
        qi = pl.program_id(1)
        qv = q_ref[0]
        suffix = _suffix_ones(bk)
        nkb = (qi + 1) * (bq // bk)

        def step(it, carry):
            later, acc = carry
            kb = nkb - 1 - it
            rows = pl.ds(pl.multiple_of(kb * bk, bk), bk)
            z, mask, _, sp = _sb_block(qv, k_ref[0, rows, :], qi, kb)
            csum = _split_dot(sp, suffix)
            att = jnp.where(mask, jnp.exp(z - csum - later), 0.0)
            acc = acc + _dot(att, v_ref[0, rows, :])
            return later + jnp.sum(sp, axis=1, keepdims=True), acc

        _, acc = lax.fori_loop(0, nkb, step, (jnp.zeros((bq, 1), F32), jnp.zeros((bq, d), F32)))
        o_ref[0] = acc

    qspec = pl.BlockSpec((1, bq, d), lambda hh, i: (hh, i, 0))
    kvspec = pl.BlockSpec((1, t, d), lambda hh, i: (hh, 0, 0))
    return pl.pallas_call(
        body, name=name, grid=(h, t // bq), in_specs=[qspec, kvspec, kvspec], out_specs=qspec,
        out_shape=jax.ShapeDtypeStruct((h, t, d), F32), compiler_params=_cparams("parallel", "parallel"),
    )(q, k, v)


def _sb_bwd(q, k, v, do, *, name):
    h, t, d = q.shape
    bq, bk = _sb_blocks(t)
    scale = SB_DIM ** -0.5

    def body(q_ref, k_ref, v_ref, do_ref, dq_ref, dk_ref, dv_ref, dl_keep, sig_keep):
        qi = pl.program_id(1)

        @pl.when(qi == 0)
        def _():
            dk_ref[...] = jnp.zeros_like(dk_ref)
            dv_ref[...] = jnp.zeros_like(dv_ref)

        qv = q_ref[0]
        dov = do_ref[0]
        suffix = _suffix_ones(bk)
        prefix = _prefix_ones(bk)
        nkb = (qi + 1) * (bq // bk)

        def back(it, later):
            kb = nkb - 1 - it
            rows = pl.ds(pl.multiple_of(kb * bk, bk), bk)
            vblk = v_ref[0, rows, :]
            z, mask, e, sp = _sb_block(qv, k_ref[0, rows, :], qi, kb)
            csum = _split_dot(sp, suffix)
            att = jnp.where(mask, jnp.exp(z - csum - later), 0.0)
            dv_ref[0, rows, :] += _dot(att, dov, 0, 0)
            dl_keep[kb] = att * _dot(dov, vblk, 1, 1)
            sig_keep[kb] = jnp.where(mask, jnp.where(z >= 0, 1.0, e) / (1.0 + e), 0.0)
            return later + jnp.sum(sp, axis=1, keepdims=True)

        lax.fori_loop(0, nkb, back, jnp.zeros((bq, 1), F32))

        def forth(kb, carry):
            before, dq = carry
            rows = pl.ds(pl.multiple_of(kb * bk, bk), bk)
            dl = dl_keep[kb]
            dz = (dl - sig_keep[kb] * (before + _split_dot(dl, prefix))) * scale
            dk_ref[0, rows, :] += _dot(dz, qv, 0, 0)
            return before + jnp.sum(dl, axis=1, keepdims=True), dq + _dot(dz, k_ref[0, rows, :])

        _, dq = lax.fori_loop(0, nkb, forth, (jnp.zeros((bq, 1), F32), jnp.zeros((bq, d), F32)))
        dq_ref[0] = dq

    qspec = pl.BlockSpec((1, bq, d), lambda hh, i: (hh, i, 0))
    kvspec = pl.BlockSpec((1, t, d), lambda hh, i: (hh, 0, 0))
    shape = jax.ShapeDtypeStruct((h, t, d), F32)
    return pl.pallas_call(
        body, name=name, grid=(h, t // bq), in_specs=[qspec, kvspec, kvspec, qspec],
        out_specs=[qspec, kvspec, kvspec], out_shape=[shape] * 3,
        scratch_shapes=[pltpu.VMEM((t // bk, bq, bk), F32), pltpu.VMEM((t // bk, bq, bk), F32)],
        compiler_params=_cparams("parallel", "arbitrary"),
    )(q, k, v, do)


def _gate_specs(tm):
    return [pl.BlockSpec((tm, D_MODEL), lambda i, b=b: (i, C_GATES // D_MODEL + b)) for b in range(3)]


def _merge_fwd(p, proj, *, name):
    t = proj.shape[0]
    tm = _tile(t, ROW_TILE)

    def body(p0, p1, p2, g0, g1, g2, o_ref):
        acc = jnp.zeros((tm, D_MODEL), F32)
        for pr, gr in ((p0, g0), (p1, g1), (p2, g2)):
            acc = acc + _sigmoid(gr[...].astype(F32)) * pr[...].astype(F32)
        o_ref[...] = acc.astype(o_ref.dtype)

    row = pl.BlockSpec((tm, D_MODEL), lambda i: (i, 0))
    return pl.pallas_call(
        body, name=name, grid=(t // tm,), in_specs=[row] * 3 + _gate_specs(tm), out_specs=row,
        out_shape=jax.ShapeDtypeStruct((t, D_MODEL), BF16), compiler_params=_cparams("parallel"),
    )(*p, proj, proj, proj)


def _merge_bwd(p, proj, dmerged, *, name):
    t = proj.shape[0]
    tm = _tile(t, ROW_TILE)

    def body(p0, p1, p2, g0, g1, g2, dm_ref, dp0, dp1, dp2, dg_ref):
        dm = dm_ref[...].astype(F32)
        for b, (pr, gr, dpr) in enumerate(((p0, g0, dp0), (p1, g1, dp1), (p2, g2, dp2))):
            s = _sigmoid(gr[...].astype(F32))
            dpr[...] = (dm * s).astype(dpr.dtype)
            dg_ref[:, b * D_MODEL:(b + 1) * D_MODEL] = (dm * pr[...].astype(F32) * s * (1.0 - s)).astype(dg_ref.dtype)

    row = pl.BlockSpec((tm, D_MODEL), lambda i: (i, 0))
    res = pl.pallas_call(
        body, name=name, grid=(t // tm,), in_specs=[row] * 3 + _gate_specs(tm) + [row],
        out_specs=[row] * 3 + [pl.BlockSpec((tm, 3 * D_MODEL), lambda i: (i, 0))],
        out_shape=[jax.ShapeDtypeStruct((t, D_MODEL), BF16)] * 3 + [jax.ShapeDtypeStruct((t, 3 * D_MODEL), BF16)],
        compiler_params=_cparams("parallel"),
    )(*p, proj, proj, proj, dmerged)
    return res[:3], res[3]


def _loss_head(y, target, *, name):
    t, d = y.shape
    tm = _tile(t, ROW_TILE)

    def body(y_ref, t_ref, dy_ref, l_ref):
        err = y_ref[...] - t_ref[...]
        dy_ref[...] = err * (1.0 / d)
        part = jnp.sum(err * err, axis=0, keepdims=True) * (0.5 / d)

        @pl.when(pl.program_id(0) == 0)
        def _():
            l_ref[...] = part

        @pl.when(pl.program_id(0) > 0)
        def _():
            l_ref[...] += part

    row = pl.BlockSpec((tm, d), lambda i: (i, 0))
    vec = pl.BlockSpec((1, d), lambda i: (0, 0))
    return pl.pallas_call(
        body, name=name, grid=(t // tm,), in_specs=[row, row], out_specs=[row, vec],
        out_shape=[jax.ShapeDtypeStruct((t, d), F32), jax.ShapeDtypeStruct((1, d), F32)],
        compiler_params=_cparams("arbitrary"),
    )(y, target)


def _adamw(w, g, m, v, *, name):
    r, c = w.shape
    tr = r if r * c * 4 <= 2 ** 21 else max(8, (2 ** 21 // (c * 4)) // 8 * 8)
    while r % tr:
        tr -= 8
    c1 = 1.0 - ADAM_B1 ** ADAM_STEP
    c2 = 1.0 - ADAM_B2 ** ADAM_STEP

    def body(w_ref, g_ref, m_ref, v_ref, d_ref, nm_ref, nv_ref):
        gv = g_ref[...]
        nm = ADAM_B1 * m_ref[...] + (1.0 - ADAM_B1) * gv
        nv = ADAM_B2 * v_ref[...] + (1.0 - ADAM_B2) * (gv * gv)
        nm_ref[...] = nm
        nv_ref[...] = nv
        d_ref[...] = -ADAM_LR * ((nm / c1) / (jnp.sqrt(nv / c2) + ADAM_EPS) + ADAM_WD * w_ref[...])

    spec = pl.BlockSpec((tr, c), lambda i: (i, 0))
    return pl.pallas_call(
        body, name=name, grid=(r // tr,), in_specs=[spec] * 4, out_specs=[spec] * 3,
        out_shape=[jax.ShapeDtypeStruct((r, c), F32)] * 3, compiler_params=_cparams("parallel"),
    )(w, g, m, v)


def _all_gather(xs, *, name):
    n = len(xs)

    def body(*refs):
        x_refs, out_refs = refs[:n], refs[n:2 * n]
        send_sems, recv_sems, local_sems = refs[2 * n:]
        mx, my, mc = lax.axis_index("x"), lax.axis_index("y"), lax.axis_index("c")
        me, sibling = (mx, my, mc), (mx, my, 1 - mc)
        chips = [(1 - mx, my), (mx, 1 - my), (1 - mx, 1 - my)]

        def slot(a, px, py, pc):
            return out_refs[a].at[4 * px + 2 * py + pc]

        def copy(a, k, block, to, src=None):
            return pltpu.make_async_remote_copy(
                src_ref=slot(a, *block) if src is None else src, dst_ref=slot(a, *block),
                send_sem=send_sems.at[a, k], recv_sem=recv_sems.at[a, k], device_id=to, device_id_type=MESH_ID)

        mine = [pltpu.make_async_copy(x_refs[a], slot(a, *me), local_sems.at[a]) for a in range(n)]
        for cp in mine:
            cp.start()
        first = [copy(a, 1 + j, me, (*chip, mc), src=x_refs[a]) for j, chip in enumerate(chips) for a in range(n)]
        first += [copy(a, 0, me, sibling, src=x_refs[a]) for a in range(n)]
        for cp in first:
            cp.start()
        passed = []
        for j, chip in enumerate(chips):
            for a in range(n):
                copy(a, 1 + j, (*chip, mc), me).wait_recv()
                passed.append(copy(a, 4 + j, (*chip, mc), sibling))
                passed[-1].start()
        for a in range(n):
            copy(a, 0, sibling, me).wait_recv()
        for j, chip in enumerate(chips):
            for a in range(n):
                copy(a, 4 + j, (*chip, 1 - mc), me).wait_recv()
        for cp in first + passed:
            cp.wait_send()
        for cp in mine:
            cp.wait()

    anyspace = pl.BlockSpec(memory_space=pl.ANY)
    return pl.pallas_call(
        body, name=name, in_specs=[anyspace] * n, out_specs=[anyspace] * n,
        out_shape=[jax.ShapeDtypeStruct((N_DEV,) + x.shape, x.dtype) for x in xs],
        scratch_shapes=[pltpu.SemaphoreType.DMA((n, 7)), pltpu.SemaphoreType.DMA((n, 7)), pltpu.SemaphoreType.DMA((n,))],
    )(*xs)


def _exchange_sibling(gs, *, name):
    n = len(gs)

    def body(*refs):
        g_refs, out_refs = refs[:n], refs[n:2 * n]
        send_sems, recv_sems = refs[2 * n:]
        mx, my, mc = lax.axis_index("x"), lax.axis_index("y"), lax.axis_index("c")
        sibling = (mx, my, 1 - mc)
        copies = []
        for a in range(n):
            for px in range(2):
                for py in range(2):
                    kk = 2 * px + py
                    copies.append(pltpu.make_async_remote_copy(
                        src_ref=g_refs[a].at[4 * px + 2 * py + (1 - mc)], dst_ref=out_refs[a].at[kk],
                        send_sem=send_sems.at[a, kk], recv_sem=recv_sems.at[a, kk], device_id=sibling,
                        device_id_type=MESH_ID))
        for cp in copies:
            cp.start()
        for cp in copies:
            cp.wait_recv()
        for cp in copies:
            cp.wait_send()

    anyspace = pl.BlockSpec(memory_space=pl.ANY)
    return pl.pallas_call(
        body, name=name, in_specs=[anyspace] * n, out_specs=[anyspace] * n,
        out_shape=[jax.ShapeDtypeStruct((4,) + g.shape[1:], g.dtype) for g in gs],
        scratch_shapes=[pltpu.SemaphoreType.DMA((n, 4)), pltpu.SemaphoreType.DMA((n, 4))],
    )(*gs)


def _pair_sum(g, got, *, name):
    _, r, c = g.shape
    tr = _tile(r, ROW_TILE)

    def body(core_ref, a_ref, b_ref, o_ref):
        del core_ref
        o_ref[...] = (a_ref[...].astype(F32) + b_ref[...].astype(F32)).astype(o_ref.dtype)

    grid_spec = pltpu.PrefetchScalarGridSpec(
        num_scalar_prefetch=1, grid=(4, r // tr),
        in_specs=[pl.BlockSpec((1, tr, c), lambda kk, i, core: (2 * kk + core[0], i, 0)),
                  pl.BlockSpec((1, tr, c), lambda kk, i, core: (kk, i, 0))],
        out_specs=pl.BlockSpec((1, tr, c), lambda kk, i, core: (kk, i, 0)))
    return pl.pallas_call(
        body, name=name, grid_spec=grid_spec, out_shape=jax.ShapeDtypeStruct((4, r, c), g.dtype),
        compiler_params=_cparams("parallel", "parallel"),
    )(lax.axis_index("c").astype(jnp.int32).reshape(1), g, got)


def _exchange_chips(parts, *, name):
    n = len(parts)

    def body(*refs):
        p_refs, out_refs = refs[:n], refs[n:2 * n]
        send_sems, recv_sems = refs[2 * n:]
        mx, my, mc = lax.axis_index("x"), lax.axis_index("y"), lax.axis_index("c")
        chips = [(1 - mx, my), (mx, 1 - my), (1 - mx, 1 - my)]
        copies = [pltpu.make_async_remote_copy(
            src_ref=p_refs[a].at[2 * px + py], dst_ref=out_refs[a].at[j], send_sem=send_sems.at[a, j],
            recv_sem=recv_sems.at[a, j], device_id=(px, py, mc), device_id_type=MESH_ID)
            for j, (px, py) in enumerate(chips) for a in range(n)]
        for cp in copies:
            cp.start()
        for cp in copies:
            cp.wait_recv()
        for cp in copies:
            cp.wait_send()

    anyspace = pl.BlockSpec(memory_space=pl.ANY)
    return pl.pallas_call(
        body, name=name, in_specs=[anyspace] * n, out_specs=[anyspace] * n,
        out_shape=[jax.ShapeDtypeStruct((3,) + p.shape[1:], p.dtype) for p in parts],
        scratch_shapes=[pltpu.SemaphoreType.DMA((n, 3)), pltpu.SemaphoreType.DMA((n, 3))],
    )(*parts)


def _final_sum(part, got, *, name):
    _, r, c = part.shape
    tr = _tile(r, ROW_TILE)

    def body(chip_ref, a_ref, b_ref, o_ref):
        del chip_ref
        acc = a_ref[0].astype(F32)
        for j in range(3):
            acc = acc + b_ref[j].astype(F32)
        o_ref[...] = acc

    grid_spec = pltpu.PrefetchScalarGridSpec(
        num_scalar_prefetch=1, grid=(r // tr,),
        in_specs=[pl.BlockSpec((1, tr, c), lambda i, chip: (chip[0], i, 0)),
                  pl.BlockSpec((3, tr, c), lambda i, chip: (0, i, 0))],
        out_specs=pl.BlockSpec((tr, c), lambda i, chip: (i, 0)))
    chip = (2 * lax.axis_index("x") + lax.axis_index("y")).astype(jnp.int32).reshape(1)
    return pl.pallas_call(
        body, name=name, grid_spec=grid_spec, out_shape=jax.ShapeDtypeStruct((r, c), F32),
        compiler_params=_cparams("parallel"),
    )(chip, part, got)


def _sum_devices(x, *, name):
    _, r, c = x.shape

    def body(x_ref, o_ref):
        acc = x_ref[0]
        for j in range(1, N_DEV):
            acc = acc + x_ref[j]
        o_ref[...] = acc

    return pl.pallas_call(body, name=name, out_shape=jax.ShapeDtypeStruct((r, c), F32),
                          compiler_params=_cparams())(x)


BIG = ("w_in", "w_branch", "w_out", "w_ff1", "w_ff2")
BIG_AXIS = {"w_in": 2, "w_branch": 3, "w_out": 1, "w_ff1": 2, "w_ff2": 1}


def _to_global(blocks, axis):
    moved = jnp.moveaxis(blocks, 0, axis)
    shp = moved.shape
    return moved.reshape(shp[:axis] + (shp[axis] * shp[axis + 1],) + shp[axis + 2:])


def _to_blocks(full, axis):
    shp = full.shape
    split = full.reshape(shp[:axis] + (N_DEV, shp[axis] // N_DEV) + shp[axis + 1:])
    return jnp.moveaxis(split, axis, 0)


def _rows(blocks):
    return blocks.reshape(blocks.shape[0], -1, blocks.shape[-1])


def _vec_rows(n):
    return -(-n // 128 // 8) * 8


def _pack_vec(parts):
    flat = jnp.concatenate([p.reshape(-1).astype(F32) for p in parts])
    rows = _vec_rows(flat.shape[0])
    return jnp.pad(flat, (0, rows * 128 - flat.shape[0])).reshape(rows, 128)


def _unpack_vec(flat, shapes):
    lead = flat.shape[:-2]
    flat = flat.reshape(lead + (-1,))
    out, off = [], 0
    for s in shapes:
        n = 1
        for dim in s:
            n *= dim
        out.append(flat[..., off:off + n].reshape(lead + tuple(s)))
        off += n
    return out


def _dot_each(a_list, b_list, ca=1, cb=0):
    return [_dot(a, b, ca, cb) for a, b in zip(a_list, b_list)]


def _dot3_each(a_list, b_list, ca=1, cb=0):
    sa = [_split2(a) for a in a_list]
    sb = [_split2(b) for b in b_list]
    prods = [(_dot(a1, b1, ca, cb), _dot(a1, b2, ca, cb), _dot(a2, b1, ca, cb)) for (a1, a2), (b1, b2) in zip(sa, sb)]
    return [x + (y + z) for x, y, z in prods]


def _split3(x):
    hi = x.astype(BF16)
    rest = x - hi.astype(F32)
    mid = rest.astype(BF16)
    return hi, mid, (rest - mid.astype(F32)).astype(BF16)


def _ones_dot_each(m, x_list, ca=1, cb=0):
    mb = m.astype(BF16)
    parts = [[_dot(mb, p, ca, cb) for p in _split3(x)] for x in x_list]
    return [p[0] + (p[1] + p[2]) for p in parts]


def _dot_ones_each(x_list, m, ca=1, cb=0):
    mb = m.astype(BF16)
    parts = [[_dot(p, mb, ca, cb) for p in _split3(x)] for x in x_list]
    return [p[0] + (p[1] + p[2]) for p in parts]


def _tri_inv_each(a_list):
    c = a_list[0].shape[0]
    ri = lax.broadcasted_iota(jnp.int32, (c, c), 0)
    ci = lax.broadcasted_iota(jnp.int32, (c, c), 1)
    eye = (ri == ci).astype(F32)
    blk = 8
    pws = [-jnp.where(ri // blk == ci // blk, a, 0.0) for a in a_list]
    invs = [eye + b for b in pws]
    for _ in range(2):
        pws = _dot3_each(pws, pws)
        invs = [i + u for i, u in zip(invs, _dot3_each(invs, pws))]
    while blk < c:
        sel = (ri // (2 * blk) == ci // (2 * blk)) & (ri // blk != ci // blk)
        offs = [jnp.where(sel, a, 0.0) for a in a_list]
        invs = [i - t for i, t in zip(invs, _dot3_each(_dot3_each(invs, offs), invs))]
        blk *= 2
    return invs


def _gdn_chunks(qs, ks, vs, abs_, head, ea, dtb):
    c = qs[0].shape[0]
    lane = lax.broadcasted_iota(jnp.int32, abs_[0].shape, 1)
    a_s = [jnp.sum(jnp.where(lane == head, ab, 0.0), axis=1, keepdims=True) for ab in abs_]
    b_s = [jnp.sum(jnp.where(lane == GDN_HEADS + head, ab, 0.0), axis=1, keepdims=True) for ab in abs_]
    ri = lax.broadcasted_iota(jnp.int32, (c, c), 0)
    ci = lax.broadcasted_iota(jnp.int32, (c, c), 1)
    tri, strict = ri >= ci, ri > ci
    ltri = tri.astype(F32)
    beta = [_sigmoid(b) for b in b_s]
    sig_a = [_sigmoid(a + dtb) for a in a_s]
    g = [-ea * _softplus(a + dtb) for a in a_s]
    g_cc = [jnp.broadcast_to(x, (c, c)) for x in g]
    gi = _ones_dot_each(ltri, g_cc)
    gj = _dot_ones_each(g_cc, (ri <= ci).astype(F32), 0, 0)
    decay = [jnp.exp(jnp.where(tri, x - y, -1e30)) for x, y in zip(gi, gj)]
    gc = _ones_dot_each(ltri, [jnp.broadcast_to(x, (c, GDN_DIM)) for x in g])
    g_tot = [jnp.sum(x, axis=0, keepdims=True) for x in g]
    egc = [jnp.exp(x) for x in gc]
    ekd = [jnp.exp(t - x) for t, x in zip(g_tot, gc)]
    kb = [k * b for k, b in zip(ks, beta)]
    vb = [v * b for v, b in zip(vs, beta)]
    kbg = [x * e for x, e in zip(kb, egc)]
    mkk = _dot_each(kb, ks, 1, 1)
    a_kk = [jnp.where(strict, m * d, 0.0) for m, d in zip(mkk, decay)]
    tinv = _tri_inv_each(a_kk)
    u = _dot3_each(tinv, vb)
    w = _dot3_each(tinv, kbg)
    mqk = _dot_each(qs, ks, 1, 1)
    a_qk = [jnp.where(tri, m * d, 0.0) for m, d in zip(mqk, decay)]
    qd = [q * e for q, e in zip(qs, egc)]
    kd = [k * e for k, e in zip(ks, ekd)]
    return dict(beta=beta, sig_a=sig_a, g=g, decay=decay, egc=egc, ekd=ekd, g_tot=g_tot, kb=kb, vb=vb, kbg=kbg,
                a_kk=a_kk, tinv=tinv, u=u, w=w, a_qk=a_qk, qd=qd, kd=kd, tri=tri, strict=strict)


def _chunks_in_step(n):
    for cps in (8, 4, 2):
        if n % cps == 0:
            return cps
    return 1


def _gdn_local_fwd_staged(qkv, ab, a_log, dt_bias, *, name):
    t = qkv.shape[1]
    n = t // CHUNK
    cps = _chunks_in_step(n)
    rows = cps * CHUNK
    sls = [slice(j * CHUNK, (j + 1) * CHUNK) for j in range(cps)]

    def body(q_ref, k_ref, v_ref, ab_ref, al_ref, dt_ref, u_ref, w_ref, qd_ref, kd_ref, aqk_ref, gl_ref):
        head = pl.program_id(0)
        ea = jnp.exp(_scalar_row(al_ref, head))
        dtb = _scalar_row(dt_ref, head)
        r = _gdn_chunks([q_ref[0, sl, :] for sl in sls], [k_ref[0, sl, :] for sl in sls],
                        [v_ref[0, sl, :] for sl in sls], [ab_ref[sl, :] for sl in sls], head, ea, dtb)
        for j, sl in enumerate(sls):
            u_ref[0, sl, :] = r["u"][j]
            w_ref[0, sl, :] = r["w"][j]
            qd_ref[0, sl, :] = r["qd"][j]
            kd_ref[0, sl, :] = r["kd"][j]
            aqk_ref[0, sl, :] = r["a_qk"][j]
            gl_ref[0, j] = jnp.exp(jnp.broadcast_to(r["g_tot"][j], (1, GDN_DIM)))

    big = pl.BlockSpec((1, rows, GDN_DIM), lambda h, i: (h, i, 0))
    big_shape = jax.ShapeDtypeStruct((GDN_HEADS, t, GDN_DIM), F32)
    return pl.pallas_call(
        body, name=name, grid=(GDN_HEADS, n // cps), in_specs=_gdn_local_specs(t, cps),
        out_specs=[big] * 4 + [pl.BlockSpec((1, rows, CHUNK), lambda h, i: (h, i, 0)),
                               pl.BlockSpec((1, cps, 1, GDN_DIM), lambda h, i: (h, i, 0, 0))],
        out_shape=[big_shape] * 4 + [jax.ShapeDtypeStruct((GDN_HEADS, t, CHUNK), F32),
                                     jax.ShapeDtypeStruct((GDN_HEADS, n, 1, GDN_DIM), F32)],
        compiler_params=_cparams("parallel", "parallel"),
    )(qkv, qkv, qkv, ab, a_log, dt_bias)


def _gdn_local_bwd_staged(qkv, ab, a_log, dt_bias, du, dw, dqd, dkd, daqk, dgl, *, name):
    t = qkv.shape[1]
    n = t // CHUNK
    cps = _chunks_in_step(n)
    rows = cps * CHUNK
    sls = [slice(j * CHUNK, (j + 1) * CHUNK) for j in range(cps)]

    def body(q_ref, k_ref, v_ref, ab_ref, al_ref, dt_ref, du_ref, dw_ref, dqd_ref, dkd_ref, daqk_ref, dgl_ref,
             dq_ref, dk_ref, dv_ref, dab_ref, dsc_ref):
        head = pl.program_id(0)
        ea = jnp.exp(_scalar_row(al_ref, head))
        dtb = _scalar_row(dt_ref, head)
        lane = lax.broadcasted_iota(jnp.int32, (CHUNK, AB_W), 1)
        lane1 = lax.broadcasted_iota(jnp.int32, (1, GDN_DIM), 1)
        ri = lax.broadcasted_iota(jnp.int32, (CHUNK, CHUNK), 0)
        ci = lax.broadcasted_iota(jnp.int32, (CHUNK, CHUNK), 1)
        utri = (ri <= ci).astype(F32)
        ones = jnp.ones((CHUNK, GDN_DIM), F32)
        qs, ks, vs = ([ref[0, sl, :] for sl in sls] for ref in (q_ref, k_ref, v_ref))
        r = _gdn_chunks(qs, ks, vs, [ab_ref[sl, :] for sl in sls], head, ea, dtb)
        duv, dwv, dqdv, dkdv = ([ref[0, sl, :] for sl in sls] for ref in (du_ref, dw_ref, dqd_ref, dkd_ref))
        d_aqk = [jnp.where(r["tri"], daqk_ref[0, sl, :], 0.0) for sl in sls]
        dvb = _dot3_each(r["tinv"], duv, 0, 0)
        dkbg = _dot3_each(r["tinv"], dwv, 0, 0)
        outer = [x + y for x, y in zip(_dot_each(dvb, r["u"], 1, 1), _dot_each(dkbg, r["w"], 1, 1))]
        d_akk = [-jnp.where(r["strict"], x, 0.0) for x in outer]
        e = [x * a + y * b for x, a, y, b in zip(d_akk, r["a_kk"], d_aqk, r["a_qk"])]
        dmkk = [x * d for x, d in zip(d_akk, r["decay"])]
        dmqk = [x * d for x, d in zip(d_aqk, r["decay"])]
        dkb = [x + y * eg for x, y, eg in zip(_dot_each(dmkk, ks), dkbg, r["egc"])]
        dk = [a + b + x * ek + y * bt for a, b, x, ek, y, bt in zip(
            _dot_each(dmkk, r["kb"], 0, 0), _dot_each(dmqk, qs, 0, 0), dkdv, r["ekd"], dkb, r["beta"])]
        dq = [a + x * eg for a, x, eg in zip(_dot_each(dmqk, ks), dqdv, r["egc"])]
        col_sums = _dot_ones_each(e, ones, 0, 0)
        acc_alog = jnp.zeros((1, 1), F32)
        acc_dtb = jnp.zeros((1, 1), F32)
        dgc_lanes, d_tots, dbetas = [], [], []
        for j in range(cps):
            dbetas.append(jnp.sum(dkb[j] * ks[j] + dvb[j] * vs[j], axis=1, keepdims=True))
            kd_term = jnp.sum(dkdv[j] * r["kd"][j], axis=1, keepdims=True)
            dgc = (jnp.sum(e[j], axis=1, keepdims=True)
                   + jnp.sum(dqdv[j] * r["qd"][j] + dkbg[j] * r["kbg"][j], axis=1, keepdims=True) - kd_term)
            dgc_lanes.append(jnp.broadcast_to(dgc, (CHUNK, GDN_DIM)) - col_sums[j])
            dgl_tot = jnp.sum(dgl_ref[0, j], axis=1, keepdims=True) * jnp.exp(r["g_tot"][j])
            d_tots.append(jnp.sum(kd_term, axis=0, keepdims=True) + dgl_tot)
        suffix_sums = _ones_dot_each(utri, dgc_lanes)
        for j, sl in enumerate(sls):
            dq_ref[0, sl, :] = dq[j]
            dk_ref[0, sl, :] = dk[j]
            dv_ref[0, sl, :] = dvb[j] * r["beta"][j]
            dg = jnp.sum(jnp.where(lane == 0, suffix_sums[j] + d_tots[j], 0.0), axis=1, keepdims=True)
            da = dg * (-ea) * r["sig_a"][j]
            db = dbetas[j] * r["beta"][j] * (1.0 - r["beta"][j])
            dab_ref[0, sl, :] = jnp.where(lane == head, da, 0.0) + jnp.where(lane == GDN_HEADS + head, db, 0.0)
            acc_alog = acc_alog + jnp.sum(dg * r["g"][j], axis=0, keepdims=True)
            acc_dtb = acc_dtb + jnp.sum(da, axis=0, keepdims=True)
        dsc_ref[0, 0] = jnp.where(lane1 == 0, acc_alog, 0.0) + jnp.where(lane1 == 1, acc_dtb, 0.0)

    big = pl.BlockSpec((1, rows, GDN_DIM), lambda h, i: (h, i, 0))
    big_shape = jax.ShapeDtypeStruct((GDN_HEADS, t, GDN_DIM), F32)
    return pl.pallas_call(
        body, name=name, grid=(GDN_HEADS, n // cps),
        in_specs=_gdn_local_specs(t, cps) + [big] * 4 + [pl.BlockSpec((1, rows, CHUNK), lambda h, i: (h, i, 0)),
                                                        pl.BlockSpec((1, cps, 1, GDN_DIM), lambda h, i: (h, i, 0, 0))],
        out_specs=[big] * 4 + [pl.BlockSpec((1, 1, 1, GDN_DIM), lambda h, i: (h, i, 0, 0))],
        out_shape=[big_shape] * 4 + [jax.ShapeDtypeStruct((GDN_HEADS, n // cps, 1, GDN_DIM), F32)],
        compiler_params=_cparams("parallel", "parallel"),
    )(qkv, qkv, qkv, ab, a_log, dt_bias, du, dw, dqd, dkd, daqk, dgl)


def _gdn_scan_fwd_staged(u, w, qd, kd, aqk, gl, *, name):
    h, t, _ = u.shape
    n = t // CHUNK
    hs = range(h)

    def body(u_ref, w_ref, qd_ref, kd_ref, aqk_ref, gl_ref, o_ref, s_ref, state):
        @pl.when(pl.program_id(0) == 0)
        def _():
            state[...] = jnp.zeros_like(state)

        ss = [state[hh] for hh in hs]
        for hh in hs:
            s_ref[hh, 0] = ss[hh]
        vn = [u_ref[hh] - x for hh, x in zip(hs, _dot_each([w_ref[hh] for hh in hs], ss))]
        from_state = _dot_each([qd_ref[hh] for hh in hs], ss)
        from_chunk = _dot_each([aqk_ref[hh] for hh in hs], vn)
        writes = _dot_each([kd_ref[hh] for hh in hs], vn, 0, 0)
        for hh in hs:
            o_ref[hh] = from_state[hh] + from_chunk[hh]
            state[hh] = ss[hh] * gl_ref[hh, 0] + writes[hh]

    big = pl.BlockSpec((h, CHUNK, GDN_DIM), lambda i: (0, i, 0))
    return pl.pallas_call(
        body, name=name, grid=(n,),
        in_specs=[big] * 4 + [pl.BlockSpec((h, CHUNK, CHUNK), lambda i: (0, i, 0)),
                              pl.BlockSpec((h, 1, 1, GDN_DIM), lambda i: (0, i, 0, 0))],
        out_specs=[big, pl.BlockSpec((h, 1, GDN_DIM, GDN_DIM), lambda i: (0, i, 0, 0))],
        out_shape=[jax.ShapeDtypeStruct((h, t, GDN_DIM), F32), jax.ShapeDtypeStruct((h, n, GDN_DIM, GDN_DIM), F32)],
        scratch_shapes=[pltpu.VMEM((h, GDN_DIM, GDN_DIM), F32)],
        compiler_params=_cparams("arbitrary"),
    )(u, w, qd, kd, aqk, gl)


def _gdn_scan_bwd_staged(u, w, qd, kd, aqk, gl, states, do, *, name):
    h, t, _ = u.shape
    n = t // CHUNK
    hs = range(h)

    def body(u_ref, w_ref, qd_ref, kd_ref, aqk_ref, gl_ref, s_ref, do_ref,
             du_ref, dw_ref, dqd_ref, dkd_ref, daqk_ref, dgl_ref, dstate):
        @pl.when(pl.program_id(0) == 0)
        def _():
            dstate[...] = jnp.zeros_like(dstate)

        ri = lax.broadcasted_iota(jnp.int32, (CHUNK, CHUNK), 0)
        ci = lax.broadcasted_iota(jnp.int32, (CHUNK, CHUNK), 1)
        ss = [s_ref[hh, 0] for hh in hs]
        dsn = [dstate[hh] for hh in hs]
        dov = [do_ref[hh] for hh in hs]
        wv = [w_ref[hh] for hh in hs]
        vn = [u_ref[hh] - x for hh, x in zip(hs, _dot_each(wv, ss))]
        dvn = [x + y for x, y in zip(_dot_each([aqk_ref[hh] for hh in hs], dov, 0, 0),
                                     _dot_each([kd_ref[hh] for hh in hs], dsn))]
        dws = _dot_each(dvn, ss, 1, 1)
        dqds = _dot_each(dov, ss, 1, 1)
        dkds = _dot_each(vn, dsn, 1, 1)
        daqks = _dot_each(dov, vn, 1, 1)
        reads = _dot_each([qd_ref[hh] for hh in hs], dov, 0, 0)
        solves = _dot_each(wv, dvn, 0, 0)
        for hh in hs:
            du_ref[hh] = dvn[hh]
            dw_ref[hh] = -dws[hh]
            dqd_ref[hh] = dqds[hh]
            dkd_ref[hh] = dkds[hh]
            daqk_ref[hh] = jnp.where(ri >= ci, daqks[hh], 0.0)
            dgl_ref[hh, 0] = jnp.sum(dsn[hh] * ss[hh], axis=0, keepdims=True)
            dstate[hh] = reads[hh] + dsn[hh] * gl_ref[hh, 0] - solves[hh]

    big = pl.BlockSpec((h, CHUNK, GDN_DIM), lambda i: (0, n - 1 - i, 0))
    sq = pl.BlockSpec((h, CHUNK, CHUNK), lambda i: (0, n - 1 - i, 0))
    glb = pl.BlockSpec((h, 1, 1, GDN_DIM), lambda i: (0, n - 1 - i, 0, 0))
    big_shape = jax.ShapeDtypeStruct((h, t, GDN_DIM), F32)
    return pl.pallas_call(
        body, name=name, grid=(n,),
        in_specs=[big] * 4 + [sq, glb, pl.BlockSpec((h, 1, GDN_DIM, GDN_DIM), lambda i: (0, n - 1 - i, 0, 0)), big],
        out_specs=[big] * 4 + [sq, glb],
        out_shape=[big_shape] * 4 + [jax.ShapeDtypeStruct((h, t, CHUNK), F32),
                                     jax.ShapeDtypeStruct((h, n, 1, GDN_DIM), F32)],
        scratch_shapes=[pltpu.VMEM((h, GDN_DIM, GDN_DIM), F32)],
        compiler_params=_cparams("arbitrary"),
    )(u, w, qd, kd, aqk, gl, states, do)


def _gather_phases(x_refs, out_refs, send_sems, recv_sems, local_sems):
    n = len(x_refs)
    mx, my, mc = lax.axis_index("x"), lax.axis_index("y"), lax.axis_index("c")
    me, sibling = (mx, my, mc), (mx, my, 1 - mc)
    chips = [(1 - mx, my), (mx, 1 - my), (1 - mx, 1 - my)]

    def slot(a, px, py, pc):
        return out_refs[a].at[4 * px + 2 * py + pc]

    def copy(a, k, block, to, src=None):
        return pltpu.make_async_remote_copy(
            src_ref=slot(a, *block) if src is None else src, dst_ref=slot(a, *block),
            send_sem=send_sems.at[a, k], recv_sem=recv_sems.at[a, k], device_id=to, device_id_type=MESH_ID)

    def mine():
        return [pltpu.make_async_copy(x_refs[a], slot(a, *me), local_sems.at[a]) for a in range(n)]

    def first():
        out = [copy(a, 1 + j, me, (*chip, mc), src=x_refs[a]) for j, chip in enumerate(chips) for a in range(n)]
        return out + [copy(a, 0, me, sibling, src=x_refs[a]) for a in range(n)]

    def passed():
        return [copy(a, 4 + j, (*chip, mc), sibling) for j, chip in enumerate(chips) for a in range(n)]

    def start():
        for cp in mine() + first():
            cp.start()

    def pass_on():
        for j, chip in enumerate(chips):
            for a in range(n):
                copy(a, 1 + j, (*chip, mc), me).wait_recv()
                copy(a, 4 + j, (*chip, mc), sibling).start()

    def finish():
        for a in range(n):
            copy(a, 0, sibling, me).wait_recv()
        for j, chip in enumerate(chips):
            for a in range(n):
                copy(a, 4 + j, (*chip, 1 - mc), me).wait_recv()
        for cp in first() + passed():
            cp.wait_send()
        for cp in mine():
            cp.wait()

    return start, pass_on, finish


def _gather_extras(shards):
    n = len(shards)
    anyspace = pl.BlockSpec(memory_space=pl.ANY)
    return ([anyspace] * n, [anyspace] * n, [jax.ShapeDtypeStruct((N_DEV,) + x.shape, x.dtype) for x in shards],
            [pltpu.SemaphoreType.DMA((n, 7)), pltpu.SemaphoreType.DMA((n, 7)), pltpu.SemaphoreType.DMA((n,))])


def _scatter_phases(g_refs, out_refs, send_sems, recv_sems, local_sems):
    n = len(g_refs)
    mx, my, mc = lax.axis_index("x"), lax.axis_index("y"), lax.axis_index("c")
    me_id = 4 * mx + 2 * my + mc

    def peer(r):
        return (1 - mx if r & 4 else mx, 1 - my if r & 2 else my, 1 - mc if r & 1 else mc)

    def peer_id(r):
        px, py, pc = peer(r)
        return 4 * px + 2 * py + pc

    def copies():
        return [pltpu.make_async_remote_copy(
            src_ref=g_refs[a].at[peer_id(r)], dst_ref=out_refs[a].at[me_id], send_sem=send_sems.at[a, r - 1],
            recv_sem=recv_sems.at[a, r - 1], device_id=peer(r), device_id_type=MESH_ID)
            for r in range(1, N_DEV) for a in range(n)]

    def arrivals():
        return [pltpu.make_async_remote_copy(
            src_ref=g_refs[a].at[peer_id(r)], dst_ref=out_refs[a].at[peer_id(r)], send_sem=send_sems.at[a, r - 1],
            recv_sem=recv_sems.at[a, r - 1], device_id=peer(r), device_id_type=MESH_ID)
            for r in range(1, N_DEV) for a in range(n)]

    def mine():
        return [pltpu.make_async_copy(g_refs[a].at[me_id], out_refs[a].at[me_id], local_sems.at[a]) for a in range(n)]

    def start():
        for cp in mine() + copies():
            cp.start()

    def finish():
        for cp in arrivals():
            cp.wait_recv()
        for cp in copies():
            cp.wait_send()
        for cp in mine():
            cp.wait()

    return start, finish


def _scatter_extras(blocks):
    n = len(blocks)
    anyspace = pl.BlockSpec(memory_space=pl.ANY)
    return ([anyspace] * n, [anyspace] * n, [jax.ShapeDtypeStruct(b.shape, b.dtype) for b in blocks],
            [pltpu.SemaphoreType.DMA((n, 7)), pltpu.SemaphoreType.DMA((n, 7)), pltpu.SemaphoreType.DMA((n,))])


def _sum_slots(x, *, name):
    _, r, c = x.shape
    tr = _tile(r, ROW_TILE)

    def body(x_ref, o_ref):
        acc = x_ref[0].astype(F32)
        for s in range(1, N_DEV):
            acc = acc + x_ref[s].astype(F32)
        o_ref[...] = acc

    return pl.pallas_call(
        body, name=name, grid=(r // tr,), in_specs=[pl.BlockSpec((N_DEV, tr, c), lambda i: (0, i, 0))],
        out_specs=pl.BlockSpec((tr, c), lambda i: (i, 0)), out_shape=jax.ShapeDtypeStruct((r, c), F32),
        compiler_params=_cparams("parallel"),
    )(x)


SB_PAIRS = SB_HEADS // 2
SB_PAIR_QBLOCK = 512
SB_PAIR_QBLOCK_FWD = 512


def _sb_pair_blocks(t, pref=SB_PAIR_QBLOCK):
    bq = _tile(t, pref)
    return bq, _tile(bq, SB_KBLOCK)


def _sb_pair_specs(t, bq):
    base = C_SBQKV // 128
    return [pl.BlockSpec((bq, 128), lambda p, i: (i, base + p)),
            pl.BlockSpec((t, 128), lambda p, i: (0, base + SB_PAIRS + p)),
            pl.BlockSpec((t, 128), lambda p, i: (0, base + 2 * SB_PAIRS + p))]


def _halves(x, first):
    zero = jnp.zeros_like(x)
    return [jnp.where(first, x, zero), jnp.where(first, zero, x)]


def _sb_mask(qi, kb, bq, bk):
    t_idx = qi * bq + lax.broadcasted_iota(jnp.int32, (bq, bk), 0)
    s_idx = kb * bk + lax.broadcasted_iota(jnp.int32, (bq, bk), 1)
    return s_idx < t_idx


SB_SCALE = SB_DIM ** -0.5


def _sb_pair_scores(qh, kblk, mask):
    zs = _dot_each(qh, [kblk, kblk], 1, 1)
    es = [jnp.exp(-jnp.abs(z)) for z in zs]
    sps = [jnp.maximum(z, 0.0) + jnp.log(1.0 + e) for z, e in zip(zs, es)]
    if mask is not None:
        sps = [jnp.where(mask, sp, 0.0) for sp in sps]
    return zs, es, sps


def _sb_atts(zs, csums, laters, mask):
    atts = [jnp.exp(z - c - l) for z, c, l in zip(zs, csums, laters)]
    return atts if mask is None else [jnp.where(mask, a, 0.0) for a in atts]


def _scaled_queries(q_ref, first):
    return _halves(q_ref[...] * jnp.asarray(SB_SCALE, q_ref.dtype), first)


def _running_sums(x_list, m):
    return [_dot(x, m) for x in x_list]


def _sb_pair_fwd(proj, shards=(), *, name):
    t = proj.shape[0]
    bq, bk = _sb_pair_blocks(t, SB_PAIR_QBLOCK_FWD)
    n = len(shards)
    nq = t // bq
    nsteps = SB_PAIRS * nq

    def body(q_ref, k_ref, v_ref, *rest):
        o_ref = rest[n]
        qi = pl.program_id(1)
        if n:
            step_no = pl.program_id(0) * nq + qi
            start, pass_on, finish = _gather_phases(rest[:n], rest[n + 1:2 * n + 1], *rest[2 * n + 1:])
            pl.when(step_no == 0)(start)
            pl.when(step_no == (2 * nsteps) // 3)(pass_on)
        first = lax.broadcasted_iota(jnp.int32, (1, 128), 1) < SB_DIM
        qh = _scaled_queries(q_ref, first)
        suffix = _suffix_ones(bk)
        band = bq // bk
        nkb = (qi + 1) * band

        def make_step(masked):
            def step(it, carry):
                later0, later1, acc = carry
                kb = nkb - 1 - it
                rows = pl.ds(pl.multiple_of(kb * bk, bk), bk)
                mask = _sb_mask(qi, kb, bq, bk) if masked else None
                zs, _, sps = _sb_pair_scores(qh, k_ref[rows, :], mask)
                atts = _sb_atts(zs, _running_sums(sps, suffix), (later0, later1), mask)
                outs = _dot_each(atts, _halves(v_ref[rows, :], first))
                return (later0 + jnp.sum(sps[0], axis=1, keepdims=True),
                        later1 + jnp.sum(sps[1], axis=1, keepdims=True), acc + (outs[0] + outs[1]))
            return step

        zero = jnp.zeros((bq, 1), F32)
        carry = lax.fori_loop(0, band, make_step(True), (zero, zero, jnp.zeros((bq, 128), F32)))
        _, _, acc = lax.fori_loop(band, nkb, make_step(False), carry)
        o_ref[...] = acc.astype(o_ref.dtype)
        if n:
            pl.when(step_no == nsteps - 1)(finish)

    more_in, more_out, more_shapes, sems = _gather_extras(shards) if n else ([], [], [], [])
    res = pl.pallas_call(
        body, name=name, grid=(SB_PAIRS, nq), in_specs=_sb_pair_specs(t, bq) + more_in,
        out_specs=[pl.BlockSpec((bq, 128), lambda p, i: (i, p))] + more_out,
        out_shape=[jax.ShapeDtypeStruct((t, SB_W), BF16)] + more_shapes, scratch_shapes=sems,
        compiler_params=_cparams("arbitrary", "arbitrary"),
    )(proj, proj, proj, *shards)
    return res[0], list(res[1:])


def _sb_pair_bwd(proj, dy, blocks=(), *, name):
    t = proj.shape[0]
    bq, bk = _sb_pair_blocks(t)
    nq = t // bq
    n = len(blocks)

    def body(q_ref, k_ref, v_ref, do_ref, *rest):
        dq_ref, dk_ref, dv_ref = rest[n:n + 3]
        dl_keep, sig_keep, dk_acc, dv_acc = rest[2 * n + 3:2 * n + 7]
        qi = pl.program_id(1)
        if n:
            step_no = pl.program_id(0) * nq + qi
            start, finish = _scatter_phases(rest[:n], rest[n + 3:2 * n + 3], *rest[2 * n + 7:])
            pl.when(step_no == 0)(start)

        @pl.when(qi == 0)
        def _():
            dk_acc[...] = jnp.zeros_like(dk_acc)
            dv_acc[...] = jnp.zeros_like(dv_acc)

        first = lax.broadcasted_iota(jnp.int32, (1, 128), 1) < SB_DIM
        qh = _scaled_queries(q_ref, first)
        doh = _halves(do_ref[...], first)
        suffix = _suffix_ones(bk)
        prefix = _prefix_ones(bk)
        band = bq // bk
        nkb = (qi + 1) * band

        def make_back(masked):
            def back(it, carry):
                kb = nkb - 1 - it
                rows = pl.ds(pl.multiple_of(kb * bk, bk), bk)
                vblk = v_ref[rows, :]
                mask = _sb_mask(qi, kb, bq, bk) if masked else None
                zs, es, sps = _sb_pair_scores(qh, k_ref[rows, :], mask)
                atts = _sb_atts(zs, _running_sums(sps, suffix), carry, mask)
                dvs = _dot_each(atts, doh, 0, 0)
                datts = _dot_each(doh, [vblk, vblk], 1, 1)
                dv_acc[rows, :] += dvs[0] + dvs[1]
                for hh in range(2):
                    sig = jnp.where(zs[hh] >= 0, 1.0, es[hh]) * pl.reciprocal(1.0 + es[hh], approx=True)
                    dl_keep[hh, kb] = (atts[hh] * datts[hh]).astype(dl_keep.dtype)
                    sig_keep[hh, kb] = (sig if mask is None else jnp.where(mask, sig, 0.0)).astype(sig_keep.dtype)
                return tuple(l + jnp.sum(sp, axis=1, keepdims=True) for l, sp in zip(carry, sps))
            return back

        zero = jnp.zeros((bq, 1), F32)
        lax.fori_loop(band, nkb, make_back(False), lax.fori_loop(0, band, make_back(True), (zero, zero)))

        def forth(kb, carry):
            before0, before1, dq = carry
            rows = pl.ds(pl.multiple_of(kb * bk, bk), bk)
            kept = [dl_keep[hh, kb] for hh in range(2)]
            sums = _running_sums(kept, prefix)
            dls = [x.astype(F32) for x in kept]
            dzs = [dl - sig_keep[hh, kb].astype(F32) * (b + s)
                   for hh, (dl, b, s) in enumerate(zip(dls, (before0, before1), sums))]
            dks = _dot_each(dzs, qh, 0, 0)
            dqs = _dot_each(dzs, _halves(k_ref[rows, :], first))
            dk_acc[rows, :] += dks[0] + dks[1]
            return (before0 + jnp.sum(dls[0], axis=1, keepdims=True), before1 + jnp.sum(dls[1], axis=1, keepdims=True),
                    dq + (dqs[0] + dqs[1]))

        _, _, dq = lax.fori_loop(0, nkb, forth, (zero, zero, jnp.zeros((bq, 128), F32)))
        dq_ref[...] = (dq * SB_SCALE).astype(dq_ref.dtype)

        @pl.when(qi == nq - 1)
        def _():
            dk_ref[...] = dk_acc[...].astype(dk_ref.dtype)
            dv_ref[...] = dv_acc[...].astype(dv_ref.dtype)

        if n:
            pl.when(step_no == SB_PAIRS * nq - 1)(finish)

    qspec = pl.BlockSpec((bq, 128), lambda p, i: (i, p))
    kvspec = pl.BlockSpec((t, 128), lambda p, i: (0, p))
    shape = jax.ShapeDtypeStruct((t, SB_W), BF16)
    more_in, more_out, more_shapes, sems = _scatter_extras(blocks) if n else ([], [], [], [])
    res = pl.pallas_call(
        body, name=name, grid=(SB_PAIRS, nq), in_specs=_sb_pair_specs(t, bq) + [qspec] + more_in,
        out_specs=[qspec, kvspec, kvspec] + more_out, out_shape=[shape] * 3 + more_shapes,
        scratch_shapes=[pltpu.VMEM((2, t // bk, bq, bk), BF16), pltpu.VMEM((2, t // bk, bq, bk), BF16),
                        pltpu.VMEM((t, 128), F32), pltpu.VMEM((t, 128), F32)] + sems,
        compiler_params=_cparams("arbitrary", "arbitrary"),
    )(proj, proj, proj, dy, *blocks)
    return res[0], res[1], res[2], list(res[3:])


def _heads_major(cols, heads, dim):
    t = cols.shape[0]
    return cols.reshape(t, heads, dim).transpose(1, 0, 2)


def _heads_minor(x):
    h, t, dim = x.shape
    return x.transpose(1, 0, 2).reshape(t, h * dim)


def _sb_qkv(proj):
    return [_heads_major(proj[:, C_SBQKV + i * SB_W:C_SBQKV + (i + 1) * SB_W], SB_HEADS, SB_DIM) for i in range(3)]


def _relu2_epilogue(r):
    a = jnp.maximum(r, 0.0)
    return r, a * a


def _relu2_bwd_epilogue(r, a):
    return (r * 2.0 * jnp.maximum(a.astype(F32), 0.0),)


def _layer_fwd(x, p, next_shards=()):
    h = _norm_fwd(x, p["norm_mix_pre"], out_dtype=BF16, name="norm_pre_fwd")
    proj = _mm(h, p["w_main"], name="mm_in")
    ab = _mm(h, p["w_ab"], out_dtypes=(F32,), name="mm_ab")
    qkv = _gdn_pre_fwd(proj, p["conv_qkv_w"], name="gdn_pre_fwd")
    a_log, dt_bias = p["gdn_a_log"].reshape(1, GDN_HEADS), p["gdn_dt_bias"].reshape(1, GDN_HEADS)
    u, w, qd, kd, aqk, gl = _gdn_local_fwd_staged(qkv, ab, a_log, dt_bias, name="gdn_local_fwd")
    o_gdn, states = _gdn_scan_fwd_staged(u, w, qd, kd, aqk, gl, name="gdn_scan_fwd")
    y_a = _gdn_post_fwd(o_gdn, proj, p["gdn_norm_w"], name="gdn_post_fwd")
    y_b, gathered = _sb_pair_fwd(proj, next_shards, name="sb_fwd")
    y_c = _sc_fwd(proj, p["conv_sc_w"], name="sc_fwd")
    ys = (y_a, y_b, y_c)
    ps = tuple(_mm(ys[b], p["w_branch"][b], name="mm_branch") for b in range(3))
    merged = _merge_fwd(ps, proj, name="merge_fwd")
    mo = _mm(merged, p["w_out"], out_dtypes=(F32,), name="mm_out")
    x1 = _norm_fwd(mo, p["norm_mix_post"], x, out_dtype=F32, name="norm_post_fwd")
    h2 = _norm_fwd(x1, p["norm_ffn_pre"], out_dtype=BF16, name="norm_pre_fwd")
    a1, r1 = _mm(h2, p["w_ff1"], out_dtypes=(BF16, BF16), epi=_relu2_epilogue, name="mm_ff1")
    f = _mm(r1, p["w_ff2"], out_dtypes=(F32,), name="mm_ff2")
    x2 = _norm_fwd(f, p["norm_ffn_post"], x1, out_dtype=F32, name="norm_post_fwd")
    saved = dict(x=x, h=h, proj=proj, ab=ab, qkv=qkv, u=u, w=w, qd=qd, kd=kd, aqk=aqk, gl=gl, o_gdn=o_gdn,
                 states=states, ys=ys, ps=ps, merged=merged, mo=mo, x1=x1, h2=h2,
                 a1=a1, r1=r1, f=f)
    return x2, saved, gathered


def _layer_bwd(dx2, p, s, blocks=()):
    g = {}
    df, g["norm_ffn_post"] = _norm_bwd(s["f"], p["norm_ffn_post"], dx2, out_dtype=BF16, name="norm_bwd_b")
    da1 = _mm(df, p["w_ff2"], tb=True, epi=_relu2_bwd_epilogue, extras=(s["a1"],), name="mm_ff2_dx")
    g["w_ff2"] = _mm(s["r1"], df, ta=True, name="mm_ff2_dw")
    g["w_ff1"] = _mm(s["h2"], da1, ta=True, name="mm_ff1_dw")
    dh2 = _mm(da1, p["w_ff1"], tb=True, out_dtypes=(F32,), name="mm_ff1_dx")
    dx1, g["norm_ffn_pre"] = _norm_bwd(s["x1"], p["norm_ffn_pre"], dh2, dx2, out_dtype=F32, name="norm_bwd_f")
    dmo, g["norm_mix_post"] = _norm_bwd(s["mo"], p["norm_mix_post"], dx1, out_dtype=BF16, name="norm_bwd_b")
    dmerged = _mm(dmo, p["w_out"], tb=True, name="mm_out_dx")
    g["w_out"] = _mm(s["merged"], dmo, ta=True, name="mm_out_dw")
    dps, dgates = _merge_bwd(s["ps"], s["proj"], dmerged, name="merge_bwd")
    dys = [_mm(dps[b], p["w_branch"][b], tb=True, name="mm_branch_dx") for b in range(3)]
    g["w_branch"] = jnp.stack([_mm(s["ys"][b], dps[b], ta=True, name="mm_branch_dw") for b in range(3)])
    dscx, dscb, dscc, g["conv_sc_w"] = _sc_bwd(s["proj"], p["conv_sc_w"], dys[2], name="sc_bwd")
    dsq, dsk, dsv, received = _sb_pair_bwd(s["proj"], dys[1], blocks, name="sb_bwd")
    a_log, dt_bias = p["gdn_a_log"].reshape(1, GDN_HEADS), p["gdn_dt_bias"].reshape(1, GDN_HEADS)
    do_gdn, dggate, g["gdn_norm_w"] = _gdn_post_bwd(s["o_gdn"], s["proj"], p["gdn_norm_w"], dys[0], name="gdn_post_bwd")
    du, dw, dqd, dkd, daqk, dgl = _gdn_scan_bwd_staged(s["u"], s["w"], s["qd"], s["kd"], s["aqk"], s["gl"],
                                                       s["states"], do_gdn, name="gdn_scan_bwd")
    dq, dk, dv, dab_h, dsc = _gdn_local_bwd_staged(s["qkv"], s["ab"], a_log, dt_bias, du, dw, dqd, dkd, daqk, dgl,
                                                   name="gdn_local_bwd")
    dsc = jnp.sum(dsc, axis=(1, 2))
    g["gdn_a_log"], g["gdn_dt_bias"] = dsc[:, 0], dsc[:, 1]
    dqkv = jnp.concatenate([dq, dk, dv], axis=0)
    dgqkv, g["conv_qkv_w"] = _gdn_pre_bwd(s["proj"], p["conv_qkv_w"], dqkv, name="gdn_pre_bwd")
    dab = jnp.sum(dab_h, axis=0).astype(BF16)
    dproj = jnp.concatenate([dgqkv, dggate, dsq, dsk, dsv, dscx, dscb, dscc, dgates], axis=1)
    g["w_main"] = _mm(s["h"], dproj, ta=True, name="mm_in_dw")
    g["w_ab"] = _mm(s["h"], dab, ta=True, out_dtypes=(F32,), name="mm_ab_dw")
    dh = _mm(dproj, p["w_main"], tb=True, out_dtypes=(F32,), name="mm_in_dx")
    dh_ab = _mm(dab, p["w_ab"], tb=True, out_dtypes=(F32,), name="mm_ab_dx")
    dx, g["norm_mix_pre"] = _norm_bwd(s["x"], p["norm_mix_pre"], dh + dh_ab, dx1, out_dtype=F32, name="norm_bwd_f")
    return dx, g, received


NORMS = ("norm_mix_pre", "norm_mix_post", "norm_ffn_pre", "norm_ffn_post")
SMALL = NORMS + ("gdn_a_log", "gdn_dt_bias", "gdn_norm_w")
CONVS = ("conv_qkv_w", "conv_sc_w")
AB_LO = 2048


def _split_w_in(w_in):
    main = jnp.concatenate([w_in[..., :AB_LO], w_in[..., AB_LO + 2 * GDN_HEADS:]], axis=-1)
    ab = w_in[..., AB_LO:AB_LO + 2 * GDN_HEADS]
    pad = [(0, 0)] * (ab.ndim - 1) + [(0, AB_W - 2 * GDN_HEADS)]
    return main, jnp.pad(ab, pad)


def _join_w_in(main, ab):
    return jnp.concatenate([main[..., :AB_LO], ab[..., :2 * GDN_HEADS].astype(main.dtype), main[..., AB_LO:]], axis=-1)


def kernel(x, norm_mix_pre, w_in, conv_qkv_w, gdn_a_log, gdn_dt_bias, gdn_norm_w, conv_sc_w, w_branch, w_out, norm_mix_post, norm_ffn_pre, w_ff1, w_ff2, norm_ffn_post, loss_target, m_norm_mix_pre, m_w_in, m_conv_qkv_w, m_gdn_a_log, m_gdn_dt_bias, m_gdn_norm_w, m_conv_sc_w, m_w_branch, m_w_out, m_norm_mix_post, m_norm_ffn_pre, m_w_ff1, m_w_ff2, m_norm_ffn_post, v_norm_mix_pre, v_w_in, v_conv_qkv_w, v_gdn_a_log, v_gdn_dt_bias, v_gdn_norm_w, v_conv_sc_w, v_w_branch, v_w_out, v_norm_mix_post, v_norm_ffn_pre, v_w_ff1, v_w_ff2, v_norm_ffn_post):
    names = ("norm_mix_pre", "w_in", "conv_qkv_w", "gdn_a_log", "gdn_dt_bias", "gdn_norm_w", "conv_sc_w", "w_branch",
             "w_out", "norm_mix_post", "norm_ffn_pre", "w_ff1", "w_ff2", "norm_ffn_post")
    w = dict(zip(names, (norm_mix_pre, w_in, conv_qkv_w, gdn_a_log, gdn_dt_bias, gdn_norm_w, conv_sc_w, w_branch,
                         w_out, norm_mix_post, norm_ffn_pre, w_ff1, w_ff2, norm_ffn_post)))
    m = dict(zip(names, (m_norm_mix_pre, m_w_in, m_conv_qkv_w, m_gdn_a_log, m_gdn_dt_bias, m_gdn_norm_w, m_conv_sc_w,
                         m_w_branch, m_w_out, m_norm_mix_post, m_norm_ffn_pre, m_w_ff1, m_w_ff2, m_norm_ffn_post)))
    v = dict(zip(names, (v_norm_mix_pre, v_w_in, v_conv_qkv_w, v_gdn_a_log, v_gdn_dt_bias, v_gdn_norm_w, v_conv_sc_w,
                         v_w_branch, v_w_out, v_norm_mix_post, v_norm_ffn_pre, v_w_ff1, v_w_ff2, v_norm_ffn_post)))
    me = 4 * lax.axis_index("x") + 2 * lax.axis_index("y") + lax.axis_index("c")

    conv_shapes = [w[k].shape for k in CONVS]
    conv_all, = _all_gather([_pack_vec([w[k] for k in CONVS])], name="gather_small")
    convs = {k: _to_global(blk, 2) for k, blk in zip(CONVS, _unpack_vec(conv_all, conv_shapes))}
    shards = [[w[k][l].astype(BF16) for k in BIG] for l in range(DEPTH)]

    def layer_params(l, gathered):
        p = {k: _to_global(blk, BIG_AXIS[k] - 1) for k, blk in zip(BIG, gathered)}
        p["w_main"], p["w_ab"] = _split_w_in(p.pop("w_in"))
        p.update({k: convs[k][l] for k in CONVS})
        p.update({k: w[k][l] for k in SMALL})
        return p

    xs = x[0]
    gathered = _all_gather(shards[0], name="gather_weights")
    layers, saved = [], []
    for l in range(DEPTH):
        layers.append(layer_params(l, gathered))
        xs, s, gathered = _layer_fwd(xs, layers[l], shards[l + 1] if l + 1 < DEPTH else ())
        saved.append(s)
    dy, loss_lanes = _loss_head(xs, loss_target[0], name="loss_head")

    grads, big_sums, blocks = [None] * DEPTH, [None] * DEPTH, ()
    for l in reversed(range(DEPTH)):
        dy, g, received = _layer_bwd(dy, layers[l], saved[l], blocks)
        if received:
            big_sums[l + 1] = [_sum_slots(r, name="rs_sum_slots") for r in received]
        g["w_in"] = _join_w_in(g.pop("w_main"), g.pop("w_ab"))
        blocks = [_rows(_to_blocks(g[k], BIG_AXIS[k] - 1)) for k in BIG]
        grads[l] = g
    got = _exchange_sibling(blocks, name="rs_sibling")
    parts = [_pair_sum(b, r, name="rs_pair_sum") for b, r in zip(blocks, got)]
    got2 = _exchange_chips(parts, name="rs_chips")
    big_sums[0] = [_final_sum(p, r, name="rs_final_sum") for p, r in zip(parts, got2)]
    gsum = {k: jnp.stack([big_sums[l][i] for l in range(DEPTH)]).reshape(w[k].shape) for i, k in enumerate(BIG)}
    stack = {k: jnp.stack([g[k] for g in grads]) for k in SMALL + CONVS}

    small_parts = [stack[k] for k in SMALL + CONVS] + [jnp.sum(loss_lanes).reshape(1)]
    small_shapes = [stack[k].shape for k in SMALL + CONVS] + [(1,)]
    summed = _sum_devices(_all_gather([_pack_vec(small_parts)], name="gather_small_grads")[0], name="sum_small")
    small = _unpack_vec(summed, small_shapes)
    loss = small[-1][0]
    for k, val in zip(SMALL + CONVS, small[:-1]):
        gsum[k] = val
    for k in CONVS:
        per = gsum[k].shape[2] // N_DEV
        gsum[k] = lax.dynamic_slice_in_dim(gsum[k], me * per, per, axis=2)

    delta, new_m, new_v = {}, {}, {}
    for k in names:
        shp = w[k].shape
        two_d = (-1, shp[-1]) if len(shp) > 1 else (1, -1)
        d_, m_, v_ = _adamw(w[k].reshape(two_d), gsum[k].reshape(two_d), m[k].reshape(two_d), v[k].reshape(two_d),
                            name="adamw")
        delta[k], new_m[k], new_v[k] = d_.reshape(shp), m_.reshape(shp), v_.reshape(shp)

    return (loss, dy[None], *[gsum[k].reshape(w[k].shape) for k in names], *[delta[k] for k in names], *[new_m[k] for k in names],
            *[new_v[k] for k in names])
```

```python
import functools

import jax
import jax.numpy as jnp
from jax import lax
from jax.experimental import pallas as pl
from jax.experimental.pallas import tpu as pltpu

F32, BF16 = jnp.float32, jnp.bfloat16
MESH_ID = pl.DeviceIdType.MESH

N_DEV = 8
DEPTH = 4
D_MODEL = 1024
D_FF = 4096
EPS = 1e-6
GDN_HEADS, GDN_DIM, GDN_CONV = 4, 128, 4
GDN_W = GDN_HEADS * GDN_DIM
CHUNK = 64
SB_HEADS, SB_DIM = 8, 64
SB_W = SB_HEADS * SB_DIM
SB_QBLOCK, SB_KBLOCK = 512, 256
SC_W, SC_CONV = 512, 3
IN_W = 8200
C_GQKV, C_GGATE, C_SBQKV, C_SCX, C_SCB, C_SCC, C_GATES, MAIN_W = 0, 1536, 2048, 3584, 4096, 4608, 5120, 8192
AB_W = 128

ADAM_LR, ADAM_B1, ADAM_B2, ADAM_EPS, ADAM_WD, ADAM_STEP = 0.001, 0.9, 0.999, 1e-08, 0.01, 10

VMEM_LIMIT = 48 * 2 ** 20


def _cparams(*sem):
    return pltpu.CompilerParams(dimension_semantics=sem or None, vmem_limit_bytes=VMEM_LIMIT)


def _tile(n, pref):
    if n <= pref:
        return n
    t = pref
    while n % t:
        t -= 128
    assert t > 0
    return t


def _dot(a, b, ca=1, cb=0):
    return lax.dot_general(a.astype(BF16), b.astype(BF16), (((ca,), (cb,)), ((), ())), preferred_element_type=F32)


def _split2(x):
    hi = x.astype(BF16)
    return hi, (x - hi.astype(F32)).astype(BF16)


def _dot3(a, b, ca=1, cb=0):
    a1, a2 = _split2(a)
    b1, b2 = _split2(b)
    return _dot(a1, b1, ca, cb) + (_dot(a1, b2, ca, cb) + _dot(a2, b1, ca, cb))


def _dot_exact(a, b, ca=1, cb=0, ones="a"):
    x = b if ones == "a" else a
    m = (a if ones == "a" else b).astype(BF16)
    hi, rest = x.astype(BF16), None
    rest = x - hi.astype(F32)
    mid = rest.astype(BF16)
    lo = (rest - mid.astype(F32)).astype(BF16)
    parts = [_dot(m, p, ca, cb) if ones == "a" else _dot(p, m, ca, cb) for p in (hi, mid, lo)]
    return parts[0] + (parts[1] + parts[2])


def _sigmoid(z):
    e = jnp.exp(-jnp.abs(z))
    return jnp.where(z >= 0, 1.0, e) / (1.0 + e)


def _softplus(z):
    return jnp.maximum(z, 0.0) + jnp.log(1.0 + jnp.exp(-jnp.abs(z)))


def _mm(a, b, *, name, ta=False, tb=False, out_dtypes=(BF16,), epi=None, extras=()):
    assert a.dtype == BF16 and b.dtype == BF16
    m, k = (a.shape[1], a.shape[0]) if ta else a.shape
    n = b.shape[0] if tb else b.shape[1]
    assert (b.shape[1] if tb else b.shape[0]) == k
    tm, tn, tk = _tile(m, 1024), _tile(n, 1024), _tile(k, 2048)
    nk = k // tk
    ca, cb = (0 if ta else 1), (1 if tb else 0)
    n_ex, n_out = len(extras), len(out_dtypes)

    def body(*refs):
        a_ref, b_ref = refs[0], refs[1]
        ex = refs[2:2 + n_ex]
        outs = refs[2 + n_ex:2 + n_ex + n_out]
        acc = refs[-1]
        kk = pl.program_id(2)
        part = lax.dot_general(a_ref[...], b_ref[...], (((ca,), (cb,)), ((), ())), preferred_element_type=F32)

        def finish(r):
            vals = (r,) if epi is None else epi(r, *[e[...] for e in ex])
            for o, v in zip(outs, vals):
                o[...] = v.astype(o.dtype)

        if nk == 1:
            finish(part)
        else:
            @pl.when(kk == 0)
            def _():
                acc[...] = part

            @pl.when(kk > 0)
            def _():
                acc[...] += part

            @pl.when(kk == nk - 1)
            def _():
                finish(acc[...])

    a_spec = pl.BlockSpec((tk, tm), lambda i, j, kk: (kk, i)) if ta else pl.BlockSpec((tm, tk), lambda i, j, kk: (i, kk))
    b_spec = pl.BlockSpec((tn, tk), lambda i, j, kk: (j, kk)) if tb else pl.BlockSpec((tk, tn), lambda i, j, kk: (kk, j))
    io_spec = pl.BlockSpec((tm, tn), lambda i, j, kk: (i, j))
    res = pl.pallas_call(
        body, name=name, grid=(m // tm, n // tn, nk),
        in_specs=[a_spec, b_spec] + [io_spec] * n_ex,
        out_specs=[io_spec] * n_out,
        out_shape=[jax.ShapeDtypeStruct((m, n), dt) for dt in out_dtypes],
        scratch_shapes=[pltpu.VMEM((tm, tn) if nk > 1 else (8, 128), F32)],
        compiler_params=_cparams("parallel", "parallel", "arbitrary"),
    )(a, b, *extras)
    return res[0] if n_out == 1 else res


ROW_TILE = 512


def _norm_fwd(y, w, res=None, *, out_dtype, name):
    t, d = y.shape
    tm = _tile(t, ROW_TILE)
    has_res = res is not None

    def body(*refs):
        y_ref, w_ref = refs[0], refs[1]
        o_ref = refs[-1]
        yv = y_ref[...]
        r = lax.rsqrt(jnp.mean(yv * yv, axis=-1, keepdims=True) + EPS)
        out = yv * r * w_ref[...]
        if has_res:
            out = out + refs[2][...]
        o_ref[...] = out.astype(o_ref.dtype)

    row = pl.BlockSpec((tm, d), lambda i: (i, 0))
    vec = pl.BlockSpec((1, d), lambda i: (0, 0))
    args = (y, w.reshape(1, d)) + ((res,) if has_res else ())
    return pl.pallas_call(
        body, name=name, grid=(t // tm,), in_specs=[row, vec] + [row] * has_res, out_specs=row,
        out_shape=jax.ShapeDtypeStruct((t, d), out_dtype), compiler_params=_cparams("parallel"),
    )(*args)


def _norm_bwd(y, w, dout, add=None, *, out_dtype, name):
    t, d = y.shape
    tm = _tile(t, ROW_TILE)
    has_add = add is not None

    def body(*refs):
        y_ref, w_ref, do_ref = refs[0], refs[1], refs[2]
        dy_ref, dw_ref = refs[-2], refs[-1]
        yv = y_ref[...]
        r = lax.rsqrt(jnp.mean(yv * yv, axis=-1, keepdims=True) + EPS)
        yh = yv * r
        dov = do_ref[...].astype(F32)
        gw = dov * w_ref[...]
        dy = r * (gw - yh * jnp.mean(gw * yh, axis=-1, keepdims=True))
        if has_add:
            dy = dy + refs[3][...]
        dy_ref[...] = dy.astype(dy_ref.dtype)
        part = jnp.sum(dov * yh, axis=0, keepdims=True)

        @pl.when(pl.program_id(0) == 0)
        def _():
            dw_ref[...] = part

        @pl.when(pl.program_id(0) > 0)
        def _():
            dw_ref[...] += part

    row = pl.BlockSpec((tm, d), lambda i: (i, 0))
    vec = pl.BlockSpec((1, d), lambda i: (0, 0))
    args = (y, w.reshape(1, d), dout) + ((add,) if has_add else ())
    return pl.pallas_call(
        body, name=name, grid=(t // tm,), in_specs=[row, vec, row] + [row] * has_add, out_specs=[row, vec],
        out_shape=[jax.ShapeDtypeStruct((t, d), out_dtype), jax.ShapeDtypeStruct((1, d), F32)],
        compiler_params=_cparams("arbitrary"),
    )(*args)


def _shift_down(u, s):
    if s == 0:
        return u
    rows = lax.broadcasted_iota(jnp.int32, u.shape, 0)
    return jnp.where(rows >= s, pltpu.roll(u, s, 0), 0.0)


def _shift_up(u, s):
    if s == 0:
        return u
    t = u.shape[0]
    rows = lax.broadcasted_iota(jnp.int32, u.shape, 0)
    return jnp.where(rows < t - s, pltpu.roll(u, t - s, 0), 0.0)


def _conv_fwd(u, w):
    kk = w.shape[0]
    out = u * w[kk - 1:kk, :]
    for i in range(kk - 1):
        out = out + _shift_down(u, kk - 1 - i) * w[i:i + 1, :]
    return out


def _conv_bwd(u, w, dc):
    kk = w.shape[0]
    du = dc * w[kk - 1:kk, :]
    dws = []
    for i in range(kk):
        s = kk - 1 - i
        if s:
            du = du + _shift_up(dc, s) * w[i:i + 1, :]
        dws.append(jnp.sum(dc * _shift_down(u, s), axis=0, keepdims=True))
    return du, dws


def _gdn_pre_math(x, w, slab):
    c = _conv_fwd(x, w)
    sig = _sigmoid(c)
    s = c * sig
    r = lax.rsqrt(jnp.sum(s * s, axis=-1, keepdims=True) + EPS)
    scale = jnp.where(slab < GDN_HEADS, GDN_DIM ** -0.5, 1.0)
    return c, sig, s, r, scale


def _gdn_pre_fwd(proj, conv_w, *, name):
    t = proj.shape[0]
    nslab = 3 * GDN_HEADS

    def body(x_ref, w_ref, o_ref):
        slab = pl.program_id(0)
        _, _, s, r, scale = _gdn_pre_math(x_ref[...].astype(F32), w_ref[...], slab)
        o_ref[0] = jnp.where(slab < 2 * GDN_HEADS, s * r * scale, s)

    return pl.pallas_call(
        body, name=name, grid=(nslab,),
        in_specs=[pl.BlockSpec((t, GDN_DIM), lambda j: (0, j)), pl.BlockSpec((GDN_CONV, GDN_DIM), lambda j: (0, j))],
        out_specs=pl.BlockSpec((1, t, GDN_DIM), lambda j: (j, 0, 0)),
        out_shape=jax.ShapeDtypeStruct((nslab, t, GDN_DIM), F32), compiler_params=_cparams("parallel"),
    )(proj, conv_w)


def _gdn_pre_bwd(proj, conv_w, dqkv, *, name):
    t = proj.shape[0]
    nslab = 3 * GDN_HEADS

    def body(x_ref, w_ref, d_ref, dx_ref, dw_ref):
        slab = pl.program_id(0)
        x = x_ref[...].astype(F32)
        w = w_ref[...]
        c, sig, s, r, scale = _gdn_pre_math(x, w, slab)
        dout = d_ref[0]
        yn = s * r
        dn = dout * scale
        ds_norm = r * (dn - yn * jnp.sum(dn * yn, axis=-1, keepdims=True))
        ds = jnp.where(slab < 2 * GDN_HEADS, ds_norm, dout)
        dc = ds * (sig + c * sig * (1.0 - sig))
        dx, dws = _conv_bwd(x, w, dc)
        dx_ref[...] = dx.astype(dx_ref.dtype)
        for i, dwi in enumerate(dws):
            dw_ref[i:i + 1, :] = dwi

    return pl.pallas_call(
        body, name=name, grid=(nslab,),
        in_specs=[pl.BlockSpec((t, GDN_DIM), lambda j: (0, j)), pl.BlockSpec((GDN_CONV, GDN_DIM), lambda j: (0, j)),
                  pl.BlockSpec((1, t, GDN_DIM), lambda j: (j, 0, 0))],
        out_specs=[pl.BlockSpec((t, GDN_DIM), lambda j: (0, j)), pl.BlockSpec((GDN_CONV, GDN_DIM), lambda j: (0, j))],
        out_shape=[jax.ShapeDtypeStruct((t, 3 * GDN_W), BF16), jax.ShapeDtypeStruct((GDN_CONV, 3 * GDN_W), F32)],
        compiler_params=_cparams("parallel"),
    )(proj, conv_w, dqkv)


def _sc_specs(t):
    def col(base):
        return pl.BlockSpec((t, 128), lambda j: (0, base // 128 + j))
    return [col(C_SCX), col(C_SCB), col(C_SCC), pl.BlockSpec((SC_CONV, 128), lambda j: (0, j))]


def _sc_fwd(proj, conv_w, *, name):
    t = proj.shape[0]

    def body(x_ref, b_ref, c_ref, w_ref, o_ref):
        u = c_ref[...].astype(F32) * x_ref[...].astype(F32)
        o_ref[...] = (b_ref[...].astype(F32) * _conv_fwd(u, w_ref[...])).astype(o_ref.dtype)

    return pl.pallas_call(
        body, name=name, grid=(SC_W // 128,), in_specs=_sc_specs(t),
        out_specs=pl.BlockSpec((t, 128), lambda j: (0, j)),
        out_shape=jax.ShapeDtypeStruct((t, SC_W), BF16), compiler_params=_cparams("parallel"),
    )(proj, proj, proj, conv_w)


def _sc_bwd(proj, conv_w, dy, *, name):
    t = proj.shape[0]
    nj = SC_W // 128

    def body(x_ref, b_ref, c_ref, w_ref, dy_ref, dx_ref, db_ref, dc_ref, dw_ref):
        x, b, c = x_ref[...].astype(F32), b_ref[...].astype(F32), c_ref[...].astype(F32)
        w = w_ref[...]
        u = c * x
        dyv = dy_ref[...].astype(F32)
        db_ref[...] = (dyv * _conv_fwd(u, w)).astype(db_ref.dtype)
        du, dws = _conv_bwd(u, w, dyv * b)
        dx_ref[...] = (du * c).astype(dx_ref.dtype)
        dc_ref[...] = (du * x).astype(dc_ref.dtype)
        for i, dwi in enumerate(dws):
            dw_ref[i:i + 1, :] = dwi

    return pl.pallas_call(
        body, name=name, grid=(nj,),
        in_specs=_sc_specs(t) + [pl.BlockSpec((t, 128), lambda j: (0, j))],
        out_specs=[pl.BlockSpec((t, 128), lambda j: (0, j))] * 3 + [pl.BlockSpec((SC_CONV, 128), lambda j: (0, j))],
        out_shape=[jax.ShapeDtypeStruct((t, SC_W), BF16)] * 3 + [jax.ShapeDtypeStruct((SC_CONV, SC_W), F32)],
        compiler_params=_cparams("parallel"),
    )(proj, proj, proj, conv_w, dy)


def _tri_inv(a_strict):
    c = a_strict.shape[0]
    ri = lax.broadcasted_iota(jnp.int32, (c, c), 0)
    ci = lax.broadcasted_iota(jnp.int32, (c, c), 1)
    eye = (ri == ci).astype(F32)
    blk = 8
    bm = -jnp.where(ri // blk == ci // blk, a_strict, 0.0)
    inv = eye + bm
    pw = bm
    for _ in range(2):
        pw = _dot3(pw, pw)
        inv = inv + _dot3(inv, pw)
    while blk < c:
        off = jnp.where((ri // (2 * blk) == ci // (2 * blk)) & (ri // blk != ci // blk), a_strict, 0.0)
        inv = inv - _dot3(_dot3(inv, off), inv)
        blk *= 2
    return inv


def _gdn_chunk(q, k, v, ab, head, ea, dtb):
    c = q.shape[0]
    lane = lax.broadcasted_iota(jnp.int32, ab.shape, 1)
    a = jnp.sum(jnp.where(lane == head, ab, 0.0), axis=1, keepdims=True)
    b = jnp.sum(jnp.where(lane == GDN_HEADS + head, ab, 0.0), axis=1, keepdims=True)
    ri = lax.broadcasted_iota(jnp.int32, (c, c), 0)
    ci = lax.broadcasted_iota(jnp.int32, (c, c), 1)
    tri, strict = ri >= ci, ri > ci
    ltri = tri.astype(F32)
    beta = _sigmoid(b)
    sig_a = _sigmoid(a + dtb)
    g = -ea * _softplus(a + dtb)
    g_cc = jnp.broadcast_to(g, (c, c))
    gi = _dot_exact(ltri, g_cc)
    gj = _dot_exact(g_cc, (ri <= ci).astype(F32), 0, 0, ones="b")
    decay = jnp.exp(jnp.where(tri, gi - gj, -1e30))
    gc = _dot_exact(ltri, jnp.broadcast_to(g, (c, GDN_DIM)))
    g_tot = jnp.sum(g, axis=0, keepdims=True)
    egc = jnp.exp(gc)
    ekd = jnp.exp(g_tot - gc)
    kb, vb = k * beta, v * beta
    kbg = kb * egc
    mkk = _dot3(kb, k, 1, 1)
    a_kk = jnp.where(strict, mkk * decay, 0.0)
    tinv = _tri_inv(a_kk)
    u = _dot3(tinv, vb)
    w = _dot3(tinv, kbg)
    mqk = _dot3(q, k, 1, 1)
    a_qk = jnp.where(tri, mqk * decay, 0.0)
    return dict(beta=beta, sig_a=sig_a, g=g, decay=decay, egc=egc, ekd=ekd, g_tot=g_tot, kb=kb, vb=vb, kbg=kbg,
                a_kk=a_kk, tinv=tinv, u=u, w=w, a_qk=a_qk, qd=q * egc, kd=k * ekd, tri=tri, strict=strict)


def _chunks_per_step(n):
    return 4 if n % 4 == 0 else 1


def _gdn_local_specs(t, cps):
    rows = cps * CHUNK

    def slab(base):
        return pl.BlockSpec((1, rows, GDN_DIM), lambda h, n: (base + h, n, 0))
    smem = pl.BlockSpec(memory_space=pltpu.SMEM)
    return [slab(0), slab(GDN_HEADS), slab(2 * GDN_HEADS), pl.BlockSpec((rows, AB_W), lambda h, n: (n, 0)), smem, smem]


def _scalar_row(ref, head):
    return jnp.full((1, 1), ref[0, head], F32)


def _gdn_local_fwd(qkv, ab, a_log, dt_bias, *, name):
    t = qkv.shape[1]
    n = t // CHUNK
    cps = _chunks_per_step(n)
    rows = cps * CHUNK

    def body(q_ref, k_ref, v_ref, ab_ref, al_ref, dt_ref, u_ref, w_ref, qd_ref, kd_ref, aqk_ref, gl_ref):
        head = pl.program_id(0)
        ea = jnp.exp(_scalar_row(al_ref, head))
        dtb = _scalar_row(dt_ref, head)
        for j in range(cps):
            sl = slice(j * CHUNK, (j + 1) * CHUNK)
            r = _gdn_chunk(q_ref[0, sl, :], k_ref[0, sl, :], v_ref[0, sl, :], ab_ref[sl, :], head, ea, dtb)
            u_ref[0, sl, :] = r["u"]
            w_ref[0, sl, :] = r["w"]
            qd_ref[0, sl, :] = r["qd"]
            kd_ref[0, sl, :] = r["kd"]
            aqk_ref[0, sl, :] = r["a_qk"]
            gl_ref[0, j] = jnp.exp(jnp.broadcast_to(r["g_tot"], (1, GDN_DIM)))

    big = pl.BlockSpec((1, rows, GDN_DIM), lambda h, i: (h, i, 0))
    big_shape = jax.ShapeDtypeStruct((GDN_HEADS, t, GDN_DIM), F32)
    return pl.pallas_call(
        body, name=name, grid=(GDN_HEADS, n // cps), in_specs=_gdn_local_specs(t, cps),
        out_specs=[big] * 4 + [pl.BlockSpec((1, rows, CHUNK), lambda h, i: (h, i, 0)),
                               pl.BlockSpec((1, cps, 1, GDN_DIM), lambda h, i: (h, i, 0, 0))],
        out_shape=[big_shape] * 4 + [jax.ShapeDtypeStruct((GDN_HEADS, t, CHUNK), F32),
                                     jax.ShapeDtypeStruct((GDN_HEADS, n, 1, GDN_DIM), F32)],
        compiler_params=_cparams("parallel", "parallel"),
    )(qkv, qkv, qkv, ab, a_log, dt_bias)


def _gdn_scan_fwd(u, w, qd, kd, aqk, gl, *, name):
    h, t, _ = u.shape
    n = t // CHUNK

    def body(u_ref, w_ref, qd_ref, kd_ref, aqk_ref, gl_ref, o_ref, s_ref, state):
        @pl.when(pl.program_id(0) == 0)
        def _():
            state[...] = jnp.zeros_like(state)

        for hh in range(h):
            s = state[hh]
            s_ref[hh, 0] = s
            vn = u_ref[hh] - _dot3(w_ref[hh], s)
            o_ref[hh] = _dot3(qd_ref[hh], s) + _dot3(aqk_ref[hh], vn)
            state[hh] = s * gl_ref[hh, 0] + _dot3(kd_ref[hh], vn, 0, 0)

    big = pl.BlockSpec((h, CHUNK, GDN_DIM), lambda i: (0, i, 0))
    return pl.pallas_call(
        body, name=name, grid=(n,),
        in_specs=[big] * 4 + [pl.BlockSpec((h, CHUNK, CHUNK), lambda i: (0, i, 0)),
                              pl.BlockSpec((h, 1, 1, GDN_DIM), lambda i: (0, i, 0, 0))],
        out_specs=[big, pl.BlockSpec((h, 1, GDN_DIM, GDN_DIM), lambda i: (0, i, 0, 0))],
        out_shape=[jax.ShapeDtypeStruct((h, t, GDN_DIM), F32), jax.ShapeDtypeStruct((h, n, GDN_DIM, GDN_DIM), F32)],
        scratch_shapes=[pltpu.VMEM((h, GDN_DIM, GDN_DIM), F32)],
        compiler_params=_cparams("arbitrary"),
    )(u, w, qd, kd, aqk, gl)


def _gdn_scan_bwd(u, w, qd, kd, aqk, gl, states, do, *, name):
    h, t, _ = u.shape
    n = t // CHUNK

    def body(u_ref, w_ref, qd_ref, kd_ref, aqk_ref, gl_ref, s_ref, do_ref,
             du_ref, dw_ref, dqd_ref, dkd_ref, daqk_ref, dgl_ref, dstate):
        @pl.when(pl.program_id(0) == 0)
        def _():
            dstate[...] = jnp.zeros_like(dstate)

        ri = lax.broadcasted_iota(jnp.int32, (CHUNK, CHUNK), 0)
        ci = lax.broadcasted_iota(jnp.int32, (CHUNK, CHUNK), 1)
        for hh in range(h):
            s, ds_next, dov, wv = s_ref[hh, 0], dstate[hh], do_ref[hh], w_ref[hh]
            vn = u_ref[hh] - _dot3(wv, s)
            dvn = _dot3(aqk_ref[hh], dov, 0, 0) + _dot3(kd_ref[hh], ds_next)
            du_ref[hh] = dvn
            dw_ref[hh] = -_dot3(dvn, s, 1, 1)
            dqd_ref[hh] = _dot3(dov, s, 1, 1)
            dkd_ref[hh] = _dot3(vn, ds_next, 1, 1)
            daqk_ref[hh] = jnp.where(ri >= ci, _dot3(dov, vn, 1, 1), 0.0)
            dgl_ref[hh, 0] = jnp.sum(ds_next * s, axis=0, keepdims=True)
            dstate[hh] = (_dot3(qd_ref[hh], dov, 0, 0) + ds_next * gl_ref[hh, 0]
                          - _dot3(wv, dvn, 0, 0))

    big = pl.BlockSpec((h, CHUNK, GDN_DIM), lambda i: (0, n - 1 - i, 0))
    sq = pl.BlockSpec((h, CHUNK, CHUNK), lambda i: (0, n - 1 - i, 0))
    glb = pl.BlockSpec((h, 1, 1, GDN_DIM), lambda i: (0, n - 1 - i, 0, 0))
    big_shape = jax.ShapeDtypeStruct((h, t, GDN_DIM), F32)
    return pl.pallas_call(
        body, name=name, grid=(n,),
        in_specs=[big] * 4 + [sq, glb, pl.BlockSpec((h, 1, GDN_DIM, GDN_DIM), lambda i: (0, n - 1 - i, 0, 0)), big],
        out_specs=[big] * 4 + [sq, glb],
        out_shape=[big_shape] * 4 + [jax.ShapeDtypeStruct((h, t, CHUNK), F32),
                                     jax.ShapeDtypeStruct((h, n, 1, GDN_DIM), F32)],
        scratch_shapes=[pltpu.VMEM((h, GDN_DIM, GDN_DIM), F32)],
        compiler_params=_cparams("arbitrary"),
    )(u, w, qd, kd, aqk, gl, states, do)


def _gdn_local_bwd(qkv, ab, a_log, dt_bias, du, dw, dqd, dkd, daqk, dgl, *, name):
    t = qkv.shape[1]
    n = t // CHUNK
    cps = _chunks_per_step(n)
    rows = cps * CHUNK

    def body(q_ref, k_ref, v_ref, ab_ref, al_ref, dt_ref, du_ref, dw_ref, dqd_ref, dkd_ref, daqk_ref, dgl_ref,
             dq_ref, dk_ref, dv_ref, dab_ref, dsc_ref):
        head = pl.program_id(0)
        ea = jnp.exp(_scalar_row(al_ref, head))
        dtb = _scalar_row(dt_ref, head)
        lane = lax.broadcasted_iota(jnp.int32, (CHUNK, AB_W), 1)
        lane1 = lax.broadcasted_iota(jnp.int32, (1, GDN_DIM), 1)
        ri = lax.broadcasted_iota(jnp.int32, (CHUNK, CHUNK), 0)
        ci = lax.broadcasted_iota(jnp.int32, (CHUNK, CHUNK), 1)
        utri = (ri <= ci).astype(F32)
        ones = jnp.ones((CHUNK, GDN_DIM), F32)
        acc_alog = jnp.zeros((1, 1), F32)
        acc_dtb = jnp.zeros((1, 1), F32)
        for j in range(cps):
            sl = slice(j * CHUNK, (j + 1) * CHUNK)
            q, k, v = q_ref[0, sl, :], k_ref[0, sl, :], v_ref[0, sl, :]
            r = _gdn_chunk(q, k, v, ab_ref[sl, :], head, ea, dtb)
            duv, dwv, dqdv, dkdv = du_ref[0, sl, :], dw_ref[0, sl, :], dqd_ref[0, sl, :], dkd_ref[0, sl, :]
            d_aqk = jnp.where(r["tri"], daqk_ref[0, sl, :], 0.0)
            dvb = _dot3(r["tinv"], duv, 0, 0)
            dkbg = _dot3(r["tinv"], dwv, 0, 0)
            d_akk = -jnp.where(r["strict"], _dot3(dvb, r["u"], 1, 1) + _dot3(dkbg, r["w"], 1, 1), 0.0)
            e = d_akk * r["a_kk"] + d_aqk * r["a_qk"]
            dmkk, dmqk = d_akk * r["decay"], d_aqk * r["decay"]
            dkb = _dot3(dmkk, k) + dkbg * r["egc"]
            dk = (_dot3(dmkk, r["kb"], 0, 0) + _dot3(dmqk, q, 0, 0) + dkdv * r["ekd"]
                  + dkb * r["beta"])
            dq = _dot3(dmqk, k) + dqdv * r["egc"]
            dq_ref[0, sl, :] = dq
            dk_ref[0, sl, :] = dk
            dv_ref[0, sl, :] = dvb * r["beta"]
            dbeta = jnp.sum(dkb * k + dvb * v, axis=1, keepdims=True)
            kd_term = jnp.sum(dkdv * r["kd"], axis=1, keepdims=True)
            dgc = (jnp.sum(e, axis=1, keepdims=True) + jnp.sum(dqdv * r["qd"] + dkbg * r["kbg"], axis=1, keepdims=True)
                   - kd_term)
            dgc_lanes = jnp.broadcast_to(dgc, (CHUNK, GDN_DIM)) - _dot_exact(e, ones, 0, 0, ones="b")
            dgl_tot = jnp.sum(dgl_ref[0, j], axis=1, keepdims=True) * jnp.exp(r["g_tot"])
            d_tot = jnp.sum(kd_term, axis=0, keepdims=True) + dgl_tot
            dg = _dot_exact(utri, dgc_lanes) + d_tot
            dg = jnp.sum(jnp.where(lane == 0, dg, 0.0), axis=1, keepdims=True)
            da = dg * (-ea) * r["sig_a"]
            db = dbeta * r["beta"] * (1.0 - r["beta"])
            dab_ref[0, sl, :] = jnp.where(lane == head, da, 0.0) + jnp.where(lane == GDN_HEADS + head, db, 0.0)
            acc_alog = acc_alog + jnp.sum(dg * r["g"], axis=0, keepdims=True)
            acc_dtb = acc_dtb + jnp.sum(da, axis=0, keepdims=True)
        dsc_ref[0, 0] = jnp.where(lane1 == 0, acc_alog, 0.0) + jnp.where(lane1 == 1, acc_dtb, 0.0)

    big = pl.BlockSpec((1, rows, GDN_DIM), lambda h, i: (h, i, 0))
    big_shape = jax.ShapeDtypeStruct((GDN_HEADS, t, GDN_DIM), F32)
    return pl.pallas_call(
        body, name=name, grid=(GDN_HEADS, n // cps),
        in_specs=_gdn_local_specs(t, cps) + [big] * 4 + [pl.BlockSpec((1, rows, CHUNK), lambda h, i: (h, i, 0)),
                                                        pl.BlockSpec((1, cps, 1, GDN_DIM), lambda h, i: (h, i, 0, 0))],
        out_specs=[big] * 4 + [pl.BlockSpec((1, 1, 1, GDN_DIM), lambda h, i: (h, i, 0, 0))],
        out_shape=[big_shape] * 4 + [jax.ShapeDtypeStruct((GDN_HEADS, n // cps, 1, GDN_DIM), F32)],
        compiler_params=_cparams("parallel", "parallel"),
    )(qkv, qkv, qkv, ab, a_log, dt_bias, du, dw, dqd, dkd, daqk, dgl)


def _gdn_post_fwd(o, proj, norm_w, *, name):
    h, t, _ = o.shape
    tm = _tile(t, ROW_TILE)

    def body(o_ref, g_ref, w_ref, y_ref):
        for hh in range(h):
            sl = slice(hh * GDN_DIM, (hh + 1) * GDN_DIM)
            ov = o_ref[hh]
            gate = g_ref[:, sl].astype(F32)
            r = lax.rsqrt(jnp.mean(ov * ov, axis=-1, keepdims=True) + EPS)
            y_ref[:, sl] = (ov * r * w_ref[...] * (gate * _sigmoid(gate))).astype(y_ref.dtype)

    return pl.pallas_call(
        body, name=name, grid=(t // tm,),
        in_specs=[pl.BlockSpec((h, tm, GDN_DIM), lambda i: (0, i, 0)),
                  pl.BlockSpec((tm, GDN_W), lambda i: (i, C_GGATE // GDN_W)),
                  pl.BlockSpec((1, GDN_DIM), lambda i: (0, 0))],
        out_specs=pl.BlockSpec((tm, GDN_W), lambda i: (i, 0)),
        out_shape=jax.ShapeDtypeStruct((t, GDN_W), BF16), compiler_params=_cparams("parallel"),
    )(o, proj, norm_w.reshape(1, GDN_DIM))


def _gdn_post_bwd(o, proj, norm_w, dy, *, name):
    h, t, _ = o.shape
    tm = _tile(t, ROW_TILE)

    def body(o_ref, g_ref, w_ref, dy_ref, do_ref, dg_ref, dw_ref):
        part = jnp.zeros((1, GDN_DIM), F32)
        for hh in range(h):
            sl = slice(hh * GDN_DIM, (hh + 1) * GDN_DIM)
            ov = o_ref[hh]
            gate = g_ref[:, sl].astype(F32)
            sig = _sigmoid(gate)
            silu = gate * sig
            r = lax.rsqrt(jnp.mean(ov * ov, axis=-1, keepdims=True) + EPS)
            oh = ov * r
            dyv = dy_ref[:, sl].astype(F32)
            dg_ref[:, sl] = (dyv * oh * w_ref[...] * (sig + silu * (1.0 - sig))).astype(dg_ref.dtype)
            dn = dyv * silu
            part = part + jnp.sum(dn * oh, axis=0, keepdims=True)
            gw = dn * w_ref[...]
            do_ref[hh] = r * (gw - oh * jnp.mean(gw * oh, axis=-1, keepdims=True))

        @pl.when(pl.program_id(0) == 0)
        def _():
            dw_ref[...] = part

        @pl.when(pl.program_id(0) > 0)
        def _():
            dw_ref[...] += part

    return pl.pallas_call(
        body, name=name, grid=(t // tm,),
        in_specs=[pl.BlockSpec((h, tm, GDN_DIM), lambda i: (0, i, 0)),
                  pl.BlockSpec((tm, GDN_W), lambda i: (i, C_GGATE // GDN_W)),
                  pl.BlockSpec((1, GDN_DIM), lambda i: (0, 0)),
                  pl.BlockSpec((tm, GDN_W), lambda i: (i, 0))],
        out_specs=[pl.BlockSpec((h, tm, GDN_DIM), lambda i: (0, i, 0)), pl.BlockSpec((tm, GDN_W), lambda i: (i, 0)),
                   pl.BlockSpec((1, GDN_DIM), lambda i: (0, 0))],
        out_shape=[jax.ShapeDtypeStruct((h, t, GDN_DIM), F32), jax.ShapeDtypeStruct((t, GDN_W), BF16),
                   jax.ShapeDtypeStruct((1, GDN_DIM), F32)],
        compiler_params=_cparams("arbitrary"),
    )(o, proj, norm_w.reshape(1, GDN_DIM), dy)


def _split_dot(x, m):
    hi = x.astype(BF16)
    lo = (x - hi.astype(F32)).astype(BF16)
    return _dot(hi, m) + _dot(lo, m)


def _sb_block(q, kblk, qi, kb):
    bq, bk = q.shape[0], kblk.shape[0]
    z = _dot(q, kblk, 1, 1) * (SB_DIM ** -0.5)
    t_idx = qi * bq + lax.broadcasted_iota(jnp.int32, (bq, bk), 0)
    s_idx = kb * bk + lax.broadcasted_iota(jnp.int32, (bq, bk), 1)
    mask = s_idx < t_idx
    e = jnp.exp(-jnp.abs(z))
    sp = jnp.where(mask, jnp.maximum(z, 0.0) + jnp.log(1.0 + e), 0.0)
    return z, mask, e, sp


def _suffix_ones(blk):
    ri = lax.broadcasted_iota(jnp.int32, (blk, blk), 0)
    ci = lax.broadcasted_iota(jnp.int32, (blk, blk), 1)
    return (ri >= ci).astype(BF16)


def _prefix_ones(blk):
    ri = lax.broadcasted_iota(jnp.int32, (blk, blk), 0)
    ci = lax.broadcasted_iota(jnp.int32, (blk, blk), 1)
    return (ri <= ci).astype(BF16)


def _sb_blocks(t):
    bq = _tile(t, SB_QBLOCK)
    bk = _tile(bq, SB_KBLOCK)
    return bq, bk


def _sb_fwd(q, k, v, *, name):
    h, t, d = q.shape
    bq, bk = _sb_blocks(t)

    def body(q_ref, k_ref, v_ref, o_ref):
        qi = pl.program_id(1)
        qv = q_ref[0]
        suffix = _suffix_ones(bk)
        nkb = (qi + 1) * (bq // bk)

        def step(it, carry):
            later, acc = carry
            kb = nkb - 1 - it
            rows = pl.ds(pl.multiple_of(kb * bk, bk), bk)
            z, mask, _, sp = _sb_block(qv, k_ref[0, rows, :], qi, kb)
            csum = _split_dot(sp, suffix)
            att = jnp.where(mask, jnp.exp(z - csum - later), 0.0)
            acc = acc + _dot(att, v_ref[0, rows, :])
            return later + jnp.sum(sp, axis=1, keepdims=True), acc

        _, acc = lax.fori_loop(0, nkb, step, (jnp.zeros((bq, 1), F32), jnp.zeros((bq, d), F32)))
        o_ref[0] = acc

    qspec = pl.BlockSpec((1, bq, d), lambda hh, i: (hh, i, 0))
    kvspec = pl.BlockSpec((1, t, d), lambda hh, i: (hh, 0, 0))
    return pl.pallas_call(
        body, name=name, grid=(h, t // bq), in_specs=[qspec, kvspec, kvspec], out_specs=qspec,
        out_shape=jax.ShapeDtypeStruct((h, t, d), F32), compiler_params=_cparams("parallel", "parallel"),
    )(q, k, v)


def _sb_bwd(q, k, v, do, *, name):
    h, t, d = q.shape
    bq, bk = _sb_blocks(t)
    scale = SB_DIM ** -0.5

    def body(q_ref, k_ref, v_ref, do_ref, dq_ref, dk_ref, dv_ref, dl_keep, sig_keep):
        qi = pl.program_id(1)

        @pl.when(qi == 0)
        def _():
            dk_ref[...] = jnp.zeros_like(dk_ref)
            dv_ref[...] = jnp.zeros_like(dv_ref)

        qv = q_ref[0]
        dov = do_ref[0]
        suffix = _suffix_ones(bk)
        prefix = _prefix_ones(bk)
        nkb = (qi + 1) * (bq // bk)

        def back(it, later):
            kb = nkb - 1 - it
            rows = pl.ds(pl.multiple_of(kb * bk, bk), bk)
            vblk = v_ref[0, rows, :]
            z, mask, e, sp = _sb_block(qv, k_ref[0, rows, :], qi, kb)
            csum = _split_dot(sp, suffix)
            att = jnp.where(mask, jnp.exp(z - csum - later), 0.0)
            dv_ref[0, rows, :] += _dot(att, dov, 0, 0)
            dl_keep[kb] = att * _dot(dov, vblk, 1, 1)
            sig_keep[kb] = jnp.where(mask, jnp.where(z >= 0, 1.0, e) / (1.0 + e), 0.0)
            return later + jnp.sum(sp, axis=1, keepdims=True)

        lax.fori_loop(0, nkb, back, jnp.zeros((bq, 1), F32))

        def forth(kb, carry):
            before, dq = carry
            rows = pl.ds(pl.multiple_of(kb * bk, bk), bk)
            dl = dl_keep[kb]
            dz = (dl - sig_keep[kb] * (before + _split_dot(dl, prefix))) * scale
            dk_ref[0, rows, :] += _dot(dz, qv, 0, 0)
            return before + jnp.sum(dl, axis=1, keepdims=True), dq + _dot(dz, k_ref[0, rows, :])

        _, dq = lax.fori_loop(0, nkb, forth, (jnp.zeros((bq, 1), F32), jnp.zeros((bq, d), F32)))
        dq_ref[0] = dq

    qspec = pl.BlockSpec((1, bq, d), lambda hh, i: (hh, i, 0))
    kvspec = pl.BlockSpec((1, t, d), lambda hh, i: (hh, 0, 0))
    shape = jax.ShapeDtypeStruct((h, t, d), F32)
    return pl.pallas_call(
        body, name=name, grid=(h, t // bq), in_specs=[qspec, kvspec, kvspec, qspec],
        out_specs=[qspec, kvspec, kvspec], out_shape=[shape] * 3,
        scratch_shapes=[pltpu.VMEM((t // bk, bq, bk), F32), pltpu.VMEM((t // bk, bq, bk), F32)],
        compiler_params=_cparams("parallel", "arbitrary"),
    )(q, k, v, do)


def _gate_specs(tm):
    return [pl.BlockSpec((tm, D_MODEL), lambda i, b=b: (i, C_GATES // D_MODEL + b)) for b in range(3)]


def _merge_fwd(p, proj, *, name):
    t = proj.shape[0]
    tm = _tile(t, ROW_TILE)

    def body(p0, p1, p2, g0, g1, g2, o_ref):
        acc = jnp.zeros((tm, D_MODEL), F32)
        for pr, gr in ((p0, g0), (p1, g1), (p2, g2)):
            acc = acc + _sigmoid(gr[...].astype(F32)) * pr[...].astype(F32)
        o_ref[...] = acc.astype(o_ref.dtype)

    row = pl.BlockSpec((tm, D_MODEL), lambda i: (i, 0))
    return pl.pallas_call(
        body, name=name, grid=(t // tm,), in_specs=[row] * 3 + _gate_specs(tm), out_specs=row,
        out_shape=jax.ShapeDtypeStruct((t, D_MODEL), BF16), compiler_params=_cparams("parallel"),
    )(*p, proj, proj, proj)


def _merge_bwd(p, proj, dmerged, *, name):
    t = proj.shape[0]
    tm = _tile(t, ROW_TILE)

    def body(p0, p1, p2, g0, g1, g2, dm_ref, dp0, dp1, dp2, dg_ref):
        dm = dm_ref[...].astype(F32)
        for b, (pr, gr, dpr) in enumerate(((p0, g0, dp0), (p1, g1, dp1), (p2, g2, dp2))):
            s = _sigmoid(gr[...].astype(F32))
            dpr[...] = (dm * s).astype(dpr.dtype)
            dg_ref[:, b * D_MODEL:(b + 1) * D_MODEL] = (dm * pr[...].astype(F32) * s * (1.0 - s)).astype(dg_ref.dtype)

    row = pl.BlockSpec((tm, D_MODEL), lambda i: (i, 0))
    res = pl.pallas_call(
        body, name=name, grid=(t // tm,), in_specs=[row] * 3 + _gate_specs(tm) + [row],
        out_specs=[row] * 3 + [pl.BlockSpec((tm, 3 * D_MODEL), lambda i: (i, 0))],
        out_shape=[jax.ShapeDtypeStruct((t, D_MODEL), BF16)] * 3 + [jax.ShapeDtypeStruct((t, 3 * D_MODEL), BF16)],
        compiler_params=_cparams("parallel"),
    )(*p, proj, proj, proj, dmerged)
    return res[:3], res[3]


def _loss_head(y, target, *, name):
    t, d = y.shape
    tm = _tile(t, ROW_TILE)

    def body(y_ref, t_ref, dy_ref, l_ref):
        err = y_ref[...] - t_ref[...]
        dy_ref[...] = err * (1.0 / d)
        part = jnp.sum(err * err, axis=0, keepdims=True) * (0.5 / d)

        @pl.when(pl.program_id(0) == 0)
        def _():
            l_ref[...] = part

        @pl.when(pl.program_id(0) > 0)
        def _():
            l_ref[...] += part

    row = pl.BlockSpec((tm, d), lambda i: (i, 0))
    vec = pl.BlockSpec((1, d), lambda i: (0, 0))
    return pl.pallas_call(
        body, name=name, grid=(t // tm,), in_specs=[row, row], out_specs=[row, vec],
        out_shape=[jax.ShapeDtypeStruct((t, d), F32), jax.ShapeDtypeStruct((1, d), F32)],
        compiler_params=_cparams("arbitrary"),
    )(y, target)


def _adamw(w, g, m, v, *, name):
    r, c = w.shape
    tr = r if r * c * 4 <= 2 ** 21 else max(8, (2 ** 21 // (c * 4)) // 8 * 8)
    while r % tr:
        tr -= 8
    c1 = 1.0 - ADAM_B1 ** ADAM_STEP
    c2 = 1.0 - ADAM_B2 ** ADAM_STEP

    def body(w_ref, g_ref, m_ref, v_ref, d_ref, nm_ref, nv_ref):
        gv = g_ref[...]
        nm = ADAM_B1 * m_ref[...] + (1.0 - ADAM_B1) * gv
        nv = ADAM_B2 * v_ref[...] + (1.0 - ADAM_B2) * (gv * gv)
        nm_ref[...] = nm
        nv_ref[...] = nv
        d_ref[...] = -ADAM_LR * ((nm / c1) / (jnp.sqrt(nv / c2) + ADAM_EPS) + ADAM_WD * w_ref[...])

    spec = pl.BlockSpec((tr, c), lambda i: (i, 0))
    return pl.pallas_call(
        body, name=name, grid=(r // tr,), in_specs=[spec] * 4, out_specs=[spec] * 3,
        out_shape=[jax.ShapeDtypeStruct((r, c), F32)] * 3, compiler_params=_cparams("parallel"),
    )(w, g, m, v)


def _all_gather(xs, *, name):
    n = len(xs)

    def body(*refs):
        x_refs, out_refs = refs[:n], refs[n:2 * n]
        send_sems, recv_sems, local_sems = refs[2 * n:]
        mx, my, mc = lax.axis_index("x"), lax.axis_index("y"), lax.axis_index("c")
        me, sibling = (mx, my, mc), (mx, my, 1 - mc)
        chips = [(1 - mx, my), (mx, 1 - my), (1 - mx, 1 - my)]

        def slot(a, px, py, pc):
            return out_refs[a].at[4 * px + 2 * py + pc]

        def copy(a, k, block, to, src=None):
            return pltpu.make_async_remote_copy(
                src_ref=slot(a, *block) if src is None else src, dst_ref=slot(a, *block),
                send_sem=send_sems.at[a, k], recv_sem=recv_sems.at[a, k], device_id=to, device_id_type=MESH_ID)

        mine = [pltpu.make_async_copy(x_refs[a], slot(a, *me), local_sems.at[a]) for a in range(n)]
        for cp in mine:
            cp.start()
        first = [copy(a, 1 + j, me, (*chip, mc), src=x_refs[a]) for j, chip in enumerate(chips) for a in range(n)]
        first += [copy(a, 0, me, sibling, src=x_refs[a]) for a in range(n)]
        for cp in first:
            cp.start()
        passed = []
        for j, chip in enumerate(chips):
            for a in range(n):
                copy(a, 1 + j, (*chip, mc), me).wait_recv()
                passed.append(copy(a, 4 + j, (*chip, mc), sibling))
                passed[-1].start()
        for a in range(n):
            copy(a, 0, sibling, me).wait_recv()
        for j, chip in enumerate(chips):
            for a in range(n):
                copy(a, 4 + j, (*chip, 1 - mc), me).wait_recv()
        for cp in first + passed:
            cp.wait_send()
        for cp in mine:
            cp.wait()

    anyspace = pl.BlockSpec(memory_space=pl.ANY)
    return pl.pallas_call(
        body, name=name, in_specs=[anyspace] * n, out_specs=[anyspace] * n,
        out_shape=[jax.ShapeDtypeStruct((N_DEV,) + x.shape, x.dtype) for x in xs],
        scratch_shapes=[pltpu.SemaphoreType.DMA((n, 7)), pltpu.SemaphoreType.DMA((n, 7)), pltpu.SemaphoreType.DMA((n,))],
    )(*xs)


def _exchange_sibling(gs, *, name):
    n = len(gs)

    def body(*refs):
        g_refs, out_refs = refs[:n], refs[n:2 * n]
        send_sems, recv_sems = refs[2 * n:]
        mx, my, mc = lax.axis_index("x"), lax.axis_index("y"), lax.axis_index("c")
        sibling = (mx, my, 1 - mc)
        copies = []
        for a in range(n):
            for px in range(2):
                for py in range(2):
                    kk = 2 * px + py
                    copies.append(pltpu.make_async_remote_copy(
                        src_ref=g_refs[a].at[4 * px + 2 * py + (1 - mc)], dst_ref=out_refs[a].at[kk],
                        send_sem=send_sems.at[a, kk], recv_sem=recv_sems.at[a, kk], device_id=sibling,
                        device_id_type=MESH_ID))
        for cp in copies:
            cp.start()
        for cp in copies:
            cp.wait_recv()
        for cp in copies:
            cp.wait_send()

    anyspace = pl.BlockSpec(memory_space=pl.ANY)
    return pl.pallas_call(
        body, name=name, in_specs=[anyspace] * n, out_specs=[anyspace] * n,
        out_shape=[jax.ShapeDtypeStruct((4,) + g.shape[1:], g.dtype) for g in gs],
        scratch_shapes=[pltpu.SemaphoreType.DMA((n, 4)), pltpu.SemaphoreType.DMA((n, 4))],
    )(*gs)


def _pair_sum(g, got, *, name):
    _, r, c = g.shape
    tr = _tile(r, ROW_TILE)

    def body(core_ref, a_ref, b_ref, o_ref):
        del core_ref
        o_ref[...] = (a_ref[...].astype(F32) + b_ref[...].astype(F32)).astype(o_ref.dtype)

    grid_spec = pltpu.PrefetchScalarGridSpec(
        num_scalar_prefetch=1, grid=(4, r // tr),
        in_specs=[pl.BlockSpec((1, tr, c), lambda kk, i, core: (2 * kk + core[0], i, 0)),
                  pl.BlockSpec((1, tr, c), lambda kk, i, core: (kk, i, 0))],
        out_specs=pl.BlockSpec((1, tr, c), lambda kk, i, core: (kk, i, 0)))
    return pl.pallas_call(
        body, name=name, grid_spec=grid_spec, out_shape=jax.ShapeDtypeStruct((4, r, c), g.dtype),
        compiler_params=_cparams("parallel", "parallel"),
    )(lax.axis_index("c").astype(jnp.int32).reshape(1), g, got)


def _exchange_chips(parts, *, name):
    n = len(parts)

    def body(*refs):
        p_refs, out_refs = refs[:n], refs[n:2 * n]
        send_sems, recv_sems = refs[2 * n:]
        mx, my, mc = lax.axis_index("x"), lax.axis_index("y"), lax.axis_index("c")
        chips = [(1 - mx, my), (mx, 1 - my), (1 - mx, 1 - my)]
        copies = [pltpu.make_async_remote_copy(
            src_ref=p_refs[a].at[2 * px + py], dst_ref=out_refs[a].at[j], send_sem=send_sems.at[a, j],
            recv_sem=recv_sems.at[a, j], device_id=(px, py, mc), device_id_type=MESH_ID)
            for j, (px, py) in enumerate(chips) for a in range(n)]
        for cp in copies:
            cp.start()
        for cp in copies:
            cp.wait_recv()
        for cp in copies:
            cp.wait_send()

    anyspace = pl.BlockSpec(memory_space=pl.ANY)
    return pl.pallas_call(
        body, name=name, in_specs=[anyspace] * n, out_specs=[anyspace] * n,
        out_shape=[jax.ShapeDtypeStruct((3,) + p.shape[1:], p.dtype) for p in parts],
        scratch_shapes=[pltpu.SemaphoreType.DMA((n, 3)), pltpu.SemaphoreType.DMA((n, 3))],
    )(*parts)


def _final_sum(part, got, *, name):
    _, r, c = part.shape
    tr = _tile(r, ROW_TILE)

    def body(chip_ref, a_ref, b_ref, o_ref):
        del chip_ref
        acc = a_ref[0].astype(F32)
        for j in range(3):
            acc = acc + b_ref[j].astype(F32)
        o_ref[...] = acc

    grid_spec = pltpu.PrefetchScalarGridSpec(
        num_scalar_prefetch=1, grid=(r // tr,),
        in_specs=[pl.BlockSpec((1, tr, c), lambda i, chip: (chip[0], i, 0)),
                  pl.BlockSpec((3, tr, c), lambda i, chip: (0, i, 0))],
        out_specs=pl.BlockSpec((tr, c), lambda i, chip: (i, 0)))
    chip = (2 * lax.axis_index("x") + lax.axis_index("y")).astype(jnp.int32).reshape(1)
    return pl.pallas_call(
        body, name=name, grid_spec=grid_spec, out_shape=jax.ShapeDtypeStruct((r, c), F32),
        compiler_params=_cparams("parallel"),
    )(chip, part, got)


def _sum_devices(x, *, name):
    _, r, c = x.shape

    def body(x_ref, o_ref):
        acc = x_ref[0]
        for j in range(1, N_DEV):
            acc = acc + x_ref[j]
        o_ref[...] = acc

    return pl.pallas_call(body, name=name, out_shape=jax.ShapeDtypeStruct((r, c), F32),
                          compiler_params=_cparams())(x)


BIG = ("w_in", "w_branch", "w_out", "w_ff1", "w_ff2")
BIG_AXIS = {"w_in": 2, "w_branch": 3, "w_out": 1, "w_ff1": 2, "w_ff2": 1}


def _to_global(blocks, axis):
    moved = jnp.moveaxis(blocks, 0, axis)
    shp = moved.shape
    return moved.reshape(shp[:axis] + (shp[axis] * shp[axis + 1],) + shp[axis + 2:])


def _to_blocks(full, axis):
    shp = full.shape
    split = full.reshape(shp[:axis] + (N_DEV, shp[axis] // N_DEV) + shp[axis + 1:])
    return jnp.moveaxis(split, axis, 0)


def _rows(blocks):
    return blocks.reshape(blocks.shape[0], -1, blocks.shape[-1])


def _vec_rows(n):
    return -(-n // 128 // 8) * 8


def _pack_vec(parts):
    flat = jnp.concatenate([p.reshape(-1).astype(F32) for p in parts])
    rows = _vec_rows(flat.shape[0])
    return jnp.pad(flat, (0, rows * 128 - flat.shape[0])).reshape(rows, 128)


def _unpack_vec(flat, shapes):
    lead = flat.shape[:-2]
    flat = flat.reshape(lead + (-1,))
    out, off = [], 0
    for s in shapes:
        n = 1
        for dim in s:
            n *= dim
        out.append(flat[..., off:off + n].reshape(lead + tuple(s)))
        off += n
    return out


def _dot_each(a_list, b_list, ca=1, cb=0):
    return [_dot(a, b, ca, cb) for a, b in zip(a_list, b_list)]


def _dot3_each(a_list, b_list, ca=1, cb=0):
    sa = [_split2(a) for a in a_list]
    sb = [_split2(b) for b in b_list]
    prods = [(_dot(a1, b1, ca, cb), _dot(a1, b2, ca, cb), _dot(a2, b1, ca, cb)) for (a1, a2), (b1, b2) in zip(sa, sb)]
    return [x + (y + z) for x, y, z in prods]


def _split3(x):
    hi = x.astype(BF16)
    rest = x - hi.astype(F32)
    mid = rest.astype(BF16)
    return hi, mid, (rest - mid.astype(F32)).astype(BF16)


def _ones_dot_each(m, x_list, ca=1, cb=0):
    mb = m.astype(BF16)
    parts = [[_dot(mb, p, ca, cb) for p in _split3(x)] for x in x_list]
    return [p[0] + (p[1] + p[2]) for p in parts]


def _dot_ones_each(x_list, m, ca=1, cb=0):
    mb = m.astype(BF16)
    parts = [[_dot(p, mb, ca, cb) for p in _split3(x)] for x in x_list]
    return [p[0] + (p[1] + p[2]) for p in parts]


def _tri_inv_each(a_list):
    c = a_list[0].shape[0]
    ri = lax.broadcasted_iota(jnp.int32, (c, c), 0)
    ci = lax.broadcasted_iota(jnp.int32, (c, c), 1)
    eye = (ri == ci).astype(F32)
    blk = 8
    pws = [-jnp.where(ri // blk == ci // blk, a, 0.0) for a in a_list]
    invs = [eye + b for b in pws]
    for _ in range(2):
        pws = _dot_each(pws, pws)
        invs = [i + u for i, u in zip(invs, _dot_each(invs, pws))]
    while blk < c:
        sel = (ri // (2 * blk) == ci // (2 * blk)) & (ri // blk != ci // blk)
        offs = [jnp.where(sel, a, 0.0) for a in a_list]
        invs = [i - t for i, t in zip(invs, _dot_each(_dot_each(invs, offs), invs))]
        blk *= 2
    resid = [eye - x for x in _dot3_each([eye + a for a in a_list], invs)]
    return [i + t for i, t in zip(invs, _dot_each(invs, resid))]


def _gdn_chunks(qs, ks, vs, abs_, head, ea, dtb):
    c = qs[0].shape[0]
    lane = lax.broadcasted_iota(jnp.int32, abs_[0].shape, 1)
    a_s = [jnp.sum(jnp.where(lane == head, ab, 0.0), axis=1, keepdims=True) for ab in abs_]
    b_s = [jnp.sum(jnp.where(lane == GDN_HEADS + head, ab, 0.0), axis=1, keepdims=True) for ab in abs_]
    ri = lax.broadcasted_iota(jnp.int32, (c, c), 0)
    ci = lax.broadcasted_iota(jnp.int32, (c, c), 1)
    tri, strict = ri >= ci, ri > ci
    ltri = tri.astype(F32)
    beta = [_sigmoid(b) for b in b_s]
    sig_a = [_sigmoid(a + dtb) for a in a_s]
    g = [-ea * _softplus(a + dtb) for a in a_s]
    g_cc = [jnp.broadcast_to(x, (c, c)) for x in g]
    gi = _ones_dot_each(ltri, g_cc)
    gj = _dot_ones_each(g_cc, (ri <= ci).astype(F32), 0, 0)
    decay = [jnp.exp(jnp.where(tri, x - y, -1e30)) for x, y in zip(gi, gj)]
    gc = _ones_dot_each(ltri, [jnp.broadcast_to(x, (c, GDN_DIM)) for x in g])
    g_tot = [jnp.sum(x, axis=0, keepdims=True) for x in g]
    egc = [jnp.exp(x) for x in gc]
    ekd = [jnp.exp(t - x) for t, x in zip(g_tot, gc)]
    kb = [k * b for k, b in zip(ks, beta)]
    vb = [v * b for v, b in zip(vs, beta)]
    kbg = [x * e for x, e in zip(kb, egc)]
    mkk = _dot_each(kb, ks, 1, 1)
    a_kk = [jnp.where(strict, m * d, 0.0) for m, d in zip(mkk, decay)]
    tinv = _tri_inv_each(a_kk)
    u = _dot_each(tinv, vb)
    w = _dot_each(tinv, kbg)
    mqk = _dot_each(qs, ks, 1, 1)
    a_qk = [jnp.where(tri, m * d, 0.0) for m, d in zip(mqk, decay)]
    qd = [q * e for q, e in zip(qs, egc)]
    kd = [k * e for k, e in zip(ks, ekd)]
    return dict(beta=beta, sig_a=sig_a, g=g, decay=decay, egc=egc, ekd=ekd, g_tot=g_tot, kb=kb, vb=vb, kbg=kbg,
                a_kk=a_kk, tinv=tinv, u=u, w=w, a_qk=a_qk, qd=qd, kd=kd, tri=tri, strict=strict)


def _chunks_in_step(n):
    for cps in (8, 4, 2):
        if n % cps == 0:
            return cps
    return 1


def _gdn_local_fwd_staged(qkv, ab, a_log, dt_bias, *, name):
    t = qkv.shape[1]
    n = t // CHUNK
    cps = _chunks_in_step(n)
    rows = cps * CHUNK
    sls = [slice(j * CHUNK, (j + 1) * CHUNK) for j in range(cps)]

    def body(q_ref, k_ref, v_ref, ab_ref, al_ref, dt_ref, u_ref, w_ref, qd_ref, kd_ref, aqk_ref, gl_ref):
        head = pl.program_id(0)
        ea = jnp.exp(_scalar_row(al_ref, head))
        dtb = _scalar_row(dt_ref, head)
        r = _gdn_chunks([q_ref[0, sl, :] for sl in sls], [k_ref[0, sl, :] for sl in sls],
                        [v_ref[0, sl, :] for sl in sls], [ab_ref[sl, :] for sl in sls], head, ea, dtb)
        for j, sl in enumerate(sls):
            u_ref[0, sl, :] = r["u"][j]
            w_ref[0, sl, :] = r["w"][j]
            qd_ref[0, sl, :] = r["qd"][j]
            kd_ref[0, sl, :] = r["kd"][j]
            aqk_ref[0, sl, :] = r["a_qk"][j]
            gl_ref[0, j] = jnp.exp(jnp.broadcast_to(r["g_tot"][j], (1, GDN_DIM)))

    big = pl.BlockSpec((1, rows, GDN_DIM), lambda h, i: (h, i, 0))
    big_shape = jax.ShapeDtypeStruct((GDN_HEADS, t, GDN_DIM), F32)
    return pl.pallas_call(
        body, name=name, grid=(GDN_HEADS, n // cps), in_specs=_gdn_local_specs(t, cps),
        out_specs=[big] * 4 + [pl.BlockSpec((1, rows, CHUNK), lambda h, i: (h, i, 0)),
                               pl.BlockSpec((1, cps, 1, GDN_DIM), lambda h, i: (h, i, 0, 0))],
        out_shape=[big_shape] * 4 + [jax.ShapeDtypeStruct((GDN_HEADS, t, CHUNK), F32),
                                     jax.ShapeDtypeStruct((GDN_HEADS, n, 1, GDN_DIM), F32)],
        compiler_params=_cparams("parallel", "parallel"),
    )(qkv, qkv, qkv, ab, a_log, dt_bias)


def _gdn_local_bwd_staged(qkv, ab, a_log, dt_bias, du, dw, dqd, dkd, daqk, dgl, *, name):
    t = qkv.shape[1]
    n = t // CHUNK
    cps = _chunks_in_step(n)
    rows = cps * CHUNK
    sls = [slice(j * CHUNK, (j + 1) * CHUNK) for j in range(cps)]

    def body(q_ref, k_ref, v_ref, ab_ref, al_ref, dt_ref, du_ref, dw_ref, dqd_ref, dkd_ref, daqk_ref, dgl_ref,
             dq_ref, dk_ref, dv_ref, dab_ref, dsc_ref):
        head = pl.program_id(0)
        ea = jnp.exp(_scalar_row(al_ref, head))
        dtb = _scalar_row(dt_ref, head)
        lane = lax.broadcasted_iota(jnp.int32, (CHUNK, AB_W), 1)
        lane1 = lax.broadcasted_iota(jnp.int32, (1, GDN_DIM), 1)
        ri = lax.broadcasted_iota(jnp.int32, (CHUNK, CHUNK), 0)
        ci = lax.broadcasted_iota(jnp.int32, (CHUNK, CHUNK), 1)
        utri = (ri <= ci).astype(F32)
        ones = jnp.ones((CHUNK, GDN_DIM), F32)
        qs, ks, vs = ([ref[0, sl, :] for sl in sls] for ref in (q_ref, k_ref, v_ref))
        r = _gdn_chunks(qs, ks, vs, [ab_ref[sl, :] for sl in sls], head, ea, dtb)
        duv, dwv, dqdv, dkdv = ([ref[0, sl, :] for sl in sls] for ref in (du_ref, dw_ref, dqd_ref, dkd_ref))
        d_aqk = [jnp.where(r["tri"], daqk_ref[0, sl, :], 0.0) for sl in sls]
        dvb = _dot_each(r["tinv"], duv, 0, 0)
        dkbg = _dot_each(r["tinv"], dwv, 0, 0)
        outer = [x + y for x, y in zip(_dot_each(dvb, r["u"], 1, 1), _dot_each(dkbg, r["w"], 1, 1))]
        d_akk = [-jnp.where(r["strict"], x, 0.0) for x in outer]
        e = [x * a + y * b for x, a, y, b in zip(d_akk, r["a_kk"], d_aqk, r["a_qk"])]
        dmkk = [x * d for x, d in zip(d_akk, r["decay"])]
        dmqk = [x * d for x, d in zip(d_aqk, r["decay"])]
        dkb = [x + y * eg for x, y, eg in zip(_dot_each(dmkk, ks), dkbg, r["egc"])]
        dk = [a + b + x * ek + y * bt for a, b, x, ek, y, bt in zip(
            _dot_each(dmkk, r["kb"], 0, 0), _dot_each(dmqk, qs, 0, 0), dkdv, r["ekd"], dkb, r["beta"])]
        dq = [a + x * eg for a, x, eg in zip(_dot_each(dmqk, ks), dqdv, r["egc"])]
        col_sums = _dot_ones_each(e, ones, 0, 0)
        acc_alog = jnp.zeros((1, 1), F32)
        acc_dtb = jnp.zeros((1, 1), F32)
        dgc_lanes, d_tots, dbetas = [], [], []
        for j in range(cps):
            dbetas.append(jnp.sum(dkb[j] * ks[j] + dvb[j] * vs[j], axis=1, keepdims=True))
            kd_term = jnp.sum(dkdv[j] * r["kd"][j], axis=1, keepdims=True)
            dgc = (jnp.sum(e[j], axis=1, keepdims=True)
                   + jnp.sum(dqdv[j] * r["qd"][j] + dkbg[j] * r["kbg"][j], axis=1, keepdims=True) - kd_term)
            dgc_lanes.append(jnp.broadcast_to(dgc, (CHUNK, GDN_DIM)) - col_sums[j])
            dgl_tot = jnp.sum(dgl_ref[0, j], axis=1, keepdims=True) * jnp.exp(r["g_tot"][j])
            d_tots.append(jnp.sum(kd_term, axis=0, keepdims=True) + dgl_tot)
        suffix_sums = _ones_dot_each(utri, dgc_lanes)
        for j, sl in enumerate(sls):
            dq_ref[0, sl, :] = dq[j]
            dk_ref[0, sl, :] = dk[j]
            dv_ref[0, sl, :] = dvb[j] * r["beta"][j]
            dg = jnp.sum(jnp.where(lane == 0, suffix_sums[j] + d_tots[j], 0.0), axis=1, keepdims=True)
            da = dg * (-ea) * r["sig_a"][j]
            db = dbetas[j] * r["beta"][j] * (1.0 - r["beta"][j])
            dab_ref[0, sl, :] = jnp.where(lane == head, da, 0.0) + jnp.where(lane == GDN_HEADS + head, db, 0.0)
            acc_alog = acc_alog + jnp.sum(dg * r["g"][j], axis=0, keepdims=True)
            acc_dtb = acc_dtb + jnp.sum(da, axis=0, keepdims=True)
        dsc_ref[0, 0] = jnp.where(lane1 == 0, acc_alog, 0.0) + jnp.where(lane1 == 1, acc_dtb, 0.0)

    big = pl.BlockSpec((1, rows, GDN_DIM), lambda h, i: (h, i, 0))
    big_shape = jax.ShapeDtypeStruct((GDN_HEADS, t, GDN_DIM), F32)
    return pl.pallas_call(
        body, name=name, grid=(GDN_HEADS, n // cps),
        in_specs=_gdn_local_specs(t, cps) + [big] * 4 + [pl.BlockSpec((1, rows, CHUNK), lambda h, i: (h, i, 0)),
                                                        pl.BlockSpec((1, cps, 1, GDN_DIM), lambda h, i: (h, i, 0, 0))],
        out_specs=[big] * 4 + [pl.BlockSpec((1, 1, 1, GDN_DIM), lambda h, i: (h, i, 0, 0))],
        out_shape=[big_shape] * 4 + [jax.ShapeDtypeStruct((GDN_HEADS, n // cps, 1, GDN_DIM), F32)],
        compiler_params=_cparams("parallel", "parallel"),
    )(qkv, qkv, qkv, ab, a_log, dt_bias, du, dw, dqd, dkd, daqk, dgl)


def _gdn_scan_fwd_staged(u, w, qd, kd, aqk, gl, *, name):
    h, t, _ = u.shape
    n = t // CHUNK
    hs = range(h)

    def body(u_ref, w_ref, qd_ref, kd_ref, aqk_ref, gl_ref, o_ref, s_ref, state):
        @pl.when(pl.program_id(0) == 0)
        def _():
            state[...] = jnp.zeros_like(state)

        ss = [state[hh] for hh in hs]
        for hh in hs:
            s_ref[hh, 0] = ss[hh]
        vn = [u_ref[hh] - x for hh, x in zip(hs, _dot_each([w_ref[hh] for hh in hs], ss))]
        from_state = _dot_each([qd_ref[hh] for hh in hs], ss)
        from_chunk = _dot_each([aqk_ref[hh] for hh in hs], vn)
        writes = _dot_each([kd_ref[hh] for hh in hs], vn, 0, 0)
        for hh in hs:
            o_ref[hh] = from_state[hh] + from_chunk[hh]
            state[hh] = ss[hh] * gl_ref[hh, 0] + writes[hh]

    big = pl.BlockSpec((h, CHUNK, GDN_DIM), lambda i: (0, i, 0))
    return pl.pallas_call(
        body, name=name, grid=(n,),
        in_specs=[big] * 4 + [pl.BlockSpec((h, CHUNK, CHUNK), lambda i: (0, i, 0)),
                              pl.BlockSpec((h, 1, 1, GDN_DIM), lambda i: (0, i, 0, 0))],
        out_specs=[big, pl.BlockSpec((h, 1, GDN_DIM, GDN_DIM), lambda i: (0, i, 0, 0))],
        out_shape=[jax.ShapeDtypeStruct((h, t, GDN_DIM), F32), jax.ShapeDtypeStruct((h, n, GDN_DIM, GDN_DIM), F32)],
        scratch_shapes=[pltpu.VMEM((h, GDN_DIM, GDN_DIM), F32)],
        compiler_params=_cparams("arbitrary"),
    )(u, w, qd, kd, aqk, gl)


def _gdn_scan_bwd_staged(u, w, qd, kd, aqk, gl, states, do, *, name):
    h, t, _ = u.shape
    n = t // CHUNK
    hs = range(h)

    def body(u_ref, w_ref, qd_ref, kd_ref, aqk_ref, gl_ref, s_ref, do_ref,
             du_ref, dw_ref, dqd_ref, dkd_ref, daqk_ref, dgl_ref, dstate):
        @pl.when(pl.program_id(0) == 0)
        def _():
            dstate[...] = jnp.zeros_like(dstate)

        ri = lax.broadcasted_iota(jnp.int32, (CHUNK, CHUNK), 0)
        ci = lax.broadcasted_iota(jnp.int32, (CHUNK, CHUNK), 1)
        ss = [s_ref[hh, 0] for hh in hs]
        dsn = [dstate[hh] for hh in hs]
        dov = [do_ref[hh] for hh in hs]
        wv = [w_ref[hh] for hh in hs]
        vn = [u_ref[hh] - x for hh, x in zip(hs, _dot_each(wv, ss))]
        dvn = [x + y for x, y in zip(_dot_each([aqk_ref[hh] for hh in hs], dov, 0, 0),
                                     _dot_each([kd_ref[hh] for hh in hs], dsn))]
        dws = _dot_each(dvn, ss, 1, 1)
        dqds = _dot_each(dov, ss, 1, 1)
        dkds = _dot_each(vn, dsn, 1, 1)
        daqks = _dot_each(dov, vn, 1, 1)
        reads = _dot_each([qd_ref[hh] for hh in hs], dov, 0, 0)
        solves = _dot_each(wv, dvn, 0, 0)
        for hh in hs:
            du_ref[hh] = dvn[hh]
            dw_ref[hh] = -dws[hh]
            dqd_ref[hh] = dqds[hh]
            dkd_ref[hh] = dkds[hh]
            daqk_ref[hh] = jnp.where(ri >= ci, daqks[hh], 0.0)
            dgl_ref[hh, 0] = jnp.sum(dsn[hh] * ss[hh], axis=0, keepdims=True)
            dstate[hh] = reads[hh] + dsn[hh] * gl_ref[hh, 0] - solves[hh]

    big = pl.BlockSpec((h, CHUNK, GDN_DIM), lambda i: (0, n - 1 - i, 0))
    sq = pl.BlockSpec((h, CHUNK, CHUNK), lambda i: (0, n - 1 - i, 0))
    glb = pl.BlockSpec((h, 1, 1, GDN_DIM), lambda i: (0, n - 1 - i, 0, 0))
    big_shape = jax.ShapeDtypeStruct((h, t, GDN_DIM), F32)
    return pl.pallas_call(
        body, name=name, grid=(n,),
        in_specs=[big] * 4 + [sq, glb, pl.BlockSpec((h, 1, GDN_DIM, GDN_DIM), lambda i: (0, n - 1 - i, 0, 0)), big],
        out_specs=[big] * 4 + [sq, glb],
        out_shape=[big_shape] * 4 + [jax.ShapeDtypeStruct((h, t, CHUNK), F32),
                                     jax.ShapeDtypeStruct((h, n, 1, GDN_DIM), F32)],
        scratch_shapes=[pltpu.VMEM((h, GDN_DIM, GDN_DIM), F32)],
        compiler_params=_cparams("arbitrary"),
    )(u, w, qd, kd, aqk, gl, states, do)


SCAN_CHUNKS = 4


def _scan_chunks(n):
    return SCAN_CHUNKS if n % SCAN_CHUNKS == 0 else 1


def _gdn_scan_fwd_multi(u, w, qd, kd, aqk, gl, *, name):
    h, t, _ = u.shape
    n = t // CHUNK
    cps = _scan_chunks(n)
    rows = cps * CHUNK
    hs = range(h)

    def body(u_ref, w_ref, qd_ref, kd_ref, aqk_ref, gl_ref, o_ref, s_ref, state):
        @pl.when(pl.program_id(0) == 0)
        def _():
            state[...] = jnp.zeros_like(state)

        ss = [state[hh] for hh in hs]
        for j in range(cps):
            sl = slice(j * CHUNK, (j + 1) * CHUNK)
            for hh in hs:
                s_ref[hh, j] = ss[hh]
            vn = [u_ref[hh, sl, :] - x for hh, x in zip(hs, _dot_each([w_ref[hh, sl, :] for hh in hs], ss))]
            from_state = _dot_each([qd_ref[hh, sl, :] for hh in hs], ss)
            from_chunk = _dot_each([aqk_ref[hh, sl, :] for hh in hs], vn)
            writes = _dot_each([kd_ref[hh, sl, :] for hh in hs], vn, 0, 0)
            for hh in hs:
                o_ref[hh, sl, :] = from_state[hh] + from_chunk[hh]
            ss = [ss[hh] * gl_ref[hh, j] + writes[hh] for hh in hs]
        for hh in hs:
            state[hh] = ss[hh]

    big = pl.BlockSpec((h, rows, GDN_DIM), lambda i: (0, i, 0))
    return pl.pallas_call(
        body, name=name, grid=(n // cps,),
        in_specs=[big] * 4 + [pl.BlockSpec((h, rows, CHUNK), lambda i: (0, i, 0)),
                              pl.BlockSpec((h, cps, 1, GDN_DIM), lambda i: (0, i, 0, 0))],
        out_specs=[big, pl.BlockSpec((h, cps, GDN_DIM, GDN_DIM), lambda i: (0, i, 0, 0))],
        out_shape=[jax.ShapeDtypeStruct((h, t, GDN_DIM), F32), jax.ShapeDtypeStruct((h, n, GDN_DIM, GDN_DIM), F32)],
        scratch_shapes=[pltpu.VMEM((h, GDN_DIM, GDN_DIM), F32)],
        compiler_params=_cparams("arbitrary"),
    )(u, w, qd, kd, aqk, gl)


def _gdn_scan_bwd_multi(u, w, qd, kd, aqk, gl, states, do, *, name):
    h, t, _ = u.shape
    n = t // CHUNK
    cps = _scan_chunks(n)
    rows = cps * CHUNK
    steps = n // cps
    hs = range(h)

    def body(u_ref, w_ref, qd_ref, kd_ref, aqk_ref, gl_ref, s_ref, do_ref,
             du_ref, dw_ref, dqd_ref, dkd_ref, daqk_ref, dgl_ref, dstate):
        @pl.when(pl.program_id(0) == 0)
        def _():
            dstate[...] = jnp.zeros_like(dstate)

        ri = lax.broadcasted_iota(jnp.int32, (CHUNK, CHUNK), 0)
        ci = lax.broadcasted_iota(jnp.int32, (CHUNK, CHUNK), 1)
        dsn = [dstate[hh] for hh in hs]
        for j in reversed(range(cps)):
            sl = slice(j * CHUNK, (j + 1) * CHUNK)
            ss = [s_ref[hh, j] for hh in hs]
            dov = [do_ref[hh, sl, :] for hh in hs]
            wv = [w_ref[hh, sl, :] for hh in hs]
            vn = [u_ref[hh, sl, :] - x for hh, x in zip(hs, _dot_each(wv, ss))]
            dvn = [x + y for x, y in zip(_dot_each([aqk_ref[hh, sl, :] for hh in hs], dov, 0, 0),
                                         _dot_each([kd_ref[hh, sl, :] for hh in hs], dsn))]
            dws = _dot_each(dvn, ss, 1, 1)
            dqds = _dot_each(dov, ss, 1, 1)
            dkds = _dot_each(vn, dsn, 1, 1)
            daqks = _dot_each(dov, vn, 1, 1)
            reads = _dot_each([qd_ref[hh, sl, :] for hh in hs], dov, 0, 0)
            solves = _dot_each(wv, dvn, 0, 0)
            for hh in hs:
                du_ref[hh, sl, :] = dvn[hh]
                dw_ref[hh, sl, :] = -dws[hh]
                dqd_ref[hh, sl, :] = dqds[hh]
                dkd_ref[hh, sl, :] = dkds[hh]
                daqk_ref[hh, sl, :] = jnp.where(ri >= ci, daqks[hh], 0.0)
                dgl_ref[hh, j] = jnp.sum(dsn[hh] * ss[hh], axis=0, keepdims=True)
            dsn = [reads[hh] + dsn[hh] * gl_ref[hh, j] - solves[hh] for hh in hs]
        for hh in hs:
            dstate[hh] = dsn[hh]

    big = pl.BlockSpec((h, rows, GDN_DIM), lambda i: (0, steps - 1 - i, 0))
    sq = pl.BlockSpec((h, rows, CHUNK), lambda i: (0, steps - 1 - i, 0))
    glb = pl.BlockSpec((h, cps, 1, GDN_DIM), lambda i: (0, steps - 1 - i, 0, 0))
    big_shape = jax.ShapeDtypeStruct((h, t, GDN_DIM), F32)
    return pl.pallas_call(
        body, name=name, grid=(steps,),
        in_specs=[big] * 4 + [sq, glb, pl.BlockSpec((h, cps, GDN_DIM, GDN_DIM), lambda i: (0, steps - 1 - i, 0, 0)),
                              big],
        out_specs=[big] * 4 + [sq, glb],
        out_shape=[big_shape] * 4 + [jax.ShapeDtypeStruct((h, t, CHUNK), F32),
                                     jax.ShapeDtypeStruct((h, n, 1, GDN_DIM), F32)],
        scratch_shapes=[pltpu.VMEM((h, GDN_DIM, GDN_DIM), F32)],
        compiler_params=_cparams("arbitrary"),
    )(u, w, qd, kd, aqk, gl, states, do)


def _gather_phases(x_refs, out_refs, send_sems, recv_sems, local_sems):
    n = len(x_refs)
    mx, my, mc = lax.axis_index("x"), lax.axis_index("y"), lax.axis_index("c")
    me, sibling = (mx, my, mc), (mx, my, 1 - mc)
    chips = [(1 - mx, my), (mx, 1 - my), (1 - mx, 1 - my)]

    def slot(a, px, py, pc):
        return out_refs[a].at[4 * px + 2 * py + pc]

    def copy(a, k, block, to, src=None):
        return pltpu.make_async_remote_copy(
            src_ref=slot(a, *block) if src is None else src, dst_ref=slot(a, *block),
            send_sem=send_sems.at[a, k], recv_sem=recv_sems.at[a, k], device_id=to, device_id_type=MESH_ID)

    def mine():
        return [pltpu.make_async_copy(x_refs[a], slot(a, *me), local_sems.at[a]) for a in range(n)]

    def first():
        out = [copy(a, 1 + j, me, (*chip, mc), src=x_refs[a]) for j, chip in enumerate(chips) for a in range(n)]
        return out + [copy(a, 0, me, sibling, src=x_refs[a]) for a in range(n)]

    def passed():
        return [copy(a, 4 + j, (*chip, mc), sibling) for j, chip in enumerate(chips) for a in range(n)]

    def start():
        for cp in mine() + first():
            cp.start()

    def pass_on():
        for j, chip in enumerate(chips):
            for a in range(n):
                copy(a, 1 + j, (*chip, mc), me).wait_recv()
                copy(a, 4 + j, (*chip, mc), sibling).start()

    def finish():
        for a in range(n):
            copy(a, 0, sibling, me).wait_recv()
        for j, chip in enumerate(chips):
            for a in range(n):
                copy(a, 4 + j, (*chip, 1 - mc), me).wait_recv()
        for cp in first() + passed():
            cp.wait_send()
        for cp in mine():
            cp.wait()

    return start, pass_on, finish


def _gather_extras(shards):
    n = len(shards)
    anyspace = pl.BlockSpec(memory_space=pl.ANY)
    return ([anyspace] * n, [anyspace] * n, [jax.ShapeDtypeStruct((N_DEV,) + x.shape, x.dtype) for x in shards],
            [pltpu.SemaphoreType.DMA((n, 7)), pltpu.SemaphoreType.DMA((n, 7)), pltpu.SemaphoreType.DMA((n,))])


def _scatter_phases(g_refs, out_refs, send_sems, recv_sems, local_sems):
    n = len(g_refs)
    mx, my, mc = lax.axis_index("x"), lax.axis_index("y"), lax.axis_index("c")
    me_id = 4 * mx + 2 * my + mc

    def peer(r):
        return (1 - mx if r & 4 else mx, 1 - my if r & 2 else my, 1 - mc if r & 1 else mc)

    def peer_id(r):
        px, py, pc = peer(r)
        return 4 * px + 2 * py + pc

    def copies():
        return [pltpu.make_async_remote_copy(
            src_ref=g_refs[a].at[peer_id(r)], dst_ref=out_refs[a].at[me_id], send_sem=send_sems.at[a, r - 1],
            recv_sem=recv_sems.at[a, r - 1], device_id=peer(r), device_id_type=MESH_ID)
            for r in range(1, N_DEV) for a in range(n)]

    def arrivals():
        return [pltpu.make_async_remote_copy(
            src_ref=g_refs[a].at[peer_id(r)], dst_ref=out_refs[a].at[peer_id(r)], send_sem=send_sems.at[a, r - 1],
            recv_sem=recv_sems.at[a, r - 1], device_id=peer(r), device_id_type=MESH_ID)
            for r in range(1, N_DEV) for a in range(n)]

    def mine():
        return [pltpu.make_async_copy(g_refs[a].at[me_id], out_refs[a].at[me_id], local_sems.at[a]) for a in range(n)]

    def start():
        for cp in mine() + copies():
            cp.start()

    def finish():
        for cp in arrivals():
            cp.wait_recv()
        for cp in copies():
            cp.wait_send()
        for cp in mine():
            cp.wait()

    return start, finish


def _scatter_extras(blocks):
    n = len(blocks)
    anyspace = pl.BlockSpec(memory_space=pl.ANY)
    return ([anyspace] * n, [anyspace] * n, [jax.ShapeDtypeStruct(b.shape, b.dtype) for b in blocks],
            [pltpu.SemaphoreType.DMA((n, 7)), pltpu.SemaphoreType.DMA((n, 7)), pltpu.SemaphoreType.DMA((n,))])


def _sum_slots(x, *, name):
    _, r, c = x.shape
    tr = _tile(r, ROW_TILE)

    def body(x_ref, o_ref):
        acc = x_ref[0].astype(F32)
        for s in range(1, N_DEV):
            acc = acc + x_ref[s].astype(F32)
        o_ref[...] = acc

    return pl.pallas_call(
        body, name=name, grid=(r // tr,), in_specs=[pl.BlockSpec((N_DEV, tr, c), lambda i: (0, i, 0))],
        out_specs=pl.BlockSpec((tr, c), lambda i: (i, 0)), out_shape=jax.ShapeDtypeStruct((r, c), F32),
        compiler_params=_cparams("parallel"),
    )(x)


SB_PAIRS = SB_HEADS // 2
SB_PAIR_QBLOCK = 512
SB_PAIR_QBLOCK_FWD = 512


def _sb_pair_blocks(t, pref=SB_PAIR_QBLOCK):
    bq = _tile(t, pref)
    return bq, _tile(bq, SB_KBLOCK)


def _sb_pair_specs(t, bq):
    base = C_SBQKV // 128
    return [pl.BlockSpec((bq, 128), lambda p, i: (i, base + p)),
            pl.BlockSpec((t, 128), lambda p, i: (0, base + SB_PAIRS + p)),
            pl.BlockSpec((t, 128), lambda p, i: (0, base + 2 * SB_PAIRS + p))]


def _halves(x, first):
    zero = jnp.zeros_like(x)
    return [jnp.where(first, x, zero), jnp.where(first, zero, x)]


def _sb_mask(qi, kb, bq, bk):
    t_idx = qi * bq + lax.broadcasted_iota(jnp.int32, (bq, bk), 0)
    s_idx = kb * bk + lax.broadcasted_iota(jnp.int32, (bq, bk), 1)
    return s_idx < t_idx


SB_SCALE = SB_DIM ** -0.5


def _sb_pair_scores(qh, kblk, mask):
    zs = _dot_each(qh, [kblk, kblk], 1, 1)
    es = [jnp.exp(-jnp.abs(z)) for z in zs]
    sps = [jnp.maximum(z, 0.0) + jnp.log(1.0 + e) for z, e in zip(zs, es)]
    if mask is not None:
        sps = [jnp.where(mask, sp, 0.0) for sp in sps]
    return zs, es, sps


def _sb_atts(zs, csums, laters, mask):
    atts = [jnp.exp(z - c - l) for z, c, l in zip(zs, csums, laters)]
    return atts if mask is None else [jnp.where(mask, a, 0.0) for a in atts]


def _scaled_queries(q_ref, first):
    return _halves(q_ref[...] * jnp.asarray(SB_SCALE, q_ref.dtype), first)


def _running_sums(x_list, m):
    return [_dot(x, m) for x in x_list]


def _sb_pair_fwd(proj, shards=(), *, name):
    t = proj.shape[0]
    bq, bk = _sb_pair_blocks(t, SB_PAIR_QBLOCK_FWD)
    n = len(shards)
    nq = t // bq
    nsteps = SB_PAIRS * nq

    def body(q_ref, k_ref, v_ref, *rest):
        o_ref = rest[n]
        qi = pl.program_id(1)
        if n:
            step_no = pl.program_id(0) * nq + qi
            start, pass_on, finish = _gather_phases(rest[:n], rest[n + 1:2 * n + 1], *rest[2 * n + 1:])
            pl.when(step_no == 0)(start)
            pl.when(step_no == (2 * nsteps) // 3)(pass_on)
        first = lax.broadcasted_iota(jnp.int32, (1, 128), 1) < SB_DIM
        qh = _scaled_queries(q_ref, first)
        suffix = _suffix_ones(bk)
        band = bq // bk
        nkb = (qi + 1) * band

        def make_step(masked):
            def step(it, carry):
                later0, later1, acc = carry
                kb = nkb - 1 - it
                rows = pl.ds(pl.multiple_of(kb * bk, bk), bk)
                mask = _sb_mask(qi, kb, bq, bk) if masked else None
                zs, _, sps = _sb_pair_scores(qh, k_ref[rows, :], mask)
                atts = _sb_atts(zs, _running_sums(sps, suffix), (later0, later1), mask)
                outs = _dot_each(atts, _halves(v_ref[rows, :], first))
                return (later0 + jnp.sum(sps[0], axis=1, keepdims=True),
                        later1 + jnp.sum(sps[1], axis=1, keepdims=True), acc + (outs[0] + outs[1]))
            return step

        zero = jnp.zeros((bq, 1), F32)
        carry = lax.fori_loop(0, band, make_step(True), (zero, zero, jnp.zeros((bq, 128), F32)))
        _, _, acc = lax.fori_loop(band, nkb, make_step(False), carry)
        o_ref[...] = acc.astype(o_ref.dtype)
        if n:
            pl.when(step_no == nsteps - 1)(finish)

    more_in, more_out, more_shapes, sems = _gather_extras(shards) if n else ([], [], [], [])
    res = pl.pallas_call(
        body, name=name, grid=(SB_PAIRS, nq), in_specs=_sb_pair_specs(t, bq) + more_in,
        out_specs=[pl.BlockSpec((bq, 128), lambda p, i: (i, p))] + more_out,
        out_shape=[jax.ShapeDtypeStruct((t, SB_W), BF16)] + more_shapes, scratch_shapes=sems,
        compiler_params=_cparams("arbitrary", "arbitrary"),
    )(proj, proj, proj, *shards)
    return res[0], list(res[1:])


def _sb_pair_bwd(proj, dy, blocks=(), *, name):
    t = proj.shape[0]
    bq, bk = _sb_pair_blocks(t)
    nq = t // bq
    n = len(blocks)

    def body(q_ref, k_ref, v_ref, do_ref, *rest):
        dq_ref, dk_ref, dv_ref = rest[n:n + 3]
        dl_keep, sig_keep, dk_acc, dv_acc = rest[2 * n + 3:2 * n + 7]
        qi = pl.program_id(1)
        if n:
            step_no = pl.program_id(0) * nq + qi
            start, finish = _scatter_phases(rest[:n], rest[n + 3:2 * n + 3], *rest[2 * n + 7:])
            pl.when(step_no == 0)(start)

        @pl.when(qi == 0)
        def _():
            dk_acc[...] = jnp.zeros_like(dk_acc)
            dv_acc[...] = jnp.zeros_like(dv_acc)

        first = lax.broadcasted_iota(jnp.int32, (1, 128), 1) < SB_DIM
        qh = _scaled_queries(q_ref, first)
        doh = _halves(do_ref[...], first)
        suffix = _suffix_ones(bk)
        prefix = _prefix_ones(bk)
        band = bq // bk
        nkb = (qi + 1) * band

        def make_back(masked):
            def back(it, carry):
                kb = nkb - 1 - it
                rows = pl.ds(pl.multiple_of(kb * bk, bk), bk)
                vblk = v_ref[rows, :]
                mask = _sb_mask(qi, kb, bq, bk) if masked else None
                zs, es, sps = _sb_pair_scores(qh, k_ref[rows, :], mask)
                atts = _sb_atts(zs, _running_sums(sps, suffix), carry, mask)
                dvs = _dot_each(atts, doh, 0, 0)
                datts = _dot_each(doh, [vblk, vblk], 1, 1)
                dv_acc[rows, :] += dvs[0] + dvs[1]
                for hh in range(2):
                    sig = jnp.where(zs[hh] >= 0, 1.0, es[hh]) * pl.reciprocal(1.0 + es[hh], approx=True)
                    dl_keep[hh, kb] = (atts[hh] * datts[hh]).astype(dl_keep.dtype)
                    sig_keep[hh, kb] = (sig if mask is None else jnp.where(mask, sig, 0.0)).astype(sig_keep.dtype)
                return tuple(l + jnp.sum(sp, axis=1, keepdims=True) for l, sp in zip(carry, sps))
            return back

        zero = jnp.zeros((bq, 1), F32)
        lax.fori_loop(band, nkb, make_back(False), lax.fori_loop(0, band, make_back(True), (zero, zero)))

        def forth(kb, carry):
            before0, before1, dq = carry
            rows = pl.ds(pl.multiple_of(kb * bk, bk), bk)
            kept = [dl_keep[hh, kb] for hh in range(2)]
            sums = _running_sums(kept, prefix)
            dls = [x.astype(F32) for x in kept]
            dzs = [dl - sig_keep[hh, kb].astype(F32) * (b + s)
                   for hh, (dl, b, s) in enumerate(zip(dls, (before0, before1), sums))]
            dks = _dot_each(dzs, qh, 0, 0)
            dqs = _dot_each(dzs, _halves(k_ref[rows, :], first))
            dk_acc[rows, :] += dks[0] + dks[1]
            return (before0 + jnp.sum(dls[0], axis=1, keepdims=True), before1 + jnp.sum(dls[1], axis=1, keepdims=True),
                    dq + (dqs[0] + dqs[1]))

        _, _, dq = lax.fori_loop(0, nkb, forth, (zero, zero, jnp.zeros((bq, 128), F32)))
        dq_ref[...] = (dq * SB_SCALE).astype(dq_ref.dtype)

        @pl.when(qi == nq - 1)
        def _():
            dk_ref[...] = dk_acc[...].astype(dk_ref.dtype)
            dv_ref[...] = dv_acc[...].astype(dv_ref.dtype)

        if n:
            pl.when(step_no == SB_PAIRS * nq - 1)(finish)

    qspec = pl.BlockSpec((bq, 128), lambda p, i: (i, p))
    kvspec = pl.BlockSpec((t, 128), lambda p, i: (0, p))
    shape = jax.ShapeDtypeStruct((t, SB_W), BF16)
    more_in, more_out, more_shapes, sems = _scatter_extras(blocks) if n else ([], [], [], [])
    res = pl.pallas_call(
        body, name=name, grid=(SB_PAIRS, nq), in_specs=_sb_pair_specs(t, bq) + [qspec] + more_in,
        out_specs=[qspec, kvspec, kvspec] + more_out, out_shape=[shape] * 3 + more_shapes,
        scratch_shapes=[pltpu.VMEM((2, t // bk, bq, bk), BF16), pltpu.VMEM((2, t // bk, bq, bk), BF16),
                        pltpu.VMEM((t, 128), F32), pltpu.VMEM((t, 128), F32)] + sems,
        compiler_params=_cparams("arbitrary", "arbitrary"),
    )(proj, proj, proj, dy, *blocks)
    return res[0], res[1], res[2], list(res[3:])


def _heads_major(cols, heads, dim):
    t = cols.shape[0]
    return cols.reshape(t, heads, dim).transpose(1, 0, 2)


def _heads_minor(x):
    h, t, dim = x.shape
    return x.transpose(1, 0, 2).reshape(t, h * dim)


def _sb_qkv(proj):
    return [_heads_major(proj[:, C_SBQKV + i * SB_W:C_SBQKV + (i + 1) * SB_W], SB_HEADS, SB_DIM) for i in range(3)]


def _relu2_epilogue(r):
    a = jnp.maximum(r, 0.0)
    return r, a * a


def _add_epilogue(r, other):
    return (r + other,)


def _relu2_bwd_epilogue(r, a):
    return (r * 2.0 * jnp.maximum(a.astype(F32), 0.0),)


def _layer_fwd(x, p, next_shards=()):
    h = _norm_fwd(x, p["norm_mix_pre"], out_dtype=BF16, name="norm_pre_fwd")
    proj = _mm(h, p["w_main"], name="mm_in")
    ab = _mm(h, p["w_ab"], out_dtypes=(F32,), name="mm_ab")
    qkv = _gdn_pre_fwd(proj, p["conv_qkv_w"], name="gdn_pre_fwd")
    a_log, dt_bias = p["gdn_a_log"].reshape(1, GDN_HEADS), p["gdn_dt_bias"].reshape(1, GDN_HEADS)
    u, w, qd, kd, aqk, gl = _gdn_local_fwd_staged(qkv, ab, a_log, dt_bias, name="gdn_local_fwd")
    o_gdn, states = _gdn_scan_fwd_multi(u, w, qd, kd, aqk, gl, name="gdn_scan_fwd")
    y_a = _gdn_post_fwd(o_gdn, proj, p["gdn_norm_w"], name="gdn_post_fwd")
    y_b, gathered = _sb_pair_fwd(proj, next_shards, name="sb_fwd")
    y_c = _sc_fwd(proj, p["conv_sc_w"], name="sc_fwd")
    ys = (y_a, y_b, y_c)
    ps = tuple(_mm(ys[b], p["w_branch"][b], name="mm_branch") for b in range(3))
    merged = _merge_fwd(ps, proj, name="merge_fwd")
    mo = _mm(merged, p["w_out"], out_dtypes=(F32,), name="mm_out")
    x1 = _norm_fwd(mo, p["norm_mix_post"], x, out_dtype=F32, name="norm_post_fwd")
    h2 = _norm_fwd(x1, p["norm_ffn_pre"], out_dtype=BF16, name="norm_pre_fwd")
    a1, r1 = _mm(h2, p["w_ff1"], out_dtypes=(BF16, BF16), epi=_relu2_epilogue, name="mm_ff1")
    f = _mm(r1, p["w_ff2"], out_dtypes=(F32,), name="mm_ff2")
    x2 = _norm_fwd(f, p["norm_ffn_post"], x1, out_dtype=F32, name="norm_post_fwd")
    saved = dict(x=x, h=h, proj=proj, ab=ab, qkv=qkv, u=u, w=w, qd=qd, kd=kd, aqk=aqk, gl=gl, o_gdn=o_gdn,
                 states=states, ys=ys, ps=ps, merged=merged, mo=mo, x1=x1, h2=h2,
                 a1=a1, r1=r1, f=f)
    return x2, saved, gathered


def _layer_bwd(dx2, p, s, blocks=()):
    g = {}
    df, g["norm_ffn_post"] = _norm_bwd(s["f"], p["norm_ffn_post"], dx2, out_dtype=BF16, name="norm_bwd_b")
    da1 = _mm(df, p["w_ff2"], tb=True, epi=_relu2_bwd_epilogue, extras=(s["a1"],), name="mm_ff2_dx")
    g["w_ff2"] = _mm(s["r1"], df, ta=True, name="mm_ff2_dw")
    g["w_ff1"] = _mm(s["h2"], da1, ta=True, name="mm_ff1_dw")
    dh2 = _mm(da1, p["w_ff1"], tb=True, out_dtypes=(F32,), name="mm_ff1_dx")
    dx1, g["norm_ffn_pre"] = _norm_bwd(s["x1"], p["norm_ffn_pre"], dh2, dx2, out_dtype=F32, name="norm_bwd_f")
    dmo, g["norm_mix_post"] = _norm_bwd(s["mo"], p["norm_mix_post"], dx1, out_dtype=BF16, name="norm_bwd_b")
    dmerged = _mm(dmo, p["w_out"], tb=True, name="mm_out_dx")
    g["w_out"] = _mm(s["merged"], dmo, ta=True, name="mm_out_dw")
    dps, dgates = _merge_bwd(s["ps"], s["proj"], dmerged, name="merge_bwd")
    dys = [_mm(dps[b], p["w_branch"][b], tb=True, name="mm_branch_dx") for b in range(3)]
    g["w_branch"] = jnp.stack([_mm(s["ys"][b], dps[b], ta=True, name="mm_branch_dw") for b in range(3)])
    dscx, dscb, dscc, g["conv_sc_w"] = _sc_bwd(s["proj"], p["conv_sc_w"], dys[2], name="sc_bwd")
    dsq, dsk, dsv, received = _sb_pair_bwd(s["proj"], dys[1], blocks, name="sb_bwd")
    a_log, dt_bias = p["gdn_a_log"].reshape(1, GDN_HEADS), p["gdn_dt_bias"].reshape(1, GDN_HEADS)
    do_gdn, dggate, g["gdn_norm_w"] = _gdn_post_bwd(s["o_gdn"], s["proj"], p["gdn_norm_w"], dys[0], name="gdn_post_bwd")
    du, dw, dqd, dkd, daqk, dgl = _gdn_scan_bwd_multi(s["u"], s["w"], s["qd"], s["kd"], s["aqk"], s["gl"],
                                                       s["states"], do_gdn, name="gdn_scan_bwd")
    dq, dk, dv, dab_h, dsc = _gdn_local_bwd_staged(s["qkv"], s["ab"], a_log, dt_bias, du, dw, dqd, dkd, daqk, dgl,
                                                   name="gdn_local_bwd")
    dsc = jnp.sum(dsc, axis=(1, 2))
    g["gdn_a_log"], g["gdn_dt_bias"] = dsc[:, 0], dsc[:, 1]
    dqkv = jnp.concatenate([dq, dk, dv], axis=0)
    dgqkv, g["conv_qkv_w"] = _gdn_pre_bwd(s["proj"], p["conv_qkv_w"], dqkv, name="gdn_pre_bwd")
    dab = jnp.sum(dab_h, axis=0).astype(BF16)
    dproj = jnp.concatenate([dgqkv, dggate, dsq, dsk, dsv, dscx, dscb, dscc, dgates], axis=1)
    g["w_main"] = _mm(s["h"], dproj, ta=True, name="mm_in_dw")
    g["w_ab"] = _mm(s["h"], dab, ta=True, out_dtypes=(F32,), name="mm_ab_dw")
    dh_ab = _mm(dab, p["w_ab"], tb=True, out_dtypes=(F32,), name="mm_ab_dx")
    dh = _mm(dproj, p["w_main"], tb=True, out_dtypes=(F32,), epi=_add_epilogue, extras=(dh_ab,), name="mm_in_dx")
    dx, g["norm_mix_pre"] = _norm_bwd(s["x"], p["norm_mix_pre"], dh, dx1, out_dtype=F32, name="norm_bwd_f")
    return dx, g, received


NORMS = ("norm_mix_pre", "norm_mix_post", "norm_ffn_pre", "norm_ffn_post")
SMALL = NORMS + ("gdn_a_log", "gdn_dt_bias", "gdn_norm_w")
CONVS = ("conv_qkv_w", "conv_sc_w")
AB_LO = 2048


def _split_w_in(w_in):
    main = jnp.concatenate([w_in[..., :AB_LO], w_in[..., AB_LO + 2 * GDN_HEADS:]], axis=-1)
    ab = w_in[..., AB_LO:AB_LO + 2 * GDN_HEADS]
    pad = [(0, 0)] * (ab.ndim - 1) + [(0, AB_W - 2 * GDN_HEADS)]
    return main, jnp.pad(ab, pad)


def _join_w_in(main, ab):
    return jnp.concatenate([main[..., :AB_LO], ab[..., :2 * GDN_HEADS].astype(main.dtype), main[..., AB_LO:]], axis=-1)


def kernel(x, norm_mix_pre, w_in, conv_qkv_w, gdn_a_log, gdn_dt_bias, gdn_norm_w, conv_sc_w, w_branch, w_out, norm_mix_post, norm_ffn_pre, w_ff1, w_ff2, norm_ffn_post, loss_target, m_norm_mix_pre, m_w_in, m_conv_qkv_w, m_gdn_a_log, m_gdn_dt_bias, m_gdn_norm_w, m_conv_sc_w, m_w_branch, m_w_out, m_norm_mix_post, m_norm_ffn_pre, m_w_ff1, m_w_ff2, m_norm_ffn_post, v_norm_mix_pre, v_w_in, v_conv_qkv_w, v_gdn_a_log, v_gdn_dt_bias, v_gdn_norm_w, v_conv_sc_w, v_w_branch, v_w_out, v_norm_mix_post, v_norm_ffn_pre, v_w_ff1, v_w_ff2, v_norm_ffn_post):
    names = ("norm_mix_pre", "w_in", "conv_qkv_w", "gdn_a_log", "gdn_dt_bias", "gdn_norm_w", "conv_sc_w", "w_branch",
             "w_out", "norm_mix_post", "norm_ffn_pre", "w_ff1", "w_ff2", "norm_ffn_post")
    w = dict(zip(names, (norm_mix_pre, w_in, conv_qkv_w, gdn_a_log, gdn_dt_bias, gdn_norm_w, conv_sc_w, w_branch,
                         w_out, norm_mix_post, norm_ffn_pre, w_ff1, w_ff2, norm_ffn_post)))
    m = dict(zip(names, (m_norm_mix_pre, m_w_in, m_conv_qkv_w, m_gdn_a_log, m_gdn_dt_bias, m_gdn_norm_w, m_conv_sc_w,
                         m_w_branch, m_w_out, m_norm_mix_post, m_norm_ffn_pre, m_w_ff1, m_w_ff2, m_norm_ffn_post)))
    v = dict(zip(names, (v_norm_mix_pre, v_w_in, v_conv_qkv_w, v_gdn_a_log, v_gdn_dt_bias, v_gdn_norm_w, v_conv_sc_w,
                         v_w_branch, v_w_out, v_norm_mix_post, v_norm_ffn_pre, v_w_ff1, v_w_ff2, v_norm_ffn_post)))
    me = 4 * lax.axis_index("x") + 2 * lax.axis_index("y") + lax.axis_index("c")

    conv_shapes = [w[k].shape for k in CONVS]
    conv_all, = _all_gather([_pack_vec([w[k] for k in CONVS])], name="gather_small")
    convs = {k: _to_global(blk, 2) for k, blk in zip(CONVS, _unpack_vec(conv_all, conv_shapes))}
    shards = [[w[k][l].astype(BF16) for k in BIG] for l in range(DEPTH)]

    def layer_params(l, gathered):
        p = {k: _to_global(blk, BIG_AXIS[k] - 1) for k, blk in zip(BIG, gathered)}
        p["w_main"], p["w_ab"] = _split_w_in(p.pop("w_in"))
        p.update({k: convs[k][l] for k in CONVS})
        p.update({k: w[k][l] for k in SMALL})
        return p

    xs = x[0]
    gathered = _all_gather(shards[0], name="gather_weights")
    layers, saved = [], []
    for l in range(DEPTH):
        layers.append(layer_params(l, gathered))
        xs, s, gathered = _layer_fwd(xs, layers[l], shards[l + 1] if l + 1 < DEPTH else ())
        saved.append(s)
    dy, loss_lanes = _loss_head(xs, loss_target[0], name="loss_head")

    grads, big_sums, blocks = [None] * DEPTH, [None] * DEPTH, ()
    for l in reversed(range(DEPTH)):
        dy, g, received = _layer_bwd(dy, layers[l], saved[l], blocks)
        if received:
            big_sums[l + 1] = [_sum_slots(r, name="rs_sum_slots") for r in received]
        g["w_in"] = _join_w_in(g.pop("w_main"), g.pop("w_ab"))
        blocks = [_rows(_to_blocks(g[k], BIG_AXIS[k] - 1)) for k in BIG]
        grads[l] = g
    got = _exchange_sibling(blocks, name="rs_sibling")
    parts = [_pair_sum(b, r, name="rs_pair_sum") for b, r in zip(blocks, got)]
    got2 = _exchange_chips(parts, name="rs_chips")
    big_sums[0] = [_final_sum(p, r, name="rs_final_sum") for p, r in zip(parts, got2)]
    gsum = {k: jnp.stack([big_sums[l][i] for l in range(DEPTH)]).reshape(w[k].shape) for i, k in enumerate(BIG)}
    stack = {k: jnp.stack([g[k] for g in grads]) for k in SMALL + CONVS}

    small_parts = [stack[k] for k in SMALL + CONVS] + [jnp.sum(loss_lanes).reshape(1)]
    small_shapes = [stack[k].shape for k in SMALL + CONVS] + [(1,)]
    summed = _sum_devices(_all_gather([_pack_vec(small_parts)], name="gather_small_grads")[0], name="sum_small")
    small = _unpack_vec(summed, small_shapes)
    loss = small[-1][0]
    for k, val in zip(SMALL + CONVS, small[:-1]):
        gsum[k] = val
    for k in CONVS:
        per = gsum[k].shape[2] // N_DEV
        gsum[k] = lax.dynamic_slice_in_dim(gsum[k], me * per, per, axis=2)

    delta, new_m, new_v = {}, {}, {}
    for k in names:
        shp = w[k].shape
        two_d = (-1, shp[-1]) if len(shp) > 1 else (1, -1)
        d_, m_, v_ = _adamw(w[k].reshape(two_d), gsum[k].reshape(two_d), m[k].reshape(two_d), v[k].reshape(two_d),
                            name="adamw")
        delta[k], new_m[k], new_v[k] = d_.reshape(shp), m_.reshape(shp), v_.reshape(shp)

    return (loss, dy[None], *[gsum[k].reshape(w[k].shape) for k in names], *[delta[k] for k in names], *[new_m[k] for k in names],
            *[new_v[k] for k in names])
```

```python
import functools

import jax
import jax.numpy as jnp
from jax import lax
from jax.experimental import pallas as pl
from jax.experimental.pallas import tpu as pltpu

F32, BF16 = jnp.float32, jnp.bfloat16
MESH_ID = pl.DeviceIdType.MESH

N_DEV = 8
DEPTH = 4
D_MODEL = 1024
D_FF = 4096
EPS = 1e-6
GDN_HEADS, GDN_DIM, GDN_CONV = 4, 128, 4
GDN_W = GDN_HEADS * GDN_DIM
CHUNK = 64
SB_HEADS, SB_DIM = 8, 64
SB_W = SB_HEADS * SB_DIM
SB_QBLOCK, SB_KBLOCK = 512, 256
SC_W, SC_CONV = 512, 3
IN_W = 8200
C_GQKV, C_GGATE, C_SBQKV, C_SCX, C_SCB, C_SCC, C_GATES, MAIN_W = 0, 1536, 2048, 3584, 4096, 4608, 5120, 8192
AB_W = 128

ADAM_LR, ADAM_B1, ADAM_B2, ADAM_EPS, ADAM_WD, ADAM_STEP = 0.001, 0.9, 0.999, 1e-08, 0.01, 10

VMEM_LIMIT = 48 * 2 ** 20


def _cparams(*sem):
    return pltpu.CompilerParams(dimension_semantics=sem or None, vmem_limit_bytes=VMEM_LIMIT)


def _tile(n, pref):
    if n <= pref:
        return n
    t = pref
    while n % t:
        t -= 128
    assert t > 0
    return t


def _dot(a, b, ca=1, cb=0):
    return lax.dot_general(a.astype(BF16), b.astype(BF16), (((ca,), (cb,)), ((), ())), preferred_element_type=F32)


def _split2(x):
    hi = x.astype(BF16)
    return hi, (x - hi.astype(F32)).astype(BF16)


def _dot3(a, b, ca=1, cb=0):
    a1, a2 = _split2(a)
    b1, b2 = _split2(b)
    return _dot(a1, b1, ca, cb) + (_dot(a1, b2, ca, cb) + _dot(a2, b1, ca, cb))


def _dot_exact(a, b, ca=1, cb=0, ones="a"):
    x = b if ones == "a" else a
    m = (a if ones == "a" else b).astype(BF16)
    hi, rest = x.astype(BF16), None
    rest = x - hi.astype(F32)
    mid = rest.astype(BF16)
    lo = (rest - mid.astype(F32)).astype(BF16)
    parts = [_dot(m, p, ca, cb) if ones == "a" else _dot(p, m, ca, cb) for p in (hi, mid, lo)]
    return parts[0] + (parts[1] + parts[2])


def _sigmoid(z):
    e = jnp.exp(-jnp.abs(z))
    return jnp.where(z >= 0, 1.0, e) / (1.0 + e)


def _softplus(z):
    return jnp.maximum(z, 0.0) + jnp.log(1.0 + jnp.exp(-jnp.abs(z)))


def _mm(a, b, *, name, ta=False, tb=False, out_dtypes=(BF16,), epi=None, extras=()):
    assert a.dtype == BF16 and b.dtype == BF16
    m, k = (a.shape[1], a.shape[0]) if ta else a.shape
    n = b.shape[0] if tb else b.shape[1]
    assert (b.shape[1] if tb else b.shape[0]) == k
    tm, tn, tk = _tile(m, 1024), _tile(n, 1024), _tile(k, 2048)
    nk = k // tk
    ca, cb = (0 if ta else 1), (1 if tb else 0)
    n_ex, n_out = len(extras), len(out_dtypes)

    def body(*refs):
        a_ref, b_ref = refs[0], refs[1]
        ex = refs[2:2 + n_ex]
        outs = refs[2 + n_ex:2 + n_ex + n_out]
        acc = refs[-1]
        kk = pl.program_id(2)
        part = lax.dot_general(a_ref[...], b_ref[...], (((ca,), (cb,)), ((), ())), preferred_element_type=F32)

        def finish(r):
            vals = (r,) if epi is None else epi(r, *[e[...] for e in ex])
            for o, v in zip(outs, vals):
                o[...] = v.astype(o.dtype)

        if nk == 1:
            finish(part)
        else:
            @pl.when(kk == 0)
            def _():
                acc[...] = part

            @pl.when(kk > 0)
            def _():
                acc[...] += part

            @pl.when(kk == nk - 1)
            def _():
                finish(acc[...])

    a_spec = pl.BlockSpec((tk, tm), lambda i, j, kk: (kk, i)) if ta else pl.BlockSpec((tm, tk), lambda i, j, kk: (i, kk))
    b_spec = pl.BlockSpec((tn, tk), lambda i, j, kk: (j, kk)) if tb else pl.BlockSpec((tk, tn), lambda i, j, kk: (kk, j))
    io_spec = pl.BlockSpec((tm, tn), lambda i, j, kk: (i, j))
    res = pl.pallas_call(
        body, name=name, grid=(m // tm, n // tn, nk),
        in_specs=[a_spec, b_spec] + [io_spec] * n_ex,
        out_specs=[io_spec] * n_out,
        out_shape=[jax.ShapeDtypeStruct((m, n), dt) for dt in out_dtypes],
        scratch_shapes=[pltpu.VMEM((tm, tn) if nk > 1 else (8, 128), F32)],
        compiler_params=_cparams("parallel", "parallel", "arbitrary"),
    )(a, b, *extras)
    return res[0] if n_out == 1 else res


ROW_TILE = 512


def _norm_fwd(y, w, res=None, *, out_dtype, name):
    t, d = y.shape
    tm = _tile(t, ROW_TILE)
    has_res = res is not None

    def body(*refs):
        y_ref, w_ref = refs[0], refs[1]
        o_ref = refs[-1]
        yv = y_ref[...]
        r = lax.rsqrt(jnp.mean(yv * yv, axis=-1, keepdims=True) + EPS)
        out = yv * r * w_ref[...]
        if has_res:
            out = out + refs[2][...]
        o_ref[...] = out.astype(o_ref.dtype)

    row = pl.BlockSpec((tm, d), lambda i: (i, 0))
    vec = pl.BlockSpec((1, d), lambda i: (0, 0))
    args = (y, w.reshape(1, d)) + ((res,) if has_res else ())
    return pl.pallas_call(
        body, name=name, grid=(t // tm,), in_specs=[row, vec] + [row] * has_res, out_specs=row,
        out_shape=jax.ShapeDtypeStruct((t, d), out_dtype), compiler_params=_cparams("parallel"),
    )(*args)


def _norm_bwd(y, w, dout, add=None, *, out_dtype, name):
    t, d = y.shape
    tm = _tile(t, ROW_TILE)
    has_add = add is not None

    def body(*refs):
        y_ref, w_ref, do_ref = refs[0], refs[1], refs[2]
        dy_ref, dw_ref = refs[-2], refs[-1]
        yv = y_ref[...]
        r = lax.rsqrt(jnp.mean(yv * yv, axis=-1, keepdims=True) + EPS)
        yh = yv * r
        dov = do_ref[...].astype(F32)
        gw = dov * w_ref[...]
        dy = r * (gw - yh * jnp.mean(gw * yh, axis=-1, keepdims=True))
        if has_add:
            dy = dy + refs[3][...]
        dy_ref[...] = dy.astype(dy_ref.dtype)
        part = jnp.sum(dov * yh, axis=0, keepdims=True)

        @pl.when(pl.program_id(0) == 0)
        def _():
            dw_ref[...] = part

        @pl.when(pl.program_id(0) > 0)
        def _():
            dw_ref[...] += part

    row = pl.BlockSpec((tm, d), lambda i: (i, 0))
    vec = pl.BlockSpec((1, d), lambda i: (0, 0))
    args = (y, w.reshape(1, d), dout) + ((add,) if has_add else ())
    return pl.pallas_call(
        body, name=name, grid=(t // tm,), in_specs=[row, vec, row] + [row] * has_add, out_specs=[row, vec],
        out_shape=[jax.ShapeDtypeStruct((t, d), out_dtype), jax.ShapeDtypeStruct((1, d), F32)],
        compiler_params=_cparams("arbitrary"),
    )(*args)


def _shift_down(u, s):
    if s == 0:
        return u
    rows = lax.broadcasted_iota(jnp.int32, u.shape, 0)
    return jnp.where(rows >= s, pltpu.roll(u, s, 0), 0.0)


def _shift_up(u, s):
    if s == 0:
        return u
    t = u.shape[0]
    rows = lax.broadcasted_iota(jnp.int32, u.shape, 0)
    return jnp.where(rows < t - s, pltpu.roll(u, t - s, 0), 0.0)


def _conv_fwd(u, w):
    kk = w.shape[0]
    out = u * w[kk - 1:kk, :]
    for i in range(kk - 1):
        out = out + _shift_down(u, kk - 1 - i) * w[i:i + 1, :]
    return out


def _conv_bwd(u, w, dc):
    kk = w.shape[0]
    du = dc * w[kk - 1:kk, :]
    dws = []
    for i in range(kk):
        s = kk - 1 - i
        if s:
            du = du + _shift_up(dc, s) * w[i:i + 1, :]
        dws.append(jnp.sum(dc * _shift_down(u, s), axis=0, keepdims=True))
    return du, dws


def _gdn_pre_math(x, w, slab):
    c = _conv_fwd(x, w)
    sig = _sigmoid(c)
    s = c * sig
    r = lax.rsqrt(jnp.sum(s * s, axis=-1, keepdims=True) + EPS)
    scale = jnp.where(slab < GDN_HEADS, GDN_DIM ** -0.5, 1.0)
    return c, sig, s, r, scale


def _gdn_pre_fwd(proj, conv_w, *, name):
    t = proj.shape[0]
    nslab = 3 * GDN_HEADS

    def body(x_ref, w_ref, o_ref):
        slab = pl.program_id(0)
        _, _, s, r, scale = _gdn_pre_math(x_ref[...].astype(F32), w_ref[...], slab)
        o_ref[0] = jnp.where(slab < 2 * GDN_HEADS, s * r * scale, s)

    return pl.pallas_call(
        body, name=name, grid=(nslab,),
        in_specs=[pl.BlockSpec((t, GDN_DIM), lambda j: (0, j)), pl.BlockSpec((GDN_CONV, GDN_DIM), lambda j: (0, j))],
        out_specs=pl.BlockSpec((1, t, GDN_DIM), lambda j: (j, 0, 0)),
        out_shape=jax.ShapeDtypeStruct((nslab, t, GDN_DIM), F32), compiler_params=_cparams("parallel"),
    )(proj, conv_w)


def _gdn_pre_bwd(proj, conv_w, dqkv, *, name):
    t = proj.shape[0]
    nslab = 3 * GDN_HEADS

    def body(x_ref, w_ref, d_ref, dx_ref, dw_ref):
        slab = pl.program_id(0)
        x = x_ref[...].astype(F32)
        w = w_ref[...]
        c, sig, s, r, scale = _gdn_pre_math(x, w, slab)
        dout = d_ref[0]
        yn = s * r
        dn = dout * scale
        ds_norm = r * (dn - yn * jnp.sum(dn * yn, axis=-1, keepdims=True))
        ds = jnp.where(slab < 2 * GDN_HEADS, ds_norm, dout)
        dc = ds * (sig + c * sig * (1.0 - sig))
        dx, dws = _conv_bwd(x, w, dc)
        dx_ref[...] = dx.astype(dx_ref.dtype)
        for i, dwi in enumerate(dws):
            dw_ref[i:i + 1, :] = dwi

    return pl.pallas_call(
        body, name=name, grid=(nslab,),
        in_specs=[pl.BlockSpec((t, GDN_DIM), lambda j: (0, j)), pl.BlockSpec((GDN_CONV, GDN_DIM), lambda j: (0, j)),
                  pl.BlockSpec((1, t, GDN_DIM), lambda j: (j, 0, 0))],
        out_specs=[pl.BlockSpec((t, GDN_DIM), lambda j: (0, j)), pl.BlockSpec((GDN_CONV, GDN_DIM), lambda j: (0, j))],
        out_shape=[jax.ShapeDtypeStruct((t, 3 * GDN_W), BF16), jax.ShapeDtypeStruct((GDN_CONV, 3 * GDN_W), F32)],
        compiler_params=_cparams("parallel"),
    )(proj, conv_w, dqkv)


def _sc_specs(t):
    def col(base):
        return pl.BlockSpec((t, 128), lambda j: (0, base // 128 + j))
    return [col(C_SCX), col(C_SCB), col(C_SCC), pl.BlockSpec((SC_CONV, 128), lambda j: (0, j))]


def _sc_fwd(proj, conv_w, *, name):
    t = proj.shape[0]

    def body(x_ref, b_ref, c_ref, w_ref, o_ref):
        u = c_ref[...].astype(F32) * x_ref[...].astype(F32)
        o_ref[...] = (b_ref[...].astype(F32) * _conv_fwd(u, w_ref[...])).astype(o_ref.dtype)

    return pl.pallas_call(
        body, name=name, grid=(SC_W // 128,), in_specs=_sc_specs(t),
        out_specs=pl.BlockSpec((t, 128), lambda j: (0, j)),
        out_shape=jax.ShapeDtypeStruct((t, SC_W), BF16), compiler_params=_cparams("parallel"),
    )(proj, proj, proj, conv_w)


def _sc_bwd(proj, conv_w, dy, *, name):
    t = proj.shape[0]
    nj = SC_W // 128

    def body(x_ref, b_ref, c_ref, w_ref, dy_ref, dx_ref, db_ref, dc_ref, dw_ref):
        x, b, c = x_ref[...].astype(F32), b_ref[...].astype(F32), c_ref[...].astype(F32)
        w = w_ref[...]
        u = c * x
        dyv = dy_ref[...].astype(F32)
        db_ref[...] = (dyv * _conv_fwd(u, w)).astype(db_ref.dtype)
        du, dws = _conv_bwd(u, w, dyv * b)
        dx_ref[...] = (du * c).astype(dx_ref.dtype)
        dc_ref[...] = (du * x).astype(dc_ref.dtype)
        for i, dwi in enumerate(dws):
            dw_ref[i:i + 1, :] = dwi

    return pl.pallas_call(
        body, name=name, grid=(nj,),
        in_specs=_sc_specs(t) + [pl.BlockSpec((t, 128), lambda j: (0, j))],
        out_specs=[pl.BlockSpec((t, 128), lambda j: (0, j))] * 3 + [pl.BlockSpec((SC_CONV, 128), lambda j: (0, j))],
        out_shape=[jax.ShapeDtypeStruct((t, SC_W), BF16)] * 3 + [jax.ShapeDtypeStruct((SC_CONV, SC_W), F32)],
        compiler_params=_cparams("parallel"),
    )(proj, proj, proj, conv_w, dy)


def _tri_inv(a_strict):
    c = a_strict.shape[0]
    ri = lax.broadcasted_iota(jnp.int32, (c, c), 0)
    ci = lax.broadcasted_iota(jnp.int32, (c, c), 1)
    eye = (ri == ci).astype(F32)
    blk = 8
    bm = -jnp.where(ri // blk == ci // blk, a_strict, 0.0)
    inv = eye + bm
    pw = bm
    for _ in range(2):
        pw = _dot3(pw, pw)
        inv = inv + _dot3(inv, pw)
    while blk < c:
        off = jnp.where((ri // (2 * blk) == ci // (2 * blk)) & (ri // blk != ci // blk), a_strict, 0.0)
        inv = inv - _dot3(_dot3(inv, off), inv)
        blk *= 2
    return inv


def _gdn_chunk(q, k, v, ab, head, ea, dtb):
    c = q.shape[0]
    lane = lax.broadcasted_iota(jnp.int32, ab.shape, 1)
    a = jnp.sum(jnp.where(lane == head, ab, 0.0), axis=1, keepdims=True)
    b = jnp.sum(jnp.where(lane == GDN_HEADS + head, ab, 0.0), axis=1, keepdims=True)
    ri = lax.broadcasted_iota(jnp.int32, (c, c), 0)
    ci = lax.broadcasted_iota(jnp.int32, (c, c), 1)
    tri, strict = ri >= ci, ri > ci
    ltri = tri.astype(F32)
    beta = _sigmoid(b)
    sig_a = _sigmoid(a + dtb)
    g = -ea * _softplus(a + dtb)
    g_cc = jnp.broadcast_to(g, (c, c))
    gi = _dot_exact(ltri, g_cc)
    gj = _dot_exact(g_cc, (ri <= ci).astype(F32), 0, 0, ones="b")
    decay = jnp.exp(jnp.where(tri, gi - gj, -1e30))
    gc = _dot_exact(ltri, jnp.broadcast_to(g, (c, GDN_DIM)))
    g_tot = jnp.sum(g, axis=0, keepdims=True)
    egc = jnp.exp(gc)
    ekd = jnp.exp(g_tot - gc)
    kb, vb = k * beta, v * beta
    kbg = kb * egc
    mkk = _dot3(kb, k, 1, 1)
    a_kk = jnp.where(strict, mkk * decay, 0.0)
    tinv = _tri_inv(a_kk)
    u = _dot3(tinv, vb)
    w = _dot3(tinv, kbg)
    mqk = _dot3(q, k, 1, 1)
    a_qk = jnp.where(tri, mqk * decay, 0.0)
    return dict(beta=beta, sig_a=sig_a, g=g, decay=decay, egc=egc, ekd=ekd, g_tot=g_tot, kb=kb, vb=vb, kbg=kbg,
                a_kk=a_kk, tinv=tinv, u=u, w=w, a_qk=a_qk, qd=q * egc, kd=k * ekd, tri=tri, strict=strict)


def _chunks_per_step(n):
    return 4 if n % 4 == 0 else 1


def _gdn_local_specs(t, cps):
    rows = cps * CHUNK

    def slab(base):
        return pl.BlockSpec((1, rows, GDN_DIM), lambda h, n: (base + h, n, 0))
    smem = pl.BlockSpec(memory_space=pltpu.SMEM)
    return [slab(0), slab(GDN_HEADS), slab(2 * GDN_HEADS), pl.BlockSpec((rows, AB_W), lambda h, n: (n, 0)), smem, smem]


def _scalar_row(ref, head):
    return jnp.full((1, 1), ref[0, head], F32)


def _gdn_local_fwd(qkv, ab, a_log, dt_bias, *, name):
    t = qkv.shape[1]
    n = t // CHUNK
    cps = _chunks_per_step(n)
    rows = cps * CHUNK

    def body(q_ref, k_ref, v_ref, ab_ref, al_ref, dt_ref, u_ref, w_ref, qd_ref, kd_ref, aqk_ref, gl_ref):
        head = pl.program_id(0)
        ea = jnp.exp(_scalar_row(al_ref, head))
        dtb = _scalar_row(dt_ref, head)
        for j in range(cps):
            sl = slice(j * CHUNK, (j + 1) * CHUNK)
            r = _gdn_chunk(q_ref[0, sl, :], k_ref[0, sl, :], v_ref[0, sl, :], ab_ref[sl, :], head, ea, dtb)
            u_ref[0, sl, :] = r["u"]
            w_ref[0, sl, :] = r["w"]
            qd_ref[0, sl, :] = r["qd"]
            kd_ref[0, sl, :] = r["kd"]
            aqk_ref[0, sl, :] = r["a_qk"]
            gl_ref[0, j] = jnp.exp(jnp.broadcast_to(r["g_tot"], (1, GDN_DIM)))

    big = pl.BlockSpec((1, rows, GDN_DIM), lambda h, i: (h, i, 0))
    big_shape = jax.ShapeDtypeStruct((GDN_HEADS, t, GDN_DIM), F32)
    return pl.pallas_call(
        body, name=name, grid=(GDN_HEADS, n // cps), in_specs=_gdn_local_specs(t, cps),
        out_specs=[big] * 4 + [pl.BlockSpec((1, rows, CHUNK), lambda h, i: (h, i, 0)),
                               pl.BlockSpec((1, cps, 1, GDN_DIM), lambda h, i: (h, i, 0, 0))],
        out_shape=[big_shape] * 4 + [jax.ShapeDtypeStruct((GDN_HEADS, t, CHUNK), F32),
                                     jax.ShapeDtypeStruct((GDN_HEADS, n, 1, GDN_DIM), F32)],
        compiler_params=_cparams("parallel", "parallel"),
    )(qkv, qkv, qkv, ab, a_log, dt_bias)


def _gdn_scan_fwd(u, w, qd, kd, aqk, gl, *, name):
    h, t, _ = u.shape
    n = t // CHUNK

    def body(u_ref, w_ref, qd_ref, kd_ref, aqk_ref, gl_ref, o_ref, s_ref, state):
        @pl.when(pl.program_id(0) == 0)
        def _():
            state[...] = jnp.zeros_like(state)

        for hh in range(h):
            s = state[hh]
            s_ref[hh, 0] = s
            vn = u_ref[hh] - _dot3(w_ref[hh], s)
            o_ref[hh] = _dot3(qd_ref[hh], s) + _dot3(aqk_ref[hh], vn)
            state[hh] = s * gl_ref[hh, 0] + _dot3(kd_ref[hh], vn, 0, 0)

    big = pl.BlockSpec((h, CHUNK, GDN_DIM), lambda i: (0, i, 0))
    return pl.pallas_call(
        body, name=name, grid=(n,),
        in_specs=[big] * 4 + [pl.BlockSpec((h, CHUNK, CHUNK), lambda i: (0, i, 0)),
                              pl.BlockSpec((h, 1, 1, GDN_DIM), lambda i: (0, i, 0, 0))],
        out_specs=[big, pl.BlockSpec((h, 1, GDN_DIM, GDN_DIM), lambda i: (0, i, 0, 0))],
        out_shape=[jax.ShapeDtypeStruct((h, t, GDN_DIM), F32), jax.ShapeDtypeStruct((h, n, GDN_DIM, GDN_DIM), F32)],
        scratch_shapes=[pltpu.VMEM((h, GDN_DIM, GDN_DIM), F32)],
        compiler_params=_cparams("arbitrary"),
    )(u, w, qd, kd, aqk, gl)


def _gdn_scan_bwd(u, w, qd, kd, aqk, gl, states, do, *, name):
    h, t, _ = u.shape
    n = t // CHUNK

    def body(u_ref, w_ref, qd_ref, kd_ref, aqk_ref, gl_ref, s_ref, do_ref,
             du_ref, dw_ref, dqd_ref, dkd_ref, daqk_ref, dgl_ref, dstate):
        @pl.when(pl.program_id(0) == 0)
        def _():
            dstate[...] = jnp.zeros_like(dstate)

        ri = lax.broadcasted_iota(jnp.int32, (CHUNK, CHUNK), 0)
        ci = lax.broadcasted_iota(jnp.int32, (CHUNK, CHUNK), 1)
        for hh in range(h):
            s, ds_next, dov, wv = s_ref[hh, 0], dstate[hh], do_ref[hh], w_ref[hh]
            vn = u_ref[hh] - _dot3(wv, s)
            dvn = _dot3(aqk_ref[hh], dov, 0, 0) + _dot3(kd_ref[hh], ds_next)
            du_ref[hh] = dvn
            dw_ref[hh] = -_dot3(dvn, s, 1, 1)
            dqd_ref[hh] = _dot3(dov, s, 1, 1)
            dkd_ref[hh] = _dot3(vn, ds_next, 1, 1)
            daqk_ref[hh] = jnp.where(ri >= ci, _dot3(dov, vn, 1, 1), 0.0)
            dgl_ref[hh, 0] = jnp.sum(ds_next * s, axis=0, keepdims=True)
            dstate[hh] = (_dot3(qd_ref[hh], dov, 0, 0) + ds_next * gl_ref[hh, 0]
                          - _dot3(wv, dvn, 0, 0))

    big = pl.BlockSpec((h, CHUNK, GDN_DIM), lambda i: (0, n - 1 - i, 0))
    sq = pl.BlockSpec((h, CHUNK, CHUNK), lambda i: (0, n - 1 - i, 0))
    glb = pl.BlockSpec((h, 1, 1, GDN_DIM), lambda i: (0, n - 1 - i, 0, 0))
    big_shape = jax.ShapeDtypeStruct((h, t, GDN_DIM), F32)
    return pl.pallas_call(
        body, name=name, grid=(n,),
        in_specs=[big] * 4 + [sq, glb, pl.BlockSpec((h, 1, GDN_DIM, GDN_DIM), lambda i: (0, n - 1 - i, 0, 0)), big],
        out_specs=[big] * 4 + [sq, glb],
        out_shape=[big_shape] * 4 + [jax.ShapeDtypeStruct((h, t, CHUNK), F32),
                                     jax.ShapeDtypeStruct((h, n, 1, GDN_DIM), F32)],
        scratch_shapes=[pltpu.VMEM((h, GDN_DIM, GDN_DIM), F32)],
        compiler_params=_cparams("arbitrary"),
    )(u, w, qd, kd, aqk, gl, states, do)


def _gdn_local_bwd(qkv, ab, a_log, dt_bias, du, dw, dqd, dkd, daqk, dgl, *, name):
    t = qkv.shape[1]
    n = t // CHUNK
    cps = _chunks_per_step(n)
    rows = cps * CHUNK

    def body(q_ref, k_ref, v_ref, ab_ref, al_ref, dt_ref, du_ref, dw_ref, dqd_ref, dkd_ref, daqk_ref, dgl_ref,
             dq_ref, dk_ref, dv_ref, dab_ref, dsc_ref):
        head = pl.program_id(0)
        ea = jnp.exp(_scalar_row(al_ref, head))
        dtb = _scalar_row(dt_ref, head)
        lane = lax.broadcasted_iota(jnp.int32, (CHUNK, AB_W), 1)
        lane1 = lax.broadcasted_iota(jnp.int32, (1, GDN_DIM), 1)
        ri = lax.broadcasted_iota(jnp.int32, (CHUNK, CHUNK), 0)
        ci = lax.broadcasted_iota(jnp.int32, (CHUNK, CHUNK), 1)
        utri = (ri <= ci).astype(F32)
        ones = jnp.ones((CHUNK, GDN_DIM), F32)
        acc_alog = jnp.zeros((1, 1), F32)
        acc_dtb = jnp.zeros((1, 1), F32)
        for j in range(cps):
            sl = slice(j * CHUNK, (j + 1) * CHUNK)
            q, k, v = q_ref[0, sl, :], k_ref[0, sl, :], v_ref[0, sl, :]
            r = _gdn_chunk(q, k, v, ab_ref[sl, :], head, ea, dtb)
            duv, dwv, dqdv, dkdv = du_ref[0, sl, :], dw_ref[0, sl, :], dqd_ref[0, sl, :], dkd_ref[0, sl, :]
            d_aqk = jnp.where(r["tri"], daqk_ref[0, sl, :], 0.0)
            dvb = _dot3(r["tinv"], duv, 0, 0)
            dkbg = _dot3(r["tinv"], dwv, 0, 0)
            d_akk = -jnp.where(r["strict"], _dot3(dvb, r["u"], 1, 1) + _dot3(dkbg, r["w"], 1, 1), 0.0)
            e = d_akk * r["a_kk"] + d_aqk * r["a_qk"]
            dmkk, dmqk = d_akk * r["decay"], d_aqk * r["decay"]
            dkb = _dot3(dmkk, k) + dkbg * r["egc"]
            dk = (_dot3(dmkk, r["kb"], 0, 0) + _dot3(dmqk, q, 0, 0) + dkdv * r["ekd"]
                  + dkb * r["beta"])
            dq = _dot3(dmqk, k) + dqdv * r["egc"]
            dq_ref[0, sl, :] = dq
            dk_ref[0, sl, :] = dk
            dv_ref[0, sl, :] = dvb * r["beta"]
            dbeta = jnp.sum(dkb * k + dvb * v, axis=1, keepdims=True)
            kd_term = jnp.sum(dkdv * r["kd"], axis=1, keepdims=True)
            dgc = (jnp.sum(e, axis=1, keepdims=True) + jnp.sum(dqdv * r["qd"] + dkbg * r["kbg"], axis=1, keepdims=True)
                   - kd_term)
            dgc_lanes = jnp.broadcast_to(dgc, (CHUNK, GDN_DIM)) - _dot_exact(e, ones, 0, 0, ones="b")
            dgl_tot = jnp.sum(dgl_ref[0, j], axis=1, keepdims=True) * jnp.exp(r["g_tot"])
            d_tot = jnp.sum(kd_term, axis=0, keepdims=True) + dgl_tot
            dg = _dot_exact(utri, dgc_lanes) + d_tot
            dg = jnp.sum(jnp.where(lane == 0, dg, 0.0), axis=1, keepdims=True)
            da = dg * (-ea) * r["sig_a"]
            db = dbeta * r["beta"] * (1.0 - r["beta"])
            dab_ref[0, sl, :] = jnp.where(lane == head, da, 0.0) + jnp.where(lane == GDN_HEADS + head, db, 0.0)
            acc_alog = acc_alog + jnp.sum(dg * r["g"], axis=0, keepdims=True)
            acc_dtb = acc_dtb + jnp.sum(da, axis=0, keepdims=True)
        dsc_ref[0, 0] = jnp.where(lane1 == 0, acc_alog, 0.0) + jnp.where(lane1 == 1, acc_dtb, 0.0)

    big = pl.BlockSpec((1, rows, GDN_DIM), lambda h, i: (h, i, 0))
    big_shape = jax.ShapeDtypeStruct((GDN_HEADS, t, GDN_DIM), F32)
    return pl.pallas_call(
        body, name=name, grid=(GDN_HEADS, n // cps),
        in_specs=_gdn_local_specs(t, cps) + [big] * 4 + [pl.BlockSpec((1, rows, CHUNK), lambda h, i: (h, i, 0)),
                                                        pl.BlockSpec((1, cps, 1, GDN_DIM), lambda h, i: (h, i, 0, 0))],
        out_specs=[big] * 4 + [pl.BlockSpec((1, 1, 1, GDN_DIM), lambda h, i: (h, i, 0, 0))],
        out_shape=[big_shape] * 4 + [jax.ShapeDtypeStruct((GDN_HEADS, n // cps, 1, GDN_DIM), F32)],
        compiler_params=_cparams("parallel", "parallel"),
    )(qkv, qkv, qkv, ab, a_log, dt_bias, du, dw, dqd, dkd, daqk, dgl)


def _gdn_post_fwd(o, proj, norm_w, *, name):
    h, t, _ = o.shape
    tm = _tile(t, ROW_TILE)

    def body(o_ref, g_ref, w_ref, y_ref):
        for hh in range(h):
            sl = slice(hh * GDN_DIM, (hh + 1) * GDN_DIM)
            ov = o_ref[hh]
            gate = g_ref[:, sl].astype(F32)
            r = lax.rsqrt(jnp.mean(ov * ov, axis=-1, keepdims=True) + EPS)
            y_ref[:, sl] = (ov * r * w_ref[...] * (gate * _sigmoid(gate))).astype(y_ref.dtype)

    return pl.pallas_call(
        body, name=name, grid=(t // tm,),
        in_specs=[pl.BlockSpec((h, tm, GDN_DIM), lambda i: (0, i, 0)),
                  pl.BlockSpec((tm, GDN_W), lambda i: (i, C_GGATE // GDN_W)),
                  pl.BlockSpec((1, GDN_DIM), lambda i: (0, 0))],
        out_specs=pl.BlockSpec((tm, GDN_W), lambda i: (i, 0)),
        out_shape=jax.ShapeDtypeStruct((t, GDN_W), BF16), compiler_params=_cparams("parallel"),
    )(o, proj, norm_w.reshape(1, GDN_DIM))


def _gdn_post_bwd(o, proj, norm_w, dy, *, name):
    h, t, _ = o.shape
    tm = _tile(t, ROW_TILE)

    def body(o_ref, g_ref, w_ref, dy_ref, do_ref, dg_ref, dw_ref):
        part = jnp.zeros((1, GDN_DIM), F32)
        for hh in range(h):
            sl = slice(hh * GDN_DIM, (hh + 1) * GDN_DIM)
            ov = o_ref[hh]
            gate = g_ref[:, sl].astype(F32)
            sig = _sigmoid(gate)
            silu = gate * sig
            r = lax.rsqrt(jnp.mean(ov * ov, axis=-1, keepdims=True) + EPS)
            oh = ov * r
            dyv = dy_ref[:, sl].astype(F32)
            dg_ref[:, sl] = (dyv * oh * w_ref[...] * (sig + silu * (1.0 - sig))).astype(dg_ref.dtype)
            dn = dyv * silu
            part = part + jnp.sum(dn * oh, axis=0, keepdims=True)
            gw = dn * w_ref[...]
            do_ref[hh] = r * (gw - oh * jnp.mean(gw * oh, axis=-1, keepdims=True))

        @pl.when(pl.program_id(0) == 0)
        def _():
            dw_ref[...] = part

        @pl.when(pl.program_id(0) > 0)
        def _():
            dw_ref[...] += part

    return pl.pallas_call(
        body, name=name, grid=(t // tm,),
        in_specs=[pl.BlockSpec((h, tm, GDN_DIM), lambda i: (0, i, 0)),
                  pl.BlockSpec((tm, GDN_W), lambda i: (i, C_GGATE // GDN_W)),
                  pl.BlockSpec((1, GDN_DIM), lambda i: (0, 0)),
                  pl.BlockSpec((tm, GDN_W), lambda i: (i, 0))],
        out_specs=[pl.BlockSpec((h, tm, GDN_DIM), lambda i: (0, i, 0)), pl.BlockSpec((tm, GDN_W), lambda i: (i, 0)),
                   pl.BlockSpec((1, GDN_DIM), lambda i: (0, 0))],
        out_shape=[jax.ShapeDtypeStruct((h, t, GDN_DIM), F32), jax.ShapeDtypeStruct((t, GDN_W), BF16),
                   jax.ShapeDtypeStruct((1, GDN_DIM), F32)],
        compiler_params=_cparams("arbitrary"),
    )(o, proj, norm_w.reshape(1, GDN_DIM), dy)


def _split_dot(x, m):
    hi = x.astype(BF16)
    lo = (x - hi.astype(F32)).astype(BF16)
    return _dot(hi, m) + _dot(lo, m)


def _sb_block(q, kblk, qi, kb):
    bq, bk = q.shape[0], kblk.shape[0]
    z = _dot(q, kblk, 1, 1) * (SB_DIM ** -0.5)
    t_idx = qi * bq + lax.broadcasted_iota(jnp.int32, (bq, bk), 0)
    s_idx = kb * bk + lax.broadcasted_iota(jnp.int32, (bq, bk), 1)
    mask = s_idx < t_idx
    e = jnp.exp(-jnp.abs(z))
    sp = jnp.where(mask, jnp.maximum(z, 0.0) + jnp.log(1.0 + e), 0.0)
    return z, mask, e, sp


def _suffix_ones(blk):
    ri = lax.broadcasted_iota(jnp.int32, (blk, blk), 0)
    ci = lax.broadcasted_iota(jnp.int32, (blk, blk), 1)
    return (ri >= ci).astype(BF16)


def _prefix_ones(blk):
    ri = lax.broadcasted_iota(jnp.int32, (blk, blk), 0)
    ci = lax.broadcasted_iota(jnp.int32, (blk, blk), 1)
    return (ri <= ci).astype(BF16)


def _sb_blocks(t):
    bq = _tile(t, SB_QBLOCK)
    bk = _tile(bq, SB_KBLOCK)
    return bq, bk


def _sb_fwd(q, k, v, *, name):
    h, t, d = q.shape
    bq, bk = _sb_blocks(t)

    def body(q_ref, k_ref, v_ref, o_ref):
        qi = pl.program_id(1)
        qv = q_ref[0]
        suffix = _suffix_ones(bk)
        nkb = (qi + 1) * (bq // bk)

        def step(it, carry):
            later, acc = carry
            kb = nkb - 1 - it
            rows = pl.ds(pl.multiple_of(kb * bk, bk), bk)
            z, mask, _, sp = _sb_block(qv, k_ref[0, rows, :], qi, kb)
            csum = _split_dot(sp, suffix)
            att = jnp.where(mask, jnp.exp(z - csum - later), 0.0)
            acc = acc + _dot(att, v_ref[0, rows, :])
            return later + jnp.sum(sp, axis=1, keepdims=True), acc

        _, acc = lax.fori_loop(0, nkb, step, (jnp.zeros((bq, 1), F32), jnp.zeros((bq, d), F32)))
        o_ref[0] = acc

    qspec = pl.BlockSpec((1, bq, d), lambda hh, i: (hh, i, 0))
    kvspec = pl.BlockSpec((1, t, d), lambda hh, i: (hh, 0, 0))
    return pl.pallas_call(
        body, name=name, grid=(h, t // bq), in_specs=[qspec, kvspec, kvspec], out_specs=qspec,
        out_shape=jax.ShapeDtypeStruct((h, t, d), F32), compiler_params=_cparams("parallel", "parallel"),
    )(q, k, v)


def _sb_bwd(q, k, v, do, *, name):
    h, t, d = q.shape
    bq, bk = _sb_blocks(t)
    scale = SB_DIM ** -0.5

    def body(q_ref, k_ref, v_ref, do_ref, dq_ref, dk_ref, dv_ref, dl_keep, sig_keep):
        qi = pl.program_id(1)

        @pl.when(qi == 0)
        def _():
            dk_ref[...] = jnp.zeros_like(dk_ref)
            dv_ref[...] = jnp.zeros_like(dv_ref)

        qv = q_ref[0]
        dov = do_ref[0]
        suffix = _suffix_ones(bk)
        prefix = _prefix_ones(bk)
        nkb = (qi + 1) * (bq // bk)

        def back(it, later):
            kb = nkb - 1 - it
            rows = pl.ds(pl.multiple_of(kb * bk, bk), bk)
            vblk = v_ref[0, rows, :]
            z, mask, e, sp = _sb_block(qv, k_ref[0, rows, :], qi, kb)
            csum = _split_dot(sp, suffix)
            att = jnp.where(mask, jnp.exp(z - csum - later), 0.0)
            dv_ref[0, rows, :] += _dot(att, dov, 0, 0)
            dl_keep[kb] = att * _dot(dov, vblk, 1, 1)
            sig_keep[kb] = jnp.where(mask, jnp.where(z >= 0, 1.0, e) / (1.0 + e), 0.0)
            return later + jnp.sum(sp, axis=1, keepdims=True)

        lax.fori_loop(0, nkb, back, jnp.zeros((bq, 1), F32))

        def forth(kb, carry):
            before, dq = carry
            rows = pl.ds(pl.multiple_of(kb * bk, bk), bk)
            dl = dl_keep[kb]
            dz = (dl - sig_keep[kb] * (before + _split_dot(dl, prefix))) * scale
            dk_ref[0, rows, :] += _dot(dz, qv, 0, 0)
            return before + jnp.sum(dl, axis=1, keepdims=True), dq + _dot(dz, k_ref[0, rows, :])

        _, dq = lax.fori_loop(0, nkb, forth, (jnp.zeros((bq, 1), F32), jnp.zeros((bq, d), F32)))
        dq_ref[0] = dq

    qspec = pl.BlockSpec((1, bq, d), lambda hh, i: (hh, i, 0))
    kvspec = pl.BlockSpec((1, t, d), lambda hh, i: (hh, 0, 0))
    shape = jax.ShapeDtypeStruct((h, t, d), F32)
    return pl.pallas_call(
        body, name=name, grid=(h, t // bq), in_specs=[qspec, kvspec, kvspec, qspec],
        out_specs=[qspec, kvspec, kvspec], out_shape=[shape] * 3,
        scratch_shapes=[pltpu.VMEM((t // bk, bq, bk), F32), pltpu.VMEM((t // bk, bq, bk), F32)],
        compiler_params=_cparams("parallel", "arbitrary"),
    )(q, k, v, do)


def _gate_specs(tm):
    return [pl.BlockSpec((tm, D_MODEL), lambda i, b=b: (i, C_GATES // D_MODEL + b)) for b in range(3)]


def _merge_fwd(p, proj, *, name):
    t = proj.shape[0]
    tm = _tile(t, ROW_TILE)

    def body(p0, p1, p2, g0, g1, g2, o_ref):
        acc = jnp.zeros((tm, D_MODEL), F32)
        for pr, gr in ((p0, g0), (p1, g1), (p2, g2)):
            acc = acc + _sigmoid(gr[...].astype(F32)) * pr[...].astype(F32)
        o_ref[...] = acc.astype(o_ref.dtype)

    row = pl.BlockSpec((tm, D_MODEL), lambda i: (i, 0))
    return pl.pallas_call(
        body, name=name, grid=(t // tm,), in_specs=[row] * 3 + _gate_specs(tm), out_specs=row,
        out_shape=jax.ShapeDtypeStruct((t, D_MODEL), BF16), compiler_params=_cparams("parallel"),
    )(*p, proj, proj, proj)


def _merge_bwd(p, proj, dmerged, *, name):
    t = proj.shape[0]
    tm = _tile(t, ROW_TILE)

    def body(p0, p1, p2, g0, g1, g2, dm_ref, dp0, dp1, dp2, dg_ref):
        dm = dm_ref[...].astype(F32)
        for b, (pr, gr, dpr) in enumerate(((p0, g0, dp0), (p1, g1, dp1), (p2, g2, dp2))):
            s = _sigmoid(gr[...].astype(F32))
            dpr[...] = (dm * s).astype(dpr.dtype)
            dg_ref[:, b * D_MODEL:(b + 1) * D_MODEL] = (dm * pr[...].astype(F32) * s * (1.0 - s)).astype(dg_ref.dtype)

    row = pl.BlockSpec((tm, D_MODEL), lambda i: (i, 0))
    res = pl.pallas_call(
        body, name=name, grid=(t // tm,), in_specs=[row] * 3 + _gate_specs(tm) + [row],
        out_specs=[row] * 3 + [pl.BlockSpec((tm, 3 * D_MODEL), lambda i: (i, 0))],
        out_shape=[jax.ShapeDtypeStruct((t, D_MODEL), BF16)] * 3 + [jax.ShapeDtypeStruct((t, 3 * D_MODEL), BF16)],
        compiler_params=_cparams("parallel"),
    )(*p, proj, proj, proj, dmerged)
    return res[:3], res[3]


def _loss_head(y, target, *, name):
    t, d = y.shape
    tm = _tile(t, ROW_TILE)

    def body(y_ref, t_ref, dy_ref, l_ref):
        err = y_ref[...] - t_ref[...]
        dy_ref[...] = err * (1.0 / d)
        part = jnp.sum(err * err, axis=0, keepdims=True) * (0.5 / d)

        @pl.when(pl.program_id(0) == 0)
        def _():
            l_ref[...] = part

        @pl.when(pl.program_id(0) > 0)
        def _():
            l_ref[...] += part

    row = pl.BlockSpec((tm, d), lambda i: (i, 0))
    vec = pl.BlockSpec((1, d), lambda i: (0, 0))
    return pl.pallas_call(
        body, name=name, grid=(t // tm,), in_specs=[row, row], out_specs=[row, vec],
        out_shape=[jax.ShapeDtypeStruct((t, d), F32), jax.ShapeDtypeStruct((1, d), F32)],
        compiler_params=_cparams("arbitrary"),
    )(y, target)


def _adamw(w, g, m, v, *, name):
    r, c = w.shape
    tr = r if r * c * 4 <= 2 ** 21 else max(8, (2 ** 21 // (c * 4)) // 8 * 8)
    while r % tr:
        tr -= 8
    c1 = 1.0 - ADAM_B1 ** ADAM_STEP
    c2 = 1.0 - ADAM_B2 ** ADAM_STEP

    def body(w_ref, g_ref, m_ref, v_ref, d_ref, nm_ref, nv_ref):
        gv = g_ref[...]
        nm = ADAM_B1 * m_ref[...] + (1.0 - ADAM_B1) * gv
        nv = ADAM_B2 * v_ref[...] + (1.0 - ADAM_B2) * (gv * gv)
        nm_ref[...] = nm
        nv_ref[...] = nv
        d_ref[...] = -ADAM_LR * ((nm / c1) / (jnp.sqrt(nv / c2) + ADAM_EPS) + ADAM_WD * w_ref[...])

    spec = pl.BlockSpec((tr, c), lambda i: (i, 0))
    return pl.pallas_call(
        body, name=name, grid=(r // tr,), in_specs=[spec] * 4, out_specs=[spec] * 3,
        out_shape=[jax.ShapeDtypeStruct((r, c), F32)] * 3, compiler_params=_cparams("parallel"),
    )(w, g, m, v)


def _all_gather(xs, *, name):
    n = len(xs)

    def body(*refs):
        x_refs, out_refs = refs[:n], refs[n:2 * n]
        send_sems, recv_sems, local_sems = refs[2 * n:]
        mx, my, mc = lax.axis_index("x"), lax.axis_index("y"), lax.axis_index("c")
        me, sibling = (mx, my, mc), (mx, my, 1 - mc)
        chips = [(1 - mx, my), (mx, 1 - my), (1 - mx, 1 - my)]

        def slot(a, px, py, pc):
            return out_refs[a].at[4 * px + 2 * py + pc]

        def copy(a, k, block, to, src=None):
            return pltpu.make_async_remote_copy(
                src_ref=slot(a, *block) if src is None else src, dst_ref=slot(a, *block),
                send_sem=send_sems.at[a, k], recv_sem=recv_sems.at[a, k], device_id=to, device_id_type=MESH_ID)

        mine = [pltpu.make_async_copy(x_refs[a], slot(a, *me), local_sems.at[a]) for a in range(n)]
        for cp in mine:
            cp.start()
        first = [copy(a, 1 + j, me, (*chip, mc), src=x_refs[a]) for j, chip in enumerate(chips) for a in range(n)]
        first += [copy(a, 0, me, sibling, src=x_refs[a]) for a in range(n)]
        for cp in first:
            cp.start()
        passed = []
        for j, chip in enumerate(chips):
            for a in range(n):
                copy(a, 1 + j, (*chip, mc), me).wait_recv()
                passed.append(copy(a, 4 + j, (*chip, mc), sibling))
                passed[-1].start()
        for a in range(n):
            copy(a, 0, sibling, me).wait_recv()
        for j, chip in enumerate(chips):
            for a in range(n):
                copy(a, 4 + j, (*chip, 1 - mc), me).wait_recv()
        for cp in first + passed:
            cp.wait_send()
        for cp in mine:
            cp.wait()

    anyspace = pl.BlockSpec(memory_space=pl.ANY)
    return pl.pallas_call(
        body, name=name, in_specs=[anyspace] * n, out_specs=[anyspace] * n,
        out_shape=[jax.ShapeDtypeStruct((N_DEV,) + x.shape, x.dtype) for x in xs],
        scratch_shapes=[pltpu.SemaphoreType.DMA((n, 7)), pltpu.SemaphoreType.DMA((n, 7)), pltpu.SemaphoreType.DMA((n,))],
    )(*xs)


def _exchange_sibling(gs, *, name):
    n = len(gs)

    def body(*refs):
        g_refs, out_refs = refs[:n], refs[n:2 * n]
        send_sems, recv_sems = refs[2 * n:]
        mx, my, mc = lax.axis_index("x"), lax.axis_index("y"), lax.axis_index("c")
        sibling = (mx, my, 1 - mc)
        copies = []
        for a in range(n):
            for px in range(2):
                for py in range(2):
                    kk = 2 * px + py
                    copies.append(pltpu.make_async_remote_copy(
                        src_ref=g_refs[a].at[4 * px + 2 * py + (1 - mc)], dst_ref=out_refs[a].at[kk],
                        send_sem=send_sems.at[a, kk], recv_sem=recv_sems.at[a, kk], device_id=sibling,
                        device_id_type=MESH_ID))
        for cp in copies:
            cp.start()
        for cp in copies:
            cp.wait_recv()
        for cp in copies:
            cp.wait_send()

    anyspace = pl.BlockSpec(memory_space=pl.ANY)
    return pl.pallas_call(
        body, name=name, in_specs=[anyspace] * n, out_specs=[anyspace] * n,
        out_shape=[jax.ShapeDtypeStruct((4,) + g.shape[1:], g.dtype) for g in gs],
        scratch_shapes=[pltpu.SemaphoreType.DMA((n, 4)), pltpu.SemaphoreType.DMA((n, 4))],
    )(*gs)


def _pair_sum(g, got, *, name):
    _, r, c = g.shape
    tr = _tile(r, ROW_TILE)

    def body(core_ref, a_ref, b_ref, o_ref):
        del core_ref
        o_ref[...] = (a_ref[...].astype(F32) + b_ref[...].astype(F32)).astype(o_ref.dtype)

    grid_spec = pltpu.PrefetchScalarGridSpec(
        num_scalar_prefetch=1, grid=(4, r // tr),
        in_specs=[pl.BlockSpec((1, tr, c), lambda kk, i, core: (2 * kk + core[0], i, 0)),
                  pl.BlockSpec((1, tr, c), lambda kk, i, core: (kk, i, 0))],
        out_specs=pl.BlockSpec((1, tr, c), lambda kk, i, core: (kk, i, 0)))
    return pl.pallas_call(
        body, name=name, grid_spec=grid_spec, out_shape=jax.ShapeDtypeStruct((4, r, c), g.dtype),
        compiler_params=_cparams("parallel", "parallel"),
    )(lax.axis_index("c").astype(jnp.int32).reshape(1), g, got)


def _exchange_chips(parts, *, name):
    n = len(parts)

    def body(*refs):
        p_refs, out_refs = refs[:n], refs[n:2 * n]
        send_sems, recv_sems = refs[2 * n:]
        mx, my, mc = lax.axis_index("x"), lax.axis_index("y"), lax.axis_index("c")
        chips = [(1 - mx, my), (mx, 1 - my), (1 - mx, 1 - my)]
        copies = [pltpu.make_async_remote_copy(
            src_ref=p_refs[a].at[2 * px + py], dst_ref=out_refs[a].at[j], send_sem=send_sems.at[a, j],
            recv_sem=recv_sems.at[a, j], device_id=(px, py, mc), device_id_type=MESH_ID)
            for j, (px, py) in enumerate(chips) for a in range(n)]
        for cp in copies:
            cp.start()
        for cp in copies:
            cp.wait_recv()
        for cp in copies:
            cp.wait_send()

    anyspace = pl.BlockSpec(memory_space=pl.ANY)
    return pl.pallas_call(
        body, name=name, in_specs=[anyspace] * n, out_specs=[anyspace] * n,
        out_shape=[jax.ShapeDtypeStruct((3,) + p.shape[1:], p.dtype) for p in parts],
        scratch_shapes=[pltpu.SemaphoreType.DMA((n, 3)), pltpu.SemaphoreType.DMA((n, 3))],
    )(*parts)


def _final_sum(part, got, *, name):
    _, r, c = part.shape
    tr = _tile(r, ROW_TILE)

    def body(chip_ref, a_ref, b_ref, o_ref):
        del chip_ref
        acc = a_ref[0].astype(F32)
        for j in range(3):
            acc = acc + b_ref[j].astype(F32)
        o_ref[...] = acc

    grid_spec = pltpu.PrefetchScalarGridSpec(
        num_scalar_prefetch=1, grid=(r // tr,),
        in_specs=[pl.BlockSpec((1, tr, c), lambda i, chip: (chip[0], i, 0)),
                  pl.BlockSpec((3, tr, c), lambda i, chip: (0, i, 0))],
        out_specs=pl.BlockSpec((tr, c), lambda i, chip: (i, 0)))
    chip = (2 * lax.axis_index("x") + lax.axis_index("y")).astype(jnp.int32).reshape(1)
    return pl.pallas_call(
        body, name=name, grid_spec=grid_spec, out_shape=jax.ShapeDtypeStruct((r, c), F32),
        compiler_params=_cparams("parallel"),
    )(chip, part, got)


def _sum_devices(x, *, name):
    _, r, c = x.shape

    def body(x_ref, o_ref):
        acc = x_ref[0]
        for j in range(1, N_DEV):
            acc = acc + x_ref[j]
        o_ref[...] = acc

    return pl.pallas_call(body, name=name, out_shape=jax.ShapeDtypeStruct((r, c), F32),
                          compiler_params=_cparams())(x)


BIG = ("w_in", "w_branch", "w_out", "w_ff1", "w_ff2")
BIG_AXIS = {"w_in": 2, "w_branch": 3, "w_out": 1, "w_ff1": 2, "w_ff2": 1}


def _to_global(blocks, axis):
    moved = jnp.moveaxis(blocks, 0, axis)
    shp = moved.shape
    return moved.reshape(shp[:axis] + (shp[axis] * shp[axis + 1],) + shp[axis + 2:])


def _to_blocks(full, axis):
    shp = full.shape
    split = full.reshape(shp[:axis] + (N_DEV, shp[axis] // N_DEV) + shp[axis + 1:])
    return jnp.moveaxis(split, axis, 0)


def _rows(blocks):
    return blocks.reshape(blocks.shape[0], -1, blocks.shape[-1])


def _vec_rows(n):
    return -(-n // 128 // 8) * 8


def _pack_vec(parts):
    flat = jnp.concatenate([p.reshape(-1).astype(F32) for p in parts])
    rows = _vec_rows(flat.shape[0])
    return jnp.pad(flat, (0, rows * 128 - flat.shape[0])).reshape(rows, 128)


def _unpack_vec(flat, shapes):
    lead = flat.shape[:-2]
    flat = flat.reshape(lead + (-1,))
    out, off = [], 0
    for s in shapes:
        n = 1
        for dim in s:
            n *= dim
        out.append(flat[..., off:off + n].reshape(lead + tuple(s)))
        off += n
    return out


def _dot_each(a_list, b_list, ca=1, cb=0):
    return [_dot(a, b, ca, cb) for a, b in zip(a_list, b_list)]


def _dot3_each(a_list, b_list, ca=1, cb=0):
    sa = [_split2(a) for a in a_list]
    sb = [_split2(b) for b in b_list]
    prods = [(_dot(a1, b1, ca, cb), _dot(a1, b2, ca, cb), _dot(a2, b1, ca, cb)) for (a1, a2), (b1, b2) in zip(sa, sb)]
    return [x + (y + z) for x, y, z in prods]


def _split3(x):
    hi = x.astype(BF16)
    rest = x - hi.astype(F32)
    mid = rest.astype(BF16)
    return hi, mid, (rest - mid.astype(F32)).astype(BF16)


def _ones_dot_each(m, x_list, ca=1, cb=0):
    mb = m.astype(BF16)
    parts = [[_dot(mb, p, ca, cb) for p in _split3(x)] for x in x_list]
    return [p[0] + (p[1] + p[2]) for p in parts]


def _dot_ones_each(x_list, m, ca=1, cb=0):
    mb = m.astype(BF16)
    parts = [[_dot(p, mb, ca, cb) for p in _split3(x)] for x in x_list]
    return [p[0] + (p[1] + p[2]) for p in parts]


def _tri_inv_each(a_list):
    c = a_list[0].shape[0]
    ri = lax.broadcasted_iota(jnp.int32, (c, c), 0)
    ci = lax.broadcasted_iota(jnp.int32, (c, c), 1)
    eye = (ri == ci).astype(F32)
    blk = 8
    pws = [-jnp.where(ri // blk == ci // blk, a, 0.0) for a in a_list]
    invs = [eye + b for b in pws]
    for _ in range(2):
        pws = _dot_each(pws, pws)
        invs = [i + u for i, u in zip(invs, _dot_each(invs, pws))]
    while blk < c:
        sel = (ri // (2 * blk) == ci // (2 * blk)) & (ri // blk != ci // blk)
        offs = [jnp.where(sel, a, 0.0) for a in a_list]
        invs = [i - t for i, t in zip(invs, _dot_each(_dot_each(invs, offs), invs))]
        blk *= 2
    resid = [eye - x for x in _dot3_each([eye + a for a in a_list], invs)]
    return [i + t for i, t in zip(invs, _dot_each(invs, resid))]


def _gdn_chunks(qs, ks, vs, abs_, head, ea, dtb):
    c = qs[0].shape[0]
    lane = lax.broadcasted_iota(jnp.int32, abs_[0].shape, 1)
    a_s = [jnp.sum(jnp.where(lane == head, ab, 0.0), axis=1, keepdims=True) for ab in abs_]
    b_s = [jnp.sum(jnp.where(lane == GDN_HEADS + head, ab, 0.0), axis=1, keepdims=True) for ab in abs_]
    ri = lax.broadcasted_iota(jnp.int32, (c, c), 0)
    ci = lax.broadcasted_iota(jnp.int32, (c, c), 1)
    tri, strict = ri >= ci, ri > ci
    ltri = tri.astype(F32)
    beta = [_sigmoid(b) for b in b_s]
    sig_a = [_sigmoid(a + dtb) for a in a_s]
    g = [-ea * _softplus(a + dtb) for a in a_s]
    g_cc = [jnp.broadcast_to(x, (c, c)) for x in g]
    gi = _ones_dot_each(ltri, g_cc)
    gj = _dot_ones_each(g_cc, (ri <= ci).astype(F32), 0, 0)
    decay = [jnp.exp(jnp.where(tri, x - y, -1e30)) for x, y in zip(gi, gj)]
    gc = _ones_dot_each(ltri, [jnp.broadcast_to(x, (c, GDN_DIM)) for x in g])
    g_tot = [jnp.sum(x, axis=0, keepdims=True) for x in g]
    egc = [jnp.exp(x) for x in gc]
    ekd = [jnp.exp(t - x) for t, x in zip(g_tot, gc)]
    kb = [k * b for k, b in zip(ks, beta)]
    vb = [v * b for v, b in zip(vs, beta)]
    kbg = [x * e for x, e in zip(kb, egc)]
    mkk = _dot_each(kb, ks, 1, 1)
    a_kk = [jnp.where(strict, m * d, 0.0) for m, d in zip(mkk, decay)]
    tinv = _tri_inv_each(a_kk)
    u = _dot_each(tinv, vb)
    w = _dot_each(tinv, kbg)
    mqk = _dot_each(qs, ks, 1, 1)
    a_qk = [jnp.where(tri, m * d, 0.0) for m, d in zip(mqk, decay)]
    qd = [q * e for q, e in zip(qs, egc)]
    kd = [k * e for k, e in zip(ks, ekd)]
    return dict(beta=beta, sig_a=sig_a, g=g, decay=decay, egc=egc, ekd=ekd, g_tot=g_tot, kb=kb, vb=vb, kbg=kbg,
                a_kk=a_kk, tinv=tinv, u=u, w=w, a_qk=a_qk, qd=qd, kd=kd, tri=tri, strict=strict)


def _chunks_in_step(n):
    for cps in (8, 4, 2):
        if n % cps == 0:
            return cps
    return 1


def _gdn_local_fwd_staged(qkv, ab, a_log, dt_bias, *, name):
    t = qkv.shape[1]
    n = t // CHUNK
    cps = _chunks_in_step(n)
    rows = cps * CHUNK
    sls = [slice(j * CHUNK, (j + 1) * CHUNK) for j in range(cps)]

    def body(q_ref, k_ref, v_ref, ab_ref, al_ref, dt_ref, u_ref, w_ref, qd_ref, kd_ref, aqk_ref, gl_ref):
        head = pl.program_id(0)
        ea = jnp.exp(_scalar_row(al_ref, head))
        dtb = _scalar_row(dt_ref, head)
        r = _gdn_chunks([q_ref[0, sl, :] for sl in sls], [k_ref[0, sl, :] for sl in sls],
                        [v_ref[0, sl, :] for sl in sls], [ab_ref[sl, :] for sl in sls], head, ea, dtb)
        for j, sl in enumerate(sls):
            u_ref[0, sl, :] = r["u"][j]
            w_ref[0, sl, :] = r["w"][j]
            qd_ref[0, sl, :] = r["qd"][j]
            kd_ref[0, sl, :] = r["kd"][j]
            aqk_ref[0, sl, :] = r["a_qk"][j]
            gl_ref[0, j] = jnp.exp(jnp.broadcast_to(r["g_tot"][j], (1, GDN_DIM)))

    big = pl.BlockSpec((1, rows, GDN_DIM), lambda h, i: (h, i, 0))
    big_shape = jax.ShapeDtypeStruct((GDN_HEADS, t, GDN_DIM), F32)
    return pl.pallas_call(
        body, name=name, grid=(GDN_HEADS, n // cps), in_specs=_gdn_local_specs(t, cps),
        out_specs=[big] * 4 + [pl.BlockSpec((1, rows, CHUNK), lambda h, i: (h, i, 0)),
                               pl.BlockSpec((1, cps, 1, GDN_DIM), lambda h, i: (h, i, 0, 0))],
        out_shape=[big_shape] * 4 + [jax.ShapeDtypeStruct((GDN_HEADS, t, CHUNK), F32),
                                     jax.ShapeDtypeStruct((GDN_HEADS, n, 1, GDN_DIM), F32)],
        compiler_params=_cparams("parallel", "parallel"),
    )(qkv, qkv, qkv, ab, a_log, dt_bias)


def _gdn_local_bwd_staged(qkv, ab, a_log, dt_bias, du, dw, dqd, dkd, daqk, dgl, *, name):
    t = qkv.shape[1]
    n = t // CHUNK
    cps = _chunks_in_step(n)
    rows = cps * CHUNK
    sls = [slice(j * CHUNK, (j + 1) * CHUNK) for j in range(cps)]

    def body(q_ref, k_ref, v_ref, ab_ref, al_ref, dt_ref, du_ref, dw_ref, dqd_ref, dkd_ref, daqk_ref, dgl_ref,
             dq_ref, dk_ref, dv_ref, dab_ref, dsc_ref):
        head = pl.program_id(0)
        ea = jnp.exp(_scalar_row(al_ref, head))
        dtb = _scalar_row(dt_ref, head)
        lane = lax.broadcasted_iota(jnp.int32, (CHUNK, AB_W), 1)
        lane1 = lax.broadcasted_iota(jnp.int32, (1, GDN_DIM), 1)
        ri = lax.broadcasted_iota(jnp.int32, (CHUNK, CHUNK), 0)
        ci = lax.broadcasted_iota(jnp.int32, (CHUNK, CHUNK), 1)
        utri = (ri <= ci).astype(F32)
        ones = jnp.ones((CHUNK, GDN_DIM), F32)
        qs, ks, vs = ([ref[0, sl, :] for sl in sls] for ref in (q_ref, k_ref, v_ref))
        r = _gdn_chunks(qs, ks, vs, [ab_ref[sl, :] for sl in sls], head, ea, dtb)
        duv, dwv, dqdv, dkdv = ([ref[0, sl, :] for sl in sls] for ref in (du_ref, dw_ref, dqd_ref, dkd_ref))
        d_aqk = [jnp.where(r["tri"], daqk_ref[0, sl, :], 0.0) for sl in sls]
        dvb = _dot_each(r["tinv"], duv, 0, 0)
        dkbg = _dot_each(r["tinv"], dwv, 0, 0)
        outer = [x + y for x, y in zip(_dot_each(dvb, r["u"], 1, 1), _dot_each(dkbg, r["w"], 1, 1))]
        d_akk = [-jnp.where(r["strict"], x, 0.0) for x in outer]
        e = [x * a + y * b for x, a, y, b in zip(d_akk, r["a_kk"], d_aqk, r["a_qk"])]
        dmkk = [x * d for x, d in zip(d_akk, r["decay"])]
        dmqk = [x * d for x, d in zip(d_aqk, r["decay"])]
        dkb = [x + y * eg for x, y, eg in zip(_dot_each(dmkk, ks), dkbg, r["egc"])]
        dk = [a + b + x * ek + y * bt for a, b, x, ek, y, bt in zip(
            _dot_each(dmkk, r["kb"], 0, 0), _dot_each(dmqk, qs, 0, 0), dkdv, r["ekd"], dkb, r["beta"])]
        dq = [a + x * eg for a, x, eg in zip(_dot_each(dmqk, ks), dqdv, r["egc"])]
        col_sums = _dot_ones_each(e, ones, 0, 0)
        acc_alog = jnp.zeros((1, 1), F32)
        acc_dtb = jnp.zeros((1, 1), F32)
        dgc_lanes, d_tots, dbetas = [], [], []
        for j in range(cps):
            dbetas.append(jnp.sum(dkb[j] * ks[j] + dvb[j] * vs[j], axis=1, keepdims=True))
            kd_term = jnp.sum(dkdv[j] * r["kd"][j], axis=1, keepdims=True)
            dgc = (jnp.sum(e[j], axis=1, keepdims=True)
                   + jnp.sum(dqdv[j] * r["qd"][j] + dkbg[j] * r["kbg"][j], axis=1, keepdims=True) - kd_term)
            dgc_lanes.append(jnp.broadcast_to(dgc, (CHUNK, GDN_DIM)) - col_sums[j])
            dgl_tot = jnp.sum(dgl_ref[0, j], axis=1, keepdims=True) * jnp.exp(r["g_tot"][j])
            d_tots.append(jnp.sum(kd_term, axis=0, keepdims=True) + dgl_tot)
        suffix_sums = _ones_dot_each(utri, dgc_lanes)
        for j, sl in enumerate(sls):
            dq_ref[0, sl, :] = dq[j]
            dk_ref[0, sl, :] = dk[j]
            dv_ref[0, sl, :] = dvb[j] * r["beta"][j]
            dg = jnp.sum(jnp.where(lane == 0, suffix_sums[j] + d_tots[j], 0.0), axis=1, keepdims=True)
            da = dg * (-ea) * r["sig_a"][j]
            db = dbetas[j] * r["beta"][j] * (1.0 - r["beta"][j])
            dab_ref[0, sl, :] = jnp.where(lane == head, da, 0.0) + jnp.where(lane == GDN_HEADS + head, db, 0.0)
            acc_alog = acc_alog + jnp.sum(dg * r["g"][j], axis=0, keepdims=True)
            acc_dtb = acc_dtb + jnp.sum(da, axis=0, keepdims=True)
        dsc_ref[0, 0] = jnp.where(lane1 == 0, acc_alog, 0.0) + jnp.where(lane1 == 1, acc_dtb, 0.0)

    big = pl.BlockSpec((1, rows, GDN_DIM), lambda h, i: (h, i, 0))
    big_shape = jax.ShapeDtypeStruct((GDN_HEADS, t, GDN_DIM), F32)
    return pl.pallas_call(
        body, name=name, grid=(GDN_HEADS, n // cps),
        in_specs=_gdn_local_specs(t, cps) + [big] * 4 + [pl.BlockSpec((1, rows, CHUNK), lambda h, i: (h, i, 0)),
                                                        pl.BlockSpec((1, cps, 1, GDN_DIM), lambda h, i: (h, i, 0, 0))],
        out_specs=[big] * 4 + [pl.BlockSpec((1, 1, 1, GDN_DIM), lambda h, i: (h, i, 0, 0))],
        out_shape=[big_shape] * 4 + [jax.ShapeDtypeStruct((GDN_HEADS, n // cps, 1, GDN_DIM), F32)],
        compiler_params=_cparams("parallel", "parallel"),
    )(qkv, qkv, qkv, ab, a_log, dt_bias, du, dw, dqd, dkd, daqk, dgl)


def _gdn_scan_fwd_staged(u, w, qd, kd, aqk, gl, *, name):
    h, t, _ = u.shape
    n = t // CHUNK
    hs = range(h)

    def body(u_ref, w_ref, qd_ref, kd_ref, aqk_ref, gl_ref, o_ref, s_ref, state):
        @pl.when(pl.program_id(0) == 0)
        def _():
            state[...] = jnp.zeros_like(state)

        ss = [state[hh] for hh in hs]
        for hh in hs:
            s_ref[hh, 0] = ss[hh]
        vn = [u_ref[hh] - x for hh, x in zip(hs, _dot_each([w_ref[hh] for hh in hs], ss))]
        from_state = _dot_each([qd_ref[hh] for hh in hs], ss)
        from_chunk = _dot_each([aqk_ref[hh] for hh in hs], vn)
        writes = _dot_each([kd_ref[hh] for hh in hs], vn, 0, 0)
        for hh in hs:
            o_ref[hh] = from_state[hh] + from_chunk[hh]
            state[hh] = ss[hh] * gl_ref[hh, 0] + writes[hh]

    big = pl.BlockSpec((h, CHUNK, GDN_DIM), lambda i: (0, i, 0))
    return pl.pallas_call(
        body, name=name, grid=(n,),
        in_specs=[big] * 4 + [pl.BlockSpec((h, CHUNK, CHUNK), lambda i: (0, i, 0)),
                              pl.BlockSpec((h, 1, 1, GDN_DIM), lambda i: (0, i, 0, 0))],
        out_specs=[big, pl.BlockSpec((h, 1, GDN_DIM, GDN_DIM), lambda i: (0, i, 0, 0))],
        out_shape=[jax.ShapeDtypeStruct((h, t, GDN_DIM), F32), jax.ShapeDtypeStruct((h, n, GDN_DIM, GDN_DIM), F32)],
        scratch_shapes=[pltpu.VMEM((h, GDN_DIM, GDN_DIM), F32)],
        compiler_params=_cparams("arbitrary"),
    )(u, w, qd, kd, aqk, gl)


def _gdn_scan_bwd_staged(u, w, qd, kd, aqk, gl, states, do, *, name):
    h, t, _ = u.shape
    n = t // CHUNK
    hs = range(h)

    def body(u_ref, w_ref, qd_ref, kd_ref, aqk_ref, gl_ref, s_ref, do_ref,
             du_ref, dw_ref, dqd_ref, dkd_ref, daqk_ref, dgl_ref, dstate):
        @pl.when(pl.program_id(0) == 0)
        def _():
            dstate[...] = jnp.zeros_like(dstate)

        ri = lax.broadcasted_iota(jnp.int32, (CHUNK, CHUNK), 0)
        ci = lax.broadcasted_iota(jnp.int32, (CHUNK, CHUNK), 1)
        ss = [s_ref[hh, 0] for hh in hs]
        dsn = [dstate[hh] for hh in hs]
        dov = [do_ref[hh] for hh in hs]
        wv = [w_ref[hh] for hh in hs]
        vn = [u_ref[hh] - x for hh, x in zip(hs, _dot_each(wv, ss))]
        dvn = [x + y for x, y in zip(_dot_each([aqk_ref[hh] for hh in hs], dov, 0, 0),
                                     _dot_each([kd_ref[hh] for hh in hs], dsn))]
        dws = _dot_each(dvn, ss, 1, 1)
        dqds = _dot_each(dov, ss, 1, 1)
        dkds = _dot_each(vn, dsn, 1, 1)
        daqks = _dot_each(dov, vn, 1, 1)
        reads = _dot_each([qd_ref[hh] for hh in hs], dov, 0, 0)
        solves = _dot_each(wv, dvn, 0, 0)
        for hh in hs:
            du_ref[hh] = dvn[hh]
            dw_ref[hh] = -dws[hh]
            dqd_ref[hh] = dqds[hh]
            dkd_ref[hh] = dkds[hh]
            daqk_ref[hh] = jnp.where(ri >= ci, daqks[hh], 0.0)
            dgl_ref[hh, 0] = jnp.sum(dsn[hh] * ss[hh], axis=0, keepdims=True)
            dstate[hh] = reads[hh] + dsn[hh] * gl_ref[hh, 0] - solves[hh]

    big = pl.BlockSpec((h, CHUNK, GDN_DIM), lambda i: (0, n - 1 - i, 0))
    sq = pl.BlockSpec((h, CHUNK, CHUNK), lambda i: (0, n - 1 - i, 0))
    glb = pl.BlockSpec((h, 1, 1, GDN_DIM), lambda i: (0, n - 1 - i, 0, 0))
    big_shape = jax.ShapeDtypeStruct((h, t, GDN_DIM), F32)
    return pl.pallas_call(
        body, name=name, grid=(n,),
        in_specs=[big] * 4 + [sq, glb, pl.BlockSpec((h, 1, GDN_DIM, GDN_DIM), lambda i: (0, n - 1 - i, 0, 0)), big],
        out_specs=[big] * 4 + [sq, glb],
        out_shape=[big_shape] * 4 + [jax.ShapeDtypeStruct((h, t, CHUNK), F32),
                                     jax.ShapeDtypeStruct((h, n, 1, GDN_DIM), F32)],
        scratch_shapes=[pltpu.VMEM((h, GDN_DIM, GDN_DIM), F32)],
        compiler_params=_cparams("arbitrary"),
    )(u, w, qd, kd, aqk, gl, states, do)


SCAN_CHUNKS = 4


def _scan_chunks(n):
    return SCAN_CHUNKS if n % SCAN_CHUNKS == 0 else 1


def _gdn_scan_fwd_multi(u, w, qd, kd, aqk, gl, *, name):
    h, t, _ = u.shape
    n = t // CHUNK
    cps = _scan_chunks(n)
    rows = cps * CHUNK
    hs = range(h)

    def body(u_ref, w_ref, qd_ref, kd_ref, aqk_ref, gl_ref, o_ref, s_ref, state):
        @pl.when(pl.program_id(0) == 0)
        def _():
            state[...] = jnp.zeros_like(state)

        ss = [state[hh] for hh in hs]
        for j in range(cps):
            sl = slice(j * CHUNK, (j + 1) * CHUNK)
            for hh in hs:
                s_ref[hh, j] = ss[hh]
            vn = [u_ref[hh, sl, :] - x for hh, x in zip(hs, _dot_each([w_ref[hh, sl, :] for hh in hs], ss))]
            from_state = _dot_each([qd_ref[hh, sl, :] for hh in hs], ss)
            from_chunk = _dot_each([aqk_ref[hh, sl, :] for hh in hs], vn)
            writes = _dot_each([kd_ref[hh, sl, :] for hh in hs], vn, 0, 0)
            for hh in hs:
                o_ref[hh, sl, :] = from_state[hh] + from_chunk[hh]
            ss = [ss[hh] * gl_ref[hh, j] + writes[hh] for hh in hs]
        for hh in hs:
            state[hh] = ss[hh]

    big = pl.BlockSpec((h, rows, GDN_DIM), lambda i: (0, i, 0))
    return pl.pallas_call(
        body, name=name, grid=(n // cps,),
        in_specs=[big] * 4 + [pl.BlockSpec((h, rows, CHUNK), lambda i: (0, i, 0)),
                              pl.BlockSpec((h, cps, 1, GDN_DIM), lambda i: (0, i, 0, 0))],
        out_specs=[big, pl.BlockSpec((h, cps, GDN_DIM, GDN_DIM), lambda i: (0, i, 0, 0))],
        out_shape=[jax.ShapeDtypeStruct((h, t, GDN_DIM), F32), jax.ShapeDtypeStruct((h, n, GDN_DIM, GDN_DIM), F32)],
        scratch_shapes=[pltpu.VMEM((h, GDN_DIM, GDN_DIM), F32)],
        compiler_params=_cparams("arbitrary"),
    )(u, w, qd, kd, aqk, gl)


def _gdn_scan_bwd_multi(u, w, qd, kd, aqk, gl, states, do, *, name):
    h, t, _ = u.shape
    n = t // CHUNK
    cps = _scan_chunks(n)
    rows = cps * CHUNK
    steps = n // cps
    hs = range(h)

    def body(u_ref, w_ref, qd_ref, kd_ref, aqk_ref, gl_ref, s_ref, do_ref,
             du_ref, dw_ref, dqd_ref, dkd_ref, daqk_ref, dgl_ref, dstate):
        @pl.when(pl.program_id(0) == 0)
        def _():
            dstate[...] = jnp.zeros_like(dstate)

        ri = lax.broadcasted_iota(jnp.int32, (CHUNK, CHUNK), 0)
        ci = lax.broadcasted_iota(jnp.int32, (CHUNK, CHUNK), 1)
        dsn = [dstate[hh] for hh in hs]
        for j in reversed(range(cps)):
            sl = slice(j * CHUNK, (j + 1) * CHUNK)
            ss = [s_ref[hh, j] for hh in hs]
            dov = [do_ref[hh, sl, :] for hh in hs]
            wv = [w_ref[hh, sl, :] for hh in hs]
            vn = [u_ref[hh, sl, :] - x for hh, x in zip(hs, _dot_each(wv, ss))]
            dvn = [x + y for x, y in zip(_dot_each([aqk_ref[hh, sl, :] for hh in hs], dov, 0, 0),
                                         _dot_each([kd_ref[hh, sl, :] for hh in hs], dsn))]
            dws = _dot_each(dvn, ss, 1, 1)
            dqds = _dot_each(dov, ss, 1, 1)
            dkds = _dot_each(vn, dsn, 1, 1)
            daqks = _dot_each(dov, vn, 1, 1)
            reads = _dot_each([qd_ref[hh, sl, :] for hh in hs], dov, 0, 0)
            solves = _dot_each(wv, dvn, 0, 0)
            for hh in hs:
                du_ref[hh, sl, :] = dvn[hh]
                dw_ref[hh, sl, :] = -dws[hh]
                dqd_ref[hh, sl, :] = dqds[hh]
                dkd_ref[hh, sl, :] = dkds[hh]
                daqk_ref[hh, sl, :] = jnp.where(ri >= ci, daqks[hh], 0.0)
                dgl_ref[hh, j] = jnp.sum(dsn[hh] * ss[hh], axis=0, keepdims=True)
            dsn = [reads[hh] + dsn[hh] * gl_ref[hh, j] - solves[hh] for hh in hs]
        for hh in hs:
            dstate[hh] = dsn[hh]

    big = pl.BlockSpec((h, rows, GDN_DIM), lambda i: (0, steps - 1 - i, 0))
    sq = pl.BlockSpec((h, rows, CHUNK), lambda i: (0, steps - 1 - i, 0))
    glb = pl.BlockSpec((h, cps, 1, GDN_DIM), lambda i: (0, steps - 1 - i, 0, 0))
    big_shape = jax.ShapeDtypeStruct((h, t, GDN_DIM), F32)
    return pl.pallas_call(
        body, name=name, grid=(steps,),
        in_specs=[big] * 4 + [sq, glb, pl.BlockSpec((h, cps, GDN_DIM, GDN_DIM), lambda i: (0, steps - 1 - i, 0, 0)),
                              big],
        out_specs=[big] * 4 + [sq, glb],
        out_shape=[big_shape] * 4 + [jax.ShapeDtypeStruct((h, t, CHUNK), F32),
                                     jax.ShapeDtypeStruct((h, n, 1, GDN_DIM), F32)],
        scratch_shapes=[pltpu.VMEM((h, GDN_DIM, GDN_DIM), F32)],
        compiler_params=_cparams("arbitrary"),
    )(u, w, qd, kd, aqk, gl, states, do)


def _gather_phases(x_refs, out_refs, send_sems, recv_sems, local_sems):
    n = len(x_refs)
    mx, my, mc = lax.axis_index("x"), lax.axis_index("y"), lax.axis_index("c")
    me, sibling = (mx, my, mc), (mx, my, 1 - mc)
    chips = [(1 - mx, my), (mx, 1 - my), (1 - mx, 1 - my)]

    def slot(a, px, py, pc):
        return out_refs[a].at[4 * px + 2 * py + pc]

    def copy(a, k, block, to, src=None):
        return pltpu.make_async_remote_copy(
            src_ref=slot(a, *block) if src is None else src, dst_ref=slot(a, *block),
            send_sem=send_sems.at[a, k], recv_sem=recv_sems.at[a, k], device_id=to, device_id_type=MESH_ID)

    def mine():
        return [pltpu.make_async_copy(x_refs[a], slot(a, *me), local_sems.at[a]) for a in range(n)]

    def first():
        out = [copy(a, 1 + j, me, (*chip, mc), src=x_refs[a]) for j, chip in enumerate(chips) for a in range(n)]
        return out + [copy(a, 0, me, sibling, src=x_refs[a]) for a in range(n)]

    def passed():
        return [copy(a, 4 + j, (*chip, mc), sibling) for j, chip in enumerate(chips) for a in range(n)]

    def start():
        for cp in mine() + first():
            cp.start()

    def pass_on():
        for j, chip in enumerate(chips):
            for a in range(n):
                copy(a, 1 + j, (*chip, mc), me).wait_recv()
                copy(a, 4 + j, (*chip, mc), sibling).start()

    def finish():
        for a in range(n):
            copy(a, 0, sibling, me).wait_recv()
        for j, chip in enumerate(chips):
            for a in range(n):
                copy(a, 4 + j, (*chip, 1 - mc), me).wait_recv()
        for cp in first() + passed():
            cp.wait_send()
        for cp in mine():
            cp.wait()

    return start, pass_on, finish


def _gather_extras(shards):
    n = len(shards)
    anyspace = pl.BlockSpec(memory_space=pl.ANY)
    return ([anyspace] * n, [anyspace] * n, [jax.ShapeDtypeStruct((N_DEV,) + x.shape, x.dtype) for x in shards],
            [pltpu.SemaphoreType.DMA((n, 7)), pltpu.SemaphoreType.DMA((n, 7)), pltpu.SemaphoreType.DMA((n,))])


def _scatter_phases(g_refs, out_refs, send_sems, recv_sems, local_sems):
    n = len(g_refs)
    mx, my, mc = lax.axis_index("x"), lax.axis_index("y"), lax.axis_index("c")
    me_id = 4 * mx + 2 * my + mc

    def peer(r):
        return (1 - mx if r & 4 else mx, 1 - my if r & 2 else my, 1 - mc if r & 1 else mc)

    def peer_id(r):
        px, py, pc = peer(r)
        return 4 * px + 2 * py + pc

    def copies():
        return [pltpu.make_async_remote_copy(
            src_ref=g_refs[a].at[peer_id(r)], dst_ref=out_refs[a].at[me_id], send_sem=send_sems.at[a, r - 1],
            recv_sem=recv_sems.at[a, r - 1], device_id=peer(r), device_id_type=MESH_ID)
            for r in range(1, N_DEV) for a in range(n)]

    def arrivals():
        return [pltpu.make_async_remote_copy(
            src_ref=g_refs[a].at[peer_id(r)], dst_ref=out_refs[a].at[peer_id(r)], send_sem=send_sems.at[a, r - 1],
            recv_sem=recv_sems.at[a, r - 1], device_id=peer(r), device_id_type=MESH_ID)
            for r in range(1, N_DEV) for a in range(n)]

    def mine():
        return [pltpu.make_async_copy(g_refs[a].at[me_id], out_refs[a].at[me_id], local_sems.at[a]) for a in range(n)]

    def start():
        for cp in mine() + copies():
            cp.start()

    def finish():
        for cp in arrivals():
            cp.wait_recv()
        for cp in copies():
            cp.wait_send()
        for cp in mine():
            cp.wait()

    return start, finish


def _scatter_extras(blocks):
    n = len(blocks)
    anyspace = pl.BlockSpec(memory_space=pl.ANY)
    return ([anyspace] * n, [anyspace] * n, [jax.ShapeDtypeStruct(b.shape, b.dtype) for b in blocks],
            [pltpu.SemaphoreType.DMA((n, 7)), pltpu.SemaphoreType.DMA((n, 7)), pltpu.SemaphoreType.DMA((n,))])


def _sum_slots(x, *, name):
    _, r, c = x.shape
    tr = _tile(r, ROW_TILE)

    def body(x_ref, o_ref):
        acc = x_ref[0].astype(F32)
        for s in range(1, N_DEV):
            acc = acc + x_ref[s].astype(F32)
        o_ref[...] = acc

    return pl.pallas_call(
        body, name=name, grid=(r // tr,), in_specs=[pl.BlockSpec((N_DEV, tr, c), lambda i: (0, i, 0))],
        out_specs=pl.BlockSpec((tr, c), lambda i: (i, 0)), out_shape=jax.ShapeDtypeStruct((r, c), F32),
        compiler_params=_cparams("parallel"),
    )(x)


SB_PAIRS = SB_HEADS // 2
SB_PAIR_QBLOCK = 512
SB_PAIR_QBLOCK_FWD = 512


def _sb_pair_blocks(t, pref=SB_PAIR_QBLOCK):
    bq = _tile(t, pref)
    return bq, _tile(bq, SB_KBLOCK)


def _sb_pair_specs(t, bq):
    base = C_SBQKV // 128
    return [pl.BlockSpec((bq, 128), lambda p, i: (i, base + p)),
            pl.BlockSpec((t, 128), lambda p, i: (0, base + SB_PAIRS + p)),
            pl.BlockSpec((t, 128), lambda p, i: (0, base + 2 * SB_PAIRS + p))]


def _halves(x, first):
    zero = jnp.zeros_like(x)
    return [jnp.where(first, x, zero), jnp.where(first, zero, x)]


def _sb_mask(qi, kb, bq, bk):
    t_idx = qi * bq + lax.broadcasted_iota(jnp.int32, (bq, bk), 0)
    s_idx = kb * bk + lax.broadcasted_iota(jnp.int32, (bq, bk), 1)
    return s_idx < t_idx


SB_SCALE = SB_DIM ** -0.5


def _sb_pair_scores(qh, kblk, mask):
    zs = _dot_each(qh, [kblk, kblk], 1, 1)
    es = [jnp.exp(-jnp.abs(z)) for z in zs]
    sps = [jnp.maximum(z, 0.0) + jnp.log(1.0 + e) for z, e in zip(zs, es)]
    if mask is not None:
        sps = [jnp.where(mask, sp, 0.0) for sp in sps]
    return zs, es, sps


def _sb_atts(zs, csums, laters, mask):
    atts = [jnp.exp(z - c - l) for z, c, l in zip(zs, csums, laters)]
    return atts if mask is None else [jnp.where(mask, a, 0.0) for a in atts]


def _scaled_queries(q_ref, first):
    return _halves(q_ref[...] * jnp.asarray(SB_SCALE, q_ref.dtype), first)


def _running_sums(x_list, m):
    return [_dot(x, m) for x in x_list]


def _sb_pair_fwd(proj, shards=(), *, name):
    t = proj.shape[0]
    bq, bk = _sb_pair_blocks(t, SB_PAIR_QBLOCK_FWD)
    n = len(shards)
    nq = t // bq
    nsteps = SB_PAIRS * nq

    def body(q_ref, k_ref, v_ref, *rest):
        o_ref = rest[n]
        qi = pl.program_id(1)
        if n:
            step_no = pl.program_id(0) * nq + qi
            start, pass_on, finish = _gather_phases(rest[:n], rest[n + 1:2 * n + 1], *rest[2 * n + 1:])
            pl.when(step_no == 0)(start)
            pl.when(step_no == (2 * nsteps) // 3)(pass_on)
        first = lax.broadcasted_iota(jnp.int32, (1, 128), 1) < SB_DIM
        qh = _scaled_queries(q_ref, first)
        suffix = _suffix_ones(bk)
        band = bq // bk
        nkb = (qi + 1) * band

        def make_step(masked):
            def step(it, carry):
                later0, later1, acc = carry
                kb = nkb - 1 - it
                rows = pl.ds(pl.multiple_of(kb * bk, bk), bk)
                mask = _sb_mask(qi, kb, bq, bk) if masked else None
                zs, _, sps = _sb_pair_scores(qh, k_ref[rows, :], mask)
                atts = _sb_atts(zs, _running_sums(sps, suffix), (later0, later1), mask)
                outs = _dot_each(atts, _halves(v_ref[rows, :], first))
                return (later0 + jnp.sum(sps[0], axis=1, keepdims=True),
                        later1 + jnp.sum(sps[1], axis=1, keepdims=True), acc + (outs[0] + outs[1]))
            return step

        zero = jnp.zeros((bq, 1), F32)
        carry = lax.fori_loop(0, band, make_step(True), (zero, zero, jnp.zeros((bq, 128), F32)))
        _, _, acc = lax.fori_loop(band, nkb, make_step(False), carry)
        o_ref[...] = acc.astype(o_ref.dtype)
        if n:
            pl.when(step_no == nsteps - 1)(finish)

    more_in, more_out, more_shapes, sems = _gather_extras(shards) if n else ([], [], [], [])
    res = pl.pallas_call(
        body, name=name, grid=(SB_PAIRS, nq), in_specs=_sb_pair_specs(t, bq) + more_in,
        out_specs=[pl.BlockSpec((bq, 128), lambda p, i: (i, p))] + more_out,
        out_shape=[jax.ShapeDtypeStruct((t, SB_W), BF16)] + more_shapes, scratch_shapes=sems,
        compiler_params=_cparams("arbitrary", "arbitrary"),
    )(proj, proj, proj, *shards)
    return res[0], list(res[1:])


def _sb_pair_bwd(proj, dy, blocks=(), *, name):
    t = proj.shape[0]
    bq, bk = _sb_pair_blocks(t)
    nq = t // bq
    n = len(blocks)

    def body(q_ref, k_ref, v_ref, do_ref, *rest):
        dq_ref, dk_ref, dv_ref = rest[n:n + 3]
        dl_keep, sig_keep, dk_acc, dv_acc = rest[2 * n + 3:2 * n + 7]
        qi = pl.program_id(1)
        if n:
            step_no = pl.program_id(0) * nq + qi
            start, finish = _scatter_phases(rest[:n], rest[n + 3:2 * n + 3], *rest[2 * n + 7:])
            pl.when(step_no == 0)(start)

        @pl.when(qi == 0)
        def _():
            dk_acc[...] = jnp.zeros_like(dk_acc)
            dv_acc[...] = jnp.zeros_like(dv_acc)

        first = lax.broadcasted_iota(jnp.int32, (1, 128), 1) < SB_DIM
        qh = _scaled_queries(q_ref, first)
        doh = _halves(do_ref[...], first)
        suffix = _suffix_ones(bk)
        prefix = _prefix_ones(bk)
        band = bq // bk
        nkb = (qi + 1) * band

        def make_back(masked):
            def back(it, carry):
                kb = nkb - 1 - it
                rows = pl.ds(pl.multiple_of(kb * bk, bk), bk)
                vblk = v_ref[rows, :]
                mask = _sb_mask(qi, kb, bq, bk) if masked else None
                zs, es, sps = _sb_pair_scores(qh, k_ref[rows, :], mask)
                atts = _sb_atts(zs, _running_sums(sps, suffix), carry, mask)
                dvs = _dot_each(atts, doh, 0, 0)
                datts = _dot_each(doh, [vblk, vblk], 1, 1)
                dv_acc[rows, :] += dvs[0] + dvs[1]
                for hh in range(2):
                    sig = jnp.where(zs[hh] >= 0, 1.0, es[hh]) * pl.reciprocal(1.0 + es[hh], approx=True)
                    dl_keep[hh, kb] = (atts[hh] * datts[hh]).astype(dl_keep.dtype)
                    sig_keep[hh, kb] = (sig if mask is None else jnp.where(mask, sig, 0.0)).astype(sig_keep.dtype)
                return tuple(l + jnp.sum(sp, axis=1, keepdims=True) for l, sp in zip(carry, sps))
            return back

        zero = jnp.zeros((bq, 1), F32)
        lax.fori_loop(band, nkb, make_back(False), lax.fori_loop(0, band, make_back(True), (zero, zero)))

        def forth(kb, carry):
            before0, before1, dq = carry
            rows = pl.ds(pl.multiple_of(kb * bk, bk), bk)
            kept = [dl_keep[hh, kb] for hh in range(2)]
            sums = _running_sums(kept, prefix)
            dls = [x.astype(F32) for x in kept]
            dzs = [dl - sig_keep[hh, kb].astype(F32) * (b + s)
                   for hh, (dl, b, s) in enumerate(zip(dls, (before0, before1), sums))]
            dks = _dot_each(dzs, qh, 0, 0)
            dqs = _dot_each(dzs, _halves(k_ref[rows, :], first))
            dk_acc[rows, :] += dks[0] + dks[1]
            return (before0 + jnp.sum(dls[0], axis=1, keepdims=True), before1 + jnp.sum(dls[1], axis=1, keepdims=True),
                    dq + (dqs[0] + dqs[1]))

        _, _, dq = lax.fori_loop(0, nkb, forth, (zero, zero, jnp.zeros((bq, 128), F32)))
        dq_ref[...] = (dq * SB_SCALE).astype(dq_ref.dtype)

        @pl.when(qi == nq - 1)
        def _():
            dk_ref[...] = dk_acc[...].astype(dk_ref.dtype)
            dv_ref[...] = dv_acc[...].astype(dv_ref.dtype)

        if n:
            pl.when(step_no == SB_PAIRS * nq - 1)(finish)

    qspec = pl.BlockSpec((bq, 128), lambda p, i: (i, p))
    kvspec = pl.BlockSpec((t, 128), lambda p, i: (0, p))
    shape = jax.ShapeDtypeStruct((t, SB_W), BF16)
    more_in, more_out, more_shapes, sems = _scatter_extras(blocks) if n else ([], [], [], [])
    res = pl.pallas_call(
        body, name=name, grid=(SB_PAIRS, nq), in_specs=_sb_pair_specs(t, bq) + [qspec] + more_in,
        out_specs=[qspec, kvspec, kvspec] + more_out, out_shape=[shape] * 3 + more_shapes,
        scratch_shapes=[pltpu.VMEM((2, t // bk, bq, bk), BF16), pltpu.VMEM((2, t // bk, bq, bk), BF16),
                        pltpu.VMEM((t, 128), F32), pltpu.VMEM((t, 128), F32)] + sems,
        compiler_params=_cparams("arbitrary", "arbitrary"),
    )(proj, proj, proj, dy, *blocks)
    return res[0], res[1], res[2], list(res[3:])


def _heads_major(cols, heads, dim):
    t = cols.shape[0]
    return cols.reshape(t, heads, dim).transpose(1, 0, 2)


def _heads_minor(x):
    h, t, dim = x.shape
    return x.transpose(1, 0, 2).reshape(t, h * dim)


def _sb_qkv(proj):
    return [_heads_major(proj[:, C_SBQKV + i * SB_W:C_SBQKV + (i + 1) * SB_W], SB_HEADS, SB_DIM) for i in range(3)]


def _relu2_epilogue(r):
    a = jnp.maximum(r, 0.0)
    return r, a * a


def _add_epilogue(r, other):
    return (r + other,)


def _relu2_bwd_epilogue(r, a):
    return (r * 2.0 * jnp.maximum(a.astype(F32), 0.0),)


def _layer_fwd(x, p, shards):
    h = _norm_fwd(x, p["norm_mix_pre"], out_dtype=BF16, name="norm_pre_fwd")
    proj = _mm(h, p["w_main"], name="mm_in")
    ab = _mm(h, p["w_ab"], out_dtypes=(F32,), name="mm_ab")
    qkv = _gdn_pre_fwd(proj, p["conv_qkv_w"], name="gdn_pre_fwd")
    a_log, dt_bias = p["gdn_a_log"].reshape(1, GDN_HEADS), p["gdn_dt_bias"].reshape(1, GDN_HEADS)
    u, w, qd, kd, aqk, gl = _gdn_local_fwd_staged(qkv, ab, a_log, dt_bias, name="gdn_local_fwd")
    o_gdn, states = _gdn_scan_fwd_multi(u, w, qd, kd, aqk, gl, name="gdn_scan_fwd")
    y_a = _gdn_post_fwd(o_gdn, proj, p["gdn_norm_w"], name="gdn_post_fwd")
    y_b, gathered = _sb_pair_fwd(proj, shards, name="sb_fwd")
    p = dict(p)
    p.update({k: _to_global(blk, BIG_AXIS[k] - 1) for k, blk in zip(BIG[1:], gathered)})
    y_c = _sc_fwd(proj, p["conv_sc_w"], name="sc_fwd")
    ys = (y_a, y_b, y_c)
    ps = tuple(_mm(ys[b], p["w_branch"][b], name="mm_branch") for b in range(3))
    merged = _merge_fwd(ps, proj, name="merge_fwd")
    mo = _mm(merged, p["w_out"], out_dtypes=(F32,), name="mm_out")
    x1 = _norm_fwd(mo, p["norm_mix_post"], x, out_dtype=F32, name="norm_post_fwd")
    h2 = _norm_fwd(x1, p["norm_ffn_pre"], out_dtype=BF16, name="norm_pre_fwd")
    a1, r1 = _mm(h2, p["w_ff1"], out_dtypes=(BF16, BF16), epi=_relu2_epilogue, name="mm_ff1")
    f = _mm(r1, p["w_ff2"], out_dtypes=(F32,), name="mm_ff2")
    x2 = _norm_fwd(f, p["norm_ffn_post"], x1, out_dtype=F32, name="norm_post_fwd")
    saved = dict(x=x, h=h, proj=proj, ab=ab, qkv=qkv, u=u, w=w, qd=qd, kd=kd, aqk=aqk, gl=gl, o_gdn=o_gdn,
                 states=states, ys=ys, ps=ps, merged=merged, mo=mo, x1=x1, h2=h2,
                 a1=a1, r1=r1, f=f)
    return x2, saved, p, gathered[len(BIG) - 1:]


def _layer_bwd(dx2, p, s, above=()):
    g = {}
    df, g["norm_ffn_post"] = _norm_bwd(s["f"], p["norm_ffn_post"], dx2, out_dtype=BF16, name="norm_bwd_b")
    da1 = _mm(df, p["w_ff2"], tb=True, epi=_relu2_bwd_epilogue, extras=(s["a1"],), name="mm_ff2_dx")
    g["w_ff2"] = _mm(s["r1"], df, ta=True, name="mm_ff2_dw")
    g["w_ff1"] = _mm(s["h2"], da1, ta=True, name="mm_ff1_dw")
    dh2 = _mm(da1, p["w_ff1"], tb=True, out_dtypes=(F32,), name="mm_ff1_dx")
    dx1, g["norm_ffn_pre"] = _norm_bwd(s["x1"], p["norm_ffn_pre"], dh2, dx2, out_dtype=F32, name="norm_bwd_f")
    dmo, g["norm_mix_post"] = _norm_bwd(s["mo"], p["norm_mix_post"], dx1, out_dtype=BF16, name="norm_bwd_b")
    dmerged = _mm(dmo, p["w_out"], tb=True, name="mm_out_dx")
    g["w_out"] = _mm(s["merged"], dmo, ta=True, name="mm_out_dw")
    dps, dgates = _merge_bwd(s["ps"], s["proj"], dmerged, name="merge_bwd")
    dys = [_mm(dps[b], p["w_branch"][b], tb=True, name="mm_branch_dx") for b in range(3)]
    g["w_branch"] = jnp.stack([_mm(s["ys"][b], dps[b], ta=True, name="mm_branch_dw") for b in range(3)])
    dscx, dscb, dscc, g["conv_sc_w"] = _sc_bwd(s["proj"], p["conv_sc_w"], dys[2], name="sc_bwd")
    own = [_rows(_to_blocks(g[k], BIG_AXIS[k] - 1)) for k in BIG[1:]]
    dsq, dsk, dsv, received = _sb_pair_bwd(s["proj"], dys[1], own + list(above), name="sb_bwd")
    a_log, dt_bias = p["gdn_a_log"].reshape(1, GDN_HEADS), p["gdn_dt_bias"].reshape(1, GDN_HEADS)
    do_gdn, dggate, g["gdn_norm_w"] = _gdn_post_bwd(s["o_gdn"], s["proj"], p["gdn_norm_w"], dys[0], name="gdn_post_bwd")
    du, dw, dqd, dkd, daqk, dgl = _gdn_scan_bwd_multi(s["u"], s["w"], s["qd"], s["kd"], s["aqk"], s["gl"],
                                                       s["states"], do_gdn, name="gdn_scan_bwd")
    dq, dk, dv, dab_h, dsc = _gdn_local_bwd_staged(s["qkv"], s["ab"], a_log, dt_bias, du, dw, dqd, dkd, daqk, dgl,
                                                   name="gdn_local_bwd")
    dsc = jnp.sum(dsc, axis=(1, 2))
    g["gdn_a_log"], g["gdn_dt_bias"] = dsc[:, 0], dsc[:, 1]
    dqkv = jnp.concatenate([dq, dk, dv], axis=0)
    dgqkv, g["conv_qkv_w"] = _gdn_pre_bwd(s["proj"], p["conv_qkv_w"], dqkv, name="gdn_pre_bwd")
    dab = jnp.sum(dab_h, axis=0).astype(BF16)
    dproj = jnp.concatenate([dgqkv, dggate, dsq, dsk, dsv, dscx, dscb, dscc, dgates], axis=1)
    g["w_main"] = _mm(s["h"], dproj, ta=True, name="mm_in_dw")
    g["w_ab"] = _mm(s["h"], dab, ta=True, out_dtypes=(F32,), name="mm_ab_dw")
    dh_ab = _mm(dab, p["w_ab"], tb=True, out_dtypes=(F32,), name="mm_ab_dx")
    dh = _mm(dproj, p["w_main"], tb=True, out_dtypes=(F32,), epi=_add_epilogue, extras=(dh_ab,), name="mm_in_dx")
    dx, g["norm_mix_pre"] = _norm_bwd(s["x"], p["norm_mix_pre"], dh, dx1, out_dtype=F32, name="norm_bwd_f")
    return dx, g, received


NORMS = ("norm_mix_pre", "norm_mix_post", "norm_ffn_pre", "norm_ffn_post")
SMALL = NORMS + ("gdn_a_log", "gdn_dt_bias", "gdn_norm_w")
CONVS = ("conv_qkv_w", "conv_sc_w")
AB_LO = 2048


def _split_w_in(w_in):
    main = jnp.concatenate([w_in[..., :AB_LO], w_in[..., AB_LO + 2 * GDN_HEADS:]], axis=-1)
    ab = w_in[..., AB_LO:AB_LO + 2 * GDN_HEADS]
    pad = [(0, 0)] * (ab.ndim - 1) + [(0, AB_W - 2 * GDN_HEADS)]
    return main, jnp.pad(ab, pad)


def _join_w_in(main, ab):
    return jnp.concatenate([main[..., :AB_LO], ab[..., :2 * GDN_HEADS].astype(main.dtype), main[..., AB_LO:]], axis=-1)


def kernel(x, norm_mix_pre, w_in, conv_qkv_w, gdn_a_log, gdn_dt_bias, gdn_norm_w, conv_sc_w, w_branch, w_out, norm_mix_post, norm_ffn_pre, w_ff1, w_ff2, norm_ffn_post, loss_target, m_norm_mix_pre, m_w_in, m_conv_qkv_w, m_gdn_a_log, m_gdn_dt_bias, m_gdn_norm_w, m_conv_sc_w, m_w_branch, m_w_out, m_norm_mix_post, m_norm_ffn_pre, m_w_ff1, m_w_ff2, m_norm_ffn_post, v_norm_mix_pre, v_w_in, v_conv_qkv_w, v_gdn_a_log, v_gdn_dt_bias, v_gdn_norm_w, v_conv_sc_w, v_w_branch, v_w_out, v_norm_mix_post, v_norm_ffn_pre, v_w_ff1, v_w_ff2, v_norm_ffn_post):
    names = ("norm_mix_pre", "w_in", "conv_qkv_w", "gdn_a_log", "gdn_dt_bias", "gdn_norm_w", "conv_sc_w", "w_branch",
             "w_out", "norm_mix_post", "norm_ffn_pre", "w_ff1", "w_ff2", "norm_ffn_post")
    w = dict(zip(names, (norm_mix_pre, w_in, conv_qkv_w, gdn_a_log, gdn_dt_bias, gdn_norm_w, conv_sc_w, w_branch,
                         w_out, norm_mix_post, norm_ffn_pre, w_ff1, w_ff2, norm_ffn_post)))
    m = dict(zip(names, (m_norm_mix_pre, m_w_in, m_conv_qkv_w, m_gdn_a_log, m_gdn_dt_bias, m_gdn_norm_w, m_conv_sc_w,
                         m_w_branch, m_w_out, m_norm_mix_post, m_norm_ffn_pre, m_w_ff1, m_w_ff2, m_norm_ffn_post)))
    v = dict(zip(names, (v_norm_mix_pre, v_w_in, v_conv_qkv_w, v_gdn_a_log, v_gdn_dt_bias, v_gdn_norm_w, v_conv_sc_w,
                         v_w_branch, v_w_out, v_norm_mix_post, v_norm_ffn_pre, v_w_ff1, v_w_ff2, v_norm_ffn_post)))
    me = 4 * lax.axis_index("x") + 2 * lax.axis_index("y") + lax.axis_index("c")

    conv_shapes = [w[k].shape for k in CONVS]
    conv_all, = _all_gather([_pack_vec([w[k] for k in CONVS])], name="gather_small")
    convs = {k: _to_global(blk, 2) for k, blk in zip(CONVS, _unpack_vec(conv_all, conv_shapes))}
    shards = [[w[k][l].astype(BF16) for k in BIG] for l in range(DEPTH)]
    n_big = len(BIG)

    xs = x[0]
    w_in_blocks, = _all_gather(shards[0][:1], name="gather_weights")
    layers, saved = [], []
    for l in range(DEPTH):
        p = {k: convs[k][l] for k in CONVS}
        p.update({k: w[k][l] for k in SMALL})
        p["w_main"], p["w_ab"] = _split_w_in(_to_global(w_in_blocks, BIG_AXIS["w_in"] - 1))
        riding = shards[l][1:] + (shards[l + 1][:1] if l + 1 < DEPTH else [])
        xs, s, p, rest = _layer_fwd(xs, p, riding)
        layers.append(p)
        saved.append(s)
        w_in_blocks = rest[0] if rest else None
    dy, loss_lanes = _loss_head(xs, loss_target[0], name="loss_head")

    grads, big_sums, above = [None] * DEPTH, [[None] * n_big for _ in range(DEPTH)], []
    for l in reversed(range(DEPTH)):
        dy, g, received = _layer_bwd(dy, layers[l], saved[l], above)
        sums = [_sum_slots(r, name="rs_sum_slots") for r in received]
        big_sums[l][1:] = sums[:n_big - 1]
        if above:
            big_sums[l + 1][0] = sums[n_big - 1]
        g["w_in"] = _join_w_in(g.pop("w_main"), g.pop("w_ab"))
        above = [_rows(_to_blocks(g["w_in"], BIG_AXIS["w_in"] - 1))]
        grads[l] = g
    got = _exchange_sibling(above, name="rs_sibling")
    parts = [_pair_sum(b, r, name="rs_pair_sum") for b, r in zip(above, got)]
    got2 = _exchange_chips(parts, name="rs_chips")
    big_sums[0][0] = _final_sum(parts[0], got2[0], name="rs_final_sum")
    gsum = {k: jnp.stack([big_sums[l][i] for l in range(DEPTH)]).reshape(w[k].shape) for i, k in enumerate(BIG)}
    stack = {k: jnp.stack([g[k] for g in grads]) for k in SMALL + CONVS}

    small_parts = [stack[k] for k in SMALL + CONVS] + [jnp.sum(loss_lanes).reshape(1)]
    small_shapes = [stack[k].shape for k in SMALL + CONVS] + [(1,)]
    summed = _sum_devices(_all_gather([_pack_vec(small_parts)], name="gather_small_grads")[0], name="sum_small")
    small = _unpack_vec(summed, small_shapes)
    loss = small[-1][0]
    for k, val in zip(SMALL + CONVS, small[:-1]):
        gsum[k] = val
    for k in CONVS:
        per = gsum[k].shape[2] // N_DEV
        gsum[k] = lax.dynamic_slice_in_dim(gsum[k], me * per, per, axis=2)

    delta, new_m, new_v = {}, {}, {}
    for k in names:
        shp = w[k].shape
        two_d = (-1, shp[-1]) if len(shp) > 1 else (1, -1)
        d_, m_, v_ = _adamw(w[k].reshape(two_d), gsum[k].reshape(two_d), m[k].reshape(two_d), v[k].reshape(two_d),
                            name="adamw")
        delta[k], new_m[k], new_v[k] = d_.reshape(shp), m_.reshape(shp), v_.reshape(shp)

    return (loss, dy[None], *[gsum[k].reshape(w[k].shape) for k in names], *[delta[k] for k in names], *[new_m[k] for k in names],
            *[new_v[k] for k in names])
```

```python
import jax
import jax.numpy as jnp
from jax import lax
from jax.experimental import pallas as pl
from jax.experimental.pallas import tpu as pltpu

F32, BF16 = jnp.float32, jnp.bfloat16
MESH_ID = pl.DeviceIdType.MESH

N_DEV = 8
DEPTH = 4
D_MODEL = 1024
D_FF = 4096
EPS = 1e-6
GDN_HEADS, GDN_DIM, GDN_CONV = 4, 128, 4
GDN_W = GDN_HEADS * GDN_DIM
CHUNK = 64
SB_HEADS, SB_DIM = 8, 64
SB_W = SB_HEADS * SB_DIM
SB_KBLOCK = 256
SC_W, SC_CONV = 512, 3
C_GQKV, C_GGATE, C_SBQKV, C_SCX, C_SCB, C_SCC, C_GATES = 0, 1536, 2048, 3584, 4096, 4608, 5120
AB_W = 128

ADAM_LR, ADAM_B1, ADAM_B2, ADAM_EPS, ADAM_WD, ADAM_STEP = 0.001, 0.9, 0.999, 1e-08, 0.01, 10

VMEM_LIMIT = 48 * 2 ** 20


def _cparams(*sem):
    return pltpu.CompilerParams(dimension_semantics=sem or None, vmem_limit_bytes=VMEM_LIMIT)


def _tile(n, pref):
    if n <= pref:
        return n
    t = pref
    while n % t:
        t -= 128
    assert t > 0
    return t


def _dot(a, b, ca=1, cb=0):
    return lax.dot_general(a.astype(BF16), b.astype(BF16), (((ca,), (cb,)), ((), ())), preferred_element_type=F32)


def _split2(x):
    hi = x.astype(BF16)
    return hi, (x - hi.astype(F32)).astype(BF16)


def _sigmoid(z):
    e = jnp.exp(-jnp.abs(z))
    return jnp.where(z >= 0, 1.0, e) / (1.0 + e)


def _softplus(z):
    return jnp.maximum(z, 0.0) + jnp.log(1.0 + jnp.exp(-jnp.abs(z)))


def _mm(a, b, *, name, ta=False, tb=False, out_dtypes=(BF16,), epi=None, extras=()):
    assert a.dtype == BF16 and b.dtype == BF16
    m, k = (a.shape[1], a.shape[0]) if ta else a.shape
    n = b.shape[0] if tb else b.shape[1]
    assert (b.shape[1] if tb else b.shape[0]) == k
    tm, tn, tk = _tile(m, 1024), _tile(n, 1024), _tile(k, 2048)
    nk = k // tk
    ca, cb = (0 if ta else 1), (1 if tb else 0)
    n_ex, n_out = len(extras), len(out_dtypes)

    def body(*refs):
        a_ref, b_ref = refs[0], refs[1]
        ex = refs[2:2 + n_ex]
        outs = refs[2 + n_ex:2 + n_ex + n_out]
        acc = refs[-1]
        kk = pl.program_id(2)
        part = lax.dot_general(a_ref[...], b_ref[...], (((ca,), (cb,)), ((), ())), preferred_element_type=F32)

        def finish(r):
            vals = (r,) if epi is None else epi(r, *[e[...] for e in ex])
            for o, v in zip(outs, vals):
                o[...] = v.astype(o.dtype)

        if nk == 1:
            finish(part)
        else:
            @pl.when(kk == 0)
            def _():
                acc[...] = part

            @pl.when(kk > 0)
            def _():
                acc[...] += part

            @pl.when(kk == nk - 1)
            def _():
                finish(acc[...])

    a_spec = pl.BlockSpec((tk, tm), lambda i, j, kk: (kk, i)) if ta else pl.BlockSpec((tm, tk), lambda i, j, kk: (i, kk))
    b_spec = pl.BlockSpec((tn, tk), lambda i, j, kk: (j, kk)) if tb else pl.BlockSpec((tk, tn), lambda i, j, kk: (kk, j))
    io_spec = pl.BlockSpec((tm, tn), lambda i, j, kk: (i, j))
    res = pl.pallas_call(
        body, name=name, grid=(m // tm, n // tn, nk),
        in_specs=[a_spec, b_spec] + [io_spec] * n_ex,
        out_specs=[io_spec] * n_out,
        out_shape=[jax.ShapeDtypeStruct((m, n), dt) for dt in out_dtypes],
        scratch_shapes=[pltpu.VMEM((tm, tn) if nk > 1 else (8, 128), F32)],
        compiler_params=_cparams("parallel", "parallel", "arbitrary"),
    )(a, b, *extras)
    return res[0] if n_out == 1 else res


ROW_TILE = 512


def _norm_fwd(y, w, res=None, *, out_dtype, name):
    t, d = y.shape
    tm = _tile(t, ROW_TILE)
    has_res = res is not None

    def body(*refs):
        y_ref, w_ref = refs[0], refs[1]
        o_ref = refs[-1]
        yv = y_ref[...]
        r = lax.rsqrt(jnp.mean(yv * yv, axis=-1, keepdims=True) + EPS)
        out = yv * r * w_ref[...]
        if has_res:
            out = out + refs[2][...]
        o_ref[...] = out.astype(o_ref.dtype)

    row = pl.BlockSpec((tm, d), lambda i: (i, 0))
    vec = pl.BlockSpec((1, d), lambda i: (0, 0))
    args = (y, w.reshape(1, d)) + ((res,) if has_res else ())
    return pl.pallas_call(
        body, name=name, grid=(t // tm,), in_specs=[row, vec] + [row] * has_res, out_specs=row,
        out_shape=jax.ShapeDtypeStruct((t, d), out_dtype), compiler_params=_cparams("parallel"),
    )(*args)


def _norm_bwd(y, w, dout, add=None, *, out_dtype, name):
    t, d = y.shape
    tm = _tile(t, ROW_TILE)
    has_add = add is not None

    def body(*refs):
        y_ref, w_ref, do_ref = refs[0], refs[1], refs[2]
        dy_ref, dw_ref = refs[-2], refs[-1]
        yv = y_ref[...]
        r = lax.rsqrt(jnp.mean(yv * yv, axis=-1, keepdims=True) + EPS)
        yh = yv * r
        dov = do_ref[...].astype(F32)
        gw = dov * w_ref[...]
        dy = r * (gw - yh * jnp.mean(gw * yh, axis=-1, keepdims=True))
        if has_add:
            dy = dy + refs[3][...]
        dy_ref[...] = dy.astype(dy_ref.dtype)
        part = jnp.sum(dov * yh, axis=0, keepdims=True)

        @pl.when(pl.program_id(0) == 0)
        def _():
            dw_ref[...] = part

        @pl.when(pl.program_id(0) > 0)
        def _():
            dw_ref[...] += part

    row = pl.BlockSpec((tm, d), lambda i: (i, 0))
    vec = pl.BlockSpec((1, d), lambda i: (0, 0))
    args = (y, w.reshape(1, d), dout) + ((add,) if has_add else ())
    return pl.pallas_call(
        body, name=name, grid=(t // tm,), in_specs=[row, vec, row] + [row] * has_add, out_specs=[row, vec],
        out_shape=[jax.ShapeDtypeStruct((t, d), out_dtype), jax.ShapeDtypeStruct((1, d), F32)],
        compiler_params=_cparams("arbitrary"),
    )(*args)


def _shift_down(u, s):
    if s == 0:
        return u
    rows = lax.broadcasted_iota(jnp.int32, u.shape, 0)
    return jnp.where(rows >= s, pltpu.roll(u, s, 0), 0.0)


def _shift_up(u, s):
    if s == 0:
        return u
    t = u.shape[0]
    rows = lax.broadcasted_iota(jnp.int32, u.shape, 0)
    return jnp.where(rows < t - s, pltpu.roll(u, t - s, 0), 0.0)


def _conv_fwd(u, w):
    kk = w.shape[0]
    out = u * w[kk - 1:kk, :]
    for i in range(kk - 1):
        out = out + _shift_down(u, kk - 1 - i) * w[i:i + 1, :]
    return out


def _conv_bwd(u, w, dc):
    kk = w.shape[0]
    du = dc * w[kk - 1:kk, :]
    dws = []
    for i in range(kk):
        s = kk - 1 - i
        if s:
            du = du + _shift_up(dc, s) * w[i:i + 1, :]
        dws.append(jnp.sum(dc * _shift_down(u, s), axis=0, keepdims=True))
    return du, dws


def _gdn_pre_math(x, w, slab):
    c = _conv_fwd(x, w)
    sig = _sigmoid(c)
    s = c * sig
    r = lax.rsqrt(jnp.sum(s * s, axis=-1, keepdims=True) + EPS)
    scale = jnp.where(slab < GDN_HEADS, GDN_DIM ** -0.5, 1.0)
    return c, sig, s, r, scale


def _gdn_pre_fwd(proj, conv_w, *, name):
    t = proj.shape[0]
    nslab = 3 * GDN_HEADS

    def body(x_ref, w_ref, o_ref):
        slab = pl.program_id(0)
        _, _, s, r, scale = _gdn_pre_math(x_ref[...].astype(F32), w_ref[...], slab)
        o_ref[0] = jnp.where(slab < 2 * GDN_HEADS, s * r * scale, s)

    return pl.pallas_call(
        body, name=name, grid=(nslab,),
        in_specs=[pl.BlockSpec((t, GDN_DIM), lambda j: (0, j)), pl.BlockSpec((GDN_CONV, GDN_DIM), lambda j: (0, j))],
        out_specs=pl.BlockSpec((1, t, GDN_DIM), lambda j: (j, 0, 0)),
        out_shape=jax.ShapeDtypeStruct((nslab, t, GDN_DIM), F32), compiler_params=_cparams("parallel"),
    )(proj, conv_w)


def _gdn_pre_bwd(proj, conv_w, dqkv, *, name):
    t = proj.shape[0]
    nslab = 3 * GDN_HEADS

    def body(x_ref, w_ref, d_ref, dx_ref, dw_ref):
        slab = pl.program_id(0)
        x = x_ref[...].astype(F32)
        w = w_ref[...]
        c, sig, s, r, scale = _gdn_pre_math(x, w, slab)
        dout = d_ref[0]
        yn = s * r
        dn = dout * scale
        ds_norm = r * (dn - yn * jnp.sum(dn * yn, axis=-1, keepdims=True))
        ds = jnp.where(slab < 2 * GDN_HEADS, ds_norm, dout)
        dc = ds * (sig + c * sig * (1.0 - sig))
        dx, dws = _conv_bwd(x, w, dc)
        dx_ref[...] = dx.astype(dx_ref.dtype)
        for i, dwi in enumerate(dws):
            dw_ref[i:i + 1, :] = dwi

    return pl.pallas_call(
        body, name=name, grid=(nslab,),
        in_specs=[pl.BlockSpec((t, GDN_DIM), lambda j: (0, j)), pl.BlockSpec((GDN_CONV, GDN_DIM), lambda j: (0, j)),
                  pl.BlockSpec((1, t, GDN_DIM), lambda j: (j, 0, 0))],
        out_specs=[pl.BlockSpec((t, GDN_DIM), lambda j: (0, j)), pl.BlockSpec((GDN_CONV, GDN_DIM), lambda j: (0, j))],
        out_shape=[jax.ShapeDtypeStruct((t, 3 * GDN_W), BF16), jax.ShapeDtypeStruct((GDN_CONV, 3 * GDN_W), F32)],
        compiler_params=_cparams("parallel"),
    )(proj, conv_w, dqkv)


def _sc_specs(t):
    def col(base):
        return pl.BlockSpec((t, 128), lambda j: (0, base // 128 + j))
    return [col(C_SCX), col(C_SCB), col(C_SCC), pl.BlockSpec((SC_CONV, 128), lambda j: (0, j))]


def _sc_fwd(proj, conv_w, *, name):
    t = proj.shape[0]

    def body(x_ref, b_ref, c_ref, w_ref, o_ref):
        u = c_ref[...].astype(F32) * x_ref[...].astype(F32)
        o_ref[...] = (b_ref[...].astype(F32) * _conv_fwd(u, w_ref[...])).astype(o_ref.dtype)

    return pl.pallas_call(
        body, name=name, grid=(SC_W // 128,), in_specs=_sc_specs(t),
        out_specs=pl.BlockSpec((t, 128), lambda j: (0, j)),
        out_shape=jax.ShapeDtypeStruct((t, SC_W), BF16), compiler_params=_cparams("parallel"),
    )(proj, proj, proj, conv_w)


def _sc_bwd(proj, conv_w, dy, *, name):
    t = proj.shape[0]
    nj = SC_W // 128

    def body(x_ref, b_ref, c_ref, w_ref, dy_ref, dx_ref, db_ref, dc_ref, dw_ref):
        x, b, c = x_ref[...].astype(F32), b_ref[...].astype(F32), c_ref[...].astype(F32)
        w = w_ref[...]
        u = c * x
        dyv = dy_ref[...].astype(F32)
        db_ref[...] = (dyv * _conv_fwd(u, w)).astype(db_ref.dtype)
        du, dws = _conv_bwd(u, w, dyv * b)
        dx_ref[...] = (du * c).astype(dx_ref.dtype)
        dc_ref[...] = (du * x).astype(dc_ref.dtype)
        for i, dwi in enumerate(dws):
            dw_ref[i:i + 1, :] = dwi

    return pl.pallas_call(
        body, name=name, grid=(nj,),
        in_specs=_sc_specs(t) + [pl.BlockSpec((t, 128), lambda j: (0, j))],
        out_specs=[pl.BlockSpec((t, 128), lambda j: (0, j))] * 3 + [pl.BlockSpec((SC_CONV, 128), lambda j: (0, j))],
        out_shape=[jax.ShapeDtypeStruct((t, SC_W), BF16)] * 3 + [jax.ShapeDtypeStruct((SC_CONV, SC_W), F32)],
        compiler_params=_cparams("parallel"),
    )(proj, proj, proj, conv_w, dy)


def _dot_each(a_list, b_list, ca=1, cb=0):
    return [_dot(a, b, ca, cb) for a, b in zip(a_list, b_list)]


def _dot3_each(a_list, b_list, ca=1, cb=0):
    sa = [_split2(a) for a in a_list]
    sb = [_split2(b) for b in b_list]
    prods = [(_dot(a1, b1, ca, cb), _dot(a1, b2, ca, cb), _dot(a2, b1, ca, cb)) for (a1, a2), (b1, b2) in zip(sa, sb)]
    return [x + (y + z) for x, y, z in prods]


def _split3(x):
    hi = x.astype(BF16)
    rest = x - hi.astype(F32)
    mid = rest.astype(BF16)
    return hi, mid, (rest - mid.astype(F32)).astype(BF16)


def _ones_dot_each(m, x_list, ca=1, cb=0):
    mb = m.astype(BF16)
    parts = [[_dot(mb, p, ca, cb) for p in _split3(x)] for x in x_list]
    return [p[0] + (p[1] + p[2]) for p in parts]


def _dot_ones_each(x_list, m, ca=1, cb=0):
    mb = m.astype(BF16)
    parts = [[_dot(p, mb, ca, cb) for p in _split3(x)] for x in x_list]
    return [p[0] + (p[1] + p[2]) for p in parts]


def _tri_inv_each(a_list):
    c = a_list[0].shape[0]
    ri = lax.broadcasted_iota(jnp.int32, (c, c), 0)
    ci = lax.broadcasted_iota(jnp.int32, (c, c), 1)
    eye = (ri == ci).astype(F32)
    blk = 8
    pws = [-jnp.where(ri // blk == ci // blk, a, 0.0) for a in a_list]
    invs = [eye + b for b in pws]
    for _ in range(2):
        pws = _dot_each(pws, pws)
        invs = [i + u for i, u in zip(invs, _dot_each(invs, pws))]
    while blk < c:
        sel = (ri // (2 * blk) == ci // (2 * blk)) & (ri // blk != ci // blk)
        offs = [jnp.where(sel, a, 0.0) for a in a_list]
        invs = [i - t for i, t in zip(invs, _dot_each(_dot_each(invs, offs), invs))]
        blk *= 2
    resid = [eye - x for x in _dot3_each([eye + a for a in a_list], invs)]
    return [i + t for i, t in zip(invs, _dot_each(invs, resid))]


def _gdn_chunks(qs, ks, vs, abs_, head, ea, dtb):
    c = qs[0].shape[0]
    lane = lax.broadcasted_iota(jnp.int32, abs_[0].shape, 1)
    a_s = [jnp.sum(jnp.where(lane == head, ab, 0.0), axis=1, keepdims=True) for ab in abs_]
    b_s = [jnp.sum(jnp.where(lane == GDN_HEADS + head, ab, 0.0), axis=1, keepdims=True) for ab in abs_]
    ri = lax.broadcasted_iota(jnp.int32, (c, c), 0)
    ci = lax.broadcasted_iota(jnp.int32, (c, c), 1)
    tri, strict = ri >= ci, ri > ci
    ltri = tri.astype(F32)
    beta = [_sigmoid(b) for b in b_s]
    sig_a = [_sigmoid(a + dtb) for a in a_s]
    g = [-ea * _softplus(a + dtb) for a in a_s]
    g_cc = [jnp.broadcast_to(x, (c, c)) for x in g]
    gi = _ones_dot_each(ltri, g_cc)
    gj = _dot_ones_each(g_cc, (ri <= ci).astype(F32), 0, 0)
    decay = [jnp.exp(jnp.where(tri, x - y, -1e30)) for x, y in zip(gi, gj)]
    gc = _ones_dot_each(ltri, [jnp.broadcast_to(x, (c, GDN_DIM)) for x in g])
    g_tot = [jnp.sum(x, axis=0, keepdims=True) for x in g]
    egc = [jnp.exp(x) for x in gc]
    ekd = [jnp.exp(t - x) for t, x in zip(g_tot, gc)]
    kb = [k * b for k, b in zip(ks, beta)]
    vb = [v * b for v, b in zip(vs, beta)]
    kbg = [x * e for x, e in zip(kb, egc)]
    mkk = _dot_each(kb, ks, 1, 1)
    a_kk = [jnp.where(strict, m * d, 0.0) for m, d in zip(mkk, decay)]
    tinv = _tri_inv_each(a_kk)
    u = _dot_each(tinv, vb)
    w = _dot_each(tinv, kbg)
    mqk = _dot_each(qs, ks, 1, 1)
    a_qk = [jnp.where(tri, m * d, 0.0) for m, d in zip(mqk, decay)]
    qd = [q * e for q, e in zip(qs, egc)]
    kd = [k * e for k, e in zip(ks, ekd)]
    return dict(beta=beta, sig_a=sig_a, g=g, decay=decay, egc=egc, ekd=ekd, g_tot=g_tot, kb=kb, vb=vb, kbg=kbg,
                a_kk=a_kk, tinv=tinv, u=u, w=w, a_qk=a_qk, qd=qd, kd=kd, tri=tri, strict=strict)


def _chunks_in_step(n):
    for cps in (8, 4, 2):
        if n % cps == 0:
            return cps
    return 1


def _gdn_local_specs(t, cps):
    rows = cps * CHUNK

    def slab(base):
        return pl.BlockSpec((1, rows, GDN_DIM), lambda h, n: (base + h, n, 0))
    smem = pl.BlockSpec(memory_space=pltpu.SMEM)
    return [slab(0), slab(GDN_HEADS), slab(2 * GDN_HEADS), pl.BlockSpec((rows, AB_W), lambda h, n: (n, 0)), smem, smem]


def _scalar_row(ref, head):
    return jnp.full((1, 1), ref[0, head], F32)


def _gdn_local_fwd(qkv, ab, a_log, dt_bias, *, name):
    t = qkv.shape[1]
    n = t // CHUNK
    cps = _chunks_in_step(n)
    rows = cps * CHUNK
    sls = [slice(j * CHUNK, (j + 1) * CHUNK) for j in range(cps)]

    def body(q_ref, k_ref, v_ref, ab_ref, al_ref, dt_ref, u_ref, w_ref, qd_ref, kd_ref, aqk_ref, gl_ref):
        head = pl.program_id(0)
        ea = jnp.exp(_scalar_row(al_ref, head))
        dtb = _scalar_row(dt_ref, head)
        r = _gdn_chunks([q_ref[0, sl, :] for sl in sls], [k_ref[0, sl, :] for sl in sls],
                        [v_ref[0, sl, :] for sl in sls], [ab_ref[sl, :] for sl in sls], head, ea, dtb)
        for j, sl in enumerate(sls):
            u_ref[0, sl, :] = r["u"][j]
            w_ref[0, sl, :] = r["w"][j]
            qd_ref[0, sl, :] = r["qd"][j]
            kd_ref[0, sl, :] = r["kd"][j]
            aqk_ref[0, sl, :] = r["a_qk"][j]
            gl_ref[0, j] = jnp.exp(jnp.broadcast_to(r["g_tot"][j], (1, GDN_DIM)))

    big = pl.BlockSpec((1, rows, GDN_DIM), lambda h, i: (h, i, 0))
    big_shape = jax.ShapeDtypeStruct((GDN_HEADS, t, GDN_DIM), F32)
    return pl.pallas_call(
        body, name=name, grid=(GDN_HEADS, n // cps), in_specs=_gdn_local_specs(t, cps),
        out_specs=[big] * 4 + [pl.BlockSpec((1, rows, CHUNK), lambda h, i: (h, i, 0)),
                               pl.BlockSpec((1, cps, 1, GDN_DIM), lambda h, i: (h, i, 0, 0))],
        out_shape=[big_shape] * 4 + [jax.ShapeDtypeStruct((GDN_HEADS, t, CHUNK), F32),
                                     jax.ShapeDtypeStruct((GDN_HEADS, n, 1, GDN_DIM), F32)],
        compiler_params=_cparams("parallel", "parallel"),
    )(qkv, qkv, qkv, ab, a_log, dt_bias)


def _gdn_local_bwd(qkv, ab, a_log, dt_bias, du, dw, dqd, dkd, daqk, dgl, *, name):
    t = qkv.shape[1]
    n = t // CHUNK
    cps = _chunks_in_step(n)
    rows = cps * CHUNK
    sls = [slice(j * CHUNK, (j + 1) * CHUNK) for j in range(cps)]

    def body(q_ref, k_ref, v_ref, ab_ref, al_ref, dt_ref, du_ref, dw_ref, dqd_ref, dkd_ref, daqk_ref, dgl_ref,
             dq_ref, dk_ref, dv_ref, dab_ref, dsc_ref):
        head = pl.program_id(0)
        ea = jnp.exp(_scalar_row(al_ref, head))
        dtb = _scalar_row(dt_ref, head)
        lane = lax.broadcasted_iota(jnp.int32, (CHUNK, AB_W), 1)
        lane1 = lax.broadcasted_iota(jnp.int32, (1, GDN_DIM), 1)
        ri = lax.broadcasted_iota(jnp.int32, (CHUNK, CHUNK), 0)
        ci = lax.broadcasted_iota(jnp.int32, (CHUNK, CHUNK), 1)
        utri = (ri <= ci).astype(F32)
        ones = jnp.ones((CHUNK, GDN_DIM), F32)
        qs, ks, vs = ([ref[0, sl, :] for sl in sls] for ref in (q_ref, k_ref, v_ref))
        r = _gdn_chunks(qs, ks, vs, [ab_ref[sl, :] for sl in sls], head, ea, dtb)
        duv, dwv, dqdv, dkdv = ([ref[0, sl, :] for sl in sls] for ref in (du_ref, dw_ref, dqd_ref, dkd_ref))
        d_aqk = [jnp.where(r["tri"], daqk_ref[0, sl, :], 0.0) for sl in sls]
        dvb = _dot_each(r["tinv"], duv, 0, 0)
        dkbg = _dot_each(r["tinv"], dwv, 0, 0)
        outer = [x + y for x, y in zip(_dot_each(dvb, r["u"], 1, 1), _dot_each(dkbg, r["w"], 1, 1))]
        d_akk = [-jnp.where(r["strict"], x, 0.0) for x in outer]
        e = [x * a + y * b for x, a, y, b in zip(d_akk, r["a_kk"], d_aqk, r["a_qk"])]
        dmkk = [x * d for x, d in zip(d_akk, r["decay"])]
        dmqk = [x * d for x, d in zip(d_aqk, r["decay"])]
        dkb = [x + y * eg for x, y, eg in zip(_dot_each(dmkk, ks), dkbg, r["egc"])]
        dk = [a + b + x * ek + y * bt for a, b, x, ek, y, bt in zip(
            _dot_each(dmkk, r["kb"], 0, 0), _dot_each(dmqk, qs, 0, 0), dkdv, r["ekd"], dkb, r["beta"])]
        dq = [a + x * eg for a, x, eg in zip(_dot_each(dmqk, ks), dqdv, r["egc"])]
        col_sums = _dot_ones_each(e, ones, 0, 0)
        acc_alog = jnp.zeros((1, 1), F32)
        acc_dtb = jnp.zeros((1, 1), F32)
        dgc_lanes, d_tots, dbetas = [], [], []
        for j in range(cps):
            dbetas.append(jnp.sum(dkb[j] * ks[j] + dvb[j] * vs[j], axis=1, keepdims=True))
            kd_term = jnp.sum(dkdv[j] * r["kd"][j], axis=1, keepdims=True)
            dgc = (jnp.sum(e[j], axis=1, keepdims=True)
                   + jnp.sum(dqdv[j] * r["qd"][j] + dkbg[j] * r["kbg"][j], axis=1, keepdims=True) - kd_term)
            dgc_lanes.append(jnp.broadcast_to(dgc, (CHUNK, GDN_DIM)) - col_sums[j])
            dgl_tot = jnp.sum(dgl_ref[0, j], axis=1, keepdims=True) * jnp.exp(r["g_tot"][j])
            d_tots.append(jnp.sum(kd_term, axis=0, keepdims=True) + dgl_tot)
        suffix_sums = _ones_dot_each(utri, dgc_lanes)
        for j, sl in enumerate(sls):
            dq_ref[0, sl, :] = dq[j]
            dk_ref[0, sl, :] = dk[j]
            dv_ref[0, sl, :] = dvb[j] * r["beta"][j]
            dg = jnp.sum(jnp.where(lane == 0, suffix_sums[j] + d_tots[j], 0.0), axis=1, keepdims=True)
            da = dg * (-ea) * r["sig_a"][j]
            db = dbetas[j] * r["beta"][j] * (1.0 - r["beta"][j])
            dab_ref[0, sl, :] = jnp.where(lane == head, da, 0.0) + jnp.where(lane == GDN_HEADS + head, db, 0.0)
            acc_alog = acc_alog + jnp.sum(dg * r["g"][j], axis=0, keepdims=True)
            acc_dtb = acc_dtb + jnp.sum(da, axis=0, keepdims=True)
        dsc_ref[0, 0] = jnp.where(lane1 == 0, acc_alog, 0.0) + jnp.where(lane1 == 1, acc_dtb, 0.0)

    big = pl.BlockSpec((1, rows, GDN_DIM), lambda h, i: (h, i, 0))
    big_shape = jax.ShapeDtypeStruct((GDN_HEADS, t, GDN_DIM), F32)
    return pl.pallas_call(
        body, name=name, grid=(GDN_HEADS, n // cps),
        in_specs=_gdn_local_specs(t, cps) + [big] * 4 + [pl.BlockSpec((1, rows, CHUNK), lambda h, i: (h, i, 0)),
                                                        pl.BlockSpec((1, cps, 1, GDN_DIM), lambda h, i: (h, i, 0, 0))],
        out_specs=[big] * 4 + [pl.BlockSpec((1, 1, 1, GDN_DIM), lambda h, i: (h, i, 0, 0))],
        out_shape=[big_shape] * 4 + [jax.ShapeDtypeStruct((GDN_HEADS, n // cps, 1, GDN_DIM), F32)],
        compiler_params=_cparams("parallel", "parallel"),
    )(qkv, qkv, qkv, ab, a_log, dt_bias, du, dw, dqd, dkd, daqk, dgl)


SCAN_CHUNKS = 4


def _scan_chunks(n):
    return SCAN_CHUNKS if n % SCAN_CHUNKS == 0 else 1


def _gdn_scan_fwd(u, w, qd, kd, aqk, gl, *, name):
    h, t, _ = u.shape
    n = t // CHUNK
    cps = _scan_chunks(n)
    rows = cps * CHUNK
    hs = range(h)

    def body(u_ref, w_ref, qd_ref, kd_ref, aqk_ref, gl_ref, o_ref, s_ref, state):
        @pl.when(pl.program_id(0) == 0)
        def _():
            state[...] = jnp.zeros_like(state)

        ss = [state[hh] for hh in hs]
        for j in range(cps):
            sl = slice(j * CHUNK, (j + 1) * CHUNK)
            for hh in hs:
                s_ref[hh, j] = ss[hh]
            vn = [u_ref[hh, sl, :] - x for hh, x in zip(hs, _dot_each([w_ref[hh, sl, :] for hh in hs], ss))]
            from_state = _dot_each([qd_ref[hh, sl, :] for hh in hs], ss)
            from_chunk = _dot_each([aqk_ref[hh, sl, :] for hh in hs], vn)
            writes = _dot_each([kd_ref[hh, sl, :] for hh in hs], vn, 0, 0)
            for hh in hs:
                o_ref[hh, sl, :] = from_state[hh] + from_chunk[hh]
            ss = [ss[hh] * gl_ref[hh, j] + writes[hh] for hh in hs]
        for hh in hs:
            state[hh] = ss[hh]

    big = pl.BlockSpec((h, rows, GDN_DIM), lambda i: (0, i, 0))
    return pl.pallas_call(
        body, name=name, grid=(n // cps,),
        in_specs=[big] * 4 + [pl.BlockSpec((h, rows, CHUNK), lambda i: (0, i, 0)),
                              pl.BlockSpec((h, cps, 1, GDN_DIM), lambda i: (0, i, 0, 0))],
        out_specs=[big, pl.BlockSpec((h, cps, GDN_DIM, GDN_DIM), lambda i: (0, i, 0, 0))],
        out_shape=[jax.ShapeDtypeStruct((h, t, GDN_DIM), F32), jax.ShapeDtypeStruct((h, n, GDN_DIM, GDN_DIM), F32)],
        scratch_shapes=[pltpu.VMEM((h, GDN_DIM, GDN_DIM), F32)],
        compiler_params=_cparams("arbitrary"),
    )(u, w, qd, kd, aqk, gl)


def _gdn_scan_bwd(u, w, qd, kd, aqk, gl, states, do, *, name):
    h, t, _ = u.shape
    n = t // CHUNK
    cps = _scan_chunks(n)
    rows = cps * CHUNK
    steps = n // cps
    hs = range(h)

    def body(u_ref, w_ref, qd_ref, kd_ref, aqk_ref, gl_ref, s_ref, do_ref,
             du_ref, dw_ref, dqd_ref, dkd_ref, daqk_ref, dgl_ref, dstate):
        @pl.when(pl.program_id(0) == 0)
        def _():
            dstate[...] = jnp.zeros_like(dstate)

        ri = lax.broadcasted_iota(jnp.int32, (CHUNK, CHUNK), 0)
        ci = lax.broadcasted_iota(jnp.int32, (CHUNK, CHUNK), 1)
        dsn = [dstate[hh] for hh in hs]
        for j in reversed(range(cps)):
            sl = slice(j * CHUNK, (j + 1) * CHUNK)
            ss = [s_ref[hh, j] for hh in hs]
            dov = [do_ref[hh, sl, :] for hh in hs]
            wv = [w_ref[hh, sl, :] for hh in hs]
            vn = [u_ref[hh, sl, :] - x for hh, x in zip(hs, _dot_each(wv, ss))]
            dvn = [x + y for x, y in zip(_dot_each([aqk_ref[hh, sl, :] for hh in hs], dov, 0, 0),
                                         _dot_each([kd_ref[hh, sl, :] for hh in hs], dsn))]
            dws = _dot_each(dvn, ss, 1, 1)
            dqds = _dot_each(dov, ss, 1, 1)
            dkds = _dot_each(vn, dsn, 1, 1)
            daqks = _dot_each(dov, vn, 1, 1)
            reads = _dot_each([qd_ref[hh, sl, :] for hh in hs], dov, 0, 0)
            solves = _dot_each(wv, dvn, 0, 0)
            for hh in hs:
                du_ref[hh, sl, :] = dvn[hh]
                dw_ref[hh, sl, :] = -dws[hh]
                dqd_ref[hh, sl, :] = dqds[hh]
                dkd_ref[hh, sl, :] = dkds[hh]
                daqk_ref[hh, sl, :] = jnp.where(ri >= ci, daqks[hh], 0.0)
                dgl_ref[hh, j] = jnp.sum(dsn[hh] * ss[hh], axis=0, keepdims=True)
            dsn = [reads[hh] + dsn[hh] * gl_ref[hh, j] - solves[hh] for hh in hs]
        for hh in hs:
            dstate[hh] = dsn[hh]

    big = pl.BlockSpec((h, rows, GDN_DIM), lambda i: (0, steps - 1 - i, 0))
    sq = pl.BlockSpec((h, rows, CHUNK), lambda i: (0, steps - 1 - i, 0))
    glb = pl.BlockSpec((h, cps, 1, GDN_DIM), lambda i: (0, steps - 1 - i, 0, 0))
    big_shape = jax.ShapeDtypeStruct((h, t, GDN_DIM), F32)
    return pl.pallas_call(
        body, name=name, grid=(steps,),
        in_specs=[big] * 4 + [sq, glb, pl.BlockSpec((h, cps, GDN_DIM, GDN_DIM), lambda i: (0, steps - 1 - i, 0, 0)),
                              big],
        out_specs=[big] * 4 + [sq, glb],
        out_shape=[big_shape] * 4 + [jax.ShapeDtypeStruct((h, t, CHUNK), F32),
                                     jax.ShapeDtypeStruct((h, n, 1, GDN_DIM), F32)],
        scratch_shapes=[pltpu.VMEM((h, GDN_DIM, GDN_DIM), F32)],
        compiler_params=_cparams("arbitrary"),
    )(u, w, qd, kd, aqk, gl, states, do)


def _gdn_post_fwd(o, proj, norm_w, *, name):
    h, t, _ = o.shape
    tm = _tile(t, ROW_TILE)

    def body(o_ref, g_ref, w_ref, y_ref):
        for hh in range(h):
            sl = slice(hh * GDN_DIM, (hh + 1) * GDN_DIM)
            ov = o_ref[hh]
            gate = g_ref[:, sl].astype(F32)
            r = lax.rsqrt(jnp.mean(ov * ov, axis=-1, keepdims=True) + EPS)
            y_ref[:, sl] = (ov * r * w_ref[...] * (gate * _sigmoid(gate))).astype(y_ref.dtype)

    return pl.pallas_call(
        body, name=name, grid=(t // tm,),
        in_specs=[pl.BlockSpec((h, tm, GDN_DIM), lambda i: (0, i, 0)),
                  pl.BlockSpec((tm, GDN_W), lambda i: (i, C_GGATE // GDN_W)),
                  pl.BlockSpec((1, GDN_DIM), lambda i: (0, 0))],
        out_specs=pl.BlockSpec((tm, GDN_W), lambda i: (i, 0)),
        out_shape=jax.ShapeDtypeStruct((t, GDN_W), BF16), compiler_params=_cparams("parallel"),
    )(o, proj, norm_w.reshape(1, GDN_DIM))


def _gdn_post_bwd(o, proj, norm_w, dy, *, name):
    h, t, _ = o.shape
    tm = _tile(t, ROW_TILE)

    def body(o_ref, g_ref, w_ref, dy_ref, do_ref, dg_ref, dw_ref):
        part = jnp.zeros((1, GDN_DIM), F32)
        for hh in range(h):
            sl = slice(hh * GDN_DIM, (hh + 1) * GDN_DIM)
            ov = o_ref[hh]
            gate = g_ref[:, sl].astype(F32)
            sig = _sigmoid(gate)
            silu = gate * sig
            r = lax.rsqrt(jnp.mean(ov * ov, axis=-1, keepdims=True) + EPS)
            oh = ov * r
            dyv = dy_ref[:, sl].astype(F32)
            dg_ref[:, sl] = (dyv * oh * w_ref[...] * (sig + silu * (1.0 - sig))).astype(dg_ref.dtype)
            dn = dyv * silu
            part = part + jnp.sum(dn * oh, axis=0, keepdims=True)
            gw = dn * w_ref[...]
            do_ref[hh] = r * (gw - oh * jnp.mean(gw * oh, axis=-1, keepdims=True))

        @pl.when(pl.program_id(0) == 0)
        def _():
            dw_ref[...] = part

        @pl.when(pl.program_id(0) > 0)
        def _():
            dw_ref[...] += part

    return pl.pallas_call(
        body, name=name, grid=(t // tm,),
        in_specs=[pl.BlockSpec((h, tm, GDN_DIM), lambda i: (0, i, 0)),
                  pl.BlockSpec((tm, GDN_W), lambda i: (i, C_GGATE // GDN_W)),
                  pl.BlockSpec((1, GDN_DIM), lambda i: (0, 0)),
                  pl.BlockSpec((tm, GDN_W), lambda i: (i, 0))],
        out_specs=[pl.BlockSpec((h, tm, GDN_DIM), lambda i: (0, i, 0)), pl.BlockSpec((tm, GDN_W), lambda i: (i, 0)),
                   pl.BlockSpec((1, GDN_DIM), lambda i: (0, 0))],
        out_shape=[jax.ShapeDtypeStruct((h, t, GDN_DIM), F32), jax.ShapeDtypeStruct((t, GDN_W), BF16),
                   jax.ShapeDtypeStruct((1, GDN_DIM), F32)],
        compiler_params=_cparams("arbitrary"),
    )(o, proj, norm_w.reshape(1, GDN_DIM), dy)


def _gather_phases(x_refs, out_refs, send_sems, recv_sems, local_sems):
    n = len(x_refs)
    mx, my, mc = lax.axis_index("x"), lax.axis_index("y"), lax.axis_index("c")
    me, sibling = (mx, my, mc), (mx, my, 1 - mc)
    chips = [(1 - mx, my), (mx, 1 - my), (1 - mx, 1 - my)]

    def slot(a, px, py, pc):
        return out_refs[a].at[4 * px + 2 * py + pc]

    def copy(a, k, block, to, src=None):
        return pltpu.make_async_remote_copy(
            src_ref=slot(a, *block) if src is None else src, dst_ref=slot(a, *block),
            send_sem=send_sems.at[a, k], recv_sem=recv_sems.at[a, k], device_id=to, device_id_type=MESH_ID)

    def mine():
        return [pltpu.make_async_copy(x_refs[a], slot(a, *me), local_sems.at[a]) for a in range(n)]

    def first():
        out = [copy(a, 1 + j, me, (*chip, mc), src=x_refs[a]) for j, chip in enumerate(chips) for a in range(n)]
        return out + [copy(a, 0, me, sibling, src=x_refs[a]) for a in range(n)]

    def passed():
        return [copy(a, 4 + j, (*chip, mc), sibling) for j, chip in enumerate(chips) for a in range(n)]

    def start():
        for cp in mine() + first():
            cp.start()

    def pass_on():
        for j, chip in enumerate(chips):
            for a in range(n):
                copy(a, 1 + j, (*chip, mc), me).wait_recv()
                copy(a, 4 + j, (*chip, mc), sibling).start()

    def finish():
        for a in range(n):
            copy(a, 0, sibling, me).wait_recv()
        for j, chip in enumerate(chips):
            for a in range(n):
                copy(a, 4 + j, (*chip, 1 - mc), me).wait_recv()
        for cp in first() + passed():
            cp.wait_send()
        for cp in mine():
            cp.wait()

    return start, pass_on, finish


def _gather_extras(shards):
    n = len(shards)
    anyspace = pl.BlockSpec(memory_space=pl.ANY)
    return ([anyspace] * n, [anyspace] * n, [jax.ShapeDtypeStruct((N_DEV,) + x.shape, x.dtype) for x in shards],
            [pltpu.SemaphoreType.DMA((n, 7)), pltpu.SemaphoreType.DMA((n, 7)), pltpu.SemaphoreType.DMA((n,))])


def _scatter_phases(g_refs, out_refs, send_sems, recv_sems, local_sems):
    n = len(g_refs)
    mx, my, mc = lax.axis_index("x"), lax.axis_index("y"), lax.axis_index("c")
    me_id = 4 * mx + 2 * my + mc

    def peer(r):
        return (1 - mx if r & 4 else mx, 1 - my if r & 2 else my, 1 - mc if r & 1 else mc)

    def peer_id(r):
        px, py, pc = peer(r)
        return 4 * px + 2 * py + pc

    def copies():
        return [pltpu.make_async_remote_copy(
            src_ref=g_refs[a].at[peer_id(r)], dst_ref=out_refs[a].at[me_id], send_sem=send_sems.at[a, r - 1],
            recv_sem=recv_sems.at[a, r - 1], device_id=peer(r), device_id_type=MESH_ID)
            for r in range(1, N_DEV) for a in range(n)]

    def arrivals():
        return [pltpu.make_async_remote_copy(
            src_ref=g_refs[a].at[peer_id(r)], dst_ref=out_refs[a].at[peer_id(r)], send_sem=send_sems.at[a, r - 1],
            recv_sem=recv_sems.at[a, r - 1], device_id=peer(r), device_id_type=MESH_ID)
            for r in range(1, N_DEV) for a in range(n)]

    def mine():
        return [pltpu.make_async_copy(g_refs[a].at[me_id], out_refs[a].at[me_id], local_sems.at[a]) for a in range(n)]

    def start():
        for cp in mine() + copies():
            cp.start()

    def finish():
        for cp in arrivals():
            cp.wait_recv()
        for cp in copies():
            cp.wait_send()
        for cp in mine():
            cp.wait()

    return start, finish


def _scatter_extras(blocks):
    n = len(blocks)
    anyspace = pl.BlockSpec(memory_space=pl.ANY)
    return ([anyspace] * n, [anyspace] * n, [jax.ShapeDtypeStruct(b.shape, b.dtype) for b in blocks],
            [pltpu.SemaphoreType.DMA((n, 7)), pltpu.SemaphoreType.DMA((n, 7)), pltpu.SemaphoreType.DMA((n,))])


def _sum_slots(x, *, name):
    _, r, c = x.shape
    tr = _tile(r, ROW_TILE)

    def body(x_ref, o_ref):
        acc = x_ref[0].astype(F32)
        for s in range(1, N_DEV):
            acc = acc + x_ref[s].astype(F32)
        o_ref[...] = acc

    return pl.pallas_call(
        body, name=name, grid=(r // tr,), in_specs=[pl.BlockSpec((N_DEV, tr, c), lambda i: (0, i, 0))],
        out_specs=pl.BlockSpec((tr, c), lambda i: (i, 0)), out_shape=jax.ShapeDtypeStruct((r, c), F32),
        compiler_params=_cparams("parallel"),
    )(x)


SB_PAIRS = SB_HEADS // 2
SB_PAIR_QBLOCK = 512
SB_PAIR_QBLOCK_FWD = 512
SB_SCALE = SB_DIM ** -0.5


def _suffix_ones(blk):
    ri = lax.broadcasted_iota(jnp.int32, (blk, blk), 0)
    ci = lax.broadcasted_iota(jnp.int32, (blk, blk), 1)
    return (ri >= ci).astype(BF16)


def _prefix_ones(blk):
    ri = lax.broadcasted_iota(jnp.int32, (blk, blk), 0)
    ci = lax.broadcasted_iota(jnp.int32, (blk, blk), 1)
    return (ri <= ci).astype(BF16)


def _sb_pair_blocks(t, pref=SB_PAIR_QBLOCK):
    bq = _tile(t, pref)
    return bq, _tile(bq, SB_KBLOCK)


def _sb_pair_specs(t, bq):
    base = C_SBQKV // 128
    return [pl.BlockSpec((bq, 128), lambda p, i: (i, base + p)),
            pl.BlockSpec((t, 128), lambda p, i: (0, base + SB_PAIRS + p)),
            pl.BlockSpec((t, 128), lambda p, i: (0, base + 2 * SB_PAIRS + p))]


def _halves(x, first):
    zero = jnp.zeros_like(x)
    return [jnp.where(first, x, zero), jnp.where(first, zero, x)]


def _sb_mask(qi, kb, bq, bk):
    t_idx = qi * bq + lax.broadcasted_iota(jnp.int32, (bq, bk), 0)
    s_idx = kb * bk + lax.broadcasted_iota(jnp.int32, (bq, bk), 1)
    return s_idx < t_idx


def _sb_pair_scores(qh, kblk, mask):
    zs = _dot_each(qh, [kblk, kblk], 1, 1)
    es = [jnp.exp(-jnp.abs(z)) for z in zs]
    sps = [jnp.maximum(z, 0.0) + jnp.log(1.0 + e) for z, e in zip(zs, es)]
    if mask is not None:
        sps = [jnp.where(mask, sp, 0.0) for sp in sps]
    return zs, es, sps


def _sb_atts(zs, csums, laters, mask):
    atts = [jnp.exp(z - c - l) for z, c, l in zip(zs, csums, laters)]
    return atts if mask is None else [jnp.where(mask, a, 0.0) for a in atts]


def _scaled_queries(q_ref, first):
    return _halves(q_ref[...] * jnp.asarray(SB_SCALE, q_ref.dtype), first)


def _running_sums(x_list, m):
    return [_dot(x, m) for x in x_list]


def _sb_pair_fwd(proj, shards=(), *, name):
    t = proj.shape[0]
    bq, bk = _sb_pair_blocks(t, SB_PAIR_QBLOCK_FWD)
    n = len(shards)
    nq = t // bq
    nsteps = SB_PAIRS * nq

    def body(q_ref, k_ref, v_ref, *rest):
        o_ref = rest[n]
        qi = pl.program_id(1)
        if n:
            step_no = pl.program_id(0) * nq + qi
            start, pass_on, finish = _gather_phases(rest[:n], rest[n + 1:2 * n + 1], *rest[2 * n + 1:])
            pl.when(step_no == 0)(start)
            pl.when(step_no == nsteps // 2)(pass_on)
        first = lax.broadcasted_iota(jnp.int32, (1, 128), 1) < SB_DIM
        qh = _scaled_queries(q_ref, first)
        suffix = _suffix_ones(bk)
        band = bq // bk
        nkb = (qi + 1) * band

        def make_step(masked):
            def step(it, carry):
                later0, later1, acc = carry
                kb = nkb - 1 - it
                rows = pl.ds(pl.multiple_of(kb * bk, bk), bk)
                mask = _sb_mask(qi, kb, bq, bk) if masked else None
                zs, _, sps = _sb_pair_scores(qh, k_ref[rows, :], mask)
                atts = _sb_atts(zs, _running_sums(sps, suffix), (later0, later1), mask)
                outs = _dot_each(atts, _halves(v_ref[rows, :], first))
                return (later0 + jnp.sum(sps[0], axis=1, keepdims=True),
                        later1 + jnp.sum(sps[1], axis=1, keepdims=True), acc + (outs[0] + outs[1]))
            return step

        zero = jnp.zeros((bq, 1), F32)
        carry = lax.fori_loop(0, band, make_step(True), (zero, zero, jnp.zeros((bq, 128), F32)))
        _, _, acc = lax.fori_loop(band, nkb, make_step(False), carry)
        o_ref[...] = acc.astype(o_ref.dtype)
        if n:
            pl.when(step_no == nsteps - 1)(finish)

    more_in, more_out, more_shapes, sems = _gather_extras(shards) if n else ([], [], [], [])
    res = pl.pallas_call(
        body, name=name, grid=(SB_PAIRS, nq), in_specs=_sb_pair_specs(t, bq) + more_in,
        out_specs=[pl.BlockSpec((bq, 128), lambda p, i: (i, p))] + more_out,
        out_shape=[jax.ShapeDtypeStruct((t, SB_W), BF16)] + more_shapes, scratch_shapes=sems,
        compiler_params=_cparams("arbitrary", "arbitrary"),
    )(proj, proj, proj, *shards)
    return res[0], list(res[1:])


def _sb_pair_bwd(proj, dy, blocks=(), *, name):
    t = proj.shape[0]
    bq, bk = _sb_pair_blocks(t)
    nq = t // bq
    n = len(blocks)

    def body(q_ref, k_ref, v_ref, do_ref, *rest):
        dq_ref, dk_ref, dv_ref = rest[n:n + 3]
        dl_keep, sig_keep, dk_acc, dv_acc = rest[2 * n + 3:2 * n + 7]
        qi = pl.program_id(1)
        if n:
            step_no = pl.program_id(0) * nq + qi
            start, finish = _scatter_phases(rest[:n], rest[n + 3:2 * n + 3], *rest[2 * n + 7:])
            pl.when(step_no == 0)(start)

        @pl.when(qi == 0)
        def _():
            dk_acc[...] = jnp.zeros_like(dk_acc)
            dv_acc[...] = jnp.zeros_like(dv_acc)

        first = lax.broadcasted_iota(jnp.int32, (1, 128), 1) < SB_DIM
        qh = _scaled_queries(q_ref, first)
        doh = _halves(do_ref[...], first)
        suffix = _suffix_ones(bk)
        prefix = _prefix_ones(bk)
        band = bq // bk
        nkb = (qi + 1) * band

        def make_back(masked):
            def back(it, carry):
                kb = nkb - 1 - it
                rows = pl.ds(pl.multiple_of(kb * bk, bk), bk)
                vblk = v_ref[rows, :]
                mask = _sb_mask(qi, kb, bq, bk) if masked else None
                zs, es, sps = _sb_pair_scores(qh, k_ref[rows, :], mask)
                atts = _sb_atts(zs, _running_sums(sps, suffix), carry, mask)
                dvs = _dot_each(atts, doh, 0, 0)
                datts = _dot_each(doh, [vblk, vblk], 1, 1)
                dv_acc[rows, :] += dvs[0] + dvs[1]
                for hh in range(2):
                    sig = jnp.where(zs[hh] >= 0, 1.0, es[hh]) * pl.reciprocal(1.0 + es[hh], approx=True)
                    dl_keep[hh, kb] = (atts[hh] * datts[hh]).astype(dl_keep.dtype)
                    sig_keep[hh, kb] = (sig if mask is None else jnp.where(mask, sig, 0.0)).astype(sig_keep.dtype)
                return tuple(l + jnp.sum(sp, axis=1, keepdims=True) for l, sp in zip(carry, sps))
            return back

        zero = jnp.zeros((bq, 1), F32)
        lax.fori_loop(band, nkb, make_back(False), lax.fori_loop(0, band, make_back(True), (zero, zero)))

        def forth(kb, carry):
            before0, before1, dq = carry
            rows = pl.ds(pl.multiple_of(kb * bk, bk), bk)
            kept = [dl_keep[hh, kb] for hh in range(2)]
            sums = _running_sums(kept, prefix)
            dls = [x.astype(F32) for x in kept]
            dzs = [dl - sig_keep[hh, kb].astype(F32) * (b + s)
                   for hh, (dl, b, s) in enumerate(zip(dls, (before0, before1), sums))]
            dks = _dot_each(dzs, qh, 0, 0)
            dqs = _dot_each(dzs, _halves(k_ref[rows, :], first))
            dk_acc[rows, :] += dks[0] + dks[1]
            return (before0 + jnp.sum(dls[0], axis=1, keepdims=True), before1 + jnp.sum(dls[1], axis=1, keepdims=True),
                    dq + (dqs[0] + dqs[1]))

        _, _, dq = lax.fori_loop(0, nkb, forth, (zero, zero, jnp.zeros((bq, 128), F32)))
        dq_ref[...] = (dq * SB_SCALE).astype(dq_ref.dtype)

        @pl.when(qi == nq - 1)
        def _():
            dk_ref[...] = dk_acc[...].astype(dk_ref.dtype)
            dv_ref[...] = dv_acc[...].astype(dv_ref.dtype)

        if n:
            pl.when(step_no == SB_PAIRS * nq - 1)(finish)

    qspec = pl.BlockSpec((bq, 128), lambda p, i: (i, p))
    kvspec = pl.BlockSpec((t, 128), lambda p, i: (0, p))
    shape = jax.ShapeDtypeStruct((t, SB_W), BF16)
    more_in, more_out, more_shapes, sems = _scatter_extras(blocks) if n else ([], [], [], [])
    res = pl.pallas_call(
        body, name=name, grid=(SB_PAIRS, nq), in_specs=_sb_pair_specs(t, bq) + [qspec] + more_in,
        out_specs=[qspec, kvspec, kvspec] + more_out, out_shape=[shape] * 3 + more_shapes,
        scratch_shapes=[pltpu.VMEM((2, t // bk, bq, bk), BF16), pltpu.VMEM((2, t // bk, bq, bk), BF16),
                        pltpu.VMEM((t, 128), F32), pltpu.VMEM((t, 128), F32)] + sems,
        compiler_params=_cparams("arbitrary", "arbitrary"),
    )(proj, proj, proj, dy, *blocks)
    return res[0], res[1], res[2], list(res[3:])


def _gate_specs(tm):
    return [pl.BlockSpec((tm, D_MODEL), lambda i, b=b: (i, C_GATES // D_MODEL + b)) for b in range(3)]


def _merge_fwd(p, proj, *, name):
    t = proj.shape[0]
    tm = _tile(t, ROW_TILE)

    def body(p0, p1, p2, g0, g1, g2, o_ref):
        acc = jnp.zeros((tm, D_MODEL), F32)
        for pr, gr in ((p0, g0), (p1, g1), (p2, g2)):
            acc = acc + _sigmoid(gr[...].astype(F32)) * pr[...].astype(F32)
        o_ref[...] = acc.astype(o_ref.dtype)

    row = pl.BlockSpec((tm, D_MODEL), lambda i: (i, 0))
    return pl.pallas_call(
        body, name=name, grid=(t // tm,), in_specs=[row] * 3 + _gate_specs(tm), out_specs=row,
        out_shape=jax.ShapeDtypeStruct((t, D_MODEL), BF16), compiler_params=_cparams("parallel"),
    )(*p, proj, proj, proj)


def _merge_bwd(p, proj, dmerged, *, name):
    t = proj.shape[0]
    tm = _tile(t, ROW_TILE)

    def body(p0, p1, p2, g0, g1, g2, dm_ref, dp0, dp1, dp2, dg_ref):
        dm = dm_ref[...].astype(F32)
        for b, (pr, gr, dpr) in enumerate(((p0, g0, dp0), (p1, g1, dp1), (p2, g2, dp2))):
            s = _sigmoid(gr[...].astype(F32))
            dpr[...] = (dm * s).astype(dpr.dtype)
            dg_ref[:, b * D_MODEL:(b + 1) * D_MODEL] = (dm * pr[...].astype(F32) * s * (1.0 - s)).astype(dg_ref.dtype)

    row = pl.BlockSpec((tm, D_MODEL), lambda i: (i, 0))
    res = pl.pallas_call(
        body, name=name, grid=(t // tm,), in_specs=[row] * 3 + _gate_specs(tm) + [row],
        out_specs=[row] * 3 + [pl.BlockSpec((tm, 3 * D_MODEL), lambda i: (i, 0))],
        out_shape=[jax.ShapeDtypeStruct((t, D_MODEL), BF16)] * 3 + [jax.ShapeDtypeStruct((t, 3 * D_MODEL), BF16)],
        compiler_params=_cparams("parallel"),
    )(*p, proj, proj, proj, dmerged)
    return res[:3], res[3]


def _loss_head(y, target, *, name):
    t, d = y.shape
    tm = _tile(t, ROW_TILE)

    def body(y_ref, t_ref, dy_ref, l_ref):
        err = y_ref[...] - t_ref[...]
        dy_ref[...] = err * (1.0 / d)
        part = jnp.sum(err * err, axis=0, keepdims=True) * (0.5 / d)

        @pl.when(pl.program_id(0) == 0)
        def _():
            l_ref[...] = part

        @pl.when(pl.program_id(0) > 0)
        def _():
            l_ref[...] += part

    row = pl.BlockSpec((tm, d), lambda i: (i, 0))
    vec = pl.BlockSpec((1, d), lambda i: (0, 0))
    return pl.pallas_call(
        body, name=name, grid=(t // tm,), in_specs=[row, row], out_specs=[row, vec],
        out_shape=[jax.ShapeDtypeStruct((t, d), F32), jax.ShapeDtypeStruct((1, d), F32)],
        compiler_params=_cparams("arbitrary"),
    )(y, target)


def _adamw(w, g, m, v, *, name):
    r, c = w.shape
    tr = r if r * c * 4 <= 2 ** 21 else max(8, (2 ** 21 // (c * 4)) // 8 * 8)
    while r % tr:
        tr -= 8
    c1 = 1.0 - ADAM_B1 ** ADAM_STEP
    c2 = 1.0 - ADAM_B2 ** ADAM_STEP

    def body(w_ref, g_ref, m_ref, v_ref, d_ref, nm_ref, nv_ref):
        gv = g_ref[...]
        nm = ADAM_B1 * m_ref[...] + (1.0 - ADAM_B1) * gv
        nv = ADAM_B2 * v_ref[...] + (1.0 - ADAM_B2) * (gv * gv)
        nm_ref[...] = nm
        nv_ref[...] = nv
        d_ref[...] = -ADAM_LR * ((nm / c1) / (jnp.sqrt(nv / c2) + ADAM_EPS) + ADAM_WD * w_ref[...])

    spec = pl.BlockSpec((tr, c), lambda i: (i, 0))
    return pl.pallas_call(
        body, name=name, grid=(r // tr,), in_specs=[spec] * 4, out_specs=[spec] * 3,
        out_shape=[jax.ShapeDtypeStruct((r, c), F32)] * 3, compiler_params=_cparams("parallel"),
    )(w, g, m, v)


def _all_gather(xs, *, name):
    n = len(xs)

    def body(*refs):
        start, pass_on, finish = _gather_phases(refs[:n], refs[n:2 * n], *refs[2 * n:])
        start()
        pass_on()
        finish()

    more_in, more_out, more_shapes, sems = _gather_extras(xs)
    return pl.pallas_call(body, name=name, in_specs=more_in, out_specs=more_out, out_shape=more_shapes,
                          scratch_shapes=sems)(*xs)


def _exchange_sibling(gs, *, name):
    n = len(gs)

    def body(*refs):
        g_refs, out_refs = refs[:n], refs[n:2 * n]
        send_sems, recv_sems = refs[2 * n:]
        mx, my, mc = lax.axis_index("x"), lax.axis_index("y"), lax.axis_index("c")
        sibling = (mx, my, 1 - mc)
        copies = []
        for a in range(n):
            for px in range(2):
                for py in range(2):
                    kk = 2 * px + py
                    copies.append(pltpu.make_async_remote_copy(
                        src_ref=g_refs[a].at[4 * px + 2 * py + (1 - mc)], dst_ref=out_refs[a].at[kk],
                        send_sem=send_sems.at[a, kk], recv_sem=recv_sems.at[a, kk], device_id=sibling,
                        device_id_type=MESH_ID))
        for cp in copies:
            cp.start()
        for cp in copies:
            cp.wait_recv()
        for cp in copies:
            cp.wait_send()

    anyspace = pl.BlockSpec(memory_space=pl.ANY)
    return pl.pallas_call(
        body, name=name, in_specs=[anyspace] * n, out_specs=[anyspace] * n,
        out_shape=[jax.ShapeDtypeStruct((4,) + g.shape[1:], g.dtype) for g in gs],
        scratch_shapes=[pltpu.SemaphoreType.DMA((n, 4)), pltpu.SemaphoreType.DMA((n, 4))],
    )(*gs)


def _pair_sum(g, got, *, name):
    _, r, c = g.shape
    tr = _tile(r, ROW_TILE)

    def body(core_ref, a_ref, b_ref, o_ref):
        del core_ref
        o_ref[...] = (a_ref[...].astype(F32) + b_ref[...].astype(F32)).astype(o_ref.dtype)

    grid_spec = pltpu.PrefetchScalarGridSpec(
        num_scalar_prefetch=1, grid=(4, r // tr),
        in_specs=[pl.BlockSpec((1, tr, c), lambda kk, i, core: (2 * kk + core[0], i, 0)),
                  pl.BlockSpec((1, tr, c), lambda kk, i, core: (kk, i, 0))],
        out_specs=pl.BlockSpec((1, tr, c), lambda kk, i, core: (kk, i, 0)))
    return pl.pallas_call(
        body, name=name, grid_spec=grid_spec, out_shape=jax.ShapeDtypeStruct((4, r, c), g.dtype),
        compiler_params=_cparams("parallel", "parallel"),
    )(lax.axis_index("c").astype(jnp.int32).reshape(1), g, got)


def _exchange_chips(parts, *, name):
    n = len(parts)

    def body(*refs):
        p_refs, out_refs = refs[:n], refs[n:2 * n]
        send_sems, recv_sems = refs[2 * n:]
        mx, my, mc = lax.axis_index("x"), lax.axis_index("y"), lax.axis_index("c")
        chips = [(1 - mx, my), (mx, 1 - my), (1 - mx, 1 - my)]
        copies = [pltpu.make_async_remote_copy(
            src_ref=p_refs[a].at[2 * px + py], dst_ref=out_refs[a].at[j], send_sem=send_sems.at[a, j],
            recv_sem=recv_sems.at[a, j], device_id=(px, py, mc), device_id_type=MESH_ID)
            for j, (px, py) in enumerate(chips) for a in range(n)]
        for cp in copies:
            cp.start()
        for cp in copies:
            cp.wait_recv()
        for cp in copies:
            cp.wait_send()

    anyspace = pl.BlockSpec(memory_space=pl.ANY)
    return pl.pallas_call(
        body, name=name, in_specs=[anyspace] * n, out_specs=[anyspace] * n,
        out_shape=[jax.ShapeDtypeStruct((3,) + p.shape[1:], p.dtype) for p in parts],
        scratch_shapes=[pltpu.SemaphoreType.DMA((n, 3)), pltpu.SemaphoreType.DMA((n, 3))],
    )(*parts)


def _final_sum(part, got, *, name):
    _, r, c = part.shape
    tr = _tile(r, ROW_TILE)

    def body(chip_ref, a_ref, b_ref, o_ref):
        del chip_ref
        acc = a_ref[0].astype(F32)
        for j in range(3):
            acc = acc + b_ref[j].astype(F32)
        o_ref[...] = acc

    grid_spec = pltpu.PrefetchScalarGridSpec(
        num_scalar_prefetch=1, grid=(r // tr,),
        in_specs=[pl.BlockSpec((1, tr, c), lambda i, chip: (chip[0], i, 0)),
                  pl.BlockSpec((3, tr, c), lambda i, chip: (0, i, 0))],
        out_specs=pl.BlockSpec((tr, c), lambda i, chip: (i, 0)))
    chip = (2 * lax.axis_index("x") + lax.axis_index("y")).astype(jnp.int32).reshape(1)
    return pl.pallas_call(
        body, name=name, grid_spec=grid_spec, out_shape=jax.ShapeDtypeStruct((r, c), F32),
        compiler_params=_cparams("parallel"),
    )(chip, part, got)


def _sum_devices(x, *, name):
    _, r, c = x.shape

    def body(x_ref, o_ref):
        acc = x_ref[0]
        for j in range(1, N_DEV):
            acc = acc + x_ref[j]
        o_ref[...] = acc

    return pl.pallas_call(body, name=name, out_shape=jax.ShapeDtypeStruct((r, c), F32),
                          compiler_params=_cparams())(x)


BIG = ("w_in", "w_branch", "w_out", "w_ff1", "w_ff2")
BIG_AXIS = {"w_in": 2, "w_branch": 3, "w_out": 1, "w_ff1": 2, "w_ff2": 1}


def _to_global(blocks, axis):
    moved = jnp.moveaxis(blocks, 0, axis)
    shp = moved.shape
    return moved.reshape(shp[:axis] + (shp[axis] * shp[axis + 1],) + shp[axis + 2:])


def _to_blocks(full, axis):
    shp = full.shape
    split = full.reshape(shp[:axis] + (N_DEV, shp[axis] // N_DEV) + shp[axis + 1:])
    return jnp.moveaxis(split, axis, 0)


def _rows(blocks):
    return blocks.reshape(blocks.shape[0], -1, blocks.shape[-1])


def _vec_rows(n):
    return -(-n // 128 // 8) * 8


def _pack_vec(parts):
    flat = jnp.concatenate([p.reshape(-1).astype(F32) for p in parts])
    rows = _vec_rows(flat.shape[0])
    return jnp.pad(flat, (0, rows * 128 - flat.shape[0])).reshape(rows, 128)


def _unpack_vec(flat, shapes):
    lead = flat.shape[:-2]
    flat = flat.reshape(lead + (-1,))
    out, off = [], 0
    for s in shapes:
        n = 1
        for dim in s:
            n *= dim
        out.append(flat[..., off:off + n].reshape(lead + tuple(s)))
        off += n
    return out


def _relu2_epilogue(r):
    a = jnp.maximum(r, 0.0)
    return r, a * a


def _add_epilogue(r, other):
    return (r + other,)


def _relu2_bwd_epilogue(r, a):
    return (r * 2.0 * jnp.maximum(a.astype(F32), 0.0),)


def _layer_fwd(x, p, shards):
    h = _norm_fwd(x, p["norm_mix_pre"], out_dtype=BF16, name="norm_pre_fwd")
    proj = _mm(h, p["w_main"], name="mm_in")
    ab = _mm(h, p["w_ab"], out_dtypes=(F32,), name="mm_ab")
    qkv = _gdn_pre_fwd(proj, p["conv_qkv_w"], name="gdn_pre_fwd")
    a_log, dt_bias = p["gdn_a_log"].reshape(1, GDN_HEADS), p["gdn_dt_bias"].reshape(1, GDN_HEADS)
    u, w, qd, kd, aqk, gl = _gdn_local_fwd(qkv, ab, a_log, dt_bias, name="gdn_local_fwd")
    o_gdn, states = _gdn_scan_fwd(u, w, qd, kd, aqk, gl, name="gdn_scan_fwd")
    y_a = _gdn_post_fwd(o_gdn, proj, p["gdn_norm_w"], name="gdn_post_fwd")
    y_b, gathered = _sb_pair_fwd(proj, shards, name="sb_fwd")
    p = dict(p)
    p.update({k: _to_global(blk, BIG_AXIS[k] - 1) for k, blk in zip(BIG[1:], gathered)})
    y_c = _sc_fwd(proj, p["conv_sc_w"], name="sc_fwd")
    ys = (y_a, y_b, y_c)
    ps = tuple(_mm(ys[b], p["w_branch"][b], name="mm_branch") for b in range(3))
    merged = _merge_fwd(ps, proj, name="merge_fwd")
    mo = _mm(merged, p["w_out"], out_dtypes=(F32,), name="mm_out")
    x1 = _norm_fwd(mo, p["norm_mix_post"], x, out_dtype=F32, name="norm_post_fwd")
    h2 = _norm_fwd(x1, p["norm_ffn_pre"], out_dtype=BF16, name="norm_pre_fwd")
    a1, r1 = _mm(h2, p["w_ff1"], out_dtypes=(BF16, BF16), epi=_relu2_epilogue, name="mm_ff1")
    f = _mm(r1, p["w_ff2"], out_dtypes=(F32,), name="mm_ff2")
    x2 = _norm_fwd(f, p["norm_ffn_post"], x1, out_dtype=F32, name="norm_post_fwd")
    saved = dict(x=x, h=h, proj=proj, ab=ab, qkv=qkv, u=u, w=w, qd=qd, kd=kd, aqk=aqk, gl=gl, o_gdn=o_gdn,
                 states=states, ys=ys, ps=ps, merged=merged, mo=mo, x1=x1, h2=h2,
                 a1=a1, r1=r1, f=f)
    return x2, saved, p, gathered[len(BIG) - 1:]


def _layer_bwd(dx2, p, s, above=()):
    g = {}
    df, g["norm_ffn_post"] = _norm_bwd(s["f"], p["norm_ffn_post"], dx2, out_dtype=BF16, name="norm_bwd_b")
    da1 = _mm(df, p["w_ff2"], tb=True, epi=_relu2_bwd_epilogue, extras=(s["a1"],), name="mm_ff2_dx")
    g["w_ff2"] = _mm(s["r1"], df, ta=True, name="mm_ff2_dw")
    g["w_ff1"] = _mm(s["h2"], da1, ta=True, name="mm_ff1_dw")
    dh2 = _mm(da1, p["w_ff1"], tb=True, out_dtypes=(F32,), name="mm_ff1_dx")
    dx1, g["norm_ffn_pre"] = _norm_bwd(s["x1"], p["norm_ffn_pre"], dh2, dx2, out_dtype=F32, name="norm_bwd_f")
    dmo, g["norm_mix_post"] = _norm_bwd(s["mo"], p["norm_mix_post"], dx1, out_dtype=BF16, name="norm_bwd_b")
    dmerged = _mm(dmo, p["w_out"], tb=True, name="mm_out_dx")
    g["w_out"] = _mm(s["merged"], dmo, ta=True, name="mm_out_dw")
    dps, dgates = _merge_bwd(s["ps"], s["proj"], dmerged, name="merge_bwd")
    dys = [_mm(dps[b], p["w_branch"][b], tb=True, name="mm_branch_dx") for b in range(3)]
    g["w_branch"] = jnp.stack([_mm(s["ys"][b], dps[b], ta=True, name="mm_branch_dw") for b in range(3)])
    dscx, dscb, dscc, g["conv_sc_w"] = _sc_bwd(s["proj"], p["conv_sc_w"], dys[2], name="sc_bwd")
    own = [_rows(_to_blocks(g[k], BIG_AXIS[k] - 1)) for k in BIG[1:]]
    dsq, dsk, dsv, received = _sb_pair_bwd(s["proj"], dys[1], own + list(above), name="sb_bwd")
    a_log, dt_bias = p["gdn_a_log"].reshape(1, GDN_HEADS), p["gdn_dt_bias"].reshape(1, GDN_HEADS)
    do_gdn, dggate, g["gdn_norm_w"] = _gdn_post_bwd(s["o_gdn"], s["proj"], p["gdn_norm_w"], dys[0], name="gdn_post_bwd")
    du, dw, dqd, dkd, daqk, dgl = _gdn_scan_bwd(s["u"], s["w"], s["qd"], s["kd"], s["aqk"], s["gl"], s["states"],
                                                do_gdn, name="gdn_scan_bwd")
    dq, dk, dv, dab_h, dsc = _gdn_local_bwd(s["qkv"], s["ab"], a_log, dt_bias, du, dw, dqd, dkd, daqk, dgl,
                                            name="gdn_local_bwd")
    dsc = jnp.sum(dsc, axis=(1, 2))
    g["gdn_a_log"], g["gdn_dt_bias"] = dsc[:, 0], dsc[:, 1]
    dqkv = jnp.concatenate([dq, dk, dv], axis=0)
    dgqkv, g["conv_qkv_w"] = _gdn_pre_bwd(s["proj"], p["conv_qkv_w"], dqkv, name="gdn_pre_bwd")
    dab = jnp.sum(dab_h, axis=0).astype(BF16)
    dproj = jnp.concatenate([dgqkv, dggate, dsq, dsk, dsv, dscx, dscb, dscc, dgates], axis=1)
    g["w_main"] = _mm(s["h"], dproj, ta=True, name="mm_in_dw")
    g["w_ab"] = _mm(s["h"], dab, ta=True, out_dtypes=(F32,), name="mm_ab_dw")
    dh_ab = _mm(dab, p["w_ab"], tb=True, out_dtypes=(F32,), name="mm_ab_dx")
    dh = _mm(dproj, p["w_main"], tb=True, out_dtypes=(F32,), epi=_add_epilogue, extras=(dh_ab,), name="mm_in_dx")
    dx, g["norm_mix_pre"] = _norm_bwd(s["x"], p["norm_mix_pre"], dh, dx1, out_dtype=F32, name="norm_bwd_f")
    return dx, g, received


NORMS = ("norm_mix_pre", "norm_mix_post", "norm_ffn_pre", "norm_ffn_post")
SMALL = NORMS + ("gdn_a_log", "gdn_dt_bias", "gdn_norm_w")
CONVS = ("conv_qkv_w", "conv_sc_w")
AB_LO = 2048


def _split_w_in(w_in):
    main = jnp.concatenate([w_in[..., :AB_LO], w_in[..., AB_LO + 2 * GDN_HEADS:]], axis=-1)
    ab = w_in[..., AB_LO:AB_LO + 2 * GDN_HEADS]
    pad = [(0, 0)] * (ab.ndim - 1) + [(0, AB_W - 2 * GDN_HEADS)]
    return main, jnp.pad(ab, pad)


def _join_w_in(main, ab):
    return jnp.concatenate([main[..., :AB_LO], ab[..., :2 * GDN_HEADS].astype(main.dtype), main[..., AB_LO:]], axis=-1)


def kernel(x, norm_mix_pre, w_in, conv_qkv_w, gdn_a_log, gdn_dt_bias, gdn_norm_w, conv_sc_w, w_branch, w_out, norm_mix_post, norm_ffn_pre, w_ff1, w_ff2, norm_ffn_post, loss_target, m_norm_mix_pre, m_w_in, m_conv_qkv_w, m_gdn_a_log, m_gdn_dt_bias, m_gdn_norm_w, m_conv_sc_w, m_w_branch, m_w_out, m_norm_mix_post, m_norm_ffn_pre, m_w_ff1, m_w_ff2, m_norm_ffn_post, v_norm_mix_pre, v_w_in, v_conv_qkv_w, v_gdn_a_log, v_gdn_dt_bias, v_gdn_norm_w, v_conv_sc_w, v_w_branch, v_w_out, v_norm_mix_post, v_norm_ffn_pre, v_w_ff1, v_w_ff2, v_norm_ffn_post):
    names = ("norm_mix_pre", "w_in", "conv_qkv_w", "gdn_a_log", "gdn_dt_bias", "gdn_norm_w", "conv_sc_w", "w_branch",
             "w_out", "norm_mix_post", "norm_ffn_pre", "w_ff1", "w_ff2", "norm_ffn_post")
    w = dict(zip(names, (norm_mix_pre, w_in, conv_qkv_w, gdn_a_log, gdn_dt_bias, gdn_norm_w, conv_sc_w, w_branch,
                         w_out, norm_mix_post, norm_ffn_pre, w_ff1, w_ff2, norm_ffn_post)))
    m = dict(zip(names, (m_norm_mix_pre, m_w_in, m_conv_qkv_w, m_gdn_a_log, m_gdn_dt_bias, m_gdn_norm_w, m_conv_sc_w,
                         m_w_branch, m_w_out, m_norm_mix_post, m_norm_ffn_pre, m_w_ff1, m_w_ff2, m_norm_ffn_post)))
    v = dict(zip(names, (v_norm_mix_pre, v_w_in, v_conv_qkv_w, v_gdn_a_log, v_gdn_dt_bias, v_gdn_norm_w, v_conv_sc_w,
                         v_w_branch, v_w_out, v_norm_mix_post, v_norm_ffn_pre, v_w_ff1, v_w_ff2, v_norm_ffn_post)))
    me = 4 * lax.axis_index("x") + 2 * lax.axis_index("y") + lax.axis_index("c")

    conv_shapes = [w[k].shape for k in CONVS]
    conv_all, = _all_gather([_pack_vec([w[k] for k in CONVS])], name="gather_small")
    convs = {k: _to_global(blk, 2) for k, blk in zip(CONVS, _unpack_vec(conv_all, conv_shapes))}
    shards = [[w[k][l].astype(BF16) for k in BIG] for l in range(DEPTH)]
    n_big = len(BIG)

    xs = x[0]
    w_in_blocks, = _all_gather(shards[0][:1], name="gather_weights")
    layers, saved = [], []
    for l in range(DEPTH):
        p = {k: convs[k][l] for k in CONVS}
        p.update({k: w[k][l] for k in SMALL})
        p["w_main"], p["w_ab"] = _split_w_in(_to_global(w_in_blocks, BIG_AXIS["w_in"] - 1))
        riding = shards[l][1:] + (shards[l + 1][:1] if l + 1 < DEPTH else [])
        xs, s, p, rest = _layer_fwd(xs, p, riding)
        layers.append(p)
        saved.append(s)
        w_in_blocks = rest[0] if rest else None
    dy, loss_lanes = _loss_head(xs, loss_target[0], name="loss_head")

    grads, big_sums, above = [None] * DEPTH, [[None] * n_big for _ in range(DEPTH)], []
    for l in reversed(range(DEPTH)):
        dy, g, received = _layer_bwd(dy, layers[l], saved[l], above)
        sums = [_sum_slots(r, name="rs_sum_slots") for r in received]
        big_sums[l][1:] = sums[:n_big - 1]
        if above:
            big_sums[l + 1][0] = sums[n_big - 1]
        g["w_in"] = _join_w_in(g.pop("w_main"), g.pop("w_ab"))
        above = [_rows(_to_blocks(g["w_in"], BIG_AXIS["w_in"] - 1))]
        grads[l] = g
    got = _exchange_sibling(above, name="rs_sibling")
    parts = [_pair_sum(b, r, name="rs_pair_sum") for b, r in zip(above, got)]
    got2 = _exchange_chips(parts, name="rs_chips")
    big_sums[0][0] = _final_sum(parts[0], got2[0], name="rs_final_sum")
    gsum = {k: jnp.stack([big_sums[l][i] for l in range(DEPTH)]).reshape(w[k].shape) for i, k in enumerate(BIG)}
    stack = {k: jnp.stack([g[k] for g in grads]) for k in SMALL + CONVS}

    small_parts = [stack[k] for k in SMALL + CONVS] + [jnp.sum(loss_lanes).reshape(1)]
    small_shapes = [stack[k].shape for k in SMALL + CONVS] + [(1,)]
    summed = _sum_devices(_all_gather([_pack_vec(small_parts)], name="gather_small_grads")[0], name="sum_small")
    small = _unpack_vec(summed, small_shapes)
    loss = small[-1][0]
    for k, val in zip(SMALL + CONVS, small[:-1]):
        gsum[k] = val
    for k in CONVS:
        per = gsum[k].shape[2] // N_DEV
        gsum[k] = lax.dynamic_slice_in_dim(gsum[k], me * per, per, axis=2)

    delta, new_m, new_v = {}, {}, {}
    for k in names:
        shp = w[k].shape
        two_d = (-1, shp[-1]) if len(shp) > 1 else (1, -1)
        d_, m_, v_ = _adamw(w[k].reshape(two_d), gsum[k].reshape(two_d), m[k].reshape(two_d), v[k].reshape(two_d),
                            name="adamw")
        delta[k], new_m[k], new_v[k] = d_.reshape(shp), m_.reshape(shp), v_.reshape(shp)

    return (loss, dy[None], *[gsum[k].reshape(w[k].shape) for k in names], *[delta[k] for k in names], *[new_m[k] for k in names],
            *[new_v[k] for k in names])
```

```python
import jax
import jax.numpy as jnp
from jax import lax
from jax.experimental import pallas as pl
from jax.experimental.pallas import tpu as pltpu

F32, BF16 = jnp.float32, jnp.bfloat16
MESH_ID = pl.DeviceIdType.MESH

N_DEV = 8
DEPTH = 4
D_MODEL = 1024
D_FF = 4096
EPS = 1e-6
GDN_HEADS, GDN_DIM, GDN_CONV = 4, 128, 4
GDN_W = GDN_HEADS * GDN_DIM
CHUNK = 64
SB_HEADS, SB_DIM = 8, 64
SB_W = SB_HEADS * SB_DIM
SB_KBLOCK = 256
SC_W, SC_CONV = 512, 3
C_GQKV, C_GGATE, C_SBQKV, C_SCX, C_SCB, C_SCC, C_GATES = 0, 1536, 2048, 3584, 4096, 4608, 5120
AB_W = 128

ADAM_LR, ADAM_B1, ADAM_B2, ADAM_EPS, ADAM_WD, ADAM_STEP = 0.001, 0.9, 0.999, 1e-08, 0.01, 10

VMEM_LIMIT = 48 * 2 ** 20


def _cparams(*sem):
    return pltpu.CompilerParams(dimension_semantics=sem or None, vmem_limit_bytes=VMEM_LIMIT)


def _tile(n, pref):
    if n <= pref:
        return n
    t = pref
    while n % t:
        t -= 128
    assert t > 0
    return t


def _dot(a, b, ca=1, cb=0):
    return lax.dot_general(a.astype(BF16), b.astype(BF16), (((ca,), (cb,)), ((), ())), preferred_element_type=F32)


def _split2(x):
    hi = x.astype(BF16)
    return hi, (x - hi.astype(F32)).astype(BF16)


def _sigmoid(z):
    e = jnp.exp(-jnp.abs(z))
    return jnp.where(z >= 0, 1.0, e) / (1.0 + e)


def _softplus(z):
    return jnp.maximum(z, 0.0) + jnp.log(1.0 + jnp.exp(-jnp.abs(z)))


def _mm(a, b, *, name, ta=False, tb=False, out_dtypes=(BF16,), epi=None, extras=()):
    assert a.dtype == BF16 and b.dtype == BF16
    m, k = (a.shape[1], a.shape[0]) if ta else a.shape
    n = b.shape[0] if tb else b.shape[1]
    assert (b.shape[1] if tb else b.shape[0]) == k
    tm, tn, tk = _tile(m, 1024), _tile(n, 1024), _tile(k, 2048)
    nk = k // tk
    ca, cb = (0 if ta else 1), (1 if tb else 0)
    n_ex, n_out = len(extras), len(out_dtypes)

    def body(*refs):
        a_ref, b_ref = refs[0], refs[1]
        ex = refs[2:2 + n_ex]
        outs = refs[2 + n_ex:2 + n_ex + n_out]
        acc = refs[-1]
        kk = pl.program_id(2)
        part = lax.dot_general(a_ref[...], b_ref[...], (((ca,), (cb,)), ((), ())), preferred_element_type=F32)

        def finish(r):
            vals = (r,) if epi is None else epi(r, *[e[...] for e in ex])
            for o, v in zip(outs, vals):
                o[...] = v.astype(o.dtype)

        if nk == 1:
            finish(part)
        else:
            @pl.when(kk == 0)
            def _():
                acc[...] = part

            @pl.when(kk > 0)
            def _():
                acc[...] += part

            @pl.when(kk == nk - 1)
            def _():
                finish(acc[...])

    a_spec = pl.BlockSpec((tk, tm), lambda i, j, kk: (kk, i)) if ta else pl.BlockSpec((tm, tk), lambda i, j, kk: (i, kk))
    b_spec = pl.BlockSpec((tn, tk), lambda i, j, kk: (j, kk)) if tb else pl.BlockSpec((tk, tn), lambda i, j, kk: (kk, j))
    io_spec = pl.BlockSpec((tm, tn), lambda i, j, kk: (i, j))
    res = pl.pallas_call(
        body, name=name, grid=(m // tm, n // tn, nk),
        in_specs=[a_spec, b_spec] + [io_spec] * n_ex,
        out_specs=[io_spec] * n_out,
        out_shape=[jax.ShapeDtypeStruct((m, n), dt) for dt in out_dtypes],
        scratch_shapes=[pltpu.VMEM((tm, tn) if nk > 1 else (8, 128), F32)],
        compiler_params=_cparams("parallel", "parallel", "arbitrary"),
    )(a, b, *extras)
    return res[0] if n_out == 1 else res


ROW_TILE = 512


def _norm_fwd(y, w, res=None, *, out_dtype, name):
    t, d = y.shape
    tm = _tile(t, ROW_TILE)
    has_res = res is not None

    def body(*refs):
        y_ref, w_ref = refs[0], refs[1]
        o_ref = refs[-1]
        yv = y_ref[...]
        r = lax.rsqrt(jnp.mean(yv * yv, axis=-1, keepdims=True) + EPS)
        out = yv * r * w_ref[...]
        if has_res:
            out = out + refs[2][...]
        o_ref[...] = out.astype(o_ref.dtype)

    row = pl.BlockSpec((tm, d), lambda i: (i, 0))
    vec = pl.BlockSpec((1, d), lambda i: (0, 0))
    args = (y, w.reshape(1, d)) + ((res,) if has_res else ())
    return pl.pallas_call(
        body, name=name, grid=(t // tm,), in_specs=[row, vec] + [row] * has_res, out_specs=row,
        out_shape=jax.ShapeDtypeStruct((t, d), out_dtype), compiler_params=_cparams("parallel"),
    )(*args)


def _norm_bwd(y, w, dout, add=None, *, out_dtype, name):
    t, d = y.shape
    tm = _tile(t, ROW_TILE)
    has_add = add is not None

    def body(*refs):
        y_ref, w_ref, do_ref = refs[0], refs[1], refs[2]
        dy_ref, dw_ref = refs[-2], refs[-1]
        yv = y_ref[...]
        r = lax.rsqrt(jnp.mean(yv * yv, axis=-1, keepdims=True) + EPS)
        yh = yv * r
        dov = do_ref[...].astype(F32)
        gw = dov * w_ref[...]
        dy = r * (gw - yh * jnp.mean(gw * yh, axis=-1, keepdims=True))
        if has_add:
            dy = dy + refs[3][...]
        dy_ref[...] = dy.astype(dy_ref.dtype)
        part = jnp.sum(dov * yh, axis=0, keepdims=True)

        @pl.when(pl.program_id(0) == 0)
        def _():
            dw_ref[...] = part

        @pl.when(pl.program_id(0) > 0)
        def _():
            dw_ref[...] += part

    row = pl.BlockSpec((tm, d), lambda i: (i, 0))
    vec = pl.BlockSpec((1, d), lambda i: (0, 0))
    args = (y, w.reshape(1, d), dout) + ((add,) if has_add else ())
    return pl.pallas_call(
        body, name=name, grid=(t // tm,), in_specs=[row, vec, row] + [row] * has_add, out_specs=[row, vec],
        out_shape=[jax.ShapeDtypeStruct((t, d), out_dtype), jax.ShapeDtypeStruct((1, d), F32)],
        compiler_params=_cparams("arbitrary"),
    )(*args)


def _shift_down(u, s):
    if s == 0:
        return u
    rows = lax.broadcasted_iota(jnp.int32, u.shape, 0)
    return jnp.where(rows >= s, pltpu.roll(u, s, 0), 0.0)


def _shift_up(u, s):
    if s == 0:
        return u
    t = u.shape[0]
    rows = lax.broadcasted_iota(jnp.int32, u.shape, 0)
    return jnp.where(rows < t - s, pltpu.roll(u, t - s, 0), 0.0)


def _conv_fwd(u, w):
    kk = w.shape[0]
    out = u * w[kk - 1:kk, :]
    for i in range(kk - 1):
        out = out + _shift_down(u, kk - 1 - i) * w[i:i + 1, :]
    return out


def _conv_bwd(u, w, dc):
    kk = w.shape[0]
    du = dc * w[kk - 1:kk, :]
    dws = []
    for i in range(kk):
        s = kk - 1 - i
        if s:
            du = du + _shift_up(dc, s) * w[i:i + 1, :]
        dws.append(jnp.sum(dc * _shift_down(u, s), axis=0, keepdims=True))
    return du, dws


def _gdn_pre_math(x, w, slab):
    c = _conv_fwd(x, w)
    sig = _sigmoid(c)
    s = c * sig
    r = lax.rsqrt(jnp.sum(s * s, axis=-1, keepdims=True) + EPS)
    scale = jnp.where(slab < GDN_HEADS, GDN_DIM ** -0.5, 1.0)
    return c, sig, s, r, scale


def _gdn_pre_fwd(proj, conv_w, *, name):
    t = proj.shape[0]
    nslab = 3 * GDN_HEADS

    def body(x_ref, w_ref, o_ref):
        slab = pl.program_id(0)
        _, _, s, r, scale = _gdn_pre_math(x_ref[...].astype(F32), w_ref[...], slab)
        o_ref[0] = jnp.where(slab < 2 * GDN_HEADS, s * r * scale, s)

    return pl.pallas_call(
        body, name=name, grid=(nslab,),
        in_specs=[pl.BlockSpec((t, GDN_DIM), lambda j: (0, j)), pl.BlockSpec((GDN_CONV, GDN_DIM), lambda j: (0, j))],
        out_specs=pl.BlockSpec((1, t, GDN_DIM), lambda j: (j, 0, 0)),
        out_shape=jax.ShapeDtypeStruct((nslab, t, GDN_DIM), F32), compiler_params=_cparams("parallel"),
    )(proj, conv_w)


def _gdn_pre_bwd(proj, conv_w, dqkv, *, name):
    t = proj.shape[0]
    nslab = 3 * GDN_HEADS

    def body(x_ref, w_ref, d_ref, dx_ref, dw_ref):
        slab = pl.program_id(0)
        x = x_ref[...].astype(F32)
        w = w_ref[...]
        c, sig, s, r, scale = _gdn_pre_math(x, w, slab)
        dout = d_ref[0]
        yn = s * r
        dn = dout * scale
        ds_norm = r * (dn - yn * jnp.sum(dn * yn, axis=-1, keepdims=True))
        ds = jnp.where(slab < 2 * GDN_HEADS, ds_norm, dout)
        dc = ds * (sig + c * sig * (1.0 - sig))
        dx, dws = _conv_bwd(x, w, dc)
        dx_ref[...] = dx.astype(dx_ref.dtype)
        for i, dwi in enumerate(dws):
            dw_ref[i:i + 1, :] = dwi

    return pl.pallas_call(
        body, name=name, grid=(nslab,),
        in_specs=[pl.BlockSpec((t, GDN_DIM), lambda j: (0, j)), pl.BlockSpec((GDN_CONV, GDN_DIM), lambda j: (0, j)),
                  pl.BlockSpec((1, t, GDN_DIM), lambda j: (j, 0, 0))],
        out_specs=[pl.BlockSpec((t, GDN_DIM), lambda j: (0, j)), pl.BlockSpec((GDN_CONV, GDN_DIM), lambda j: (0, j))],
        out_shape=[jax.ShapeDtypeStruct((t, 3 * GDN_W), BF16), jax.ShapeDtypeStruct((GDN_CONV, 3 * GDN_W), F32)],
        compiler_params=_cparams("parallel"),
    )(proj, conv_w, dqkv)


def _sc_specs(t):
    def col(base):
        return pl.BlockSpec((t, 128), lambda j: (0, base // 128 + j))
    return [col(C_SCX), col(C_SCB), col(C_SCC), pl.BlockSpec((SC_CONV, 128), lambda j: (0, j))]


def _sc_fwd(proj, conv_w, *, name):
    t = proj.shape[0]

    def body(x_ref, b_ref, c_ref, w_ref, o_ref):
        u = c_ref[...].astype(F32) * x_ref[...].astype(F32)
        o_ref[...] = (b_ref[...].astype(F32) * _conv_fwd(u, w_ref[...])).astype(o_ref.dtype)

    return pl.pallas_call(
        body, name=name, grid=(SC_W // 128,), in_specs=_sc_specs(t),
        out_specs=pl.BlockSpec((t, 128), lambda j: (0, j)),
        out_shape=jax.ShapeDtypeStruct((t, SC_W), BF16), compiler_params=_cparams("parallel"),
    )(proj, proj, proj, conv_w)


def _sc_bwd(proj, conv_w, dy, *, name):
    t = proj.shape[0]
    nj = SC_W // 128

    def body(x_ref, b_ref, c_ref, w_ref, dy_ref, dx_ref, db_ref, dc_ref, dw_ref):
        x, b, c = x_ref[...].astype(F32), b_ref[...].astype(F32), c_ref[...].astype(F32)
        w = w_ref[...]
        u = c * x
        dyv = dy_ref[...].astype(F32)
        db_ref[...] = (dyv * _conv_fwd(u, w)).astype(db_ref.dtype)
        du, dws = _conv_bwd(u, w, dyv * b)
        dx_ref[...] = (du * c).astype(dx_ref.dtype)
        dc_ref[...] = (du * x).astype(dc_ref.dtype)
        for i, dwi in enumerate(dws):
            dw_ref[i:i + 1, :] = dwi

    return pl.pallas_call(
        body, name=name, grid=(nj,),
        in_specs=_sc_specs(t) + [pl.BlockSpec((t, 128), lambda j: (0, j))],
        out_specs=[pl.BlockSpec((t, 128), lambda j: (0, j))] * 3 + [pl.BlockSpec((SC_CONV, 128), lambda j: (0, j))],
        out_shape=[jax.ShapeDtypeStruct((t, SC_W), BF16)] * 3 + [jax.ShapeDtypeStruct((SC_CONV, SC_W), F32)],
        compiler_params=_cparams("parallel"),
    )(proj, proj, proj, conv_w, dy)


def _dot_each(a_list, b_list, ca=1, cb=0):
    return [_dot(a, b, ca, cb) for a, b in zip(a_list, b_list)]


def _dot3_each(a_list, b_list, ca=1, cb=0):
    sa = [_split2(a) for a in a_list]
    sb = [_split2(b) for b in b_list]
    prods = [(_dot(a1, b1, ca, cb), _dot(a1, b2, ca, cb), _dot(a2, b1, ca, cb)) for (a1, a2), (b1, b2) in zip(sa, sb)]
    return [x + (y + z) for x, y, z in prods]


def _split3(x):
    hi = x.astype(BF16)
    rest = x - hi.astype(F32)
    mid = rest.astype(BF16)
    return hi, mid, (rest - mid.astype(F32)).astype(BF16)


def _ones_dot_each(m, x_list, ca=1, cb=0):
    mb = m.astype(BF16)
    parts = [[_dot(mb, p, ca, cb) for p in _split3(x)] for x in x_list]
    return [p[0] + (p[1] + p[2]) for p in parts]


def _dot_ones_each(x_list, m, ca=1, cb=0):
    mb = m.astype(BF16)
    parts = [[_dot(p, mb, ca, cb) for p in _split3(x)] for x in x_list]
    return [p[0] + (p[1] + p[2]) for p in parts]


def _tri_inv_each(a_list):
    c = a_list[0].shape[0]
    ri = lax.broadcasted_iota(jnp.int32, (c, c), 0)
    ci = lax.broadcasted_iota(jnp.int32, (c, c), 1)
    eye = (ri == ci).astype(F32)
    blk = 8
    pws = [-jnp.where(ri // blk == ci // blk, a, 0.0) for a in a_list]
    invs = [eye + b for b in pws]
    for _ in range(2):
        pws = _dot_each(pws, pws)
        invs = [i + u for i, u in zip(invs, _dot_each(invs, pws))]
    while blk < c:
        sel = (ri // (2 * blk) == ci // (2 * blk)) & (ri // blk != ci // blk)
        offs = [jnp.where(sel, a, 0.0) for a in a_list]
        invs = [i - t for i, t in zip(invs, _dot_each(_dot_each(invs, offs), invs))]
        blk *= 2
    resid = [eye - x for x in _dot3_each([eye + a for a in a_list], invs)]
    return [i + t for i, t in zip(invs, _dot_each(invs, resid))]


def _gdn_chunks(qs, ks, vs, abs_, head, ea, dtb):
    c = qs[0].shape[0]
    lane = lax.broadcasted_iota(jnp.int32, abs_[0].shape, 1)
    a_s = [jnp.sum(jnp.where(lane == head, ab, 0.0), axis=1, keepdims=True) for ab in abs_]
    b_s = [jnp.sum(jnp.where(lane == GDN_HEADS + head, ab, 0.0), axis=1, keepdims=True) for ab in abs_]
    ri = lax.broadcasted_iota(jnp.int32, (c, c), 0)
    ci = lax.broadcasted_iota(jnp.int32, (c, c), 1)
    tri, strict = ri >= ci, ri > ci
    ltri = tri.astype(F32)
    beta = [_sigmoid(b) for b in b_s]
    sig_a = [_sigmoid(a + dtb) for a in a_s]
    g = [-ea * _softplus(a + dtb) for a in a_s]
    g_cc = [jnp.broadcast_to(x, (c, c)) for x in g]
    gi = _ones_dot_each(ltri, g_cc)
    gj = _dot_ones_each(g_cc, (ri <= ci).astype(F32), 0, 0)
    decay = [jnp.exp(jnp.where(tri, x - y, -1e30)) for x, y in zip(gi, gj)]
    gc = _ones_dot_each(ltri, [jnp.broadcast_to(x, (c, GDN_DIM)) for x in g])
    g_tot = [jnp.sum(x, axis=0, keepdims=True) for x in g]
    egc = [jnp.exp(x) for x in gc]
    ekd = [jnp.exp(t - x) for t, x in zip(g_tot, gc)]
    kb = [k * b for k, b in zip(ks, beta)]
    vb = [v * b for v, b in zip(vs, beta)]
    kbg = [x * e for x, e in zip(kb, egc)]
    mkk = _dot_each(kb, ks, 1, 1)
    a_kk = [jnp.where(strict, m * d, 0.0) for m, d in zip(mkk, decay)]
    tinv = _tri_inv_each(a_kk)
    u = _dot_each(tinv, vb)
    w = _dot_each(tinv, kbg)
    mqk = _dot_each(qs, ks, 1, 1)
    a_qk = [jnp.where(tri, m * d, 0.0) for m, d in zip(mqk, decay)]
    qd = [q * e for q, e in zip(qs, egc)]
    kd = [k * e for k, e in zip(ks, ekd)]
    return dict(beta=beta, sig_a=sig_a, g=g, decay=decay, egc=egc, ekd=ekd, g_tot=g_tot, kb=kb, vb=vb, kbg=kbg,
                a_kk=a_kk, tinv=tinv, u=u, w=w, a_qk=a_qk, qd=qd, kd=kd, tri=tri, strict=strict)


def _chunks_in_step(n):
    for cps in (8, 4, 2):
        if n % cps == 0:
            return cps
    return 1


def _gdn_local_specs(t, cps):
    rows = cps * CHUNK

    def slab(base):
        return pl.BlockSpec((1, rows, GDN_DIM), lambda h, n: (base + h, n, 0))
    smem = pl.BlockSpec(memory_space=pltpu.SMEM)
    return [slab(0), slab(GDN_HEADS), slab(2 * GDN_HEADS), pl.BlockSpec((rows, AB_W), lambda h, n: (n, 0)), smem, smem]


def _scalar_row(ref, head):
    return jnp.full((1, 1), ref[0, head], F32)


def _gdn_local_fwd(qkv, ab, a_log, dt_bias, *, name):
    t = qkv.shape[1]
    n = t // CHUNK
    cps = _chunks_in_step(n)
    rows = cps * CHUNK
    sls = [slice(j * CHUNK, (j + 1) * CHUNK) for j in range(cps)]

    def body(q_ref, k_ref, v_ref, ab_ref, al_ref, dt_ref, u_ref, w_ref, qd_ref, kd_ref, aqk_ref, gl_ref):
        head = pl.program_id(0)
        ea = jnp.exp(_scalar_row(al_ref, head))
        dtb = _scalar_row(dt_ref, head)
        r = _gdn_chunks([q_ref[0, sl, :] for sl in sls], [k_ref[0, sl, :] for sl in sls],
                        [v_ref[0, sl, :] for sl in sls], [ab_ref[sl, :] for sl in sls], head, ea, dtb)
        for j, sl in enumerate(sls):
            u_ref[0, sl, :] = r["u"][j]
            w_ref[0, sl, :] = r["w"][j]
            qd_ref[0, sl, :] = r["qd"][j]
            kd_ref[0, sl, :] = r["kd"][j]
            aqk_ref[0, sl, :] = r["a_qk"][j]
            gl_ref[0, j] = jnp.exp(jnp.broadcast_to(r["g_tot"][j], (1, GDN_DIM)))

    big = pl.BlockSpec((1, rows, GDN_DIM), lambda h, i: (h, i, 0))
    big_shape = jax.ShapeDtypeStruct((GDN_HEADS, t, GDN_DIM), F32)
    return pl.pallas_call(
        body, name=name, grid=(GDN_HEADS, n // cps), in_specs=_gdn_local_specs(t, cps),
        out_specs=[big] * 4 + [pl.BlockSpec((1, rows, CHUNK), lambda h, i: (h, i, 0)),
                               pl.BlockSpec((1, cps, 1, GDN_DIM), lambda h, i: (h, i, 0, 0))],
        out_shape=[big_shape] * 4 + [jax.ShapeDtypeStruct((GDN_HEADS, t, CHUNK), F32),
                                     jax.ShapeDtypeStruct((GDN_HEADS, n, 1, GDN_DIM), F32)],
        compiler_params=_cparams("parallel", "parallel"),
    )(qkv, qkv, qkv, ab, a_log, dt_bias)


def _gdn_local_bwd(qkv, ab, a_log, dt_bias, du, dw, dqd, dkd, daqk, dgl, *, name):
    t = qkv.shape[1]
    n = t // CHUNK
    cps = _chunks_in_step(n)
    rows = cps * CHUNK
    sls = [slice(j * CHUNK, (j + 1) * CHUNK) for j in range(cps)]

    def body(q_ref, k_ref, v_ref, ab_ref, al_ref, dt_ref, du_ref, dw_ref, dqd_ref, dkd_ref, daqk_ref, dgl_ref,
             dq_ref, dk_ref, dv_ref, dab_ref, dsc_ref):
        head = pl.program_id(0)
        ea = jnp.exp(_scalar_row(al_ref, head))
        dtb = _scalar_row(dt_ref, head)
        lane = lax.broadcasted_iota(jnp.int32, (CHUNK, AB_W), 1)
        lane1 = lax.broadcasted_iota(jnp.int32, (1, GDN_DIM), 1)
        ri = lax.broadcasted_iota(jnp.int32, (CHUNK, CHUNK), 0)
        ci = lax.broadcasted_iota(jnp.int32, (CHUNK, CHUNK), 1)
        utri = (ri <= ci).astype(F32)
        ones = jnp.ones((CHUNK, GDN_DIM), F32)
        qs, ks, vs = ([ref[0, sl, :] for sl in sls] for ref in (q_ref, k_ref, v_ref))
        r = _gdn_chunks(qs, ks, vs, [ab_ref[sl, :] for sl in sls], head, ea, dtb)
        duv, dwv, dqdv, dkdv = ([ref[0, sl, :] for sl in sls] for ref in (du_ref, dw_ref, dqd_ref, dkd_ref))
        d_aqk = [jnp.where(r["tri"], daqk_ref[0, sl, :], 0.0) for sl in sls]
        dvb = _dot_each(r["tinv"], duv, 0, 0)
        dkbg = _dot_each(r["tinv"], dwv, 0, 0)
        outer = [x + y for x, y in zip(_dot_each(dvb, r["u"], 1, 1), _dot_each(dkbg, r["w"], 1, 1))]
        d_akk = [-jnp.where(r["strict"], x, 0.0) for x in outer]
        e = [x * a + y * b for x, a, y, b in zip(d_akk, r["a_kk"], d_aqk, r["a_qk"])]
        dmkk = [x * d for x, d in zip(d_akk, r["decay"])]
        dmqk = [x * d for x, d in zip(d_aqk, r["decay"])]
        dkb = [x + y * eg for x, y, eg in zip(_dot_each(dmkk, ks), dkbg, r["egc"])]
        dk = [a + b + x * ek + y * bt for a, b, x, ek, y, bt in zip(
            _dot_each(dmkk, r["kb"], 0, 0), _dot_each(dmqk, qs, 0, 0), dkdv, r["ekd"], dkb, r["beta"])]
        dq = [a + x * eg for a, x, eg in zip(_dot_each(dmqk, ks), dqdv, r["egc"])]
        col_sums = _dot_ones_each(e, ones, 0, 0)
        acc_alog = jnp.zeros((1, 1), F32)
        acc_dtb = jnp.zeros((1, 1), F32)
        dgc_lanes, d_tots, dbetas = [], [], []
        for j in range(cps):
            dbetas.append(jnp.sum(dkb[j] * ks[j] + dvb[j] * vs[j], axis=1, keepdims=True))
            kd_term = jnp.sum(dkdv[j] * r["kd"][j], axis=1, keepdims=True)
            dgc = (jnp.sum(e[j], axis=1, keepdims=True)
                   + jnp.sum(dqdv[j] * r["qd"][j] + dkbg[j] * r["kbg"][j], axis=1, keepdims=True) - kd_term)
            dgc_lanes.append(jnp.broadcast_to(dgc, (CHUNK, GDN_DIM)) - col_sums[j])
            dgl_tot = jnp.sum(dgl_ref[0, j], axis=1, keepdims=True) * jnp.exp(r["g_tot"][j])
            d_tots.append(jnp.sum(kd_term, axis=0, keepdims=True) + dgl_tot)
        suffix_sums = _ones_dot_each(utri, dgc_lanes)
        for j, sl in enumerate(sls):
            dq_ref[0, sl, :] = dq[j]
            dk_ref[0, sl, :] = dk[j]
            dv_ref[0, sl, :] = dvb[j] * r["beta"][j]
            dg = jnp.sum(jnp.where(lane == 0, suffix_sums[j] + d_tots[j], 0.0), axis=1, keepdims=True)
            da = dg * (-ea) * r["sig_a"][j]
            db = dbetas[j] * r["beta"][j] * (1.0 - r["beta"][j])
            dab_ref[0, sl, :] = jnp.where(lane == head, da, 0.0) + jnp.where(lane == GDN_HEADS + head, db, 0.0)
            acc_alog = acc_alog + jnp.sum(dg * r["g"][j], axis=0, keepdims=True)
            acc_dtb = acc_dtb + jnp.sum(da, axis=0, keepdims=True)
        dsc_ref[0, 0] = jnp.where(lane1 == 0, acc_alog, 0.0) + jnp.where(lane1 == 1, acc_dtb, 0.0)

    big = pl.BlockSpec((1, rows, GDN_DIM), lambda h, i: (h, i, 0))
    big_shape = jax.ShapeDtypeStruct((GDN_HEADS, t, GDN_DIM), F32)
    return pl.pallas_call(
        body, name=name, grid=(GDN_HEADS, n // cps),
        in_specs=_gdn_local_specs(t, cps) + [big] * 4 + [pl.BlockSpec((1, rows, CHUNK), lambda h, i: (h, i, 0)),
                                                        pl.BlockSpec((1, cps, 1, GDN_DIM), lambda h, i: (h, i, 0, 0))],
        out_specs=[big] * 4 + [pl.BlockSpec((1, 1, 1, GDN_DIM), lambda h, i: (h, i, 0, 0))],
        out_shape=[big_shape] * 4 + [jax.ShapeDtypeStruct((GDN_HEADS, n // cps, 1, GDN_DIM), F32)],
        compiler_params=_cparams("parallel", "parallel"),
    )(qkv, qkv, qkv, ab, a_log, dt_bias, du, dw, dqd, dkd, daqk, dgl)


SCAN_CHUNKS = 4


def _scan_chunks(n):
    return SCAN_CHUNKS if n % SCAN_CHUNKS == 0 else 1


def _gdn_scan_fwd(u, w, qd, kd, aqk, gl, *, name):
    h, t, _ = u.shape
    n = t // CHUNK
    cps = _scan_chunks(n)
    rows = cps * CHUNK
    hs = range(h)

    def body(u_ref, w_ref, qd_ref, kd_ref, aqk_ref, gl_ref, o_ref, s_ref, state):
        @pl.when(pl.program_id(0) == 0)
        def _():
            state[...] = jnp.zeros_like(state)

        ss = [state[hh] for hh in hs]
        for j in range(cps):
            sl = slice(j * CHUNK, (j + 1) * CHUNK)
            for hh in hs:
                s_ref[hh, j] = ss[hh]
            vn = [u_ref[hh, sl, :] - x for hh, x in zip(hs, _dot_each([w_ref[hh, sl, :] for hh in hs], ss))]
            from_state = _dot_each([qd_ref[hh, sl, :] for hh in hs], ss)
            from_chunk = _dot_each([aqk_ref[hh, sl, :] for hh in hs], vn)
            writes = _dot_each([kd_ref[hh, sl, :] for hh in hs], vn, 0, 0)
            for hh in hs:
                o_ref[hh, sl, :] = from_state[hh] + from_chunk[hh]
            ss = [ss[hh] * gl_ref[hh, j] + writes[hh] for hh in hs]
        for hh in hs:
            state[hh] = ss[hh]

    big = pl.BlockSpec((h, rows, GDN_DIM), lambda i: (0, i, 0))
    return pl.pallas_call(
        body, name=name, grid=(n // cps,),
        in_specs=[big] * 4 + [pl.BlockSpec((h, rows, CHUNK), lambda i: (0, i, 0)),
                              pl.BlockSpec((h, cps, 1, GDN_DIM), lambda i: (0, i, 0, 0))],
        out_specs=[big, pl.BlockSpec((h, cps, GDN_DIM, GDN_DIM), lambda i: (0, i, 0, 0))],
        out_shape=[jax.ShapeDtypeStruct((h, t, GDN_DIM), F32), jax.ShapeDtypeStruct((h, n, GDN_DIM, GDN_DIM), F32)],
        scratch_shapes=[pltpu.VMEM((h, GDN_DIM, GDN_DIM), F32)],
        compiler_params=_cparams("arbitrary"),
    )(u, w, qd, kd, aqk, gl)


def _gdn_scan_bwd(u, w, qd, kd, aqk, gl, states, do, *, name):
    h, t, _ = u.shape
    n = t // CHUNK
    cps = _scan_chunks(n)
    rows = cps * CHUNK
    steps = n // cps
    hs = range(h)

    def body(u_ref, w_ref, qd_ref, kd_ref, aqk_ref, gl_ref, s_ref, do_ref,
             du_ref, dw_ref, dqd_ref, dkd_ref, daqk_ref, dgl_ref, dstate):
        @pl.when(pl.program_id(0) == 0)
        def _():
            dstate[...] = jnp.zeros_like(dstate)

        ri = lax.broadcasted_iota(jnp.int32, (CHUNK, CHUNK), 0)
        ci = lax.broadcasted_iota(jnp.int32, (CHUNK, CHUNK), 1)
        dsn = [dstate[hh] for hh in hs]
        for j in reversed(range(cps)):
            sl = slice(j * CHUNK, (j + 1) * CHUNK)
            ss = [s_ref[hh, j] for hh in hs]
            dov = [do_ref[hh, sl, :] for hh in hs]
            wv = [w_ref[hh, sl, :] for hh in hs]
            vn = [u_ref[hh, sl, :] - x for hh, x in zip(hs, _dot_each(wv, ss))]
            dvn = [x + y for x, y in zip(_dot_each([aqk_ref[hh, sl, :] for hh in hs], dov, 0, 0),
                                         _dot_each([kd_ref[hh, sl, :] for hh in hs], dsn))]
            dws = _dot_each(dvn, ss, 1, 1)
            dqds = _dot_each(dov, ss, 1, 1)
            dkds = _dot_each(vn, dsn, 1, 1)
            daqks = _dot_each(dov, vn, 1, 1)
            reads = _dot_each([qd_ref[hh, sl, :] for hh in hs], dov, 0, 0)
            solves = _dot_each(wv, dvn, 0, 0)
            for hh in hs:
                du_ref[hh, sl, :] = dvn[hh]
                dw_ref[hh, sl, :] = -dws[hh]
                dqd_ref[hh, sl, :] = dqds[hh]
                dkd_ref[hh, sl, :] = dkds[hh]
                daqk_ref[hh, sl, :] = jnp.where(ri >= ci, daqks[hh], 0.0)
                dgl_ref[hh, j] = jnp.sum(dsn[hh] * ss[hh], axis=0, keepdims=True)
            dsn = [reads[hh] + dsn[hh] * gl_ref[hh, j] - solves[hh] for hh in hs]
        for hh in hs:
            dstate[hh] = dsn[hh]

    big = pl.BlockSpec((h, rows, GDN_DIM), lambda i: (0, steps - 1 - i, 0))
    sq = pl.BlockSpec((h, rows, CHUNK), lambda i: (0, steps - 1 - i, 0))
    glb = pl.BlockSpec((h, cps, 1, GDN_DIM), lambda i: (0, steps - 1 - i, 0, 0))
    big_shape = jax.ShapeDtypeStruct((h, t, GDN_DIM), F32)
    return pl.pallas_call(
        body, name=name, grid=(steps,),
        in_specs=[big] * 4 + [sq, glb, pl.BlockSpec((h, cps, GDN_DIM, GDN_DIM), lambda i: (0, steps - 1 - i, 0, 0)),
                              big],
        out_specs=[big] * 4 + [sq, glb],
        out_shape=[big_shape] * 4 + [jax.ShapeDtypeStruct((h, t, CHUNK), F32),
                                     jax.ShapeDtypeStruct((h, n, 1, GDN_DIM), F32)],
        scratch_shapes=[pltpu.VMEM((h, GDN_DIM, GDN_DIM), F32)],
        compiler_params=_cparams("arbitrary"),
    )(u, w, qd, kd, aqk, gl, states, do)


def _gdn_post_fwd(o, proj, norm_w, *, name):
    h, t, _ = o.shape
    tm = _tile(t, ROW_TILE)

    def body(o_ref, g_ref, w_ref, y_ref):
        for hh in range(h):
            sl = slice(hh * GDN_DIM, (hh + 1) * GDN_DIM)
            ov = o_ref[hh]
            gate = g_ref[:, sl].astype(F32)
            r = lax.rsqrt(jnp.mean(ov * ov, axis=-1, keepdims=True) + EPS)
            y_ref[:, sl] = (ov * r * w_ref[...] * (gate * _sigmoid(gate))).astype(y_ref.dtype)

    return pl.pallas_call(
        body, name=name, grid=(t // tm,),
        in_specs=[pl.BlockSpec((h, tm, GDN_DIM), lambda i: (0, i, 0)),
                  pl.BlockSpec((tm, GDN_W), lambda i: (i, C_GGATE // GDN_W)),
                  pl.BlockSpec((1, GDN_DIM), lambda i: (0, 0))],
        out_specs=pl.BlockSpec((tm, GDN_W), lambda i: (i, 0)),
        out_shape=jax.ShapeDtypeStruct((t, GDN_W), BF16), compiler_params=_cparams("parallel"),
    )(o, proj, norm_w.reshape(1, GDN_DIM))


def _gdn_post_bwd(o, proj, norm_w, dy, *, name):
    h, t, _ = o.shape
    tm = _tile(t, ROW_TILE)

    def body(o_ref, g_ref, w_ref, dy_ref, do_ref, dg_ref, dw_ref):
        part = jnp.zeros((1, GDN_DIM), F32)
        for hh in range(h):
            sl = slice(hh * GDN_DIM, (hh + 1) * GDN_DIM)
            ov = o_ref[hh]
            gate = g_ref[:, sl].astype(F32)
            sig = _sigmoid(gate)
            silu = gate * sig
            r = lax.rsqrt(jnp.mean(ov * ov, axis=-1, keepdims=True) + EPS)
            oh = ov * r
            dyv = dy_ref[:, sl].astype(F32)
            dg_ref[:, sl] = (dyv * oh * w_ref[...] * (sig + silu * (1.0 - sig))).astype(dg_ref.dtype)
            dn = dyv * silu
            part = part + jnp.sum(dn * oh, axis=0, keepdims=True)
            gw = dn * w_ref[...]
            do_ref[hh] = r * (gw - oh * jnp.mean(gw * oh, axis=-1, keepdims=True))

        @pl.when(pl.program_id(0) == 0)
        def _():
            dw_ref[...] = part

        @pl.when(pl.program_id(0) > 0)
        def _():
            dw_ref[...] += part

    return pl.pallas_call(
        body, name=name, grid=(t // tm,),
        in_specs=[pl.BlockSpec((h, tm, GDN_DIM), lambda i: (0, i, 0)),
                  pl.BlockSpec((tm, GDN_W), lambda i: (i, C_GGATE // GDN_W)),
                  pl.BlockSpec((1, GDN_DIM), lambda i: (0, 0)),
                  pl.BlockSpec((tm, GDN_W), lambda i: (i, 0))],
        out_specs=[pl.BlockSpec((h, tm, GDN_DIM), lambda i: (0, i, 0)), pl.BlockSpec((tm, GDN_W), lambda i: (i, 0)),
                   pl.BlockSpec((1, GDN_DIM), lambda i: (0, 0))],
        out_shape=[jax.ShapeDtypeStruct((h, t, GDN_DIM), F32), jax.ShapeDtypeStruct((t, GDN_W), BF16),
                   jax.ShapeDtypeStruct((1, GDN_DIM), F32)],
        compiler_params=_cparams("arbitrary"),
    )(o, proj, norm_w.reshape(1, GDN_DIM), dy)


def _gather_phases(x_refs, out_refs, send_sems, recv_sems, local_sems):
    n = len(x_refs)
    mx, my, mc = lax.axis_index("x"), lax.axis_index("y"), lax.axis_index("c")
    me, sibling = (mx, my, mc), (mx, my, 1 - mc)
    chips = [(1 - mx, my), (mx, 1 - my), (1 - mx, 1 - my)]

    def slot(a, px, py, pc):
        return out_refs[a].at[4 * px + 2 * py + pc]

    def copy(a, k, block, to, src=None):
        return pltpu.make_async_remote_copy(
            src_ref=slot(a, *block) if src is None else src, dst_ref=slot(a, *block),
            send_sem=send_sems.at[a, k], recv_sem=recv_sems.at[a, k], device_id=to, device_id_type=MESH_ID)

    def mine():
        return [pltpu.make_async_copy(x_refs[a], slot(a, *me), local_sems.at[a]) for a in range(n)]

    def first():
        out = [copy(a, 1 + j, me, (*chip, mc), src=x_refs[a]) for j, chip in enumerate(chips) for a in range(n)]
        return out + [copy(a, 0, me, sibling, src=x_refs[a]) for a in range(n)]

    def passed():
        return [copy(a, 4 + j, (*chip, mc), sibling) for j, chip in enumerate(chips) for a in range(n)]

    def start():
        for cp in mine() + first():
            cp.start()

    def pass_on():
        for j, chip in enumerate(chips):
            for a in range(n):
                copy(a, 1 + j, (*chip, mc), me).wait_recv()
                copy(a, 4 + j, (*chip, mc), sibling).start()

    def finish():
        for a in range(n):
            copy(a, 0, sibling, me).wait_recv()
        for j, chip in enumerate(chips):
            for a in range(n):
                copy(a, 4 + j, (*chip, 1 - mc), me).wait_recv()
        for cp in first() + passed():
            cp.wait_send()
        for cp in mine():
            cp.wait()

    return start, pass_on, finish


def _gather_extras(shards):
    n = len(shards)
    anyspace = pl.BlockSpec(memory_space=pl.ANY)
    return ([anyspace] * n, [anyspace] * n, [jax.ShapeDtypeStruct((N_DEV,) + x.shape, x.dtype) for x in shards],
            [pltpu.SemaphoreType.DMA((n, 7)), pltpu.SemaphoreType.DMA((n, 7)), pltpu.SemaphoreType.DMA((n,))])


def _scatter_phases(g_refs, out_refs, send_sems, recv_sems, local_sems):
    n = len(g_refs)
    mx, my, mc = lax.axis_index("x"), lax.axis_index("y"), lax.axis_index("c")
    me_id = 4 * mx + 2 * my + mc

    def peer(r):
        return (1 - mx if r & 4 else mx, 1 - my if r & 2 else my, 1 - mc if r & 1 else mc)

    def peer_id(r):
        px, py, pc = peer(r)
        return 4 * px + 2 * py + pc

    def copies():
        return [pltpu.make_async_remote_copy(
            src_ref=g_refs[a].at[peer_id(r)], dst_ref=out_refs[a].at[me_id], send_sem=send_sems.at[a, r - 1],
            recv_sem=recv_sems.at[a, r - 1], device_id=peer(r), device_id_type=MESH_ID)
            for r in range(1, N_DEV) for a in range(n)]

    def arrivals():
        return [pltpu.make_async_remote_copy(
            src_ref=g_refs[a].at[peer_id(r)], dst_ref=out_refs[a].at[peer_id(r)], send_sem=send_sems.at[a, r - 1],
            recv_sem=recv_sems.at[a, r - 1], device_id=peer(r), device_id_type=MESH_ID)
            for r in range(1, N_DEV) for a in range(n)]

    def mine():
        return [pltpu.make_async_copy(g_refs[a].at[me_id], out_refs[a].at[me_id], local_sems.at[a]) for a in range(n)]

    def start():
        for cp in mine() + copies():
            cp.start()

    def finish():
        for cp in arrivals():
            cp.wait_recv()
        for cp in copies():
            cp.wait_send()
        for cp in mine():
            cp.wait()

    return start, finish


def _scatter_extras(blocks):
    n = len(blocks)
    anyspace = pl.BlockSpec(memory_space=pl.ANY)
    return ([anyspace] * n, [anyspace] * n, [jax.ShapeDtypeStruct(b.shape, b.dtype) for b in blocks],
            [pltpu.SemaphoreType.DMA((n, 7)), pltpu.SemaphoreType.DMA((n, 7)), pltpu.SemaphoreType.DMA((n,))])


def _sum_slots(x, *, name):
    _, r, c = x.shape
    tr = _tile(r, ROW_TILE)

    def body(x_ref, o_ref):
        acc = x_ref[0].astype(F32)
        for s in range(1, N_DEV):
            acc = acc + x_ref[s].astype(F32)
        o_ref[...] = acc

    return pl.pallas_call(
        body, name=name, grid=(r // tr,), in_specs=[pl.BlockSpec((N_DEV, tr, c), lambda i: (0, i, 0))],
        out_specs=pl.BlockSpec((tr, c), lambda i: (i, 0)), out_shape=jax.ShapeDtypeStruct((r, c), F32),
        compiler_params=_cparams("parallel"),
    )(x)


SB_PAIRS = SB_HEADS // 2
SB_PAIR_QBLOCK = 512
SB_PAIR_QBLOCK_FWD = 512
SB_SCALE = SB_DIM ** -0.5


def _suffix_ones(blk):
    ri = lax.broadcasted_iota(jnp.int32, (blk, blk), 0)
    ci = lax.broadcasted_iota(jnp.int32, (blk, blk), 1)
    return (ri >= ci).astype(BF16)


def _prefix_ones(blk):
    ri = lax.broadcasted_iota(jnp.int32, (blk, blk), 0)
    ci = lax.broadcasted_iota(jnp.int32, (blk, blk), 1)
    return (ri <= ci).astype(BF16)


def _sb_pair_blocks(t, pref=SB_PAIR_QBLOCK):
    bq = _tile(t, pref)
    return bq, _tile(bq, SB_KBLOCK)


def _sb_pair_specs(t, bq):
    base = C_SBQKV // 128
    return [pl.BlockSpec((bq, 128), lambda p, i: (i, base + p)),
            pl.BlockSpec((t, 128), lambda p, i: (0, base + SB_PAIRS + p)),
            pl.BlockSpec((t, 128), lambda p, i: (0, base + 2 * SB_PAIRS + p))]


def _halves(x, first):
    zero = jnp.zeros_like(x)
    return [jnp.where(first, x, zero), jnp.where(first, zero, x)]


def _sb_mask(qi, kb, bq, bk):
    t_idx = qi * bq + lax.broadcasted_iota(jnp.int32, (bq, bk), 0)
    s_idx = kb * bk + lax.broadcasted_iota(jnp.int32, (bq, bk), 1)
    return s_idx < t_idx


def _sb_pair_scores(qh, kblk, mask):
    zs = _dot_each(qh, [kblk, kblk], 1, 1)
    es = [jnp.exp(-jnp.abs(z)) for z in zs]
    sps = [jnp.maximum(z, 0.0) + jnp.log(1.0 + e) for z, e in zip(zs, es)]
    if mask is not None:
        sps = [jnp.where(mask, sp, 0.0) for sp in sps]
    return zs, es, sps


def _sb_atts(zs, csums, laters, mask):
    atts = [jnp.exp(z - c - l) for z, c, l in zip(zs, csums, laters)]
    return atts if mask is None else [jnp.where(mask, a, 0.0) for a in atts]


def _scaled_queries(q_ref, first):
    return _halves(q_ref[...] * jnp.asarray(SB_SCALE, q_ref.dtype), first)


def _running_sums(x_list, m):
    return [_dot(x, m) for x in x_list]


def _sb_pair_fwd(proj, shards=(), *, name):
    t = proj.shape[0]
    bq, bk = _sb_pair_blocks(t, SB_PAIR_QBLOCK_FWD)
    n = len(shards)
    nq = t // bq
    nsteps = SB_PAIRS * nq

    def body(q_ref, k_ref, v_ref, *rest):
        o_ref = rest[n]
        qi = pl.program_id(1)
        if n:
            step_no = pl.program_id(0) * nq + qi
            start, pass_on, finish = _gather_phases(rest[:n], rest[n + 1:2 * n + 1], *rest[2 * n + 1:])
            pl.when(step_no == 0)(start)
            pl.when(step_no == (7 * nsteps) // 8)(pass_on)
        first = lax.broadcasted_iota(jnp.int32, (1, 128), 1) < SB_DIM
        qh = _scaled_queries(q_ref, first)
        suffix = _suffix_ones(bk)
        band = bq // bk
        nkb = (qi + 1) * band

        def make_step(masked):
            def step(it, carry):
                later0, later1, acc = carry
                kb = nkb - 1 - it
                rows = pl.ds(pl.multiple_of(kb * bk, bk), bk)
                mask = _sb_mask(qi, kb, bq, bk) if masked else None
                zs, _, sps = _sb_pair_scores(qh, k_ref[rows, :], mask)
                atts = _sb_atts(zs, _running_sums(sps, suffix), (later0, later1), mask)
                outs = _dot_each(atts, _halves(v_ref[rows, :], first))
                return (later0 + jnp.sum(sps[0], axis=1, keepdims=True),
                        later1 + jnp.sum(sps[1], axis=1, keepdims=True), acc + (outs[0] + outs[1]))
            return step

        zero = jnp.zeros((bq, 1), F32)
        carry = lax.fori_loop(0, band, make_step(True), (zero, zero, jnp.zeros((bq, 128), F32)))
        _, _, acc = lax.fori_loop(band, nkb, make_step(False), carry)
        o_ref[...] = acc.astype(o_ref.dtype)
        if n:
            pl.when(step_no == nsteps - 1)(finish)

    more_in, more_out, more_shapes, sems = _gather_extras(shards) if n else ([], [], [], [])
    res = pl.pallas_call(
        body, name=name, grid=(SB_PAIRS, nq), in_specs=_sb_pair_specs(t, bq) + more_in,
        out_specs=[pl.BlockSpec((bq, 128), lambda p, i: (i, p))] + more_out,
        out_shape=[jax.ShapeDtypeStruct((t, SB_W), BF16)] + more_shapes, scratch_shapes=sems,
        compiler_params=_cparams("arbitrary", "arbitrary"),
    )(proj, proj, proj, *shards)
    return res[0], list(res[1:])


def _sb_pair_bwd(proj, dy, blocks=(), *, name):
    t = proj.shape[0]
    bq, bk = _sb_pair_blocks(t)
    nq = t // bq
    n = len(blocks)

    def body(q_ref, k_ref, v_ref, do_ref, *rest):
        dq_ref, dk_ref, dv_ref = rest[n:n + 3]
        dl_keep, sig_keep, dk_acc, dv_acc = rest[2 * n + 3:2 * n + 7]
        qi = pl.program_id(1)
        if n:
            step_no = pl.program_id(0) * nq + qi
            start, finish = _scatter_phases(rest[:n], rest[n + 3:2 * n + 3], *rest[2 * n + 7:])
            pl.when(step_no == 0)(start)

        @pl.when(qi == 0)
        def _():
            dk_acc[...] = jnp.zeros_like(dk_acc)
            dv_acc[...] = jnp.zeros_like(dv_acc)

        first = lax.broadcasted_iota(jnp.int32, (1, 128), 1) < SB_DIM
        qh = _scaled_queries(q_ref, first)
        doh = _halves(do_ref[...], first)
        suffix = _suffix_ones(bk)
        prefix = _prefix_ones(bk)
        band = bq // bk
        nkb = (qi + 1) * band

        def make_back(masked):
            def back(it, carry):
                kb = nkb - 1 - it
                rows = pl.ds(pl.multiple_of(kb * bk, bk), bk)
                vblk = v_ref[rows, :]
                mask = _sb_mask(qi, kb, bq, bk) if masked else None
                zs, es, sps = _sb_pair_scores(qh, k_ref[rows, :], mask)
                atts = _sb_atts(zs, _running_sums(sps, suffix), carry, mask)
                dvs = _dot_each(atts, doh, 0, 0)
                datts = _dot_each(doh, [vblk, vblk], 1, 1)
                dv_acc[rows, :] += dvs[0] + dvs[1]
                for hh in range(2):
                    sig = jnp.where(zs[hh] >= 0, 1.0, es[hh]) * pl.reciprocal(1.0 + es[hh], approx=True)
                    dl_keep[hh, kb] = (atts[hh] * datts[hh]).astype(dl_keep.dtype)
                    sig_keep[hh, kb] = (sig if mask is None else jnp.where(mask, sig, 0.0)).astype(sig_keep.dtype)
                return tuple(l + jnp.sum(sp, axis=1, keepdims=True) for l, sp in zip(carry, sps))
            return back

        zero = jnp.zeros((bq, 1), F32)
        lax.fori_loop(band, nkb, make_back(False), lax.fori_loop(0, band, make_back(True), (zero, zero)))

        def forth(kb, carry):
            before0, before1, dq = carry
            rows = pl.ds(pl.multiple_of(kb * bk, bk), bk)
            kept = [dl_keep[hh, kb] for hh in range(2)]
            sums = _running_sums(kept, prefix)
            dls = [x.astype(F32) for x in kept]
            dzs = [dl - sig_keep[hh, kb].astype(F32) * (b + s)
                   for hh, (dl, b, s) in enumerate(zip(dls, (before0, before1), sums))]
            dks = _dot_each(dzs, qh, 0, 0)
            dqs = _dot_each(dzs, _halves(k_ref[rows, :], first))
            dk_acc[rows, :] += dks[0] + dks[1]
            return (before0 + jnp.sum(dls[0], axis=1, keepdims=True), before1 + jnp.sum(dls[1], axis=1, keepdims=True),
                    dq + (dqs[0] + dqs[1]))

        _, _, dq = lax.fori_loop(0, nkb, forth, (zero, zero, jnp.zeros((bq, 128), F32)))
        dq_ref[...] = (dq * SB_SCALE).astype(dq_ref.dtype)

        @pl.when(qi == nq - 1)
        def _():
            dk_ref[...] = dk_acc[...].astype(dk_ref.dtype)
            dv_ref[...] = dv_acc[...].astype(dv_ref.dtype)

        if n:
            pl.when(step_no == SB_PAIRS * nq - 1)(finish)

    qspec = pl.BlockSpec((bq, 128), lambda p, i: (i, p))
    kvspec = pl.BlockSpec((t, 128), lambda p, i: (0, p))
    shape = jax.ShapeDtypeStruct((t, SB_W), BF16)
    more_in, more_out, more_shapes, sems = _scatter_extras(blocks) if n else ([], [], [], [])
    res = pl.pallas_call(
        body, name=name, grid=(SB_PAIRS, nq), in_specs=_sb_pair_specs(t, bq) + [qspec] + more_in,
        out_specs=[qspec, kvspec, kvspec] + more_out, out_shape=[shape] * 3 + more_shapes,
        scratch_shapes=[pltpu.VMEM((2, t // bk, bq, bk), BF16), pltpu.VMEM((2, t // bk, bq, bk), BF16),
                        pltpu.VMEM((t, 128), F32), pltpu.VMEM((t, 128), F32)] + sems,
        compiler_params=_cparams("arbitrary", "arbitrary"),
    )(proj, proj, proj, dy, *blocks)
    return res[0], res[1], res[2], list(res[3:])


def _gate_specs(tm):
    return [pl.BlockSpec((tm, D_MODEL), lambda i, b=b: (i, C_GATES // D_MODEL + b)) for b in range(3)]


def _merge_fwd(p, proj, *, name):
    t = proj.shape[0]
    tm = _tile(t, ROW_TILE)

    def body(p0, p1, p2, g0, g1, g2, o_ref):
        acc = jnp.zeros((tm, D_MODEL), F32)
        for pr, gr in ((p0, g0), (p1, g1), (p2, g2)):
            acc = acc + _sigmoid(gr[...].astype(F32)) * pr[...].astype(F32)
        o_ref[...] = acc.astype(o_ref.dtype)

    row = pl.BlockSpec((tm, D_MODEL), lambda i: (i, 0))
    return pl.pallas_call(
        body, name=name, grid=(t // tm,), in_specs=[row] * 3 + _gate_specs(tm), out_specs=row,
        out_shape=jax.ShapeDtypeStruct((t, D_MODEL), BF16), compiler_params=_cparams("parallel"),
    )(*p, proj, proj, proj)


def _merge_bwd(p, proj, dmerged, *, name):
    t = proj.shape[0]
    tm = _tile(t, ROW_TILE)

    def body(p0, p1, p2, g0, g1, g2, dm_ref, dp0, dp1, dp2, dg_ref):
        dm = dm_ref[...].astype(F32)
        for b, (pr, gr, dpr) in enumerate(((p0, g0, dp0), (p1, g1, dp1), (p2, g2, dp2))):
            s = _sigmoid(gr[...].astype(F32))
            dpr[...] = (dm * s).astype(dpr.dtype)
            dg_ref[:, b * D_MODEL:(b + 1) * D_MODEL] = (dm * pr[...].astype(F32) * s * (1.0 - s)).astype(dg_ref.dtype)

    row = pl.BlockSpec((tm, D_MODEL), lambda i: (i, 0))
    res = pl.pallas_call(
        body, name=name, grid=(t // tm,), in_specs=[row] * 3 + _gate_specs(tm) + [row],
        out_specs=[row] * 3 + [pl.BlockSpec((tm, 3 * D_MODEL), lambda i: (i, 0))],
        out_shape=[jax.ShapeDtypeStruct((t, D_MODEL), BF16)] * 3 + [jax.ShapeDtypeStruct((t, 3 * D_MODEL), BF16)],
        compiler_params=_cparams("parallel"),
    )(*p, proj, proj, proj, dmerged)
    return res[:3], res[3]


def _loss_head(y, target, *, name):
    t, d = y.shape
    tm = _tile(t, ROW_TILE)

    def body(y_ref, t_ref, dy_ref, l_ref):
        err = y_ref[...] - t_ref[...]
        dy_ref[...] = err * (1.0 / d)
        part = jnp.sum(err * err, axis=0, keepdims=True) * (0.5 / d)

        @pl.when(pl.program_id(0) == 0)
        def _():
            l_ref[...] = part

        @pl.when(pl.program_id(0) > 0)
        def _():
            l_ref[...] += part

    row = pl.BlockSpec((tm, d), lambda i: (i, 0))
    vec = pl.BlockSpec((1, d), lambda i: (0, 0))
    return pl.pallas_call(
        body, name=name, grid=(t // tm,), in_specs=[row, row], out_specs=[row, vec],
        out_shape=[jax.ShapeDtypeStruct((t, d), F32), jax.ShapeDtypeStruct((1, d), F32)],
        compiler_params=_cparams("arbitrary"),
    )(y, target)


def _adamw(w, g, m, v, *, name):
    r, c = w.shape
    tr = r if r * c * 4 <= 2 ** 21 else max(8, (2 ** 21 // (c * 4)) // 8 * 8)
    while r % tr:
        tr -= 8
    c1 = 1.0 - ADAM_B1 ** ADAM_STEP
    c2 = 1.0 - ADAM_B2 ** ADAM_STEP

    def body(w_ref, g_ref, m_ref, v_ref, d_ref, nm_ref, nv_ref):
        gv = g_ref[...]
        nm = ADAM_B1 * m_ref[...] + (1.0 - ADAM_B1) * gv
        nv = ADAM_B2 * v_ref[...] + (1.0 - ADAM_B2) * (gv * gv)
        nm_ref[...] = nm
        nv_ref[...] = nv
        d_ref[...] = -ADAM_LR * ((nm / c1) / (jnp.sqrt(nv / c2) + ADAM_EPS) + ADAM_WD * w_ref[...])

    spec = pl.BlockSpec((tr, c), lambda i: (i, 0))
    return pl.pallas_call(
        body, name=name, grid=(r // tr,), in_specs=[spec] * 4, out_specs=[spec] * 3,
        out_shape=[jax.ShapeDtypeStruct((r, c), F32)] * 3, compiler_params=_cparams("parallel"),
    )(w, g, m, v)


def _all_gather(xs, *, name):
    n = len(xs)

    def body(*refs):
        start, pass_on, finish = _gather_phases(refs[:n], refs[n:2 * n], *refs[2 * n:])
        start()
        pass_on()
        finish()

    more_in, more_out, more_shapes, sems = _gather_extras(xs)
    return pl.pallas_call(body, name=name, in_specs=more_in, out_specs=more_out, out_shape=more_shapes,
                          scratch_shapes=sems)(*xs)


def _exchange_sibling(gs, *, name):
    n = len(gs)

    def body(*refs):
        g_refs, out_refs = refs[:n], refs[n:2 * n]
        send_sems, recv_sems = refs[2 * n:]
        mx, my, mc = lax.axis_index("x"), lax.axis_index("y"), lax.axis_index("c")
        sibling = (mx, my, 1 - mc)
        copies = []
        for a in range(n):
            for px in range(2):
                for py in range(2):
                    kk = 2 * px + py
                    copies.append(pltpu.make_async_remote_copy(
                        src_ref=g_refs[a].at[4 * px + 2 * py + (1 - mc)], dst_ref=out_refs[a].at[kk],
                        send_sem=send_sems.at[a, kk], recv_sem=recv_sems.at[a, kk], device_id=sibling,
                        device_id_type=MESH_ID))
        for cp in copies:
            cp.start()
        for cp in copies:
            cp.wait_recv()
        for cp in copies:
            cp.wait_send()

    anyspace = pl.BlockSpec(memory_space=pl.ANY)
    return pl.pallas_call(
        body, name=name, in_specs=[anyspace] * n, out_specs=[anyspace] * n,
        out_shape=[jax.ShapeDtypeStruct((4,) + g.shape[1:], g.dtype) for g in gs],
        scratch_shapes=[pltpu.SemaphoreType.DMA((n, 4)), pltpu.SemaphoreType.DMA((n, 4))],
    )(*gs)


def _pair_sum(g, got, *, name):
    _, r, c = g.shape
    tr = _tile(r, ROW_TILE)

    def body(core_ref, a_ref, b_ref, o_ref):
        del core_ref
        o_ref[...] = (a_ref[...].astype(F32) + b_ref[...].astype(F32)).astype(o_ref.dtype)

    grid_spec = pltpu.PrefetchScalarGridSpec(
        num_scalar_prefetch=1, grid=(4, r // tr),
        in_specs=[pl.BlockSpec((1, tr, c), lambda kk, i, core: (2 * kk + core[0], i, 0)),
                  pl.BlockSpec((1, tr, c), lambda kk, i, core: (kk, i, 0))],
        out_specs=pl.BlockSpec((1, tr, c), lambda kk, i, core: (kk, i, 0)))
    return pl.pallas_call(
        body, name=name, grid_spec=grid_spec, out_shape=jax.ShapeDtypeStruct((4, r, c), g.dtype),
        compiler_params=_cparams("parallel", "parallel"),
    )(lax.axis_index("c").astype(jnp.int32).reshape(1), g, got)


def _exchange_chips(parts, *, name):
    n = len(parts)

    def body(*refs):
        p_refs, out_refs = refs[:n], refs[n:2 * n]
        send_sems, recv_sems = refs[2 * n:]
        mx, my, mc = lax.axis_index("x"), lax.axis_index("y"), lax.axis_index("c")
        chips = [(1 - mx, my), (mx, 1 - my), (1 - mx, 1 - my)]
        copies = [pltpu.make_async_remote_copy(
            src_ref=p_refs[a].at[2 * px + py], dst_ref=out_refs[a].at[j], send_sem=send_sems.at[a, j],
            recv_sem=recv_sems.at[a, j], device_id=(px, py, mc), device_id_type=MESH_ID)
            for j, (px, py) in enumerate(chips) for a in range(n)]
        for cp in copies:
            cp.start()
        for cp in copies:
            cp.wait_recv()
        for cp in copies:
            cp.wait_send()

    anyspace = pl.BlockSpec(memory_space=pl.ANY)
    return pl.pallas_call(
        body, name=name, in_specs=[anyspace] * n, out_specs=[anyspace] * n,
        out_shape=[jax.ShapeDtypeStruct((3,) + p.shape[1:], p.dtype) for p in parts],
        scratch_shapes=[pltpu.SemaphoreType.DMA((n, 3)), pltpu.SemaphoreType.DMA((n, 3))],
    )(*parts)


def _final_sum(part, got, *, name):
    _, r, c = part.shape
    tr = _tile(r, ROW_TILE)

    def body(chip_ref, a_ref, b_ref, o_ref):
        del chip_ref
        acc = a_ref[0].astype(F32)
        for j in range(3):
            acc = acc + b_ref[j].astype(F32)
        o_ref[...] = acc

    grid_spec = pltpu.PrefetchScalarGridSpec(
        num_scalar_prefetch=1, grid=(r // tr,),
        in_specs=[pl.BlockSpec((1, tr, c), lambda i, chip: (chip[0], i, 0)),
                  pl.BlockSpec((3, tr, c), lambda i, chip: (0, i, 0))],
        out_specs=pl.BlockSpec((tr, c), lambda i, chip: (i, 0)))
    chip = (2 * lax.axis_index("x") + lax.axis_index("y")).astype(jnp.int32).reshape(1)
    return pl.pallas_call(
        body, name=name, grid_spec=grid_spec, out_shape=jax.ShapeDtypeStruct((r, c), F32),
        compiler_params=_cparams("parallel"),
    )(chip, part, got)


def _sum_devices(x, *, name):
    _, r, c = x.shape

    def body(x_ref, o_ref):
        acc = x_ref[0]
        for j in range(1, N_DEV):
            acc = acc + x_ref[j]
        o_ref[...] = acc

    return pl.pallas_call(body, name=name, out_shape=jax.ShapeDtypeStruct((r, c), F32),
                          compiler_params=_cparams())(x)


BIG = ("w_in", "w_branch", "w_out", "w_ff1", "w_ff2")
BIG_AXIS = {"w_in": 2, "w_branch": 3, "w_out": 1, "w_ff1": 2, "w_ff2": 1}


def _to_global(blocks, axis):
    moved = jnp.moveaxis(blocks, 0, axis)
    shp = moved.shape
    return moved.reshape(shp[:axis] + (shp[axis] * shp[axis + 1],) + shp[axis + 2:])


def _to_blocks(full, axis):
    shp = full.shape
    split = full.reshape(shp[:axis] + (N_DEV, shp[axis] // N_DEV) + shp[axis + 1:])
    return jnp.moveaxis(split, axis, 0)


def _rows(blocks):
    return blocks.reshape(blocks.shape[0], -1, blocks.shape[-1])


def _vec_rows(n):
    return -(-n // 128 // 8) * 8


def _pack_vec(parts):
    flat = jnp.concatenate([p.reshape(-1).astype(F32) for p in parts])
    rows = _vec_rows(flat.shape[0])
    return jnp.pad(flat, (0, rows * 128 - flat.shape[0])).reshape(rows, 128)


def _unpack_vec(flat, shapes):
    lead = flat.shape[:-2]
    flat = flat.reshape(lead + (-1,))
    out, off = [], 0
    for s in shapes:
        n = 1
        for dim in s:
            n *= dim
        out.append(flat[..., off:off + n].reshape(lead + tuple(s)))
        off += n
    return out


def _relu2_epilogue(r):
    a = jnp.maximum(r, 0.0)
    return r, a * a


def _add_epilogue(r, other):
    return (r + other,)


def _relu2_bwd_epilogue(r, a):
    return (r * 2.0 * jnp.maximum(a.astype(F32), 0.0),)


def _layer_fwd(x, p, shards):
    h = _norm_fwd(x, p["norm_mix_pre"], out_dtype=BF16, name="norm_pre_fwd")
    proj = _mm(h, p["w_main"], name="mm_in")
    ab = _mm(h, p["w_ab"], out_dtypes=(F32,), name="mm_ab")
    qkv = _gdn_pre_fwd(proj, p["conv_qkv_w"], name="gdn_pre_fwd")
    a_log, dt_bias = p["gdn_a_log"].reshape(1, GDN_HEADS), p["gdn_dt_bias"].reshape(1, GDN_HEADS)
    u, w, qd, kd, aqk, gl = _gdn_local_fwd(qkv, ab, a_log, dt_bias, name="gdn_local_fwd")
    o_gdn, states = _gdn_scan_fwd(u, w, qd, kd, aqk, gl, name="gdn_scan_fwd")
    y_a = _gdn_post_fwd(o_gdn, proj, p["gdn_norm_w"], name="gdn_post_fwd")
    y_b, gathered = _sb_pair_fwd(proj, shards, name="sb_fwd")
    p = dict(p)
    p.update({k: _to_global(blk, BIG_AXIS[k] - 1) for k, blk in zip(BIG[1:], gathered)})
    y_c = _sc_fwd(proj, p["conv_sc_w"], name="sc_fwd")
    ys = (y_a, y_b, y_c)
    ps = tuple(_mm(ys[b], p["w_branch"][b], name="mm_branch") for b in range(3))
    merged = _merge_fwd(ps, proj, name="merge_fwd")
    mo = _mm(merged, p["w_out"], out_dtypes=(F32,), name="mm_out")
    x1 = _norm_fwd(mo, p["norm_mix_post"], x, out_dtype=F32, name="norm_post_fwd")
    h2 = _norm_fwd(x1, p["norm_ffn_pre"], out_dtype=BF16, name="norm_pre_fwd")
    a1, r1 = _mm(h2, p["w_ff1"], out_dtypes=(BF16, BF16), epi=_relu2_epilogue, name="mm_ff1")
    f = _mm(r1, p["w_ff2"], out_dtypes=(F32,), name="mm_ff2")
    x2 = _norm_fwd(f, p["norm_ffn_post"], x1, out_dtype=F32, name="norm_post_fwd")
    saved = dict(x=x, h=h, proj=proj, ab=ab, qkv=qkv, u=u, w=w, qd=qd, kd=kd, aqk=aqk, gl=gl, o_gdn=o_gdn,
                 states=states, ys=ys, ps=ps, merged=merged, mo=mo, x1=x1, h2=h2,
                 a1=a1, r1=r1, f=f)
    return x2, saved, p, gathered[len(BIG) - 1:]


def _layer_bwd(dx2, p, s, above=()):
    g = {}
    df, g["norm_ffn_post"] = _norm_bwd(s["f"], p["norm_ffn_post"], dx2, out_dtype=BF16, name="norm_bwd_b")
    da1 = _mm(df, p["w_ff2"], tb=True, epi=_relu2_bwd_epilogue, extras=(s["a1"],), name="mm_ff2_dx")
    g["w_ff2"] = _mm(s["r1"], df, ta=True, name="mm_ff2_dw")
    g["w_ff1"] = _mm(s["h2"], da1, ta=True, name="mm_ff1_dw")
    dh2 = _mm(da1, p["w_ff1"], tb=True, out_dtypes=(F32,), name="mm_ff1_dx")
    dx1, g["norm_ffn_pre"] = _norm_bwd(s["x1"], p["norm_ffn_pre"], dh2, dx2, out_dtype=F32, name="norm_bwd_f")
    dmo, g["norm_mix_post"] = _norm_bwd(s["mo"], p["norm_mix_post"], dx1, out_dtype=BF16, name="norm_bwd_b")
    dmerged = _mm(dmo, p["w_out"], tb=True, name="mm_out_dx")
    g["w_out"] = _mm(s["merged"], dmo, ta=True, name="mm_out_dw")
    dps, dgates = _merge_bwd(s["ps"], s["proj"], dmerged, name="merge_bwd")
    dys = [_mm(dps[b], p["w_branch"][b], tb=True, name="mm_branch_dx") for b in range(3)]
    g["w_branch"] = jnp.stack([_mm(s["ys"][b], dps[b], ta=True, name="mm_branch_dw") for b in range(3)])
    dscx, dscb, dscc, g["conv_sc_w"] = _sc_bwd(s["proj"], p["conv_sc_w"], dys[2], name="sc_bwd")
    own = [_rows(_to_blocks(g[k], BIG_AXIS[k] - 1)) for k in BIG[1:]]
    dsq, dsk, dsv, received = _sb_pair_bwd(s["proj"], dys[1], own + list(above), name="sb_bwd")
    a_log, dt_bias = p["gdn_a_log"].reshape(1, GDN_HEADS), p["gdn_dt_bias"].reshape(1, GDN_HEADS)
    do_gdn, dggate, g["gdn_norm_w"] = _gdn_post_bwd(s["o_gdn"], s["proj"], p["gdn_norm_w"], dys[0], name="gdn_post_bwd")
    du, dw, dqd, dkd, daqk, dgl = _gdn_scan_bwd(s["u"], s["w"], s["qd"], s["kd"], s["aqk"], s["gl"], s["states"],
                                                do_gdn, name="gdn_scan_bwd")
    dq, dk, dv, dab_h, dsc = _gdn_local_bwd(s["qkv"], s["ab"], a_log, dt_bias, du, dw, dqd, dkd, daqk, dgl,
                                            name="gdn_local_bwd")
    dsc = jnp.sum(dsc, axis=(1, 2))
    g["gdn_a_log"], g["gdn_dt_bias"] = dsc[:, 0], dsc[:, 1]
    dqkv = jnp.concatenate([dq, dk, dv], axis=0)
    dgqkv, g["conv_qkv_w"] = _gdn_pre_bwd(s["proj"], p["conv_qkv_w"], dqkv, name="gdn_pre_bwd")
    dab = jnp.sum(dab_h, axis=0).astype(BF16)
    dproj = jnp.concatenate([dgqkv, dggate, dsq, dsk, dsv, dscx, dscb, dscc, dgates], axis=1)
    g["w_main"] = _mm(s["h"], dproj, ta=True, name="mm_in_dw")
    g["w_ab"] = _mm(s["h"], dab, ta=True, out_dtypes=(F32,), name="mm_ab_dw")
    dh_ab = _mm(dab, p["w_ab"], tb=True, out_dtypes=(F32,), name="mm_ab_dx")
    dh = _mm(dproj, p["w_main"], tb=True, out_dtypes=(F32,), epi=_add_epilogue, extras=(dh_ab,), name="mm_in_dx")
    dx, g["norm_mix_pre"] = _norm_bwd(s["x"], p["norm_mix_pre"], dh, dx1, out_dtype=F32, name="norm_bwd_f")
    return dx, g, received


NORMS = ("norm_mix_pre", "norm_mix_post", "norm_ffn_pre", "norm_ffn_post")
SMALL = NORMS + ("gdn_a_log", "gdn_dt_bias", "gdn_norm_w")
CONVS = ("conv_qkv_w", "conv_sc_w")
AB_LO = 2048


def _split_w_in(w_in):
    main = jnp.concatenate([w_in[..., :AB_LO], w_in[..., AB_LO + 2 * GDN_HEADS:]], axis=-1)
    ab = w_in[..., AB_LO:AB_LO + 2 * GDN_HEADS]
    pad = [(0, 0)] * (ab.ndim - 1) + [(0, AB_W - 2 * GDN_HEADS)]
    return main, jnp.pad(ab, pad)


def _join_w_in(main, ab):
    return jnp.concatenate([main[..., :AB_LO], ab[..., :2 * GDN_HEADS].astype(main.dtype), main[..., AB_LO:]], axis=-1)


def kernel(x, norm_mix_pre, w_in, conv_qkv_w, gdn_a_log, gdn_dt_bias, gdn_norm_w, conv_sc_w, w_branch, w_out, norm_mix_post, norm_ffn_pre, w_ff1, w_ff2, norm_ffn_post, loss_target, m_norm_mix_pre, m_w_in, m_conv_qkv_w, m_gdn_a_log, m_gdn_dt_bias, m_gdn_norm_w, m_conv_sc_w, m_w_branch, m_w_out, m_norm_mix_post, m_norm_ffn_pre, m_w_ff1, m_w_ff2, m_norm_ffn_post, v_norm_mix_pre, v_w_in, v_conv_qkv_w, v_gdn_a_log, v_gdn_dt_bias, v_gdn_norm_w, v_conv_sc_w, v_w_branch, v_w_out, v_norm_mix_post, v_norm_ffn_pre, v_w_ff1, v_w_ff2, v_norm_ffn_post):
    names = ("norm_mix_pre", "w_in", "conv_qkv_w", "gdn_a_log", "gdn_dt_bias", "gdn_norm_w", "conv_sc_w", "w_branch",
             "w_out", "norm_mix_post", "norm_ffn_pre", "w_ff1", "w_ff2", "norm_ffn_post")
    w = dict(zip(names, (norm_mix_pre, w_in, conv_qkv_w, gdn_a_log, gdn_dt_bias, gdn_norm_w, conv_sc_w, w_branch,
                         w_out, norm_mix_post, norm_ffn_pre, w_ff1, w_ff2, norm_ffn_post)))
    m = dict(zip(names, (m_norm_mix_pre, m_w_in, m_conv_qkv_w, m_gdn_a_log, m_gdn_dt_bias, m_gdn_norm_w, m_conv_sc_w,
                         m_w_branch, m_w_out, m_norm_mix_post, m_norm_ffn_pre, m_w_ff1, m_w_ff2, m_norm_ffn_post)))
    v = dict(zip(names, (v_norm_mix_pre, v_w_in, v_conv_qkv_w, v_gdn_a_log, v_gdn_dt_bias, v_gdn_norm_w, v_conv_sc_w,
                         v_w_branch, v_w_out, v_norm_mix_post, v_norm_ffn_pre, v_w_ff1, v_w_ff2, v_norm_ffn_post)))
    me = 4 * lax.axis_index("x") + 2 * lax.axis_index("y") + lax.axis_index("c")

    conv_shapes = [w[k].shape for k in CONVS]
    conv_all, = _all_gather([_pack_vec([w[k] for k in CONVS])], name="gather_small")
    convs = {k: _to_global(blk, 2) for k, blk in zip(CONVS, _unpack_vec(conv_all, conv_shapes))}
    shards = [[w[k][l].astype(BF16) for k in BIG] for l in range(DEPTH)]
    n_big = len(BIG)

    xs = x[0]
    w_in_blocks, = _all_gather(shards[0][:1], name="gather_weights")
    layers, saved = [], []
    for l in range(DEPTH):
        p = {k: convs[k][l] for k in CONVS}
        p.update({k: w[k][l] for k in SMALL})
        p["w_main"], p["w_ab"] = _split_w_in(_to_global(w_in_blocks, BIG_AXIS["w_in"] - 1))
        riding = shards[l][1:] + (shards[l + 1][:1] if l + 1 < DEPTH else [])
        xs, s, p, rest = _layer_fwd(xs, p, riding)
        layers.append(p)
        saved.append(s)
        w_in_blocks = rest[0] if rest else None
    dy, loss_lanes = _loss_head(xs, loss_target[0], name="loss_head")

    grads, big_sums, above = [None] * DEPTH, [[None] * n_big for _ in range(DEPTH)], []
    for l in reversed(range(DEPTH)):
        dy, g, received = _layer_bwd(dy, layers[l], saved[l], above)
        sums = [_sum_slots(r, name="rs_sum_slots") for r in received]
        big_sums[l][1:] = sums[:n_big - 1]
        if above:
            big_sums[l + 1][0] = sums[n_big - 1]
        g["w_in"] = _join_w_in(g.pop("w_main"), g.pop("w_ab"))
        above = [_rows(_to_blocks(g["w_in"], BIG_AXIS["w_in"] - 1))]
        grads[l] = g
    got = _exchange_sibling(above, name="rs_sibling")
    parts = [_pair_sum(b, r, name="rs_pair_sum") for b, r in zip(above, got)]
    got2 = _exchange_chips(parts, name="rs_chips")
    big_sums[0][0] = _final_sum(parts[0], got2[0], name="rs_final_sum")
    gsum = {k: jnp.stack([big_sums[l][i] for l in range(DEPTH)]).reshape(w[k].shape) for i, k in enumerate(BIG)}
    stack = {k: jnp.stack([g[k] for g in grads]) for k in SMALL + CONVS}

    small_parts = [stack[k] for k in SMALL + CONVS] + [jnp.sum(loss_lanes).reshape(1)]
    small_shapes = [stack[k].shape for k in SMALL + CONVS] + [(1,)]
    summed = _sum_devices(_all_gather([_pack_vec(small_parts)], name="gather_small_grads")[0], name="sum_small")
    small = _unpack_vec(summed, small_shapes)
    loss = small[-1][0]
    for k, val in zip(SMALL + CONVS, small[:-1]):
        gsum[k] = val
    for k in CONVS:
        per = gsum[k].shape[2] // N_DEV
        gsum[k] = lax.dynamic_slice_in_dim(gsum[k], me * per, per, axis=2)

    delta, new_m, new_v = {}, {}, {}
    for k in names:
        shp = w[k].shape
        two_d = (-1, shp[-1]) if len(shp) > 1 else (1, -1)
        d_, m_, v_ = _adamw(w[k].reshape(two_d), gsum[k].reshape(two_d), m[k].reshape(two_d), v[k].reshape(two_d),
                            name="adamw")
        delta[k], new_m[k], new_v[k] = d_.reshape(shp), m_.reshape(shp), v_.reshape(shp)

    return (loss, dy[None], *[gsum[k].reshape(w[k].shape) for k in names], *[delta[k] for k in names], *[new_m[k] for k in names],
            *[new_v[k] for k in names])
```

```python
import jax
import jax.numpy as jnp
from jax import lax
from jax.experimental import pallas as pl
from jax.experimental.pallas import tpu as pltpu

F32, BF16 = jnp.float32, jnp.bfloat16
MESH_ID = pl.DeviceIdType.MESH

N_DEV = 8
DEPTH = 4
D_MODEL = 1024
D_FF = 4096
EPS = 1e-6
GDN_HEADS, GDN_DIM, GDN_CONV = 4, 128, 4
GDN_W = GDN_HEADS * GDN_DIM
CHUNK = 64
SB_HEADS, SB_DIM = 8, 64
SB_W = SB_HEADS * SB_DIM
SB_KBLOCK = 256
SC_W, SC_CONV = 512, 3
C_GQKV, C_GGATE, C_SBQKV, C_SCX, C_SCB, C_SCC, C_GATES = 0, 1536, 2048, 3584, 4096, 4608, 5120
AB_W = 128

ADAM_LR, ADAM_B1, ADAM_B2, ADAM_EPS, ADAM_WD, ADAM_STEP = 0.001, 0.9, 0.999, 1e-08, 0.01, 10

VMEM_LIMIT = 48 * 2 ** 20


def _cparams(*sem):
    return pltpu.CompilerParams(dimension_semantics=sem or None, vmem_limit_bytes=VMEM_LIMIT)


def _tile(n, pref):
    if n <= pref:
        return n
    t = pref
    while n % t:
        t -= 128
    assert t > 0
    return t


def _dot(a, b, ca=1, cb=0):
    return lax.dot_general(a.astype(BF16), b.astype(BF16), (((ca,), (cb,)), ((), ())), preferred_element_type=F32)


def _split2(x):
    hi = x.astype(BF16)
    return hi, (x - hi.astype(F32)).astype(BF16)


def _sigmoid(z):
    e = jnp.exp(-jnp.abs(z))
    return jnp.where(z >= 0, 1.0, e) / (1.0 + e)


def _softplus(z):
    return jnp.maximum(z, 0.0) + jnp.log(1.0 + jnp.exp(-jnp.abs(z)))


def _mm(a, b, *, name, ta=False, tb=False, out_dtypes=(BF16,), epi=None, extras=()):
    assert a.dtype == BF16 and b.dtype == BF16
    m, k = (a.shape[1], a.shape[0]) if ta else a.shape
    n = b.shape[0] if tb else b.shape[1]
    assert (b.shape[1] if tb else b.shape[0]) == k
    tm, tn, tk = _tile(m, 1024), _tile(n, 1024), _tile(k, 2048)
    nk = k // tk
    ca, cb = (0 if ta else 1), (1 if tb else 0)
    n_ex, n_out = len(extras), len(out_dtypes)

    def body(*refs):
        a_ref, b_ref = refs[0], refs[1]
        ex = refs[2:2 + n_ex]
        outs = refs[2 + n_ex:2 + n_ex + n_out]
        acc = refs[-1]
        kk = pl.program_id(2)
        part = lax.dot_general(a_ref[...], b_ref[...], (((ca,), (cb,)), ((), ())), preferred_element_type=F32)

        def finish(r):
            vals = (r,) if epi is None else epi(r, *[e[...] for e in ex])
            for o, v in zip(outs, vals):
                o[...] = v.astype(o.dtype)

        if nk == 1:
            finish(part)
        else:
            @pl.when(kk == 0)
            def _():
                acc[...] = part

            @pl.when(kk > 0)
            def _():
                acc[...] += part

            @pl.when(kk == nk - 1)
            def _():
                finish(acc[...])

    a_spec = pl.BlockSpec((tk, tm), lambda i, j, kk: (kk, i)) if ta else pl.BlockSpec((tm, tk), lambda i, j, kk: (i, kk))
    b_spec = pl.BlockSpec((tn, tk), lambda i, j, kk: (j, kk)) if tb else pl.BlockSpec((tk, tn), lambda i, j, kk: (kk, j))
    io_spec = pl.BlockSpec((tm, tn), lambda i, j, kk: (i, j))
    res = pl.pallas_call(
        body, name=name, grid=(m // tm, n // tn, nk),
        in_specs=[a_spec, b_spec] + [io_spec] * n_ex,
        out_specs=[io_spec] * n_out,
        out_shape=[jax.ShapeDtypeStruct((m, n), dt) for dt in out_dtypes],
        scratch_shapes=[pltpu.VMEM((tm, tn) if nk > 1 else (8, 128), F32)],
        compiler_params=_cparams("parallel", "parallel", "arbitrary"),
    )(a, b, *extras)
    return res[0] if n_out == 1 else res


ROW_TILE = 512


def _norm_fwd(y, w, res=None, *, out_dtype, name):
    t, d = y.shape
    tm = _tile(t, ROW_TILE)
    has_res = res is not None

    def body(*refs):
        y_ref, w_ref = refs[0], refs[1]
        o_ref = refs[-1]
        yv = y_ref[...]
        r = lax.rsqrt(jnp.mean(yv * yv, axis=-1, keepdims=True) + EPS)
        out = yv * r * w_ref[...]
        if has_res:
            out = out + refs[2][...]
        o_ref[...] = out.astype(o_ref.dtype)

    row = pl.BlockSpec((tm, d), lambda i: (i, 0))
    vec = pl.BlockSpec((1, d), lambda i: (0, 0))
    args = (y, w.reshape(1, d)) + ((res,) if has_res else ())
    return pl.pallas_call(
        body, name=name, grid=(t // tm,), in_specs=[row, vec] + [row] * has_res, out_specs=row,
        out_shape=jax.ShapeDtypeStruct((t, d), out_dtype), compiler_params=_cparams("parallel"),
    )(*args)


def _norm_post_pre(y, w_post, res, w_pre, *, name):
    t, d = y.shape
    tm = _tile(t, ROW_TILE)

    def body(y_ref, wa_ref, r_ref, wb_ref, x_ref, h_ref):
        yv = y_ref[...]
        r = lax.rsqrt(jnp.mean(yv * yv, axis=-1, keepdims=True) + EPS)
        x1 = yv * r * wa_ref[...] + r_ref[...]
        x_ref[...] = x1
        r2 = lax.rsqrt(jnp.mean(x1 * x1, axis=-1, keepdims=True) + EPS)
        h_ref[...] = (x1 * r2 * wb_ref[...]).astype(h_ref.dtype)

    row = pl.BlockSpec((tm, d), lambda i: (i, 0))
    vec = pl.BlockSpec((1, d), lambda i: (0, 0))
    return pl.pallas_call(
        body, name=name, grid=(t // tm,), in_specs=[row, vec, row, vec], out_specs=[row, row],
        out_shape=[jax.ShapeDtypeStruct((t, d), F32), jax.ShapeDtypeStruct((t, d), BF16)],
        compiler_params=_cparams("parallel"),
    )(y, w_post.reshape(1, d), res, w_pre.reshape(1, d))


def _norm_bwd(y, w, dout, add=None, *, out_dtype, name):
    t, d = y.shape
    tm = _tile(t, ROW_TILE)
    has_add = add is not None

    def body(*refs):
        y_ref, w_ref, do_ref = refs[0], refs[1], refs[2]
        dy_ref, dw_ref = refs[-2], refs[-1]
        yv = y_ref[...]
        r = lax.rsqrt(jnp.mean(yv * yv, axis=-1, keepdims=True) + EPS)
        yh = yv * r
        dov = do_ref[...].astype(F32)
        gw = dov * w_ref[...]
        dy = r * (gw - yh * jnp.mean(gw * yh, axis=-1, keepdims=True))
        if has_add:
            dy = dy + refs[3][...]
        dy_ref[...] = dy.astype(dy_ref.dtype)
        part = jnp.sum(dov * yh, axis=0, keepdims=True)

        @pl.when(pl.program_id(0) == 0)
        def _():
            dw_ref[...] = part

        @pl.when(pl.program_id(0) > 0)
        def _():
            dw_ref[...] += part

    row = pl.BlockSpec((tm, d), lambda i: (i, 0))
    vec = pl.BlockSpec((1, d), lambda i: (0, 0))
    args = (y, w.reshape(1, d), dout) + ((add,) if has_add else ())
    return pl.pallas_call(
        body, name=name, grid=(t // tm,), in_specs=[row, vec, row] + [row] * has_add, out_specs=[row, vec],
        out_shape=[jax.ShapeDtypeStruct((t, d), out_dtype), jax.ShapeDtypeStruct((1, d), F32)],
        compiler_params=_cparams("arbitrary"),
    )(*args)


def _shift_down(u, s):
    if s == 0:
        return u
    rows = lax.broadcasted_iota(jnp.int32, u.shape, 0)
    return jnp.where(rows >= s, pltpu.roll(u, s, 0), 0.0)


def _shift_up(u, s):
    if s == 0:
        return u
    t = u.shape[0]
    rows = lax.broadcasted_iota(jnp.int32, u.shape, 0)
    return jnp.where(rows < t - s, pltpu.roll(u, t - s, 0), 0.0)


def _conv_fwd(u, w):
    kk = w.shape[0]
    out = u * w[kk - 1:kk, :]
    for i in range(kk - 1):
        out = out + _shift_down(u, kk - 1 - i) * w[i:i + 1, :]
    return out


def _conv_bwd(u, w, dc):
    kk = w.shape[0]
    du = dc * w[kk - 1:kk, :]
    dws = []
    for i in range(kk):
        s = kk - 1 - i
        if s:
            du = du + _shift_up(dc, s) * w[i:i + 1, :]
        dws.append(jnp.sum(dc * _shift_down(u, s), axis=0, keepdims=True))
    return du, dws


def _gdn_pre_math(x, w, slab):
    c = _conv_fwd(x, w)
    sig = _sigmoid(c)
    s = c * sig
    r = lax.rsqrt(jnp.sum(s * s, axis=-1, keepdims=True) + EPS)
    scale = jnp.where(slab < GDN_HEADS, GDN_DIM ** -0.5, 1.0)
    return c, sig, s, r, scale


def _gdn_pre_fwd(proj, conv_w, *, name):
    t = proj.shape[0]
    nslab = 3 * GDN_HEADS

    def body(x_ref, w_ref, o_ref):
        slab = pl.program_id(0)
        _, _, s, r, scale = _gdn_pre_math(x_ref[...].astype(F32), w_ref[...], slab)
        o_ref[0] = jnp.where(slab < 2 * GDN_HEADS, s * r * scale, s)

    return pl.pallas_call(
        body, name=name, grid=(nslab,),
        in_specs=[pl.BlockSpec((t, GDN_DIM), lambda j: (0, j)), pl.BlockSpec((GDN_CONV, GDN_DIM), lambda j: (0, j))],
        out_specs=pl.BlockSpec((1, t, GDN_DIM), lambda j: (j, 0, 0)),
        out_shape=jax.ShapeDtypeStruct((nslab, t, GDN_DIM), F32), compiler_params=_cparams("parallel"),
    )(proj, conv_w)


def _gdn_pre_bwd(proj, conv_w, dqkv, *, name):
    t = proj.shape[0]
    nslab = 3 * GDN_HEADS

    def body(x_ref, w_ref, d_ref, dx_ref, dw_ref):
        slab = pl.program_id(0)
        x = x_ref[...].astype(F32)
        w = w_ref[...]
        c, sig, s, r, scale = _gdn_pre_math(x, w, slab)
        dout = d_ref[0]
        yn = s * r
        dn = dout * scale
        ds_norm = r * (dn - yn * jnp.sum(dn * yn, axis=-1, keepdims=True))
        ds = jnp.where(slab < 2 * GDN_HEADS, ds_norm, dout)
        dc = ds * (sig + c * sig * (1.0 - sig))
        dx, dws = _conv_bwd(x, w, dc)
        dx_ref[...] = dx.astype(dx_ref.dtype)
        for i, dwi in enumerate(dws):
            dw_ref[i:i + 1, :] = dwi

    return pl.pallas_call(
        body, name=name, grid=(nslab,),
        in_specs=[pl.BlockSpec((t, GDN_DIM), lambda j: (0, j)), pl.BlockSpec((GDN_CONV, GDN_DIM), lambda j: (0, j)),
                  pl.BlockSpec((1, t, GDN_DIM), lambda j: (j, 0, 0))],
        out_specs=[pl.BlockSpec((t, GDN_DIM), lambda j: (0, j)), pl.BlockSpec((GDN_CONV, GDN_DIM), lambda j: (0, j))],
        out_shape=[jax.ShapeDtypeStruct((t, 3 * GDN_W), BF16), jax.ShapeDtypeStruct((GDN_CONV, 3 * GDN_W), F32)],
        compiler_params=_cparams("parallel"),
    )(proj, conv_w, dqkv)


def _sc_specs(t):
    def col(base):
        return pl.BlockSpec((t, 128), lambda j: (0, base // 128 + j))
    return [col(C_SCX), col(C_SCB), col(C_SCC), pl.BlockSpec((SC_CONV, 128), lambda j: (0, j))]


def _sc_fwd(proj, conv_w, *, name):
    t = proj.shape[0]

    def body(x_ref, b_ref, c_ref, w_ref, o_ref):
        u = c_ref[...].astype(F32) * x_ref[...].astype(F32)
        o_ref[...] = (b_ref[...].astype(F32) * _conv_fwd(u, w_ref[...])).astype(o_ref.dtype)

    return pl.pallas_call(
        body, name=name, grid=(SC_W // 128,), in_specs=_sc_specs(t),
        out_specs=pl.BlockSpec((t, 128), lambda j: (0, j)),
        out_shape=jax.ShapeDtypeStruct((t, SC_W), BF16), compiler_params=_cparams("parallel"),
    )(proj, proj, proj, conv_w)


def _sc_bwd(proj, conv_w, dy, *, name):
    t = proj.shape[0]
    nj = SC_W // 128

    def body(x_ref, b_ref, c_ref, w_ref, dy_ref, dx_ref, db_ref, dc_ref, dw_ref):
        x, b, c = x_ref[...].astype(F32), b_ref[...].astype(F32), c_ref[...].astype(F32)
        w = w_ref[...]
        u = c * x
        dyv = dy_ref[...].astype(F32)
        db_ref[...] = (dyv * _conv_fwd(u, w)).astype(db_ref.dtype)
        du, dws = _conv_bwd(u, w, dyv * b)
        dx_ref[...] = (du * c).astype(dx_ref.dtype)
        dc_ref[...] = (du * x).astype(dc_ref.dtype)
        for i, dwi in enumerate(dws):
            dw_ref[i:i + 1, :] = dwi

    return pl.pallas_call(
        body, name=name, grid=(nj,),
        in_specs=_sc_specs(t) + [pl.BlockSpec((t, 128), lambda j: (0, j))],
        out_specs=[pl.BlockSpec((t, 128), lambda j: (0, j))] * 3 + [pl.BlockSpec((SC_CONV, 128), lambda j: (0, j))],
        out_shape=[jax.ShapeDtypeStruct((t, SC_W), BF16)] * 3 + [jax.ShapeDtypeStruct((SC_CONV, SC_W), F32)],
        compiler_params=_cparams("parallel"),
    )(proj, proj, proj, conv_w, dy)


def _dot_each(a_list, b_list, ca=1, cb=0):
    return [_dot(a, b, ca, cb) for a, b in zip(a_list, b_list)]


def _dot3_each(a_list, b_list, ca=1, cb=0):
    sa = [_split2(a) for a in a_list]
    sb = [_split2(b) for b in b_list]
    prods = [(_dot(a1, b1, ca, cb), _dot(a1, b2, ca, cb), _dot(a2, b1, ca, cb)) for (a1, a2), (b1, b2) in zip(sa, sb)]
    return [x + (y + z) for x, y, z in prods]


def _split3(x):
    hi = x.astype(BF16)
    rest = x - hi.astype(F32)
    mid = rest.astype(BF16)
    return hi, mid, (rest - mid.astype(F32)).astype(BF16)


def _ones_dot_each(m, x_list, ca=1, cb=0):
    mb = m.astype(BF16)
    parts = [[_dot(mb, p, ca, cb) for p in _split3(x)] for x in x_list]
    return [p[0] + (p[1] + p[2]) for p in parts]


def _dot_ones_each(x_list, m, ca=1, cb=0):
    mb = m.astype(BF16)
    parts = [[_dot(p, mb, ca, cb) for p in _split3(x)] for x in x_list]
    return [p[0] + (p[1] + p[2]) for p in parts]


def _tri_inv_each(a_list):
    c = a_list[0].shape[0]
    ri = lax.broadcasted_iota(jnp.int32, (c, c), 0)
    ci = lax.broadcasted_iota(jnp.int32, (c, c), 1)
    eye = (ri == ci).astype(F32)
    blk = 8
    pws = [-jnp.where(ri // blk == ci // blk, a, 0.0) for a in a_list]
    invs = [eye + b for b in pws]
    for _ in range(2):
        pws = _dot_each(pws, pws)
        invs = [i + u for i, u in zip(invs, _dot_each(invs, pws))]
    while blk < c:
        sel = (ri // (2 * blk) == ci // (2 * blk)) & (ri // blk != ci // blk)
        offs = [jnp.where(sel, a, 0.0) for a in a_list]
        invs = [i - t for i, t in zip(invs, _dot_each(_dot_each(invs, offs), invs))]
        blk *= 2
    resid = [eye - x for x in _dot3_each([eye + a for a in a_list], invs)]
    return [i + t for i, t in zip(invs, _dot_each(invs, resid))]


def _gdn_chunks(qs, ks, vs, abs_, head, ea, dtb):
    c = qs[0].shape[0]
    lane = lax.broadcasted_iota(jnp.int32, abs_[0].shape, 1)
    a_s = [jnp.sum(jnp.where(lane == head, ab, 0.0), axis=1, keepdims=True) for ab in abs_]
    b_s = [jnp.sum(jnp.where(lane == GDN_HEADS + head, ab, 0.0), axis=1, keepdims=True) for ab in abs_]
    ri = lax.broadcasted_iota(jnp.int32, (c, c), 0)
    ci = lax.broadcasted_iota(jnp.int32, (c, c), 1)
    tri, strict = ri >= ci, ri > ci
    ltri = tri.astype(F32)
    beta = [_sigmoid(b) for b in b_s]
    sig_a = [_sigmoid(a + dtb) for a in a_s]
    g = [-ea * _softplus(a + dtb) for a in a_s]
    g_cc = [jnp.broadcast_to(x, (c, c)) for x in g]
    gi = _ones_dot_each(ltri, g_cc)
    gj = _dot_ones_each(g_cc, (ri <= ci).astype(F32), 0, 0)
    decay = [jnp.exp(jnp.where(tri, x - y, -1e30)) for x, y in zip(gi, gj)]
    gc = _ones_dot_each(ltri, [jnp.broadcast_to(x, (c, GDN_DIM)) for x in g])
    g_tot = [jnp.sum(x, axis=0, keepdims=True) for x in g]
    egc = [jnp.exp(x) for x in gc]
    ekd = [jnp.exp(t - x) for t, x in zip(g_tot, gc)]
    kb = [k * b for k, b in zip(ks, beta)]
    vb = [v * b for v, b in zip(vs, beta)]
    kbg = [x * e for x, e in zip(kb, egc)]
    mkk = _dot_each(kb, ks, 1, 1)
    a_kk = [jnp.where(strict, m * d, 0.0) for m, d in zip(mkk, decay)]
    tinv = _tri_inv_each(a_kk)
    u = _dot_each(tinv, vb)
    w = _dot_each(tinv, kbg)
    mqk = _dot_each(qs, ks, 1, 1)
    a_qk = [jnp.where(tri, m * d, 0.0) for m, d in zip(mqk, decay)]
    qd = [q * e for q, e in zip(qs, egc)]
    kd = [k * e for k, e in zip(ks, ekd)]
    return dict(beta=beta, sig_a=sig_a, g=g, decay=decay, egc=egc, ekd=ekd, g_tot=g_tot, kb=kb, vb=vb, kbg=kbg,
                a_kk=a_kk, tinv=tinv, u=u, w=w, a_qk=a_qk, qd=qd, kd=kd, tri=tri, strict=strict)


def _chunks_in_step(n):
    for cps in (8, 4, 2):
        if n % cps == 0:
            return cps
    return 1


def _gdn_local_specs(t, cps):
    rows = cps * CHUNK

    def slab(base):
        return pl.BlockSpec((1, rows, GDN_DIM), lambda h, n: (base + h, n, 0))
    smem = pl.BlockSpec(memory_space=pltpu.SMEM)
    return [slab(0), slab(GDN_HEADS), slab(2 * GDN_HEADS), pl.BlockSpec((rows, AB_W), lambda h, n: (n, 0)), smem, smem]


def _scalar_row(ref, head):
    return jnp.full((1, 1), ref[0, head], F32)


def _gdn_local_fwd(qkv, ab, a_log, dt_bias, *, name):
    t = qkv.shape[1]
    n = t // CHUNK
    cps = _chunks_in_step(n)
    rows = cps * CHUNK
    sls = [slice(j * CHUNK, (j + 1) * CHUNK) for j in range(cps)]

    def body(q_ref, k_ref, v_ref, ab_ref, al_ref, dt_ref, u_ref, w_ref, qd_ref, kd_ref, aqk_ref, gl_ref):
        head = pl.program_id(0)
        ea = jnp.exp(_scalar_row(al_ref, head))
        dtb = _scalar_row(dt_ref, head)
        r = _gdn_chunks([q_ref[0, sl, :] for sl in sls], [k_ref[0, sl, :] for sl in sls],
                        [v_ref[0, sl, :] for sl in sls], [ab_ref[sl, :] for sl in sls], head, ea, dtb)
        for j, sl in enumerate(sls):
            u_ref[0, sl, :] = r["u"][j]
            w_ref[0, sl, :] = r["w"][j]
            qd_ref[0, sl, :] = r["qd"][j]
            kd_ref[0, sl, :] = r["kd"][j]
            aqk_ref[0, sl, :] = r["a_qk"][j]
            gl_ref[0, j] = jnp.exp(jnp.broadcast_to(r["g_tot"][j], (1, GDN_DIM)))

    big = pl.BlockSpec((1, rows, GDN_DIM), lambda h, i: (h, i, 0))
    big_shape = jax.ShapeDtypeStruct((GDN_HEADS, t, GDN_DIM), F32)
    return pl.pallas_call(
        body, name=name, grid=(GDN_HEADS, n // cps), in_specs=_gdn_local_specs(t, cps),
        out_specs=[big] * 4 + [pl.BlockSpec((1, rows, CHUNK), lambda h, i: (h, i, 0)),
                               pl.BlockSpec((1, cps, 1, GDN_DIM), lambda h, i: (h, i, 0, 0))],
        out_shape=[big_shape] * 4 + [jax.ShapeDtypeStruct((GDN_HEADS, t, CHUNK), F32),
                                     jax.ShapeDtypeStruct((GDN_HEADS, n, 1, GDN_DIM), F32)],
        compiler_params=_cparams("parallel", "parallel"),
    )(qkv, qkv, qkv, ab, a_log, dt_bias)


def _gdn_local_bwd(qkv, ab, a_log, dt_bias, du, dw, dqd, dkd, daqk, dgl, *, name):
    t = qkv.shape[1]
    n = t // CHUNK
    cps = _chunks_in_step(n)
    rows = cps * CHUNK
    sls = [slice(j * CHUNK, (j + 1) * CHUNK) for j in range(cps)]

    def body(q_ref, k_ref, v_ref, ab_ref, al_ref, dt_ref, du_ref, dw_ref, dqd_ref, dkd_ref, daqk_ref, dgl_ref,
             dq_ref, dk_ref, dv_ref, dab_ref, dsc_ref):
        head = pl.program_id(0)
        ea = jnp.exp(_scalar_row(al_ref, head))
        dtb = _scalar_row(dt_ref, head)
        lane = lax.broadcasted_iota(jnp.int32, (CHUNK, AB_W), 1)
        lane1 = lax.broadcasted_iota(jnp.int32, (1, GDN_DIM), 1)
        ri = lax.broadcasted_iota(jnp.int32, (CHUNK, CHUNK), 0)
        ci = lax.broadcasted_iota(jnp.int32, (CHUNK, CHUNK), 1)
        utri = (ri <= ci).astype(F32)
        ones = jnp.ones((CHUNK, GDN_DIM), F32)
        qs, ks, vs = ([ref[0, sl, :] for sl in sls] for ref in (q_ref, k_ref, v_ref))
        r = _gdn_chunks(qs, ks, vs, [ab_ref[sl, :] for sl in sls], head, ea, dtb)
        duv, dwv, dqdv, dkdv = ([ref[0, sl, :] for sl in sls] for ref in (du_ref, dw_ref, dqd_ref, dkd_ref))
        d_aqk = [jnp.where(r["tri"], daqk_ref[0, sl, :], 0.0) for sl in sls]
        dvb = _dot_each(r["tinv"], duv, 0, 0)
        dkbg = _dot_each(r["tinv"], dwv, 0, 0)
        outer = [x + y for x, y in zip(_dot_each(dvb, r["u"], 1, 1), _dot_each(dkbg, r["w"], 1, 1))]
        d_akk = [-jnp.where(r["strict"], x, 0.0) for x in outer]
        e = [x * a + y * b for x, a, y, b in zip(d_akk, r["a_kk"], d_aqk, r["a_qk"])]
        dmkk = [x * d for x, d in zip(d_akk, r["decay"])]
        dmqk = [x * d for x, d in zip(d_aqk, r["decay"])]
        dkb = [x + y * eg for x, y, eg in zip(_dot_each(dmkk, ks), dkbg, r["egc"])]
        dk = [a + b + x * ek + y * bt for a, b, x, ek, y, bt in zip(
            _dot_each(dmkk, r["kb"], 0, 0), _dot_each(dmqk, qs, 0, 0), dkdv, r["ekd"], dkb, r["beta"])]
        dq = [a + x * eg for a, x, eg in zip(_dot_each(dmqk, ks), dqdv, r["egc"])]
        col_sums = _dot_ones_each(e, ones, 0, 0)
        acc_alog = jnp.zeros((1, 1), F32)
        acc_dtb = jnp.zeros((1, 1), F32)
        dgc_lanes, d_tots, dbetas = [], [], []
        for j in range(cps):
            dbetas.append(jnp.sum(dkb[j] * ks[j] + dvb[j] * vs[j], axis=1, keepdims=True))
            kd_term = jnp.sum(dkdv[j] * r["kd"][j], axis=1, keepdims=True)
            dgc = (jnp.sum(e[j], axis=1, keepdims=True)
                   + jnp.sum(dqdv[j] * r["qd"][j] + dkbg[j] * r["kbg"][j], axis=1, keepdims=True) - kd_term)
            dgc_lanes.append(jnp.broadcast_to(dgc, (CHUNK, GDN_DIM)) - col_sums[j])
            dgl_tot = jnp.sum(dgl_ref[0, j], axis=1, keepdims=True) * jnp.exp(r["g_tot"][j])
            d_tots.append(jnp.sum(kd_term, axis=0, keepdims=True) + dgl_tot)
        suffix_sums = _ones_dot_each(utri, dgc_lanes)
        for j, sl in enumerate(sls):
            dq_ref[0, sl, :] = dq[j]
            dk_ref[0, sl, :] = dk[j]
            dv_ref[0, sl, :] = dvb[j] * r["beta"][j]
            dg = jnp.sum(jnp.where(lane == 0, suffix_sums[j] + d_tots[j], 0.0), axis=1, keepdims=True)
            da = dg * (-ea) * r["sig_a"][j]
            db = dbetas[j] * r["beta"][j] * (1.0 - r["beta"][j])
            dab_ref[0, sl, :] = jnp.where(lane == head, da, 0.0) + jnp.where(lane == GDN_HEADS + head, db, 0.0)
            acc_alog = acc_alog + jnp.sum(dg * r["g"][j], axis=0, keepdims=True)
            acc_dtb = acc_dtb + jnp.sum(da, axis=0, keepdims=True)
        dsc_ref[0, 0] = jnp.where(lane1 == 0, acc_alog, 0.0) + jnp.where(lane1 == 1, acc_dtb, 0.0)

    big = pl.BlockSpec((1, rows, GDN_DIM), lambda h, i: (h, i, 0))
    big_shape = jax.ShapeDtypeStruct((GDN_HEADS, t, GDN_DIM), F32)
    return pl.pallas_call(
        body, name=name, grid=(GDN_HEADS, n // cps),
        in_specs=_gdn_local_specs(t, cps) + [big] * 4 + [pl.BlockSpec((1, rows, CHUNK), lambda h, i: (h, i, 0)),
                                                        pl.BlockSpec((1, cps, 1, GDN_DIM), lambda h, i: (h, i, 0, 0))],
        out_specs=[big] * 4 + [pl.BlockSpec((1, 1, 1, GDN_DIM), lambda h, i: (h, i, 0, 0))],
        out_shape=[big_shape] * 4 + [jax.ShapeDtypeStruct((GDN_HEADS, n // cps, 1, GDN_DIM), F32)],
        compiler_params=_cparams("parallel", "parallel"),
    )(qkv, qkv, qkv, ab, a_log, dt_bias, du, dw, dqd, dkd, daqk, dgl)


SCAN_CHUNKS = 4


def _scan_chunks(n):
    return SCAN_CHUNKS if n % SCAN_CHUNKS == 0 else 1


def _gdn_scan_fwd(u, w, qd, kd, aqk, gl, *, name):
    h, t, _ = u.shape
    n = t // CHUNK
    cps = _scan_chunks(n)
    rows = cps * CHUNK
    hs = range(h)

    def body(u_ref, w_ref, qd_ref, kd_ref, aqk_ref, gl_ref, o_ref, s_ref, state):
        @pl.when(pl.program_id(0) == 0)
        def _():
            state[...] = jnp.zeros_like(state)

        ss = [state[hh] for hh in hs]
        for j in range(cps):
            sl = slice(j * CHUNK, (j + 1) * CHUNK)
            for hh in hs:
                s_ref[hh, j] = ss[hh]
            vn = [u_ref[hh, sl, :] - x for hh, x in zip(hs, _dot_each([w_ref[hh, sl, :] for hh in hs], ss))]
            from_state = _dot_each([qd_ref[hh, sl, :] for hh in hs], ss)
            from_chunk = _dot_each([aqk_ref[hh, sl, :] for hh in hs], vn)
            writes = _dot_each([kd_ref[hh, sl, :] for hh in hs], vn, 0, 0)
            for hh in hs:
                o_ref[hh, sl, :] = from_state[hh] + from_chunk[hh]
            ss = [ss[hh] * gl_ref[hh, j] + writes[hh] for hh in hs]
        for hh in hs:
            state[hh] = ss[hh]

    big = pl.BlockSpec((h, rows, GDN_DIM), lambda i: (0, i, 0))
    return pl.pallas_call(
        body, name=name, grid=(n // cps,),
        in_specs=[big] * 4 + [pl.BlockSpec((h, rows, CHUNK), lambda i: (0, i, 0)),
                              pl.BlockSpec((h, cps, 1, GDN_DIM), lambda i: (0, i, 0, 0))],
        out_specs=[big, pl.BlockSpec((h, cps, GDN_DIM, GDN_DIM), lambda i: (0, i, 0, 0))],
        out_shape=[jax.ShapeDtypeStruct((h, t, GDN_DIM), F32), jax.ShapeDtypeStruct((h, n, GDN_DIM, GDN_DIM), F32)],
        scratch_shapes=[pltpu.VMEM((h, GDN_DIM, GDN_DIM), F32)],
        compiler_params=_cparams("arbitrary"),
    )(u, w, qd, kd, aqk, gl)


def _gdn_scan_bwd(u, w, qd, kd, aqk, gl, states, do, *, name):
    h, t, _ = u.shape
    n = t // CHUNK
    cps = _scan_chunks(n)
    rows = cps * CHUNK
    steps = n // cps
    hs = range(h)

    def body(u_ref, w_ref, qd_ref, kd_ref, aqk_ref, gl_ref, s_ref, do_ref,
             du_ref, dw_ref, dqd_ref, dkd_ref, daqk_ref, dgl_ref, dstate):
        @pl.when(pl.program_id(0) == 0)
        def _():
            dstate[...] = jnp.zeros_like(dstate)

        ri = lax.broadcasted_iota(jnp.int32, (CHUNK, CHUNK), 0)
        ci = lax.broadcasted_iota(jnp.int32, (CHUNK, CHUNK), 1)
        dsn = [dstate[hh] for hh in hs]
        for j in reversed(range(cps)):
            sl = slice(j * CHUNK, (j + 1) * CHUNK)
            ss = [s_ref[hh, j] for hh in hs]
            dov = [do_ref[hh, sl, :] for hh in hs]
            wv = [w_ref[hh, sl, :] for hh in hs]
            vn = [u_ref[hh, sl, :] - x for hh, x in zip(hs, _dot_each(wv, ss))]
            dvn = [x + y for x, y in zip(_dot_each([aqk_ref[hh, sl, :] for hh in hs], dov, 0, 0),
                                         _dot_each([kd_ref[hh, sl, :] for hh in hs], dsn))]
            dws = _dot_each(dvn, ss, 1, 1)
            dqds = _dot_each(dov, ss, 1, 1)
            dkds = _dot_each(vn, dsn, 1, 1)
            daqks = _dot_each(dov, vn, 1, 1)
            reads = _dot_each([qd_ref[hh, sl, :] for hh in hs], dov, 0, 0)
            solves = _dot_each(wv, dvn, 0, 0)
            for hh in hs:
                du_ref[hh, sl, :] = dvn[hh]
                dw_ref[hh, sl, :] = -dws[hh]
                dqd_ref[hh, sl, :] = dqds[hh]
                dkd_ref[hh, sl, :] = dkds[hh]
                daqk_ref[hh, sl, :] = jnp.where(ri >= ci, daqks[hh], 0.0)
                dgl_ref[hh, j] = jnp.sum(dsn[hh] * ss[hh], axis=0, keepdims=True)
            dsn = [reads[hh] + dsn[hh] * gl_ref[hh, j] - solves[hh] for hh in hs]
        for hh in hs:
            dstate[hh] = dsn[hh]

    big = pl.BlockSpec((h, rows, GDN_DIM), lambda i: (0, steps - 1 - i, 0))
    sq = pl.BlockSpec((h, rows, CHUNK), lambda i: (0, steps - 1 - i, 0))
    glb = pl.BlockSpec((h, cps, 1, GDN_DIM), lambda i: (0, steps - 1 - i, 0, 0))
    big_shape = jax.ShapeDtypeStruct((h, t, GDN_DIM), F32)
    return pl.pallas_call(
        body, name=name, grid=(steps,),
        in_specs=[big] * 4 + [sq, glb, pl.BlockSpec((h, cps, GDN_DIM, GDN_DIM), lambda i: (0, steps - 1 - i, 0, 0)),
                              big],
        out_specs=[big] * 4 + [sq, glb],
        out_shape=[big_shape] * 4 + [jax.ShapeDtypeStruct((h, t, CHUNK), F32),
                                     jax.ShapeDtypeStruct((h, n, 1, GDN_DIM), F32)],
        scratch_shapes=[pltpu.VMEM((h, GDN_DIM, GDN_DIM), F32)],
        compiler_params=_cparams("arbitrary"),
    )(u, w, qd, kd, aqk, gl, states, do)


def _gdn_post_fwd(o, proj, norm_w, *, name):
    h, t, _ = o.shape
    tm = _tile(t, ROW_TILE)

    def body(o_ref, g_ref, w_ref, y_ref):
        for hh in range(h):
            sl = slice(hh * GDN_DIM, (hh + 1) * GDN_DIM)
            ov = o_ref[hh]
            gate = g_ref[:, sl].astype(F32)
            r = lax.rsqrt(jnp.mean(ov * ov, axis=-1, keepdims=True) + EPS)
            y_ref[:, sl] = (ov * r * w_ref[...] * (gate * _sigmoid(gate))).astype(y_ref.dtype)

    return pl.pallas_call(
        body, name=name, grid=(t // tm,),
        in_specs=[pl.BlockSpec((h, tm, GDN_DIM), lambda i: (0, i, 0)),
                  pl.BlockSpec((tm, GDN_W), lambda i: (i, C_GGATE // GDN_W)),
                  pl.BlockSpec((1, GDN_DIM), lambda i: (0, 0))],
        out_specs=pl.BlockSpec((tm, GDN_W), lambda i: (i, 0)),
        out_shape=jax.ShapeDtypeStruct((t, GDN_W), BF16), compiler_params=_cparams("parallel"),
    )(o, proj, norm_w.reshape(1, GDN_DIM))


def _gdn_post_bwd(o, proj, norm_w, dy, *, name):
    h, t, _ = o.shape
    tm = _tile(t, ROW_TILE)

    def body(o_ref, g_ref, w_ref, dy_ref, do_ref, dg_ref, dw_ref):
        part = jnp.zeros((1, GDN_DIM), F32)
        for hh in range(h):
            sl = slice(hh * GDN_DIM, (hh + 1) * GDN_DIM)
            ov = o_ref[hh]
            gate = g_ref[:, sl].astype(F32)
            sig = _sigmoid(gate)
            silu = gate * sig
            r = lax.rsqrt(jnp.mean(ov * ov, axis=-1, keepdims=True) + EPS)
            oh = ov * r
            dyv = dy_ref[:, sl].astype(F32)
            dg_ref[:, sl] = (dyv * oh * w_ref[...] * (sig + silu * (1.0 - sig))).astype(dg_ref.dtype)
            dn = dyv * silu
            part = part + jnp.sum(dn * oh, axis=0, keepdims=True)
            gw = dn * w_ref[...]
            do_ref[hh] = r * (gw - oh * jnp.mean(gw * oh, axis=-1, keepdims=True))

        @pl.when(pl.program_id(0) == 0)
        def _():
            dw_ref[...] = part

        @pl.when(pl.program_id(0) > 0)
        def _():
            dw_ref[...] += part

    return pl.pallas_call(
        body, name=name, grid=(t // tm,),
        in_specs=[pl.BlockSpec((h, tm, GDN_DIM), lambda i: (0, i, 0)),
                  pl.BlockSpec((tm, GDN_W), lambda i: (i, C_GGATE // GDN_W)),
                  pl.BlockSpec((1, GDN_DIM), lambda i: (0, 0)),
                  pl.BlockSpec((tm, GDN_W), lambda i: (i, 0))],
        out_specs=[pl.BlockSpec((h, tm, GDN_DIM), lambda i: (0, i, 0)), pl.BlockSpec((tm, GDN_W), lambda i: (i, 0)),
                   pl.BlockSpec((1, GDN_DIM), lambda i: (0, 0))],
        out_shape=[jax.ShapeDtypeStruct((h, t, GDN_DIM), F32), jax.ShapeDtypeStruct((t, GDN_W), BF16),
                   jax.ShapeDtypeStruct((1, GDN_DIM), F32)],
        compiler_params=_cparams("arbitrary"),
    )(o, proj, norm_w.reshape(1, GDN_DIM), dy)


def _gather_phases(x_refs, out_refs, send_sems, recv_sems, local_sems):
    n = len(x_refs)
    mx, my, mc = lax.axis_index("x"), lax.axis_index("y"), lax.axis_index("c")
    me, sibling = (mx, my, mc), (mx, my, 1 - mc)
    chips = [(1 - mx, my), (mx, 1 - my), (1 - mx, 1 - my)]

    def slot(a, px, py, pc):
        return out_refs[a].at[4 * px + 2 * py + pc]

    def copy(a, k, block, to, src=None):
        return pltpu.make_async_remote_copy(
            src_ref=slot(a, *block) if src is None else src, dst_ref=slot(a, *block),
            send_sem=send_sems.at[a, k], recv_sem=recv_sems.at[a, k], device_id=to, device_id_type=MESH_ID)

    def mine():
        return [pltpu.make_async_copy(x_refs[a], slot(a, *me), local_sems.at[a]) for a in range(n)]

    def first():
        out = [copy(a, 1 + j, me, (*chip, mc), src=x_refs[a]) for j, chip in enumerate(chips) for a in range(n)]
        return out + [copy(a, 0, me, sibling, src=x_refs[a]) for a in range(n)]

    def passed():
        return [copy(a, 4 + j, (*chip, mc), sibling) for j, chip in enumerate(chips) for a in range(n)]

    def start():
        for cp in mine() + first():
            cp.start()

    def pass_on():
        for j, chip in enumerate(chips):
            for a in range(n):
                copy(a, 1 + j, (*chip, mc), me).wait_recv()
                copy(a, 4 + j, (*chip, mc), sibling).start()

    def finish():
        for a in range(n):
            copy(a, 0, sibling, me).wait_recv()
        for j, chip in enumerate(chips):
            for a in range(n):
                copy(a, 4 + j, (*chip, 1 - mc), me).wait_recv()
        for cp in first() + passed():
            cp.wait_send()
        for cp in mine():
            cp.wait()

    return start, pass_on, finish


def _gather_extras(shards):
    n = len(shards)
    anyspace = pl.BlockSpec(memory_space=pl.ANY)
    return ([anyspace] * n, [anyspace] * n, [jax.ShapeDtypeStruct((N_DEV,) + x.shape, x.dtype) for x in shards],
            [pltpu.SemaphoreType.DMA((n, 7)), pltpu.SemaphoreType.DMA((n, 7)), pltpu.SemaphoreType.DMA((n,))])


def _scatter_phases(g_refs, out_refs, send_sems, recv_sems, local_sems):
    n = len(g_refs)
    mx, my, mc = lax.axis_index("x"), lax.axis_index("y"), lax.axis_index("c")
    me_id = 4 * mx + 2 * my + mc

    def peer(r):
        return (1 - mx if r & 4 else mx, 1 - my if r & 2 else my, 1 - mc if r & 1 else mc)

    def peer_id(r):
        px, py, pc = peer(r)
        return 4 * px + 2 * py + pc

    def copies():
        return [pltpu.make_async_remote_copy(
            src_ref=g_refs[a].at[peer_id(r)], dst_ref=out_refs[a].at[me_id], send_sem=send_sems.at[a, r - 1],
            recv_sem=recv_sems.at[a, r - 1], device_id=peer(r), device_id_type=MESH_ID)
            for r in range(1, N_DEV) for a in range(n)]

    def arrivals():
        return [pltpu.make_async_remote_copy(
            src_ref=g_refs[a].at[peer_id(r)], dst_ref=out_refs[a].at[peer_id(r)], send_sem=send_sems.at[a, r - 1],
            recv_sem=recv_sems.at[a, r - 1], device_id=peer(r), device_id_type=MESH_ID)
            for r in range(1, N_DEV) for a in range(n)]

    def mine():
        return [pltpu.make_async_copy(g_refs[a].at[me_id], out_refs[a].at[me_id], local_sems.at[a]) for a in range(n)]

    def start():
        for cp in mine() + copies():
            cp.start()

    def finish():
        for cp in arrivals():
            cp.wait_recv()
        for cp in copies():
            cp.wait_send()
        for cp in mine():
            cp.wait()

    return start, finish


def _scatter_extras(blocks):
    n = len(blocks)
    anyspace = pl.BlockSpec(memory_space=pl.ANY)
    return ([anyspace] * n, [anyspace] * n, [jax.ShapeDtypeStruct(b.shape, b.dtype) for b in blocks],
            [pltpu.SemaphoreType.DMA((n, 7)), pltpu.SemaphoreType.DMA((n, 7)), pltpu.SemaphoreType.DMA((n,))])


def _sum_slots(x, *, name):
    _, r, c = x.shape
    tr = _tile(r, ROW_TILE)

    def body(x_ref, o_ref):
        acc = x_ref[0].astype(F32)
        for s in range(1, N_DEV):
            acc = acc + x_ref[s].astype(F32)
        o_ref[...] = acc

    return pl.pallas_call(
        body, name=name, grid=(r // tr,), in_specs=[pl.BlockSpec((N_DEV, tr, c), lambda i: (0, i, 0))],
        out_specs=pl.BlockSpec((tr, c), lambda i: (i, 0)), out_shape=jax.ShapeDtypeStruct((r, c), F32),
        compiler_params=_cparams("parallel"),
    )(x)


SB_PAIRS = SB_HEADS // 2
SB_PAIR_QBLOCK = 512
SB_PAIR_QBLOCK_FWD = 512
SB_SCALE = SB_DIM ** -0.5


def _suffix_ones(blk):
    ri = lax.broadcasted_iota(jnp.int32, (blk, blk), 0)
    ci = lax.broadcasted_iota(jnp.int32, (blk, blk), 1)
    return (ri >= ci).astype(BF16)


def _prefix_ones(blk):
    ri = lax.broadcasted_iota(jnp.int32, (blk, blk), 0)
    ci = lax.broadcasted_iota(jnp.int32, (blk, blk), 1)
    return (ri <= ci).astype(BF16)


def _sb_pair_blocks(t, pref=SB_PAIR_QBLOCK):
    bq = _tile(t, pref)
    return bq, _tile(bq, SB_KBLOCK)


def _sb_pair_specs(t, bq):
    base = C_SBQKV // 128
    return [pl.BlockSpec((bq, 128), lambda p, i: (i, base + p)),
            pl.BlockSpec((t, 128), lambda p, i: (0, base + SB_PAIRS + p)),
            pl.BlockSpec((t, 128), lambda p, i: (0, base + 2 * SB_PAIRS + p))]


def _halves(x, first):
    zero = jnp.zeros_like(x)
    return [jnp.where(first, x, zero), jnp.where(first, zero, x)]


def _sb_mask(qi, kb, bq, bk):
    t_idx = qi * bq + lax.broadcasted_iota(jnp.int32, (bq, bk), 0)
    s_idx = kb * bk + lax.broadcasted_iota(jnp.int32, (bq, bk), 1)
    return s_idx < t_idx


def _sb_pair_scores(qh, kblk, mask):
    zs = _dot_each(qh, [kblk, kblk], 1, 1)
    es = [jnp.exp(-jnp.abs(z)) for z in zs]
    sps = [jnp.maximum(z, 0.0) + jnp.log(1.0 + e) for z, e in zip(zs, es)]
    if mask is not None:
        sps = [jnp.where(mask, sp, 0.0) for sp in sps]
    return zs, es, sps


def _sb_atts(zs, csums, laters, mask):
    atts = [jnp.exp(z - c - l) for z, c, l in zip(zs, csums, laters)]
    return atts if mask is None else [jnp.where(mask, a, 0.0) for a in atts]


def _scaled_queries(q_ref, first):
    return _halves(q_ref[...] * jnp.asarray(SB_SCALE, q_ref.dtype), first)


def _running_sums(x_list, m):
    return [_dot(x, m) for x in x_list]


def _sb_pair_fwd(proj, shards=(), *, name):
    t = proj.shape[0]
    bq, bk = _sb_pair_blocks(t, SB_PAIR_QBLOCK_FWD)
    n = len(shards)
    nq = t // bq
    nsteps = SB_PAIRS * nq

    def body(q_ref, k_ref, v_ref, *rest):
        o_ref = rest[n]
        qi = pl.program_id(1)
        if n:
            step_no = pl.program_id(0) * nq + qi
            start, pass_on, finish = _gather_phases(rest[:n], rest[n + 1:2 * n + 1], *rest[2 * n + 1:])
            pl.when(step_no == 0)(start)
            pl.when(step_no == (7 * nsteps) // 8)(pass_on)
        first = lax.broadcasted_iota(jnp.int32, (1, 128), 1) < SB_DIM
        qh = _scaled_queries(q_ref, first)
        suffix = _suffix_ones(bk)
        band = bq // bk
        nkb = (qi + 1) * band

        def make_step(masked):
            def step(it, carry):
                later0, later1, acc = carry
                kb = nkb - 1 - it
                rows = pl.ds(pl.multiple_of(kb * bk, bk), bk)
                mask = _sb_mask(qi, kb, bq, bk) if masked else None
                zs, _, sps = _sb_pair_scores(qh, k_ref[rows, :], mask)
                atts = _sb_atts(zs, _running_sums(sps, suffix), (later0, later1), mask)
                outs = _dot_each(atts, _halves(v_ref[rows, :], first))
                return (later0 + jnp.sum(sps[0], axis=1, keepdims=True),
                        later1 + jnp.sum(sps[1], axis=1, keepdims=True), acc + (outs[0] + outs[1]))
            return step

        zero = jnp.zeros((bq, 1), F32)
        carry = lax.fori_loop(0, band, make_step(True), (zero, zero, jnp.zeros((bq, 128), F32)))
        _, _, acc = lax.fori_loop(band, nkb, make_step(False), carry)
        o_ref[...] = acc.astype(o_ref.dtype)
        if n:
            pl.when(step_no == nsteps - 1)(finish)

    more_in, more_out, more_shapes, sems = _gather_extras(shards) if n else ([], [], [], [])
    res = pl.pallas_call(
        body, name=name, grid=(SB_PAIRS, nq), in_specs=_sb_pair_specs(t, bq) + more_in,
        out_specs=[pl.BlockSpec((bq, 128), lambda p, i: (i, p))] + more_out,
        out_shape=[jax.ShapeDtypeStruct((t, SB_W), BF16)] + more_shapes, scratch_shapes=sems,
        compiler_params=_cparams("arbitrary", "arbitrary"),
    )(proj, proj, proj, *shards)
    return res[0], list(res[1:])


def _sb_pair_bwd(proj, dy, blocks=(), *, name):
    t = proj.shape[0]
    bq, bk = _sb_pair_blocks(t)
    nq = t // bq
    n = len(blocks)

    def body(q_ref, k_ref, v_ref, do_ref, *rest):
        dq_ref, dk_ref, dv_ref = rest[n:n + 3]
        dl_keep, sig_keep, dk_acc, dv_acc = rest[2 * n + 3:2 * n + 7]
        qi = pl.program_id(1)
        if n:
            step_no = pl.program_id(0) * nq + qi
            start, finish = _scatter_phases(rest[:n], rest[n + 3:2 * n + 3], *rest[2 * n + 7:])
            pl.when(step_no == 0)(start)

        @pl.when(qi == 0)
        def _():
            dk_acc[...] = jnp.zeros_like(dk_acc)
            dv_acc[...] = jnp.zeros_like(dv_acc)

        first = lax.broadcasted_iota(jnp.int32, (1, 128), 1) < SB_DIM
        qh = _scaled_queries(q_ref, first)
        doh = _halves(do_ref[...], first)
        suffix = _suffix_ones(bk)
        prefix = _prefix_ones(bk)
        band = bq // bk
        nkb = (qi + 1) * band

        def make_back(masked):
            def back(it, carry):
                kb = nkb - 1 - it
                rows = pl.ds(pl.multiple_of(kb * bk, bk), bk)
                vblk = v_ref[rows, :]
                mask = _sb_mask(qi, kb, bq, bk) if masked else None
                zs, es, sps = _sb_pair_scores(qh, k_ref[rows, :], mask)
                atts = _sb_atts(zs, _running_sums(sps, suffix), carry, mask)
                dvs = _dot_each(atts, doh, 0, 0)
                datts = _dot_each(doh, [vblk, vblk], 1, 1)
                dv_acc[rows, :] += dvs[0] + dvs[1]
                for hh in range(2):
                    sig = jnp.where(zs[hh] >= 0, 1.0, es[hh]) * pl.reciprocal(1.0 + es[hh], approx=True)
                    dl_keep[hh, kb] = (atts[hh] * datts[hh]).astype(dl_keep.dtype)
                    sig_keep[hh, kb] = (sig if mask is None else jnp.where(mask, sig, 0.0)).astype(sig_keep.dtype)
                return tuple(l + jnp.sum(sp, axis=1, keepdims=True) for l, sp in zip(carry, sps))
            return back

        zero = jnp.zeros((bq, 1), F32)
        lax.fori_loop(band, nkb, make_back(False), lax.fori_loop(0, band, make_back(True), (zero, zero)))

        def forth(kb, carry):
            before0, before1, dq = carry
            rows = pl.ds(pl.multiple_of(kb * bk, bk), bk)
            kept = [dl_keep[hh, kb] for hh in range(2)]
            sums = _running_sums(kept, prefix)
            dls = [x.astype(F32) for x in kept]
            dzs = [dl - sig_keep[hh, kb].astype(F32) * (b + s)
                   for hh, (dl, b, s) in enumerate(zip(dls, (before0, before1), sums))]
            dks = _dot_each(dzs, qh, 0, 0)
            dqs = _dot_each(dzs, _halves(k_ref[rows, :], first))
            dk_acc[rows, :] += dks[0] + dks[1]
            return (before0 + jnp.sum(dls[0], axis=1, keepdims=True), before1 + jnp.sum(dls[1], axis=1, keepdims=True),
                    dq + (dqs[0] + dqs[1]))

        _, _, dq = lax.fori_loop(0, nkb, forth, (zero, zero, jnp.zeros((bq, 128), F32)))
        dq_ref[...] = (dq * SB_SCALE).astype(dq_ref.dtype)

        @pl.when(qi == nq - 1)
        def _():
            dk_ref[...] = dk_acc[...].astype(dk_ref.dtype)
            dv_ref[...] = dv_acc[...].astype(dv_ref.dtype)

        if n:
            pl.when(step_no == SB_PAIRS * nq - 1)(finish)

    qspec = pl.BlockSpec((bq, 128), lambda p, i: (i, p))
    kvspec = pl.BlockSpec((t, 128), lambda p, i: (0, p))
    shape = jax.ShapeDtypeStruct((t, SB_W), BF16)
    more_in, more_out, more_shapes, sems = _scatter_extras(blocks) if n else ([], [], [], [])
    res = pl.pallas_call(
        body, name=name, grid=(SB_PAIRS, nq), in_specs=_sb_pair_specs(t, bq) + [qspec] + more_in,
        out_specs=[qspec, kvspec, kvspec] + more_out, out_shape=[shape] * 3 + more_shapes,
        scratch_shapes=[pltpu.VMEM((2, t // bk, bq, bk), BF16), pltpu.VMEM((2, t // bk, bq, bk), BF16),
                        pltpu.VMEM((t, 128), F32), pltpu.VMEM((t, 128), F32)] + sems,
        compiler_params=_cparams("arbitrary", "arbitrary"),
    )(proj, proj, proj, dy, *blocks)
    return res[0], res[1], res[2], list(res[3:])


def _gate_specs(tm):
    return [pl.BlockSpec((tm, D_MODEL), lambda i, b=b: (i, C_GATES // D_MODEL + b)) for b in range(3)]


def _merge_fwd(p, proj, *, name):
    t = proj.shape[0]
    tm = _tile(t, ROW_TILE)

    def body(p0, p1, p2, g0, g1, g2, o_ref):
        acc = jnp.zeros((tm, D_MODEL), F32)
        for pr, gr in ((p0, g0), (p1, g1), (p2, g2)):
            acc = acc + _sigmoid(gr[...].astype(F32)) * pr[...].astype(F32)
        o_ref[...] = acc.astype(o_ref.dtype)

    row = pl.BlockSpec((tm, D_MODEL), lambda i: (i, 0))
    return pl.pallas_call(
        body, name=name, grid=(t // tm,), in_specs=[row] * 3 + _gate_specs(tm), out_specs=row,
        out_shape=jax.ShapeDtypeStruct((t, D_MODEL), BF16), compiler_params=_cparams("parallel"),
    )(*p, proj, proj, proj)


def _merge_bwd(p, proj, dmerged, *, name):
    t = proj.shape[0]
    tm = _tile(t, ROW_TILE)

    def body(p0, p1, p2, g0, g1, g2, dm_ref, dp0, dp1, dp2, dg_ref):
        dm = dm_ref[...].astype(F32)
        for b, (pr, gr, dpr) in enumerate(((p0, g0, dp0), (p1, g1, dp1), (p2, g2, dp2))):
            s = _sigmoid(gr[...].astype(F32))
            dpr[...] = (dm * s).astype(dpr.dtype)
            dg_ref[:, b * D_MODEL:(b + 1) * D_MODEL] = (dm * pr[...].astype(F32) * s * (1.0 - s)).astype(dg_ref.dtype)

    row = pl.BlockSpec((tm, D_MODEL), lambda i: (i, 0))
    res = pl.pallas_call(
        body, name=name, grid=(t // tm,), in_specs=[row] * 3 + _gate_specs(tm) + [row],
        out_specs=[row] * 3 + [pl.BlockSpec((tm, 3 * D_MODEL), lambda i: (i, 0))],
        out_shape=[jax.ShapeDtypeStruct((t, D_MODEL), BF16)] * 3 + [jax.ShapeDtypeStruct((t, 3 * D_MODEL), BF16)],
        compiler_params=_cparams("parallel"),
    )(*p, proj, proj, proj, dmerged)
    return res[:3], res[3]


def _loss_head(y, target, *, name):
    t, d = y.shape
    tm = _tile(t, ROW_TILE)

    def body(y_ref, t_ref, dy_ref, l_ref):
        err = y_ref[...] - t_ref[...]
        dy_ref[...] = err * (1.0 / d)
        part = jnp.sum(err * err, axis=0, keepdims=True) * (0.5 / d)

        @pl.when(pl.program_id(0) == 0)
        def _():
            l_ref[...] = part

        @pl.when(pl.program_id(0) > 0)
        def _():
            l_ref[...] += part

    row = pl.BlockSpec((tm, d), lambda i: (i, 0))
    vec = pl.BlockSpec((1, d), lambda i: (0, 0))
    return pl.pallas_call(
        body, name=name, grid=(t // tm,), in_specs=[row, row], out_specs=[row, vec],
        out_shape=[jax.ShapeDtypeStruct((t, d), F32), jax.ShapeDtypeStruct((1, d), F32)],
        compiler_params=_cparams("arbitrary"),
    )(y, target)


def _adamw(w, g, m, v, *, name):
    r, c = w.shape
    tr = r if r * c * 4 <= 2 ** 21 else max(8, (2 ** 21 // (c * 4)) // 8 * 8)
    while r % tr:
        tr -= 8
    c1 = 1.0 - ADAM_B1 ** ADAM_STEP
    c2 = 1.0 - ADAM_B2 ** ADAM_STEP

    def body(w_ref, g_ref, m_ref, v_ref, d_ref, nm_ref, nv_ref):
        gv = g_ref[...]
        nm = ADAM_B1 * m_ref[...] + (1.0 - ADAM_B1) * gv
        nv = ADAM_B2 * v_ref[...] + (1.0 - ADAM_B2) * (gv * gv)
        nm_ref[...] = nm
        nv_ref[...] = nv
        d_ref[...] = -ADAM_LR * ((nm / c1) / (jnp.sqrt(nv / c2) + ADAM_EPS) + ADAM_WD * w_ref[...])

    spec = pl.BlockSpec((tr, c), lambda i: (i, 0))
    return pl.pallas_call(
        body, name=name, grid=(r // tr,), in_specs=[spec] * 4, out_specs=[spec] * 3,
        out_shape=[jax.ShapeDtypeStruct((r, c), F32)] * 3, compiler_params=_cparams("parallel"),
    )(w, g, m, v)


def _all_gather(xs, *, name):
    n = len(xs)

    def body(*refs):
        start, pass_on, finish = _gather_phases(refs[:n], refs[n:2 * n], *refs[2 * n:])
        start()
        pass_on()
        finish()

    more_in, more_out, more_shapes, sems = _gather_extras(xs)
    return pl.pallas_call(body, name=name, in_specs=more_in, out_specs=more_out, out_shape=more_shapes,
                          scratch_shapes=sems)(*xs)


def _exchange_sibling(gs, *, name):
    n = len(gs)

    def body(*refs):
        g_refs, out_refs = refs[:n], refs[n:2 * n]
        send_sems, recv_sems = refs[2 * n:]
        mx, my, mc = lax.axis_index("x"), lax.axis_index("y"), lax.axis_index("c")
        sibling = (mx, my, 1 - mc)
        copies = []
        for a in range(n):
            for px in range(2):
                for py in range(2):
                    kk = 2 * px + py
                    copies.append(pltpu.make_async_remote_copy(
                        src_ref=g_refs[a].at[4 * px + 2 * py + (1 - mc)], dst_ref=out_refs[a].at[kk],
                        send_sem=send_sems.at[a, kk], recv_sem=recv_sems.at[a, kk], device_id=sibling,
                        device_id_type=MESH_ID))
        for cp in copies:
            cp.start()
        for cp in copies:
            cp.wait_recv()
        for cp in copies:
            cp.wait_send()

    anyspace = pl.BlockSpec(memory_space=pl.ANY)
    return pl.pallas_call(
        body, name=name, in_specs=[anyspace] * n, out_specs=[anyspace] * n,
        out_shape=[jax.ShapeDtypeStruct((4,) + g.shape[1:], g.dtype) for g in gs],
        scratch_shapes=[pltpu.SemaphoreType.DMA((n, 4)), pltpu.SemaphoreType.DMA((n, 4))],
    )(*gs)


def _pair_sum(g, got, *, name):
    _, r, c = g.shape
    tr = _tile(r, ROW_TILE)

    def body(core_ref, a_ref, b_ref, o_ref):
        del core_ref
        o_ref[...] = (a_ref[...].astype(F32) + b_ref[...].astype(F32)).astype(o_ref.dtype)

    grid_spec = pltpu.PrefetchScalarGridSpec(
        num_scalar_prefetch=1, grid=(4, r // tr),
        in_specs=[pl.BlockSpec((1, tr, c), lambda kk, i, core: (2 * kk + core[0], i, 0)),
                  pl.BlockSpec((1, tr, c), lambda kk, i, core: (kk, i, 0))],
        out_specs=pl.BlockSpec((1, tr, c), lambda kk, i, core: (kk, i, 0)))
    return pl.pallas_call(
        body, name=name, grid_spec=grid_spec, out_shape=jax.ShapeDtypeStruct((4, r, c), g.dtype),
        compiler_params=_cparams("parallel", "parallel"),
    )(lax.axis_index("c").astype(jnp.int32).reshape(1), g, got)


def _exchange_chips(parts, *, name):
    n = len(parts)

    def body(*refs):
        p_refs, out_refs = refs[:n], refs[n:2 * n]
        send_sems, recv_sems = refs[2 * n:]
        mx, my, mc = lax.axis_index("x"), lax.axis_index("y"), lax.axis_index("c")
        chips = [(1 - mx, my), (mx, 1 - my), (1 - mx, 1 - my)]
        copies = [pltpu.make_async_remote_copy(
            src_ref=p_refs[a].at[2 * px + py], dst_ref=out_refs[a].at[j], send_sem=send_sems.at[a, j],
            recv_sem=recv_sems.at[a, j], device_id=(px, py, mc), device_id_type=MESH_ID)
            for j, (px, py) in enumerate(chips) for a in range(n)]
        for cp in copies:
            cp.start()
        for cp in copies:
            cp.wait_recv()
        for cp in copies:
            cp.wait_send()

    anyspace = pl.BlockSpec(memory_space=pl.ANY)
    return pl.pallas_call(
        body, name=name, in_specs=[anyspace] * n, out_specs=[anyspace] * n,
        out_shape=[jax.ShapeDtypeStruct((3,) + p.shape[1:], p.dtype) for p in parts],
        scratch_shapes=[pltpu.SemaphoreType.DMA((n, 3)), pltpu.SemaphoreType.DMA((n, 3))],
    )(*parts)


def _final_sum(part, got, *, name):
    _, r, c = part.shape
    tr = _tile(r, ROW_TILE)

    def body(chip_ref, a_ref, b_ref, o_ref):
        del chip_ref
        acc = a_ref[0].astype(F32)
        for j in range(3):
            acc = acc + b_ref[j].astype(F32)
        o_ref[...] = acc

    grid_spec = pltpu.PrefetchScalarGridSpec(
        num_scalar_prefetch=1, grid=(r // tr,),
        in_specs=[pl.BlockSpec((1, tr, c), lambda i, chip: (chip[0], i, 0)),
                  pl.BlockSpec((3, tr, c), lambda i, chip: (0, i, 0))],
        out_specs=pl.BlockSpec((tr, c), lambda i, chip: (i, 0)))
    chip = (2 * lax.axis_index("x") + lax.axis_index("y")).astype(jnp.int32).reshape(1)
    return pl.pallas_call(
        body, name=name, grid_spec=grid_spec, out_shape=jax.ShapeDtypeStruct((r, c), F32),
        compiler_params=_cparams("parallel"),
    )(chip, part, got)


def _sum_devices(x, *, name):
    _, r, c = x.shape

    def body(x_ref, o_ref):
        acc = x_ref[0]
        for j in range(1, N_DEV):
            acc = acc + x_ref[j]
        o_ref[...] = acc

    return pl.pallas_call(body, name=name, out_shape=jax.ShapeDtypeStruct((r, c), F32),
                          compiler_params=_cparams())(x)


BIG = ("w_in", "w_branch", "w_out", "w_ff1", "w_ff2")
BIG_AXIS = {"w_in": 2, "w_branch": 3, "w_out": 1, "w_ff1": 2, "w_ff2": 1}


def _to_global(blocks, axis):
    moved = jnp.moveaxis(blocks, 0, axis)
    shp = moved.shape
    return moved.reshape(shp[:axis] + (shp[axis] * shp[axis + 1],) + shp[axis + 2:])


def _to_blocks(full, axis):
    shp = full.shape
    split = full.reshape(shp[:axis] + (N_DEV, shp[axis] // N_DEV) + shp[axis + 1:])
    return jnp.moveaxis(split, axis, 0)


def _rows(blocks):
    return blocks.reshape(blocks.shape[0], -1, blocks.shape[-1])


def _vec_rows(n):
    return -(-n // 128 // 8) * 8


def _pack_vec(parts):
    flat = jnp.concatenate([p.reshape(-1).astype(F32) for p in parts])
    rows = _vec_rows(flat.shape[0])
    return jnp.pad(flat, (0, rows * 128 - flat.shape[0])).reshape(rows, 128)


def _unpack_vec(flat, shapes):
    lead = flat.shape[:-2]
    flat = flat.reshape(lead + (-1,))
    out, off = [], 0
    for s in shapes:
        n = 1
        for dim in s:
            n *= dim
        out.append(flat[..., off:off + n].reshape(lead + tuple(s)))
        off += n
    return out


def _relu2_epilogue(r):
    a = jnp.maximum(r, 0.0)
    return r, a * a


def _add_epilogue(r, other):
    return (r + other,)


def _relu2_bwd_epilogue(r, a):
    return (r * 2.0 * jnp.maximum(a.astype(F32), 0.0),)


def _layer_fwd(x, p, shards):
    h = _norm_fwd(x, p["norm_mix_pre"], out_dtype=BF16, name="norm_pre_fwd")
    proj = _mm(h, p["w_main"], name="mm_in")
    ab = _mm(h, p["w_ab"], out_dtypes=(F32,), name="mm_ab")
    qkv = _gdn_pre_fwd(proj, p["conv_qkv_w"], name="gdn_pre_fwd")
    a_log, dt_bias = p["gdn_a_log"].reshape(1, GDN_HEADS), p["gdn_dt_bias"].reshape(1, GDN_HEADS)
    u, w, qd, kd, aqk, gl = _gdn_local_fwd(qkv, ab, a_log, dt_bias, name="gdn_local_fwd")
    o_gdn, states = _gdn_scan_fwd(u, w, qd, kd, aqk, gl, name="gdn_scan_fwd")
    y_a = _gdn_post_fwd(o_gdn, proj, p["gdn_norm_w"], name="gdn_post_fwd")
    y_b, gathered = _sb_pair_fwd(proj, shards, name="sb_fwd")
    p = dict(p)
    p.update({k: _to_global(blk, BIG_AXIS[k] - 1) for k, blk in zip(BIG[1:], gathered)})
    y_c = _sc_fwd(proj, p["conv_sc_w"], name="sc_fwd")
    ys = (y_a, y_b, y_c)
    ps = tuple(_mm(ys[b], p["w_branch"][b], name="mm_branch") for b in range(3))
    merged = _merge_fwd(ps, proj, name="merge_fwd")
    mo = _mm(merged, p["w_out"], out_dtypes=(F32,), name="mm_out")
    x1, h2 = _norm_post_pre(mo, p["norm_mix_post"], x, p["norm_ffn_pre"], name="norm_post_pre_fwd")
    a1, r1 = _mm(h2, p["w_ff1"], out_dtypes=(BF16, BF16), epi=_relu2_epilogue, name="mm_ff1")
    f = _mm(r1, p["w_ff2"], out_dtypes=(F32,), name="mm_ff2")
    x2 = _norm_fwd(f, p["norm_ffn_post"], x1, out_dtype=F32, name="norm_post_fwd")
    saved = dict(x=x, h=h, proj=proj, ab=ab, qkv=qkv, u=u, w=w, qd=qd, kd=kd, aqk=aqk, gl=gl, o_gdn=o_gdn,
                 states=states, ys=ys, ps=ps, merged=merged, mo=mo, x1=x1, h2=h2,
                 a1=a1, r1=r1, f=f)
    return x2, saved, p, gathered[len(BIG) - 1:]


def _layer_bwd(dx2, p, s, above=()):
    g = {}
    df, g["norm_ffn_post"] = _norm_bwd(s["f"], p["norm_ffn_post"], dx2, out_dtype=BF16, name="norm_bwd_b")
    da1 = _mm(df, p["w_ff2"], tb=True, epi=_relu2_bwd_epilogue, extras=(s["a1"],), name="mm_ff2_dx")
    g["w_ff2"] = _mm(s["r1"], df, ta=True, name="mm_ff2_dw")
    g["w_ff1"] = _mm(s["h2"], da1, ta=True, name="mm_ff1_dw")
    dh2 = _mm(da1, p["w_ff1"], tb=True, out_dtypes=(F32,), name="mm_ff1_dx")
    dx1, g["norm_ffn_pre"] = _norm_bwd(s["x1"], p["norm_ffn_pre"], dh2, dx2, out_dtype=F32, name="norm_bwd_f")
    dmo, g["norm_mix_post"] = _norm_bwd(s["mo"], p["norm_mix_post"], dx1, out_dtype=BF16, name="norm_bwd_b")
    dmerged = _mm(dmo, p["w_out"], tb=True, name="mm_out_dx")
    g["w_out"] = _mm(s["merged"], dmo, ta=True, name="mm_out_dw")
    dps, dgates = _merge_bwd(s["ps"], s["proj"], dmerged, name="merge_bwd")
    dys = [_mm(dps[b], p["w_branch"][b], tb=True, name="mm_branch_dx") for b in range(3)]
    g["w_branch"] = jnp.stack([_mm(s["ys"][b], dps[b], ta=True, name="mm_branch_dw") for b in range(3)])
    dscx, dscb, dscc, g["conv_sc_w"] = _sc_bwd(s["proj"], p["conv_sc_w"], dys[2], name="sc_bwd")
    own = [_rows(_to_blocks(g[k], BIG_AXIS[k] - 1)) for k in BIG[1:]]
    dsq, dsk, dsv, received = _sb_pair_bwd(s["proj"], dys[1], own + list(above), name="sb_bwd")
    a_log, dt_bias = p["gdn_a_log"].reshape(1, GDN_HEADS), p["gdn_dt_bias"].reshape(1, GDN_HEADS)
    do_gdn, dggate, g["gdn_norm_w"] = _gdn_post_bwd(s["o_gdn"], s["proj"], p["gdn_norm_w"], dys[0], name="gdn_post_bwd")
    du, dw, dqd, dkd, daqk, dgl = _gdn_scan_bwd(s["u"], s["w"], s["qd"], s["kd"], s["aqk"], s["gl"], s["states"],
                                                do_gdn, name="gdn_scan_bwd")
    dq, dk, dv, dab_h, dsc = _gdn_local_bwd(s["qkv"], s["ab"], a_log, dt_bias, du, dw, dqd, dkd, daqk, dgl,
                                            name="gdn_local_bwd")
    dsc = jnp.sum(dsc, axis=(1, 2))
    g["gdn_a_log"], g["gdn_dt_bias"] = dsc[:, 0], dsc[:, 1]
    dqkv = jnp.concatenate([dq, dk, dv], axis=0)
    dgqkv, g["conv_qkv_w"] = _gdn_pre_bwd(s["proj"], p["conv_qkv_w"], dqkv, name="gdn_pre_bwd")
    dab = jnp.sum(dab_h, axis=0).astype(BF16)
    dproj = jnp.concatenate([dgqkv, dggate, dsq, dsk, dsv, dscx, dscb, dscc, dgates], axis=1)
    g["w_main"] = _mm(s["h"], dproj, ta=True, name="mm_in_dw")
    g["w_ab"] = _mm(s["h"], dab, ta=True, out_dtypes=(F32,), name="mm_ab_dw")
    dh_ab = _mm(dab, p["w_ab"], tb=True, out_dtypes=(F32,), name="mm_ab_dx")
    dh = _mm(dproj, p["w_main"], tb=True, out_dtypes=(F32,), epi=_add_epilogue, extras=(dh_ab,), name="mm_in_dx")
    dx, g["norm_mix_pre"] = _norm_bwd(s["x"], p["norm_mix_pre"], dh, dx1, out_dtype=F32, name="norm_bwd_f")
    return dx, g, received


NORMS = ("norm_mix_pre", "norm_mix_post", "norm_ffn_pre", "norm_ffn_post")
SMALL = NORMS + ("gdn_a_log", "gdn_dt_bias", "gdn_norm_w")
CONVS = ("conv_qkv_w", "conv_sc_w")
AB_LO = 2048


def _split_w_in(w_in):
    main = jnp.concatenate([w_in[..., :AB_LO], w_in[..., AB_LO + 2 * GDN_HEADS:]], axis=-1)
    ab = w_in[..., AB_LO:AB_LO + 2 * GDN_HEADS]
    pad = [(0, 0)] * (ab.ndim - 1) + [(0, AB_W - 2 * GDN_HEADS)]
    return main, jnp.pad(ab, pad)


def _join_w_in(main, ab):
    return jnp.concatenate([main[..., :AB_LO], ab[..., :2 * GDN_HEADS].astype(main.dtype), main[..., AB_LO:]], axis=-1)


def kernel(x, norm_mix_pre, w_in, conv_qkv_w, gdn_a_log, gdn_dt_bias, gdn_norm_w, conv_sc_w, w_branch, w_out, norm_mix_post, norm_ffn_pre, w_ff1, w_ff2, norm_ffn_post, loss_target, m_norm_mix_pre, m_w_in, m_conv_qkv_w, m_gdn_a_log, m_gdn_dt_bias, m_gdn_norm_w, m_conv_sc_w, m_w_branch, m_w_out, m_norm_mix_post, m_norm_ffn_pre, m_w_ff1, m_w_ff2, m_norm_ffn_post, v_norm_mix_pre, v_w_in, v_conv_qkv_w, v_gdn_a_log, v_gdn_dt_bias, v_gdn_norm_w, v_conv_sc_w, v_w_branch, v_w_out, v_norm_mix_post, v_norm_ffn_pre, v_w_ff1, v_w_ff2, v_norm_ffn_post):
    names = ("norm_mix_pre", "w_in", "conv_qkv_w", "gdn_a_log", "gdn_dt_bias", "gdn_norm_w", "conv_sc_w", "w_branch",
             "w_out", "norm_mix_post", "norm_ffn_pre", "w_ff1", "w_ff2", "norm_ffn_post")
    w = dict(zip(names, (norm_mix_pre, w_in, conv_qkv_w, gdn_a_log, gdn_dt_bias, gdn_norm_w, conv_sc_w, w_branch,
                         w_out, norm_mix_post, norm_ffn_pre, w_ff1, w_ff2, norm_ffn_post)))
    m = dict(zip(names, (m_norm_mix_pre, m_w_in, m_conv_qkv_w, m_gdn_a_log, m_gdn_dt_bias, m_gdn_norm_w, m_conv_sc_w,
                         m_w_branch, m_w_out, m_norm_mix_post, m_norm_ffn_pre, m_w_ff1, m_w_ff2, m_norm_ffn_post)))
    v = dict(zip(names, (v_norm_mix_pre, v_w_in, v_conv_qkv_w, v_gdn_a_log, v_gdn_dt_bias, v_gdn_norm_w, v_conv_sc_w,
                         v_w_branch, v_w_out, v_norm_mix_post, v_norm_ffn_pre, v_w_ff1, v_w_ff2, v_norm_ffn_post)))
    me = 4 * lax.axis_index("x") + 2 * lax.axis_index("y") + lax.axis_index("c")

    conv_shapes = [w[k].shape for k in CONVS]
    conv_all, = _all_gather([_pack_vec([w[k] for k in CONVS])], name="gather_small")
    convs = {k: _to_global(blk, 2) for k, blk in zip(CONVS, _unpack_vec(conv_all, conv_shapes))}
    shards = [[w[k][l].astype(BF16) for k in BIG] for l in range(DEPTH)]
    n_big = len(BIG)

    xs = x[0]
    w_in_blocks, = _all_gather(shards[0][:1], name="gather_weights")
    layers, saved = [], []
    for l in range(DEPTH):
        p = {k: convs[k][l] for k in CONVS}
        p.update({k: w[k][l] for k in SMALL})
        p["w_main"], p["w_ab"] = _split_w_in(_to_global(w_in_blocks, BIG_AXIS["w_in"] - 1))
        riding = shards[l][1:] + (shards[l + 1][:1] if l + 1 < DEPTH else [])
        xs, s, p, rest = _layer_fwd(xs, p, riding)
        layers.append(p)
        saved.append(s)
        w_in_blocks = rest[0] if rest else None
    dy, loss_lanes = _loss_head(xs, loss_target[0], name="loss_head")

    grads, big_sums, above = [None] * DEPTH, [[None] * n_big for _ in range(DEPTH)], []
    for l in reversed(range(DEPTH)):
        dy, g, received = _layer_bwd(dy, layers[l], saved[l], above)
        sums = [_sum_slots(r, name="rs_sum_slots") for r in received]
        big_sums[l][1:] = sums[:n_big - 1]
        if above:
            big_sums[l + 1][0] = sums[n_big - 1]
        g["w_in"] = _join_w_in(g.pop("w_main"), g.pop("w_ab"))
        above = [_rows(_to_blocks(g["w_in"], BIG_AXIS["w_in"] - 1))]
        grads[l] = g
    got = _exchange_sibling(above, name="rs_sibling")
    parts = [_pair_sum(b, r, name="rs_pair_sum") for b, r in zip(above, got)]
    got2 = _exchange_chips(parts, name="rs_chips")
    big_sums[0][0] = _final_sum(parts[0], got2[0], name="rs_final_sum")
    gsum = {k: jnp.stack([big_sums[l][i] for l in range(DEPTH)]).reshape(w[k].shape) for i, k in enumerate(BIG)}
    stack = {k: jnp.stack([g[k] for g in grads]) for k in SMALL + CONVS}

    small_parts = [stack[k] for k in SMALL + CONVS] + [jnp.sum(loss_lanes).reshape(1)]
    small_shapes = [stack[k].shape for k in SMALL + CONVS] + [(1,)]
    summed = _sum_devices(_all_gather([_pack_vec(small_parts)], name="gather_small_grads")[0], name="sum_small")
    small = _unpack_vec(summed, small_shapes)
    loss = small[-1][0]
    for k, val in zip(SMALL + CONVS, small[:-1]):
        gsum[k] = val
    for k in CONVS:
        per = gsum[k].shape[2] // N_DEV
        gsum[k] = lax.dynamic_slice_in_dim(gsum[k], me * per, per, axis=2)

    delta, new_m, new_v = {}, {}, {}
    for k in names:
        shp = w[k].shape
        two_d = (-1, shp[-1]) if len(shp) > 1 else (1, -1)
        d_, m_, v_ = _adamw(w[k].reshape(two_d), gsum[k].reshape(two_d), m[k].reshape(two_d), v[k].reshape(two_d),
                            name="adamw")
        delta[k], new_m[k], new_v[k] = d_.reshape(shp), m_.reshape(shp), v_.reshape(shp)

    return (loss, dy[None], *[gsum[k].reshape(w[k].shape) for k in names], *[delta[k] for k in names], *[new_m[k] for k in names],
            *[new_v[k] for k in names])
```
